```python
import math
import jax, jax.numpy as jnp
from jax import lax
import numpy as np

D_MODEL = 1024
BATCH = 8
SEQ = 2048
DEPTH = 2

CHUNK = 64
Q_BLOCK = 128
EPS = 1e-6
N_HEADS = 8
QK_NOPE = 64
QK_ROPE = 32
QK_HEAD = QK_NOPE + QK_ROPE
V_HEAD = 64
Q_LORA = 256
KV_LORA = 128
ROPE_THETA = 10000.0
ATTN_WIDTH = N_HEADS * V_HEAD
SSM_WIDTH = 256
SSM_GROUP = 16
SSM_GROUPS = SSM_WIDTH // SSM_GROUP
SSM_STATE = 64
DT_MIN = 1e-3
DT_MAX = 1e-1
CONV_WIDTH = 256
CONV_K = 31
MIX_WIDTH = ATTN_WIDTH + SSM_WIDTH + CONV_WIDTH
IN_SIZES = [Q_LORA, KV_LORA, QK_ROPE, SSM_WIDTH, CONV_WIDTH, CONV_WIDTH]
IN_PROJ = sum(IN_SIZES)
IN_SPLITS = [int(v) for v in np.cumsum(IN_SIZES)[:-1]]
N_EGROUPS = 4
EXP_PER_GROUP = 8
N_EXPERTS = N_EGROUPS * EXP_PER_GROUP
TOP_K = 2
D_FF_E = 512
MOE_BLOCK = 128

kernel_name = "hybrid_mla_s5_conformer_hmoe"


def rms_norm(x, w, eps=EPS):
    xf = x.astype(jnp.float32)
    y = xf * lax.rsqrt(jnp.mean(xf * xf, -1, keepdims=True) + eps)
    return (y * w.astype(jnp.float32)).astype(x.dtype)


def layer_norm(x, w, b, eps=1e-5):
    xf = x.astype(jnp.float32)
    mu = jnp.mean(xf, -1, keepdims=True)
    var = jnp.mean(jnp.square(xf - mu), -1, keepdims=True)
    y = (xf - mu) * lax.rsqrt(var + eps) * w.astype(jnp.float32) + b.astype(jnp.float32)
    return y.astype(x.dtype)


def rotary(x, cos, sin):
    x1, x2 = jnp.split(x, 2, axis=-1)
    return jnp.concatenate([x1 * cos - x2 * sin, x1 * sin + x2 * cos], axis=-1)


def mla_mixer(c_q, c_kv, k_pe, cos, sin, q_a_norm, w_uq, kv_a_norm, w_ukv, q_norm, k_norm):
    B, L, _ = c_q.shape
    q = (rms_norm(c_q, q_a_norm) @ w_uq).reshape(B, L, N_HEADS, QK_HEAD)
    kv = (rms_norm(c_kv, kv_a_norm) @ w_ukv).reshape(B, L, N_HEADS, QK_NOPE + V_HEAD)
    k_nope, v = kv[..., :QK_NOPE], kv[..., QK_NOPE:]
    k = jnp.concatenate([k_nope, jnp.broadcast_to(k_pe[:, :, None, :], (B, L, N_HEADS, QK_ROPE))], -1)
    q = rms_norm(q, q_norm)
    k = rms_norm(k, k_norm)
    cos = cos.astype(q.dtype)
    sin = sin.astype(q.dtype)
    q = jnp.concatenate([q[..., :QK_NOPE], rotary(q[..., QK_NOPE:], cos, sin)], -1)
    k = jnp.concatenate([k[..., :QK_NOPE], rotary(k[..., QK_NOPE:], cos, sin)], -1)
    q = q.transpose(0, 2, 1, 3)
    k = k.transpose(0, 2, 1, 3)
    v = v.transpose(0, 2, 1, 3)
    scale = QK_HEAD ** -0.5
    chunk_id = jnp.arange(L) // CHUNK
    neg = jnp.finfo(jnp.float32).min
    outs = []
    for i in range(L // Q_BLOCK):
        q0 = i * Q_BLOCK
        kv_len = q0 + Q_BLOCK
        s = jnp.einsum('bhqd,bhkd->bhqk', q[:, :, q0:kv_len], k[:, :, :kv_len]).astype(jnp.float32) * scale
        mask = chunk_id[None, :kv_len] <= chunk_id[q0:kv_len, None]
        s = jnp.where(mask, s, neg)
        p = jax.nn.softmax(s, axis=-1).astype(v.dtype)
        outs.append(jnp.einsum('bhqk,bhkd->bqhd', p, v[:, :, :kv_len]))
    o = jnp.concatenate(outs, axis=1)
    return o.reshape(B, L, ATTN_WIDTH)


def s5_mixer(u, lam_re, lam_im, b_re, b_im, c_re, c_im, d_skip, log_dt, w_glu):
    B, L, _ = u.shape
    f32 = jnp.float32
    lam = lax.complex(lam_re.astype(f32), lam_im.astype(f32))
    dt = jnp.exp(log_dt.astype(f32))[:, None]
    lam_bar = jnp.exp(lam * dt)
    b_mat = lax.complex(b_re.astype(f32), b_im.astype(f32))
    b_bar = ((lam_bar - 1.0) / lam)[..., None] * b_mat
    c_mat = lax.complex(c_re.astype(f32), c_im.astype(f32))
    ug = u.astype(f32).reshape(B, L, SSM_GROUPS, SSM_GROUP)
    bu = jnp.einsum('blgc,gpc->blgp', ug.astype(jnp.complex64), b_bar)
    a = jnp.broadcast_to(lam_bar, bu.shape)

    def combine(e1, e2):
        a1, b1 = e1
        a2, b2 = e2
        return a1 * a2, a2 * b1 + b2

    _, xs = lax.associative_scan(combine, (a, bu), axis=1)
    y = jnp.einsum('blgp,gcp->blgc', xs, c_mat).real + d_skip.astype(f32) * ug
    y = y.reshape(B, L, SSM_WIDTH)
    z = jax.nn.gelu(y)
    out = z * jax.nn.sigmoid(z @ w_glu.astype(f32))
    return out.astype(u.dtype)


def conv_mixer(a, g, dw_w, dw_b, ln_w, ln_b):
    u = a * jax.nn.sigmoid(g)
    u = jnp.pad(u, ((0, 0), (CONV_K - 1, 0), (0, 0)))
    y = lax.conv_general_dilated(u, dw_w[:, None, :].astype(u.dtype), window_strides=(1,),
                                 padding='VALID', dimension_numbers=('NWC', 'WIO', 'NWC'),
                                 feature_group_count=CONV_WIDTH)
    y = layer_norm(y + dw_b.astype(y.dtype), ln_w, ln_b)
    return jax.nn.silu(y)


def grouped_experts(xf, expert_id, gates, w1, w3, w2):
    N, D = xf.shape
    A = N * TOP_K
    flat_e = expert_id.reshape(A)
    flat_tok = jnp.repeat(jnp.arange(N, dtype=jnp.int32), TOP_K)
    flat_gate = gates.reshape(A)
    order = jnp.argsort(flat_e)
    e_sorted = flat_e[order]
    counts = jnp.bincount(flat_e, length=N_EXPERTS)
    padded = (counts + MOE_BLOCK - 1) // MOE_BLOCK * MOE_BLOCK
    pad_end = jnp.cumsum(padded)
    pad_start = pad_end - padded
    start = jnp.cumsum(counts) - counts
    dest = pad_start[e_sorted] + jnp.arange(A, dtype=jnp.int32) - start[e_sorted]
    n_rows = (A + MOE_BLOCK - 1) // MOE_BLOCK * MOE_BLOCK + N_EXPERTS * MOE_BLOCK
    n_blocks = n_rows // MOE_BLOCK
    row_tok = jnp.full((n_rows,), N, jnp.int32).at[dest].set(flat_tok[order])
    row_gate = jnp.zeros((n_rows,), flat_gate.dtype).at[dest].set(flat_gate[order])
    block_e = jnp.minimum(jnp.searchsorted(pad_end, jnp.arange(n_blocks, dtype=pad_end.dtype) * MOE_BLOCK,
                                           side='right'), N_EXPERTS - 1)
    x_pad = jnp.concatenate([xf, jnp.zeros((1, D), xf.dtype)], axis=0)
    x_rows = x_pad[row_tok].reshape(n_blocks, MOE_BLOCK, D)

    def expert_block(args):
        xb, e = args
        hb = jax.nn.silu(xb @ w1[e]) * (xb @ w3[e])
        return hb @ w2[e]

    y_rows = lax.map(expert_block, (x_rows, block_e)).reshape(n_rows, D)
    y_rows = y_rows * row_gate[:, None].astype(y_rows.dtype)
    return jax.ops.segment_sum(y_rows, row_tok, num_segments=N + 1)[:N]


def hier_moe(h, w_grp, b_grp, w_exp, b_exp, w1, w3, w2):
    B, L, D = h.shape
    xf = h.reshape(B * L, D)
    N = xf.shape[0]
    rows = jnp.arange(N)
    grp_logits = (xf @ w_grp).astype(jnp.float32) + b_grp.astype(jnp.float32)
    p_grp = jax.nn.softmax(grp_logits, axis=-1)
    g_sel = jnp.argmax(grp_logits, axis=-1)
    exp_logits = ((xf @ w_exp).astype(jnp.float32) + b_exp.astype(jnp.float32)).reshape(N, N_EGROUPS, EXP_PER_GROUP)
    p_in = jax.nn.softmax(exp_logits[rows, g_sel], axis=-1)
    top_p, top_i = lax.top_k(p_in, TOP_K)
    top_p = top_p / jnp.sum(top_p, axis=-1, keepdims=True)
    gates = p_grp[rows, g_sel][:, None] * top_p
    expert_id = (g_sel[:, None] * EXP_PER_GROUP + top_i).astype(jnp.int32)
    y = grouped_experts(xf, expert_id, gates, w1, w3, w2)
    return y.reshape(B, L, D)


def setup_inputs(seed: int = 0) -> dict:
    key = jax.random.key(seed)
    ks = iter(jax.random.split(key, 40))
    f32 = jnp.float32

    def nrm(shape, scale):
        return jax.random.normal(next(ks), shape, f32) * scale

    def gain(shape):
        return 1.0 + 0.01 * jax.random.normal(next(ks), shape, f32)

    Lr = DEPTH
    x = jax.random.normal(next(ks), (BATCH, SEQ, D_MODEL), f32)
    offsets = jax.random.randint(next(ks), (BATCH, 1), 0, 8192, jnp.int32)
    positions = offsets + jnp.arange(SEQ, dtype=jnp.int32)[None, :]
    n_idx = jnp.arange(SSM_STATE, dtype=f32)
    lam_re = -0.5 + 0.01 * jax.random.normal(next(ks), (Lr, SSM_GROUPS, SSM_STATE), f32)
    lam_im = math.pi * n_idx + 0.01 * jax.random.normal(next(ks), (Lr, SSM_GROUPS, SSM_STATE), f32)
    log_dt = jax.random.uniform(next(ks), (Lr, SSM_GROUPS), f32, math.log(DT_MIN), math.log(DT_MAX))
    return {
        "x": x,
        "positions": positions,
        "mix_norm": gain((Lr, D_MODEL)),
        "w_in": nrm((Lr, D_MODEL, IN_PROJ), D_MODEL ** -0.5),
        "q_a_norm": gain((Lr, Q_LORA)),
        "w_uq": nrm((Lr, Q_LORA, N_HEADS * QK_HEAD), Q_LORA ** -0.5),
        "kv_a_norm": gain((Lr, KV_LORA)),
        "w_ukv": nrm((Lr, KV_LORA, N_HEADS * (QK_NOPE + V_HEAD)), KV_LORA ** -0.5),
        "q_norm": gain((Lr, QK_HEAD)),
        "k_norm": gain((Lr, QK_HEAD)),
        "ssm_lam_re": lam_re,
        "ssm_lam_im": lam_im,
        "ssm_b_re": nrm((Lr, SSM_GROUPS, SSM_STATE, SSM_GROUP), (2 * SSM_GROUP) ** -0.5),
        "ssm_b_im": nrm((Lr, SSM_GROUPS, SSM_STATE, SSM_GROUP), (2 * SSM_GROUP) ** -0.5),
        "ssm_c_re": nrm((Lr, SSM_GROUPS, SSM_GROUP, SSM_STATE), (2 * SSM_STATE) ** -0.5),
        "ssm_c_im": nrm((Lr, SSM_GROUPS, SSM_GROUP, SSM_STATE), (2 * SSM_STATE) ** -0.5),
        "ssm_d": nrm((Lr, SSM_GROUPS, SSM_GROUP), 0.5),
        "ssm_log_dt": log_dt,
        "ssm_w_glu": nrm((Lr, SSM_WIDTH, SSM_WIDTH), SSM_WIDTH ** -0.5),
        "conv_dw_w": nrm((Lr, CONV_K, CONV_WIDTH), CONV_K ** -0.5),
        "conv_dw_b": nrm((Lr, CONV_WIDTH), 0.01),
        "conv_ln_w": gain((Lr, CONV_WIDTH)),
        "conv_ln_b": nrm((Lr, CONV_WIDTH), 0.01),
        "out_norm": gain((Lr, MIX_WIDTH)),
        "w_out": nrm((Lr, MIX_WIDTH, D_MODEL), MIX_WIDTH ** -0.5),
        "ffn_norm": gain((Lr, D_MODEL)),
        "w_grp": nrm((Lr, D_MODEL, N_EGROUPS), D_MODEL ** -0.5),
        "b_grp": nrm((Lr, N_EGROUPS), 0.01),
        "w_exp": nrm((Lr, D_MODEL, N_EXPERTS), D_MODEL ** -0.5),
        "b_exp": nrm((Lr, N_EXPERTS), 0.01),
        "w1": nrm((Lr, N_EXPERTS, D_MODEL, D_FF_E), D_MODEL ** -0.5),
        "w3": nrm((Lr, N_EXPERTS, D_MODEL, D_FF_E), D_MODEL ** -0.5),
        "w2": nrm((Lr, N_EXPERTS, D_FF_E, D_MODEL), D_FF_E ** -0.5),
    }


def reference(x, positions, mix_norm, w_in, q_a_norm, w_uq, kv_a_norm, w_ukv, q_norm, k_norm,
              ssm_lam_re, ssm_lam_im, ssm_b_re, ssm_b_im, ssm_c_re, ssm_c_im, ssm_d, ssm_log_dt, ssm_w_glu,
              conv_dw_w, conv_dw_b, conv_ln_w, conv_ln_b, out_norm, w_out,
              ffn_norm, w_grp, b_grp, w_exp, b_exp, w1, w3, w2):
    inv_freq = ROPE_THETA ** (-jnp.arange(0, QK_ROPE, 2, dtype=jnp.float32) / QK_ROPE)
    ang = positions.astype(jnp.float32)[..., None] * inv_freq
    cos = jnp.cos(ang)[:, :, None, :]
    sin = jnp.sin(ang)[:, :, None, :]
    for l in range(DEPTH):
        h = rms_norm(x, mix_norm[l])
        proj = h @ w_in[l]
        c_q, c_kv, k_pe, u_ssm, conv_a, conv_g = jnp.split(proj, IN_SPLITS, axis=-1)
        y_attn = mla_mixer(c_q, c_kv, k_pe, cos, sin, q_a_norm[l], w_uq[l], kv_a_norm[l], w_ukv[l],
                           q_norm[l], k_norm[l])
        y_ssm = s5_mixer(u_ssm, ssm_lam_re[l], ssm_lam_im[l], ssm_b_re[l], ssm_b_im[l], ssm_c_re[l],
                         ssm_c_im[l], ssm_d[l], ssm_log_dt[l], ssm_w_glu[l])
        y_conv = conv_mixer(conv_a, conv_g, conv_dw_w[l], conv_dw_b[l], conv_ln_w[l], conv_ln_b[l])
        g = out_norm[l]
        y = jnp.concatenate([
            rms_norm(y_attn, g[:ATTN_WIDTH]),
            rms_norm(y_ssm, g[ATTN_WIDTH:ATTN_WIDTH + SSM_WIDTH]),
            rms_norm(y_conv, g[ATTN_WIDTH + SSM_WIDTH:]),
        ], axis=-1)
        x = x + y @ w_out[l]
        x = x + hier_moe(rms_norm(x, ffn_norm[l]), w_grp[l], b_grp[l], w_exp[l], b_exp[l],
                         w1[l], w3[l], w2[l])
    return x
```

```python
import functools
import math

import jax
import jax.numpy as jnp
from jax import lax
from jax.experimental import pallas as pl
from jax.experimental.pallas import tpu as pltpu

D_MODEL = 1024
CHUNK = 64
EPS = 1e-6
N_HEADS = 8
QK_NOPE = 64
QK_ROPE = 32
QK_HEAD = QK_NOPE + QK_ROPE
V_HEAD = 64
Q_LORA = 256
KV_LORA = 128
ROPE_THETA = 10000.0
ATTN_WIDTH = N_HEADS * V_HEAD
SSM_WIDTH = 256
SSM_GROUP = 16
SSM_GROUPS = SSM_WIDTH // SSM_GROUP
SSM_STATE = 64
SSM_FLAT = SSM_GROUPS * SSM_STATE
CONV_WIDTH = 256
CONV_K = 31
N_EGROUPS = 4
EXP_PER_GROUP = 8
N_EXPERTS = N_EGROUPS * EXP_PER_GROUP
D_FF_E = 512

LANES = 128
HEAD_PAD = LANES
IN_PROJ_PAD = Q_LORA + KV_LORA + SSM_WIDTH + 2 * CONV_WIDTH + LANES
CONV_HALO = 32
VMEM_LIMIT = 48 * 1024 * 1024

TM_PROJ = 512
TQ_ATTN = 256
TC_SSM = 64
T_CONV = 256
CONV_SUB = 64
TM_OUT = 256
TB_EXP = 256

BF16 = jnp.bfloat16
F32 = jnp.float32


def _rms(x, w):
    return x * lax.rsqrt(jnp.mean(x * x, axis=-1, keepdims=True) + EPS) * w


def _params(*sem):
    return pltpu.CompilerParams(dimension_semantics=sem, vmem_limit_bytes=VMEM_LIMIT)


def _in_proj_kernel(x_ref, rope_ref, mixn_ref, win_ref, qan_ref, wuq_ref, kvan_ref, wuk_ref, wuv_ref,
                    qn_ref, kn_ref, q_ref, k_ref, v_ref, u_ref, ca_ref, cg_ref):
    x = x_ref[0]
    h = _rms(x, mixn_ref[...]).astype(BF16)
    proj = jnp.dot(h, win_ref[...], preferred_element_type=F32)
    o = 0
    c_q = proj[:, o:o + Q_LORA]; o += Q_LORA
    c_kv = proj[:, o:o + KV_LORA]; o += KV_LORA
    u_ref[...] = proj[:, o:o + SSM_WIDTH].astype(BF16); o += SSM_WIDTH
    ca_ref[0] = proj[:, o:o + CONV_WIDTH].astype(BF16); o += CONV_WIDTH
    cg_ref[0] = proj[:, o:o + CONV_WIDTH].astype(BF16); o += CONV_WIDTH
    k_pe = proj[:, o:o + LANES]

    q = jnp.dot(_rms(c_q, qan_ref[...]).astype(BF16), wuq_ref[...], preferred_element_type=F32)
    ckv_n = _rms(c_kv, kvan_ref[...]).astype(BF16)
    kn = jnp.dot(ckv_n, wuk_ref[...], preferred_element_type=F32)
    v_ref[0] = jnp.dot(ckv_n, wuv_ref[...], preferred_element_type=F32).astype(BF16)

    rope = rope_ref[0]
    cos_t = rope[:, 0:LANES]
    sin_lo = rope[:, LANES:2 * LANES]
    sin_hi = rope[:, 2 * LANES:3 * LANES]
    half = QK_ROPE // 2

    def head_norm_rope(xh, w):
        ss = jnp.sum(xh * xh, axis=-1, keepdims=True) * (1.0 / QK_HEAD)
        xn = xh * lax.rsqrt(ss + EPS) * w
        return (xn * cos_t + pltpu.roll(xn, LANES - half, 1) * sin_lo
                + pltpu.roll(xn, half, 1) * sin_hi)

    for hd in range(N_HEADS):
        sl = slice(hd * HEAD_PAD, (hd + 1) * HEAD_PAD)
        q_ref[0, hd] = head_norm_rope(q[:, sl], qn_ref[...]).astype(BF16)
        k_ref[0, hd] = head_norm_rope(kn[:, sl] + k_pe, kn_ref[...]).astype(BF16)


def _in_proj(x, rope, mixn, win, qan, wuq, kvan, wuk, wuv, qn, kn):
    B, L, D = x.shape
    tm = min(TM_PROJ, L)
    full = lambda shape: pl.BlockSpec(shape, lambda b, t: (0,) * len(shape))
    return pl.pallas_call(
        _in_proj_kernel,
        grid=(B, L // tm),
        in_specs=[
            pl.BlockSpec((1, tm, D), lambda b, t: (b, t, 0)),
            pl.BlockSpec((1, tm, 3 * LANES), lambda b, t: (b, t, 0)),
            full((1, D)), full((D, IN_PROJ_PAD)), full((1, Q_LORA)), full((Q_LORA, N_HEADS * HEAD_PAD)),
            full((1, KV_LORA)), full((KV_LORA, N_HEADS * HEAD_PAD)), full((KV_LORA, ATTN_WIDTH)),
            full((1, HEAD_PAD)), full((1, HEAD_PAD)),
        ],
        out_specs=[
            pl.BlockSpec((1, N_HEADS, tm, HEAD_PAD), lambda b, t: (b, 0, t, 0)),
            pl.BlockSpec((1, N_HEADS, tm, HEAD_PAD), lambda b, t: (b, 0, t, 0)),
            pl.BlockSpec((1, tm, ATTN_WIDTH), lambda b, t: (b, t, 0)),
            pl.BlockSpec((tm, SSM_WIDTH), lambda b, t: (t, b)),
            pl.BlockSpec((1, tm, CONV_WIDTH), lambda b, t: (b, t, 0)),
            pl.BlockSpec((1, tm, CONV_WIDTH), lambda b, t: (b, t, 0)),
        ],
        out_shape=[
            jax.ShapeDtypeStruct((B, N_HEADS, L, HEAD_PAD), BF16),
            jax.ShapeDtypeStruct((B, N_HEADS, L, HEAD_PAD), BF16),
            jax.ShapeDtypeStruct((B, L, ATTN_WIDTH), BF16),
            jax.ShapeDtypeStruct((L, B * SSM_WIDTH), BF16),
            jax.ShapeDtypeStruct((B, L, CONV_WIDTH), BF16),
            jax.ShapeDtypeStruct((B, L, CONV_WIDTH), BF16),
        ],
        compiler_params=_params("parallel", "parallel"),
    )(x, rope, mixn, win, qan, wuq, kvan, wuk, wuv, qn, kn)


def _attention_kernel(q_ref, k_ref, v_ref, o_ref, *, seq, tq):
    row_chunk = lax.broadcasted_iota(jnp.int32, (tq, tq), 0) // CHUNK
    col_chunk = lax.broadcasted_iota(jnp.int32, (tq, tq), 1) // CHUNK
    visible = col_chunk <= row_chunk
    neg = jnp.finfo(F32).min
    nt = (((1,), (1,)), ((), ()))
    low_lanes = lax.broadcasted_iota(jnp.int32, (tq, 2 * V_HEAD), 1) < V_HEAD
    for i in range(seq // tq):
        q0 = i * tq
        outs = []
        for hh in range(2):
            qb = q_ref[0, hh, q0:q0 + tq, :]
            s_d = lax.dot_general(qb, k_ref[0, hh, q0:q0 + tq, :], nt, preferred_element_type=F32)
            s_d = jnp.where(visible, s_d, neg)
            m = jnp.max(s_d, axis=-1, keepdims=True)
            if i > 0:
                s_l = lax.dot_general(qb, k_ref[0, hh, 0:q0, :], nt, preferred_element_type=F32)
                m = jnp.maximum(m, jnp.max(s_l, axis=-1, keepdims=True))
            p_d = jnp.exp(s_d - m)
            denom = jnp.sum(p_d, axis=-1, keepdims=True)
            acc = jnp.dot(p_d.astype(BF16), v_ref[0, q0:q0 + tq, :], preferred_element_type=F32)
            if i > 0:
                p_l = jnp.exp(s_l - m)
                denom = denom + jnp.sum(p_l, axis=-1, keepdims=True)
                acc = acc + jnp.dot(p_l.astype(BF16), v_ref[0, 0:q0, :], preferred_element_type=F32)
            outs.append(acc / denom)
        o_ref[0, q0:q0 + tq, :] = jnp.where(low_lanes, outs[0], outs[1]).astype(BF16)


def _attention(q, k, v):
    B, H, L, _ = q.shape
    tq = min(TQ_ATTN, L)
    return pl.pallas_call(
        functools.partial(_attention_kernel, seq=L, tq=tq),
        grid=(B, H // 2),
        in_specs=[
            pl.BlockSpec((1, 2, L, HEAD_PAD), lambda b, p: (b, p, 0, 0)),
            pl.BlockSpec((1, 2, L, HEAD_PAD), lambda b, p: (b, p, 0, 0)),
            pl.BlockSpec((1, L, 2 * V_HEAD), lambda b, p: (b, 0, p)),
        ],
        out_specs=pl.BlockSpec((1, L, 2 * V_HEAD), lambda b, p: (b, 0, p)),
        out_shape=jax.ShapeDtypeStruct((B, L, ATTN_WIDTH), BF16),
        compiler_params=_params("parallel", "parallel"),
    )(q, k, v)


def _ssm_kernel(u_ref, bbd_ref, lam_ref, cbd_ref, d_ref, wglu_ref, g_ref, o_ref, state_ref, bu_ref, *, batch, tc):
    @pl.when(pl.program_id(0) == 0)
    def _():
        state_ref[...] = jnp.zeros_like(state_ref)

    u = u_ref[...]
    bu_ref[...] = jnp.dot(u, bbd_ref[...], preferred_element_type=F32)
    lam_re = jnp.broadcast_to(lam_ref[0:1, :], (batch, SSM_FLAT))
    lam_im = jnp.broadcast_to(lam_ref[1:2, :], (batch, SSM_FLAT))

    def step(t, carry):
        xr, xi = carry
        rows = pl.ds(pl.multiple_of(t * batch, batch), batch)
        nr = lam_re * xr - lam_im * xi + bu_ref[rows, 0:SSM_FLAT]
        ni = lam_re * xi + lam_im * xr + bu_ref[rows, SSM_FLAT:2 * SSM_FLAT]
        bu_ref[rows, 0:SSM_FLAT] = nr
        bu_ref[rows, SSM_FLAT:2 * SSM_FLAT] = ni
        return nr, ni

    xr, xi = lax.fori_loop(0, tc, step, (state_ref[:, 0:SSM_FLAT], state_ref[:, SSM_FLAT:2 * SSM_FLAT]),
                           unroll=2)
    state_ref[:, 0:SSM_FLAT] = xr
    state_ref[:, SSM_FLAT:2 * SSM_FLAT] = xi

    y = jnp.dot(bu_ref[...].astype(BF16), cbd_ref[...], preferred_element_type=F32)
    y = y + d_ref[...] * u.astype(F32)
    z = jax.nn.gelu(y)
    gate = jax.nn.sigmoid(jnp.dot(z.astype(BF16), wglu_ref[...], preferred_element_type=F32))
    o_ref[...] = _rms(z * gate, g_ref[...]).astype(BF16)


def _ssm(u_tm, bbd, lam, cbd, d, wglu, g, batch):
    rows = u_tm.shape[0]
    seq = rows // batch
    tc = min(TC_SSM, seq)
    blk = tc * batch
    full = lambda shape: pl.BlockSpec(shape, lambda t: (0,) * len(shape))
    return pl.pallas_call(
        functools.partial(_ssm_kernel, batch=batch, tc=tc),
        grid=(seq // tc,),
        in_specs=[
            pl.BlockSpec((blk, SSM_WIDTH), lambda t: (t, 0)),
            full((SSM_WIDTH, 2 * SSM_FLAT)), full((2, SSM_FLAT)), full((2 * SSM_FLAT, SSM_WIDTH)),
            full((1, SSM_WIDTH)), full((SSM_WIDTH, SSM_WIDTH)), full((1, SSM_WIDTH)),
        ],
        out_specs=pl.BlockSpec((blk, SSM_WIDTH), lambda t: (t, 0)),
        out_shape=jax.ShapeDtypeStruct((rows, SSM_WIDTH), BF16),
        scratch_shapes=[pltpu.VMEM((batch, 2 * SSM_FLAT), F32), pltpu.VMEM((blk, 2 * SSM_FLAT), F32)],
        compiler_params=_params("arbitrary"),
    )(u_tm, bbd, lam, cbd, d, wglu, g)


def _conv_kernel(a_ref, g_ref, ah_ref, gh_ref, w_ref, b_ref, lnw_ref, lnb_ref, gn_ref, o_ref, upad_ref, *, tt):
    first = pl.program_id(1) == 0
    halo = ah_ref[0].astype(F32) * jax.nn.sigmoid(gh_ref[0].astype(F32))
    upad_ref[0:CONV_HALO, :] = jnp.where(first, 0.0, halo)
    upad_ref[CONV_HALO:, :] = a_ref[0].astype(F32) * jax.nn.sigmoid(g_ref[0].astype(F32))
    base = CONV_HALO - (CONV_K - 1)
    for s in range(tt // CONV_SUB):
        r0 = s * CONV_SUB
        acc = jnp.zeros((CONV_SUB, CONV_WIDTH), F32)
        for kk in range(CONV_K):
            acc = acc + w_ref[kk:kk + 1, :] * upad_ref[r0 + base + kk:r0 + base + kk + CONV_SUB, :]
        y = acc + b_ref[...]
        mu = jnp.mean(y, axis=-1, keepdims=True)
        var = jnp.mean(jnp.square(y - mu), axis=-1, keepdims=True)
        y = (y - mu) * lax.rsqrt(var + 1e-5) * lnw_ref[...] + lnb_ref[...]
        y = y * jax.nn.sigmoid(y)
        o_ref[0, r0:r0 + CONV_SUB, :] = _rms(y, gn_ref[...]).astype(BF16)


def _conv(a, g, w, b, lnw, lnb, gn):
    B, L, C = a.shape
    tt = min(T_CONV, L)
    ratio = tt // CONV_HALO
    full = lambda shape: pl.BlockSpec(shape, lambda bb, t: (0,) * len(shape))
    cur = pl.BlockSpec((1, tt, C), lambda bb, t: (bb, t, 0))
    prev = pl.BlockSpec((1, CONV_HALO, C), lambda bb, t: (bb, jnp.maximum(t * ratio - 1, 0), 0))
    return pl.pallas_call(
        functools.partial(_conv_kernel, tt=tt),
        grid=(B, L // tt),
        in_specs=[cur, cur, prev, prev, full((CONV_K, C)), full((1, C)), full((1, C)), full((1, C)), full((1, C))],
        out_specs=pl.BlockSpec((1, tt, C), lambda bb, t: (bb, t, 0)),
        out_shape=jax.ShapeDtypeStruct((B, L, C), BF16),
        scratch_shapes=[pltpu.VMEM((tt + CONV_HALO, C), F32)],
        compiler_params=_params("parallel", "parallel"),
    )(a, g, a, g, w, b, lnw, lnb, gn)


ROUTE_E1, ROUTE_E2, ROUTE_R1, ROUTE_R2, ROUTE_G1, ROUTE_G2 = range(6)
GRP_LANE0 = N_EXPERTS


def _out_router_kernel(x_ref, ya_ref, ys_ref, yc_ref, ga_ref, wo_ref, fn_ref, wr_ref, br_ref,
                       x1_ref, h2_ref, route_ref, cnt_ref, carry_ref, *, tm):
    @pl.when(pl.program_id(0) == 0)
    def _():
        carry_ref[...] = jnp.zeros_like(carry_ref)

    ya = _rms(ya_ref[...].astype(F32), ga_ref[...]).astype(BF16)
    acc = jnp.dot(ya, wo_ref[0:ATTN_WIDTH, :], preferred_element_type=F32)
    acc += jnp.dot(ys_ref[...], wo_ref[ATTN_WIDTH:ATTN_WIDTH + SSM_WIDTH, :], preferred_element_type=F32)
    acc += jnp.dot(yc_ref[...], wo_ref[ATTN_WIDTH + SSM_WIDTH:, :], preferred_element_type=F32)
    x1 = x_ref[...] + acc
    x1_ref[...] = x1
    h2 = _rms(x1, fn_ref[...])
    h2_ref[...] = h2

    logits = jnp.dot(h2, wr_ref[...], preferred_element_type=F32, precision=lax.Precision.HIGHEST) + br_ref[...]
    lane = lax.broadcasted_iota(jnp.int32, (tm, LANES), 1)
    ninf = -jnp.inf
    big = LANES

    def first_argmax(vals, vmax):
        return jnp.min(jnp.where(vals == vmax, lane, big), axis=-1, keepdims=True)

    grp = jnp.where((lane >= GRP_LANE0) & (lane < GRP_LANE0 + N_EGROUPS), logits, ninf)
    gmax = jnp.max(grp, axis=-1, keepdims=True)
    gsel = first_argmax(grp, gmax) - GRP_LANE0
    p_grp = 1.0 / jnp.sum(jnp.exp(grp - gmax), axis=-1, keepdims=True)

    el = jnp.where((lane < N_EXPERTS) & ((lane // EXP_PER_GROUP) == gsel), logits, ninf)
    m1 = jnp.max(el, axis=-1, keepdims=True)
    e1 = first_argmax(el, m1)
    el2 = jnp.where(lane == e1, ninf, el)
    m2 = jnp.max(el2, axis=-1, keepdims=True)
    e2 = first_argmax(el2, m2)
    t2 = jnp.exp(m2 - m1)
    g1 = p_grp / (1.0 + t2)
    g2 = p_grp * t2 / (1.0 + t2)

    hit1 = lane == e1
    hit2 = lane == e2
    cnt = (hit1 | hit2).astype(F32)
    rr = lax.broadcasted_iota(jnp.int32, (tm, tm), 0)
    cc = lax.broadcasted_iota(jnp.int32, (tm, tm), 1)
    tri = (cc < rr).astype(BF16)
    before = jnp.dot(tri, cnt.astype(BF16), preferred_element_type=F32) + carry_ref[...]
    r1 = jnp.sum(jnp.where(hit1, before, 0.0), axis=-1, keepdims=True)
    r2 = jnp.sum(jnp.where(hit2, before, 0.0), axis=-1, keepdims=True)
    carry_ref[...] += jnp.sum(cnt, axis=0, keepdims=True)
    cnt_ref[...] = carry_ref[...]

    rec = jnp.where(lane == ROUTE_E1, e1.astype(F32), 0.0)
    rec = jnp.where(lane == ROUTE_E2, e2.astype(F32), rec)
    rec = jnp.where(lane == ROUTE_R1, r1, rec)
    rec = jnp.where(lane == ROUTE_R2, r2, rec)
    rec = jnp.where(lane == ROUTE_G1, g1, rec)
    rec = jnp.where(lane == ROUTE_G2, g2, rec)
    route_ref[...] = rec


def _out_router(x, ya, ys_tm, yc, ga, wo, fn, wr, br, batch):
    N, D = x.shape
    seq = N // batch
    tm = min(TM_OUT, seq)
    per_b = seq // tm
    full = lambda shape: pl.BlockSpec(shape, lambda i: (0,) * len(shape))
    tile = lambda w: pl.BlockSpec((tm, w), lambda i: (i, 0))
    return pl.pallas_call(
        functools.partial(_out_router_kernel, tm=tm),
        grid=(N // tm,),
        in_specs=[
            tile(D), tile(ATTN_WIDTH),
            pl.BlockSpec((tm, SSM_WIDTH), lambda i: (i % per_b, i // per_b)),
            tile(CONV_WIDTH),
            full((1, ATTN_WIDTH)), full((D, D)), full((1, D)), full((D, LANES)), full((1, LANES)),
        ],
        out_specs=[tile(D), tile(D), tile(LANES), full((1, LANES))],
        out_shape=[
            jax.ShapeDtypeStruct((N, D), F32),
            jax.ShapeDtypeStruct((N, D), F32),
            jax.ShapeDtypeStruct((N, LANES), F32),
            jax.ShapeDtypeStruct((1, LANES), F32),
        ],
        scratch_shapes=[pltpu.VMEM((1, LANES), F32)],
        compiler_params=_params("arbitrary"),
    )(x, ya, ys_tm, yc, ga, wo, fn, wr, br)


def _dispatch_kernel(dest_ref, h_ref, rows_ref, sem, *, tm):
    base = pl.program_id(0) * tm * 2

    def issue(r, c):
        src = h_ref.at[pl.ds(r, 1)]
        pltpu.make_async_copy(src, rows_ref.at[pl.ds(dest_ref[base + 2 * r], 1)], sem).start()
        pltpu.make_async_copy(src, rows_ref.at[pl.ds(dest_ref[base + 2 * r + 1], 1)], sem).start()
        return c

    lax.fori_loop(0, tm, issue, 0)
    for _ in range(2):
        pltpu.make_async_copy(h_ref, rows_ref.at[pl.ds(0, tm)], sem).wait()


def _dispatch(dest_flat, h2, n_rows):
    N, D = h2.shape
    tm = min(TM_OUT, N)
    return pl.pallas_call(
        functools.partial(_dispatch_kernel, tm=tm),
        grid_spec=pltpu.PrefetchScalarGridSpec(
            num_scalar_prefetch=1,
            grid=(N // tm,),
            in_specs=[pl.BlockSpec((tm, D), lambda i, d: (i, 0))],
            out_specs=pl.BlockSpec(memory_space=pl.ANY),
            scratch_shapes=[pltpu.SemaphoreType.DMA(())],
        ),
        out_shape=jax.ShapeDtypeStruct((n_rows, D), F32),
        compiler_params=_params("arbitrary"),
    )(dest_flat, h2)


def _experts_kernel(blk_ref, exp_ref, used_ref, x_ref, w1_ref, w3_ref, w2_ref, y_ref, w13_s, w2_s):
    j = pl.program_id(0)

    @pl.when(j < used_ref[0])
    def _():
        prev = exp_ref[jnp.maximum(j - 1, 0)]

        @pl.when((j == 0) | (exp_ref[j] != prev))
        def _():
            w13_s[:, 0:D_FF_E] = w1_ref[0].astype(BF16)
            w13_s[:, D_FF_E:] = w3_ref[0].astype(BF16)
            w2_s[...] = w2_ref[0].astype(BF16)

        x = x_ref[...].astype(BF16)
        ab = jnp.dot(x, w13_s[...], preferred_element_type=F32)
        a = ab[:, 0:D_FF_E]
        hmid = (a * jax.nn.sigmoid(a) * ab[:, D_FF_E:]).astype(BF16)
        y_ref[...] = jnp.dot(hmid, w2_s[...], preferred_element_type=F32)


def _experts(blk_map, blk_exp, n_used, x_rows, w1, w3, w2):
    R, D = x_rows.shape
    nb = R // TB_EXP
    return pl.pallas_call(
        _experts_kernel,
        grid_spec=pltpu.PrefetchScalarGridSpec(
            num_scalar_prefetch=3,
            grid=(nb,),
            in_specs=[
                pl.BlockSpec((TB_EXP, D), lambda j, bm, be, nu: (bm[j], 0)),
                pl.BlockSpec((1, D, D_FF_E), lambda j, bm, be, nu: (be[j], 0, 0)),
                pl.BlockSpec((1, D, D_FF_E), lambda j, bm, be, nu: (be[j], 0, 0)),
                pl.BlockSpec((1, D_FF_E, D), lambda j, bm, be, nu: (be[j], 0, 0)),
            ],
            out_specs=pl.BlockSpec((TB_EXP, D), lambda j, bm, be, nu: (bm[j], 0)),
            scratch_shapes=[pltpu.VMEM((D, 2 * D_FF_E), BF16), pltpu.VMEM((D_FF_E, D), BF16)],
        ),
        out_shape=jax.ShapeDtypeStruct((R, D), F32),
        compiler_params=_params("arbitrary"),
    )(blk_map, blk_exp, n_used, x_rows, w1, w3, w2)


def _combine_kernel(dest_ref, x1_ref, route_ref, rows_ref, o_ref, buf, sem, *, tm):
    base = pl.program_id(0) * tm * 2

    def issue(r, c):
        pltpu.make_async_copy(rows_ref.at[pl.ds(dest_ref[base + 2 * r], 1)], buf.at[0, pl.ds(r, 1)], sem).start()
        pltpu.make_async_copy(rows_ref.at[pl.ds(dest_ref[base + 2 * r + 1], 1)], buf.at[1, pl.ds(r, 1)], sem).start()
        return c

    lax.fori_loop(0, tm, issue, 0)
    for k in range(2):
        pltpu.make_async_copy(rows_ref.at[pl.ds(0, tm)], buf.at[k], sem).wait()
    route = route_ref[...]
    g1 = route[:, ROUTE_G1:ROUTE_G1 + 1]
    g2 = route[:, ROUTE_G2:ROUTE_G2 + 1]
    o_ref[...] = x1_ref[...] + g1 * buf[0] + g2 * buf[1]


def _combine(dest_flat, x1, route, y_rows):
    N, D = x1.shape
    tm = min(TM_OUT, N)
    return pl.pallas_call(
        functools.partial(_combine_kernel, tm=tm),
        grid_spec=pltpu.PrefetchScalarGridSpec(
            num_scalar_prefetch=1,
            grid=(N // tm,),
            in_specs=[
                pl.BlockSpec((tm, D), lambda i, d: (i, 0)),
                pl.BlockSpec((tm, LANES), lambda i, d: (i, 0)),
                pl.BlockSpec(memory_space=pl.ANY),
            ],
            out_specs=pl.BlockSpec((tm, D), lambda i, d: (i, 0)),
            scratch_shapes=[pltpu.VMEM((2, tm, D), F32), pltpu.SemaphoreType.DMA(())],
        ),
        out_shape=jax.ShapeDtypeStruct((N, D), F32),
        compiler_params=_params("arbitrary"),
    )(dest_flat, x1, route, y_rows)


def _moe(x1, h2, route, counts, w1, w3, w2):
    N, _ = x1.shape
    nb = (2 * N + N_EXPERTS * (TB_EXP - 1)) // TB_EXP + 1
    e_id = route[:, ROUTE_E1:ROUTE_E2 + 1].astype(jnp.int32)
    rank = route[:, ROUTE_R1:ROUTE_R2 + 1].astype(jnp.int32)
    cnt = counts[0, :N_EXPERTS].astype(jnp.int32)
    padded = (cnt + TB_EXP - 1) // TB_EXP * TB_EXP
    pad_end = jnp.cumsum(padded)
    pad_start = pad_end - padded
    dest = (pad_start[e_id] + rank).reshape(2 * N)
    n_used = pad_end[-1] // TB_EXP
    blk_map = jnp.minimum(jnp.arange(nb, dtype=jnp.int32), n_used - 1)
    blk_exp = jnp.minimum(jnp.searchsorted(pad_end, blk_map * TB_EXP, side="right"), N_EXPERTS - 1).astype(jnp.int32)
    x_rows = _dispatch(dest, h2, nb * TB_EXP)
    y_rows = _experts(blk_map, blk_exp, n_used.reshape(1).astype(jnp.int32), x_rows, w1, w3, w2)
    return _combine(dest, x1, route, y_rows)


def _pad_heads(w, width):
    k = w.shape[0]
    w = w.reshape(k, N_HEADS, width)
    return jnp.pad(w, ((0, 0), (0, 0), (0, HEAD_PAD - width))).reshape(k, N_HEADS * HEAD_PAD)


def _rope_tables(positions):
    inv_freq = ROPE_THETA ** (-jnp.arange(0, QK_ROPE, 2, dtype=F32) / QK_ROPE)
    ang = positions.astype(F32)[..., None] * inv_freq
    cos, sin = jnp.cos(ang), jnp.sin(ang)
    ones = jnp.ones(ang.shape[:-1] + (QK_NOPE,), F32)
    z16 = jnp.zeros_like(ang)
    z32 = jnp.zeros(ang.shape[:-1] + (HEAD_PAD - QK_HEAD,), F32)
    z64 = jnp.zeros_like(ones)
    cos_t = jnp.concatenate([ones, cos, cos, z32], -1)
    sin_lo = jnp.concatenate([z64, -sin, z16, z32], -1)
    sin_hi = jnp.concatenate([z64, z16, sin, z32], -1)
    return jnp.concatenate([cos_t, sin_lo, sin_hi], -1)


def _ssm_params(lam_re, lam_im, b_re, b_im, c_re, c_im, log_dt):
    lam = lax.complex(lam_re, lam_im)
    dt = jnp.exp(log_dt)[:, None]
    lam_bar = jnp.exp(lam * dt)
    b_bar = ((lam_bar - 1.0) / lam)[..., None] * lax.complex(b_re, b_im)
    eye = jnp.eye(SSM_GROUPS, dtype=F32)

    def in_blockdiag(m):
        return jnp.einsum("gpc,gh->gchp", m, eye).reshape(SSM_WIDTH, SSM_FLAT)

    def out_blockdiag(m):
        return jnp.einsum("gcp,gh->gphc", m, eye).reshape(SSM_FLAT, SSM_WIDTH)

    bbd = jnp.concatenate([in_blockdiag(jnp.real(b_bar)), in_blockdiag(jnp.imag(b_bar))], axis=1)
    cbd = jnp.concatenate([out_blockdiag(c_re), out_blockdiag(-c_im)], axis=0)
    lam_rows = jnp.stack([jnp.real(lam_bar).reshape(SSM_FLAT), jnp.imag(lam_bar).reshape(SSM_FLAT)])
    return bbd.astype(BF16), lam_rows, cbd.astype(BF16)


def kernel(x, positions, mix_norm, w_in, q_a_norm, w_uq, kv_a_norm, w_ukv, q_norm, k_norm, ssm_lam_re, ssm_lam_im, ssm_b_re, ssm_b_im, ssm_c_re, ssm_c_im, ssm_d, ssm_log_dt, ssm_w_glu, conv_dw_w, conv_dw_b, conv_ln_w, conv_ln_b, out_norm, w_out, ffn_norm, w_grp, b_grp, w_exp, b_exp, w1, w3, w2):
    B, L, D = x.shape
    depth = w_in.shape[0]
    rope = _rope_tables(positions)
    row = lambda v: v.reshape(1, -1)
    lane_pad = lambda v: jnp.pad(v, (0, LANES - v.shape[0])).reshape(1, LANES)
    for l in range(depth):
        c_q, c_kv, k_pe, u_s, c_a, c_g = jnp.split(
            w_in[l], [Q_LORA, Q_LORA + KV_LORA, Q_LORA + KV_LORA + QK_ROPE,
                      Q_LORA + KV_LORA + QK_ROPE + SSM_WIDTH,
                      Q_LORA + KV_LORA + QK_ROPE + SSM_WIDTH + CONV_WIDTH], axis=1)
        k_pe = jnp.pad(k_pe, ((0, 0), (QK_NOPE, LANES - QK_HEAD)))
        win = jnp.concatenate([c_q, c_kv, u_s, c_a, c_g, k_pe], axis=1).astype(BF16)
        wkv = w_ukv[l].reshape(KV_LORA, N_HEADS, QK_NOPE + V_HEAD)
        wuk = _pad_heads(wkv[:, :, :QK_NOPE].reshape(KV_LORA, N_HEADS * QK_NOPE), QK_NOPE).astype(BF16)
        wuv = wkv[:, :, QK_NOPE:].reshape(KV_LORA, ATTN_WIDTH).astype(BF16)
        wuq = _pad_heads(w_uq[l], QK_HEAD).astype(BF16)
        q, k, v, u_tm, ca, cg = _in_proj(
            x, rope, row(mix_norm[l]), win, row(q_a_norm[l]), wuq, row(kv_a_norm[l]), wuk, wuv,
            lane_pad(q_norm[l] * (QK_HEAD ** -0.5)), lane_pad(k_norm[l]))
        y_attn = _attention(q, k, v)

        g_out = out_norm[l]
        bbd, lam_rows, cbd = _ssm_params(ssm_lam_re[l], ssm_lam_im[l], ssm_b_re[l], ssm_b_im[l],
                                         ssm_c_re[l], ssm_c_im[l], ssm_log_dt[l])
        y_ssm_tm = _ssm(u_tm.reshape(L * B, SSM_WIDTH), bbd, lam_rows, cbd, row(ssm_d[l]),
                        ssm_w_glu[l].astype(BF16), row(g_out[ATTN_WIDTH:ATTN_WIDTH + SSM_WIDTH]), B)
        y_conv = _conv(ca, cg, conv_dw_w[l], row(conv_dw_b[l]), row(conv_ln_w[l]), row(conv_ln_b[l]),
                       row(g_out[ATTN_WIDTH + SSM_WIDTH:]))

        w_route = jnp.pad(jnp.concatenate([w_exp[l], w_grp[l]], axis=1), ((0, 0), (0, LANES - N_EXPERTS - N_EGROUPS)))
        b_route = lane_pad(jnp.concatenate([b_exp[l], b_grp[l]]))
        x1, h2, route, counts = _out_router(
            x.reshape(B * L, D), y_attn.reshape(B * L, ATTN_WIDTH), y_ssm_tm.reshape(L, B * SSM_WIDTH),
            y_conv.reshape(B * L, CONV_WIDTH), row(g_out[:ATTN_WIDTH]), w_out[l].astype(BF16),
            row(ffn_norm[l]), w_route, b_route, B)
        x = _moe(x1, h2, route, counts, w1[l], w3[l], w2[l]).reshape(B, L, D)
    return x
```

```python
import functools
import math

import jax
import jax.numpy as jnp
from jax import lax
from jax.experimental import pallas as pl
from jax.experimental.pallas import tpu as pltpu

D_MODEL = 1024
CHUNK = 64
EPS = 1e-6
N_HEADS = 8
QK_NOPE = 64
QK_ROPE = 32
QK_HEAD = QK_NOPE + QK_ROPE
V_HEAD = 64
Q_LORA = 256
KV_LORA = 128
ROPE_THETA = 10000.0
ATTN_WIDTH = N_HEADS * V_HEAD
SSM_WIDTH = 256
SSM_GROUP = 16
SSM_GROUPS = SSM_WIDTH // SSM_GROUP
SSM_STATE = 64
SSM_FLAT = SSM_GROUPS * SSM_STATE
CONV_WIDTH = 256
CONV_K = 31
N_EGROUPS = 4
EXP_PER_GROUP = 8
N_EXPERTS = N_EGROUPS * EXP_PER_GROUP
D_FF_E = 512

LANES = 128
HEAD_PAD = LANES
IN_PROJ_PAD = Q_LORA + KV_LORA + SSM_WIDTH + 2 * CONV_WIDTH + LANES
CONV_HALO = 32
VMEM_LIMIT = 48 * 1024 * 1024

TM_PROJ = 512
TQ_ATTN = 256
TC_SSM = 64
T_CONV = 256
CONV_SUB = 64
TM_OUT = 256
TB_EXP = 256

BF16 = jnp.bfloat16
F32 = jnp.float32


def _rms(x, w):
    return x * lax.rsqrt(jnp.mean(x * x, axis=-1, keepdims=True) + EPS) * w


def _params(*sem):
    return pltpu.CompilerParams(dimension_semantics=sem, vmem_limit_bytes=VMEM_LIMIT)


def _in_proj_kernel(x_ref, rope_ref, mixn_ref, win_ref, qan_ref, wuq_ref, kvan_ref, wuk_ref, wuv_ref,
                    qn_ref, kn_ref, q_ref, k_ref, v_ref, u_ref, ca_ref, cg_ref):
    x = x_ref[0]
    h = _rms(x, mixn_ref[...]).astype(BF16)
    proj = jnp.dot(h, win_ref[...], preferred_element_type=F32)
    o = 0
    c_q = proj[:, o:o + Q_LORA]; o += Q_LORA
    c_kv = proj[:, o:o + KV_LORA]; o += KV_LORA
    u_ref[...] = proj[:, o:o + SSM_WIDTH].astype(BF16); o += SSM_WIDTH
    ca_ref[0] = proj[:, o:o + CONV_WIDTH].astype(BF16); o += CONV_WIDTH
    cg_ref[0] = proj[:, o:o + CONV_WIDTH].astype(BF16); o += CONV_WIDTH
    k_pe = proj[:, o:o + LANES]

    q = jnp.dot(_rms(c_q, qan_ref[...]).astype(BF16), wuq_ref[...], preferred_element_type=F32)
    ckv_n = _rms(c_kv, kvan_ref[...]).astype(BF16)
    kn = jnp.dot(ckv_n, wuk_ref[...], preferred_element_type=F32)
    v_ref[0] = jnp.dot(ckv_n, wuv_ref[...], preferred_element_type=F32).astype(BF16)

    rope = rope_ref[0]
    half = QK_ROPE // 2
    lane = lax.broadcasted_iota(jnp.int32, rope.shape, 1)
    lo = (lane >= QK_NOPE) & (lane < QK_NOPE + half)
    hi = (lane >= QK_NOPE + half) & (lane < QK_HEAD)
    cos_t = jnp.where(lane < QK_NOPE, 1.0, jnp.where(lo, rope, jnp.where(hi, pltpu.roll(rope, half, 1), 0.0)))
    sin_lo = jnp.where(lo, -pltpu.roll(rope, LANES - half, 1), 0.0)
    sin_hi = jnp.where(hi, rope, 0.0)

    def head_norm_rope(xh, w):
        ss = jnp.sum(xh * xh, axis=-1, keepdims=True) * (1.0 / QK_HEAD)
        xn = xh * lax.rsqrt(ss + EPS) * w
        return (xn * cos_t + pltpu.roll(xn, LANES - half, 1) * sin_lo
                + pltpu.roll(xn, half, 1) * sin_hi)

    for hd in range(N_HEADS):
        sl = slice(hd * HEAD_PAD, (hd + 1) * HEAD_PAD)
        q_ref[0, hd] = head_norm_rope(q[:, sl], qn_ref[...]).astype(BF16)
        k_ref[0, hd] = head_norm_rope(kn[:, sl] + k_pe, kn_ref[...]).astype(BF16)


def _in_proj(x, rope, mixn, win, qan, wuq, kvan, wuk, wuv, qn, kn):
    B, L, D = x.shape
    tm = min(TM_PROJ, L)
    full = lambda shape: pl.BlockSpec(shape, lambda b, t: (0,) * len(shape))
    return pl.pallas_call(
        _in_proj_kernel,
        grid=(B, L // tm),
        in_specs=[
            pl.BlockSpec((1, tm, D), lambda b, t: (b, t, 0)),
            pl.BlockSpec((1, tm, LANES), lambda b, t: (b, t, 0)),
            full((1, D)), full((D, IN_PROJ_PAD)), full((1, Q_LORA)), full((Q_LORA, N_HEADS * HEAD_PAD)),
            full((1, KV_LORA)), full((KV_LORA, N_HEADS * HEAD_PAD)), full((KV_LORA, ATTN_WIDTH)),
            full((1, HEAD_PAD)), full((1, HEAD_PAD)),
        ],
        out_specs=[
            pl.BlockSpec((1, N_HEADS, tm, HEAD_PAD), lambda b, t: (b, 0, t, 0)),
            pl.BlockSpec((1, N_HEADS, tm, HEAD_PAD), lambda b, t: (b, 0, t, 0)),
            pl.BlockSpec((1, tm, ATTN_WIDTH), lambda b, t: (b, t, 0)),
            pl.BlockSpec((tm, SSM_WIDTH), lambda b, t: (t, b)),
            pl.BlockSpec((1, tm, CONV_WIDTH), lambda b, t: (b, t, 0)),
            pl.BlockSpec((1, tm, CONV_WIDTH), lambda b, t: (b, t, 0)),
        ],
        out_shape=[
            jax.ShapeDtypeStruct((B, N_HEADS, L, HEAD_PAD), BF16),
            jax.ShapeDtypeStruct((B, N_HEADS, L, HEAD_PAD), BF16),
            jax.ShapeDtypeStruct((B, L, ATTN_WIDTH), BF16),
            jax.ShapeDtypeStruct((L, B * SSM_WIDTH), BF16),
            jax.ShapeDtypeStruct((B, L, CONV_WIDTH), BF16),
            jax.ShapeDtypeStruct((B, L, CONV_WIDTH), BF16),
        ],
        compiler_params=_params("parallel", "parallel"),
    )(x, rope, mixn, win, qan, wuq, kvan, wuk, wuv, qn, kn)


def _attention_kernel(q_ref, k_ref, v_ref, o_ref, *, seq, tq):
    row_chunk = lax.broadcasted_iota(jnp.int32, (tq, tq), 0) // CHUNK
    col_chunk = lax.broadcasted_iota(jnp.int32, (tq, tq), 1) // CHUNK
    visible = col_chunk <= row_chunk
    neg = jnp.finfo(F32).min
    nt = (((1,), (1,)), ((), ()))
    low_lanes = lax.broadcasted_iota(jnp.int32, (tq, 2 * V_HEAD), 1) < V_HEAD
    for i in range(seq // tq):
        q0 = i * tq
        outs = []
        for hh in range(2):
            qb = q_ref[0, hh, q0:q0 + tq, :]
            s_d = lax.dot_general(qb, k_ref[0, hh, q0:q0 + tq, :], nt, preferred_element_type=F32)
            s_d = jnp.where(visible, s_d, neg)
            m = jnp.max(s_d, axis=-1, keepdims=True)
            if i > 0:
                s_l = lax.dot_general(qb, k_ref[0, hh, 0:q0, :], nt, preferred_element_type=F32)
                m = jnp.maximum(m, jnp.max(s_l, axis=-1, keepdims=True))
            p_d = jnp.exp(s_d - m)
            denom = jnp.sum(p_d, axis=-1, keepdims=True)
            acc = jnp.dot(p_d.astype(BF16), v_ref[0, q0:q0 + tq, :], preferred_element_type=F32)
            if i > 0:
                p_l = jnp.exp(s_l - m)
                denom = denom + jnp.sum(p_l, axis=-1, keepdims=True)
                acc = acc + jnp.dot(p_l.astype(BF16), v_ref[0, 0:q0, :], preferred_element_type=F32)
            outs.append(acc / denom)
        o_ref[0, q0:q0 + tq, :] = jnp.where(low_lanes, outs[0], outs[1]).astype(BF16)


def _attention(q, k, v):
    B, H, L, _ = q.shape
    tq = min(TQ_ATTN, L)
    return pl.pallas_call(
        functools.partial(_attention_kernel, seq=L, tq=tq),
        grid=(B, H // 2),
        in_specs=[
            pl.BlockSpec((1, 2, L, HEAD_PAD), lambda b, p: (b, p, 0, 0)),
            pl.BlockSpec((1, 2, L, HEAD_PAD), lambda b, p: (b, p, 0, 0)),
            pl.BlockSpec((1, L, 2 * V_HEAD), lambda b, p: (b, 0, p)),
        ],
        out_specs=pl.BlockSpec((1, L, 2 * V_HEAD), lambda b, p: (b, 0, p)),
        out_shape=jax.ShapeDtypeStruct((B, L, ATTN_WIDTH), BF16),
        compiler_params=_params("parallel", "parallel"),
    )(q, k, v)


def _ssm_kernel(u_ref, bbd_ref, lam_ref, cbd_ref, d_ref, wglu_ref, g_ref, o_ref, state_ref, bu_ref, *, batch, tc):
    @pl.when(pl.program_id(0) == 0)
    def _():
        state_ref[...] = jnp.zeros_like(state_ref)

    u = u_ref[...]
    bu_ref[...] = jnp.dot(u, bbd_ref[...], preferred_element_type=F32)
    lam_re = jnp.broadcast_to(lam_ref[0:1, :], (batch, SSM_FLAT))
    lam_im = jnp.broadcast_to(lam_ref[1:2, :], (batch, SSM_FLAT))

    def step(t, carry):
        xr, xi = carry
        rows = pl.ds(pl.multiple_of(t * batch, batch), batch)
        nr = lam_re * xr - lam_im * xi + bu_ref[rows, 0:SSM_FLAT]
        ni = lam_re * xi + lam_im * xr + bu_ref[rows, SSM_FLAT:2 * SSM_FLAT]
        bu_ref[rows, 0:SSM_FLAT] = nr
        bu_ref[rows, SSM_FLAT:2 * SSM_FLAT] = ni
        return nr, ni

    xr, xi = lax.fori_loop(0, tc, step, (state_ref[:, 0:SSM_FLAT], state_ref[:, SSM_FLAT:2 * SSM_FLAT]),
                           unroll=2)
    state_ref[:, 0:SSM_FLAT] = xr
    state_ref[:, SSM_FLAT:2 * SSM_FLAT] = xi

    y = jnp.dot(bu_ref[...].astype(BF16), cbd_ref[...], preferred_element_type=F32)
    y = y + d_ref[...] * u.astype(F32)
    z = jax.nn.gelu(y)
    gate = jax.nn.sigmoid(jnp.dot(z.astype(BF16), wglu_ref[...], preferred_element_type=F32))
    o_ref[...] = _rms(z * gate, g_ref[...]).astype(BF16)


def _ssm(u_tm, bbd, lam, cbd, d, wglu, g, batch):
    rows = u_tm.shape[0]
    seq = rows // batch
    tc = min(TC_SSM, seq)
    blk = tc * batch
    full = lambda shape: pl.BlockSpec(shape, lambda t: (0,) * len(shape))
    return pl.pallas_call(
        functools.partial(_ssm_kernel, batch=batch, tc=tc),
        grid=(seq // tc,),
        in_specs=[
            pl.BlockSpec((blk, SSM_WIDTH), lambda t: (t, 0)),
            full((SSM_WIDTH, 2 * SSM_FLAT)), full((2, SSM_FLAT)), full((2 * SSM_FLAT, SSM_WIDTH)),
            full((1, SSM_WIDTH)), full((SSM_WIDTH, SSM_WIDTH)), full((1, SSM_WIDTH)),
        ],
        out_specs=pl.BlockSpec((blk, SSM_WIDTH), lambda t: (t, 0)),
        out_shape=jax.ShapeDtypeStruct((rows, SSM_WIDTH), BF16),
        scratch_shapes=[pltpu.VMEM((batch, 2 * SSM_FLAT), F32), pltpu.VMEM((blk, 2 * SSM_FLAT), F32)],
        compiler_params=_params("arbitrary"),
    )(u_tm, bbd, lam, cbd, d, wglu, g)


def _conv_kernel(a_ref, g_ref, ah_ref, gh_ref, w_ref, b_ref, lnw_ref, lnb_ref, gn_ref, o_ref, upad_ref, *, tt):
    first = pl.program_id(1) == 0
    halo = ah_ref[0].astype(F32) * jax.nn.sigmoid(gh_ref[0].astype(F32))
    upad_ref[0:CONV_HALO, :] = jnp.where(first, 0.0, halo)
    upad_ref[CONV_HALO:, :] = a_ref[0].astype(F32) * jax.nn.sigmoid(g_ref[0].astype(F32))
    base = CONV_HALO - (CONV_K - 1)
    for s in range(tt // CONV_SUB):
        r0 = s * CONV_SUB
        acc = jnp.zeros((CONV_SUB, CONV_WIDTH), F32)
        for kk in range(CONV_K):
            acc = acc + w_ref[kk:kk + 1, :] * upad_ref[r0 + base + kk:r0 + base + kk + CONV_SUB, :]
        y = acc + b_ref[...]
        mu = jnp.mean(y, axis=-1, keepdims=True)
        var = jnp.mean(jnp.square(y - mu), axis=-1, keepdims=True)
        y = (y - mu) * lax.rsqrt(var + 1e-5) * lnw_ref[...] + lnb_ref[...]
        y = y * jax.nn.sigmoid(y)
        o_ref[0, r0:r0 + CONV_SUB, :] = _rms(y, gn_ref[...]).astype(BF16)


def _conv(a, g, w, b, lnw, lnb, gn):
    B, L, C = a.shape
    tt = min(T_CONV, L)
    ratio = tt // CONV_HALO
    full = lambda shape: pl.BlockSpec(shape, lambda bb, t: (0,) * len(shape))
    cur = pl.BlockSpec((1, tt, C), lambda bb, t: (bb, t, 0))
    prev = pl.BlockSpec((1, CONV_HALO, C), lambda bb, t: (bb, jnp.maximum(t * ratio - 1, 0), 0))
    return pl.pallas_call(
        functools.partial(_conv_kernel, tt=tt),
        grid=(B, L // tt),
        in_specs=[cur, cur, prev, prev, full((CONV_K, C)), full((1, C)), full((1, C)), full((1, C)), full((1, C))],
        out_specs=pl.BlockSpec((1, tt, C), lambda bb, t: (bb, t, 0)),
        out_shape=jax.ShapeDtypeStruct((B, L, C), BF16),
        scratch_shapes=[pltpu.VMEM((tt + CONV_HALO, C), F32)],
        compiler_params=_params("parallel", "parallel"),
    )(a, g, a, g, w, b, lnw, lnb, gn)


ROUTE_E1, ROUTE_E2, ROUTE_R1, ROUTE_R2, ROUTE_G1, ROUTE_G2 = range(6)
ROUTE_ROWS = 8
GRP_LANE0 = N_EXPERTS


def _out_router_kernel(x_ref, ya_ref, ys_ref, yc_ref, ga_ref, wo_ref, fn_ref, wr_ref, br_ref,
                       x1_ref, h2_ref, route_ref, route_t_ref, cnt_ref, carry_ref, *, tm):
    @pl.when(pl.program_id(0) == 0)
    def _():
        carry_ref[...] = jnp.zeros_like(carry_ref)

    ya = _rms(ya_ref[...].astype(F32), ga_ref[...]).astype(BF16)
    acc = jnp.dot(ya, wo_ref[0:ATTN_WIDTH, :], preferred_element_type=F32)
    acc += jnp.dot(ys_ref[...], wo_ref[ATTN_WIDTH:ATTN_WIDTH + SSM_WIDTH, :], preferred_element_type=F32)
    acc += jnp.dot(yc_ref[...], wo_ref[ATTN_WIDTH + SSM_WIDTH:, :], preferred_element_type=F32)
    x1 = x_ref[...] + acc
    x1_ref[...] = x1
    h2 = _rms(x1, fn_ref[...])
    h2_ref[...] = h2

    logits = jnp.dot(h2, wr_ref[...], preferred_element_type=F32, precision=lax.Precision.HIGHEST) + br_ref[...]
    lane = lax.broadcasted_iota(jnp.int32, (tm, LANES), 1)
    ninf = -jnp.inf
    big = LANES

    def first_argmax(vals, vmax):
        return jnp.min(jnp.where(vals == vmax, lane, big), axis=-1, keepdims=True)

    grp = jnp.where((lane >= GRP_LANE0) & (lane < GRP_LANE0 + N_EGROUPS), logits, ninf)
    gmax = jnp.max(grp, axis=-1, keepdims=True)
    gsel = first_argmax(grp, gmax) - GRP_LANE0
    p_grp = 1.0 / jnp.sum(jnp.exp(grp - gmax), axis=-1, keepdims=True)

    el = jnp.where((lane < N_EXPERTS) & ((lane // EXP_PER_GROUP) == gsel), logits, ninf)
    m1 = jnp.max(el, axis=-1, keepdims=True)
    e1 = first_argmax(el, m1)
    el2 = jnp.where(lane == e1, ninf, el)
    m2 = jnp.max(el2, axis=-1, keepdims=True)
    e2 = first_argmax(el2, m2)
    t2 = jnp.exp(m2 - m1)
    g1 = p_grp / (1.0 + t2)
    g2 = p_grp * t2 / (1.0 + t2)

    hit1 = lane == e1
    hit2 = lane == e2
    cnt = (hit1 | hit2).astype(F32)
    rr = lax.broadcasted_iota(jnp.int32, (tm, tm), 0)
    cc = lax.broadcasted_iota(jnp.int32, (tm, tm), 1)
    tri = (cc < rr).astype(BF16)
    before = jnp.dot(tri, cnt.astype(BF16), preferred_element_type=F32) + carry_ref[...]
    r1 = jnp.sum(jnp.where(hit1, before, 0.0), axis=-1, keepdims=True)
    r2 = jnp.sum(jnp.where(hit2, before, 0.0), axis=-1, keepdims=True)
    carry_ref[...] += jnp.sum(cnt, axis=0, keepdims=True)
    cnt_ref[...] = carry_ref[...]

    rec = jnp.where(lane == ROUTE_E1, e1.astype(F32), 0.0)
    rec = jnp.where(lane == ROUTE_E2, e2.astype(F32), rec)
    rec = jnp.where(lane == ROUTE_R1, r1, rec)
    rec = jnp.where(lane == ROUTE_R2, r2, rec)
    rec = jnp.where(lane == ROUTE_G1, g1, rec)
    rec = jnp.where(lane == ROUTE_G2, g2, rec)
    route_ref[...] = rec
    route_t_ref[...] = rec.T[0:ROUTE_ROWS, :]


def _out_router(x, ya, ys_tm, yc, ga, wo, fn, wr, br, batch):
    N, D = x.shape
    seq = N // batch
    tm = min(TM_OUT, seq)
    per_b = seq // tm
    full = lambda shape: pl.BlockSpec(shape, lambda i: (0,) * len(shape))
    tile = lambda w: pl.BlockSpec((tm, w), lambda i: (i, 0))
    return pl.pallas_call(
        functools.partial(_out_router_kernel, tm=tm),
        grid=(N // tm,),
        in_specs=[
            tile(D), tile(ATTN_WIDTH),
            pl.BlockSpec((tm, SSM_WIDTH), lambda i: (i % per_b, i // per_b)),
            tile(CONV_WIDTH),
            full((1, ATTN_WIDTH)), full((D, D)), full((1, D)), full((D, LANES)), full((1, LANES)),
        ],
        out_specs=[tile(D), tile(D), tile(LANES), pl.BlockSpec((ROUTE_ROWS, tm), lambda i: (0, i)), full((1, LANES))],
        out_shape=[
            jax.ShapeDtypeStruct((N, D), F32),
            jax.ShapeDtypeStruct((N, D), F32),
            jax.ShapeDtypeStruct((N, LANES), F32),
            jax.ShapeDtypeStruct((ROUTE_ROWS, N), F32),
            jax.ShapeDtypeStruct((1, LANES), F32),
        ],
        scratch_shapes=[pltpu.VMEM((1, LANES), F32)],
        compiler_params=_params("arbitrary"),
    )(x, ya, ys_tm, yc, ga, wo, fn, wr, br)


def _dispatch_kernel(dest_ref, h_ref, rows_ref, sem, *, tm, n_tok):
    base = pl.program_id(0) * tm

    def issue(r, c):
        src = h_ref.at[pl.ds(r, 1)]
        pltpu.make_async_copy(src, rows_ref.at[pl.ds(dest_ref[base + r], 1)], sem).start()
        pltpu.make_async_copy(src, rows_ref.at[pl.ds(dest_ref[n_tok + base + r], 1)], sem).start()
        return c

    lax.fori_loop(0, tm, issue, 0)
    for _ in range(2):
        pltpu.make_async_copy(h_ref, rows_ref.at[pl.ds(0, tm)], sem).wait()


def _dispatch(dest_flat, h2, n_rows):
    N, D = h2.shape
    tm = min(TM_OUT, N)
    return pl.pallas_call(
        functools.partial(_dispatch_kernel, tm=tm, n_tok=N),
        grid_spec=pltpu.PrefetchScalarGridSpec(
            num_scalar_prefetch=1,
            grid=(N // tm,),
            in_specs=[pl.BlockSpec((tm, D), lambda i, d: (i, 0))],
            out_specs=pl.BlockSpec(memory_space=pl.ANY),
            scratch_shapes=[pltpu.SemaphoreType.DMA(())],
        ),
        out_shape=jax.ShapeDtypeStruct((n_rows, D), F32),
        compiler_params=_params("arbitrary"),
    )(dest_flat, h2)


def _experts_kernel(blk_ref, exp_ref, used_ref, x_ref, w1_ref, w3_ref, w2_ref, y_ref, w13_s, w2_s):
    j = pl.program_id(0)

    @pl.when(j < used_ref[0])
    def _():
        prev = exp_ref[jnp.maximum(j - 1, 0)]

        @pl.when((j == 0) | (exp_ref[j] != prev))
        def _():
            w13_s[:, 0:D_FF_E] = w1_ref[0, 0].astype(BF16)
            w13_s[:, D_FF_E:] = w3_ref[0, 0].astype(BF16)
            w2_s[...] = w2_ref[0, 0].astype(BF16)

        x = x_ref[...].astype(BF16)
        ab = jnp.dot(x, w13_s[...], preferred_element_type=F32)
        a = ab[:, 0:D_FF_E]
        hmid = (a * jax.nn.sigmoid(a) * ab[:, D_FF_E:]).astype(BF16)
        y_ref[...] = jnp.dot(hmid, w2_s[...], preferred_element_type=F32)


def _experts(blk_map, blk_exp, n_used, x_rows, w1, w3, w2, layer):
    R, D = x_rows.shape
    nb = R // TB_EXP
    return pl.pallas_call(
        _experts_kernel,
        grid_spec=pltpu.PrefetchScalarGridSpec(
            num_scalar_prefetch=3,
            grid=(nb,),
            in_specs=[
                pl.BlockSpec((TB_EXP, D), lambda j, bm, be, nu: (bm[j], 0)),
                pl.BlockSpec((1, 1, D, D_FF_E), lambda j, bm, be, nu: (layer, be[j], 0, 0)),
                pl.BlockSpec((1, 1, D, D_FF_E), lambda j, bm, be, nu: (layer, be[j], 0, 0)),
                pl.BlockSpec((1, 1, D_FF_E, D), lambda j, bm, be, nu: (layer, be[j], 0, 0)),
            ],
            out_specs=pl.BlockSpec((TB_EXP, D), lambda j, bm, be, nu: (bm[j], 0)),
            scratch_shapes=[pltpu.VMEM((D, 2 * D_FF_E), BF16), pltpu.VMEM((D_FF_E, D), BF16)],
        ),
        out_shape=jax.ShapeDtypeStruct((R, D), F32),
        compiler_params=_params("arbitrary"),
    )(blk_map, blk_exp, n_used, x_rows, w1, w3, w2)


def _combine_kernel(dest_ref, x1_ref, route_ref, rows_ref, o_ref, buf, sem, *, tm, n_tok):
    base = pl.program_id(0) * tm

    def issue(r, c):
        pltpu.make_async_copy(rows_ref.at[pl.ds(dest_ref[base + r], 1)], buf.at[0, pl.ds(r, 1)], sem).start()
        pltpu.make_async_copy(rows_ref.at[pl.ds(dest_ref[n_tok + base + r], 1)], buf.at[1, pl.ds(r, 1)], sem).start()
        return c

    lax.fori_loop(0, tm, issue, 0)
    for k in range(2):
        pltpu.make_async_copy(rows_ref.at[pl.ds(0, tm)], buf.at[k], sem).wait()
    route = route_ref[...]
    g1 = route[:, ROUTE_G1:ROUTE_G1 + 1]
    g2 = route[:, ROUTE_G2:ROUTE_G2 + 1]
    o_ref[...] = x1_ref[...] + g1 * buf[0] + g2 * buf[1]


def _combine(dest_flat, x1, route, y_rows):
    N, D = x1.shape
    tm = min(TM_OUT, N)
    return pl.pallas_call(
        functools.partial(_combine_kernel, tm=tm, n_tok=N),
        grid_spec=pltpu.PrefetchScalarGridSpec(
            num_scalar_prefetch=1,
            grid=(N // tm,),
            in_specs=[
                pl.BlockSpec((tm, D), lambda i, d: (i, 0)),
                pl.BlockSpec((tm, LANES), lambda i, d: (i, 0)),
                pl.BlockSpec(memory_space=pl.ANY),
            ],
            out_specs=pl.BlockSpec((tm, D), lambda i, d: (i, 0)),
            scratch_shapes=[pltpu.VMEM((2, tm, D), F32), pltpu.SemaphoreType.DMA(())],
        ),
        out_shape=jax.ShapeDtypeStruct((N, D), F32),
        compiler_params=_params("arbitrary"),
    )(dest_flat, x1, route, y_rows)


def _moe(x1, h2, route, route_t, counts, w1, w3, w2, layer):
    N, _ = x1.shape
    nb = (2 * N + N_EXPERTS * (TB_EXP - 1)) // TB_EXP + 1
    e_id = route_t[ROUTE_E1:ROUTE_E2 + 1].astype(jnp.int32)
    rank = route_t[ROUTE_R1:ROUTE_R2 + 1].astype(jnp.int32)
    cnt = counts[0, :N_EXPERTS].astype(jnp.int32)
    padded = (cnt + TB_EXP - 1) // TB_EXP * TB_EXP
    pad_end = jnp.cumsum(padded)
    pad_start = pad_end - padded
    dest = (pad_start[e_id] + rank).reshape(2 * N)
    n_used = pad_end[-1] // TB_EXP
    blk_map = jnp.minimum(jnp.arange(nb, dtype=jnp.int32), n_used - 1)
    blk_exp = jnp.sum(pad_end[None, :] <= (blk_map * TB_EXP)[:, None], axis=1).astype(jnp.int32)
    blk_exp = jnp.minimum(blk_exp, N_EXPERTS - 1)
    x_rows = _dispatch(dest, h2, nb * TB_EXP)
    y_rows = _experts(blk_map, blk_exp, n_used.reshape(1).astype(jnp.int32), x_rows, w1, w3, w2, layer)
    return _combine(dest, x1, route, y_rows)


def _pad_heads(w, width):
    k = w.shape[0]
    w = w.reshape(k, N_HEADS, width)
    return jnp.pad(w, ((0, 0), (0, 0), (0, HEAD_PAD - width))).reshape(k, N_HEADS * HEAD_PAD)


def _rope_tables(positions):
    inv_freq = ROPE_THETA ** (-jnp.arange(0, QK_ROPE, 2, dtype=F32) / QK_ROPE)
    ang = positions.astype(F32)[..., None] * inv_freq
    table = jnp.concatenate([jnp.cos(ang), jnp.sin(ang)], -1)
    return jnp.pad(table, ((0, 0), (0, 0), (QK_NOPE, LANES - QK_HEAD)))


def _ssm_params(lam_re, lam_im, b_re, b_im, c_re, c_im, log_dt):
    lam = lax.complex(lam_re, lam_im)
    dt = jnp.exp(log_dt)[:, None]
    lam_bar = jnp.exp(lam * dt)
    b_bar = ((lam_bar - 1.0) / lam)[..., None] * lax.complex(b_re, b_im)
    eye = jnp.eye(SSM_GROUPS, dtype=F32)

    def in_blockdiag(m):
        return jnp.einsum("gpc,gh->gchp", m, eye).reshape(SSM_WIDTH, SSM_FLAT)

    def out_blockdiag(m):
        return jnp.einsum("gcp,gh->gphc", m, eye).reshape(SSM_FLAT, SSM_WIDTH)

    bbd = jnp.concatenate([in_blockdiag(jnp.real(b_bar)), in_blockdiag(jnp.imag(b_bar))], axis=1)
    cbd = jnp.concatenate([out_blockdiag(c_re), out_blockdiag(-c_im)], axis=0)
    lam_rows = jnp.stack([jnp.real(lam_bar).reshape(SSM_FLAT), jnp.imag(lam_bar).reshape(SSM_FLAT)])
    return bbd.astype(BF16), lam_rows, cbd.astype(BF16)


def kernel(x, positions, mix_norm, w_in, q_a_norm, w_uq, kv_a_norm, w_ukv, q_norm, k_norm, ssm_lam_re, ssm_lam_im, ssm_b_re, ssm_b_im, ssm_c_re, ssm_c_im, ssm_d, ssm_log_dt, ssm_w_glu, conv_dw_w, conv_dw_b, conv_ln_w, conv_ln_b, out_norm, w_out, ffn_norm, w_grp, b_grp, w_exp, b_exp, w1, w3, w2):
    B, L, D = x.shape
    depth = w_in.shape[0]
    rope = _rope_tables(positions)
    row = lambda v: v.reshape(1, -1)
    lane_pad = lambda v: jnp.pad(v, (0, LANES - v.shape[0])).reshape(1, LANES)
    for l in range(depth):
        c_q, c_kv, k_pe, u_s, c_a, c_g = jnp.split(
            w_in[l], [Q_LORA, Q_LORA + KV_LORA, Q_LORA + KV_LORA + QK_ROPE,
                      Q_LORA + KV_LORA + QK_ROPE + SSM_WIDTH,
                      Q_LORA + KV_LORA + QK_ROPE + SSM_WIDTH + CONV_WIDTH], axis=1)
        k_pe = jnp.pad(k_pe, ((0, 0), (QK_NOPE, LANES - QK_HEAD)))
        win = jnp.concatenate([c_q, c_kv, u_s, c_a, c_g, k_pe], axis=1).astype(BF16)
        wkv = w_ukv[l].reshape(KV_LORA, N_HEADS, QK_NOPE + V_HEAD)
        wuk = _pad_heads(wkv[:, :, :QK_NOPE].reshape(KV_LORA, N_HEADS * QK_NOPE), QK_NOPE).astype(BF16)
        wuv = wkv[:, :, QK_NOPE:].reshape(KV_LORA, ATTN_WIDTH).astype(BF16)
        wuq = _pad_heads(w_uq[l], QK_HEAD).astype(BF16)
        q, k, v, u_tm, ca, cg = _in_proj(
            x, rope, row(mix_norm[l]), win, row(q_a_norm[l]), wuq, row(kv_a_norm[l]), wuk, wuv,
            lane_pad(q_norm[l] * (QK_HEAD ** -0.5)), lane_pad(k_norm[l]))
        y_attn = _attention(q, k, v)

        g_out = out_norm[l]
        bbd, lam_rows, cbd = _ssm_params(ssm_lam_re[l], ssm_lam_im[l], ssm_b_re[l], ssm_b_im[l],
                                         ssm_c_re[l], ssm_c_im[l], ssm_log_dt[l])
        y_ssm_tm = _ssm(u_tm.reshape(L * B, SSM_WIDTH), bbd, lam_rows, cbd, row(ssm_d[l]),
                        ssm_w_glu[l].astype(BF16), row(g_out[ATTN_WIDTH:ATTN_WIDTH + SSM_WIDTH]), B)
        y_conv = _conv(ca, cg, conv_dw_w[l], row(conv_dw_b[l]), row(conv_ln_w[l]), row(conv_ln_b[l]),
                       row(g_out[ATTN_WIDTH + SSM_WIDTH:]))

        w_route = jnp.pad(jnp.concatenate([w_exp[l], w_grp[l]], axis=1), ((0, 0), (0, LANES - N_EXPERTS - N_EGROUPS)))
        b_route = lane_pad(jnp.concatenate([b_exp[l], b_grp[l]]))
        x1, h2, route, route_t, counts = _out_router(
            x.reshape(B * L, D), y_attn.reshape(B * L, ATTN_WIDTH), y_ssm_tm.reshape(L, B * SSM_WIDTH),
            y_conv.reshape(B * L, CONV_WIDTH), row(g_out[:ATTN_WIDTH]), w_out[l].astype(BF16),
            row(ffn_norm[l]), w_route, b_route, B)
        x = _moe(x1, h2, route, route_t, counts, w1, w3, w2, l).reshape(B, L, D)
    return x
```

```python
import functools
import math

import jax
import jax.numpy as jnp
from jax import lax
from jax.experimental import pallas as pl
from jax.experimental.pallas import tpu as pltpu

D_MODEL = 1024
CHUNK = 64
EPS = 1e-6
N_HEADS = 8
QK_NOPE = 64
QK_ROPE = 32
QK_HEAD = QK_NOPE + QK_ROPE
V_HEAD = 64
Q_LORA = 256
KV_LORA = 128
ROPE_THETA = 10000.0
ATTN_WIDTH = N_HEADS * V_HEAD
SSM_WIDTH = 256
SSM_GROUP = 16
SSM_GROUPS = SSM_WIDTH // SSM_GROUP
SSM_STATE = 64
SSM_FLAT = SSM_GROUPS * SSM_STATE
CONV_WIDTH = 256
CONV_K = 31
N_EGROUPS = 4
EXP_PER_GROUP = 8
N_EXPERTS = N_EGROUPS * EXP_PER_GROUP
D_FF_E = 512

LANES = 128
HEAD_PAD = LANES
IN_PROJ_PAD = Q_LORA + KV_LORA + SSM_WIDTH + 2 * CONV_WIDTH + 2 * LANES
CONV_HALO = 32
VMEM_LIMIT = 48 * 1024 * 1024

TM_PROJ = 512
TQ_ATTN = 256
TC_SSM = 64
T_CONV = 256
CONV_SUB = 64
TM_OUT = 512
TM_MOVE = 256
DMA_UNROLL = 8
TB_EXP = 256
FF_CHUNK = 256

BF16 = jnp.bfloat16
F32 = jnp.float32


def _rms(x, w):
    return x * lax.rsqrt(jnp.mean(x * x, axis=-1, keepdims=True) + EPS) * w


def _params(*sem):
    return pltpu.CompilerParams(dimension_semantics=sem, vmem_limit_bytes=VMEM_LIMIT)


def _in_proj_kernel(x_ref, rope_ref, mixn_ref, win_ref, qan_ref, wuq_ref, kvan_ref, wuk_ref, wuv_ref,
                    qn_ref, kn_ref, q_ref, k_ref, v_ref, u_ref, ca_ref, cg_ref):
    x = x_ref[0]
    h = _rms(x, mixn_ref[...]).astype(BF16)
    proj = jnp.dot(h, win_ref[...], preferred_element_type=F32)
    o = 0
    c_q = proj[:, o:o + Q_LORA]; o += Q_LORA
    c_kv = proj[:, o:o + KV_LORA]; o += KV_LORA
    u_ref[...] = proj[:, o:o + SSM_WIDTH].astype(BF16); o += SSM_WIDTH
    ca_ref[0] = proj[:, o:o + CONV_WIDTH].astype(BF16); o += CONV_WIDTH
    cg_ref[0] = proj[:, o:o + CONV_WIDTH].astype(BF16); o += CONV_WIDTH
    k_pe = proj[:, o:o + LANES]; o += LANES
    k_pe_sw = proj[:, o:o + LANES]

    width = N_HEADS * HEAD_PAD
    q2 = jnp.dot(_rms(c_q, qan_ref[...]).astype(BF16), wuq_ref[...], preferred_element_type=F32)
    q, q_sw = q2[:, 0:width], q2[:, width:2 * width]
    ckv_n = _rms(c_kv, kvan_ref[...]).astype(BF16)
    kn = jnp.dot(ckv_n, wuk_ref[...], preferred_element_type=F32)
    v_ref[0] = jnp.dot(ckv_n, wuv_ref[...], preferred_element_type=F32).astype(BF16)

    rope = rope_ref[0]
    half = QK_ROPE // 2
    lane = lax.broadcasted_iota(jnp.int32, rope.shape, 1)
    lo = (lane >= QK_NOPE) & (lane < QK_NOPE + half)
    hi = (lane >= QK_NOPE + half) & (lane < QK_HEAD)
    cos_t = jnp.where(lane < QK_NOPE, 1.0, jnp.where(lo, rope, jnp.where(hi, pltpu.roll(rope, half, 1), 0.0)))
    sin_t = jnp.where(lo, -pltpu.roll(rope, LANES - half, 1), jnp.where(hi, rope, 0.0))
    a_q, b_q = qn_ref[0:1, :] * cos_t, qn_ref[1:2, :] * sin_t
    a_k = kn_ref[0:1, :] * cos_t
    k_sw_term = k_pe_sw * (kn_ref[1:2, :] * sin_t)

    def inv_rms(y):
        return lax.rsqrt(jnp.sum(y * y, axis=-1, keepdims=True) * (1.0 / QK_HEAD) + EPS)

    for hd in range(N_HEADS):
        sl = slice(hd * HEAD_PAD, (hd + 1) * HEAD_PAD)
        yq = q[:, sl]
        q_ref[0, hd] = ((yq * a_q + q_sw[:, sl] * b_q) * inv_rms(yq)).astype(BF16)
        yk = kn[:, sl] + k_pe
        k_ref[0, hd] = ((yk * a_k + k_sw_term) * inv_rms(yk)).astype(BF16)


def _in_proj(x, rope, mixn, win, qan, wuq, kvan, wuk, wuv, qn, kn):
    B, L, D = x.shape
    tm = min(TM_PROJ, L)
    full = lambda shape: pl.BlockSpec(shape, lambda b, t: (0,) * len(shape))
    return pl.pallas_call(
        _in_proj_kernel,
        grid=(B, L // tm),
        in_specs=[
            pl.BlockSpec((1, tm, D), lambda b, t: (b, t, 0)),
            pl.BlockSpec((1, tm, LANES), lambda b, t: (b, t, 0)),
            full((1, D)), full((D, IN_PROJ_PAD)), full((1, Q_LORA)), full((Q_LORA, 2 * N_HEADS * HEAD_PAD)),
            full((1, KV_LORA)), full((KV_LORA, N_HEADS * HEAD_PAD)), full((KV_LORA, ATTN_WIDTH)),
            full((2, HEAD_PAD)), full((2, HEAD_PAD)),
        ],
        out_specs=[
            pl.BlockSpec((1, N_HEADS, tm, HEAD_PAD), lambda b, t: (b, 0, t, 0)),
            pl.BlockSpec((1, N_HEADS, tm, HEAD_PAD), lambda b, t: (b, 0, t, 0)),
            pl.BlockSpec((1, tm, ATTN_WIDTH), lambda b, t: (b, t, 0)),
            pl.BlockSpec((tm, SSM_WIDTH), lambda b, t: (t, b)),
            pl.BlockSpec((1, tm, CONV_WIDTH), lambda b, t: (b, t, 0)),
            pl.BlockSpec((1, tm, CONV_WIDTH), lambda b, t: (b, t, 0)),
        ],
        out_shape=[
            jax.ShapeDtypeStruct((B, N_HEADS, L, HEAD_PAD), BF16),
            jax.ShapeDtypeStruct((B, N_HEADS, L, HEAD_PAD), BF16),
            jax.ShapeDtypeStruct((B, L, ATTN_WIDTH), BF16),
            jax.ShapeDtypeStruct((L, B * SSM_WIDTH), BF16),
            jax.ShapeDtypeStruct((B, L, CONV_WIDTH), BF16),
            jax.ShapeDtypeStruct((B, L, CONV_WIDTH), BF16),
        ],
        compiler_params=_params("parallel", "parallel"),
    )(x, rope, mixn, win, qan, wuq, kvan, wuk, wuv, qn, kn)


def _attention_kernel(q_ref, k_ref, v_ref, o_ref, vext_ref, *, seq, tq):
    pair = 2 * V_HEAD
    vext_ref[:, 0:pair] = v_ref[0]
    vext_ref[:, pair:] = jnp.ones((seq, LANES), BF16)
    row_chunk = lax.broadcasted_iota(jnp.int32, (tq, tq), 0) // CHUNK
    col_chunk = lax.broadcasted_iota(jnp.int32, (tq, tq), 1) // CHUNK
    visible = col_chunk <= row_chunk
    neg = jnp.finfo(F32).min
    nt = (((1,), (1,)), ((), ()))
    low_lanes = lax.broadcasted_iota(jnp.int32, (tq, pair), 1) < V_HEAD
    for i in range(seq // tq):
        q0 = i * tq
        outs = []
        for hh in range(2):
            qb = q_ref[0, hh, q0:q0 + tq, :]
            s_d = lax.dot_general(qb, k_ref[0, hh, q0:q0 + tq, :], nt, preferred_element_type=F32)
            s_d = jnp.where(visible, s_d, neg)
            m = jnp.max(s_d, axis=-1, keepdims=True)
            if i > 0:
                s_l = lax.dot_general(qb, k_ref[0, hh, 0:q0, :], nt, preferred_element_type=F32)
                m = jnp.maximum(m, jnp.max(s_l, axis=-1, keepdims=True))
            acc = jnp.dot(jnp.exp(s_d - m).astype(BF16), vext_ref[q0:q0 + tq, :], preferred_element_type=F32)
            if i > 0:
                acc = acc + jnp.dot(jnp.exp(s_l - m).astype(BF16), vext_ref[0:q0, :], preferred_element_type=F32)
            outs.append(acc[:, 0:pair] / acc[:, pair:])
        o_ref[0, q0:q0 + tq, :] = jnp.where(low_lanes, outs[0], outs[1]).astype(BF16)


def _attention(q, k, v):
    B, H, L, _ = q.shape
    tq = min(TQ_ATTN, L)
    return pl.pallas_call(
        functools.partial(_attention_kernel, seq=L, tq=tq),
        grid=(B, H // 2),
        in_specs=[
            pl.BlockSpec((1, 2, L, HEAD_PAD), lambda b, p: (b, p, 0, 0)),
            pl.BlockSpec((1, 2, L, HEAD_PAD), lambda b, p: (b, p, 0, 0)),
            pl.BlockSpec((1, L, 2 * V_HEAD), lambda b, p: (b, 0, p)),
        ],
        out_specs=pl.BlockSpec((1, L, 2 * V_HEAD), lambda b, p: (b, 0, p)),
        out_shape=jax.ShapeDtypeStruct((B, L, ATTN_WIDTH), BF16),
        scratch_shapes=[pltpu.VMEM((L, 2 * V_HEAD + LANES), BF16)],
        compiler_params=_params("parallel", "parallel"),
    )(q, k, v)


def _ssm_kernel(u_ref, bbd_ref, lam_ref, cbd_ref, d_ref, wglu_ref, g_ref, o_ref, state_ref, bu_ref, *, batch, tc):
    @pl.when(pl.program_id(0) == 0)
    def _():
        state_ref[...] = jnp.zeros_like(state_ref)

    u = u_ref[...]
    bu_ref[...] = jnp.dot(u, bbd_ref[...], preferred_element_type=F32)
    lam_re = jnp.broadcast_to(lam_ref[0:1, :], (batch, SSM_FLAT))
    lam_im = jnp.broadcast_to(lam_ref[1:2, :], (batch, SSM_FLAT))

    def step(t, carry):
        xr, xi = carry
        rows = pl.ds(pl.multiple_of(t * batch, batch), batch)
        nr = lam_re * xr - lam_im * xi + bu_ref[rows, 0:SSM_FLAT]
        ni = lam_re * xi + lam_im * xr + bu_ref[rows, SSM_FLAT:2 * SSM_FLAT]
        bu_ref[rows, 0:SSM_FLAT] = nr
        bu_ref[rows, SSM_FLAT:2 * SSM_FLAT] = ni
        return nr, ni

    xr, xi = lax.fori_loop(0, tc, step, (state_ref[:, 0:SSM_FLAT], state_ref[:, SSM_FLAT:2 * SSM_FLAT]),
                           unroll=2)
    state_ref[:, 0:SSM_FLAT] = xr
    state_ref[:, SSM_FLAT:2 * SSM_FLAT] = xi

    y = jnp.dot(bu_ref[...].astype(BF16), cbd_ref[...], preferred_element_type=F32)
    y = y + d_ref[...] * u.astype(F32)
    z = jax.nn.gelu(y)
    gate = jax.nn.sigmoid(jnp.dot(z.astype(BF16), wglu_ref[...], preferred_element_type=F32))
    o_ref[...] = _rms(z * gate, g_ref[...]).astype(BF16)


def _ssm(u_tm, bbd, lam, cbd, d, wglu, g, batch):
    rows = u_tm.shape[0]
    seq = rows // batch
    tc = min(TC_SSM, seq)
    blk = tc * batch
    full = lambda shape: pl.BlockSpec(shape, lambda t: (0,) * len(shape))
    return pl.pallas_call(
        functools.partial(_ssm_kernel, batch=batch, tc=tc),
        grid=(seq // tc,),
        in_specs=[
            pl.BlockSpec((blk, SSM_WIDTH), lambda t: (t, 0)),
            full((SSM_WIDTH, 2 * SSM_FLAT)), full((2, SSM_FLAT)), full((2 * SSM_FLAT, SSM_WIDTH)),
            full((1, SSM_WIDTH)), full((SSM_WIDTH, SSM_WIDTH)), full((1, SSM_WIDTH)),
        ],
        out_specs=pl.BlockSpec((blk, SSM_WIDTH), lambda t: (t, 0)),
        out_shape=jax.ShapeDtypeStruct((rows, SSM_WIDTH), BF16),
        scratch_shapes=[pltpu.VMEM((batch, 2 * SSM_FLAT), F32), pltpu.VMEM((blk, 2 * SSM_FLAT), F32)],
        compiler_params=_params("arbitrary"),
    )(u_tm, bbd, lam, cbd, d, wglu, g)


def _conv_kernel(a_ref, g_ref, ah_ref, gh_ref, w_ref, b_ref, lnw_ref, lnb_ref, gn_ref, o_ref, upad_ref, *, tt):
    first = pl.program_id(1) == 0
    halo = ah_ref[0].astype(F32) * jax.nn.sigmoid(gh_ref[0].astype(F32))
    upad_ref[0:CONV_HALO, :] = jnp.where(first, 0.0, halo)
    upad_ref[CONV_HALO:, :] = a_ref[0].astype(F32) * jax.nn.sigmoid(g_ref[0].astype(F32))
    base = CONV_HALO - (CONV_K - 1)
    for s in range(tt // CONV_SUB):
        r0 = s * CONV_SUB
        acc = jnp.zeros((CONV_SUB, CONV_WIDTH), F32)
        for kk in range(CONV_K):
            acc = acc + w_ref[kk:kk + 1, :] * upad_ref[r0 + base + kk:r0 + base + kk + CONV_SUB, :]
        y = acc + b_ref[...]
        mu = jnp.mean(y, axis=-1, keepdims=True)
        var = jnp.mean(jnp.square(y - mu), axis=-1, keepdims=True)
        y = (y - mu) * lax.rsqrt(var + 1e-5) * lnw_ref[...] + lnb_ref[...]
        y = y * jax.nn.sigmoid(y)
        o_ref[0, r0:r0 + CONV_SUB, :] = _rms(y, gn_ref[...]).astype(BF16)


def _conv(a, g, w, b, lnw, lnb, gn):
    B, L, C = a.shape
    tt = min(T_CONV, L)
    ratio = tt // CONV_HALO
    full = lambda shape: pl.BlockSpec(shape, lambda bb, t: (0,) * len(shape))
    cur = pl.BlockSpec((1, tt, C), lambda bb, t: (bb, t, 0))
    prev = pl.BlockSpec((1, CONV_HALO, C), lambda bb, t: (bb, jnp.maximum(t * ratio - 1, 0), 0))
    return pl.pallas_call(
        functools.partial(_conv_kernel, tt=tt),
        grid=(B, L // tt),
        in_specs=[cur, cur, prev, prev, full((CONV_K, C)), full((1, C)), full((1, C)), full((1, C)), full((1, C))],
        out_specs=pl.BlockSpec((1, tt, C), lambda bb, t: (bb, t, 0)),
        out_shape=jax.ShapeDtypeStruct((B, L, C), BF16),
        scratch_shapes=[pltpu.VMEM((tt + CONV_HALO, C), F32)],
        compiler_params=_params("parallel", "parallel"),
    )(a, g, a, g, w, b, lnw, lnb, gn)


ROUTE_E1, ROUTE_E2, ROUTE_R1, ROUTE_R2, ROUTE_G1, ROUTE_G2 = range(6)
ROUTE_ROWS = 8
GRP_LANE0 = N_EXPERTS


def _out_router_kernel(x_ref, ya_ref, ys_ref, yc_ref, ga_ref, wo_ref, fn_ref, wr_ref, br_ref,
                       x1_ref, h2_ref, route_ref, route_t_ref, cnt_ref, carry_ref, *, tm):
    @pl.when(pl.program_id(0) == 0)
    def _():
        carry_ref[...] = jnp.zeros_like(carry_ref)

    ya = _rms(ya_ref[...].astype(F32), ga_ref[...]).astype(BF16)
    acc = jnp.dot(ya, wo_ref[0:ATTN_WIDTH, :], preferred_element_type=F32)
    acc += jnp.dot(ys_ref[...], wo_ref[ATTN_WIDTH:ATTN_WIDTH + SSM_WIDTH, :], preferred_element_type=F32)
    acc += jnp.dot(yc_ref[...], wo_ref[ATTN_WIDTH + SSM_WIDTH:, :], preferred_element_type=F32)
    x1 = x_ref[...] + acc
    x1_ref[...] = x1
    h2 = _rms(x1, fn_ref[...])
    h2_ref[...] = h2

    h_hi = h2.astype(BF16)
    h_lo = (h2 - h_hi.astype(F32)).astype(BF16)
    part = jnp.dot(h_hi, wr_ref[...], preferred_element_type=F32)
    logits = (part[:, 0:LANES] + part[:, LANES:2 * LANES]
              + jnp.dot(h_lo, wr_ref[:, 0:LANES], preferred_element_type=F32) + br_ref[...])
    lane = lax.broadcasted_iota(jnp.int32, (tm, LANES), 1)
    ninf = -jnp.inf
    big = LANES

    def first_argmax(vals, vmax):
        return jnp.min(jnp.where(vals == vmax, lane, big), axis=-1, keepdims=True)

    grp = jnp.where((lane >= GRP_LANE0) & (lane < GRP_LANE0 + N_EGROUPS), logits, ninf)
    gmax = jnp.max(grp, axis=-1, keepdims=True)
    gsel = first_argmax(grp, gmax) - GRP_LANE0
    p_grp = 1.0 / jnp.sum(jnp.exp(grp - gmax), axis=-1, keepdims=True)

    el = jnp.where((lane < N_EXPERTS) & ((lane // EXP_PER_GROUP) == gsel), logits, ninf)
    m1 = jnp.max(el, axis=-1, keepdims=True)
    e1 = first_argmax(el, m1)
    el2 = jnp.where(lane == e1, ninf, el)
    m2 = jnp.max(el2, axis=-1, keepdims=True)
    e2 = first_argmax(el2, m2)
    t2 = jnp.exp(m2 - m1)
    g1 = p_grp / (1.0 + t2)
    g2 = p_grp * t2 / (1.0 + t2)

    hit1 = lane == e1
    hit2 = lane == e2
    cnt = (hit1 | hit2).astype(F32)
    rr = lax.broadcasted_iota(jnp.int32, (tm, tm), 0)
    cc = lax.broadcasted_iota(jnp.int32, (tm, tm), 1)
    tri = (cc < rr).astype(BF16)
    before = jnp.dot(tri, cnt.astype(BF16), preferred_element_type=F32) + carry_ref[...]
    r1 = jnp.sum(jnp.where(hit1, before, 0.0), axis=-1, keepdims=True)
    r2 = jnp.sum(jnp.where(hit2, before, 0.0), axis=-1, keepdims=True)
    carry_ref[...] += jnp.sum(cnt, axis=0, keepdims=True)
    cnt_ref[...] = carry_ref[...]

    rec = jnp.where(lane == ROUTE_E1, e1.astype(F32), 0.0)
    rec = jnp.where(lane == ROUTE_E2, e2.astype(F32), rec)
    rec = jnp.where(lane == ROUTE_R1, r1, rec)
    rec = jnp.where(lane == ROUTE_R2, r2, rec)
    rec = jnp.where(lane == ROUTE_G1, g1, rec)
    rec = jnp.where(lane == ROUTE_G2, g2, rec)
    route_ref[...] = rec
    route_t_ref[...] = rec.T[0:ROUTE_ROWS, :]


def _out_router(x, ya, ys_tm, yc, ga, wo, fn, wr, br, batch):
    N, D = x.shape
    seq = N // batch
    tm = min(TM_OUT, seq)
    per_b = seq // tm
    full = lambda shape: pl.BlockSpec(shape, lambda i: (0,) * len(shape))
    tile = lambda w: pl.BlockSpec((tm, w), lambda i: (i, 0))
    return pl.pallas_call(
        functools.partial(_out_router_kernel, tm=tm),
        grid=(N // tm,),
        in_specs=[
            tile(D), tile(ATTN_WIDTH),
            pl.BlockSpec((tm, SSM_WIDTH), lambda i: (i % per_b, i // per_b)),
            tile(CONV_WIDTH),
            full((1, ATTN_WIDTH)), full((D, D)), full((1, D)), full((D, 2 * LANES)), full((1, LANES)),
        ],
        out_specs=[tile(D), tile(D), tile(LANES), pl.BlockSpec((ROUTE_ROWS, tm), lambda i: (0, i)), full((1, LANES))],
        out_shape=[
            jax.ShapeDtypeStruct((N, D), F32),
            jax.ShapeDtypeStruct((N, D), F32),
            jax.ShapeDtypeStruct((N, LANES), F32),
            jax.ShapeDtypeStruct((ROUTE_ROWS, N), F32),
            jax.ShapeDtypeStruct((1, LANES), F32),
        ],
        scratch_shapes=[pltpu.VMEM((1, LANES), F32)],
        compiler_params=_params("arbitrary"),
    )(x, ya, ys_tm, yc, ga, wo, fn, wr, br)


def _dispatch_kernel(dest_ref, h_ref, rows_ref, sem, *, tm, n_tok):
    base = pl.program_id(0) * tm

    def issue(r, c):
        src = h_ref.at[pl.ds(r, 1)]
        pltpu.make_async_copy(src, rows_ref.at[pl.ds(dest_ref[base + r], 1)], sem).start()
        pltpu.make_async_copy(src, rows_ref.at[pl.ds(dest_ref[n_tok + base + r], 1)], sem).start()
        return c

    lax.fori_loop(0, tm, issue, 0, unroll=DMA_UNROLL)
    for _ in range(2):
        pltpu.make_async_copy(h_ref, rows_ref.at[pl.ds(0, tm)], sem).wait()


def _dispatch(dest_flat, h2, n_rows):
    N, D = h2.shape
    tm = min(TM_MOVE, N)
    return pl.pallas_call(
        functools.partial(_dispatch_kernel, tm=tm, n_tok=N),
        grid_spec=pltpu.PrefetchScalarGridSpec(
            num_scalar_prefetch=1,
            grid=(N // tm,),
            in_specs=[pl.BlockSpec((tm, D), lambda i, d: (i, 0))],
            out_specs=pl.BlockSpec(memory_space=pl.ANY),
            scratch_shapes=[pltpu.SemaphoreType.DMA(())],
        ),
        out_shape=jax.ShapeDtypeStruct((n_rows, D), F32),
        compiler_params=_params("arbitrary"),
    )(dest_flat, h2)


def _experts_kernel(blk_ref, exp_ref, used_ref, x_ref, w1_ref, w3_ref, w2_ref, y_ref, w13_s, w2_s):
    j = pl.program_id(0)

    @pl.when(j < used_ref[0])
    def _():
        prev = exp_ref[jnp.maximum(j - 1, 0)]

        @pl.when((j == 0) | (exp_ref[j] != prev))
        def _():
            for c in range(D_FF_E // FF_CHUNK):
                w13_s[:, 2 * FF_CHUNK * c:2 * FF_CHUNK * c + FF_CHUNK] = \
                    w1_ref[0, 0, :, FF_CHUNK * c:FF_CHUNK * (c + 1)].astype(BF16)
                w13_s[:, 2 * FF_CHUNK * c + FF_CHUNK:2 * FF_CHUNK * (c + 1)] = \
                    w3_ref[0, 0, :, FF_CHUNK * c:FF_CHUNK * (c + 1)].astype(BF16)
            w2_s[...] = w2_ref[0, 0].astype(BF16)

        x = x_ref[...].astype(BF16)
        y = None
        for c in range(D_FF_E // FF_CHUNK):
            ab = jnp.dot(x, w13_s[:, 2 * FF_CHUNK * c:2 * FF_CHUNK * (c + 1)], preferred_element_type=F32)
            a = ab[:, 0:FF_CHUNK]
            hmid = (a * jax.nn.sigmoid(a) * ab[:, FF_CHUNK:]).astype(BF16)
            part = jnp.dot(hmid, w2_s[FF_CHUNK * c:FF_CHUNK * (c + 1), :], preferred_element_type=F32)
            y = part if y is None else y + part
        y_ref[...] = y


def _experts(blk_map, blk_exp, n_used, x_rows, w1, w3, w2, layer):
    R, D = x_rows.shape
    nb = R // TB_EXP
    return pl.pallas_call(
        _experts_kernel,
        grid_spec=pltpu.PrefetchScalarGridSpec(
            num_scalar_prefetch=3,
            grid=(nb,),
            in_specs=[
                pl.BlockSpec((TB_EXP, D), lambda j, bm, be, nu: (bm[j], 0)),
                pl.BlockSpec((1, 1, D, D_FF_E), lambda j, bm, be, nu: (layer, be[j], 0, 0)),
                pl.BlockSpec((1, 1, D, D_FF_E), lambda j, bm, be, nu: (layer, be[j], 0, 0)),
                pl.BlockSpec((1, 1, D_FF_E, D), lambda j, bm, be, nu: (layer, be[j], 0, 0)),
            ],
            out_specs=pl.BlockSpec((TB_EXP, D), lambda j, bm, be, nu: (bm[j], 0)),
            scratch_shapes=[pltpu.VMEM((D, 2 * D_FF_E), BF16), pltpu.VMEM((D_FF_E, D), BF16)],
        ),
        out_shape=jax.ShapeDtypeStruct((R, D), F32),
        compiler_params=_params("arbitrary"),
    )(blk_map, blk_exp, n_used, x_rows, w1, w3, w2)


def _combine_kernel(dest_ref, x1_ref, route_ref, rows_ref, o_ref, buf, sem, *, tm, n_tok):
    base = pl.program_id(0) * tm

    def issue(r, c):
        pltpu.make_async_copy(rows_ref.at[pl.ds(dest_ref[base + r], 1)], buf.at[0, pl.ds(r, 1)], sem).start()
        pltpu.make_async_copy(rows_ref.at[pl.ds(dest_ref[n_tok + base + r], 1)], buf.at[1, pl.ds(r, 1)], sem).start()
        return c

    lax.fori_loop(0, tm, issue, 0, unroll=DMA_UNROLL)
    for k in range(2):
        pltpu.make_async_copy(rows_ref.at[pl.ds(0, tm)], buf.at[k], sem).wait()
    route = route_ref[...]
    g1 = route[:, ROUTE_G1:ROUTE_G1 + 1]
    g2 = route[:, ROUTE_G2:ROUTE_G2 + 1]
    o_ref[...] = x1_ref[...] + g1 * buf[0] + g2 * buf[1]


def _combine(dest_flat, x1, route, y_rows):
    N, D = x1.shape
    tm = min(TM_MOVE, N)
    return pl.pallas_call(
        functools.partial(_combine_kernel, tm=tm, n_tok=N),
        grid_spec=pltpu.PrefetchScalarGridSpec(
            num_scalar_prefetch=1,
            grid=(N // tm,),
            in_specs=[
                pl.BlockSpec((tm, D), lambda i, d: (i, 0)),
                pl.BlockSpec((tm, LANES), lambda i, d: (i, 0)),
                pl.BlockSpec(memory_space=pl.ANY),
            ],
            out_specs=pl.BlockSpec((tm, D), lambda i, d: (i, 0)),
            scratch_shapes=[pltpu.VMEM((2, tm, D), F32), pltpu.SemaphoreType.DMA(())],
        ),
        out_shape=jax.ShapeDtypeStruct((N, D), F32),
        compiler_params=_params("arbitrary"),
    )(dest_flat, x1, route, y_rows)


def _moe(x1, h2, route, route_t, counts, w1, w3, w2, layer):
    N, _ = x1.shape
    nb = (2 * N + N_EXPERTS * (TB_EXP - 1)) // TB_EXP + 1
    e_id = route_t[ROUTE_E1:ROUTE_E2 + 1].astype(jnp.int32)
    rank = route_t[ROUTE_R1:ROUTE_R2 + 1].astype(jnp.int32)
    cnt = counts[0, :N_EXPERTS].astype(jnp.int32)
    padded = (cnt + TB_EXP - 1) // TB_EXP * TB_EXP
    pad_end = jnp.cumsum(padded)
    pad_start = pad_end - padded
    e_flat = e_id.reshape(1, 2 * N)
    is_e = e_flat == jnp.arange(N_EXPERTS, dtype=jnp.int32)[:, None]
    dest = jnp.sum(jnp.where(is_e, pad_start[:, None], 0), axis=0) + rank.reshape(2 * N)
    n_used = pad_end[-1] // TB_EXP
    blk_map = jnp.minimum(jnp.arange(nb, dtype=jnp.int32), jnp.maximum(n_used - 1, 0))
    blk_exp = jnp.sum(pad_end[None, :] <= (blk_map * TB_EXP)[:, None], axis=1).astype(jnp.int32)
    blk_exp = jnp.minimum(blk_exp, N_EXPERTS - 1)
    x_rows = _dispatch(dest, h2, nb * TB_EXP)
    y_rows = _experts(blk_map, blk_exp, n_used.reshape(1).astype(jnp.int32), x_rows, w1, w3, w2, layer)
    return _combine(dest, x1, route, y_rows)


def _pad_heads(w, width):
    k = w.shape[0]
    w = w.reshape(k, N_HEADS, width)
    return jnp.pad(w, ((0, 0), (0, 0), (0, HEAD_PAD - width))).reshape(k, N_HEADS * HEAD_PAD)


def _swap_rope(w):
    half = QK_ROPE // 2
    lo, hi = w[..., QK_NOPE:QK_NOPE + half], w[..., QK_NOPE + half:QK_HEAD]
    pad = [(0, 0)] * (w.ndim - 1)
    return jnp.pad(jnp.concatenate([hi, lo], -1), pad + [(QK_NOPE, HEAD_PAD - QK_HEAD)])


def _rope_tables(positions):
    inv_freq = ROPE_THETA ** (-jnp.arange(0, QK_ROPE, 2, dtype=F32) / QK_ROPE)
    ang = positions.astype(F32)[..., None] * inv_freq
    table = jnp.concatenate([jnp.cos(ang), jnp.sin(ang)], -1)
    return jnp.pad(table, ((0, 0), (0, 0), (QK_NOPE, LANES - QK_HEAD)))


def _ssm_params(lam_re, lam_im, b_re, b_im, c_re, c_im, log_dt):
    lam = lax.complex(lam_re, lam_im)
    dt = jnp.exp(log_dt)[:, None]
    lam_bar = jnp.exp(lam * dt)
    b_bar = ((lam_bar - 1.0) / lam)[..., None] * lax.complex(b_re, b_im)
    eye = jnp.eye(SSM_GROUPS, dtype=F32)

    def in_blockdiag(m):
        return jnp.einsum("gpc,gh->gchp", m, eye).reshape(SSM_WIDTH, SSM_FLAT)

    def out_blockdiag(m):
        return jnp.einsum("gcp,gh->gphc", m, eye).reshape(SSM_FLAT, SSM_WIDTH)

    bbd = jnp.concatenate([in_blockdiag(jnp.real(b_bar)), in_blockdiag(jnp.imag(b_bar))], axis=1)
    cbd = jnp.concatenate([out_blockdiag(c_re), out_blockdiag(-c_im)], axis=0)
    lam_rows = jnp.stack([jnp.real(lam_bar).reshape(SSM_FLAT), jnp.imag(lam_bar).reshape(SSM_FLAT)])
    return bbd.astype(BF16), lam_rows, cbd.astype(BF16)


def kernel(x, positions, mix_norm, w_in, q_a_norm, w_uq, kv_a_norm, w_ukv, q_norm, k_norm, ssm_lam_re, ssm_lam_im, ssm_b_re, ssm_b_im, ssm_c_re, ssm_c_im, ssm_d, ssm_log_dt, ssm_w_glu, conv_dw_w, conv_dw_b, conv_ln_w, conv_ln_b, out_norm, w_out, ffn_norm, w_grp, b_grp, w_exp, b_exp, w1, w3, w2):
    B, L, D = x.shape
    depth = w_in.shape[0]
    rope = _rope_tables(positions)
    row = lambda v: v.reshape(1, -1)
    lane_pad = lambda v: jnp.pad(v, (0, LANES - v.shape[0])).reshape(1, LANES)
    for l in range(depth):
        c_q, c_kv, k_pe, u_s, c_a, c_g = jnp.split(
            w_in[l], [Q_LORA, Q_LORA + KV_LORA, Q_LORA + KV_LORA + QK_ROPE,
                      Q_LORA + KV_LORA + QK_ROPE + SSM_WIDTH,
                      Q_LORA + KV_LORA + QK_ROPE + SSM_WIDTH + CONV_WIDTH], axis=1)
        k_pe_full = jnp.pad(k_pe, ((0, 0), (QK_NOPE, 0)))
        win = jnp.concatenate([c_q, c_kv, u_s, c_a, c_g, jnp.pad(k_pe_full, ((0, 0), (0, HEAD_PAD - QK_HEAD))),
                               _swap_rope(k_pe_full)], axis=1).astype(BF16)
        wkv = w_ukv[l].reshape(KV_LORA, N_HEADS, QK_NOPE + V_HEAD)
        wuk = _pad_heads(wkv[:, :, :QK_NOPE].reshape(KV_LORA, N_HEADS * QK_NOPE), QK_NOPE).astype(BF16)
        wuv = wkv[:, :, QK_NOPE:].reshape(KV_LORA, ATTN_WIDTH).astype(BF16)
        wuq_sw = _swap_rope(w_uq[l].reshape(Q_LORA, N_HEADS, QK_HEAD)).reshape(Q_LORA, N_HEADS * HEAD_PAD)
        wuq = jnp.concatenate([_pad_heads(w_uq[l], QK_HEAD), wuq_sw], axis=1).astype(BF16)
        qn_scaled = q_norm[l] * (QK_HEAD ** -0.5)
        norm_rows = lambda w: jnp.stack([jnp.pad(w, (0, HEAD_PAD - QK_HEAD)), _swap_rope(w)])
        q, k, v, u_tm, ca, cg = _in_proj(
            x, rope, row(mix_norm[l]), win, row(q_a_norm[l]), wuq, row(kv_a_norm[l]), wuk, wuv,
            norm_rows(qn_scaled), norm_rows(k_norm[l]))
        y_attn = _attention(q, k, v)

        g_out = out_norm[l]
        bbd, lam_rows, cbd = _ssm_params(ssm_lam_re[l], ssm_lam_im[l], ssm_b_re[l], ssm_b_im[l],
                                         ssm_c_re[l], ssm_c_im[l], ssm_log_dt[l])
        y_ssm_tm = _ssm(u_tm.reshape(L * B, SSM_WIDTH), bbd, lam_rows, cbd, row(ssm_d[l]),
                        ssm_w_glu[l].astype(BF16), row(g_out[ATTN_WIDTH:ATTN_WIDTH + SSM_WIDTH]), B)
        y_conv = _conv(ca, cg, conv_dw_w[l], row(conv_dw_b[l]), row(conv_ln_w[l]), row(conv_ln_b[l]),
                       row(g_out[ATTN_WIDTH + SSM_WIDTH:]))

        w_route = jnp.pad(jnp.concatenate([w_exp[l], w_grp[l]], axis=1), ((0, 0), (0, LANES - N_EXPERTS - N_EGROUPS)))
        w_route_hi = w_route.astype(BF16)
        w_route = jnp.concatenate([w_route_hi, (w_route - w_route_hi.astype(F32)).astype(BF16)], axis=1)
        b_route = lane_pad(jnp.concatenate([b_exp[l], b_grp[l]]))
        x1, h2, route, route_t, counts = _out_router(
            x.reshape(B * L, D), y_attn.reshape(B * L, ATTN_WIDTH), y_ssm_tm.reshape(L, B * SSM_WIDTH),
            y_conv.reshape(B * L, CONV_WIDTH), row(g_out[:ATTN_WIDTH]), w_out[l].astype(BF16),
            row(ffn_norm[l]), w_route, b_route, B)
        x = _moe(x1, h2, route, route_t, counts, w1, w3, w2, l).reshape(B, L, D)
    return x
```

```python
import functools
import math

import jax
import jax.numpy as jnp
from jax import lax
from jax.experimental import pallas as pl
from jax.experimental.pallas import tpu as pltpu

D_MODEL = 1024
CHUNK = 64
EPS = 1e-6
N_HEADS = 8
QK_NOPE = 64
QK_ROPE = 32
QK_HEAD = QK_NOPE + QK_ROPE
V_HEAD = 64
Q_LORA = 256
KV_LORA = 128
ROPE_THETA = 10000.0
ATTN_WIDTH = N_HEADS * V_HEAD
SSM_WIDTH = 256
SSM_GROUP = 16
SSM_GROUPS = SSM_WIDTH // SSM_GROUP
SSM_STATE = 64
SSM_FLAT = SSM_GROUPS * SSM_STATE
CONV_WIDTH = 256
CONV_K = 31
N_EGROUPS = 4
EXP_PER_GROUP = 8
N_EXPERTS = N_EGROUPS * EXP_PER_GROUP
D_FF_E = 512

LANES = 128
SUBLANES = 8
HEAD_PAD = LANES
IN_PROJ_PAD = Q_LORA + KV_LORA + SSM_WIDTH + 2 * CONV_WIDTH + 2 * LANES
CONV_HALO = 32
VMEM_LIMIT = 48 * 1024 * 1024

TM_PROJ = 512
TQ_ATTN = 256
TC_SSM = 64
T_CONV = 256
CONV_SUB = 64
TM_OUT = 512
TM_MOVE = 256
RUN_ROWS = SUBLANES
RUN_SLOTS = 2 * N_EXPERTS + 2 * TM_MOVE // RUN_ROWS
DMA_UNROLL = 8
TB_EXP = 256
FF_CHUNK = 256

BF16 = jnp.bfloat16
F32 = jnp.float32


def _rms(x, w):
    return x * lax.rsqrt(jnp.mean(x * x, axis=-1, keepdims=True) + EPS) * w


def _params(*sem):
    return pltpu.CompilerParams(dimension_semantics=sem, vmem_limit_bytes=VMEM_LIMIT)


def _in_proj_kernel(x_ref, rope_ref, mixn_ref, win_ref, qan_ref, wuq_ref, kvan_ref, wuk_ref, wuv_ref,
                    qn_ref, kn_ref, q_ref, k_ref, v_ref, u_ref, ca_ref, cg_ref):
    x = x_ref[0]
    h = _rms(x, mixn_ref[...]).astype(BF16)
    proj = jnp.dot(h, win_ref[...], preferred_element_type=F32)
    o = 0
    c_q = proj[:, o:o + Q_LORA]; o += Q_LORA
    c_kv = proj[:, o:o + KV_LORA]; o += KV_LORA
    u_ref[...] = proj[:, o:o + SSM_WIDTH].astype(BF16); o += SSM_WIDTH
    ca_ref[0] = proj[:, o:o + CONV_WIDTH].astype(BF16); o += CONV_WIDTH
    cg_ref[0] = proj[:, o:o + CONV_WIDTH].astype(BF16); o += CONV_WIDTH
    k_pe = proj[:, o:o + LANES]; o += LANES
    k_pe_sw = proj[:, o:o + LANES]

    width = N_HEADS * HEAD_PAD
    q2 = jnp.dot(_rms(c_q, qan_ref[...]).astype(BF16), wuq_ref[...], preferred_element_type=F32)
    q, q_sw = q2[:, 0:width], q2[:, width:2 * width]
    ckv_n = _rms(c_kv, kvan_ref[...]).astype(BF16)
    kn = jnp.dot(ckv_n, wuk_ref[...], preferred_element_type=F32)
    v_ref[0] = jnp.dot(ckv_n, wuv_ref[...], preferred_element_type=F32).astype(BF16)

    rope = rope_ref[0]
    half = QK_ROPE // 2
    lane = lax.broadcasted_iota(jnp.int32, rope.shape, 1)
    lo = (lane >= QK_NOPE) & (lane < QK_NOPE + half)
    hi = (lane >= QK_NOPE + half) & (lane < QK_HEAD)
    cos_t = jnp.where(lane < QK_NOPE, 1.0, jnp.where(lo, rope, jnp.where(hi, pltpu.roll(rope, half, 1), 0.0)))
    sin_t = jnp.where(lo, -pltpu.roll(rope, LANES - half, 1), jnp.where(hi, rope, 0.0))
    a_q, b_q = qn_ref[0:1, :] * cos_t, qn_ref[1:2, :] * sin_t
    a_k = kn_ref[0:1, :] * cos_t
    k_sw_term = k_pe_sw * (kn_ref[1:2, :] * sin_t)

    def inv_rms(y):
        return lax.rsqrt(jnp.sum(y * y, axis=-1, keepdims=True) * (1.0 / QK_HEAD) + EPS)

    for hd in range(N_HEADS):
        sl = slice(hd * HEAD_PAD, (hd + 1) * HEAD_PAD)
        yq = q[:, sl]
        q_ref[0, hd] = ((yq * a_q + q_sw[:, sl] * b_q) * inv_rms(yq)).astype(BF16)
        yk = kn[:, sl] + k_pe
        k_ref[0, hd] = ((yk * a_k + k_sw_term) * inv_rms(yk)).astype(BF16)


def _in_proj(x, rope, mixn, win, qan, wuq, kvan, wuk, wuv, qn, kn):
    B, L, D = x.shape
    tm = min(TM_PROJ, L)
    full = lambda shape: pl.BlockSpec(shape, lambda b, t: (0,) * len(shape))
    return pl.pallas_call(
        _in_proj_kernel,
        grid=(B, L // tm),
        in_specs=[
            pl.BlockSpec((1, tm, D), lambda b, t: (b, t, 0)),
            pl.BlockSpec((1, tm, LANES), lambda b, t: (b, t, 0)),
            full((1, D)), full((D, IN_PROJ_PAD)), full((1, Q_LORA)), full((Q_LORA, 2 * N_HEADS * HEAD_PAD)),
            full((1, KV_LORA)), full((KV_LORA, N_HEADS * HEAD_PAD)), full((KV_LORA, ATTN_WIDTH)),
            full((2, HEAD_PAD)), full((2, HEAD_PAD)),
        ],
        out_specs=[
            pl.BlockSpec((1, N_HEADS, tm, HEAD_PAD), lambda b, t: (b, 0, t, 0)),
            pl.BlockSpec((1, N_HEADS, tm, HEAD_PAD), lambda b, t: (b, 0, t, 0)),
            pl.BlockSpec((1, tm, ATTN_WIDTH), lambda b, t: (b, t, 0)),
            pl.BlockSpec((tm, SSM_WIDTH), lambda b, t: (t, b)),
            pl.BlockSpec((1, tm, CONV_WIDTH), lambda b, t: (b, t, 0)),
            pl.BlockSpec((1, tm, CONV_WIDTH), lambda b, t: (b, t, 0)),
        ],
        out_shape=[
            jax.ShapeDtypeStruct((B, N_HEADS, L, HEAD_PAD), BF16),
            jax.ShapeDtypeStruct((B, N_HEADS, L, HEAD_PAD), BF16),
            jax.ShapeDtypeStruct((B, L, ATTN_WIDTH), BF16),
            jax.ShapeDtypeStruct((L, B * SSM_WIDTH), BF16),
            jax.ShapeDtypeStruct((B, L, CONV_WIDTH), BF16),
            jax.ShapeDtypeStruct((B, L, CONV_WIDTH), BF16),
        ],
        compiler_params=_params("parallel", "parallel"),
    )(x, rope, mixn, win, qan, wuq, kvan, wuk, wuv, qn, kn)


def _attention_kernel(q_ref, k_ref, v_ref, o_ref, vext_ref, *, seq, tq):
    pair = 2 * V_HEAD
    vext_ref[:, 0:pair] = v_ref[0]
    vext_ref[:, pair:] = jnp.ones((seq, LANES), BF16)
    row_chunk = lax.broadcasted_iota(jnp.int32, (tq, tq), 0) // CHUNK
    col_chunk = lax.broadcasted_iota(jnp.int32, (tq, tq), 1) // CHUNK
    visible = col_chunk <= row_chunk
    neg = jnp.finfo(F32).min
    nt = (((1,), (1,)), ((), ()))
    low_lanes = lax.broadcasted_iota(jnp.int32, (tq, pair), 1) < V_HEAD
    for i in range(seq // tq):
        q0 = i * tq
        outs = []
        for hh in range(2):
            qb = q_ref[0, hh, q0:q0 + tq, :]
            s_d = lax.dot_general(qb, k_ref[0, hh, q0:q0 + tq, :], nt, preferred_element_type=F32)
            s_d = jnp.where(visible, s_d, neg)
            m = jnp.max(s_d, axis=-1, keepdims=True)
            if i > 0:
                s_l = lax.dot_general(qb, k_ref[0, hh, 0:q0, :], nt, preferred_element_type=F32)
                m = jnp.maximum(m, jnp.max(s_l, axis=-1, keepdims=True))
            acc = jnp.dot(jnp.exp(s_d - m).astype(BF16), vext_ref[q0:q0 + tq, :], preferred_element_type=F32)
            if i > 0:
                acc = acc + jnp.dot(jnp.exp(s_l - m).astype(BF16), vext_ref[0:q0, :], preferred_element_type=F32)
            outs.append(acc[:, 0:pair] / acc[:, pair:])
        o_ref[0, q0:q0 + tq, :] = jnp.where(low_lanes, outs[0], outs[1]).astype(BF16)


def _attention(q, k, v):
    B, H, L, _ = q.shape
    tq = min(TQ_ATTN, L)
    return pl.pallas_call(
        functools.partial(_attention_kernel, seq=L, tq=tq),
        grid=(B, H // 2),
        in_specs=[
            pl.BlockSpec((1, 2, L, HEAD_PAD), lambda b, p: (b, p, 0, 0)),
            pl.BlockSpec((1, 2, L, HEAD_PAD), lambda b, p: (b, p, 0, 0)),
            pl.BlockSpec((1, L, 2 * V_HEAD), lambda b, p: (b, 0, p)),
        ],
        out_specs=pl.BlockSpec((1, L, 2 * V_HEAD), lambda b, p: (b, 0, p)),
        out_shape=jax.ShapeDtypeStruct((B, L, ATTN_WIDTH), BF16),
        scratch_shapes=[pltpu.VMEM((L, 2 * V_HEAD + LANES), BF16)],
        compiler_params=_params("parallel", "parallel"),
    )(q, k, v)


def _ssm_kernel(u_ref, bbd_ref, lam_ref, cbd_ref, d_ref, wglu_ref, g_ref, o_ref, state_ref, bu_ref, *, batch, tc):
    @pl.when(pl.program_id(0) == 0)
    def _():
        state_ref[...] = jnp.zeros_like(state_ref)

    u = u_ref[...]
    bu_ref[...] = jnp.dot(u, bbd_ref[...], preferred_element_type=F32)
    lam_re = jnp.broadcast_to(lam_ref[0:1, :], (batch, SSM_FLAT))
    lam_im = jnp.broadcast_to(lam_ref[1:2, :], (batch, SSM_FLAT))

    def step(t, carry):
        xr, xi = carry
        rows = pl.ds(pl.multiple_of(t * batch, batch), batch)
        nr = lam_re * xr - lam_im * xi + bu_ref[rows, 0:SSM_FLAT]
        ni = lam_re * xi + lam_im * xr + bu_ref[rows, SSM_FLAT:2 * SSM_FLAT]
        bu_ref[rows, 0:SSM_FLAT] = nr
        bu_ref[rows, SSM_FLAT:2 * SSM_FLAT] = ni
        return nr, ni

    xr, xi = lax.fori_loop(0, tc, step, (state_ref[:, 0:SSM_FLAT], state_ref[:, SSM_FLAT:2 * SSM_FLAT]),
                           unroll=2)
    state_ref[:, 0:SSM_FLAT] = xr
    state_ref[:, SSM_FLAT:2 * SSM_FLAT] = xi

    y = jnp.dot(bu_ref[...].astype(BF16), cbd_ref[...], preferred_element_type=F32)
    y = y + d_ref[...] * u.astype(F32)
    z = jax.nn.gelu(y)
    gate = jax.nn.sigmoid(jnp.dot(z.astype(BF16), wglu_ref[...], preferred_element_type=F32))
    o_ref[...] = _rms(z * gate, g_ref[...]).astype(BF16)


def _ssm(u_tm, bbd, lam, cbd, d, wglu, g, batch):
    rows = u_tm.shape[0]
    seq = rows // batch
    tc = min(TC_SSM, seq)
    blk = tc * batch
    full = lambda shape: pl.BlockSpec(shape, lambda t: (0,) * len(shape))
    return pl.pallas_call(
        functools.partial(_ssm_kernel, batch=batch, tc=tc),
        grid=(seq // tc,),
        in_specs=[
            pl.BlockSpec((blk, SSM_WIDTH), lambda t: (t, 0)),
            full((SSM_WIDTH, 2 * SSM_FLAT)), full((2, SSM_FLAT)), full((2 * SSM_FLAT, SSM_WIDTH)),
            full((1, SSM_WIDTH)), full((SSM_WIDTH, SSM_WIDTH)), full((1, SSM_WIDTH)),
        ],
        out_specs=pl.BlockSpec((blk, SSM_WIDTH), lambda t: (t, 0)),
        out_shape=jax.ShapeDtypeStruct((rows, SSM_WIDTH), BF16),
        scratch_shapes=[pltpu.VMEM((batch, 2 * SSM_FLAT), F32), pltpu.VMEM((blk, 2 * SSM_FLAT), F32)],
        compiler_params=_params("arbitrary"),
    )(u_tm, bbd, lam, cbd, d, wglu, g)


def _conv_kernel(a_ref, g_ref, ah_ref, gh_ref, w_ref, b_ref, lnw_ref, lnb_ref, gn_ref, o_ref, upad_ref, *, tt):
    first = pl.program_id(1) == 0
    halo = ah_ref[0].astype(F32) * jax.nn.sigmoid(gh_ref[0].astype(F32))
    upad_ref[0:CONV_HALO, :] = jnp.where(first, 0.0, halo)
    upad_ref[CONV_HALO:, :] = a_ref[0].astype(F32) * jax.nn.sigmoid(g_ref[0].astype(F32))
    base = CONV_HALO - (CONV_K - 1)
    for s in range(tt // CONV_SUB):
        r0 = s * CONV_SUB
        acc = jnp.zeros((CONV_SUB, CONV_WIDTH), F32)
        for off in range(SUBLANES):
            n = CONV_SUB + (SUBLANES if off else 0)
            part = None
            for a8 in range(0, CONV_HALO + 1, SUBLANES):
                kk = a8 + off - base
                if 0 <= kk < CONV_K:
                    term = w_ref[kk:kk + 1, :] * upad_ref[r0 + a8:r0 + a8 + n, :]
                    part = term if part is None else part + term
            acc = acc + part[off:off + CONV_SUB]
        y = acc + b_ref[...]
        mu = jnp.mean(y, axis=-1, keepdims=True)
        var = jnp.mean(jnp.square(y - mu), axis=-1, keepdims=True)
        y = (y - mu) * lax.rsqrt(var + 1e-5) * lnw_ref[...] + lnb_ref[...]
        y = y * jax.nn.sigmoid(y)
        o_ref[0, r0:r0 + CONV_SUB, :] = _rms(y, gn_ref[...]).astype(BF16)


def _conv(a, g, w, b, lnw, lnb, gn):
    B, L, C = a.shape
    tt = min(T_CONV, L)
    ratio = tt // CONV_HALO
    full = lambda shape: pl.BlockSpec(shape, lambda bb, t: (0,) * len(shape))
    cur = pl.BlockSpec((1, tt, C), lambda bb, t: (bb, t, 0))
    prev = pl.BlockSpec((1, CONV_HALO, C), lambda bb, t: (bb, jnp.maximum(t * ratio - 1, 0), 0))
    return pl.pallas_call(
        functools.partial(_conv_kernel, tt=tt),
        grid=(B, L // tt),
        in_specs=[cur, cur, prev, prev, full((CONV_K, C)), full((1, C)), full((1, C)), full((1, C)), full((1, C))],
        out_specs=pl.BlockSpec((1, tt, C), lambda bb, t: (bb, t, 0)),
        out_shape=jax.ShapeDtypeStruct((B, L, C), BF16),
        scratch_shapes=[pltpu.VMEM((tt + CONV_HALO, C), F32)],
        compiler_params=_params("parallel", "parallel"),
    )(a, g, a, g, w, b, lnw, lnb, gn)


ROUTE_E1, ROUTE_E2, ROUTE_R1, ROUTE_R2, ROUTE_G1, ROUTE_G2 = range(6)
ROUTE_ROWS = 8
GRP_LANE0 = N_EXPERTS


def _out_router_kernel(x_ref, ya_ref, ys_ref, yc_ref, ga_ref, wo_ref, fn_ref, wr_ref, br_ref,
                       x1_ref, h2_ref, route_ref, route_t_ref, cnt_ref, sub_ref, carry_ref, *, tm):
    @pl.when(pl.program_id(0) == 0)
    def _():
        carry_ref[...] = jnp.zeros_like(carry_ref)

    ya = _rms(ya_ref[...].astype(F32), ga_ref[...]).astype(BF16)
    acc = jnp.dot(ya, wo_ref[0:ATTN_WIDTH, :], preferred_element_type=F32)
    acc += jnp.dot(ys_ref[...], wo_ref[ATTN_WIDTH:ATTN_WIDTH + SSM_WIDTH, :], preferred_element_type=F32)
    acc += jnp.dot(yc_ref[...], wo_ref[ATTN_WIDTH + SSM_WIDTH:, :], preferred_element_type=F32)
    x1 = x_ref[...] + acc
    x1_ref[...] = x1
    h2 = _rms(x1, fn_ref[...])
    h2_ref[...] = h2

    h_hi = h2.astype(BF16)
    h_lo = (h2 - h_hi.astype(F32)).astype(BF16)
    part = jnp.dot(h_hi, wr_ref[...], preferred_element_type=F32)
    logits = (part[:, 0:LANES] + part[:, LANES:2 * LANES]
              + jnp.dot(h_lo, wr_ref[:, 0:LANES], preferred_element_type=F32) + br_ref[...])
    lane = lax.broadcasted_iota(jnp.int32, (tm, LANES), 1)
    ninf = -jnp.inf
    big = LANES

    def first_argmax(vals, vmax):
        return jnp.min(jnp.where(vals == vmax, lane, big), axis=-1, keepdims=True)

    grp = jnp.where((lane >= GRP_LANE0) & (lane < GRP_LANE0 + N_EGROUPS), logits, ninf)
    gmax = jnp.max(grp, axis=-1, keepdims=True)
    gsel = first_argmax(grp, gmax) - GRP_LANE0
    p_grp = 1.0 / jnp.sum(jnp.exp(grp - gmax), axis=-1, keepdims=True)

    el = jnp.where((lane < N_EXPERTS) & ((lane // EXP_PER_GROUP) == gsel), logits, ninf)
    m1 = jnp.max(el, axis=-1, keepdims=True)
    e1 = first_argmax(el, m1)
    el2 = jnp.where(lane == e1, ninf, el)
    m2 = jnp.max(el2, axis=-1, keepdims=True)
    e2 = first_argmax(el2, m2)
    t2 = jnp.exp(m2 - m1)
    g1 = p_grp / (1.0 + t2)
    g2 = p_grp * t2 / (1.0 + t2)

    hit1 = lane == e1
    hit2 = lane == e2
    cnt = (hit1 | hit2).astype(F32)
    rr = lax.broadcasted_iota(jnp.int32, (tm, tm), 0)
    cc = lax.broadcasted_iota(jnp.int32, (tm, tm), 1)
    tri = (cc < rr).astype(BF16)
    before = jnp.dot(tri, cnt.astype(BF16), preferred_element_type=F32) + carry_ref[...]
    r1 = jnp.sum(jnp.where(hit1, before, 0.0), axis=-1, keepdims=True)
    r2 = jnp.sum(jnp.where(hit2, before, 0.0), axis=-1, keepdims=True)
    subs = [before[h * TM_MOVE:h * TM_MOVE + 1, :] for h in range(tm // TM_MOVE)]
    sub_ref[...] = jnp.concatenate(subs + [jnp.zeros((SUBLANES - len(subs), LANES), F32)], axis=0)
    carry_ref[...] += jnp.sum(cnt, axis=0, keepdims=True)
    cnt_ref[...] = carry_ref[...]

    rec = jnp.where(lane == ROUTE_E1, e1.astype(F32), 0.0)
    rec = jnp.where(lane == ROUTE_E2, e2.astype(F32), rec)
    rec = jnp.where(lane == ROUTE_R1, r1, rec)
    rec = jnp.where(lane == ROUTE_R2, r2, rec)
    rec = jnp.where(lane == ROUTE_G1, g1, rec)
    rec = jnp.where(lane == ROUTE_G2, g2, rec)
    route_ref[...] = rec
    route_t_ref[...] = rec.T[0:ROUTE_ROWS, :]


def _out_router(x, ya, ys_tm, yc, ga, wo, fn, wr, br, batch):
    N, D = x.shape
    seq = N // batch
    tm = min(TM_OUT, seq)
    per_b = seq // tm
    full = lambda shape: pl.BlockSpec(shape, lambda i: (0,) * len(shape))
    tile = lambda w: pl.BlockSpec((tm, w), lambda i: (i, 0))
    return pl.pallas_call(
        functools.partial(_out_router_kernel, tm=tm),
        grid=(N // tm,),
        in_specs=[
            tile(D), tile(ATTN_WIDTH),
            pl.BlockSpec((tm, SSM_WIDTH), lambda i: (i % per_b, i // per_b)),
            tile(CONV_WIDTH),
            full((1, ATTN_WIDTH)), full((D, D)), full((1, D)), full((D, 2 * LANES)), full((1, LANES)),
        ],
        out_specs=[tile(D), tile(D), tile(LANES), pl.BlockSpec((ROUTE_ROWS, tm), lambda i: (0, i)), full((1, LANES)),
                   pl.BlockSpec((SUBLANES, LANES), lambda i: (i, 0))],
        out_shape=[
            jax.ShapeDtypeStruct((N, D), F32),
            jax.ShapeDtypeStruct((N, D), F32),
            jax.ShapeDtypeStruct((N, LANES), F32),
            jax.ShapeDtypeStruct((ROUTE_ROWS, N), F32),
            jax.ShapeDtypeStruct((1, LANES), F32),
            jax.ShapeDtypeStruct((N // tm * SUBLANES, LANES), F32),
        ],
        scratch_shapes=[pltpu.VMEM((1, LANES), F32)],
        compiler_params=_params("arbitrary"),
    )(x, ya, ys_tm, yc, ga, wo, fn, wr, br)


def _dispatch_kernel(dest_ref, h_ref, rows_ref, sem, *, tm, n_tok):
    base = pl.program_id(0) * tm

    def issue(r, c):
        src = h_ref.at[pl.ds(r, 1)]
        pltpu.make_async_copy(src, rows_ref.at[pl.ds(dest_ref[base + r], 1)], sem).start()
        pltpu.make_async_copy(src, rows_ref.at[pl.ds(dest_ref[n_tok + base + r], 1)], sem).start()
        return c

    lax.fori_loop(0, tm, issue, 0, unroll=DMA_UNROLL)
    for _ in range(2):
        pltpu.make_async_copy(h_ref, rows_ref.at[pl.ds(0, tm)], sem).wait()


def _dispatch(dest_flat, h2, n_rows):
    N, D = h2.shape
    tm = min(TM_MOVE, N)
    return pl.pallas_call(
        functools.partial(_dispatch_kernel, tm=tm, n_tok=N),
        grid_spec=pltpu.PrefetchScalarGridSpec(
            num_scalar_prefetch=1,
            grid=(N // tm,),
            in_specs=[pl.BlockSpec((tm, D), lambda i, d: (i, 0))],
            out_specs=pl.BlockSpec(memory_space=pl.ANY),
            scratch_shapes=[pltpu.SemaphoreType.DMA(())],
        ),
        out_shape=jax.ShapeDtypeStruct((n_rows, D), F32),
        compiler_params=_params("arbitrary"),
    )(dest_flat, h2)


def _experts_kernel(blk_ref, exp_ref, valid_ref, used_ref, x_ref, w1_ref, w3_ref, w2_ref, y_ref, w13_s, w2_s):
    j = pl.program_id(0)

    @pl.when(j < used_ref[0])
    def _():
        prev = exp_ref[jnp.maximum(j - 1, 0)]

        @pl.when((j == 0) | (exp_ref[j] != prev))
        def _():
            for c in range(D_FF_E // FF_CHUNK):
                w13_s[:, 2 * FF_CHUNK * c:2 * FF_CHUNK * c + FF_CHUNK] = \
                    w1_ref[0, 0, :, FF_CHUNK * c:FF_CHUNK * (c + 1)].astype(BF16)
                w13_s[:, 2 * FF_CHUNK * c + FF_CHUNK:2 * FF_CHUNK * (c + 1)] = \
                    w3_ref[0, 0, :, FF_CHUNK * c:FF_CHUNK * (c + 1)].astype(BF16)
            w2_s[...] = w2_ref[0, 0].astype(BF16)

        x = x_ref[...].astype(BF16)
        y = None
        for c in range(D_FF_E // FF_CHUNK):
            ab = jnp.dot(x, w13_s[:, 2 * FF_CHUNK * c:2 * FF_CHUNK * (c + 1)], preferred_element_type=F32)
            a = ab[:, 0:FF_CHUNK]
            hmid = (a * jax.nn.sigmoid(a) * ab[:, FF_CHUNK:]).astype(BF16)
            part = jnp.dot(hmid, w2_s[FF_CHUNK * c:FF_CHUNK * (c + 1), :], preferred_element_type=F32)
            y = part if y is None else y + part
        row = lax.broadcasted_iota(jnp.int32, (TB_EXP, 1), 0)
        y_ref[...] = jnp.where(row < valid_ref[j], y, 0.0)


def _experts(blk_map, blk_exp, blk_valid, n_used, x_rows, w1, w3, w2, layer):
    R, D = x_rows.shape
    nb = R // TB_EXP
    return pl.pallas_call(
        _experts_kernel,
        grid_spec=pltpu.PrefetchScalarGridSpec(
            num_scalar_prefetch=4,
            grid=(nb,),
            in_specs=[
                pl.BlockSpec((TB_EXP, D), lambda j, bm, be, bv, nu: (bm[j], 0)),
                pl.BlockSpec((1, 1, D, D_FF_E), lambda j, bm, be, bv, nu: (layer, be[j], 0, 0)),
                pl.BlockSpec((1, 1, D, D_FF_E), lambda j, bm, be, bv, nu: (layer, be[j], 0, 0)),
                pl.BlockSpec((1, 1, D_FF_E, D), lambda j, bm, be, bv, nu: (layer, be[j], 0, 0)),
            ],
            out_specs=pl.BlockSpec((TB_EXP, D), lambda j, bm, be, bv, nu: (bm[j], 0)),
            scratch_shapes=[pltpu.VMEM((D, 2 * D_FF_E), BF16), pltpu.VMEM((D_FF_E, D), BF16)],
        ),
        out_shape=jax.ShapeDtypeStruct((R, D), F32),
        compiler_params=_params("arbitrary"),
    )(blk_map, blk_exp, blk_valid, n_used, x_rows, w1, w3, w2)


def _combine_kernel(src_ref, nslot_ref, x1_ref, route_ref, off_ref, rows_ref, o_ref, buf, sem, *, tm):
    i = pl.program_id(0)

    @pl.when(i == 0)
    def _():
        buf[...] = jnp.zeros_like(buf)

    def slot_copy(s):
        src = pl.multiple_of(src_ref[i * RUN_SLOTS + s], RUN_ROWS)
        return pltpu.make_async_copy(rows_ref.at[pl.ds(src, RUN_ROWS)],
                                     buf.at[pl.ds(pl.multiple_of(s * RUN_ROWS, RUN_ROWS), RUN_ROWS)], sem)

    def issue(s, c):
        slot_copy(s).start()
        return c

    def wait(s, c):
        slot_copy(s).wait()
        return c

    lax.fori_loop(0, nslot_ref[i], issue, 0)
    route = route_ref[...]
    lane = lax.broadcasted_iota(jnp.int32, (tm, LANES), 1).astype(F32)
    off = off_ref[0]

    def buf_row(e_lane, r_lane):
        e = route[:, e_lane:e_lane + 1]
        base = jnp.sum(jnp.where(lane == e, off, 0.0), axis=-1, keepdims=True)
        return (base + route[:, r_lane:r_lane + 1]).astype(jnp.int32)

    col = lax.broadcasted_iota(jnp.int32, (tm, RUN_SLOTS * RUN_ROWS), 1)
    pick = (jnp.where(col == buf_row(ROUTE_E1, ROUTE_R1), route[:, ROUTE_G1:ROUTE_G1 + 1], 0.0)
            + jnp.where(col == buf_row(ROUTE_E2, ROUTE_R2), route[:, ROUTE_G2:ROUTE_G2 + 1], 0.0))
    lax.fori_loop(0, nslot_ref[i], wait, 0)
    o_ref[...] = x1_ref[...] + jnp.dot(pick.astype(BF16), buf[...].astype(BF16), preferred_element_type=F32)


def _combine(src, nslot, x1, route, off, y_rows):
    N, D = x1.shape
    tm = min(TM_MOVE, N)
    return pl.pallas_call(
        functools.partial(_combine_kernel, tm=tm),
        grid_spec=pltpu.PrefetchScalarGridSpec(
            num_scalar_prefetch=2,
            grid=(N // tm,),
            in_specs=[
                pl.BlockSpec((tm, D), lambda i, s, n: (i, 0)),
                pl.BlockSpec((tm, LANES), lambda i, s, n: (i, 0)),
                pl.BlockSpec((1, 1, LANES), lambda i, s, n: (i, 0, 0)),
                pl.BlockSpec(memory_space=pl.ANY),
            ],
            out_specs=pl.BlockSpec((tm, D), lambda i, s, n: (i, 0)),
            scratch_shapes=[pltpu.VMEM((RUN_SLOTS * RUN_ROWS, D), F32), pltpu.SemaphoreType.DMA(())],
        ),
        out_shape=jax.ShapeDtypeStruct((N, D), F32),
        compiler_params=_params("arbitrary"),
    )(src, nslot, x1, route, off, y_rows)


def _moe(x1, h2, route, route_t, counts, sub_carry, w1, w3, w2, layer):
    N, _ = x1.shape
    tm = min(TM_MOVE, N)
    n_tiles = N // tm
    experts = jnp.arange(N_EXPERTS, dtype=jnp.int32)
    nb = (2 * N + N_EXPERTS * (TB_EXP - 1)) // TB_EXP + 1
    e_id = route_t[ROUTE_E1:ROUTE_E2 + 1].astype(jnp.int32)
    rank = route_t[ROUTE_R1:ROUTE_R2 + 1].astype(jnp.int32)
    cnt = counts[0, :N_EXPERTS].astype(jnp.int32)
    padded = (cnt + TB_EXP - 1) // TB_EXP * TB_EXP
    pad_end = jnp.cumsum(padded)
    pad_start = pad_end - padded
    e_flat = e_id.reshape(1, 2 * N)
    dest = jnp.sum(jnp.where(e_flat == experts[:, None], pad_start[:, None], 0), axis=0) + rank.reshape(2 * N)
    n_used = pad_end[-1] // TB_EXP
    blk_map = jnp.minimum(jnp.arange(nb, dtype=jnp.int32), jnp.maximum(n_used - 1, 0))
    blk_exp = jnp.sum(pad_end[None, :] <= (blk_map * TB_EXP)[:, None], axis=1).astype(jnp.int32)
    blk_exp = jnp.minimum(blk_exp, N_EXPERTS - 1)
    is_blk_e = blk_exp[:, None] == experts[None, :]
    blk_valid = jnp.clip(jnp.sum(jnp.where(is_blk_e, (pad_start + cnt)[None, :], 0), axis=1) - blk_map * TB_EXP,
                         0, TB_EXP).astype(jnp.int32)

    per_router_tile = TM_OUT // tm if N >= TM_OUT else 1
    carry = sub_carry.reshape(-1, SUBLANES, LANES)[:, :per_router_tile, :N_EXPERTS].reshape(n_tiles, N_EXPERTS)
    carry = carry.astype(jnp.int32)
    tile_cnt = jnp.concatenate([carry[1:], cnt[None, :]], axis=0) - carry
    run_start = pad_start[None, :] + carry
    first_blk = run_start // RUN_ROWS
    nslot = jnp.where(tile_cnt > 0, (run_start + tile_cnt - 1) // RUN_ROWS - first_blk + 1, 0)
    slot_end = jnp.cumsum(nslot, axis=1)
    slot_base = slot_end - nslot
    slots = jnp.arange(RUN_SLOTS, dtype=jnp.int32)
    slot_e = jnp.minimum(jnp.sum(slot_end[:, None, :] <= slots[None, :, None], axis=2), N_EXPERTS - 1)
    is_slot_e = slot_e[:, :, None] == experts[None, None, :]
    pick = lambda tbl: jnp.sum(jnp.where(is_slot_e, tbl[:, None, :], 0), axis=2)
    src = (pick(first_blk) + slots[None, :] - pick(slot_base)) * RUN_ROWS
    src = jnp.clip(src, 0, nb * TB_EXP - RUN_ROWS).reshape(-1).astype(jnp.int32)
    off = (slot_base - first_blk) * RUN_ROWS + pad_start[None, :]
    off = jnp.pad(off.astype(F32), ((0, 0), (0, LANES - N_EXPERTS)))

    x_rows = _dispatch(dest, h2, nb * TB_EXP)
    y_rows = _experts(blk_map, blk_exp, blk_valid, n_used.reshape(1).astype(jnp.int32), x_rows, w1, w3, w2, layer)
    return _combine(src, slot_end[:, -1].astype(jnp.int32), x1, route, off.reshape(n_tiles, 1, LANES), y_rows)


def _pad_heads(w, width):
    k = w.shape[0]
    w = w.reshape(k, N_HEADS, width)
    return jnp.pad(w, ((0, 0), (0, 0), (0, HEAD_PAD - width))).reshape(k, N_HEADS * HEAD_PAD)


def _swap_rope(w):
    half = QK_ROPE // 2
    lo, hi = w[..., QK_NOPE:QK_NOPE + half], w[..., QK_NOPE + half:QK_HEAD]
    pad = [(0, 0)] * (w.ndim - 1)
    return jnp.pad(jnp.concatenate([hi, lo], -1), pad + [(QK_NOPE, HEAD_PAD - QK_HEAD)])


def _rope_tables(positions):
    inv_freq = ROPE_THETA ** (-jnp.arange(0, QK_ROPE, 2, dtype=F32) / QK_ROPE)
    ang = positions.astype(F32)[..., None] * inv_freq
    table = jnp.concatenate([jnp.cos(ang), jnp.sin(ang)], -1)
    return jnp.pad(table, ((0, 0), (0, 0), (QK_NOPE, LANES - QK_HEAD)))


def _ssm_params(lam_re, lam_im, b_re, b_im, c_re, c_im, log_dt):
    lam = lax.complex(lam_re, lam_im)
    dt = jnp.exp(log_dt)[:, None]
    lam_bar = jnp.exp(lam * dt)
    b_bar = ((lam_bar - 1.0) / lam)[..., None] * lax.complex(b_re, b_im)
    eye = jnp.eye(SSM_GROUPS, dtype=F32)

    def in_blockdiag(m):
        return jnp.einsum("gpc,gh->gchp", m, eye).reshape(SSM_WIDTH, SSM_FLAT)

    def out_blockdiag(m):
        return jnp.einsum("gcp,gh->gphc", m, eye).reshape(SSM_FLAT, SSM_WIDTH)

    bbd = jnp.concatenate([in_blockdiag(jnp.real(b_bar)), in_blockdiag(jnp.imag(b_bar))], axis=1)
    cbd = jnp.concatenate([out_blockdiag(c_re), out_blockdiag(-c_im)], axis=0)
    lam_rows = jnp.stack([jnp.real(lam_bar).reshape(SSM_FLAT), jnp.imag(lam_bar).reshape(SSM_FLAT)])
    return bbd.astype(BF16), lam_rows, cbd.astype(BF16)


def kernel(x, positions, mix_norm, w_in, q_a_norm, w_uq, kv_a_norm, w_ukv, q_norm, k_norm, ssm_lam_re, ssm_lam_im, ssm_b_re, ssm_b_im, ssm_c_re, ssm_c_im, ssm_d, ssm_log_dt, ssm_w_glu, conv_dw_w, conv_dw_b, conv_ln_w, conv_ln_b, out_norm, w_out, ffn_norm, w_grp, b_grp, w_exp, b_exp, w1, w3, w2):
    B, L, D = x.shape
    depth = w_in.shape[0]
    rope = _rope_tables(positions)
    row = lambda v: v.reshape(1, -1)
    lane_pad = lambda v: jnp.pad(v, (0, LANES - v.shape[0])).reshape(1, LANES)
    for l in range(depth):
        c_q, c_kv, k_pe, u_s, c_a, c_g = jnp.split(
            w_in[l], [Q_LORA, Q_LORA + KV_LORA, Q_LORA + KV_LORA + QK_ROPE,
                      Q_LORA + KV_LORA + QK_ROPE + SSM_WIDTH,
                      Q_LORA + KV_LORA + QK_ROPE + SSM_WIDTH + CONV_WIDTH], axis=1)
        k_pe_full = jnp.pad(k_pe, ((0, 0), (QK_NOPE, 0)))
        win = jnp.concatenate([c_q, c_kv, u_s, c_a, c_g, jnp.pad(k_pe_full, ((0, 0), (0, HEAD_PAD - QK_HEAD))),
                               _swap_rope(k_pe_full)], axis=1).astype(BF16)
        wkv = w_ukv[l].reshape(KV_LORA, N_HEADS, QK_NOPE + V_HEAD)
        wuk = _pad_heads(wkv[:, :, :QK_NOPE].reshape(KV_LORA, N_HEADS * QK_NOPE), QK_NOPE).astype(BF16)
        wuv = wkv[:, :, QK_NOPE:].reshape(KV_LORA, ATTN_WIDTH).astype(BF16)
        wuq_sw = _swap_rope(w_uq[l].reshape(Q_LORA, N_HEADS, QK_HEAD)).reshape(Q_LORA, N_HEADS * HEAD_PAD)
        wuq = jnp.concatenate([_pad_heads(w_uq[l], QK_HEAD), wuq_sw], axis=1).astype(BF16)
        qn_scaled = q_norm[l] * (QK_HEAD ** -0.5)
        norm_rows = lambda w: jnp.stack([jnp.pad(w, (0, HEAD_PAD - QK_HEAD)), _swap_rope(w)])
        q, k, v, u_tm, ca, cg = _in_proj(
            x, rope, row(mix_norm[l]), win, row(q_a_norm[l]), wuq, row(kv_a_norm[l]), wuk, wuv,
            norm_rows(qn_scaled), norm_rows(k_norm[l]))
        y_attn = _attention(q, k, v)

        g_out = out_norm[l]
        bbd, lam_rows, cbd = _ssm_params(ssm_lam_re[l], ssm_lam_im[l], ssm_b_re[l], ssm_b_im[l],
                                         ssm_c_re[l], ssm_c_im[l], ssm_log_dt[l])
        y_ssm_tm = _ssm(u_tm.reshape(L * B, SSM_WIDTH), bbd, lam_rows, cbd, row(ssm_d[l]),
                        ssm_w_glu[l].astype(BF16), row(g_out[ATTN_WIDTH:ATTN_WIDTH + SSM_WIDTH]), B)
        y_conv = _conv(ca, cg, conv_dw_w[l], row(conv_dw_b[l]), row(conv_ln_w[l]), row(conv_ln_b[l]),
                       row(g_out[ATTN_WIDTH + SSM_WIDTH:]))

        w_route = jnp.pad(jnp.concatenate([w_exp[l], w_grp[l]], axis=1), ((0, 0), (0, LANES - N_EXPERTS - N_EGROUPS)))
        w_route_hi = w_route.astype(BF16)
        w_route = jnp.concatenate([w_route_hi, (w_route - w_route_hi.astype(F32)).astype(BF16)], axis=1)
        b_route = lane_pad(jnp.concatenate([b_exp[l], b_grp[l]]))
        x1, h2, route, route_t, counts, sub_carry = _out_router(
            x.reshape(B * L, D), y_attn.reshape(B * L, ATTN_WIDTH), y_ssm_tm.reshape(L, B * SSM_WIDTH),
            y_conv.reshape(B * L, CONV_WIDTH), row(g_out[:ATTN_WIDTH]), w_out[l].astype(BF16),
            row(ffn_norm[l]), w_route, b_route, B)
        x = _moe(x1, h2, route, route_t, counts, sub_carry, w1, w3, w2, l).reshape(B, L, D)
    return x
```

```python
import functools
import math

import jax
import jax.numpy as jnp
from jax import lax
from jax.experimental import pallas as pl
from jax.experimental.pallas import tpu as pltpu

D_MODEL = 1024
CHUNK = 64
EPS = 1e-6
N_HEADS = 8
QK_NOPE = 64
QK_ROPE = 32
QK_HEAD = QK_NOPE + QK_ROPE
V_HEAD = 64
Q_LORA = 256
KV_LORA = 128
ROPE_THETA = 10000.0
ATTN_WIDTH = N_HEADS * V_HEAD
SSM_WIDTH = 256
SSM_GROUP = 16
SSM_GROUPS = SSM_WIDTH // SSM_GROUP
SSM_STATE = 64
SSM_FLAT = SSM_GROUPS * SSM_STATE
CONV_WIDTH = 256
CONV_K = 31
N_EGROUPS = 4
EXP_PER_GROUP = 8
N_EXPERTS = N_EGROUPS * EXP_PER_GROUP
D_FF_E = 512

LANES = 128
SUBLANES = 8
HEAD_PAD = LANES
IN_PROJ_PAD = Q_LORA + KV_LORA + SSM_WIDTH + 2 * CONV_WIDTH + 2 * LANES
CONV_HALO = 32
VMEM_LIMIT = 48 * 1024 * 1024

TM_PROJ = 512
TQ_ATTN = 256
TC_SSM = 64
T_CONV = 256
CONV_SUB = 64
TM_OUT = 512
TM_MOVE = 256
RUN_ROWS = SUBLANES
RUN_SLOTS = 2 * N_EXPERTS + 2 * TM_MOVE // RUN_ROWS
DMA_UNROLL = 8
TB_EXP = 512
FF_CHUNK = 256

BF16 = jnp.bfloat16
F32 = jnp.float32


def _rms(x, w):
    return x * lax.rsqrt(jnp.mean(x * x, axis=-1, keepdims=True) + EPS) * w


def _params(*sem):
    return pltpu.CompilerParams(dimension_semantics=sem, vmem_limit_bytes=VMEM_LIMIT)


def _in_proj_kernel(x_ref, rope_ref, mixn_ref, win_ref, qan_ref, wuq_ref, kvan_ref, wuk_ref, wuv_ref,
                    qn_ref, kn_ref, q_ref, k_ref, v_ref, u_ref, ca_ref, cg_ref):
    x = x_ref[0]
    h = _rms(x, mixn_ref[...]).astype(BF16)
    proj = jnp.dot(h, win_ref[...], preferred_element_type=F32)
    o = 0
    c_q = proj[:, o:o + Q_LORA]; o += Q_LORA
    c_kv = proj[:, o:o + KV_LORA]; o += KV_LORA
    u_ref[...] = proj[:, o:o + SSM_WIDTH].astype(BF16); o += SSM_WIDTH
    ca_ref[0] = proj[:, o:o + CONV_WIDTH].astype(BF16); o += CONV_WIDTH
    cg_ref[0] = proj[:, o:o + CONV_WIDTH].astype(BF16); o += CONV_WIDTH
    k_pe = proj[:, o:o + LANES]; o += LANES
    k_pe_sw = proj[:, o:o + LANES]

    width = N_HEADS * HEAD_PAD
    q2 = jnp.dot(_rms(c_q, qan_ref[...]).astype(BF16), wuq_ref[...], preferred_element_type=F32)
    q, q_sw = q2[:, 0:width], q2[:, width:2 * width]
    ckv_n = _rms(c_kv, kvan_ref[...]).astype(BF16)
    kn = jnp.dot(ckv_n, wuk_ref[...], preferred_element_type=F32)
    v_ref[0] = jnp.dot(ckv_n, wuv_ref[...], preferred_element_type=F32).astype(BF16)

    rope = rope_ref[0]
    half = QK_ROPE // 2
    lane = lax.broadcasted_iota(jnp.int32, rope.shape, 1)
    lo = (lane >= QK_NOPE) & (lane < QK_NOPE + half)
    hi = (lane >= QK_NOPE + half) & (lane < QK_HEAD)
    cos_t = jnp.where(lane < QK_NOPE, 1.0, jnp.where(lo, rope, jnp.where(hi, pltpu.roll(rope, half, 1), 0.0)))
    sin_t = jnp.where(lo, -pltpu.roll(rope, LANES - half, 1), jnp.where(hi, rope, 0.0))
    a_q, b_q = qn_ref[0:1, :] * cos_t, qn_ref[1:2, :] * sin_t
    a_k = kn_ref[0:1, :] * cos_t
    k_sw_term = k_pe_sw * (kn_ref[1:2, :] * sin_t)

    def inv_rms(y):
        return lax.rsqrt(jnp.sum(y * y, axis=-1, keepdims=True) * (1.0 / QK_HEAD) + EPS)

    for hd in range(N_HEADS):
        sl = slice(hd * HEAD_PAD, (hd + 1) * HEAD_PAD)
        yq = q[:, sl]
        q_ref[0, hd] = ((yq * a_q + q_sw[:, sl] * b_q) * inv_rms(yq)).astype(BF16)
        yk = kn[:, sl] + k_pe
        k_ref[0, hd] = ((yk * a_k + k_sw_term) * inv_rms(yk)).astype(BF16)


def _in_proj(x, rope, mixn, win, qan, wuq, kvan, wuk, wuv, qn, kn):
    B, L, D = x.shape
    tm = min(TM_PROJ, L)
    full = lambda shape: pl.BlockSpec(shape, lambda b, t: (0,) * len(shape))
    return pl.pallas_call(
        _in_proj_kernel,
        grid=(B, L // tm),
        in_specs=[
            pl.BlockSpec((1, tm, D), lambda b, t: (b, t, 0)),
            pl.BlockSpec((1, tm, LANES), lambda b, t: (b, t, 0)),
            full((1, D)), full((D, IN_PROJ_PAD)), full((1, Q_LORA)), full((Q_LORA, 2 * N_HEADS * HEAD_PAD)),
            full((1, KV_LORA)), full((KV_LORA, N_HEADS * HEAD_PAD)), full((KV_LORA, ATTN_WIDTH)),
            full((2, HEAD_PAD)), full((2, HEAD_PAD)),
        ],
        out_specs=[
            pl.BlockSpec((1, N_HEADS, tm, HEAD_PAD), lambda b, t: (b, 0, t, 0)),
            pl.BlockSpec((1, N_HEADS, tm, HEAD_PAD), lambda b, t: (b, 0, t, 0)),
            pl.BlockSpec((1, tm, ATTN_WIDTH), lambda b, t: (b, t, 0)),
            pl.BlockSpec((tm, SSM_WIDTH), lambda b, t: (t, b)),
            pl.BlockSpec((1, tm, CONV_WIDTH), lambda b, t: (b, t, 0)),
            pl.BlockSpec((1, tm, CONV_WIDTH), lambda b, t: (b, t, 0)),
        ],
        out_shape=[
            jax.ShapeDtypeStruct((B, N_HEADS, L, HEAD_PAD), BF16),
            jax.ShapeDtypeStruct((B, N_HEADS, L, HEAD_PAD), BF16),
            jax.ShapeDtypeStruct((B, L, ATTN_WIDTH), BF16),
            jax.ShapeDtypeStruct((L, B * SSM_WIDTH), BF16),
            jax.ShapeDtypeStruct((B, L, CONV_WIDTH), BF16),
            jax.ShapeDtypeStruct((B, L, CONV_WIDTH), BF16),
        ],
        compiler_params=_params("parallel", "parallel"),
    )(x, rope, mixn, win, qan, wuq, kvan, wuk, wuv, qn, kn)


def _attention_kernel(q_ref, k_ref, v_ref, o_ref, vext_ref, *, seq, tq):
    pair = 2 * V_HEAD
    vext_ref[:, 0:pair] = v_ref[0]
    vext_ref[:, pair:] = jnp.ones((seq, LANES), BF16)
    row_chunk = lax.broadcasted_iota(jnp.int32, (tq, tq), 0) // CHUNK
    col_chunk = lax.broadcasted_iota(jnp.int32, (tq, tq), 1) // CHUNK
    visible = col_chunk <= row_chunk
    neg = jnp.finfo(F32).min
    nt = (((1,), (1,)), ((), ()))
    low_lanes = lax.broadcasted_iota(jnp.int32, (tq, pair), 1) < V_HEAD
    for i in range(seq // tq):
        q0 = i * tq
        outs = []
        for hh in range(2):
            qb = q_ref[0, hh, q0:q0 + tq, :]
            s_d = lax.dot_general(qb, k_ref[0, hh, q0:q0 + tq, :], nt, preferred_element_type=F32)
            s_d = jnp.where(visible, s_d, neg)
            m = jnp.max(s_d, axis=-1, keepdims=True)
            if i > 0:
                s_l = lax.dot_general(qb, k_ref[0, hh, 0:q0, :], nt, preferred_element_type=F32)
                m = jnp.maximum(m, jnp.max(s_l, axis=-1, keepdims=True))
            acc = jnp.dot(jnp.exp(s_d - m).astype(BF16), vext_ref[q0:q0 + tq, :], preferred_element_type=F32)
            if i > 0:
                acc = acc + jnp.dot(jnp.exp(s_l - m).astype(BF16), vext_ref[0:q0, :], preferred_element_type=F32)
            outs.append(acc[:, 0:pair] / acc[:, pair:])
        o_ref[0, q0:q0 + tq, :] = jnp.where(low_lanes, outs[0], outs[1]).astype(BF16)


def _attention(q, k, v):
    B, H, L, _ = q.shape
    tq = min(TQ_ATTN, L)
    return pl.pallas_call(
        functools.partial(_attention_kernel, seq=L, tq=tq),
        grid=(B, H // 2),
        in_specs=[
            pl.BlockSpec((1, 2, L, HEAD_PAD), lambda b, p: (b, p, 0, 0)),
            pl.BlockSpec((1, 2, L, HEAD_PAD), lambda b, p: (b, p, 0, 0)),
            pl.BlockSpec((1, L, 2 * V_HEAD), lambda b, p: (b, 0, p)),
        ],
        out_specs=pl.BlockSpec((1, L, 2 * V_HEAD), lambda b, p: (b, 0, p)),
        out_shape=jax.ShapeDtypeStruct((B, L, ATTN_WIDTH), BF16),
        scratch_shapes=[pltpu.VMEM((L, 2 * V_HEAD + LANES), BF16)],
        compiler_params=_params("parallel", "parallel"),
    )(q, k, v)


def _ssm_kernel(u_ref, bbd_ref, lam_ref, cbd_ref, d_ref, wglu_ref, g_ref, o_ref, state_ref, bu_ref, *, batch, tc):
    @pl.when(pl.program_id(0) == 0)
    def _():
        state_ref[...] = jnp.zeros_like(state_ref)

    u = u_ref[...]
    bu_ref[...] = jnp.dot(u, bbd_ref[...], preferred_element_type=F32)
    lam_re = jnp.broadcast_to(lam_ref[0:1, :], (batch, SSM_FLAT))
    lam_im = jnp.broadcast_to(lam_ref[1:2, :], (batch, SSM_FLAT))

    def step(t, carry):
        xr, xi = carry
        rows = pl.ds(pl.multiple_of(t * batch, batch), batch)
        nr = lam_re * xr - lam_im * xi + bu_ref[rows, 0:SSM_FLAT]
        ni = lam_re * xi + lam_im * xr + bu_ref[rows, SSM_FLAT:2 * SSM_FLAT]
        bu_ref[rows, 0:SSM_FLAT] = nr
        bu_ref[rows, SSM_FLAT:2 * SSM_FLAT] = ni
        return nr, ni

    xr, xi = lax.fori_loop(0, tc, step, (state_ref[:, 0:SSM_FLAT], state_ref[:, SSM_FLAT:2 * SSM_FLAT]),
                           unroll=2)
    state_ref[:, 0:SSM_FLAT] = xr
    state_ref[:, SSM_FLAT:2 * SSM_FLAT] = xi

    y = jnp.dot(bu_ref[...].astype(BF16), cbd_ref[...], preferred_element_type=F32)
    y = y + d_ref[...] * u.astype(F32)
    z = jax.nn.gelu(y)
    gate = jax.nn.sigmoid(jnp.dot(z.astype(BF16), wglu_ref[...], preferred_element_type=F32))
    o_ref[...] = _rms(z * gate, g_ref[...]).astype(BF16)


def _ssm(u_tm, bbd, lam, cbd, d, wglu, g, batch):
    rows = u_tm.shape[0]
    seq = rows // batch
    tc = min(TC_SSM, seq)
    blk = tc * batch
    full = lambda shape: pl.BlockSpec(shape, lambda t: (0,) * len(shape))
    return pl.pallas_call(
        functools.partial(_ssm_kernel, batch=batch, tc=tc),
        grid=(seq // tc,),
        in_specs=[
            pl.BlockSpec((blk, SSM_WIDTH), lambda t: (t, 0)),
            full((SSM_WIDTH, 2 * SSM_FLAT)), full((2, SSM_FLAT)), full((2 * SSM_FLAT, SSM_WIDTH)),
            full((1, SSM_WIDTH)), full((SSM_WIDTH, SSM_WIDTH)), full((1, SSM_WIDTH)),
        ],
        out_specs=pl.BlockSpec((blk, SSM_WIDTH), lambda t: (t, 0)),
        out_shape=jax.ShapeDtypeStruct((rows, SSM_WIDTH), BF16),
        scratch_shapes=[pltpu.VMEM((batch, 2 * SSM_FLAT), F32), pltpu.VMEM((blk, 2 * SSM_FLAT), F32)],
        compiler_params=_params("arbitrary"),
    )(u_tm, bbd, lam, cbd, d, wglu, g)


def _conv_kernel(a_ref, g_ref, ah_ref, gh_ref, w_ref, b_ref, lnw_ref, lnb_ref, gn_ref, o_ref, upad_ref, *, tt):
    first = pl.program_id(1) == 0
    halo = ah_ref[0].astype(F32) * jax.nn.sigmoid(gh_ref[0].astype(F32))
    upad_ref[0:CONV_HALO, :] = jnp.where(first, 0.0, halo)
    upad_ref[CONV_HALO:, :] = a_ref[0].astype(F32) * jax.nn.sigmoid(g_ref[0].astype(F32))
    base = CONV_HALO - (CONV_K - 1)
    for s in range(tt // CONV_SUB):
        r0 = s * CONV_SUB
        acc = jnp.zeros((CONV_SUB, CONV_WIDTH), F32)
        for off in range(SUBLANES):
            n = CONV_SUB + (SUBLANES if off else 0)
            part = None
            for a8 in range(0, CONV_HALO + 1, SUBLANES):
                kk = a8 + off - base
                if 0 <= kk < CONV_K:
                    term = w_ref[kk:kk + 1, :] * upad_ref[r0 + a8:r0 + a8 + n, :]
                    part = term if part is None else part + term
            acc = acc + part[off:off + CONV_SUB]
        y = acc + b_ref[...]
        mu = jnp.mean(y, axis=-1, keepdims=True)
        var = jnp.mean(jnp.square(y - mu), axis=-1, keepdims=True)
        y = (y - mu) * lax.rsqrt(var + 1e-5) * lnw_ref[...] + lnb_ref[...]
        y = y * jax.nn.sigmoid(y)
        o_ref[0, r0:r0 + CONV_SUB, :] = _rms(y, gn_ref[...]).astype(BF16)


def _conv(a, g, w, b, lnw, lnb, gn):
    B, L, C = a.shape
    tt = min(T_CONV, L)
    ratio = tt // CONV_HALO
    full = lambda shape: pl.BlockSpec(shape, lambda bb, t: (0,) * len(shape))
    cur = pl.BlockSpec((1, tt, C), lambda bb, t: (bb, t, 0))
    prev = pl.BlockSpec((1, CONV_HALO, C), lambda bb, t: (bb, jnp.maximum(t * ratio - 1, 0), 0))
    return pl.pallas_call(
        functools.partial(_conv_kernel, tt=tt),
        grid=(B, L // tt),
        in_specs=[cur, cur, prev, prev, full((CONV_K, C)), full((1, C)), full((1, C)), full((1, C)), full((1, C))],
        out_specs=pl.BlockSpec((1, tt, C), lambda bb, t: (bb, t, 0)),
        out_shape=jax.ShapeDtypeStruct((B, L, C), BF16),
        scratch_shapes=[pltpu.VMEM((tt + CONV_HALO, C), F32)],
        compiler_params=_params("parallel", "parallel"),
    )(a, g, a, g, w, b, lnw, lnb, gn)


ROUTE_E1, ROUTE_E2, ROUTE_R1, ROUTE_R2, ROUTE_G1, ROUTE_G2 = range(6)
ROUTE_ROWS = 8
GRP_LANE0 = N_EXPERTS


def _out_router_kernel(x_ref, ya_ref, ys_ref, yc_ref, ga_ref, wo_ref, fn_ref, wr_ref, br_ref,
                       x1_ref, h2_ref, route_ref, route_t_ref, cnt_ref, sub_ref, carry_ref, *, tm):
    @pl.when(pl.program_id(0) == 0)
    def _():
        carry_ref[...] = jnp.zeros_like(carry_ref)

    ya = _rms(ya_ref[...].astype(F32), ga_ref[...]).astype(BF16)
    acc = jnp.dot(ya, wo_ref[0:ATTN_WIDTH, :], preferred_element_type=F32)
    acc += jnp.dot(ys_ref[...], wo_ref[ATTN_WIDTH:ATTN_WIDTH + SSM_WIDTH, :], preferred_element_type=F32)
    acc += jnp.dot(yc_ref[...], wo_ref[ATTN_WIDTH + SSM_WIDTH:, :], preferred_element_type=F32)
    x1 = x_ref[...] + acc
    x1_ref[...] = x1
    h2 = _rms(x1, fn_ref[...])
    h2_ref[...] = h2

    h_hi = h2.astype(BF16)
    h_lo = (h2 - h_hi.astype(F32)).astype(BF16)
    part = jnp.dot(h_hi, wr_ref[...], preferred_element_type=F32)
    logits = (part[:, 0:LANES] + part[:, LANES:2 * LANES]
              + jnp.dot(h_lo, wr_ref[:, 0:LANES], preferred_element_type=F32) + br_ref[...])
    lane = lax.broadcasted_iota(jnp.int32, (tm, LANES), 1)
    ninf = -jnp.inf
    big = LANES

    def first_argmax(vals, vmax):
        return jnp.min(jnp.where(vals == vmax, lane, big), axis=-1, keepdims=True)

    grp = jnp.where((lane >= GRP_LANE0) & (lane < GRP_LANE0 + N_EGROUPS), logits, ninf)
    gmax = jnp.max(grp, axis=-1, keepdims=True)
    gsel = first_argmax(grp, gmax) - GRP_LANE0
    p_grp = 1.0 / jnp.sum(jnp.exp(grp - gmax), axis=-1, keepdims=True)

    el = jnp.where((lane < N_EXPERTS) & ((lane // EXP_PER_GROUP) == gsel), logits, ninf)
    m1 = jnp.max(el, axis=-1, keepdims=True)
    e1 = first_argmax(el, m1)
    el2 = jnp.where(lane == e1, ninf, el)
    m2 = jnp.max(el2, axis=-1, keepdims=True)
    e2 = first_argmax(el2, m2)
    t2 = jnp.exp(m2 - m1)
    g1 = p_grp / (1.0 + t2)
    g2 = p_grp * t2 / (1.0 + t2)

    hit1 = lane == e1
    hit2 = lane == e2
    cnt = (hit1 | hit2).astype(F32)
    rr = lax.broadcasted_iota(jnp.int32, (tm, tm), 0)
    cc = lax.broadcasted_iota(jnp.int32, (tm, tm), 1)
    tri = (cc < rr).astype(BF16)
    before = jnp.dot(tri, cnt.astype(BF16), preferred_element_type=F32) + carry_ref[...]
    r1 = jnp.sum(jnp.where(hit1, before, 0.0), axis=-1, keepdims=True)
    r2 = jnp.sum(jnp.where(hit2, before, 0.0), axis=-1, keepdims=True)
    subs = [before[h * TM_MOVE:h * TM_MOVE + 1, :] for h in range(tm // TM_MOVE)]
    sub_ref[...] = jnp.concatenate(subs + [jnp.zeros((SUBLANES - len(subs), LANES), F32)], axis=0)
    carry_ref[...] += jnp.sum(cnt, axis=0, keepdims=True)
    cnt_ref[...] = carry_ref[...]

    rec = jnp.where(lane == ROUTE_E1, e1.astype(F32), 0.0)
    rec = jnp.where(lane == ROUTE_E2, e2.astype(F32), rec)
    rec = jnp.where(lane == ROUTE_R1, r1, rec)
    rec = jnp.where(lane == ROUTE_R2, r2, rec)
    rec = jnp.where(lane == ROUTE_G1, g1, rec)
    rec = jnp.where(lane == ROUTE_G2, g2, rec)
    route_ref[...] = rec
    route_t_ref[...] = rec.T[0:ROUTE_ROWS, :]


def _out_router(x, ya, ys_tm, yc, ga, wo, fn, wr, br, batch):
    N, D = x.shape
    seq = N // batch
    tm = min(TM_OUT, seq)
    per_b = seq // tm
    full = lambda shape: pl.BlockSpec(shape, lambda i: (0,) * len(shape))
    tile = lambda w: pl.BlockSpec((tm, w), lambda i: (i, 0))
    return pl.pallas_call(
        functools.partial(_out_router_kernel, tm=tm),
        grid=(N // tm,),
        in_specs=[
            tile(D), tile(ATTN_WIDTH),
            pl.BlockSpec((tm, SSM_WIDTH), lambda i: (i % per_b, i // per_b)),
            tile(CONV_WIDTH),
            full((1, ATTN_WIDTH)), full((D, D)), full((1, D)), full((D, 2 * LANES)), full((1, LANES)),
        ],
        out_specs=[tile(D), tile(D), tile(LANES), pl.BlockSpec((ROUTE_ROWS, tm), lambda i: (0, i)), full((1, LANES)),
                   pl.BlockSpec((SUBLANES, LANES), lambda i: (i, 0))],
        out_shape=[
            jax.ShapeDtypeStruct((N, D), F32),
            jax.ShapeDtypeStruct((N, D), F32),
            jax.ShapeDtypeStruct((N, LANES), F32),
            jax.ShapeDtypeStruct((ROUTE_ROWS, N), F32),
            jax.ShapeDtypeStruct((1, LANES), F32),
            jax.ShapeDtypeStruct((N // tm * SUBLANES, LANES), F32),
        ],
        scratch_shapes=[pltpu.VMEM((1, LANES), F32)],
        compiler_params=_params("arbitrary"),
    )(x, ya, ys_tm, yc, ga, wo, fn, wr, br)


def _dispatch_kernel(dest_ref, h_ref, rows_ref, sem, *, tm, n_tok):
    base = pl.program_id(0) * tm

    def issue(r, c):
        src = h_ref.at[pl.ds(r, 1)]
        pltpu.make_async_copy(src, rows_ref.at[pl.ds(dest_ref[base + r], 1)], sem).start()
        pltpu.make_async_copy(src, rows_ref.at[pl.ds(dest_ref[n_tok + base + r], 1)], sem).start()
        return c

    lax.fori_loop(0, tm, issue, 0, unroll=DMA_UNROLL)
    for _ in range(2):
        pltpu.make_async_copy(h_ref, rows_ref.at[pl.ds(0, tm)], sem).wait()


def _dispatch(dest_flat, h2, n_rows):
    N, D = h2.shape
    tm = min(TM_MOVE, N)
    return pl.pallas_call(
        functools.partial(_dispatch_kernel, tm=tm, n_tok=N),
        grid_spec=pltpu.PrefetchScalarGridSpec(
            num_scalar_prefetch=1,
            grid=(N // tm,),
            in_specs=[pl.BlockSpec((tm, D), lambda i, d: (i, 0))],
            out_specs=pl.BlockSpec(memory_space=pl.ANY),
            scratch_shapes=[pltpu.SemaphoreType.DMA(())],
        ),
        out_shape=jax.ShapeDtypeStruct((n_rows, D), F32),
        compiler_params=_params("arbitrary"),
    )(dest_flat, h2)


def _experts_kernel(blk_ref, exp_ref, valid_ref, used_ref, x_ref, w1_ref, w3_ref, w2_ref, y_ref, w13_s, w2_s):
    j = pl.program_id(0)

    @pl.when(j < used_ref[0])
    def _():
        prev = exp_ref[jnp.maximum(j - 1, 0)]

        @pl.when((j == 0) | (exp_ref[j] != prev))
        def _():
            for c in range(D_FF_E // FF_CHUNK):
                w13_s[:, 2 * FF_CHUNK * c:2 * FF_CHUNK * c + FF_CHUNK] = \
                    w1_ref[0, 0, :, FF_CHUNK * c:FF_CHUNK * (c + 1)].astype(BF16)
                w13_s[:, 2 * FF_CHUNK * c + FF_CHUNK:2 * FF_CHUNK * (c + 1)] = \
                    w3_ref[0, 0, :, FF_CHUNK * c:FF_CHUNK * (c + 1)].astype(BF16)
            w2_s[...] = w2_ref[0, 0].astype(BF16)

        x = x_ref[...].astype(BF16)
        y = None
        for c in range(D_FF_E // FF_CHUNK):
            ab = jnp.dot(x, w13_s[:, 2 * FF_CHUNK * c:2 * FF_CHUNK * (c + 1)], preferred_element_type=F32)
            a = ab[:, 0:FF_CHUNK]
            hmid = (a * jax.nn.sigmoid(a) * ab[:, FF_CHUNK:]).astype(BF16)
            part = jnp.dot(hmid, w2_s[FF_CHUNK * c:FF_CHUNK * (c + 1), :], preferred_element_type=F32)
            y = part if y is None else y + part
        row = lax.broadcasted_iota(jnp.int32, (TB_EXP, 1), 0)
        y_ref[...] = jnp.where(row < valid_ref[j], y, 0.0)


def _experts(blk_map, blk_exp, blk_valid, n_used, x_rows, w1, w3, w2, layer):
    R, D = x_rows.shape
    nb = R // TB_EXP
    return pl.pallas_call(
        _experts_kernel,
        grid_spec=pltpu.PrefetchScalarGridSpec(
            num_scalar_prefetch=4,
            grid=(nb,),
            in_specs=[
                pl.BlockSpec((TB_EXP, D), lambda j, bm, be, bv, nu: (bm[j], 0)),
                pl.BlockSpec((1, 1, D, D_FF_E), lambda j, bm, be, bv, nu: (layer, be[j], 0, 0)),
                pl.BlockSpec((1, 1, D, D_FF_E), lambda j, bm, be, bv, nu: (layer, be[j], 0, 0)),
                pl.BlockSpec((1, 1, D_FF_E, D), lambda j, bm, be, bv, nu: (layer, be[j], 0, 0)),
            ],
            out_specs=pl.BlockSpec((TB_EXP, D), lambda j, bm, be, bv, nu: (bm[j], 0)),
            scratch_shapes=[pltpu.VMEM((D, 2 * D_FF_E), BF16), pltpu.VMEM((D_FF_E, D), BF16)],
        ),
        out_shape=jax.ShapeDtypeStruct((R, D), F32),
        compiler_params=_params("arbitrary"),
    )(blk_map, blk_exp, blk_valid, n_used, x_rows, w1, w3, w2)


def _combine_kernel(src_ref, nslot_ref, x1_ref, route_ref, off_ref, rows_ref, o_ref, buf, sem, *, tm):
    i = pl.program_id(0)
    cur = i % 2

    def slot_copy(tile, half, s):
        src = pl.multiple_of(src_ref[tile * RUN_SLOTS + s], RUN_ROWS)
        dst = buf.at[half, pl.ds(pl.multiple_of(s * RUN_ROWS, RUN_ROWS), RUN_ROWS)]
        return pltpu.make_async_copy(rows_ref.at[pl.ds(src, RUN_ROWS)], dst, sem.at[half])

    def fetch(tile, half):
        def issue(s, c):
            slot_copy(tile, half, s).start()
            return c
        lax.fori_loop(0, nslot_ref[tile], issue, 0)

    def wait(s, c):
        slot_copy(i, cur, s).wait()
        return c

    @pl.when(i == 0)
    def _():
        buf[...] = jnp.zeros_like(buf)
        fetch(0, 0)

    @pl.when(i + 1 < pl.num_programs(0))
    def _():
        fetch(i + 1, 1 - cur)

    route = route_ref[...]
    lane = lax.broadcasted_iota(jnp.int32, (tm, LANES), 1).astype(F32)
    off = off_ref[0]

    def buf_row(e_lane, r_lane):
        e = route[:, e_lane:e_lane + 1]
        base = jnp.sum(jnp.where(lane == e, off, 0.0), axis=-1, keepdims=True)
        return (base + route[:, r_lane:r_lane + 1]).astype(jnp.int32)

    col = lax.broadcasted_iota(jnp.int32, (tm, RUN_SLOTS * RUN_ROWS), 1)
    pick = (jnp.where(col == buf_row(ROUTE_E1, ROUTE_R1), route[:, ROUTE_G1:ROUTE_G1 + 1], 0.0)
            + jnp.where(col == buf_row(ROUTE_E2, ROUTE_R2), route[:, ROUTE_G2:ROUTE_G2 + 1], 0.0))
    lax.fori_loop(0, nslot_ref[i], wait, 0)
    o_ref[...] = x1_ref[...] + jnp.dot(pick.astype(BF16), buf[cur].astype(BF16), preferred_element_type=F32)


def _combine(src, nslot, x1, route, off, y_rows):
    N, D = x1.shape
    tm = min(TM_MOVE, N)
    return pl.pallas_call(
        functools.partial(_combine_kernel, tm=tm),
        grid_spec=pltpu.PrefetchScalarGridSpec(
            num_scalar_prefetch=2,
            grid=(N // tm,),
            in_specs=[
                pl.BlockSpec((tm, D), lambda i, s, n: (i, 0)),
                pl.BlockSpec((tm, LANES), lambda i, s, n: (i, 0)),
                pl.BlockSpec((1, 1, LANES), lambda i, s, n: (i, 0, 0)),
                pl.BlockSpec(memory_space=pl.ANY),
            ],
            out_specs=pl.BlockSpec((tm, D), lambda i, s, n: (i, 0)),
            scratch_shapes=[pltpu.VMEM((2, RUN_SLOTS * RUN_ROWS, D), F32), pltpu.SemaphoreType.DMA((2,))],
        ),
        out_shape=jax.ShapeDtypeStruct((N, D), F32),
        compiler_params=_params("arbitrary"),
    )(src, nslot, x1, route, off, y_rows)


def _moe(x1, h2, route, route_t, counts, sub_carry, w1, w3, w2, layer):
    N, _ = x1.shape
    tm = min(TM_MOVE, N)
    n_tiles = N // tm
    experts = jnp.arange(N_EXPERTS, dtype=jnp.int32)
    nb = (2 * N + N_EXPERTS * (TB_EXP - 1)) // TB_EXP + 1
    e_id = route_t[ROUTE_E1:ROUTE_E2 + 1].astype(jnp.int32)
    rank = route_t[ROUTE_R1:ROUTE_R2 + 1].astype(jnp.int32)
    cnt = counts[0, :N_EXPERTS].astype(jnp.int32)
    padded = (cnt + TB_EXP - 1) // TB_EXP * TB_EXP
    pad_end = jnp.cumsum(padded)
    pad_start = pad_end - padded
    e_flat = e_id.reshape(1, 2 * N)
    dest = jnp.sum(jnp.where(e_flat == experts[:, None], pad_start[:, None], 0), axis=0) + rank.reshape(2 * N)
    n_used = pad_end[-1] // TB_EXP
    blk_map = jnp.minimum(jnp.arange(nb, dtype=jnp.int32), jnp.maximum(n_used - 1, 0))
    blk_exp = jnp.sum(pad_end[None, :] <= (blk_map * TB_EXP)[:, None], axis=1).astype(jnp.int32)
    blk_exp = jnp.minimum(blk_exp, N_EXPERTS - 1)
    is_blk_e = blk_exp[:, None] == experts[None, :]
    blk_valid = jnp.clip(jnp.sum(jnp.where(is_blk_e, (pad_start + cnt)[None, :], 0), axis=1) - blk_map * TB_EXP,
                         0, TB_EXP).astype(jnp.int32)

    per_router_tile = TM_OUT // tm if N >= TM_OUT else 1
    carry = sub_carry.reshape(-1, SUBLANES, LANES)[:, :per_router_tile, :N_EXPERTS].reshape(n_tiles, N_EXPERTS)
    carry = carry.astype(jnp.int32)
    tile_cnt = jnp.concatenate([carry[1:], cnt[None, :]], axis=0) - carry
    run_start = pad_start[None, :] + carry
    first_blk = run_start // RUN_ROWS
    nslot = jnp.where(tile_cnt > 0, (run_start + tile_cnt - 1) // RUN_ROWS - first_blk + 1, 0)
    slot_end = jnp.cumsum(nslot, axis=1)
    slot_base = slot_end - nslot
    slots = jnp.arange(RUN_SLOTS, dtype=jnp.int32)
    slot_e = jnp.minimum(jnp.sum(slot_end[:, None, :] <= slots[None, :, None], axis=2), N_EXPERTS - 1)
    is_slot_e = slot_e[:, :, None] == experts[None, None, :]
    pick = lambda tbl: jnp.sum(jnp.where(is_slot_e, tbl[:, None, :], 0), axis=2)
    src = (pick(first_blk) + slots[None, :] - pick(slot_base)) * RUN_ROWS
    src = jnp.clip(src, 0, nb * TB_EXP - RUN_ROWS).reshape(-1).astype(jnp.int32)
    off = (slot_base - first_blk) * RUN_ROWS + pad_start[None, :]
    off = jnp.pad(off.astype(F32), ((0, 0), (0, LANES - N_EXPERTS)))

    x_rows = _dispatch(dest, h2, nb * TB_EXP)
    y_rows = _experts(blk_map, blk_exp, blk_valid, n_used.reshape(1).astype(jnp.int32), x_rows, w1, w3, w2, layer)
    return _combine(src, slot_end[:, -1].astype(jnp.int32), x1, route, off.reshape(n_tiles, 1, LANES), y_rows)


def _pad_heads(w, width):
    k = w.shape[0]
    w = w.reshape(k, N_HEADS, width)
    return jnp.pad(w, ((0, 0), (0, 0), (0, HEAD_PAD - width))).reshape(k, N_HEADS * HEAD_PAD)


def _swap_rope(w):
    half = QK_ROPE // 2
    lo, hi = w[..., QK_NOPE:QK_NOPE + half], w[..., QK_NOPE + half:QK_HEAD]
    pad = [(0, 0)] * (w.ndim - 1)
    return jnp.pad(jnp.concatenate([hi, lo], -1), pad + [(QK_NOPE, HEAD_PAD - QK_HEAD)])


def _rope_tables(positions):
    inv_freq = ROPE_THETA ** (-jnp.arange(0, QK_ROPE, 2, dtype=F32) / QK_ROPE)
    ang = positions.astype(F32)[..., None] * inv_freq
    table = jnp.concatenate([jnp.cos(ang), jnp.sin(ang)], -1)
    return jnp.pad(table, ((0, 0), (0, 0), (QK_NOPE, LANES - QK_HEAD)))


def _ssm_params(lam_re, lam_im, b_re, b_im, c_re, c_im, log_dt):
    lam = lax.complex(lam_re, lam_im)
    dt = jnp.exp(log_dt)[:, None]
    lam_bar = jnp.exp(lam * dt)
    b_bar = ((lam_bar - 1.0) / lam)[..., None] * lax.complex(b_re, b_im)
    eye = jnp.eye(SSM_GROUPS, dtype=F32)

    def in_blockdiag(m):
        return jnp.einsum("gpc,gh->gchp", m, eye).reshape(SSM_WIDTH, SSM_FLAT)

    def out_blockdiag(m):
        return jnp.einsum("gcp,gh->gphc", m, eye).reshape(SSM_FLAT, SSM_WIDTH)

    bbd = jnp.concatenate([in_blockdiag(jnp.real(b_bar)), in_blockdiag(jnp.imag(b_bar))], axis=1)
    cbd = jnp.concatenate([out_blockdiag(c_re), out_blockdiag(-c_im)], axis=0)
    lam_rows = jnp.stack([jnp.real(lam_bar).reshape(SSM_FLAT), jnp.imag(lam_bar).reshape(SSM_FLAT)])
    return bbd.astype(BF16), lam_rows, cbd.astype(BF16)


def kernel(x, positions, mix_norm, w_in, q_a_norm, w_uq, kv_a_norm, w_ukv, q_norm, k_norm, ssm_lam_re, ssm_lam_im, ssm_b_re, ssm_b_im, ssm_c_re, ssm_c_im, ssm_d, ssm_log_dt, ssm_w_glu, conv_dw_w, conv_dw_b, conv_ln_w, conv_ln_b, out_norm, w_out, ffn_norm, w_grp, b_grp, w_exp, b_exp, w1, w3, w2):
    B, L, D = x.shape
    depth = w_in.shape[0]
    rope = _rope_tables(positions)
    row = lambda v: v.reshape(1, -1)
    lane_pad = lambda v: jnp.pad(v, (0, LANES - v.shape[0])).reshape(1, LANES)
    for l in range(depth):
        c_q, c_kv, k_pe, u_s, c_a, c_g = jnp.split(
            w_in[l], [Q_LORA, Q_LORA + KV_LORA, Q_LORA + KV_LORA + QK_ROPE,
                      Q_LORA + KV_LORA + QK_ROPE + SSM_WIDTH,
                      Q_LORA + KV_LORA + QK_ROPE + SSM_WIDTH + CONV_WIDTH], axis=1)
        k_pe_full = jnp.pad(k_pe, ((0, 0), (QK_NOPE, 0)))
        win = jnp.concatenate([c_q, c_kv, u_s, c_a, c_g, jnp.pad(k_pe_full, ((0, 0), (0, HEAD_PAD - QK_HEAD))),
                               _swap_rope(k_pe_full)], axis=1).astype(BF16)
        wkv = w_ukv[l].reshape(KV_LORA, N_HEADS, QK_NOPE + V_HEAD)
        wuk = _pad_heads(wkv[:, :, :QK_NOPE].reshape(KV_LORA, N_HEADS * QK_NOPE), QK_NOPE).astype(BF16)
        wuv = wkv[:, :, QK_NOPE:].reshape(KV_LORA, ATTN_WIDTH).astype(BF16)
        wuq_sw = _swap_rope(w_uq[l].reshape(Q_LORA, N_HEADS, QK_HEAD)).reshape(Q_LORA, N_HEADS * HEAD_PAD)
        wuq = jnp.concatenate([_pad_heads(w_uq[l], QK_HEAD), wuq_sw], axis=1).astype(BF16)
        qn_scaled = q_norm[l] * (QK_HEAD ** -0.5)
        norm_rows = lambda w: jnp.stack([jnp.pad(w, (0, HEAD_PAD - QK_HEAD)), _swap_rope(w)])
        q, k, v, u_tm, ca, cg = _in_proj(
            x, rope, row(mix_norm[l]), win, row(q_a_norm[l]), wuq, row(kv_a_norm[l]), wuk, wuv,
            norm_rows(qn_scaled), norm_rows(k_norm[l]))
        y_attn = _attention(q, k, v)

        g_out = out_norm[l]
        bbd, lam_rows, cbd = _ssm_params(ssm_lam_re[l], ssm_lam_im[l], ssm_b_re[l], ssm_b_im[l],
                                         ssm_c_re[l], ssm_c_im[l], ssm_log_dt[l])
        y_ssm_tm = _ssm(u_tm.reshape(L * B, SSM_WIDTH), bbd, lam_rows, cbd, row(ssm_d[l]),
                        ssm_w_glu[l].astype(BF16), row(g_out[ATTN_WIDTH:ATTN_WIDTH + SSM_WIDTH]), B)
        y_conv = _conv(ca, cg, conv_dw_w[l], row(conv_dw_b[l]), row(conv_ln_w[l]), row(conv_ln_b[l]),
                       row(g_out[ATTN_WIDTH + SSM_WIDTH:]))

        w_route = jnp.pad(jnp.concatenate([w_exp[l], w_grp[l]], axis=1), ((0, 0), (0, LANES - N_EXPERTS - N_EGROUPS)))
        w_route_hi = w_route.astype(BF16)
        w_route = jnp.concatenate([w_route_hi, (w_route - w_route_hi.astype(F32)).astype(BF16)], axis=1)
        b_route = lane_pad(jnp.concatenate([b_exp[l], b_grp[l]]))
        x1, h2, route, route_t, counts, sub_carry = _out_router(
            x.reshape(B * L, D), y_attn.reshape(B * L, ATTN_WIDTH), y_ssm_tm.reshape(L, B * SSM_WIDTH),
            y_conv.reshape(B * L, CONV_WIDTH), row(g_out[:ATTN_WIDTH]), w_out[l].astype(BF16),
            row(ffn_norm[l]), w_route, b_route, B)
        x = _moe(x1, h2, route, route_t, counts, sub_carry, w1, w3, w2, l).reshape(B, L, D)
    return x
```

```python
import functools
import math

import jax
import jax.numpy as jnp
from jax import lax
from jax.experimental import pallas as pl
from jax.experimental.pallas import tpu as pltpu

D_MODEL = 1024
CHUNK = 64
EPS = 1e-6
N_HEADS = 8
QK_NOPE = 64
QK_ROPE = 32
QK_HEAD = QK_NOPE + QK_ROPE
V_HEAD = 64
Q_LORA = 256
KV_LORA = 128
ROPE_THETA = 10000.0
ATTN_WIDTH = N_HEADS * V_HEAD
SSM_WIDTH = 256
SSM_GROUP = 16
SSM_GROUPS = SSM_WIDTH // SSM_GROUP
SSM_STATE = 64
SSM_FLAT = SSM_GROUPS * SSM_STATE
CONV_WIDTH = 256
CONV_K = 31
N_EGROUPS = 4
EXP_PER_GROUP = 8
N_EXPERTS = N_EGROUPS * EXP_PER_GROUP
D_FF_E = 512

LANES = 128
SUBLANES = 8
HEAD_PAD = LANES
IN_PROJ_PAD = Q_LORA + KV_LORA + SSM_WIDTH + 2 * CONV_WIDTH + 2 * LANES
CONV_HALO = 32
VMEM_LIMIT = 48 * 1024 * 1024

TM_PROJ = 512
TQ_ATTN = 256
TC_SSM = 64
T_CONV = 256
CONV_SUB = 64
TM_OUT = 512
TM_MOVE = 256
RUN_ROWS = SUBLANES
RUN_SLOTS = 2 * N_EXPERTS + 2 * TM_MOVE // RUN_ROWS
DMA_UNROLL = 8
TB_EXP = 512
FF_CHUNK = 256

BF16 = jnp.bfloat16
F32 = jnp.float32
U32 = jnp.uint32
D_PACK = D_MODEL // 2


def _pack_rows(v):
    bits = lax.bitcast_convert_type(v.astype(BF16).astype(F32), U32)
    half = v.shape[1] // 2
    return bits[:, 0:half] | (bits[:, half:] >> 16)


def _unpack_rows(w):
    hi = lax.bitcast_convert_type(w & jnp.uint32(0xFFFF0000), F32)
    lo = lax.bitcast_convert_type(w << 16, F32)
    return jnp.concatenate([hi, lo], axis=1).astype(BF16)


def _rms(x, w):
    return x * lax.rsqrt(jnp.mean(x * x, axis=-1, keepdims=True) + EPS) * w


def _params(*sem):
    return pltpu.CompilerParams(dimension_semantics=sem, vmem_limit_bytes=VMEM_LIMIT)


def _in_proj_kernel(x_ref, rope_ref, mixn_ref, win_ref, qan_ref, wuq_ref, kvan_ref, wuk_ref, wuv_ref,
                    qn_ref, kn_ref, q_ref, k_ref, v_ref, u_ref, ca_ref, cg_ref):
    x = x_ref[0]
    h = _rms(x, mixn_ref[...]).astype(BF16)
    proj = jnp.dot(h, win_ref[...], preferred_element_type=F32)
    o = 0
    c_q = proj[:, o:o + Q_LORA]; o += Q_LORA
    c_kv = proj[:, o:o + KV_LORA]; o += KV_LORA
    u_ref[...] = proj[:, o:o + SSM_WIDTH].astype(BF16); o += SSM_WIDTH
    ca_ref[0] = proj[:, o:o + CONV_WIDTH].astype(BF16); o += CONV_WIDTH
    cg_ref[0] = proj[:, o:o + CONV_WIDTH].astype(BF16); o += CONV_WIDTH
    k_pe = proj[:, o:o + LANES]; o += LANES
    k_pe_sw = proj[:, o:o + LANES]

    width = N_HEADS * HEAD_PAD
    q2 = jnp.dot(_rms(c_q, qan_ref[...]).astype(BF16), wuq_ref[...], preferred_element_type=F32)
    q, q_sw = q2[:, 0:width], q2[:, width:2 * width]
    ckv_n = _rms(c_kv, kvan_ref[...]).astype(BF16)
    kn = jnp.dot(ckv_n, wuk_ref[...], preferred_element_type=F32)
    v_ref[0] = jnp.dot(ckv_n, wuv_ref[...], preferred_element_type=F32).astype(BF16)

    rope = rope_ref[0]
    half = QK_ROPE // 2
    lane = lax.broadcasted_iota(jnp.int32, rope.shape, 1)
    lo = (lane >= QK_NOPE) & (lane < QK_NOPE + half)
    hi = (lane >= QK_NOPE + half) & (lane < QK_HEAD)
    cos_t = jnp.where(lane < QK_NOPE, 1.0, jnp.where(lo, rope, jnp.where(hi, pltpu.roll(rope, half, 1), 0.0)))
    sin_t = jnp.where(lo, -pltpu.roll(rope, LANES - half, 1), jnp.where(hi, rope, 0.0))
    a_q, b_q = qn_ref[0:1, :] * cos_t, qn_ref[1:2, :] * sin_t
    a_k = kn_ref[0:1, :] * cos_t
    k_sw_term = k_pe_sw * (kn_ref[1:2, :] * sin_t)

    def inv_rms(y):
        return lax.rsqrt(jnp.sum(y * y, axis=-1, keepdims=True) * (1.0 / QK_HEAD) + EPS)

    for hd in range(N_HEADS):
        sl = slice(hd * HEAD_PAD, (hd + 1) * HEAD_PAD)
        yq = q[:, sl]
        q_ref[0, hd] = ((yq * a_q + q_sw[:, sl] * b_q) * inv_rms(yq)).astype(BF16)
        yk = kn[:, sl] + k_pe
        k_ref[0, hd] = ((yk * a_k + k_sw_term) * inv_rms(yk)).astype(BF16)


def _in_proj(x, rope, mixn, win, qan, wuq, kvan, wuk, wuv, qn, kn):
    B, L, D = x.shape
    tm = min(TM_PROJ, L)
    full = lambda shape: pl.BlockSpec(shape, lambda b, t: (0,) * len(shape))
    return pl.pallas_call(
        _in_proj_kernel,
        grid=(B, L // tm),
        in_specs=[
            pl.BlockSpec((1, tm, D), lambda b, t: (b, t, 0)),
            pl.BlockSpec((1, tm, LANES), lambda b, t: (b, t, 0)),
            full((1, D)), full((D, IN_PROJ_PAD)), full((1, Q_LORA)), full((Q_LORA, 2 * N_HEADS * HEAD_PAD)),
            full((1, KV_LORA)), full((KV_LORA, N_HEADS * HEAD_PAD)), full((KV_LORA, ATTN_WIDTH)),
            full((2, HEAD_PAD)), full((2, HEAD_PAD)),
        ],
        out_specs=[
            pl.BlockSpec((1, N_HEADS, tm, HEAD_PAD), lambda b, t: (b, 0, t, 0)),
            pl.BlockSpec((1, N_HEADS, tm, HEAD_PAD), lambda b, t: (b, 0, t, 0)),
            pl.BlockSpec((1, tm, ATTN_WIDTH), lambda b, t: (b, t, 0)),
            pl.BlockSpec((tm, SSM_WIDTH), lambda b, t: (t, b)),
            pl.BlockSpec((1, tm, CONV_WIDTH), lambda b, t: (b, t, 0)),
            pl.BlockSpec((1, tm, CONV_WIDTH), lambda b, t: (b, t, 0)),
        ],
        out_shape=[
            jax.ShapeDtypeStruct((B, N_HEADS, L, HEAD_PAD), BF16),
            jax.ShapeDtypeStruct((B, N_HEADS, L, HEAD_PAD), BF16),
            jax.ShapeDtypeStruct((B, L, ATTN_WIDTH), BF16),
            jax.ShapeDtypeStruct((L, B * SSM_WIDTH), BF16),
            jax.ShapeDtypeStruct((B, L, CONV_WIDTH), BF16),
            jax.ShapeDtypeStruct((B, L, CONV_WIDTH), BF16),
        ],
        compiler_params=_params("parallel", "parallel"),
    )(x, rope, mixn, win, qan, wuq, kvan, wuk, wuv, qn, kn)


def _attention_kernel(q_ref, k_ref, v_ref, o_ref, vext_ref, *, seq, tq):
    pair = 2 * V_HEAD
    vext_ref[:, 0:pair] = v_ref[0]
    vext_ref[:, pair:] = jnp.ones((seq, LANES), BF16)
    row_chunk = lax.broadcasted_iota(jnp.int32, (tq, tq), 0) // CHUNK
    col_chunk = lax.broadcasted_iota(jnp.int32, (tq, tq), 1) // CHUNK
    visible = col_chunk <= row_chunk
    neg = jnp.finfo(F32).min
    nt = (((1,), (1,)), ((), ()))
    low_lanes = lax.broadcasted_iota(jnp.int32, (tq, pair), 1) < V_HEAD
    for i in range(seq // tq):
        q0 = i * tq
        outs = []
        for hh in range(2):
            qb = q_ref[0, hh, q0:q0 + tq, :]
            s_d = lax.dot_general(qb, k_ref[0, hh, q0:q0 + tq, :], nt, preferred_element_type=F32)
            s_d = jnp.where(visible, s_d, neg)
            m = jnp.max(s_d, axis=-1, keepdims=True)
            if i > 0:
                s_l = lax.dot_general(qb, k_ref[0, hh, 0:q0, :], nt, preferred_element_type=F32)
                m = jnp.maximum(m, jnp.max(s_l, axis=-1, keepdims=True))
            acc = jnp.dot(jnp.exp(s_d - m).astype(BF16), vext_ref[q0:q0 + tq, :], preferred_element_type=F32)
            if i > 0:
                acc = acc + jnp.dot(jnp.exp(s_l - m).astype(BF16), vext_ref[0:q0, :], preferred_element_type=F32)
            outs.append(acc[:, 0:pair] / acc[:, pair:])
        o_ref[0, q0:q0 + tq, :] = jnp.where(low_lanes, outs[0], outs[1]).astype(BF16)


def _attention(q, k, v):
    B, H, L, _ = q.shape
    tq = min(TQ_ATTN, L)
    return pl.pallas_call(
        functools.partial(_attention_kernel, seq=L, tq=tq),
        grid=(B, H // 2),
        in_specs=[
            pl.BlockSpec((1, 2, L, HEAD_PAD), lambda b, p: (b, p, 0, 0)),
            pl.BlockSpec((1, 2, L, HEAD_PAD), lambda b, p: (b, p, 0, 0)),
            pl.BlockSpec((1, L, 2 * V_HEAD), lambda b, p: (b, 0, p)),
        ],
        out_specs=pl.BlockSpec((1, L, 2 * V_HEAD), lambda b, p: (b, 0, p)),
        out_shape=jax.ShapeDtypeStruct((B, L, ATTN_WIDTH), BF16),
        scratch_shapes=[pltpu.VMEM((L, 2 * V_HEAD + LANES), BF16)],
        compiler_params=_params("parallel", "parallel"),
    )(q, k, v)


def _ssm_kernel(u_ref, bbd_ref, lam_ref, cbd_ref, d_ref, wglu_ref, g_ref, o_ref, state_ref, bu_ref, *, batch, tc):
    @pl.when(pl.program_id(0) == 0)
    def _():
        state_ref[...] = jnp.zeros_like(state_ref)

    u = u_ref[...]
    bu_ref[...] = jnp.dot(u, bbd_ref[...], preferred_element_type=F32)
    lam_re = jnp.broadcast_to(lam_ref[0:1, :], (batch, SSM_FLAT))
    lam_im = jnp.broadcast_to(lam_ref[1:2, :], (batch, SSM_FLAT))

    def step(t, carry):
        xr, xi = carry
        rows = pl.ds(pl.multiple_of(t * batch, batch), batch)
        nr = lam_re * xr - lam_im * xi + bu_ref[rows, 0:SSM_FLAT]
        ni = lam_re * xi + lam_im * xr + bu_ref[rows, SSM_FLAT:2 * SSM_FLAT]
        bu_ref[rows, 0:SSM_FLAT] = nr
        bu_ref[rows, SSM_FLAT:2 * SSM_FLAT] = ni
        return nr, ni

    xr, xi = lax.fori_loop(0, tc, step, (state_ref[:, 0:SSM_FLAT], state_ref[:, SSM_FLAT:2 * SSM_FLAT]),
                           unroll=2)
    state_ref[:, 0:SSM_FLAT] = xr
    state_ref[:, SSM_FLAT:2 * SSM_FLAT] = xi

    y = jnp.dot(bu_ref[...].astype(BF16), cbd_ref[...], preferred_element_type=F32)
    y = y + d_ref[...] * u.astype(F32)
    z = jax.nn.gelu(y)
    gate = jax.nn.sigmoid(jnp.dot(z.astype(BF16), wglu_ref[...], preferred_element_type=F32))
    o_ref[...] = _rms(z * gate, g_ref[...]).astype(BF16)


def _ssm(u_tm, bbd, lam, cbd, d, wglu, g, batch):
    rows = u_tm.shape[0]
    seq = rows // batch
    tc = min(TC_SSM, seq)
    blk = tc * batch
    full = lambda shape: pl.BlockSpec(shape, lambda t: (0,) * len(shape))
    return pl.pallas_call(
        functools.partial(_ssm_kernel, batch=batch, tc=tc),
        grid=(seq // tc,),
        in_specs=[
            pl.BlockSpec((blk, SSM_WIDTH), lambda t: (t, 0)),
            full((SSM_WIDTH, 2 * SSM_FLAT)), full((2, SSM_FLAT)), full((2 * SSM_FLAT, SSM_WIDTH)),
            full((1, SSM_WIDTH)), full((SSM_WIDTH, SSM_WIDTH)), full((1, SSM_WIDTH)),
        ],
        out_specs=pl.BlockSpec((blk, SSM_WIDTH), lambda t: (t, 0)),
        out_shape=jax.ShapeDtypeStruct((rows, SSM_WIDTH), BF16),
        scratch_shapes=[pltpu.VMEM((batch, 2 * SSM_FLAT), F32), pltpu.VMEM((blk, 2 * SSM_FLAT), F32)],
        compiler_params=_params("arbitrary"),
    )(u_tm, bbd, lam, cbd, d, wglu, g)


def _conv_kernel(a_ref, g_ref, ah_ref, gh_ref, w_ref, b_ref, lnw_ref, lnb_ref, gn_ref, o_ref, upad_ref, *, tt):
    first = pl.program_id(1) == 0
    halo = ah_ref[0].astype(F32) * jax.nn.sigmoid(gh_ref[0].astype(F32))
    upad_ref[0:CONV_HALO, :] = jnp.where(first, 0.0, halo)
    upad_ref[CONV_HALO:, :] = a_ref[0].astype(F32) * jax.nn.sigmoid(g_ref[0].astype(F32))
    base = CONV_HALO - (CONV_K - 1)
    for s in range(tt // CONV_SUB):
        r0 = s * CONV_SUB
        acc = jnp.zeros((CONV_SUB, CONV_WIDTH), F32)
        for off in range(SUBLANES):
            n = CONV_SUB + (SUBLANES if off else 0)
            part = None
            for a8 in range(0, CONV_HALO + 1, SUBLANES):
                kk = a8 + off - base
                if 0 <= kk < CONV_K:
                    term = w_ref[kk:kk + 1, :] * upad_ref[r0 + a8:r0 + a8 + n, :]
                    part = term if part is None else part + term
            acc = acc + part[off:off + CONV_SUB]
        y = acc + b_ref[...]
        mu = jnp.mean(y, axis=-1, keepdims=True)
        var = jnp.mean(jnp.square(y - mu), axis=-1, keepdims=True)
        y = (y - mu) * lax.rsqrt(var + 1e-5) * lnw_ref[...] + lnb_ref[...]
        y = y * jax.nn.sigmoid(y)
        o_ref[0, r0:r0 + CONV_SUB, :] = _rms(y, gn_ref[...]).astype(BF16)


def _conv(a, g, w, b, lnw, lnb, gn):
    B, L, C = a.shape
    tt = min(T_CONV, L)
    ratio = tt // CONV_HALO
    full = lambda shape: pl.BlockSpec(shape, lambda bb, t: (0,) * len(shape))
    cur = pl.BlockSpec((1, tt, C), lambda bb, t: (bb, t, 0))
    prev = pl.BlockSpec((1, CONV_HALO, C), lambda bb, t: (bb, jnp.maximum(t * ratio - 1, 0), 0))
    return pl.pallas_call(
        functools.partial(_conv_kernel, tt=tt),
        grid=(B, L // tt),
        in_specs=[cur, cur, prev, prev, full((CONV_K, C)), full((1, C)), full((1, C)), full((1, C)), full((1, C))],
        out_specs=pl.BlockSpec((1, tt, C), lambda bb, t: (bb, t, 0)),
        out_shape=jax.ShapeDtypeStruct((B, L, C), BF16),
        scratch_shapes=[pltpu.VMEM((tt + CONV_HALO, C), F32)],
        compiler_params=_params("parallel", "parallel"),
    )(a, g, a, g, w, b, lnw, lnb, gn)


ROUTE_E1, ROUTE_E2, ROUTE_R1, ROUTE_R2, ROUTE_G1, ROUTE_G2 = range(6)
ROUTE_ROWS = 8
GRP_LANE0 = N_EXPERTS


def _out_router_kernel(x_ref, ya_ref, ys_ref, yc_ref, ga_ref, wo_ref, fn_ref, wr_ref, br_ref,
                       x1_ref, h2_ref, route_ref, route_t_ref, cnt_ref, sub_ref, carry_ref, *, tm):
    @pl.when(pl.program_id(0) == 0)
    def _():
        carry_ref[...] = jnp.zeros_like(carry_ref)

    ya = _rms(ya_ref[...].astype(F32), ga_ref[...]).astype(BF16)
    acc = jnp.dot(ya, wo_ref[0:ATTN_WIDTH, :], preferred_element_type=F32)
    acc += jnp.dot(ys_ref[...], wo_ref[ATTN_WIDTH:ATTN_WIDTH + SSM_WIDTH, :], preferred_element_type=F32)
    acc += jnp.dot(yc_ref[...], wo_ref[ATTN_WIDTH + SSM_WIDTH:, :], preferred_element_type=F32)
    x1 = x_ref[...] + acc
    x1_ref[...] = x1
    h2 = _rms(x1, fn_ref[...])
    h2_ref[...] = _pack_rows(h2)

    h_hi = h2.astype(BF16)
    h_lo = (h2 - h_hi.astype(F32)).astype(BF16)
    part = jnp.dot(h_hi, wr_ref[...], preferred_element_type=F32)
    logits = (part[:, 0:LANES] + part[:, LANES:2 * LANES]
              + jnp.dot(h_lo, wr_ref[:, 0:LANES], preferred_element_type=F32) + br_ref[...])
    lane = lax.broadcasted_iota(jnp.int32, (tm, LANES), 1)
    ninf = -jnp.inf
    big = LANES

    def first_argmax(vals, vmax):
        return jnp.min(jnp.where(vals == vmax, lane, big), axis=-1, keepdims=True)

    grp = jnp.where((lane >= GRP_LANE0) & (lane < GRP_LANE0 + N_EGROUPS), logits, ninf)
    gmax = jnp.max(grp, axis=-1, keepdims=True)
    gsel = first_argmax(grp, gmax) - GRP_LANE0
    p_grp = 1.0 / jnp.sum(jnp.exp(grp - gmax), axis=-1, keepdims=True)

    el = jnp.where((lane < N_EXPERTS) & ((lane // EXP_PER_GROUP) == gsel), logits, ninf)
    m1 = jnp.max(el, axis=-1, keepdims=True)
    e1 = first_argmax(el, m1)
    el2 = jnp.where(lane == e1, ninf, el)
    m2 = jnp.max(el2, axis=-1, keepdims=True)
    e2 = first_argmax(el2, m2)
    t2 = jnp.exp(m2 - m1)
    g1 = p_grp / (1.0 + t2)
    g2 = p_grp * t2 / (1.0 + t2)

    hit1 = lane == e1
    hit2 = lane == e2
    cnt = (hit1 | hit2).astype(F32)
    rr = lax.broadcasted_iota(jnp.int32, (tm, tm), 0)
    cc = lax.broadcasted_iota(jnp.int32, (tm, tm), 1)
    tri = (cc < rr).astype(BF16)
    before = jnp.dot(tri, cnt.astype(BF16), preferred_element_type=F32) + carry_ref[...]
    r1 = jnp.sum(jnp.where(hit1, before, 0.0), axis=-1, keepdims=True)
    r2 = jnp.sum(jnp.where(hit2, before, 0.0), axis=-1, keepdims=True)
    subs = [before[h * TM_MOVE:h * TM_MOVE + 1, :] for h in range(tm // TM_MOVE)]
    sub_ref[...] = jnp.concatenate(subs + [jnp.zeros((SUBLANES - len(subs), LANES), F32)], axis=0)
    carry_ref[...] += jnp.sum(cnt, axis=0, keepdims=True)
    cnt_ref[...] = carry_ref[...]

    rec = jnp.where(lane == ROUTE_E1, e1.astype(F32), 0.0)
    rec = jnp.where(lane == ROUTE_E2, e2.astype(F32), rec)
    rec = jnp.where(lane == ROUTE_R1, r1, rec)
    rec = jnp.where(lane == ROUTE_R2, r2, rec)
    rec = jnp.where(lane == ROUTE_G1, g1, rec)
    rec = jnp.where(lane == ROUTE_G2, g2, rec)
    route_ref[...] = rec
    route_t_ref[...] = rec.T[0:ROUTE_ROWS, :]


def _out_router(x, ya, ys_tm, yc, ga, wo, fn, wr, br, batch):
    N, D = x.shape
    seq = N // batch
    tm = min(TM_OUT, seq)
    per_b = seq // tm
    full = lambda shape: pl.BlockSpec(shape, lambda i: (0,) * len(shape))
    tile = lambda w: pl.BlockSpec((tm, w), lambda i: (i, 0))
    return pl.pallas_call(
        functools.partial(_out_router_kernel, tm=tm),
        grid=(N // tm,),
        in_specs=[
            tile(D), tile(ATTN_WIDTH),
            pl.BlockSpec((tm, SSM_WIDTH), lambda i: (i % per_b, i // per_b)),
            tile(CONV_WIDTH),
            full((1, ATTN_WIDTH)), full((D, D)), full((1, D)), full((D, 2 * LANES)), full((1, LANES)),
        ],
        out_specs=[tile(D), tile(D_PACK), tile(LANES), pl.BlockSpec((ROUTE_ROWS, tm), lambda i: (0, i)),
                   full((1, LANES)), pl.BlockSpec((SUBLANES, LANES), lambda i: (i, 0))],
        out_shape=[
            jax.ShapeDtypeStruct((N, D), F32),
            jax.ShapeDtypeStruct((N, D_PACK), U32),
            jax.ShapeDtypeStruct((N, LANES), F32),
            jax.ShapeDtypeStruct((ROUTE_ROWS, N), F32),
            jax.ShapeDtypeStruct((1, LANES), F32),
            jax.ShapeDtypeStruct((N // tm * SUBLANES, LANES), F32),
        ],
        scratch_shapes=[pltpu.VMEM((1, LANES), F32)],
        compiler_params=_params("arbitrary"),
    )(x, ya, ys_tm, yc, ga, wo, fn, wr, br)


def _dispatch_kernel(dest_ref, h_ref, rows_ref, sem, *, tm, n_tok):
    base = pl.program_id(0) * tm

    def issue(r, c):
        src = h_ref.at[pl.ds(r, 1)]
        pltpu.make_async_copy(src, rows_ref.at[pl.ds(dest_ref[base + r], 1)], sem).start()
        pltpu.make_async_copy(src, rows_ref.at[pl.ds(dest_ref[n_tok + base + r], 1)], sem).start()
        return c

    lax.fori_loop(0, tm, issue, 0, unroll=DMA_UNROLL)
    for _ in range(2):
        pltpu.make_async_copy(h_ref, rows_ref.at[pl.ds(0, tm)], sem).wait()


def _dispatch(dest_flat, h2, n_rows):
    N, D = h2.shape
    tm = min(TM_MOVE, N)
    return pl.pallas_call(
        functools.partial(_dispatch_kernel, tm=tm, n_tok=N),
        grid_spec=pltpu.PrefetchScalarGridSpec(
            num_scalar_prefetch=1,
            grid=(N // tm,),
            in_specs=[pl.BlockSpec((tm, D), lambda i, d: (i, 0))],
            out_specs=pl.BlockSpec(memory_space=pl.ANY),
            scratch_shapes=[pltpu.SemaphoreType.DMA(())],
        ),
        out_shape=jax.ShapeDtypeStruct((n_rows, D), h2.dtype),
        compiler_params=_params("arbitrary"),
    )(dest_flat, h2)


def _experts_kernel(blk_ref, exp_ref, valid_ref, used_ref, x_ref, w1_ref, w3_ref, w2_ref, y_ref, w13_s, w2_s):
    j = pl.program_id(0)

    @pl.when(j < used_ref[0])
    def _():
        prev = exp_ref[jnp.maximum(j - 1, 0)]

        @pl.when((j == 0) | (exp_ref[j] != prev))
        def _():
            for c in range(D_FF_E // FF_CHUNK):
                w13_s[:, 2 * FF_CHUNK * c:2 * FF_CHUNK * c + FF_CHUNK] = \
                    w1_ref[0, 0, :, FF_CHUNK * c:FF_CHUNK * (c + 1)].astype(BF16)
                w13_s[:, 2 * FF_CHUNK * c + FF_CHUNK:2 * FF_CHUNK * (c + 1)] = \
                    w3_ref[0, 0, :, FF_CHUNK * c:FF_CHUNK * (c + 1)].astype(BF16)
            w2_s[...] = w2_ref[0, 0].astype(BF16)

        x = _unpack_rows(x_ref[...])
        y = None
        for c in range(D_FF_E // FF_CHUNK):
            ab = jnp.dot(x, w13_s[:, 2 * FF_CHUNK * c:2 * FF_CHUNK * (c + 1)], preferred_element_type=F32)
            a = ab[:, 0:FF_CHUNK]
            hmid = (a * jax.nn.sigmoid(a) * ab[:, FF_CHUNK:]).astype(BF16)
            part = jnp.dot(hmid, w2_s[FF_CHUNK * c:FF_CHUNK * (c + 1), :], preferred_element_type=F32)
            y = part if y is None else y + part
        row = lax.broadcasted_iota(jnp.int32, (TB_EXP, 1), 0)
        y_ref[...] = jnp.where(row < valid_ref[j], _pack_rows(y), jnp.uint32(0))


def _experts(blk_map, blk_exp, blk_valid, n_used, x_rows, w1, w3, w2, layer):
    R, D = x_rows.shape[0], D_MODEL
    nb = R // TB_EXP
    return pl.pallas_call(
        _experts_kernel,
        grid_spec=pltpu.PrefetchScalarGridSpec(
            num_scalar_prefetch=4,
            grid=(nb,),
            in_specs=[
                pl.BlockSpec((TB_EXP, D_PACK), lambda j, bm, be, bv, nu: (bm[j], 0)),
                pl.BlockSpec((1, 1, D, D_FF_E), lambda j, bm, be, bv, nu: (layer, be[j], 0, 0)),
                pl.BlockSpec((1, 1, D, D_FF_E), lambda j, bm, be, bv, nu: (layer, be[j], 0, 0)),
                pl.BlockSpec((1, 1, D_FF_E, D), lambda j, bm, be, bv, nu: (layer, be[j], 0, 0)),
            ],
            out_specs=pl.BlockSpec((TB_EXP, D_PACK), lambda j, bm, be, bv, nu: (bm[j], 0)),
            scratch_shapes=[pltpu.VMEM((D, 2 * D_FF_E), BF16), pltpu.VMEM((D_FF_E, D), BF16)],
        ),
        out_shape=jax.ShapeDtypeStruct((R, D_PACK), U32),
        compiler_params=_params("arbitrary"),
    )(blk_map, blk_exp, blk_valid, n_used, x_rows, w1, w3, w2)


def _combine_kernel(src_ref, nslot_ref, x1_ref, route_ref, off_ref, rows_ref, o_ref, buf, sem, *, tm):
    i = pl.program_id(0)
    cur = i % 2

    def slot_copy(tile, half, s):
        src = pl.multiple_of(src_ref[tile * RUN_SLOTS + s], RUN_ROWS)
        dst = buf.at[half, pl.ds(pl.multiple_of(s * RUN_ROWS, RUN_ROWS), RUN_ROWS)]
        return pltpu.make_async_copy(rows_ref.at[pl.ds(src, RUN_ROWS)], dst, sem.at[half])

    def fetch(tile, half):
        def issue(s, c):
            slot_copy(tile, half, s).start()
            return c
        lax.fori_loop(0, nslot_ref[tile], issue, 0)

    def wait(s, c):
        slot_copy(i, cur, s).wait()
        return c

    @pl.when(i == 0)
    def _():
        buf[...] = jnp.zeros_like(buf)
        fetch(0, 0)

    @pl.when(i + 1 < pl.num_programs(0))
    def _():
        fetch(i + 1, 1 - cur)

    route = route_ref[...]
    lane = lax.broadcasted_iota(jnp.int32, (tm, LANES), 1).astype(F32)
    off = off_ref[0]

    def buf_row(e_lane, r_lane):
        e = route[:, e_lane:e_lane + 1]
        base = jnp.sum(jnp.where(lane == e, off, 0.0), axis=-1, keepdims=True)
        return (base + route[:, r_lane:r_lane + 1]).astype(jnp.int32)

    col = lax.broadcasted_iota(jnp.int32, (tm, RUN_SLOTS * RUN_ROWS), 1)
    pick = (jnp.where(col == buf_row(ROUTE_E1, ROUTE_R1), route[:, ROUTE_G1:ROUTE_G1 + 1], 0.0)
            + jnp.where(col == buf_row(ROUTE_E2, ROUTE_R2), route[:, ROUTE_G2:ROUTE_G2 + 1], 0.0))
    lax.fori_loop(0, nslot_ref[i], wait, 0)
    o_ref[...] = x1_ref[...] + jnp.dot(pick.astype(BF16), _unpack_rows(buf[cur]), preferred_element_type=F32)


def _combine(src, nslot, x1, route, off, y_rows):
    N, D = x1.shape
    tm = min(TM_MOVE, N)
    return pl.pallas_call(
        functools.partial(_combine_kernel, tm=tm),
        grid_spec=pltpu.PrefetchScalarGridSpec(
            num_scalar_prefetch=2,
            grid=(N // tm,),
            in_specs=[
                pl.BlockSpec((tm, D), lambda i, s, n: (i, 0)),
                pl.BlockSpec((tm, LANES), lambda i, s, n: (i, 0)),
                pl.BlockSpec((1, 1, LANES), lambda i, s, n: (i, 0, 0)),
                pl.BlockSpec(memory_space=pl.ANY),
            ],
            out_specs=pl.BlockSpec((tm, D), lambda i, s, n: (i, 0)),
            scratch_shapes=[pltpu.VMEM((2, RUN_SLOTS * RUN_ROWS, D_PACK), U32), pltpu.SemaphoreType.DMA((2,))],
        ),
        out_shape=jax.ShapeDtypeStruct((N, D), F32),
        compiler_params=_params("arbitrary"),
    )(src, nslot, x1, route, off, y_rows)


def _moe(x1, h2, route, route_t, counts, sub_carry, w1, w3, w2, layer):
    N, _ = x1.shape
    tm = min(TM_MOVE, N)
    n_tiles = N // tm
    experts = jnp.arange(N_EXPERTS, dtype=jnp.int32)
    nb = (2 * N + N_EXPERTS * (TB_EXP - 1)) // TB_EXP + 1
    e_id = route_t[ROUTE_E1:ROUTE_E2 + 1].astype(jnp.int32)
    rank = route_t[ROUTE_R1:ROUTE_R2 + 1].astype(jnp.int32)
    cnt = counts[0, :N_EXPERTS].astype(jnp.int32)
    padded = (cnt + TB_EXP - 1) // TB_EXP * TB_EXP
    pad_end = jnp.cumsum(padded)
    pad_start = pad_end - padded
    e_flat = e_id.reshape(1, 2 * N)
    dest = jnp.sum(jnp.where(e_flat == experts[:, None], pad_start[:, None], 0), axis=0) + rank.reshape(2 * N)
    n_used = pad_end[-1] // TB_EXP
    blk_map = jnp.minimum(jnp.arange(nb, dtype=jnp.int32), jnp.maximum(n_used - 1, 0))
    blk_exp = jnp.sum(pad_end[None, :] <= (blk_map * TB_EXP)[:, None], axis=1).astype(jnp.int32)
    blk_exp = jnp.minimum(blk_exp, N_EXPERTS - 1)
    is_blk_e = blk_exp[:, None] == experts[None, :]
    blk_valid = jnp.clip(jnp.sum(jnp.where(is_blk_e, (pad_start + cnt)[None, :], 0), axis=1) - blk_map * TB_EXP,
                         0, TB_EXP).astype(jnp.int32)

    per_router_tile = TM_OUT // tm if N >= TM_OUT else 1
    carry = sub_carry.reshape(-1, SUBLANES, LANES)[:, :per_router_tile, :N_EXPERTS].reshape(n_tiles, N_EXPERTS)
    carry = carry.astype(jnp.int32)
    tile_cnt = jnp.concatenate([carry[1:], cnt[None, :]], axis=0) - carry
    run_start = pad_start[None, :] + carry
    first_blk = run_start // RUN_ROWS
    nslot = jnp.where(tile_cnt > 0, (run_start + tile_cnt - 1) // RUN_ROWS - first_blk + 1, 0)
    slot_end = jnp.cumsum(nslot, axis=1)
    slot_base = slot_end - nslot
    slots = jnp.arange(RUN_SLOTS, dtype=jnp.int32)
    slot_e = jnp.minimum(jnp.sum(slot_end[:, None, :] <= slots[None, :, None], axis=2), N_EXPERTS - 1)
    is_slot_e = slot_e[:, :, None] == experts[None, None, :]
    pick = lambda tbl: jnp.sum(jnp.where(is_slot_e, tbl[:, None, :], 0), axis=2)
    src = (pick(first_blk) + slots[None, :] - pick(slot_base)) * RUN_ROWS
    src = jnp.clip(src, 0, nb * TB_EXP - RUN_ROWS).reshape(-1).astype(jnp.int32)
    off = (slot_base - first_blk) * RUN_ROWS + pad_start[None, :]
    off = jnp.pad(off.astype(F32), ((0, 0), (0, LANES - N_EXPERTS)))

    x_rows = _dispatch(dest, h2, nb * TB_EXP)
    y_rows = _experts(blk_map, blk_exp, blk_valid, n_used.reshape(1).astype(jnp.int32), x_rows, w1, w3, w2, layer)
    return _combine(src, slot_end[:, -1].astype(jnp.int32), x1, route, off.reshape(n_tiles, 1, LANES), y_rows)


def _pad_heads(w, width):
    k = w.shape[0]
    w = w.reshape(k, N_HEADS, width)
    return jnp.pad(w, ((0, 0), (0, 0), (0, HEAD_PAD - width))).reshape(k, N_HEADS * HEAD_PAD)


def _swap_rope(w):
    half = QK_ROPE // 2
    lo, hi = w[..., QK_NOPE:QK_NOPE + half], w[..., QK_NOPE + half:QK_HEAD]
    pad = [(0, 0)] * (w.ndim - 1)
    return jnp.pad(jnp.concatenate([hi, lo], -1), pad + [(QK_NOPE, HEAD_PAD - QK_HEAD)])


def _rope_tables(positions):
    inv_freq = ROPE_THETA ** (-jnp.arange(0, QK_ROPE, 2, dtype=F32) / QK_ROPE)
    ang = positions.astype(F32)[..., None] * inv_freq
    table = jnp.concatenate([jnp.cos(ang), jnp.sin(ang)], -1)
    return jnp.pad(table, ((0, 0), (0, 0), (QK_NOPE, LANES - QK_HEAD)))


def _ssm_params(lam_re, lam_im, b_re, b_im, c_re, c_im, log_dt):
    lam = lax.complex(lam_re, lam_im)
    dt = jnp.exp(log_dt)[:, None]
    lam_bar = jnp.exp(lam * dt)
    b_bar = ((lam_bar - 1.0) / lam)[..., None] * lax.complex(b_re, b_im)
    eye = jnp.eye(SSM_GROUPS, dtype=F32)

    def in_blockdiag(m):
        return jnp.einsum("gpc,gh->gchp", m, eye).reshape(SSM_WIDTH, SSM_FLAT)

    def out_blockdiag(m):
        return jnp.einsum("gcp,gh->gphc", m, eye).reshape(SSM_FLAT, SSM_WIDTH)

    bbd = jnp.concatenate([in_blockdiag(jnp.real(b_bar)), in_blockdiag(jnp.imag(b_bar))], axis=1)
    cbd = jnp.concatenate([out_blockdiag(c_re), out_blockdiag(-c_im)], axis=0)
    lam_rows = jnp.stack([jnp.real(lam_bar).reshape(SSM_FLAT), jnp.imag(lam_bar).reshape(SSM_FLAT)])
    return bbd.astype(BF16), lam_rows, cbd.astype(BF16)


def kernel(x, positions, mix_norm, w_in, q_a_norm, w_uq, kv_a_norm, w_ukv, q_norm, k_norm, ssm_lam_re, ssm_lam_im, ssm_b_re, ssm_b_im, ssm_c_re, ssm_c_im, ssm_d, ssm_log_dt, ssm_w_glu, conv_dw_w, conv_dw_b, conv_ln_w, conv_ln_b, out_norm, w_out, ffn_norm, w_grp, b_grp, w_exp, b_exp, w1, w3, w2):
    B, L, D = x.shape
    depth = w_in.shape[0]
    rope = _rope_tables(positions)
    row = lambda v: v.reshape(1, -1)
    lane_pad = lambda v: jnp.pad(v, (0, LANES - v.shape[0])).reshape(1, LANES)
    for l in range(depth):
        c_q, c_kv, k_pe, u_s, c_a, c_g = jnp.split(
            w_in[l], [Q_LORA, Q_LORA + KV_LORA, Q_LORA + KV_LORA + QK_ROPE,
                      Q_LORA + KV_LORA + QK_ROPE + SSM_WIDTH,
                      Q_LORA + KV_LORA + QK_ROPE + SSM_WIDTH + CONV_WIDTH], axis=1)
        k_pe_full = jnp.pad(k_pe, ((0, 0), (QK_NOPE, 0)))
        win = jnp.concatenate([c_q, c_kv, u_s, c_a, c_g, jnp.pad(k_pe_full, ((0, 0), (0, HEAD_PAD - QK_HEAD))),
                               _swap_rope(k_pe_full)], axis=1).astype(BF16)
        wkv = w_ukv[l].reshape(KV_LORA, N_HEADS, QK_NOPE + V_HEAD)
        wuk = _pad_heads(wkv[:, :, :QK_NOPE].reshape(KV_LORA, N_HEADS * QK_NOPE), QK_NOPE).astype(BF16)
        wuv = wkv[:, :, QK_NOPE:].reshape(KV_LORA, ATTN_WIDTH).astype(BF16)
        wuq_sw = _swap_rope(w_uq[l].reshape(Q_LORA, N_HEADS, QK_HEAD)).reshape(Q_LORA, N_HEADS * HEAD_PAD)
        wuq = jnp.concatenate([_pad_heads(w_uq[l], QK_HEAD), wuq_sw], axis=1).astype(BF16)
        qn_scaled = q_norm[l] * (QK_HEAD ** -0.5)
        norm_rows = lambda w: jnp.stack([jnp.pad(w, (0, HEAD_PAD - QK_HEAD)), _swap_rope(w)])
        q, k, v, u_tm, ca, cg = _in_proj(
            x, rope, row(mix_norm[l]), win, row(q_a_norm[l]), wuq, row(kv_a_norm[l]), wuk, wuv,
            norm_rows(qn_scaled), norm_rows(k_norm[l]))
        y_attn = _attention(q, k, v)

        g_out = out_norm[l]
        bbd, lam_rows, cbd = _ssm_params(ssm_lam_re[l], ssm_lam_im[l], ssm_b_re[l], ssm_b_im[l],
                                         ssm_c_re[l], ssm_c_im[l], ssm_log_dt[l])
        y_ssm_tm = _ssm(u_tm.reshape(L * B, SSM_WIDTH), bbd, lam_rows, cbd, row(ssm_d[l]),
                        ssm_w_glu[l].astype(BF16), row(g_out[ATTN_WIDTH:ATTN_WIDTH + SSM_WIDTH]), B)
        y_conv = _conv(ca, cg, conv_dw_w[l], row(conv_dw_b[l]), row(conv_ln_w[l]), row(conv_ln_b[l]),
                       row(g_out[ATTN_WIDTH + SSM_WIDTH:]))

        w_route = jnp.pad(jnp.concatenate([w_exp[l], w_grp[l]], axis=1), ((0, 0), (0, LANES - N_EXPERTS - N_EGROUPS)))
        w_route_hi = w_route.astype(BF16)
        w_route = jnp.concatenate([w_route_hi, (w_route - w_route_hi.astype(F32)).astype(BF16)], axis=1)
        b_route = lane_pad(jnp.concatenate([b_exp[l], b_grp[l]]))
        x1, h2, route, route_t, counts, sub_carry = _out_router(
            x.reshape(B * L, D), y_attn.reshape(B * L, ATTN_WIDTH), y_ssm_tm.reshape(L, B * SSM_WIDTH),
            y_conv.reshape(B * L, CONV_WIDTH), row(g_out[:ATTN_WIDTH]), w_out[l].astype(BF16),
            row(ffn_norm[l]), w_route, b_route, B)
        x = _moe(x1, h2, route, route_t, counts, sub_carry, w1, w3, w2, l).reshape(B, L, D)
    return x
```

```python
import functools
import math

import jax
import jax.numpy as jnp
from jax import lax
from jax.experimental import pallas as pl
from jax.experimental.pallas import tpu as pltpu

D_MODEL = 1024
CHUNK = 64
EPS = 1e-6
N_HEADS = 8
QK_NOPE = 64
QK_ROPE = 32
QK_HEAD = QK_NOPE + QK_ROPE
V_HEAD = 64
Q_LORA = 256
KV_LORA = 128
ROPE_THETA = 10000.0
ATTN_WIDTH = N_HEADS * V_HEAD
SSM_WIDTH = 256
SSM_GROUP = 16
SSM_GROUPS = SSM_WIDTH // SSM_GROUP
SSM_STATE = 64
SSM_FLAT = SSM_GROUPS * SSM_STATE
CONV_WIDTH = 256
CONV_K = 31
N_EGROUPS = 4
EXP_PER_GROUP = 8
N_EXPERTS = N_EGROUPS * EXP_PER_GROUP
D_FF_E = 512

LANES = 128
SUBLANES = 8
HEAD_PAD = LANES
IN_PROJ_PAD = Q_LORA + KV_LORA + SSM_WIDTH + 2 * CONV_WIDTH + 2 * LANES
CONV_HALO = 32
VMEM_LIMIT = 48 * 1024 * 1024

TM_PROJ = 512
TQ_ATTN = 256
TC_SSM = 64
T_CONV = 256
CONV_SUB = 64
TM_OUT = 512
TM_MOVE = 256
RUN_ROWS = SUBLANES
RUN_SLOTS = 2 * N_EXPERTS + 2 * TM_MOVE // RUN_ROWS
RUN_GROUP = 8
DMA_UNROLL = 8
TB_EXP = 512
FF_CHUNK = 256

BF16 = jnp.bfloat16
F32 = jnp.float32
U32 = jnp.uint32
D_PACK = D_MODEL // 2


def _pack_rows(v):
    bits = lax.bitcast_convert_type(v.astype(BF16).astype(F32), U32)
    half = v.shape[1] // 2
    return bits[:, 0:half] | (bits[:, half:] >> 16)


def _unpack_rows(w):
    hi = lax.bitcast_convert_type(w & jnp.uint32(0xFFFF0000), F32)
    lo = lax.bitcast_convert_type(w << 16, F32)
    return jnp.concatenate([hi, lo], axis=1).astype(BF16)


def _rms(x, w):
    return x * lax.rsqrt(jnp.mean(x * x, axis=-1, keepdims=True) + EPS) * w


def _params(*sem):
    return pltpu.CompilerParams(dimension_semantics=sem, vmem_limit_bytes=VMEM_LIMIT)


def _in_proj_kernel(x_ref, rope_ref, mixn_ref, win_ref, qan_ref, wuq_ref, kvan_ref, wuk_ref, wuv_ref,
                    qn_ref, kn_ref, q_ref, k_ref, v_ref, u_ref, ca_ref, cg_ref):
    x = x_ref[0]
    h = _rms(x, mixn_ref[...]).astype(BF16)
    proj = jnp.dot(h, win_ref[...], preferred_element_type=F32)
    o = 0
    c_q = proj[:, o:o + Q_LORA]; o += Q_LORA
    c_kv = proj[:, o:o + KV_LORA]; o += KV_LORA
    u_ref[...] = proj[:, o:o + SSM_WIDTH].astype(BF16); o += SSM_WIDTH
    ca_ref[0] = proj[:, o:o + CONV_WIDTH].astype(BF16); o += CONV_WIDTH
    cg_ref[0] = proj[:, o:o + CONV_WIDTH].astype(BF16); o += CONV_WIDTH
    k_pe = proj[:, o:o + LANES]; o += LANES
    k_pe_sw = proj[:, o:o + LANES]

    width = N_HEADS * HEAD_PAD
    q2 = jnp.dot(_rms(c_q, qan_ref[...]).astype(BF16), wuq_ref[...], preferred_element_type=F32)
    q, q_sw = q2[:, 0:width], q2[:, width:2 * width]
    ckv_n = _rms(c_kv, kvan_ref[...]).astype(BF16)
    kn = jnp.dot(ckv_n, wuk_ref[...], preferred_element_type=F32)
    v_ref[0] = jnp.dot(ckv_n, wuv_ref[...], preferred_element_type=F32).astype(BF16)

    rope = rope_ref[0]
    half = QK_ROPE // 2
    lane = lax.broadcasted_iota(jnp.int32, rope.shape, 1)
    lo = (lane >= QK_NOPE) & (lane < QK_NOPE + half)
    hi = (lane >= QK_NOPE + half) & (lane < QK_HEAD)
    cos_t = jnp.where(lane < QK_NOPE, 1.0, jnp.where(lo, rope, jnp.where(hi, pltpu.roll(rope, half, 1), 0.0)))
    sin_t = jnp.where(lo, -pltpu.roll(rope, LANES - half, 1), jnp.where(hi, rope, 0.0))
    a_q, b_q = qn_ref[0:1, :] * cos_t, qn_ref[1:2, :] * sin_t
    a_k = kn_ref[0:1, :] * cos_t
    k_sw_term = k_pe_sw * (kn_ref[1:2, :] * sin_t)

    def inv_rms(y):
        return lax.rsqrt(jnp.sum(y * y, axis=-1, keepdims=True) * (1.0 / QK_HEAD) + EPS)

    for hd in range(N_HEADS):
        sl = slice(hd * HEAD_PAD, (hd + 1) * HEAD_PAD)
        yq = q[:, sl]
        q_ref[0, hd] = ((yq * a_q + q_sw[:, sl] * b_q) * inv_rms(yq)).astype(BF16)
        yk = kn[:, sl] + k_pe
        k_ref[0, hd] = ((yk * a_k + k_sw_term) * inv_rms(yk)).astype(BF16)


def _in_proj(x, rope, mixn, win, qan, wuq, kvan, wuk, wuv, qn, kn):
    B, L, D = x.shape
    tm = min(TM_PROJ, L)
    full = lambda shape: pl.BlockSpec(shape, lambda b, t: (0,) * len(shape))
    return pl.pallas_call(
        _in_proj_kernel,
        grid=(B, L // tm),
        in_specs=[
            pl.BlockSpec((1, tm, D), lambda b, t: (b, t, 0)),
            pl.BlockSpec((1, tm, LANES), lambda b, t: (b, t, 0)),
            full((1, D)), full((D, IN_PROJ_PAD)), full((1, Q_LORA)), full((Q_LORA, 2 * N_HEADS * HEAD_PAD)),
            full((1, KV_LORA)), full((KV_LORA, N_HEADS * HEAD_PAD)), full((KV_LORA, ATTN_WIDTH)),
            full((2, HEAD_PAD)), full((2, HEAD_PAD)),
        ],
        out_specs=[
            pl.BlockSpec((1, N_HEADS, tm, HEAD_PAD), lambda b, t: (b, 0, t, 0)),
            pl.BlockSpec((1, N_HEADS, tm, HEAD_PAD), lambda b, t: (b, 0, t, 0)),
            pl.BlockSpec((1, tm, ATTN_WIDTH), lambda b, t: (b, t, 0)),
            pl.BlockSpec((tm, SSM_WIDTH), lambda b, t: (t, b)),
            pl.BlockSpec((1, tm, CONV_WIDTH), lambda b, t: (b, t, 0)),
            pl.BlockSpec((1, tm, CONV_WIDTH), lambda b, t: (b, t, 0)),
        ],
        out_shape=[
            jax.ShapeDtypeStruct((B, N_HEADS, L, HEAD_PAD), BF16),
            jax.ShapeDtypeStruct((B, N_HEADS, L, HEAD_PAD), BF16),
            jax.ShapeDtypeStruct((B, L, ATTN_WIDTH), BF16),
            jax.ShapeDtypeStruct((L, B * SSM_WIDTH), BF16),
            jax.ShapeDtypeStruct((B, L, CONV_WIDTH), BF16),
            jax.ShapeDtypeStruct((B, L, CONV_WIDTH), BF16),
        ],
        compiler_params=_params("parallel", "parallel"),
    )(x, rope, mixn, win, qan, wuq, kvan, wuk, wuv, qn, kn)


def _attention_kernel(q_ref, k_ref, v_ref, o_ref, vext_ref, *, seq, tq):
    pair = 2 * V_HEAD
    vext_ref[:, 0:pair] = v_ref[0]
    vext_ref[:, pair:] = jnp.ones((seq, LANES), BF16)
    row_chunk = lax.broadcasted_iota(jnp.int32, (tq, tq), 0) // CHUNK
    col_chunk = lax.broadcasted_iota(jnp.int32, (tq, tq), 1) // CHUNK
    visible = col_chunk <= row_chunk
    neg = jnp.finfo(F32).min
    nt = (((1,), (1,)), ((), ()))
    low_lanes = lax.broadcasted_iota(jnp.int32, (tq, pair), 1) < V_HEAD
    for i in range(seq // tq):
        q0 = i * tq
        outs = []
        for hh in range(2):
            qb = q_ref[0, hh, q0:q0 + tq, :]
            s_d = lax.dot_general(qb, k_ref[0, hh, q0:q0 + tq, :], nt, preferred_element_type=F32)
            s_d = jnp.where(visible, s_d, neg)
            m = jnp.max(s_d, axis=-1, keepdims=True)
            if i > 0:
                s_l = lax.dot_general(qb, k_ref[0, hh, 0:q0, :], nt, preferred_element_type=F32)
                m = jnp.maximum(m, jnp.max(s_l, axis=-1, keepdims=True))
            acc = jnp.dot(jnp.exp(s_d - m).astype(BF16), vext_ref[q0:q0 + tq, :], preferred_element_type=F32)
            if i > 0:
                acc = acc + jnp.dot(jnp.exp(s_l - m).astype(BF16), vext_ref[0:q0, :], preferred_element_type=F32)
            outs.append(acc[:, 0:pair] / acc[:, pair:])
        o_ref[0, q0:q0 + tq, :] = jnp.where(low_lanes, outs[0], outs[1]).astype(BF16)


def _attention(q, k, v):
    B, H, L, _ = q.shape
    tq = min(TQ_ATTN, L)
    return pl.pallas_call(
        functools.partial(_attention_kernel, seq=L, tq=tq),
        grid=(B, H // 2),
        in_specs=[
            pl.BlockSpec((1, 2, L, HEAD_PAD), lambda b, p: (b, p, 0, 0)),
            pl.BlockSpec((1, 2, L, HEAD_PAD), lambda b, p: (b, p, 0, 0)),
            pl.BlockSpec((1, L, 2 * V_HEAD), lambda b, p: (b, 0, p)),
        ],
        out_specs=pl.BlockSpec((1, L, 2 * V_HEAD), lambda b, p: (b, 0, p)),
        out_shape=jax.ShapeDtypeStruct((B, L, ATTN_WIDTH), BF16),
        scratch_shapes=[pltpu.VMEM((L, 2 * V_HEAD + LANES), BF16)],
        compiler_params=_params("parallel", "parallel"),
    )(q, k, v)


def _ssm_kernel(u_ref, bbd_ref, lam_ref, cbd_ref, d_ref, wglu_ref, g_ref, o_ref, state_ref, bu_a, bu_b, *, batch, tc):
    @pl.when(pl.program_id(0) == 0)
    def _():
        state_ref[...] = jnp.zeros_like(state_ref)

    half = tc // 2 * batch
    halves = ((bu_a, slice(0, half)), (bu_b, slice(half, 2 * half)))
    for bu_ref, rows in halves:
        bu_ref[...] = jnp.dot(u_ref[rows, :], bbd_ref[...], preferred_element_type=F32)
    lam_re = jnp.broadcast_to(lam_ref[0:1, :], (batch, SSM_FLAT))
    lam_im = jnp.broadcast_to(lam_ref[1:2, :], (batch, SSM_FLAT))

    carry = (state_ref[:, 0:SSM_FLAT], state_ref[:, SSM_FLAT:2 * SSM_FLAT])
    for bu_ref, _ in halves:
        def step(t, xs, bu_ref=bu_ref):
            xr, xi = xs
            at = pl.ds(pl.multiple_of(t * batch, batch), batch)
            nr = lam_re * xr - lam_im * xi + bu_ref[at, 0:SSM_FLAT]
            ni = lam_re * xi + lam_im * xr + bu_ref[at, SSM_FLAT:2 * SSM_FLAT]
            bu_ref[at, 0:SSM_FLAT] = nr
            bu_ref[at, SSM_FLAT:2 * SSM_FLAT] = ni
            return nr, ni

        carry = lax.fori_loop(0, tc // 2, step, carry, unroll=True)
    state_ref[:, 0:SSM_FLAT] = carry[0]
    state_ref[:, SSM_FLAT:2 * SSM_FLAT] = carry[1]

    for bu_ref, rows in halves:
        y = jnp.dot(bu_ref[...].astype(BF16), cbd_ref[...], preferred_element_type=F32)
        y = y + d_ref[...] * u_ref[rows, :].astype(F32)
        z = jax.nn.gelu(y)
        gate = jax.nn.sigmoid(jnp.dot(z.astype(BF16), wglu_ref[...], preferred_element_type=F32))
        o_ref[rows, :] = _rms(z * gate, g_ref[...]).astype(BF16)


def _ssm(u_tm, bbd, lam, cbd, d, wglu, g, batch):
    rows = u_tm.shape[0]
    seq = rows // batch
    tc = min(TC_SSM, seq)
    blk = tc * batch
    full = lambda shape: pl.BlockSpec(shape, lambda t: (0,) * len(shape))
    return pl.pallas_call(
        functools.partial(_ssm_kernel, batch=batch, tc=tc),
        grid=(seq // tc,),
        in_specs=[
            pl.BlockSpec((blk, SSM_WIDTH), lambda t: (t, 0)),
            full((SSM_WIDTH, 2 * SSM_FLAT)), full((2, SSM_FLAT)), full((2 * SSM_FLAT, SSM_WIDTH)),
            full((1, SSM_WIDTH)), full((SSM_WIDTH, SSM_WIDTH)), full((1, SSM_WIDTH)),
        ],
        out_specs=pl.BlockSpec((blk, SSM_WIDTH), lambda t: (t, 0)),
        out_shape=jax.ShapeDtypeStruct((rows, SSM_WIDTH), BF16),
        scratch_shapes=[pltpu.VMEM((batch, 2 * SSM_FLAT), F32), pltpu.VMEM((blk // 2, 2 * SSM_FLAT), F32),
                        pltpu.VMEM((blk // 2, 2 * SSM_FLAT), F32)],
        compiler_params=_params("arbitrary"),
    )(u_tm, bbd, lam, cbd, d, wglu, g)


def _conv_kernel(a_ref, g_ref, ah_ref, gh_ref, w_ref, b_ref, lnw_ref, lnb_ref, gn_ref, o_ref, upad_ref, *, tt):
    first = pl.program_id(1) == 0
    halo = ah_ref[0].astype(F32) * jax.nn.sigmoid(gh_ref[0].astype(F32))
    upad_ref[0:CONV_HALO, :] = jnp.where(first, 0.0, halo)
    upad_ref[CONV_HALO:, :] = a_ref[0].astype(F32) * jax.nn.sigmoid(g_ref[0].astype(F32))
    base = CONV_HALO - (CONV_K - 1)
    for s in range(tt // CONV_SUB):
        r0 = s * CONV_SUB
        acc = jnp.zeros((CONV_SUB, CONV_WIDTH), F32)
        for off in range(SUBLANES):
            n = CONV_SUB + (SUBLANES if off else 0)
            part = None
            for a8 in range(0, CONV_HALO + 1, SUBLANES):
                kk = a8 + off - base
                if 0 <= kk < CONV_K:
                    term = w_ref[kk:kk + 1, :] * upad_ref[r0 + a8:r0 + a8 + n, :]
                    part = term if part is None else part + term
            acc = acc + part[off:off + CONV_SUB]
        y = acc + b_ref[...]
        mu = jnp.mean(y, axis=-1, keepdims=True)
        var = jnp.mean(jnp.square(y - mu), axis=-1, keepdims=True)
        y = (y - mu) * lax.rsqrt(var + 1e-5) * lnw_ref[...] + lnb_ref[...]
        y = y * jax.nn.sigmoid(y)
        o_ref[0, r0:r0 + CONV_SUB, :] = _rms(y, gn_ref[...]).astype(BF16)


def _conv(a, g, w, b, lnw, lnb, gn):
    B, L, C = a.shape
    tt = min(T_CONV, L)
    ratio = tt // CONV_HALO
    full = lambda shape: pl.BlockSpec(shape, lambda bb, t: (0,) * len(shape))
    cur = pl.BlockSpec((1, tt, C), lambda bb, t: (bb, t, 0))
    prev = pl.BlockSpec((1, CONV_HALO, C), lambda bb, t: (bb, jnp.maximum(t * ratio - 1, 0), 0))
    return pl.pallas_call(
        functools.partial(_conv_kernel, tt=tt),
        grid=(B, L // tt),
        in_specs=[cur, cur, prev, prev, full((CONV_K, C)), full((1, C)), full((1, C)), full((1, C)), full((1, C))],
        out_specs=pl.BlockSpec((1, tt, C), lambda bb, t: (bb, t, 0)),
        out_shape=jax.ShapeDtypeStruct((B, L, C), BF16),
        scratch_shapes=[pltpu.VMEM((tt + CONV_HALO, C), F32)],
        compiler_params=_params("parallel", "parallel"),
    )(a, g, a, g, w, b, lnw, lnb, gn)


ROUTE_E1, ROUTE_E2, ROUTE_R1, ROUTE_R2, ROUTE_G1, ROUTE_G2 = range(6)
ROUTE_ROWS = 8
GRP_LANE0 = N_EXPERTS


def _out_router_kernel(x_ref, ya_ref, ys_ref, yc_ref, ga_ref, wo_ref, fn_ref, wr_ref, br_ref,
                       x1_ref, h2_ref, route_ref, route_t_ref, cnt_ref, sub_ref, carry_ref, *, tm):
    @pl.when(pl.program_id(0) == 0)
    def _():
        carry_ref[...] = jnp.zeros_like(carry_ref)

    ya = _rms(ya_ref[...].astype(F32), ga_ref[...]).astype(BF16)
    acc = jnp.dot(ya, wo_ref[0:ATTN_WIDTH, :], preferred_element_type=F32)
    acc += jnp.dot(ys_ref[...], wo_ref[ATTN_WIDTH:ATTN_WIDTH + SSM_WIDTH, :], preferred_element_type=F32)
    acc += jnp.dot(yc_ref[...], wo_ref[ATTN_WIDTH + SSM_WIDTH:, :], preferred_element_type=F32)
    x1 = x_ref[...] + acc
    x1_ref[...] = x1
    h2 = _rms(x1, fn_ref[...])
    h2_ref[...] = _pack_rows(h2)

    h_hi = h2.astype(BF16)
    h_lo = (h2 - h_hi.astype(F32)).astype(BF16)
    part = jnp.dot(h_hi, wr_ref[...], preferred_element_type=F32)
    logits = (part[:, 0:LANES] + part[:, LANES:2 * LANES]
              + jnp.dot(h_lo, wr_ref[:, 0:LANES], preferred_element_type=F32) + br_ref[...])
    lane = lax.broadcasted_iota(jnp.int32, (tm, LANES), 1)
    ninf = -jnp.inf
    big = LANES

    def first_argmax(vals, vmax):
        return jnp.min(jnp.where(vals == vmax, lane, big), axis=-1, keepdims=True)

    grp = jnp.where((lane >= GRP_LANE0) & (lane < GRP_LANE0 + N_EGROUPS), logits, ninf)
    gmax = jnp.max(grp, axis=-1, keepdims=True)
    gsel = first_argmax(grp, gmax) - GRP_LANE0
    p_grp = 1.0 / jnp.sum(jnp.exp(grp - gmax), axis=-1, keepdims=True)

    el = jnp.where((lane < N_EXPERTS) & ((lane // EXP_PER_GROUP) == gsel), logits, ninf)
    m1 = jnp.max(el, axis=-1, keepdims=True)
    e1 = first_argmax(el, m1)
    el2 = jnp.where(lane == e1, ninf, el)
    m2 = jnp.max(el2, axis=-1, keepdims=True)
    e2 = first_argmax(el2, m2)
    t2 = jnp.exp(m2 - m1)
    g1 = p_grp / (1.0 + t2)
    g2 = p_grp * t2 / (1.0 + t2)

    hit1 = lane == e1
    hit2 = lane == e2
    cnt = (hit1 | hit2).astype(F32)
    rr = lax.broadcasted_iota(jnp.int32, (tm, tm), 0)
    cc = lax.broadcasted_iota(jnp.int32, (tm, tm), 1)
    tri = (cc < rr).astype(BF16)
    before = jnp.dot(tri, cnt.astype(BF16), preferred_element_type=F32) + carry_ref[...]
    r1 = jnp.sum(jnp.where(hit1, before, 0.0), axis=-1, keepdims=True)
    r2 = jnp.sum(jnp.where(hit2, before, 0.0), axis=-1, keepdims=True)
    subs = [before[h * TM_MOVE:h * TM_MOVE + 1, :] for h in range(tm // TM_MOVE)]
    sub_ref[...] = jnp.concatenate(subs + [jnp.zeros((SUBLANES - len(subs), LANES), F32)], axis=0)
    carry_ref[...] += jnp.sum(cnt, axis=0, keepdims=True)
    cnt_ref[...] = carry_ref[...]

    rec = jnp.where(lane == ROUTE_E1, e1.astype(F32), 0.0)
    rec = jnp.where(lane == ROUTE_E2, e2.astype(F32), rec)
    rec = jnp.where(lane == ROUTE_R1, r1, rec)
    rec = jnp.where(lane == ROUTE_R2, r2, rec)
    rec = jnp.where(lane == ROUTE_G1, g1, rec)
    rec = jnp.where(lane == ROUTE_G2, g2, rec)
    route_ref[...] = rec
    route_t_ref[...] = rec.T[0:ROUTE_ROWS, :]


def _out_router(x, ya, ys_tm, yc, ga, wo, fn, wr, br, batch):
    N, D = x.shape
    seq = N // batch
    tm = min(TM_OUT, seq)
    per_b = seq // tm
    full = lambda shape: pl.BlockSpec(shape, lambda i: (0,) * len(shape))
    tile = lambda w: pl.BlockSpec((tm, w), lambda i: (i, 0))
    return pl.pallas_call(
        functools.partial(_out_router_kernel, tm=tm),
        grid=(N // tm,),
        in_specs=[
            tile(D), tile(ATTN_WIDTH),
            pl.BlockSpec((tm, SSM_WIDTH), lambda i: (i % per_b, i // per_b)),
            tile(CONV_WIDTH),
            full((1, ATTN_WIDTH)), full((D, D)), full((1, D)), full((D, 2 * LANES)), full((1, LANES)),
        ],
        out_specs=[tile(D), tile(D_PACK), tile(LANES), pl.BlockSpec((ROUTE_ROWS, tm), lambda i: (0, i)),
                   full((1, LANES)), pl.BlockSpec((SUBLANES, LANES), lambda i: (i, 0))],
        out_shape=[
            jax.ShapeDtypeStruct((N, D), F32),
            jax.ShapeDtypeStruct((N, D_PACK), U32),
            jax.ShapeDtypeStruct((N, LANES), F32),
            jax.ShapeDtypeStruct((ROUTE_ROWS, N), F32),
            jax.ShapeDtypeStruct((1, LANES), F32),
            jax.ShapeDtypeStruct((N // tm * SUBLANES, LANES), F32),
        ],
        scratch_shapes=[pltpu.VMEM((1, LANES), F32)],
        compiler_params=_params("arbitrary"),
    )(x, ya, ys_tm, yc, ga, wo, fn, wr, br)


def _dispatch_kernel(dest_ref, h_ref, rows_ref, sem, *, tm, n_tok):
    base = pl.program_id(0) * tm

    def issue(r, c):
        src = h_ref.at[pl.ds(r, 1)]
        pltpu.make_async_copy(src, rows_ref.at[pl.ds(dest_ref[base + r], 1)], sem).start()
        pltpu.make_async_copy(src, rows_ref.at[pl.ds(dest_ref[n_tok + base + r], 1)], sem).start()
        return c

    lax.fori_loop(0, tm, issue, 0, unroll=DMA_UNROLL)
    for _ in range(2):
        pltpu.make_async_copy(h_ref, rows_ref.at[pl.ds(0, tm)], sem).wait()


def _dispatch(dest_flat, h2, n_rows):
    N, D = h2.shape
    tm = min(TM_MOVE, N)
    return pl.pallas_call(
        functools.partial(_dispatch_kernel, tm=tm, n_tok=N),
        grid_spec=pltpu.PrefetchScalarGridSpec(
            num_scalar_prefetch=1,
            grid=(N // tm,),
            in_specs=[pl.BlockSpec((tm, D), lambda i, d: (i, 0))],
            out_specs=pl.BlockSpec(memory_space=pl.ANY),
            scratch_shapes=[pltpu.SemaphoreType.DMA(())],
        ),
        out_shape=jax.ShapeDtypeStruct((n_rows, D), h2.dtype),
        compiler_params=_params("arbitrary"),
    )(dest_flat, h2)


def _experts_kernel(blk_ref, exp_ref, valid_ref, used_ref, x_ref, w1_ref, w3_ref, w2_ref, y_ref, w13_s, w2_s):
    j = pl.program_id(0)

    @pl.when(j < used_ref[0])
    def _():
        prev = exp_ref[jnp.maximum(j - 1, 0)]

        @pl.when((j == 0) | (exp_ref[j] != prev))
        def _():
            for c in range(D_FF_E // FF_CHUNK):
                w13_s[:, 2 * FF_CHUNK * c:2 * FF_CHUNK * c + FF_CHUNK] = \
                    w1_ref[0, 0, :, FF_CHUNK * c:FF_CHUNK * (c + 1)].astype(BF16)
                w13_s[:, 2 * FF_CHUNK * c + FF_CHUNK:2 * FF_CHUNK * (c + 1)] = \
                    w3_ref[0, 0, :, FF_CHUNK * c:FF_CHUNK * (c + 1)].astype(BF16)
            w2_s[...] = w2_ref[0, 0].astype(BF16)

        x = _unpack_rows(x_ref[...])
        y = None
        for c in range(D_FF_E // FF_CHUNK):
            ab = jnp.dot(x, w13_s[:, 2 * FF_CHUNK * c:2 * FF_CHUNK * (c + 1)], preferred_element_type=F32)
            a = ab[:, 0:FF_CHUNK]
            hmid = (a * jax.nn.sigmoid(a) * ab[:, FF_CHUNK:]).astype(BF16)
            part = jnp.dot(hmid, w2_s[FF_CHUNK * c:FF_CHUNK * (c + 1), :], preferred_element_type=F32)
            y = part if y is None else y + part
        row = lax.broadcasted_iota(jnp.int32, (TB_EXP, 1), 0)
        y_ref[...] = jnp.where(row < valid_ref[j], _pack_rows(y), jnp.uint32(0))


def _experts(blk_map, blk_exp, blk_valid, n_used, x_rows, w1, w3, w2, layer):
    R, D = x_rows.shape[0], D_MODEL
    nb = R // TB_EXP
    return pl.pallas_call(
        _experts_kernel,
        grid_spec=pltpu.PrefetchScalarGridSpec(
            num_scalar_prefetch=4,
            grid=(nb,),
            in_specs=[
                pl.BlockSpec((TB_EXP, D_PACK), lambda j, bm, be, bv, nu: (bm[j], 0)),
                pl.BlockSpec((1, 1, D, D_FF_E), lambda j, bm, be, bv, nu: (layer, be[j], 0, 0)),
                pl.BlockSpec((1, 1, D, D_FF_E), lambda j, bm, be, bv, nu: (layer, be[j], 0, 0)),
                pl.BlockSpec((1, 1, D_FF_E, D), lambda j, bm, be, bv, nu: (layer, be[j], 0, 0)),
            ],
            out_specs=pl.BlockSpec((TB_EXP, D_PACK), lambda j, bm, be, bv, nu: (bm[j], 0)),
            scratch_shapes=[pltpu.VMEM((D, 2 * D_FF_E), BF16), pltpu.VMEM((D_FF_E, D), BF16)],
        ),
        out_shape=jax.ShapeDtypeStruct((R, D_PACK), U32),
        compiler_params=_params("arbitrary"),
    )(blk_map, blk_exp, blk_valid, n_used, x_rows, w1, w3, w2)


def _combine_kernel(src_ref, nslot_ref, x1_ref, route_ref, off_ref, rows_ref, o_ref, buf, sem, *, tm):
    i = pl.program_id(0)
    cur = i % 2

    def slot_copy(tile, half, s):
        src = pl.multiple_of(src_ref[tile * RUN_SLOTS + s], RUN_ROWS)
        dst = buf.at[half, pl.ds(pl.multiple_of(s * RUN_ROWS, RUN_ROWS), RUN_ROWS)]
        return pltpu.make_async_copy(rows_ref.at[pl.ds(src, RUN_ROWS)], dst, sem.at[half])

    def fetch(tile, half):
        def issue(g, c):
            for u in range(RUN_GROUP):
                slot_copy(tile, half, g * RUN_GROUP + u).start()
            return c
        lax.fori_loop(0, nslot_ref[tile], issue, 0)

    def wait(g, c):
        rows = RUN_GROUP * RUN_ROWS
        pltpu.make_async_copy(rows_ref.at[pl.ds(0, rows)], buf.at[cur, pl.ds(0, rows)], sem.at[cur]).wait()
        return c

    @pl.when(i == 0)
    def _():
        buf[...] = jnp.zeros_like(buf)
        fetch(0, 0)

    @pl.when(i + 1 < pl.num_programs(0))
    def _():
        fetch(i + 1, 1 - cur)

    route = route_ref[...]
    lane = lax.broadcasted_iota(jnp.int32, (tm, LANES), 1).astype(F32)
    off = off_ref[0]

    def buf_row(e_lane, r_lane):
        e = route[:, e_lane:e_lane + 1]
        base = jnp.sum(jnp.where(lane == e, off, 0.0), axis=-1, keepdims=True)
        return (base + route[:, r_lane:r_lane + 1]).astype(jnp.int32)

    col = lax.broadcasted_iota(jnp.int32, (tm, RUN_SLOTS * RUN_ROWS), 1)
    pick = (jnp.where(col == buf_row(ROUTE_E1, ROUTE_R1), route[:, ROUTE_G1:ROUTE_G1 + 1], 0.0)
            + jnp.where(col == buf_row(ROUTE_E2, ROUTE_R2), route[:, ROUTE_G2:ROUTE_G2 + 1], 0.0))
    lax.fori_loop(0, nslot_ref[i], wait, 0)
    o_ref[...] = x1_ref[...] + jnp.dot(pick.astype(BF16), _unpack_rows(buf[cur]), preferred_element_type=F32)


def _combine(src, nslot, x1, route, off, y_rows):
    N, D = x1.shape
    tm = min(TM_MOVE, N)
    return pl.pallas_call(
        functools.partial(_combine_kernel, tm=tm),
        grid_spec=pltpu.PrefetchScalarGridSpec(
            num_scalar_prefetch=2,
            grid=(N // tm,),
            in_specs=[
                pl.BlockSpec((tm, D), lambda i, s, n: (i, 0)),
                pl.BlockSpec((tm, LANES), lambda i, s, n: (i, 0)),
                pl.BlockSpec((1, 1, LANES), lambda i, s, n: (i, 0, 0)),
                pl.BlockSpec(memory_space=pl.ANY),
            ],
            out_specs=pl.BlockSpec((tm, D), lambda i, s, n: (i, 0)),
            scratch_shapes=[pltpu.VMEM((2, RUN_SLOTS * RUN_ROWS, D_PACK), U32), pltpu.SemaphoreType.DMA((2,))],
        ),
        out_shape=jax.ShapeDtypeStruct((N, D), F32),
        compiler_params=_params("arbitrary"),
    )(src, nslot, x1, route, off, y_rows)


def _moe(x1, h2, route, route_t, counts, sub_carry, w1, w3, w2, layer):
    N, _ = x1.shape
    tm = min(TM_MOVE, N)
    n_tiles = N // tm
    experts = jnp.arange(N_EXPERTS, dtype=jnp.int32)
    nb = (2 * N + N_EXPERTS * (TB_EXP - 1)) // TB_EXP + 1
    e_id = route_t[ROUTE_E1:ROUTE_E2 + 1].astype(jnp.int32)
    rank = route_t[ROUTE_R1:ROUTE_R2 + 1].astype(jnp.int32)
    cnt = counts[0, :N_EXPERTS].astype(jnp.int32)
    padded = (cnt + TB_EXP - 1) // TB_EXP * TB_EXP
    pad_end = jnp.cumsum(padded)
    pad_start = pad_end - padded
    e_flat = e_id.reshape(1, 2 * N)
    dest = jnp.sum(jnp.where(e_flat == experts[:, None], pad_start[:, None], 0), axis=0) + rank.reshape(2 * N)
    n_used = pad_end[-1] // TB_EXP
    blk_map = jnp.minimum(jnp.arange(nb, dtype=jnp.int32), jnp.maximum(n_used - 1, 0))
    blk_exp = jnp.sum(pad_end[None, :] <= (blk_map * TB_EXP)[:, None], axis=1).astype(jnp.int32)
    blk_exp = jnp.minimum(blk_exp, N_EXPERTS - 1)
    is_blk_e = blk_exp[:, None] == experts[None, :]
    blk_valid = jnp.clip(jnp.sum(jnp.where(is_blk_e, (pad_start + cnt)[None, :], 0), axis=1) - blk_map * TB_EXP,
                         0, TB_EXP).astype(jnp.int32)

    per_router_tile = TM_OUT // tm if N >= TM_OUT else 1
    carry = sub_carry.reshape(-1, SUBLANES, LANES)[:, :per_router_tile, :N_EXPERTS].reshape(n_tiles, N_EXPERTS)
    carry = carry.astype(jnp.int32)
    tile_cnt = jnp.concatenate([carry[1:], cnt[None, :]], axis=0) - carry
    run_start = pad_start[None, :] + carry
    first_blk = run_start // RUN_ROWS
    nslot = jnp.where(tile_cnt > 0, (run_start + tile_cnt - 1) // RUN_ROWS - first_blk + 1, 0)
    slot_end = jnp.cumsum(nslot, axis=1)
    slot_base = slot_end - nslot
    slots = jnp.arange(RUN_SLOTS, dtype=jnp.int32)
    slot_e = jnp.minimum(jnp.sum(slot_end[:, None, :] <= slots[None, :, None], axis=2), N_EXPERTS - 1)
    is_slot_e = slot_e[:, :, None] == experts[None, None, :]
    pick = lambda tbl: jnp.sum(jnp.where(is_slot_e, tbl[:, None, :], 0), axis=2)
    src = (pick(first_blk) + slots[None, :] - pick(slot_base)) * RUN_ROWS
    src = jnp.where(slots[None, :] < slot_end[:, -1:], src, 0)
    src = jnp.clip(src, 0, nb * TB_EXP - RUN_ROWS).reshape(-1).astype(jnp.int32)
    n_groups = (slot_end[:, -1] + RUN_GROUP - 1) // RUN_GROUP
    off = (slot_base - first_blk) * RUN_ROWS + pad_start[None, :]
    off = jnp.pad(off.astype(F32), ((0, 0), (0, LANES - N_EXPERTS)))

    x_rows = _dispatch(dest, h2, nb * TB_EXP)
    y_rows = _experts(blk_map, blk_exp, blk_valid, n_used.reshape(1).astype(jnp.int32), x_rows, w1, w3, w2, layer)
    return _combine(src, n_groups.astype(jnp.int32), x1, route, off.reshape(n_tiles, 1, LANES), y_rows)


def _pad_heads(w, width):
    k = w.shape[0]
    w = w.reshape(k, N_HEADS, width)
    return jnp.pad(w, ((0, 0), (0, 0), (0, HEAD_PAD - width))).reshape(k, N_HEADS * HEAD_PAD)


def _swap_rope(w):
    half = QK_ROPE // 2
    lo, hi = w[..., QK_NOPE:QK_NOPE + half], w[..., QK_NOPE + half:QK_HEAD]
    pad = [(0, 0)] * (w.ndim - 1)
    return jnp.pad(jnp.concatenate([hi, lo], -1), pad + [(QK_NOPE, HEAD_PAD - QK_HEAD)])


def _rope_tables(positions):
    inv_freq = ROPE_THETA ** (-jnp.arange(0, QK_ROPE, 2, dtype=F32) / QK_ROPE)
    ang = positions.astype(F32)[..., None] * inv_freq
    table = jnp.concatenate([jnp.cos(ang), jnp.sin(ang)], -1)
    return jnp.pad(table, ((0, 0), (0, 0), (QK_NOPE, LANES - QK_HEAD)))


def _ssm_params(lam_re, lam_im, b_re, b_im, c_re, c_im, log_dt):
    lam = lax.complex(lam_re, lam_im)
    dt = jnp.exp(log_dt)[:, None]
    lam_bar = jnp.exp(lam * dt)
    b_bar = ((lam_bar - 1.0) / lam)[..., None] * lax.complex(b_re, b_im)
    eye = jnp.eye(SSM_GROUPS, dtype=F32)

    def in_blockdiag(m):
        return jnp.einsum("gpc,gh->gchp", m, eye).reshape(SSM_WIDTH, SSM_FLAT)

    def out_blockdiag(m):
        return jnp.einsum("gcp,gh->gphc", m, eye).reshape(SSM_FLAT, SSM_WIDTH)

    bbd = jnp.concatenate([in_blockdiag(jnp.real(b_bar)), in_blockdiag(jnp.imag(b_bar))], axis=1)
    cbd = jnp.concatenate([out_blockdiag(c_re), out_blockdiag(-c_im)], axis=0)
    lam_rows = jnp.stack([jnp.real(lam_bar).reshape(SSM_FLAT), jnp.imag(lam_bar).reshape(SSM_FLAT)])
    return bbd.astype(BF16), lam_rows, cbd.astype(BF16)


def kernel(x, positions, mix_norm, w_in, q_a_norm, w_uq, kv_a_norm, w_ukv, q_norm, k_norm, ssm_lam_re, ssm_lam_im, ssm_b_re, ssm_b_im, ssm_c_re, ssm_c_im, ssm_d, ssm_log_dt, ssm_w_glu, conv_dw_w, conv_dw_b, conv_ln_w, conv_ln_b, out_norm, w_out, ffn_norm, w_grp, b_grp, w_exp, b_exp, w1, w3, w2):
    B, L, D = x.shape
    depth = w_in.shape[0]
    rope = _rope_tables(positions)
    row = lambda v: v.reshape(1, -1)
    lane_pad = lambda v: jnp.pad(v, (0, LANES - v.shape[0])).reshape(1, LANES)
    for l in range(depth):
        c_q, c_kv, k_pe, u_s, c_a, c_g = jnp.split(
            w_in[l], [Q_LORA, Q_LORA + KV_LORA, Q_LORA + KV_LORA + QK_ROPE,
                      Q_LORA + KV_LORA + QK_ROPE + SSM_WIDTH,
                      Q_LORA + KV_LORA + QK_ROPE + SSM_WIDTH + CONV_WIDTH], axis=1)
        k_pe_full = jnp.pad(k_pe, ((0, 0), (QK_NOPE, 0)))
        win = jnp.concatenate([c_q, c_kv, u_s, c_a, c_g, jnp.pad(k_pe_full, ((0, 0), (0, HEAD_PAD - QK_HEAD))),
                               _swap_rope(k_pe_full)], axis=1).astype(BF16)
        wkv = w_ukv[l].reshape(KV_LORA, N_HEADS, QK_NOPE + V_HEAD)
        wuk = _pad_heads(wkv[:, :, :QK_NOPE].reshape(KV_LORA, N_HEADS * QK_NOPE), QK_NOPE).astype(BF16)
        wuv = wkv[:, :, QK_NOPE:].reshape(KV_LORA, ATTN_WIDTH).astype(BF16)
        wuq_sw = _swap_rope(w_uq[l].reshape(Q_LORA, N_HEADS, QK_HEAD)).reshape(Q_LORA, N_HEADS * HEAD_PAD)
        wuq = jnp.concatenate([_pad_heads(w_uq[l], QK_HEAD), wuq_sw], axis=1).astype(BF16)
        qn_scaled = q_norm[l] * (QK_HEAD ** -0.5)
        norm_rows = lambda w: jnp.stack([jnp.pad(w, (0, HEAD_PAD - QK_HEAD)), _swap_rope(w)])
        q, k, v, u_tm, ca, cg = _in_proj(
            x, rope, row(mix_norm[l]), win, row(q_a_norm[l]), wuq, row(kv_a_norm[l]), wuk, wuv,
            norm_rows(qn_scaled), norm_rows(k_norm[l]))
        y_attn = _attention(q, k, v)

        g_out = out_norm[l]
        bbd, lam_rows, cbd = _ssm_params(ssm_lam_re[l], ssm_lam_im[l], ssm_b_re[l], ssm_b_im[l],
                                         ssm_c_re[l], ssm_c_im[l], ssm_log_dt[l])
        y_ssm_tm = _ssm(u_tm.reshape(L * B, SSM_WIDTH), bbd, lam_rows, cbd, row(ssm_d[l]),
                        ssm_w_glu[l].astype(BF16), row(g_out[ATTN_WIDTH:ATTN_WIDTH + SSM_WIDTH]), B)
        y_conv = _conv(ca, cg, conv_dw_w[l], row(conv_dw_b[l]), row(conv_ln_w[l]), row(conv_ln_b[l]),
                       row(g_out[ATTN_WIDTH + SSM_WIDTH:]))

        w_route = jnp.pad(jnp.concatenate([w_exp[l], w_grp[l]], axis=1), ((0, 0), (0, LANES - N_EXPERTS - N_EGROUPS)))
        w_route_hi = w_route.astype(BF16)
        w_route = jnp.concatenate([w_route_hi, (w_route - w_route_hi.astype(F32)).astype(BF16)], axis=1)
        b_route = lane_pad(jnp.concatenate([b_exp[l], b_grp[l]]))
        x1, h2, route, route_t, counts, sub_carry = _out_router(
            x.reshape(B * L, D), y_attn.reshape(B * L, ATTN_WIDTH), y_ssm_tm.reshape(L, B * SSM_WIDTH),
            y_conv.reshape(B * L, CONV_WIDTH), row(g_out[:ATTN_WIDTH]), w_out[l].astype(BF16),
            row(ffn_norm[l]), w_route, b_route, B)
        x = _moe(x1, h2, route, route_t, counts, sub_carry, w1, w3, w2, l).reshape(B, L, D)
    return x
```

```python
import functools
import math

import jax
import jax.numpy as jnp
from jax import lax
from jax.experimental import pallas as pl
from jax.experimental.pallas import tpu as pltpu
from jax.experimental.pallas import tpu_sc as plsc

D_MODEL = 1024
CHUNK = 64
EPS = 1e-6
N_HEADS = 8
QK_NOPE = 64
QK_ROPE = 32
QK_HEAD = QK_NOPE + QK_ROPE
V_HEAD = 64
Q_LORA = 256
KV_LORA = 128
ROPE_THETA = 10000.0
ATTN_WIDTH = N_HEADS * V_HEAD
SSM_WIDTH = 256
SSM_GROUP = 16
SSM_GROUPS = SSM_WIDTH // SSM_GROUP
SSM_STATE = 64
SSM_FLAT = SSM_GROUPS * SSM_STATE
CONV_WIDTH = 256
CONV_K = 31
N_EGROUPS = 4
EXP_PER_GROUP = 8
N_EXPERTS = N_EGROUPS * EXP_PER_GROUP
D_FF_E = 512

LANES = 128
SUBLANES = 8
HEAD_PAD = LANES
IN_PROJ_PAD = Q_LORA + KV_LORA + SSM_WIDTH + 2 * CONV_WIDTH + 2 * LANES
CONV_HALO = 32
VMEM_LIMIT = 48 * 1024 * 1024

TM_PROJ = 512
TQ_ATTN = 256
TC_SSM = 64
T_CONV = 256
CONV_SUB = 64
TM_OUT = 512
TM_MOVE = 256
RUN_ROWS = SUBLANES
RUN_SLOTS = 2 * N_EXPERTS + 2 * TM_MOVE // RUN_ROWS
RUN_GROUP = 8
SC_CORES, SC_SUBCORES = 2, 16
SC_CHUNK = 64
TB_EXP = 512
FF_CHUNK = 256

BF16 = jnp.bfloat16
F32 = jnp.float32
U32 = jnp.uint32
D_PACK = D_MODEL // 2


def _pack_rows(v):
    bits = lax.bitcast_convert_type(v.astype(BF16).astype(F32), U32)
    half = v.shape[1] // 2
    return bits[:, 0:half] | (bits[:, half:] >> 16)


def _unpack_rows(w):
    hi = lax.bitcast_convert_type(w & jnp.uint32(0xFFFF0000), F32)
    lo = lax.bitcast_convert_type(w << 16, F32)
    return jnp.concatenate([hi, lo], axis=1).astype(BF16)


def _rms(x, w):
    return x * lax.rsqrt(jnp.mean(x * x, axis=-1, keepdims=True) + EPS) * w


def _params(*sem):
    return pltpu.CompilerParams(dimension_semantics=sem, vmem_limit_bytes=VMEM_LIMIT)


def _in_proj_kernel(x_ref, rope_ref, mixn_ref, win_ref, qan_ref, wuq_ref, kvan_ref, wuk_ref, wuv_ref,
                    qn_ref, kn_ref, q_ref, k_ref, v_ref, u_ref, ca_ref, cg_ref):
    x = x_ref[0]
    h = _rms(x, mixn_ref[...]).astype(BF16)
    proj = jnp.dot(h, win_ref[...], preferred_element_type=F32)
    o = 0
    c_q = proj[:, o:o + Q_LORA]; o += Q_LORA
    c_kv = proj[:, o:o + KV_LORA]; o += KV_LORA
    u_ref[...] = proj[:, o:o + SSM_WIDTH].astype(BF16); o += SSM_WIDTH
    ca_ref[0] = proj[:, o:o + CONV_WIDTH].astype(BF16); o += CONV_WIDTH
    cg_ref[0] = proj[:, o:o + CONV_WIDTH].astype(BF16); o += CONV_WIDTH
    k_pe = proj[:, o:o + LANES]; o += LANES
    k_pe_sw = proj[:, o:o + LANES]

    width = N_HEADS * HEAD_PAD
    q2 = jnp.dot(_rms(c_q, qan_ref[...]).astype(BF16), wuq_ref[...], preferred_element_type=F32)
    q, q_sw = q2[:, 0:width], q2[:, width:2 * width]
    ckv_n = _rms(c_kv, kvan_ref[...]).astype(BF16)
    kn = jnp.dot(ckv_n, wuk_ref[...], preferred_element_type=F32)
    v_ref[0] = jnp.dot(ckv_n, wuv_ref[...], preferred_element_type=F32).astype(BF16)

    rope = rope_ref[0]
    half = QK_ROPE // 2
    lane = lax.broadcasted_iota(jnp.int32, rope.shape, 1)
    lo = (lane >= QK_NOPE) & (lane < QK_NOPE + half)
    hi = (lane >= QK_NOPE + half) & (lane < QK_HEAD)
    cos_t = jnp.where(lane < QK_NOPE, 1.0, jnp.where(lo, rope, jnp.where(hi, pltpu.roll(rope, half, 1), 0.0)))
    sin_t = jnp.where(lo, -pltpu.roll(rope, LANES - half, 1), jnp.where(hi, rope, 0.0))
    a_q, b_q = qn_ref[0:1, :] * cos_t, qn_ref[1:2, :] * sin_t
    a_k = kn_ref[0:1, :] * cos_t
    k_sw_term = k_pe_sw * (kn_ref[1:2, :] * sin_t)

    def inv_rms(y):
        return lax.rsqrt(jnp.sum(y * y, axis=-1, keepdims=True) * (1.0 / QK_HEAD) + EPS)

    for hd in range(N_HEADS):
        sl = slice(hd * HEAD_PAD, (hd + 1) * HEAD_PAD)
        yq = q[:, sl]
        q_ref[0, hd] = ((yq * a_q + q_sw[:, sl] * b_q) * inv_rms(yq)).astype(BF16)
        yk = kn[:, sl] + k_pe
        k_ref[0, hd] = ((yk * a_k + k_sw_term) * inv_rms(yk)).astype(BF16)


def _in_proj(x, rope, mixn, win, qan, wuq, kvan, wuk, wuv, qn, kn):
    B, L, D = x.shape
    tm = min(TM_PROJ, L)
    full = lambda shape: pl.BlockSpec(shape, lambda b, t: (0,) * len(shape))
    return pl.pallas_call(
        _in_proj_kernel,
        grid=(B, L // tm),
        in_specs=[
            pl.BlockSpec((1, tm, D), lambda b, t: (b, t, 0)),
            pl.BlockSpec((1, tm, LANES), lambda b, t: (b, t, 0)),
            full((1, D)), full((D, IN_PROJ_PAD)), full((1, Q_LORA)), full((Q_LORA, 2 * N_HEADS * HEAD_PAD)),
            full((1, KV_LORA)), full((KV_LORA, N_HEADS * HEAD_PAD)), full((KV_LORA, ATTN_WIDTH)),
            full((2, HEAD_PAD)), full((2, HEAD_PAD)),
        ],
        out_specs=[
            pl.BlockSpec((1, N_HEADS, tm, HEAD_PAD), lambda b, t: (b, 0, t, 0)),
            pl.BlockSpec((1, N_HEADS, tm, HEAD_PAD), lambda b, t: (b, 0, t, 0)),
            pl.BlockSpec((1, tm, ATTN_WIDTH), lambda b, t: (b, t, 0)),
            pl.BlockSpec((tm, SSM_WIDTH), lambda b, t: (t, b)),
            pl.BlockSpec((1, tm, CONV_WIDTH), lambda b, t: (b, t, 0)),
            pl.BlockSpec((1, tm, CONV_WIDTH), lambda b, t: (b, t, 0)),
        ],
        out_shape=[
            jax.ShapeDtypeStruct((B, N_HEADS, L, HEAD_PAD), BF16),
            jax.ShapeDtypeStruct((B, N_HEADS, L, HEAD_PAD), BF16),
            jax.ShapeDtypeStruct((B, L, ATTN_WIDTH), BF16),
            jax.ShapeDtypeStruct((L, B * SSM_WIDTH), BF16),
            jax.ShapeDtypeStruct((B, L, CONV_WIDTH), BF16),
            jax.ShapeDtypeStruct((B, L, CONV_WIDTH), BF16),
        ],
        compiler_params=_params("parallel", "parallel"),
    )(x, rope, mixn, win, qan, wuq, kvan, wuk, wuv, qn, kn)


def _attention_kernel(q_ref, k_ref, v_ref, o_ref, vext_ref, *, seq, tq):
    pair = 2 * V_HEAD
    vext_ref[:, 0:pair] = v_ref[0]
    vext_ref[:, pair:] = jnp.ones((seq, LANES), BF16)
    row_chunk = lax.broadcasted_iota(jnp.int32, (tq, tq), 0) // CHUNK
    col_chunk = lax.broadcasted_iota(jnp.int32, (tq, tq), 1) // CHUNK
    visible = col_chunk <= row_chunk
    neg = jnp.finfo(F32).min
    nt = (((1,), (1,)), ((), ()))
    low_lanes = lax.broadcasted_iota(jnp.int32, (tq, pair), 1) < V_HEAD
    for i in range(seq // tq):
        q0 = i * tq
        outs = []
        for hh in range(2):
            qb = q_ref[0, hh, q0:q0 + tq, :]
            s_d = lax.dot_general(qb, k_ref[0, hh, q0:q0 + tq, :], nt, preferred_element_type=F32)
            s_d = jnp.where(visible, s_d, neg)
            m = jnp.max(s_d, axis=-1, keepdims=True)
            if i > 0:
                s_l = lax.dot_general(qb, k_ref[0, hh, 0:q0, :], nt, preferred_element_type=F32)
                m = jnp.maximum(m, jnp.max(s_l, axis=-1, keepdims=True))
            acc = jnp.dot(jnp.exp(s_d - m).astype(BF16), vext_ref[q0:q0 + tq, :], preferred_element_type=F32)
            if i > 0:
                acc = acc + jnp.dot(jnp.exp(s_l - m).astype(BF16), vext_ref[0:q0, :], preferred_element_type=F32)
            outs.append(acc[:, 0:pair] / acc[:, pair:])
        o_ref[0, q0:q0 + tq, :] = jnp.where(low_lanes, outs[0], outs[1]).astype(BF16)


def _attention(q, k, v):
    B, H, L, _ = q.shape
    tq = min(TQ_ATTN, L)
    return pl.pallas_call(
        functools.partial(_attention_kernel, seq=L, tq=tq),
        grid=(B, H // 2),
        in_specs=[
            pl.BlockSpec((1, 2, L, HEAD_PAD), lambda b, p: (b, p, 0, 0)),
            pl.BlockSpec((1, 2, L, HEAD_PAD), lambda b, p: (b, p, 0, 0)),
            pl.BlockSpec((1, L, 2 * V_HEAD), lambda b, p: (b, 0, p)),
        ],
        out_specs=pl.BlockSpec((1, L, 2 * V_HEAD), lambda b, p: (b, 0, p)),
        out_shape=jax.ShapeDtypeStruct((B, L, ATTN_WIDTH), BF16),
        scratch_shapes=[pltpu.VMEM((L, 2 * V_HEAD + LANES), BF16)],
        compiler_params=_params("parallel", "parallel"),
    )(q, k, v)


def _ssm_kernel(u_ref, bbd_ref, lam_ref, cbd_ref, d_ref, wglu_ref, g_ref, o_ref, state_ref, bu_a, bu_b, *, batch, tc):
    @pl.when(pl.program_id(0) == 0)
    def _():
        state_ref[...] = jnp.zeros_like(state_ref)

    half = tc // 2 * batch
    halves = ((bu_a, slice(0, half)), (bu_b, slice(half, 2 * half)))
    for bu_ref, rows in halves:
        bu_ref[...] = jnp.dot(u_ref[rows, :], bbd_ref[...], preferred_element_type=F32)
    lam_re = jnp.broadcast_to(lam_ref[0:1, :], (batch, SSM_FLAT))
    lam_im = jnp.broadcast_to(lam_ref[1:2, :], (batch, SSM_FLAT))

    carry = (state_ref[:, 0:SSM_FLAT], state_ref[:, SSM_FLAT:2 * SSM_FLAT])
    for bu_ref, _ in halves:
        def step(t, xs, bu_ref=bu_ref):
            xr, xi = xs
            at = pl.ds(pl.multiple_of(t * batch, batch), batch)
            nr = lam_re * xr - lam_im * xi + bu_ref[at, 0:SSM_FLAT]
            ni = lam_re * xi + lam_im * xr + bu_ref[at, SSM_FLAT:2 * SSM_FLAT]
            bu_ref[at, 0:SSM_FLAT] = nr
            bu_ref[at, SSM_FLAT:2 * SSM_FLAT] = ni
            return nr, ni

        carry = lax.fori_loop(0, tc // 2, step, carry, unroll=True)
    state_ref[:, 0:SSM_FLAT] = carry[0]
    state_ref[:, SSM_FLAT:2 * SSM_FLAT] = carry[1]

    for bu_ref, rows in halves:
        y = jnp.dot(bu_ref[...].astype(BF16), cbd_ref[...], preferred_element_type=F32)
        y = y + d_ref[...] * u_ref[rows, :].astype(F32)
        z = jax.nn.gelu(y)
        gate = jax.nn.sigmoid(jnp.dot(z.astype(BF16), wglu_ref[...], preferred_element_type=F32))
        o_ref[rows, :] = _rms(z * gate, g_ref[...]).astype(BF16)


def _ssm(u_tm, bbd, lam, cbd, d, wglu, g, batch):
    rows = u_tm.shape[0]
    seq = rows // batch
    tc = min(TC_SSM, seq)
    blk = tc * batch
    full = lambda shape: pl.BlockSpec(shape, lambda t: (0,) * len(shape))
    return pl.pallas_call(
        functools.partial(_ssm_kernel, batch=batch, tc=tc),
        grid=(seq // tc,),
        in_specs=[
            pl.BlockSpec((blk, SSM_WIDTH), lambda t: (t, 0)),
            full((SSM_WIDTH, 2 * SSM_FLAT)), full((2, SSM_FLAT)), full((2 * SSM_FLAT, SSM_WIDTH)),
            full((1, SSM_WIDTH)), full((SSM_WIDTH, SSM_WIDTH)), full((1, SSM_WIDTH)),
        ],
        out_specs=pl.BlockSpec((blk, SSM_WIDTH), lambda t: (t, 0)),
        out_shape=jax.ShapeDtypeStruct((rows, SSM_WIDTH), BF16),
        scratch_shapes=[pltpu.VMEM((batch, 2 * SSM_FLAT), F32), pltpu.VMEM((blk // 2, 2 * SSM_FLAT), F32),
                        pltpu.VMEM((blk // 2, 2 * SSM_FLAT), F32)],
        compiler_params=_params("arbitrary"),
    )(u_tm, bbd, lam, cbd, d, wglu, g)


def _conv_kernel(a_ref, g_ref, ah_ref, gh_ref, w_ref, b_ref, lnw_ref, lnb_ref, gn_ref, o_ref, upad_ref, *, tt):
    first = pl.program_id(1) == 0
    halo = ah_ref[0].astype(F32) * jax.nn.sigmoid(gh_ref[0].astype(F32))
    upad_ref[0:CONV_HALO, :] = jnp.where(first, 0.0, halo)
    upad_ref[CONV_HALO:, :] = a_ref[0].astype(F32) * jax.nn.sigmoid(g_ref[0].astype(F32))
    base = CONV_HALO - (CONV_K - 1)
    for s in range(tt // CONV_SUB):
        r0 = s * CONV_SUB
        acc = jnp.zeros((CONV_SUB, CONV_WIDTH), F32)
        for off in range(SUBLANES):
            n = CONV_SUB + (SUBLANES if off else 0)
            part = None
            for a8 in range(0, CONV_HALO + 1, SUBLANES):
                kk = a8 + off - base
                if 0 <= kk < CONV_K:
                    term = w_ref[kk:kk + 1, :] * upad_ref[r0 + a8:r0 + a8 + n, :]
                    part = term if part is None else part + term
            acc = acc + part[off:off + CONV_SUB]
        y = acc + b_ref[...]
        mu = jnp.mean(y, axis=-1, keepdims=True)
        var = jnp.mean(jnp.square(y - mu), axis=-1, keepdims=True)
        y = (y - mu) * lax.rsqrt(var + 1e-5) * lnw_ref[...] + lnb_ref[...]
        y = y * jax.nn.sigmoid(y)
        o_ref[0, r0:r0 + CONV_SUB, :] = _rms(y, gn_ref[...]).astype(BF16)


def _conv(a, g, w, b, lnw, lnb, gn):
    B, L, C = a.shape
    tt = min(T_CONV, L)
    ratio = tt // CONV_HALO
    full = lambda shape: pl.BlockSpec(shape, lambda bb, t: (0,) * len(shape))
    cur = pl.BlockSpec((1, tt, C), lambda bb, t: (bb, t, 0))
    prev = pl.BlockSpec((1, CONV_HALO, C), lambda bb, t: (bb, jnp.maximum(t * ratio - 1, 0), 0))
    return pl.pallas_call(
        functools.partial(_conv_kernel, tt=tt),
        grid=(B, L // tt),
        in_specs=[cur, cur, prev, prev, full((CONV_K, C)), full((1, C)), full((1, C)), full((1, C)), full((1, C))],
        out_specs=pl.BlockSpec((1, tt, C), lambda bb, t: (bb, t, 0)),
        out_shape=jax.ShapeDtypeStruct((B, L, C), BF16),
        scratch_shapes=[pltpu.VMEM((tt + CONV_HALO, C), F32)],
        compiler_params=_params("parallel", "parallel"),
    )(a, g, a, g, w, b, lnw, lnb, gn)


ROUTE_E1, ROUTE_E2, ROUTE_R1, ROUTE_R2, ROUTE_G1, ROUTE_G2 = range(6)
ROUTE_ROWS = 8
GRP_LANE0 = N_EXPERTS


def _out_router_kernel(x_ref, ya_ref, ys_ref, yc_ref, ga_ref, wo_ref, fn_ref, wr_ref, br_ref,
                       x1_ref, h2_ref, route_ref, route_t_ref, cnt_ref, sub_ref, carry_ref, *, tm):
    @pl.when(pl.program_id(0) == 0)
    def _():
        carry_ref[...] = jnp.zeros_like(carry_ref)

    ya = _rms(ya_ref[...].astype(F32), ga_ref[...]).astype(BF16)
    acc = jnp.dot(ya, wo_ref[0:ATTN_WIDTH, :], preferred_element_type=F32)
    acc += jnp.dot(ys_ref[...], wo_ref[ATTN_WIDTH:ATTN_WIDTH + SSM_WIDTH, :], preferred_element_type=F32)
    acc += jnp.dot(yc_ref[...], wo_ref[ATTN_WIDTH + SSM_WIDTH:, :], preferred_element_type=F32)
    x1 = x_ref[...] + acc
    x1_ref[...] = x1
    h2 = _rms(x1, fn_ref[...])
    h2_ref[...] = _pack_rows(h2)

    h_hi = h2.astype(BF16)
    h_lo = (h2 - h_hi.astype(F32)).astype(BF16)
    part = jnp.dot(h_hi, wr_ref[...], preferred_element_type=F32)
    logits = (part[:, 0:LANES] + part[:, LANES:2 * LANES]
              + jnp.dot(h_lo, wr_ref[:, 0:LANES], preferred_element_type=F32) + br_ref[...])
    lane = lax.broadcasted_iota(jnp.int32, (tm, LANES), 1)
    ninf = -jnp.inf
    big = LANES

    def first_argmax(vals, vmax):
        return jnp.min(jnp.where(vals == vmax, lane, big), axis=-1, keepdims=True)

    grp = jnp.where((lane >= GRP_LANE0) & (lane < GRP_LANE0 + N_EGROUPS), logits, ninf)
    gmax = jnp.max(grp, axis=-1, keepdims=True)
    gsel = first_argmax(grp, gmax) - GRP_LANE0
    p_grp = 1.0 / jnp.sum(jnp.exp(grp - gmax), axis=-1, keepdims=True)

    el = jnp.where((lane < N_EXPERTS) & ((lane // EXP_PER_GROUP) == gsel), logits, ninf)
    m1 = jnp.max(el, axis=-1, keepdims=True)
    e1 = first_argmax(el, m1)
    el2 = jnp.where(lane == e1, ninf, el)
    m2 = jnp.max(el2, axis=-1, keepdims=True)
    e2 = first_argmax(el2, m2)
    t2 = jnp.exp(m2 - m1)
    g1 = p_grp / (1.0 + t2)
    g2 = p_grp * t2 / (1.0 + t2)

    hit1 = lane == e1
    hit2 = lane == e2
    cnt = (hit1 | hit2).astype(F32)
    rr = lax.broadcasted_iota(jnp.int32, (tm, tm), 0)
    cc = lax.broadcasted_iota(jnp.int32, (tm, tm), 1)
    tri = (cc < rr).astype(BF16)
    before = jnp.dot(tri, cnt.astype(BF16), preferred_element_type=F32) + carry_ref[...]
    r1 = jnp.sum(jnp.where(hit1, before, 0.0), axis=-1, keepdims=True)
    r2 = jnp.sum(jnp.where(hit2, before, 0.0), axis=-1, keepdims=True)
    subs = [before[h * TM_MOVE:h * TM_MOVE + 1, :] for h in range(tm // TM_MOVE)]
    sub_ref[...] = jnp.concatenate(subs + [jnp.zeros((SUBLANES - len(subs), LANES), F32)], axis=0)
    carry_ref[...] += jnp.sum(cnt, axis=0, keepdims=True)
    cnt_ref[...] = carry_ref[...]

    rec = jnp.where(lane == ROUTE_E1, e1.astype(F32), 0.0)
    rec = jnp.where(lane == ROUTE_E2, e2.astype(F32), rec)
    rec = jnp.where(lane == ROUTE_R1, r1, rec)
    rec = jnp.where(lane == ROUTE_R2, r2, rec)
    rec = jnp.where(lane == ROUTE_G1, g1, rec)
    rec = jnp.where(lane == ROUTE_G2, g2, rec)
    route_ref[...] = rec
    route_t_ref[...] = rec.T[0:ROUTE_ROWS, :]


def _out_router(x, ya, ys_tm, yc, ga, wo, fn, wr, br, batch):
    N, D = x.shape
    seq = N // batch
    tm = min(TM_OUT, seq)
    per_b = seq // tm
    full = lambda shape: pl.BlockSpec(shape, lambda i: (0,) * len(shape))
    tile = lambda w: pl.BlockSpec((tm, w), lambda i: (i, 0))
    return pl.pallas_call(
        functools.partial(_out_router_kernel, tm=tm),
        grid=(N // tm,),
        in_specs=[
            tile(D), tile(ATTN_WIDTH),
            pl.BlockSpec((tm, SSM_WIDTH), lambda i: (i % per_b, i // per_b)),
            tile(CONV_WIDTH),
            full((1, ATTN_WIDTH)), full((D, D)), full((1, D)), full((D, 2 * LANES)), full((1, LANES)),
        ],
        out_specs=[tile(D), tile(D_PACK), tile(LANES), pl.BlockSpec((ROUTE_ROWS, tm), lambda i: (0, i)),
                   full((1, LANES)), pl.BlockSpec((SUBLANES, LANES), lambda i: (i, 0))],
        out_shape=[
            jax.ShapeDtypeStruct((N, D), F32),
            jax.ShapeDtypeStruct((N, D_PACK), U32),
            jax.ShapeDtypeStruct((N, LANES), F32),
            jax.ShapeDtypeStruct((ROUTE_ROWS, N), F32),
            jax.ShapeDtypeStruct((1, LANES), F32),
            jax.ShapeDtypeStruct((N // tm * SUBLANES, LANES), F32),
        ],
        scratch_shapes=[pltpu.VMEM((1, LANES), F32)],
        compiler_params=_params("arbitrary"),
    )(x, ya, ys_tm, yc, ga, wo, fn, wr, br)


def _dispatch(dest_flat, h2, n_rows):
    N, D = h2.shape
    workers = SC_CORES * SC_SUBCORES
    per_worker = 2 * N // workers
    chunks = per_worker // SC_CHUNK
    idx = dest_flat.reshape(workers, chunks, SC_CHUNK)
    mesh = plsc.VectorSubcoreMesh(core_axis_name="c", subcore_axis_name="s")

    @functools.partial(
        pl.kernel, mesh=mesh,
        out_type=jax.ShapeDtypeStruct((n_rows, D), h2.dtype),
        scratch_types=[pltpu.VMEM((chunks, SC_CHUNK), jnp.int32), pltpu.VMEM((SC_CHUNK, D), h2.dtype)],
    )
    def scatter(h_hbm, idx_hbm, out_hbm, idx_v, rows_v):
        wid = lax.axis_index("s") * SC_CORES + lax.axis_index("c")
        pltpu.sync_copy(idx_hbm.at[wid], idx_v)

        @pl.loop(0, chunks)
        def _(c):
            first = lax.rem(wid * per_worker + c * SC_CHUNK, N)
            pltpu.sync_copy(h_hbm.at[pl.ds(pl.multiple_of(first, SC_CHUNK), SC_CHUNK)], rows_v)
            pltpu.sync_copy(rows_v, out_hbm.at[idx_v.at[c]])

    return scatter(h2, idx)


def _experts_kernel(blk_ref, exp_ref, valid_ref, used_ref, x_ref, w1_ref, w3_ref, w2_ref, y_ref, w13_s, w2_s):
    j = pl.program_id(0)

    @pl.when(j < used_ref[0])
    def _():
        prev = exp_ref[jnp.maximum(j - 1, 0)]

        @pl.when((j == 0) | (exp_ref[j] != prev))
        def _():
            for c in range(D_FF_E // FF_CHUNK):
                w13_s[:, 2 * FF_CHUNK * c:2 * FF_CHUNK * c + FF_CHUNK] = \
                    w1_ref[0, 0, :, FF_CHUNK * c:FF_CHUNK * (c + 1)].astype(BF16)
                w13_s[:, 2 * FF_CHUNK * c + FF_CHUNK:2 * FF_CHUNK * (c + 1)] = \
                    w3_ref[0, 0, :, FF_CHUNK * c:FF_CHUNK * (c + 1)].astype(BF16)
            w2_s[...] = w2_ref[0, 0].astype(BF16)

        x = _unpack_rows(x_ref[...])
        y = None
        for c in range(D_FF_E // FF_CHUNK):
            ab = jnp.dot(x, w13_s[:, 2 * FF_CHUNK * c:2 * FF_CHUNK * (c + 1)], preferred_element_type=F32)
            a = ab[:, 0:FF_CHUNK]
            hmid = (a * jax.nn.sigmoid(a) * ab[:, FF_CHUNK:]).astype(BF16)
            part = jnp.dot(hmid, w2_s[FF_CHUNK * c:FF_CHUNK * (c + 1), :], preferred_element_type=F32)
            y = part if y is None else y + part
        row = lax.broadcasted_iota(jnp.int32, (TB_EXP, 1), 0)
        y_ref[...] = jnp.where(row < valid_ref[j], _pack_rows(y), jnp.uint32(0))


def _experts(blk_map, blk_exp, blk_valid, n_used, x_rows, w1, w3, w2, layer):
    R, D = x_rows.shape[0], D_MODEL
    nb = R // TB_EXP
    return pl.pallas_call(
        _experts_kernel,
        grid_spec=pltpu.PrefetchScalarGridSpec(
            num_scalar_prefetch=4,
            grid=(nb,),
            in_specs=[
                pl.BlockSpec((TB_EXP, D_PACK), lambda j, bm, be, bv, nu: (bm[j], 0)),
                pl.BlockSpec((1, 1, D, D_FF_E), lambda j, bm, be, bv, nu: (layer, be[j], 0, 0)),
                pl.BlockSpec((1, 1, D, D_FF_E), lambda j, bm, be, bv, nu: (layer, be[j], 0, 0)),
                pl.BlockSpec((1, 1, D_FF_E, D), lambda j, bm, be, bv, nu: (layer, be[j], 0, 0)),
            ],
            out_specs=pl.BlockSpec((TB_EXP, D_PACK), lambda j, bm, be, bv, nu: (bm[j], 0)),
            scratch_shapes=[pltpu.VMEM((D, 2 * D_FF_E), BF16), pltpu.VMEM((D_FF_E, D), BF16)],
        ),
        out_shape=jax.ShapeDtypeStruct((R, D_PACK), U32),
        compiler_params=_params("arbitrary"),
    )(blk_map, blk_exp, blk_valid, n_used, x_rows, w1, w3, w2)


def _combine_kernel(src_ref, nslot_ref, x1_ref, route_ref, off_ref, rows_ref, o_ref, buf, sem, *, tm):
    i = pl.program_id(0)
    cur = i % 2

    def slot_copy(tile, half, s):
        src = pl.multiple_of(src_ref[tile * RUN_SLOTS + s], RUN_ROWS)
        dst = buf.at[half, pl.ds(pl.multiple_of(s * RUN_ROWS, RUN_ROWS), RUN_ROWS)]
        return pltpu.make_async_copy(rows_ref.at[pl.ds(src, RUN_ROWS)], dst, sem.at[half])

    def fetch(tile, half):
        def issue(g, c):
            for u in range(RUN_GROUP):
                slot_copy(tile, half, g * RUN_GROUP + u).start()
            return c
        lax.fori_loop(0, nslot_ref[tile], issue, 0)

    def wait(g, c):
        rows = RUN_GROUP * RUN_ROWS
        pltpu.make_async_copy(rows_ref.at[pl.ds(0, rows)], buf.at[cur, pl.ds(0, rows)], sem.at[cur]).wait()
        return c

    @pl.when(i == 0)
    def _():
        buf[...] = jnp.zeros_like(buf)
        fetch(0, 0)

    @pl.when(i + 1 < pl.num_programs(0))
    def _():
        fetch(i + 1, 1 - cur)

    route = route_ref[...]
    lane = lax.broadcasted_iota(jnp.int32, (tm, LANES), 1).astype(F32)
    off = off_ref[0]

    def buf_row(e_lane, r_lane):
        e = route[:, e_lane:e_lane + 1]
        base = jnp.sum(jnp.where(lane == e, off, 0.0), axis=-1, keepdims=True)
        return (base + route[:, r_lane:r_lane + 1]).astype(jnp.int32)

    col = lax.broadcasted_iota(jnp.int32, (tm, RUN_SLOTS * RUN_ROWS), 1)
    pick = (jnp.where(col == buf_row(ROUTE_E1, ROUTE_R1), route[:, ROUTE_G1:ROUTE_G1 + 1], 0.0)
            + jnp.where(col == buf_row(ROUTE_E2, ROUTE_R2), route[:, ROUTE_G2:ROUTE_G2 + 1], 0.0))
    lax.fori_loop(0, nslot_ref[i], wait, 0)
    o_ref[...] = x1_ref[...] + jnp.dot(pick.astype(BF16), _unpack_rows(buf[cur]), preferred_element_type=F32)


def _combine(src, nslot, x1, route, off, y_rows):
    N, D = x1.shape
    tm = min(TM_MOVE, N)
    return pl.pallas_call(
        functools.partial(_combine_kernel, tm=tm),
        grid_spec=pltpu.PrefetchScalarGridSpec(
            num_scalar_prefetch=2,
            grid=(N // tm,),
            in_specs=[
                pl.BlockSpec((tm, D), lambda i, s, n: (i, 0)),
                pl.BlockSpec((tm, LANES), lambda i, s, n: (i, 0)),
                pl.BlockSpec((1, 1, LANES), lambda i, s, n: (i, 0, 0)),
                pl.BlockSpec(memory_space=pl.ANY),
            ],
            out_specs=pl.BlockSpec((tm, D), lambda i, s, n: (i, 0)),
            scratch_shapes=[pltpu.VMEM((2, RUN_SLOTS * RUN_ROWS, D_PACK), U32), pltpu.SemaphoreType.DMA((2,))],
        ),
        out_shape=jax.ShapeDtypeStruct((N, D), F32),
        compiler_params=_params("arbitrary"),
    )(src, nslot, x1, route, off, y_rows)


def _moe(x1, h2, route, route_t, counts, sub_carry, w1, w3, w2, layer):
    N, _ = x1.shape
    tm = min(TM_MOVE, N)
    n_tiles = N // tm
    experts = jnp.arange(N_EXPERTS, dtype=jnp.int32)
    nb = (2 * N + N_EXPERTS * (TB_EXP - 1)) // TB_EXP + 1
    e_id = route_t[ROUTE_E1:ROUTE_E2 + 1].astype(jnp.int32)
    rank = route_t[ROUTE_R1:ROUTE_R2 + 1].astype(jnp.int32)
    cnt = counts[0, :N_EXPERTS].astype(jnp.int32)
    padded = (cnt + TB_EXP - 1) // TB_EXP * TB_EXP
    pad_end = jnp.cumsum(padded)
    pad_start = pad_end - padded
    e_flat = e_id.reshape(1, 2 * N)
    dest = jnp.sum(jnp.where(e_flat == experts[:, None], pad_start[:, None], 0), axis=0) + rank.reshape(2 * N)
    n_used = pad_end[-1] // TB_EXP
    blk_map = jnp.minimum(jnp.arange(nb, dtype=jnp.int32), jnp.maximum(n_used - 1, 0))
    blk_exp = jnp.sum(pad_end[None, :] <= (blk_map * TB_EXP)[:, None], axis=1).astype(jnp.int32)
    blk_exp = jnp.minimum(blk_exp, N_EXPERTS - 1)
    is_blk_e = blk_exp[:, None] == experts[None, :]
    blk_valid = jnp.clip(jnp.sum(jnp.where(is_blk_e, (pad_start + cnt)[None, :], 0), axis=1) - blk_map * TB_EXP,
                         0, TB_EXP).astype(jnp.int32)

    per_router_tile = TM_OUT // tm if N >= TM_OUT else 1
    carry = sub_carry.reshape(-1, SUBLANES, LANES)[:, :per_router_tile, :N_EXPERTS].reshape(n_tiles, N_EXPERTS)
    carry = carry.astype(jnp.int32)
    tile_cnt = jnp.concatenate([carry[1:], cnt[None, :]], axis=0) - carry
    run_start = pad_start[None, :] + carry
    first_blk = run_start // RUN_ROWS
    nslot = jnp.where(tile_cnt > 0, (run_start + tile_cnt - 1) // RUN_ROWS - first_blk + 1, 0)
    slot_end = jnp.cumsum(nslot, axis=1)
    slot_base = slot_end - nslot
    slots = jnp.arange(RUN_SLOTS, dtype=jnp.int32)
    slot_e = jnp.minimum(jnp.sum(slot_end[:, None, :] <= slots[None, :, None], axis=2), N_EXPERTS - 1)
    is_slot_e = slot_e[:, :, None] == experts[None, None, :]
    pick = lambda tbl: jnp.sum(jnp.where(is_slot_e, tbl[:, None, :], 0), axis=2)
    src = (pick(first_blk) + slots[None, :] - pick(slot_base)) * RUN_ROWS
    src = jnp.where(slots[None, :] < slot_end[:, -1:], src, 0)
    src = jnp.clip(src, 0, nb * TB_EXP - RUN_ROWS).reshape(-1).astype(jnp.int32)
    n_groups = (slot_end[:, -1] + RUN_GROUP - 1) // RUN_GROUP
    off = (slot_base - first_blk) * RUN_ROWS + pad_start[None, :]
    off = jnp.pad(off.astype(F32), ((0, 0), (0, LANES - N_EXPERTS)))

    x_rows = _dispatch(dest, h2, nb * TB_EXP)
    y_rows = _experts(blk_map, blk_exp, blk_valid, n_used.reshape(1).astype(jnp.int32), x_rows, w1, w3, w2, layer)
    return _combine(src, n_groups.astype(jnp.int32), x1, route, off.reshape(n_tiles, 1, LANES), y_rows)


def _pad_heads(w, width):
    k = w.shape[0]
    w = w.reshape(k, N_HEADS, width)
    return jnp.pad(w, ((0, 0), (0, 0), (0, HEAD_PAD - width))).reshape(k, N_HEADS * HEAD_PAD)


def _swap_rope(w):
    half = QK_ROPE // 2
    lo, hi = w[..., QK_NOPE:QK_NOPE + half], w[..., QK_NOPE + half:QK_HEAD]
    pad = [(0, 0)] * (w.ndim - 1)
    return jnp.pad(jnp.concatenate([hi, lo], -1), pad + [(QK_NOPE, HEAD_PAD - QK_HEAD)])


def _rope_tables(positions):
    inv_freq = ROPE_THETA ** (-jnp.arange(0, QK_ROPE, 2, dtype=F32) / QK_ROPE)
    ang = positions.astype(F32)[..., None] * inv_freq
    table = jnp.concatenate([jnp.cos(ang), jnp.sin(ang)], -1)
    return jnp.pad(table, ((0, 0), (0, 0), (QK_NOPE, LANES - QK_HEAD)))


def _ssm_params(lam_re, lam_im, b_re, b_im, c_re, c_im, log_dt):
    lam = lax.complex(lam_re, lam_im)
    dt = jnp.exp(log_dt)[:, None]
    lam_bar = jnp.exp(lam * dt)
    b_bar = ((lam_bar - 1.0) / lam)[..., None] * lax.complex(b_re, b_im)
    eye = jnp.eye(SSM_GROUPS, dtype=F32)

    def in_blockdiag(m):
        return jnp.einsum("gpc,gh->gchp", m, eye).reshape(SSM_WIDTH, SSM_FLAT)

    def out_blockdiag(m):
        return jnp.einsum("gcp,gh->gphc", m, eye).reshape(SSM_FLAT, SSM_WIDTH)

    bbd = jnp.concatenate([in_blockdiag(jnp.real(b_bar)), in_blockdiag(jnp.imag(b_bar))], axis=1)
    cbd = jnp.concatenate([out_blockdiag(c_re), out_blockdiag(-c_im)], axis=0)
    lam_rows = jnp.stack([jnp.real(lam_bar).reshape(SSM_FLAT), jnp.imag(lam_bar).reshape(SSM_FLAT)])
    return bbd.astype(BF16), lam_rows, cbd.astype(BF16)


def kernel(x, positions, mix_norm, w_in, q_a_norm, w_uq, kv_a_norm, w_ukv, q_norm, k_norm, ssm_lam_re, ssm_lam_im, ssm_b_re, ssm_b_im, ssm_c_re, ssm_c_im, ssm_d, ssm_log_dt, ssm_w_glu, conv_dw_w, conv_dw_b, conv_ln_w, conv_ln_b, out_norm, w_out, ffn_norm, w_grp, b_grp, w_exp, b_exp, w1, w3, w2):
    B, L, D = x.shape
    depth = w_in.shape[0]
    rope = _rope_tables(positions)
    row = lambda v: v.reshape(1, -1)
    lane_pad = lambda v: jnp.pad(v, (0, LANES - v.shape[0])).reshape(1, LANES)
    for l in range(depth):
        c_q, c_kv, k_pe, u_s, c_a, c_g = jnp.split(
            w_in[l], [Q_LORA, Q_LORA + KV_LORA, Q_LORA + KV_LORA + QK_ROPE,
                      Q_LORA + KV_LORA + QK_ROPE + SSM_WIDTH,
                      Q_LORA + KV_LORA + QK_ROPE + SSM_WIDTH + CONV_WIDTH], axis=1)
        k_pe_full = jnp.pad(k_pe, ((0, 0), (QK_NOPE, 0)))
        win = jnp.concatenate([c_q, c_kv, u_s, c_a, c_g, jnp.pad(k_pe_full, ((0, 0), (0, HEAD_PAD - QK_HEAD))),
                               _swap_rope(k_pe_full)], axis=1).astype(BF16)
        wkv = w_ukv[l].reshape(KV_LORA, N_HEADS, QK_NOPE + V_HEAD)
        wuk = _pad_heads(wkv[:, :, :QK_NOPE].reshape(KV_LORA, N_HEADS * QK_NOPE), QK_NOPE).astype(BF16)
        wuv = wkv[:, :, QK_NOPE:].reshape(KV_LORA, ATTN_WIDTH).astype(BF16)
        wuq_sw = _swap_rope(w_uq[l].reshape(Q_LORA, N_HEADS, QK_HEAD)).reshape(Q_LORA, N_HEADS * HEAD_PAD)
        wuq = jnp.concatenate([_pad_heads(w_uq[l], QK_HEAD), wuq_sw], axis=1).astype(BF16)
        qn_scaled = q_norm[l] * (QK_HEAD ** -0.5)
        norm_rows = lambda w: jnp.stack([jnp.pad(w, (0, HEAD_PAD - QK_HEAD)), _swap_rope(w)])
        q, k, v, u_tm, ca, cg = _in_proj(
            x, rope, row(mix_norm[l]), win, row(q_a_norm[l]), wuq, row(kv_a_norm[l]), wuk, wuv,
            norm_rows(qn_scaled), norm_rows(k_norm[l]))
        y_attn = _attention(q, k, v)

        g_out = out_norm[l]
        bbd, lam_rows, cbd = _ssm_params(ssm_lam_re[l], ssm_lam_im[l], ssm_b_re[l], ssm_b_im[l],
                                         ssm_c_re[l], ssm_c_im[l], ssm_log_dt[l])
        y_ssm_tm = _ssm(u_tm.reshape(L * B, SSM_WIDTH), bbd, lam_rows, cbd, row(ssm_d[l]),
                        ssm_w_glu[l].astype(BF16), row(g_out[ATTN_WIDTH:ATTN_WIDTH + SSM_WIDTH]), B)
        y_conv = _conv(ca, cg, conv_dw_w[l], row(conv_dw_b[l]), row(conv_ln_w[l]), row(conv_ln_b[l]),
                       row(g_out[ATTN_WIDTH + SSM_WIDTH:]))

        w_route = jnp.pad(jnp.concatenate([w_exp[l], w_grp[l]], axis=1), ((0, 0), (0, LANES - N_EXPERTS - N_EGROUPS)))
        w_route_hi = w_route.astype(BF16)
        w_route = jnp.concatenate([w_route_hi, (w_route - w_route_hi.astype(F32)).astype(BF16)], axis=1)
        b_route = lane_pad(jnp.concatenate([b_exp[l], b_grp[l]]))
        x1, h2, route, route_t, counts, sub_carry = _out_router(
            x.reshape(B * L, D), y_attn.reshape(B * L, ATTN_WIDTH), y_ssm_tm.reshape(L, B * SSM_WIDTH),
            y_conv.reshape(B * L, CONV_WIDTH), row(g_out[:ATTN_WIDTH]), w_out[l].astype(BF16),
            row(ffn_norm[l]), w_route, b_route, B)
        x = _moe(x1, h2, route, route_t, counts, sub_carry, w1, w3, w2, l).reshape(B, L, D)
    return x
```

```python
import functools
import math

import jax
import jax.numpy as jnp
from jax import lax
from jax.experimental import pallas as pl
from jax.experimental.pallas import tpu as pltpu
from jax.experimental.pallas import tpu_sc as plsc

D_MODEL = 1024
CHUNK = 64
EPS = 1e-6
N_HEADS = 8
QK_NOPE = 64
QK_ROPE = 32
QK_HEAD = QK_NOPE + QK_ROPE
V_HEAD = 64
Q_LORA = 256
KV_LORA = 128
ROPE_THETA = 10000.0
ATTN_WIDTH = N_HEADS * V_HEAD
SSM_WIDTH = 256
SSM_GROUP = 16
SSM_GROUPS = SSM_WIDTH // SSM_GROUP
SSM_STATE = 64
SSM_FLAT = SSM_GROUPS * SSM_STATE
CONV_WIDTH = 256
CONV_K = 31
N_EGROUPS = 4
EXP_PER_GROUP = 8
N_EXPERTS = N_EGROUPS * EXP_PER_GROUP
D_FF_E = 512

LANES = 128
SUBLANES = 8
HEAD_PAD = LANES
IN_PROJ_PAD = Q_LORA + KV_LORA + SSM_WIDTH + 2 * CONV_WIDTH + 2 * LANES
CONV_HALO = 32
VMEM_LIMIT = 48 * 1024 * 1024

TM_PROJ = 512
TQ_ATTN = 256
TC_SSM = 64
T_CONV = 256
CONV_SUB = 64
TM_OUT = 512
TM_MOVE = 256
RUN_ROWS = SUBLANES
RUN_SLOTS = 2 * N_EXPERTS + 2 * TM_MOVE // RUN_ROWS
RUN_GROUP = 8
SC_CORES, SC_SUBCORES = 2, 16
SC_CHUNK = 64
TB_EXP = 512
FF_CHUNK = 256

BF16 = jnp.bfloat16
F32 = jnp.float32
U32 = jnp.uint32
D_PACK = D_MODEL // 2


def _pack_rows(v):
    bits = lax.bitcast_convert_type(v.astype(BF16).astype(F32), U32)
    half = v.shape[1] // 2
    return bits[:, 0:half] | (bits[:, half:] >> 16)


def _unpack_rows(w):
    hi = lax.bitcast_convert_type(w & jnp.uint32(0xFFFF0000), F32)
    lo = lax.bitcast_convert_type(w << 16, F32)
    return jnp.concatenate([hi, lo], axis=1).astype(BF16)


def _rms(x, w):
    return x * lax.rsqrt(jnp.mean(x * x, axis=-1, keepdims=True) + EPS) * w


def _params(*sem):
    return pltpu.CompilerParams(dimension_semantics=sem, vmem_limit_bytes=VMEM_LIMIT)


def _in_proj_kernel(x_ref, rope_ref, mixn_ref, win_ref, qan_ref, wuq_ref, kvan_ref, wuk_ref, wuv_ref,
                    qn_ref, kn_ref, q_ref, k_ref, v_ref, u_ref, ca_ref, cg_ref):
    x = x_ref[0]
    h = _rms(x, mixn_ref[...]).astype(BF16)
    proj = jnp.dot(h, win_ref[...], preferred_element_type=F32)
    o = 0
    c_q = proj[:, o:o + Q_LORA]; o += Q_LORA
    c_kv = proj[:, o:o + KV_LORA]; o += KV_LORA
    u_ref[...] = proj[:, o:o + SSM_WIDTH].astype(BF16); o += SSM_WIDTH
    ca_ref[0] = proj[:, o:o + CONV_WIDTH].astype(BF16); o += CONV_WIDTH
    cg_ref[0] = proj[:, o:o + CONV_WIDTH].astype(BF16); o += CONV_WIDTH
    k_pe = proj[:, o:o + LANES]; o += LANES
    k_pe_sw = proj[:, o:o + LANES]

    width = N_HEADS * HEAD_PAD
    q2 = jnp.dot(_rms(c_q, qan_ref[...]).astype(BF16), wuq_ref[...], preferred_element_type=F32)
    q, q_sw = q2[:, 0:width], q2[:, width:2 * width]
    ckv_n = _rms(c_kv, kvan_ref[...]).astype(BF16)
    kn = jnp.dot(ckv_n, wuk_ref[...], preferred_element_type=F32)
    v_ref[0] = jnp.dot(ckv_n, wuv_ref[...], preferred_element_type=F32).astype(BF16)

    rope = rope_ref[0]
    half = QK_ROPE // 2
    lane = lax.broadcasted_iota(jnp.int32, rope.shape, 1)
    lo = (lane >= QK_NOPE) & (lane < QK_NOPE + half)
    hi = (lane >= QK_NOPE + half) & (lane < QK_HEAD)
    cos_t = jnp.where(lane < QK_NOPE, 1.0, jnp.where(lo, rope, jnp.where(hi, pltpu.roll(rope, half, 1), 0.0)))
    sin_t = jnp.where(lo, -pltpu.roll(rope, LANES - half, 1), jnp.where(hi, rope, 0.0))
    a_q, b_q = qn_ref[0:1, :] * cos_t, qn_ref[1:2, :] * sin_t
    a_k = kn_ref[0:1, :] * cos_t
    k_sw_term = k_pe_sw * (kn_ref[1:2, :] * sin_t)

    def inv_rms(y):
        return lax.rsqrt(jnp.sum(y * y, axis=-1, keepdims=True) * (1.0 / QK_HEAD) + EPS)

    for hd in range(N_HEADS):
        sl = slice(hd * HEAD_PAD, (hd + 1) * HEAD_PAD)
        yq = q[:, sl]
        q_ref[0, hd] = ((yq * a_q + q_sw[:, sl] * b_q) * inv_rms(yq)).astype(BF16)
        yk = kn[:, sl] + k_pe
        k_ref[0, hd] = ((yk * a_k + k_sw_term) * inv_rms(yk)).astype(BF16)


def _in_proj(x, rope, mixn, win, qan, wuq, kvan, wuk, wuv, qn, kn):
    B, L, D = x.shape
    tm = min(TM_PROJ, L)
    full = lambda shape: pl.BlockSpec(shape, lambda b, t: (0,) * len(shape))
    return pl.pallas_call(
        _in_proj_kernel,
        grid=(B, L // tm),
        in_specs=[
            pl.BlockSpec((1, tm, D), lambda b, t: (b, t, 0)),
            pl.BlockSpec((1, tm, LANES), lambda b, t: (b, t, 0)),
            full((1, D)), full((D, IN_PROJ_PAD)), full((1, Q_LORA)), full((Q_LORA, 2 * N_HEADS * HEAD_PAD)),
            full((1, KV_LORA)), full((KV_LORA, N_HEADS * HEAD_PAD)), full((KV_LORA, ATTN_WIDTH)),
            full((2, HEAD_PAD)), full((2, HEAD_PAD)),
        ],
        out_specs=[
            pl.BlockSpec((1, N_HEADS, tm, HEAD_PAD), lambda b, t: (b, 0, t, 0)),
            pl.BlockSpec((1, N_HEADS, tm, HEAD_PAD), lambda b, t: (b, 0, t, 0)),
            pl.BlockSpec((1, tm, ATTN_WIDTH), lambda b, t: (b, t, 0)),
            pl.BlockSpec((tm, SSM_WIDTH), lambda b, t: (t, b)),
            pl.BlockSpec((1, tm, CONV_WIDTH), lambda b, t: (b, t, 0)),
            pl.BlockSpec((1, tm, CONV_WIDTH), lambda b, t: (b, t, 0)),
        ],
        out_shape=[
            jax.ShapeDtypeStruct((B, N_HEADS, L, HEAD_PAD), BF16),
            jax.ShapeDtypeStruct((B, N_HEADS, L, HEAD_PAD), BF16),
            jax.ShapeDtypeStruct((B, L, ATTN_WIDTH), BF16),
            jax.ShapeDtypeStruct((L, B * SSM_WIDTH), BF16),
            jax.ShapeDtypeStruct((B, L, CONV_WIDTH), BF16),
            jax.ShapeDtypeStruct((B, L, CONV_WIDTH), BF16),
        ],
        compiler_params=_params("parallel", "parallel"),
    )(x, rope, mixn, win, qan, wuq, kvan, wuk, wuv, qn, kn)


def _attention_kernel(q_ref, k_ref, v_ref, o_ref, vext_ref, *, seq, tq):
    pair = 2 * V_HEAD
    vext_ref[:, 0:pair] = v_ref[0]
    vext_ref[:, pair:] = jnp.ones((seq, LANES), BF16)
    row_chunk = lax.broadcasted_iota(jnp.int32, (tq, tq), 0) // CHUNK
    col_chunk = lax.broadcasted_iota(jnp.int32, (tq, tq), 1) // CHUNK
    visible = col_chunk <= row_chunk
    neg = jnp.finfo(F32).min
    nt = (((1,), (1,)), ((), ()))
    low_lanes = lax.broadcasted_iota(jnp.int32, (tq, pair), 1) < V_HEAD
    for i in range(seq // tq):
        q0 = i * tq
        outs = []
        for hh in range(2):
            qb = q_ref[0, hh, q0:q0 + tq, :]
            s_d = lax.dot_general(qb, k_ref[0, hh, q0:q0 + tq, :], nt, preferred_element_type=F32)
            s_d = jnp.where(visible, s_d, neg)
            m = jnp.max(s_d, axis=-1, keepdims=True)
            if i > 0:
                s_l = lax.dot_general(qb, k_ref[0, hh, 0:q0, :], nt, preferred_element_type=F32)
                m = jnp.maximum(m, jnp.max(s_l, axis=-1, keepdims=True))
            acc = jnp.dot(jnp.exp(s_d - m).astype(BF16), vext_ref[q0:q0 + tq, :], preferred_element_type=F32)
            if i > 0:
                acc = acc + jnp.dot(jnp.exp(s_l - m).astype(BF16), vext_ref[0:q0, :], preferred_element_type=F32)
            outs.append(acc[:, 0:pair] / acc[:, pair:])
        o_ref[0, q0:q0 + tq, :] = jnp.where(low_lanes, outs[0], outs[1]).astype(BF16)


def _attention(q, k, v):
    B, H, L, _ = q.shape
    tq = min(TQ_ATTN, L)
    return pl.pallas_call(
        functools.partial(_attention_kernel, seq=L, tq=tq),
        grid=(B, H // 2),
        in_specs=[
            pl.BlockSpec((1, 2, L, HEAD_PAD), lambda b, p: (b, p, 0, 0)),
            pl.BlockSpec((1, 2, L, HEAD_PAD), lambda b, p: (b, p, 0, 0)),
            pl.BlockSpec((1, L, 2 * V_HEAD), lambda b, p: (b, 0, p)),
        ],
        out_specs=pl.BlockSpec((1, L, 2 * V_HEAD), lambda b, p: (b, 0, p)),
        out_shape=jax.ShapeDtypeStruct((B, L, ATTN_WIDTH), BF16),
        scratch_shapes=[pltpu.VMEM((L, 2 * V_HEAD + LANES), BF16)],
        compiler_params=_params("parallel", "parallel"),
    )(q, k, v)


def _ssm_kernel(u_ref, bbd_ref, lam_ref, cbd_ref, d_ref, wglu_ref, g_ref, o_ref, state_ref, bu_a, bu_b, *, batch, tc):
    @pl.when(pl.program_id(0) == 0)
    def _():
        state_ref[...] = jnp.zeros_like(state_ref)

    half = tc // 2 * batch
    halves = ((bu_a, slice(0, half)), (bu_b, slice(half, 2 * half)))
    for bu_ref, rows in halves:
        bu_ref[...] = jnp.dot(u_ref[rows, :], bbd_ref[...], preferred_element_type=F32)
    lam_re = jnp.broadcast_to(lam_ref[0:1, :], (batch, SSM_FLAT))
    lam_im = jnp.broadcast_to(lam_ref[1:2, :], (batch, SSM_FLAT))

    carry = (state_ref[:, 0:SSM_FLAT], state_ref[:, SSM_FLAT:2 * SSM_FLAT])
    for bu_ref, _ in halves:
        def step(t, xs, bu_ref=bu_ref):
            xr, xi = xs
            at = pl.ds(pl.multiple_of(t * batch, batch), batch)
            nr = lam_re * xr - lam_im * xi + bu_ref[at, 0:SSM_FLAT]
            ni = lam_re * xi + lam_im * xr + bu_ref[at, SSM_FLAT:2 * SSM_FLAT]
            bu_ref[at, 0:SSM_FLAT] = nr
            bu_ref[at, SSM_FLAT:2 * SSM_FLAT] = ni
            return nr, ni

        carry = lax.fori_loop(0, tc // 2, step, carry, unroll=True)
    state_ref[:, 0:SSM_FLAT] = carry[0]
    state_ref[:, SSM_FLAT:2 * SSM_FLAT] = carry[1]

    for bu_ref, rows in halves:
        y = jnp.dot(bu_ref[...].astype(BF16), cbd_ref[...], preferred_element_type=F32)
        y = y + d_ref[...] * u_ref[rows, :].astype(F32)
        z = jax.nn.gelu(y)
        gate = jax.nn.sigmoid(jnp.dot(z.astype(BF16), wglu_ref[...], preferred_element_type=F32))
        o_ref[rows, :] = _rms(z * gate, g_ref[...]).astype(BF16)


def _ssm(u_tm, bbd, lam, cbd, d, wglu, g, batch):
    rows = u_tm.shape[0]
    seq = rows // batch
    tc = min(TC_SSM, seq)
    blk = tc * batch
    full = lambda shape: pl.BlockSpec(shape, lambda t: (0,) * len(shape))
    return pl.pallas_call(
        functools.partial(_ssm_kernel, batch=batch, tc=tc),
        grid=(seq // tc,),
        in_specs=[
            pl.BlockSpec((blk, SSM_WIDTH), lambda t: (t, 0)),
            full((SSM_WIDTH, 2 * SSM_FLAT)), full((2, SSM_FLAT)), full((2 * SSM_FLAT, SSM_WIDTH)),
            full((1, SSM_WIDTH)), full((SSM_WIDTH, SSM_WIDTH)), full((1, SSM_WIDTH)),
        ],
        out_specs=pl.BlockSpec((blk, SSM_WIDTH), lambda t: (t, 0)),
        out_shape=jax.ShapeDtypeStruct((rows, SSM_WIDTH), BF16),
        scratch_shapes=[pltpu.VMEM((batch, 2 * SSM_FLAT), F32), pltpu.VMEM((blk // 2, 2 * SSM_FLAT), F32),
                        pltpu.VMEM((blk // 2, 2 * SSM_FLAT), F32)],
        compiler_params=_params("arbitrary"),
    )(u_tm, bbd, lam, cbd, d, wglu, g)


def _conv_kernel(a_ref, g_ref, ah_ref, gh_ref, w_ref, b_ref, lnw_ref, lnb_ref, gn_ref, o_ref, upad_ref, *, tt):
    first = pl.program_id(1) == 0
    halo = ah_ref[0].astype(F32) * jax.nn.sigmoid(gh_ref[0].astype(F32))
    upad_ref[0:CONV_HALO, :] = jnp.where(first, 0.0, halo)
    upad_ref[CONV_HALO:, :] = a_ref[0].astype(F32) * jax.nn.sigmoid(g_ref[0].astype(F32))
    base = CONV_HALO - (CONV_K - 1)
    for s in range(tt // CONV_SUB):
        r0 = s * CONV_SUB
        acc = jnp.zeros((CONV_SUB, CONV_WIDTH), F32)
        for off in range(SUBLANES):
            n = CONV_SUB + (SUBLANES if off else 0)
            part = None
            for a8 in range(0, CONV_HALO + 1, SUBLANES):
                kk = a8 + off - base
                if 0 <= kk < CONV_K:
                    term = w_ref[kk:kk + 1, :] * upad_ref[r0 + a8:r0 + a8 + n, :]
                    part = term if part is None else part + term
            acc = acc + part[off:off + CONV_SUB]
        y = acc + b_ref[...]
        mu = jnp.mean(y, axis=-1, keepdims=True)
        var = jnp.mean(jnp.square(y - mu), axis=-1, keepdims=True)
        y = (y - mu) * lax.rsqrt(var + 1e-5) * lnw_ref[...] + lnb_ref[...]
        y = y * jax.nn.sigmoid(y)
        o_ref[0, r0:r0 + CONV_SUB, :] = _rms(y, gn_ref[...]).astype(BF16)


def _conv(a, g, w, b, lnw, lnb, gn):
    B, L, C = a.shape
    tt = min(T_CONV, L)
    ratio = tt // CONV_HALO
    full = lambda shape: pl.BlockSpec(shape, lambda bb, t: (0,) * len(shape))
    cur = pl.BlockSpec((1, tt, C), lambda bb, t: (bb, t, 0))
    prev = pl.BlockSpec((1, CONV_HALO, C), lambda bb, t: (bb, jnp.maximum(t * ratio - 1, 0), 0))
    return pl.pallas_call(
        functools.partial(_conv_kernel, tt=tt),
        grid=(B, L // tt),
        in_specs=[cur, cur, prev, prev, full((CONV_K, C)), full((1, C)), full((1, C)), full((1, C)), full((1, C))],
        out_specs=pl.BlockSpec((1, tt, C), lambda bb, t: (bb, t, 0)),
        out_shape=jax.ShapeDtypeStruct((B, L, C), BF16),
        scratch_shapes=[pltpu.VMEM((tt + CONV_HALO, C), F32)],
        compiler_params=_params("parallel", "parallel"),
    )(a, g, a, g, w, b, lnw, lnb, gn)


ROUTE_E1, ROUTE_E2, ROUTE_R1, ROUTE_R2, ROUTE_G1, ROUTE_G2 = range(6)
ROUTE_ROWS = 8
GRP_LANE0 = N_EXPERTS


def _out_router_kernel(x_ref, ya_ref, ys_ref, yc_ref, ga_ref, wo_ref, fn_ref, wr_ref, br_ref,
                       x1_ref, h2_ref, route_ref, route_t_ref, cnt_ref, sub_ref, carry_ref, *, tm):
    @pl.when(pl.program_id(0) == 0)
    def _():
        carry_ref[...] = jnp.zeros_like(carry_ref)

    ya = _rms(ya_ref[...].astype(F32), ga_ref[...]).astype(BF16)
    acc = jnp.dot(ya, wo_ref[0:ATTN_WIDTH, :], preferred_element_type=F32)
    acc += jnp.dot(ys_ref[...], wo_ref[ATTN_WIDTH:ATTN_WIDTH + SSM_WIDTH, :], preferred_element_type=F32)
    acc += jnp.dot(yc_ref[...], wo_ref[ATTN_WIDTH + SSM_WIDTH:, :], preferred_element_type=F32)
    x1 = x_ref[...] + acc
    x1_ref[...] = x1
    h2 = _rms(x1, fn_ref[...])
    h2_ref[...] = _pack_rows(h2)

    h_hi = h2.astype(BF16)
    h_lo = (h2 - h_hi.astype(F32)).astype(BF16)
    part = jnp.dot(h_hi, wr_ref[...], preferred_element_type=F32)
    logits = (part[:, 0:LANES] + part[:, LANES:2 * LANES]
              + jnp.dot(h_lo, wr_ref[:, 0:LANES], preferred_element_type=F32) + br_ref[...])
    lane = lax.broadcasted_iota(jnp.int32, (tm, LANES), 1)
    ninf = -jnp.inf
    big = LANES

    def first_argmax(vals, vmax):
        return jnp.min(jnp.where(vals == vmax, lane, big), axis=-1, keepdims=True)

    grp = jnp.where((lane >= GRP_LANE0) & (lane < GRP_LANE0 + N_EGROUPS), logits, ninf)
    gmax = jnp.max(grp, axis=-1, keepdims=True)
    gsel = first_argmax(grp, gmax) - GRP_LANE0
    p_grp = 1.0 / jnp.sum(jnp.exp(grp - gmax), axis=-1, keepdims=True)

    el = jnp.where((lane < N_EXPERTS) & ((lane // EXP_PER_GROUP) == gsel), logits, ninf)
    m1 = jnp.max(el, axis=-1, keepdims=True)
    e1 = first_argmax(el, m1)
    el2 = jnp.where(lane == e1, ninf, el)
    m2 = jnp.max(el2, axis=-1, keepdims=True)
    e2 = first_argmax(el2, m2)
    t2 = jnp.exp(m2 - m1)
    g1 = p_grp / (1.0 + t2)
    g2 = p_grp * t2 / (1.0 + t2)

    hit1 = lane == e1
    hit2 = lane == e2
    cnt = (hit1 | hit2).astype(F32)
    rr = lax.broadcasted_iota(jnp.int32, (tm, tm), 0)
    cc = lax.broadcasted_iota(jnp.int32, (tm, tm), 1)
    tri = (cc < rr).astype(BF16)
    before = jnp.dot(tri, cnt.astype(BF16), preferred_element_type=F32) + carry_ref[...]
    r1 = jnp.sum(jnp.where(hit1, before, 0.0), axis=-1, keepdims=True)
    r2 = jnp.sum(jnp.where(hit2, before, 0.0), axis=-1, keepdims=True)
    subs = [before[h * TM_MOVE:h * TM_MOVE + 1, :] for h in range(tm // TM_MOVE)]
    sub_ref[...] = jnp.concatenate(subs + [jnp.zeros((SUBLANES - len(subs), LANES), F32)], axis=0)
    carry_ref[...] += jnp.sum(cnt, axis=0, keepdims=True)
    cnt_ref[...] = carry_ref[...]

    rec = jnp.where(lane == ROUTE_E1, e1.astype(F32), 0.0)
    rec = jnp.where(lane == ROUTE_E2, e2.astype(F32), rec)
    rec = jnp.where(lane == ROUTE_R1, r1, rec)
    rec = jnp.where(lane == ROUTE_R2, r2, rec)
    rec = jnp.where(lane == ROUTE_G1, g1, rec)
    rec = jnp.where(lane == ROUTE_G2, g2, rec)
    route_ref[...] = rec
    route_t_ref[...] = rec.T[0:ROUTE_ROWS, :]


def _out_router(x, ya, ys_tm, yc, ga, wo, fn, wr, br, batch):
    N, D = x.shape
    seq = N // batch
    tm = min(TM_OUT, seq)
    per_b = seq // tm
    full = lambda shape: pl.BlockSpec(shape, lambda i: (0,) * len(shape))
    tile = lambda w: pl.BlockSpec((tm, w), lambda i: (i, 0))
    return pl.pallas_call(
        functools.partial(_out_router_kernel, tm=tm),
        grid=(N // tm,),
        in_specs=[
            tile(D), tile(ATTN_WIDTH),
            pl.BlockSpec((tm, SSM_WIDTH), lambda i: (i % per_b, i // per_b)),
            tile(CONV_WIDTH),
            full((1, ATTN_WIDTH)), full((D, D)), full((1, D)), full((D, 2 * LANES)), full((1, LANES)),
        ],
        out_specs=[tile(D), tile(D_PACK), tile(LANES), pl.BlockSpec((ROUTE_ROWS, tm), lambda i: (0, i)),
                   full((1, LANES)), pl.BlockSpec((SUBLANES, LANES), lambda i: (i, 0))],
        out_shape=[
            jax.ShapeDtypeStruct((N, D), F32),
            jax.ShapeDtypeStruct((N, D_PACK), U32),
            jax.ShapeDtypeStruct((N, LANES), F32),
            jax.ShapeDtypeStruct((ROUTE_ROWS, N), F32),
            jax.ShapeDtypeStruct((1, LANES), F32),
            jax.ShapeDtypeStruct((N // tm * SUBLANES, LANES), F32),
        ],
        scratch_shapes=[pltpu.VMEM((1, LANES), F32)],
        compiler_params=_params("arbitrary"),
    )(x, ya, ys_tm, yc, ga, wo, fn, wr, br)


def _dispatch(dest_flat, h2, n_rows):
    N, D = h2.shape
    workers = SC_CORES * SC_SUBCORES
    per_worker = 2 * N // workers
    chunks = per_worker // SC_CHUNK
    idx = dest_flat.reshape(workers, chunks, SC_CHUNK)
    mesh = plsc.VectorSubcoreMesh(core_axis_name="c", subcore_axis_name="s")

    @functools.partial(
        pl.kernel, mesh=mesh,
        out_type=jax.ShapeDtypeStruct((n_rows, D), h2.dtype),
        scratch_types=[pltpu.VMEM((chunks, SC_CHUNK), jnp.int32), pltpu.VMEM((2, SC_CHUNK, D), h2.dtype),
                       pltpu.SemaphoreType.DMA((2,)), pltpu.SemaphoreType.DMA((2,))],
    )
    def scatter(h_hbm, idx_hbm, out_hbm, idx_v, rows_v, sem_in, sem_out):
        wid = lax.axis_index("s") * SC_CORES + lax.axis_index("c")
        pltpu.sync_copy(idx_hbm.at[wid], idx_v)

        def load(c):
            first = lax.rem(wid * per_worker + c * SC_CHUNK, N)
            src = h_hbm.at[pl.ds(pl.multiple_of(first, SC_CHUNK), SC_CHUNK)]
            return pltpu.async_copy(src, rows_v.at[c % 2], sem_in.at[c % 2])

        loads = {0: load(0)}
        stores = {}
        for c in range(chunks):
            loads[c].wait()
            if c >= 1:
                stores[c - 1].wait()
            if c + 1 < chunks:
                loads[c + 1] = load(c + 1)
            stores[c] = pltpu.async_copy(rows_v.at[c % 2], out_hbm.at[idx_v.at[c]], sem_out.at[c % 2])
        stores[chunks - 1].wait()

    return scatter(h2, idx)


def _experts_kernel(blk_ref, exp_ref, valid_ref, used_ref, x_ref, w1_ref, w3_ref, w2_ref, y_ref, w13_s, w2_s):
    j = pl.program_id(0)

    @pl.when(j < used_ref[0])
    def _():
        prev = exp_ref[jnp.maximum(j - 1, 0)]

        @pl.when((j == 0) | (exp_ref[j] != prev))
        def _():
            for c in range(D_FF_E // FF_CHUNK):
                w13_s[:, 2 * FF_CHUNK * c:2 * FF_CHUNK * c + FF_CHUNK] = \
                    w1_ref[0, 0, :, FF_CHUNK * c:FF_CHUNK * (c + 1)].astype(BF16)
                w13_s[:, 2 * FF_CHUNK * c + FF_CHUNK:2 * FF_CHUNK * (c + 1)] = \
                    w3_ref[0, 0, :, FF_CHUNK * c:FF_CHUNK * (c + 1)].astype(BF16)
            w2_s[...] = w2_ref[0, 0].astype(BF16)

        x = _unpack_rows(x_ref[...])
        y = None
        for c in range(D_FF_E // FF_CHUNK):
            ab = jnp.dot(x, w13_s[:, 2 * FF_CHUNK * c:2 * FF_CHUNK * (c + 1)], preferred_element_type=F32)
            a = ab[:, 0:FF_CHUNK]
            hmid = (a * jax.nn.sigmoid(a) * ab[:, FF_CHUNK:]).astype(BF16)
            part = jnp.dot(hmid, w2_s[FF_CHUNK * c:FF_CHUNK * (c + 1), :], preferred_element_type=F32)
            y = part if y is None else y + part
        row = lax.broadcasted_iota(jnp.int32, (TB_EXP, 1), 0)
        y_ref[...] = jnp.where(row < valid_ref[j], _pack_rows(y), jnp.uint32(0))


def _experts(blk_map, blk_exp, blk_valid, n_used, x_rows, w1, w3, w2, layer):
    R, D = x_rows.shape[0], D_MODEL
    nb = R // TB_EXP
    return pl.pallas_call(
        _experts_kernel,
        grid_spec=pltpu.PrefetchScalarGridSpec(
            num_scalar_prefetch=4,
            grid=(nb,),
            in_specs=[
                pl.BlockSpec((TB_EXP, D_PACK), lambda j, bm, be, bv, nu: (bm[j], 0)),
                pl.BlockSpec((1, 1, D, D_FF_E), lambda j, bm, be, bv, nu: (layer, be[j], 0, 0)),
                pl.BlockSpec((1, 1, D, D_FF_E), lambda j, bm, be, bv, nu: (layer, be[j], 0, 0)),
                pl.BlockSpec((1, 1, D_FF_E, D), lambda j, bm, be, bv, nu: (layer, be[j], 0, 0)),
            ],
            out_specs=pl.BlockSpec((TB_EXP, D_PACK), lambda j, bm, be, bv, nu: (bm[j], 0)),
            scratch_shapes=[pltpu.VMEM((D, 2 * D_FF_E), BF16), pltpu.VMEM((D_FF_E, D), BF16)],
        ),
        out_shape=jax.ShapeDtypeStruct((R, D_PACK), U32),
        compiler_params=_params("arbitrary"),
    )(blk_map, blk_exp, blk_valid, n_used, x_rows, w1, w3, w2)


def _combine_kernel(src_ref, nslot_ref, x1_ref, route_ref, off_ref, rows_ref, o_ref, buf, sem, *, tm):
    i = pl.program_id(0)
    cur = i % 2

    def slot_copy(tile, half, s):
        src = pl.multiple_of(src_ref[tile * RUN_SLOTS + s], RUN_ROWS)
        dst = buf.at[half, pl.ds(pl.multiple_of(s * RUN_ROWS, RUN_ROWS), RUN_ROWS)]
        return pltpu.make_async_copy(rows_ref.at[pl.ds(src, RUN_ROWS)], dst, sem.at[half])

    def fetch(tile, half):
        def issue(g, c):
            for u in range(RUN_GROUP):
                slot_copy(tile, half, g * RUN_GROUP + u).start()
            return c
        lax.fori_loop(0, nslot_ref[tile], issue, 0)

    def wait(g, c):
        rows = RUN_GROUP * RUN_ROWS
        pltpu.make_async_copy(rows_ref.at[pl.ds(0, rows)], buf.at[cur, pl.ds(0, rows)], sem.at[cur]).wait()
        return c

    @pl.when(i == 0)
    def _():
        buf[...] = jnp.zeros_like(buf)
        fetch(0, 0)

    @pl.when(i + 1 < pl.num_programs(0))
    def _():
        fetch(i + 1, 1 - cur)

    route = route_ref[...]
    lane = lax.broadcasted_iota(jnp.int32, (tm, LANES), 1).astype(F32)
    off = off_ref[0]

    def buf_row(e_lane, r_lane):
        e = route[:, e_lane:e_lane + 1]
        base = jnp.sum(jnp.where(lane == e, off, 0.0), axis=-1, keepdims=True)
        return (base + route[:, r_lane:r_lane + 1]).astype(jnp.int32)

    col = lax.broadcasted_iota(jnp.int32, (tm, RUN_SLOTS * RUN_ROWS), 1)
    pick = (jnp.where(col == buf_row(ROUTE_E1, ROUTE_R1), route[:, ROUTE_G1:ROUTE_G1 + 1], 0.0)
            + jnp.where(col == buf_row(ROUTE_E2, ROUTE_R2), route[:, ROUTE_G2:ROUTE_G2 + 1], 0.0))
    lax.fori_loop(0, nslot_ref[i], wait, 0)
    o_ref[...] = x1_ref[...] + jnp.dot(pick.astype(BF16), _unpack_rows(buf[cur]), preferred_element_type=F32)


def _combine(src, nslot, x1, route, off, y_rows):
    N, D = x1.shape
    tm = min(TM_MOVE, N)
    return pl.pallas_call(
        functools.partial(_combine_kernel, tm=tm),
        grid_spec=pltpu.PrefetchScalarGridSpec(
            num_scalar_prefetch=2,
            grid=(N // tm,),
            in_specs=[
                pl.BlockSpec((tm, D), lambda i, s, n: (i, 0)),
                pl.BlockSpec((tm, LANES), lambda i, s, n: (i, 0)),
                pl.BlockSpec((1, 1, LANES), lambda i, s, n: (i, 0, 0)),
                pl.BlockSpec(memory_space=pl.ANY),
            ],
            out_specs=pl.BlockSpec((tm, D), lambda i, s, n: (i, 0)),
            scratch_shapes=[pltpu.VMEM((2, RUN_SLOTS * RUN_ROWS, D_PACK), U32), pltpu.SemaphoreType.DMA((2,))],
        ),
        out_shape=jax.ShapeDtypeStruct((N, D), F32),
        compiler_params=_params("arbitrary"),
    )(src, nslot, x1, route, off, y_rows)


def _moe(x1, h2, route, route_t, counts, sub_carry, w1, w3, w2, layer):
    N, _ = x1.shape
    tm = min(TM_MOVE, N)
    n_tiles = N // tm
    experts = jnp.arange(N_EXPERTS, dtype=jnp.int32)
    nb = (2 * N + N_EXPERTS * (TB_EXP - 1)) // TB_EXP + 1
    e_id = route_t[ROUTE_E1:ROUTE_E2 + 1].astype(jnp.int32)
    rank = route_t[ROUTE_R1:ROUTE_R2 + 1].astype(jnp.int32)
    cnt = counts[0, :N_EXPERTS].astype(jnp.int32)
    padded = (cnt + TB_EXP - 1) // TB_EXP * TB_EXP
    pad_end = jnp.cumsum(padded)
    pad_start = pad_end - padded
    e_flat = e_id.reshape(1, 2 * N)
    dest = jnp.sum(jnp.where(e_flat == experts[:, None], pad_start[:, None], 0), axis=0) + rank.reshape(2 * N)
    n_used = pad_end[-1] // TB_EXP
    blk_map = jnp.minimum(jnp.arange(nb, dtype=jnp.int32), jnp.maximum(n_used - 1, 0))
    blk_exp = jnp.sum(pad_end[None, :] <= (blk_map * TB_EXP)[:, None], axis=1).astype(jnp.int32)
    blk_exp = jnp.minimum(blk_exp, N_EXPERTS - 1)
    is_blk_e = blk_exp[:, None] == experts[None, :]
    blk_valid = jnp.clip(jnp.sum(jnp.where(is_blk_e, (pad_start + cnt)[None, :], 0), axis=1) - blk_map * TB_EXP,
                         0, TB_EXP).astype(jnp.int32)

    per_router_tile = TM_OUT // tm if N >= TM_OUT else 1
    carry = sub_carry.reshape(-1, SUBLANES, LANES)[:, :per_router_tile, :N_EXPERTS].reshape(n_tiles, N_EXPERTS)
    carry = carry.astype(jnp.int32)
    tile_cnt = jnp.concatenate([carry[1:], cnt[None, :]], axis=0) - carry
    run_start = pad_start[None, :] + carry
    first_blk = run_start // RUN_ROWS
    nslot = jnp.where(tile_cnt > 0, (run_start + tile_cnt - 1) // RUN_ROWS - first_blk + 1, 0)
    slot_end = jnp.cumsum(nslot, axis=1)
    slot_base = slot_end - nslot
    slots = jnp.arange(RUN_SLOTS, dtype=jnp.int32)
    slot_e = jnp.minimum(jnp.sum(slot_end[:, None, :] <= slots[None, :, None], axis=2), N_EXPERTS - 1)
    is_slot_e = slot_e[:, :, None] == experts[None, None, :]
    pick = lambda tbl: jnp.sum(jnp.where(is_slot_e, tbl[:, None, :], 0), axis=2)
    src = (pick(first_blk) + slots[None, :] - pick(slot_base)) * RUN_ROWS
    src = jnp.where(slots[None, :] < slot_end[:, -1:], src, 0)
    src = jnp.clip(src, 0, nb * TB_EXP - RUN_ROWS).reshape(-1).astype(jnp.int32)
    n_groups = (slot_end[:, -1] + RUN_GROUP - 1) // RUN_GROUP
    off = (slot_base - first_blk) * RUN_ROWS + pad_start[None, :]
    off = jnp.pad(off.astype(F32), ((0, 0), (0, LANES - N_EXPERTS)))

    x_rows = _dispatch(dest, h2, nb * TB_EXP)
    y_rows = _experts(blk_map, blk_exp, blk_valid, n_used.reshape(1).astype(jnp.int32), x_rows, w1, w3, w2, layer)
    return _combine(src, n_groups.astype(jnp.int32), x1, route, off.reshape(n_tiles, 1, LANES), y_rows)


def _pad_heads(w, width):
    k = w.shape[0]
    w = w.reshape(k, N_HEADS, width)
    return jnp.pad(w, ((0, 0), (0, 0), (0, HEAD_PAD - width))).reshape(k, N_HEADS * HEAD_PAD)


def _swap_rope(w):
    half = QK_ROPE // 2
    lo, hi = w[..., QK_NOPE:QK_NOPE + half], w[..., QK_NOPE + half:QK_HEAD]
    pad = [(0, 0)] * (w.ndim - 1)
    return jnp.pad(jnp.concatenate([hi, lo], -1), pad + [(QK_NOPE, HEAD_PAD - QK_HEAD)])


def _rope_tables(positions):
    inv_freq = ROPE_THETA ** (-jnp.arange(0, QK_ROPE, 2, dtype=F32) / QK_ROPE)
    ang = positions.astype(F32)[..., None] * inv_freq
    table = jnp.concatenate([jnp.cos(ang), jnp.sin(ang)], -1)
    return jnp.pad(table, ((0, 0), (0, 0), (QK_NOPE, LANES - QK_HEAD)))


def _ssm_params(lam_re, lam_im, b_re, b_im, c_re, c_im, log_dt):
    lam = lax.complex(lam_re, lam_im)
    dt = jnp.exp(log_dt)[:, None]
    lam_bar = jnp.exp(lam * dt)
    b_bar = ((lam_bar - 1.0) / lam)[..., None] * lax.complex(b_re, b_im)
    eye = jnp.eye(SSM_GROUPS, dtype=F32)

    def in_blockdiag(m):
        return jnp.einsum("gpc,gh->gchp", m, eye).reshape(SSM_WIDTH, SSM_FLAT)

    def out_blockdiag(m):
        return jnp.einsum("gcp,gh->gphc", m, eye).reshape(SSM_FLAT, SSM_WIDTH)

    bbd = jnp.concatenate([in_blockdiag(jnp.real(b_bar)), in_blockdiag(jnp.imag(b_bar))], axis=1)
    cbd = jnp.concatenate([out_blockdiag(c_re), out_blockdiag(-c_im)], axis=0)
    lam_rows = jnp.stack([jnp.real(lam_bar).reshape(SSM_FLAT), jnp.imag(lam_bar).reshape(SSM_FLAT)])
    return bbd.astype(BF16), lam_rows, cbd.astype(BF16)


def kernel(x, positions, mix_norm, w_in, q_a_norm, w_uq, kv_a_norm, w_ukv, q_norm, k_norm, ssm_lam_re, ssm_lam_im, ssm_b_re, ssm_b_im, ssm_c_re, ssm_c_im, ssm_d, ssm_log_dt, ssm_w_glu, conv_dw_w, conv_dw_b, conv_ln_w, conv_ln_b, out_norm, w_out, ffn_norm, w_grp, b_grp, w_exp, b_exp, w1, w3, w2):
    B, L, D = x.shape
    depth = w_in.shape[0]
    rope = _rope_tables(positions)
    row = lambda v: v.reshape(1, -1)
    lane_pad = lambda v: jnp.pad(v, (0, LANES - v.shape[0])).reshape(1, LANES)
    for l in range(depth):
        c_q, c_kv, k_pe, u_s, c_a, c_g = jnp.split(
            w_in[l], [Q_LORA, Q_LORA + KV_LORA, Q_LORA + KV_LORA + QK_ROPE,
                      Q_LORA + KV_LORA + QK_ROPE + SSM_WIDTH,
                      Q_LORA + KV_LORA + QK_ROPE + SSM_WIDTH + CONV_WIDTH], axis=1)
        k_pe_full = jnp.pad(k_pe, ((0, 0), (QK_NOPE, 0)))
        win = jnp.concatenate([c_q, c_kv, u_s, c_a, c_g, jnp.pad(k_pe_full, ((0, 0), (0, HEAD_PAD - QK_HEAD))),
                               _swap_rope(k_pe_full)], axis=1).astype(BF16)
        wkv = w_ukv[l].reshape(KV_LORA, N_HEADS, QK_NOPE + V_HEAD)
        wuk = _pad_heads(wkv[:, :, :QK_NOPE].reshape(KV_LORA, N_HEADS * QK_NOPE), QK_NOPE).astype(BF16)
        wuv = wkv[:, :, QK_NOPE:].reshape(KV_LORA, ATTN_WIDTH).astype(BF16)
        wuq_sw = _swap_rope(w_uq[l].reshape(Q_LORA, N_HEADS, QK_HEAD)).reshape(Q_LORA, N_HEADS * HEAD_PAD)
        wuq = jnp.concatenate([_pad_heads(w_uq[l], QK_HEAD), wuq_sw], axis=1).astype(BF16)
        qn_scaled = q_norm[l] * (QK_HEAD ** -0.5)
        norm_rows = lambda w: jnp.stack([jnp.pad(w, (0, HEAD_PAD - QK_HEAD)), _swap_rope(w)])
        q, k, v, u_tm, ca, cg = _in_proj(
            x, rope, row(mix_norm[l]), win, row(q_a_norm[l]), wuq, row(kv_a_norm[l]), wuk, wuv,
            norm_rows(qn_scaled), norm_rows(k_norm[l]))
        y_attn = _attention(q, k, v)

        g_out = out_norm[l]
        bbd, lam_rows, cbd = _ssm_params(ssm_lam_re[l], ssm_lam_im[l], ssm_b_re[l], ssm_b_im[l],
                                         ssm_c_re[l], ssm_c_im[l], ssm_log_dt[l])
        y_ssm_tm = _ssm(u_tm.reshape(L * B, SSM_WIDTH), bbd, lam_rows, cbd, row(ssm_d[l]),
                        ssm_w_glu[l].astype(BF16), row(g_out[ATTN_WIDTH:ATTN_WIDTH + SSM_WIDTH]), B)
        y_conv = _conv(ca, cg, conv_dw_w[l], row(conv_dw_b[l]), row(conv_ln_w[l]), row(conv_ln_b[l]),
                       row(g_out[ATTN_WIDTH + SSM_WIDTH:]))

        w_route = jnp.pad(jnp.concatenate([w_exp[l], w_grp[l]], axis=1), ((0, 0), (0, LANES - N_EXPERTS - N_EGROUPS)))
        w_route_hi = w_route.astype(BF16)
        w_route = jnp.concatenate([w_route_hi, (w_route - w_route_hi.astype(F32)).astype(BF16)], axis=1)
        b_route = lane_pad(jnp.concatenate([b_exp[l], b_grp[l]]))
        x1, h2, route, route_t, counts, sub_carry = _out_router(
            x.reshape(B * L, D), y_attn.reshape(B * L, ATTN_WIDTH), y_ssm_tm.reshape(L, B * SSM_WIDTH),
            y_conv.reshape(B * L, CONV_WIDTH), row(g_out[:ATTN_WIDTH]), w_out[l].astype(BF16),
            row(ffn_norm[l]), w_route, b_route, B)
        x = _moe(x1, h2, route, route_t, counts, sub_carry, w1, w3, w2, l).reshape(B, L, D)
    return x
```

```python
import functools
import math

import jax
import jax.numpy as jnp
from jax import lax
from jax.experimental import pallas as pl
from jax.experimental.pallas import tpu as pltpu
from jax.experimental.pallas import tpu_sc as plsc

D_MODEL = 1024
CHUNK = 64
EPS = 1e-6
N_HEADS = 8
QK_NOPE = 64
QK_ROPE = 32
QK_HEAD = QK_NOPE + QK_ROPE
V_HEAD = 64
Q_LORA = 256
KV_LORA = 128
ROPE_THETA = 10000.0
ATTN_WIDTH = N_HEADS * V_HEAD
SSM_WIDTH = 256
SSM_GROUP = 16
SSM_GROUPS = SSM_WIDTH // SSM_GROUP
SSM_STATE = 64
SSM_FLAT = SSM_GROUPS * SSM_STATE
CONV_WIDTH = 256
CONV_K = 31
N_EGROUPS = 4
EXP_PER_GROUP = 8
N_EXPERTS = N_EGROUPS * EXP_PER_GROUP
D_FF_E = 512

LANES = 128
SUBLANES = 8
HEAD_PAD = LANES
IN_PROJ_PAD = Q_LORA + KV_LORA + SSM_WIDTH + 2 * CONV_WIDTH + 2 * LANES
CONV_HALO = 32
VMEM_LIMIT = 48 * 1024 * 1024

TM_PROJ = 512
TQ_ATTN = 256
TC_SSM = 64
T_CONV = 256
CONV_SUB = 64
TM_OUT = 512
TM_MOVE = 256
RUN_ROWS = SUBLANES
RUN_SLOTS = 2 * N_EXPERTS + 2 * TM_MOVE // RUN_ROWS
RUN_GROUP = 8
SC_CORES, SC_SUBCORES = 2, 16
SC_CHUNK = 64
TB_EXP = 512
FF_CHUNK = 256

BF16 = jnp.bfloat16
F32 = jnp.float32
U32 = jnp.uint32
D_PACK = D_MODEL // 2


def _pack_rows(v):
    bits = lax.bitcast_convert_type(v.astype(BF16).astype(F32), U32)
    half = v.shape[1] // 2
    return bits[:, 0:half] | (bits[:, half:] >> 16)


def _unpack_rows(w):
    hi = lax.bitcast_convert_type(w & jnp.uint32(0xFFFF0000), F32)
    lo = lax.bitcast_convert_type(w << 16, F32)
    return jnp.concatenate([hi, lo], axis=1).astype(BF16)


def _rms(x, w):
    return x * lax.rsqrt(jnp.mean(x * x, axis=-1, keepdims=True) + EPS) * w


def _params(*sem):
    return pltpu.CompilerParams(dimension_semantics=sem, vmem_limit_bytes=VMEM_LIMIT)


def _in_proj_kernel(x_ref, rope_ref, mixn_ref, win_ref, qan_ref, wuq_ref, kvan_ref, wuk_ref, wuv_ref,
                    qn_ref, kn_ref, q_ref, k_ref, v_ref, u_ref, ca_ref, cg_ref):
    x = x_ref[0]
    h = _rms(x, mixn_ref[...]).astype(BF16)
    proj = jnp.dot(h, win_ref[...], preferred_element_type=F32)
    o = 0
    c_q = proj[:, o:o + Q_LORA]; o += Q_LORA
    c_kv = proj[:, o:o + KV_LORA]; o += KV_LORA
    u_ref[...] = proj[:, o:o + SSM_WIDTH].astype(BF16); o += SSM_WIDTH
    ca_ref[0] = proj[:, o:o + CONV_WIDTH].astype(BF16); o += CONV_WIDTH
    cg_ref[0] = proj[:, o:o + CONV_WIDTH].astype(BF16); o += CONV_WIDTH
    k_pe = proj[:, o:o + LANES]; o += LANES
    k_pe_sw = proj[:, o:o + LANES]

    width = N_HEADS * HEAD_PAD
    q2 = jnp.dot(_rms(c_q, qan_ref[...]).astype(BF16), wuq_ref[...], preferred_element_type=F32)
    q, q_sw = q2[:, 0:width], q2[:, width:2 * width]
    ckv_n = _rms(c_kv, kvan_ref[...]).astype(BF16)
    kn = jnp.dot(ckv_n, wuk_ref[...], preferred_element_type=F32)
    v_ref[0] = jnp.dot(ckv_n, wuv_ref[...], preferred_element_type=F32).astype(BF16)

    rope = rope_ref[0]
    half = QK_ROPE // 2
    lane = lax.broadcasted_iota(jnp.int32, rope.shape, 1)
    lo = (lane >= QK_NOPE) & (lane < QK_NOPE + half)
    hi = (lane >= QK_NOPE + half) & (lane < QK_HEAD)
    cos_t = jnp.where(lane < QK_NOPE, 1.0, jnp.where(lo, rope, jnp.where(hi, pltpu.roll(rope, half, 1), 0.0)))
    sin_t = jnp.where(lo, -pltpu.roll(rope, LANES - half, 1), jnp.where(hi, rope, 0.0))
    a_q, b_q = qn_ref[0:1, :] * cos_t, qn_ref[1:2, :] * sin_t
    a_k = kn_ref[0:1, :] * cos_t
    k_sw_term = k_pe_sw * (kn_ref[1:2, :] * sin_t)

    def inv_rms(y):
        return lax.rsqrt(jnp.sum(y * y, axis=-1, keepdims=True) * (1.0 / QK_HEAD) + EPS)

    for hd in range(N_HEADS):
        sl = slice(hd * HEAD_PAD, (hd + 1) * HEAD_PAD)
        yq = q[:, sl]
        q_ref[0, hd] = ((yq * a_q + q_sw[:, sl] * b_q) * inv_rms(yq)).astype(BF16)
        yk = kn[:, sl] + k_pe
        k_ref[0, hd] = ((yk * a_k + k_sw_term) * inv_rms(yk)).astype(BF16)


def _in_proj(x, rope, mixn, win, qan, wuq, kvan, wuk, wuv, qn, kn):
    B, L, D = x.shape
    tm = min(TM_PROJ, L)
    full = lambda shape: pl.BlockSpec(shape, lambda b, t: (0,) * len(shape))
    return pl.pallas_call(
        _in_proj_kernel,
        grid=(B, L // tm),
        in_specs=[
            pl.BlockSpec((1, tm, D), lambda b, t: (b, t, 0)),
            pl.BlockSpec((1, tm, LANES), lambda b, t: (b, t, 0)),
            full((1, D)), full((D, IN_PROJ_PAD)), full((1, Q_LORA)), full((Q_LORA, 2 * N_HEADS * HEAD_PAD)),
            full((1, KV_LORA)), full((KV_LORA, N_HEADS * HEAD_PAD)), full((KV_LORA, ATTN_WIDTH)),
            full((2, HEAD_PAD)), full((2, HEAD_PAD)),
        ],
        out_specs=[
            pl.BlockSpec((1, N_HEADS, tm, HEAD_PAD), lambda b, t: (b, 0, t, 0)),
            pl.BlockSpec((1, N_HEADS, tm, HEAD_PAD), lambda b, t: (b, 0, t, 0)),
            pl.BlockSpec((1, tm, ATTN_WIDTH), lambda b, t: (b, t, 0)),
            pl.BlockSpec((tm, SSM_WIDTH), lambda b, t: (t, b)),
            pl.BlockSpec((1, tm, CONV_WIDTH), lambda b, t: (b, t, 0)),
            pl.BlockSpec((1, tm, CONV_WIDTH), lambda b, t: (b, t, 0)),
        ],
        out_shape=[
            jax.ShapeDtypeStruct((B, N_HEADS, L, HEAD_PAD), BF16),
            jax.ShapeDtypeStruct((B, N_HEADS, L, HEAD_PAD), BF16),
            jax.ShapeDtypeStruct((B, L, ATTN_WIDTH), BF16),
            jax.ShapeDtypeStruct((L, B * SSM_WIDTH), BF16),
            jax.ShapeDtypeStruct((B, L, CONV_WIDTH), BF16),
            jax.ShapeDtypeStruct((B, L, CONV_WIDTH), BF16),
        ],
        compiler_params=_params("parallel", "parallel"),
    )(x, rope, mixn, win, qan, wuq, kvan, wuk, wuv, qn, kn)


def _attention_kernel(q_ref, k_ref, v_ref, o_ref, vext_ref, *, seq, tq):
    pair = 2 * V_HEAD
    vext_ref[:, 0:pair] = v_ref[0]
    vext_ref[:, pair:] = jnp.ones((seq, LANES), BF16)
    row_chunk = lax.broadcasted_iota(jnp.int32, (tq, tq), 0) // CHUNK
    col_chunk = lax.broadcasted_iota(jnp.int32, (tq, tq), 1) // CHUNK
    visible = col_chunk <= row_chunk
    neg = jnp.finfo(F32).min
    nt = (((1,), (1,)), ((), ()))
    low_lanes = lax.broadcasted_iota(jnp.int32, (tq, pair), 1) < V_HEAD
    for i in range(seq // tq):
        q0 = i * tq
        outs = []
        for hh in range(2):
            qb = q_ref[0, hh, q0:q0 + tq, :]
            s_d = lax.dot_general(qb, k_ref[0, hh, q0:q0 + tq, :], nt, preferred_element_type=F32)
            s_d = jnp.where(visible, s_d, neg)
            m = jnp.max(s_d, axis=-1, keepdims=True)
            if i > 0:
                s_l = lax.dot_general(qb, k_ref[0, hh, 0:q0, :], nt, preferred_element_type=F32)
                m = jnp.maximum(m, jnp.max(s_l, axis=-1, keepdims=True))
            acc = jnp.dot(jnp.exp(s_d - m).astype(BF16), vext_ref[q0:q0 + tq, :], preferred_element_type=F32)
            if i > 0:
                acc = acc + jnp.dot(jnp.exp(s_l - m).astype(BF16), vext_ref[0:q0, :], preferred_element_type=F32)
            outs.append(acc[:, 0:pair] / acc[:, pair:])
        o_ref[0, q0:q0 + tq, :] = jnp.where(low_lanes, outs[0], outs[1]).astype(BF16)


def _attention(q, k, v):
    B, H, L, _ = q.shape
    tq = min(TQ_ATTN, L)
    return pl.pallas_call(
        functools.partial(_attention_kernel, seq=L, tq=tq),
        grid=(B, H // 2),
        in_specs=[
            pl.BlockSpec((1, 2, L, HEAD_PAD), lambda b, p: (b, p, 0, 0)),
            pl.BlockSpec((1, 2, L, HEAD_PAD), lambda b, p: (b, p, 0, 0)),
            pl.BlockSpec((1, L, 2 * V_HEAD), lambda b, p: (b, 0, p)),
        ],
        out_specs=pl.BlockSpec((1, L, 2 * V_HEAD), lambda b, p: (b, 0, p)),
        out_shape=jax.ShapeDtypeStruct((B, L, ATTN_WIDTH), BF16),
        scratch_shapes=[pltpu.VMEM((L, 2 * V_HEAD + LANES), BF16)],
        compiler_params=_params("parallel", "parallel"),
    )(q, k, v)


def _ssm_kernel(u_ref, bbd_ref, lam_ref, cbd_ref, d_ref, wglu_ref, g_ref, o_ref, state_ref, bu_a, bu_b, *, batch, tc):
    @pl.when(pl.program_id(0) == 0)
    def _():
        state_ref[...] = jnp.zeros_like(state_ref)

    half = tc // 2 * batch
    halves = ((bu_a, slice(0, half)), (bu_b, slice(half, 2 * half)))
    for bu_ref, rows in halves:
        bu_ref[...] = jnp.dot(u_ref[rows, :], bbd_ref[...], preferred_element_type=F32)
    lam_re = jnp.broadcast_to(lam_ref[0:1, :], (batch, SSM_FLAT))
    lam_im = jnp.broadcast_to(lam_ref[1:2, :], (batch, SSM_FLAT))

    carry = (state_ref[:, 0:SSM_FLAT], state_ref[:, SSM_FLAT:2 * SSM_FLAT])
    for bu_ref, _ in halves:
        def step(t, xs, bu_ref=bu_ref):
            xr, xi = xs
            at = pl.ds(pl.multiple_of(t * batch, batch), batch)
            nr = lam_re * xr - lam_im * xi + bu_ref[at, 0:SSM_FLAT]
            ni = lam_re * xi + lam_im * xr + bu_ref[at, SSM_FLAT:2 * SSM_FLAT]
            bu_ref[at, 0:SSM_FLAT] = nr
            bu_ref[at, SSM_FLAT:2 * SSM_FLAT] = ni
            return nr, ni

        carry = lax.fori_loop(0, tc // 2, step, carry, unroll=True)
    state_ref[:, 0:SSM_FLAT] = carry[0]
    state_ref[:, SSM_FLAT:2 * SSM_FLAT] = carry[1]

    for bu_ref, rows in halves:
        y = jnp.dot(bu_ref[...].astype(BF16), cbd_ref[...], preferred_element_type=F32)
        y = y + d_ref[...] * u_ref[rows, :].astype(F32)
        z = jax.nn.gelu(y)
        gate = jax.nn.sigmoid(jnp.dot(z.astype(BF16), wglu_ref[...], preferred_element_type=F32))
        o_ref[rows, :] = _rms(z * gate, g_ref[...]).astype(BF16)


def _ssm(u_tm, bbd, lam, cbd, d, wglu, g, batch):
    rows = u_tm.shape[0]
    seq = rows // batch
    tc = min(TC_SSM, seq)
    blk = tc * batch
    full = lambda shape: pl.BlockSpec(shape, lambda t: (0,) * len(shape))
    return pl.pallas_call(
        functools.partial(_ssm_kernel, batch=batch, tc=tc),
        grid=(seq // tc,),
        in_specs=[
            pl.BlockSpec((blk, SSM_WIDTH), lambda t: (t, 0)),
            full((SSM_WIDTH, 2 * SSM_FLAT)), full((2, SSM_FLAT)), full((2 * SSM_FLAT, SSM_WIDTH)),
            full((1, SSM_WIDTH)), full((SSM_WIDTH, SSM_WIDTH)), full((1, SSM_WIDTH)),
        ],
        out_specs=pl.BlockSpec((blk, SSM_WIDTH), lambda t: (t, 0)),
        out_shape=jax.ShapeDtypeStruct((rows, SSM_WIDTH), BF16),
        scratch_shapes=[pltpu.VMEM((batch, 2 * SSM_FLAT), F32), pltpu.VMEM((blk // 2, 2 * SSM_FLAT), F32),
                        pltpu.VMEM((blk // 2, 2 * SSM_FLAT), F32)],
        compiler_params=_params("arbitrary"),
    )(u_tm, bbd, lam, cbd, d, wglu, g)


def _conv_kernel(a_ref, g_ref, ah_ref, gh_ref, w_ref, b_ref, lnw_ref, lnb_ref, gn_ref, o_ref, upad_ref, *, tt):
    first = pl.program_id(1) == 0
    halo = ah_ref[0].astype(F32) * jax.nn.sigmoid(gh_ref[0].astype(F32))
    upad_ref[0:CONV_HALO, :] = jnp.where(first, 0.0, halo)
    upad_ref[CONV_HALO:, :] = a_ref[0].astype(F32) * jax.nn.sigmoid(g_ref[0].astype(F32))
    base = CONV_HALO - (CONV_K - 1)
    for s in range(tt // CONV_SUB):
        r0 = s * CONV_SUB
        acc = jnp.zeros((CONV_SUB, CONV_WIDTH), F32)
        for off in range(SUBLANES):
            n = CONV_SUB + (SUBLANES if off else 0)
            part = None
            for a8 in range(0, CONV_HALO + 1, SUBLANES):
                kk = a8 + off - base
                if 0 <= kk < CONV_K:
                    term = w_ref[kk:kk + 1, :] * upad_ref[r0 + a8:r0 + a8 + n, :]
                    part = term if part is None else part + term
            acc = acc + part[off:off + CONV_SUB]
        y = acc + b_ref[...]
        mu = jnp.mean(y, axis=-1, keepdims=True)
        var = jnp.mean(jnp.square(y - mu), axis=-1, keepdims=True)
        y = (y - mu) * lax.rsqrt(var + 1e-5) * lnw_ref[...] + lnb_ref[...]
        y = y * jax.nn.sigmoid(y)
        o_ref[0, r0:r0 + CONV_SUB, :] = _rms(y, gn_ref[...]).astype(BF16)


def _conv(a, g, w, b, lnw, lnb, gn):
    B, L, C = a.shape
    tt = min(T_CONV, L)
    ratio = tt // CONV_HALO
    full = lambda shape: pl.BlockSpec(shape, lambda bb, t: (0,) * len(shape))
    cur = pl.BlockSpec((1, tt, C), lambda bb, t: (bb, t, 0))
    prev = pl.BlockSpec((1, CONV_HALO, C), lambda bb, t: (bb, jnp.maximum(t * ratio - 1, 0), 0))
    return pl.pallas_call(
        functools.partial(_conv_kernel, tt=tt),
        grid=(B, L // tt),
        in_specs=[cur, cur, prev, prev, full((CONV_K, C)), full((1, C)), full((1, C)), full((1, C)), full((1, C))],
        out_specs=pl.BlockSpec((1, tt, C), lambda bb, t: (bb, t, 0)),
        out_shape=jax.ShapeDtypeStruct((B, L, C), BF16),
        scratch_shapes=[pltpu.VMEM((tt + CONV_HALO, C), F32)],
        compiler_params=_params("parallel", "parallel"),
    )(a, g, a, g, w, b, lnw, lnb, gn)


ROUTE_E1, ROUTE_E2, ROUTE_R1, ROUTE_R2, ROUTE_G1, ROUTE_G2 = range(6)
ROUTE_ROWS = 8
GRP_LANE0 = N_EXPERTS


def _out_router_kernel(x_ref, ya_ref, ys_ref, yc_ref, ga_ref, wo_ref, fn_ref, wr_ref, br_ref,
                       x1_ref, h2_ref, route_ref, route_t_ref, cnt_ref, sub_ref, carry_ref, tri_ref, *, tm):
    @pl.when(pl.program_id(0) == 0)
    def _():
        carry_ref[...] = jnp.zeros_like(carry_ref)
        rr = lax.broadcasted_iota(jnp.int32, (tm, tm), 0)
        cc = lax.broadcasted_iota(jnp.int32, (tm, tm), 1)
        tri_ref[...] = (cc < rr).astype(BF16)

    ya = _rms(ya_ref[...].astype(F32), ga_ref[...]).astype(BF16)
    acc = jnp.dot(ya, wo_ref[0:ATTN_WIDTH, :], preferred_element_type=F32)
    acc += jnp.dot(ys_ref[...], wo_ref[ATTN_WIDTH:ATTN_WIDTH + SSM_WIDTH, :], preferred_element_type=F32)
    acc += jnp.dot(yc_ref[...], wo_ref[ATTN_WIDTH + SSM_WIDTH:, :], preferred_element_type=F32)
    x1 = x_ref[...] + acc
    x1_ref[...] = x1
    h2 = _rms(x1, fn_ref[...])
    h2_ref[...] = _pack_rows(h2)

    h_hi = h2.astype(BF16)
    h_lo = (h2 - h_hi.astype(F32)).astype(BF16)
    part = jnp.dot(h_hi, wr_ref[...], preferred_element_type=F32)
    logits = (part[:, 0:LANES] + part[:, LANES:2 * LANES]
              + jnp.dot(h_lo, wr_ref[:, 0:LANES], preferred_element_type=F32) + br_ref[...])
    lane = lax.broadcasted_iota(jnp.int32, (tm, LANES), 1)
    ninf = -jnp.inf
    big = LANES

    def first_argmax(vals, vmax):
        return jnp.min(jnp.where(vals == vmax, lane, big), axis=-1, keepdims=True)

    grp = jnp.where((lane >= GRP_LANE0) & (lane < GRP_LANE0 + N_EGROUPS), logits, ninf)
    gmax = jnp.max(grp, axis=-1, keepdims=True)
    gsel = first_argmax(grp, gmax) - GRP_LANE0
    p_grp = 1.0 / jnp.sum(jnp.exp(grp - gmax), axis=-1, keepdims=True)

    el = jnp.where((lane < N_EXPERTS) & ((lane // EXP_PER_GROUP) == gsel), logits, ninf)
    m1 = jnp.max(el, axis=-1, keepdims=True)
    e1 = first_argmax(el, m1)
    el2 = jnp.where(lane == e1, ninf, el)
    m2 = jnp.max(el2, axis=-1, keepdims=True)
    e2 = first_argmax(el2, m2)
    t2 = jnp.exp(m2 - m1)
    g1 = p_grp / (1.0 + t2)
    g2 = p_grp * t2 / (1.0 + t2)

    hit1 = lane == e1
    hit2 = lane == e2
    cnt = (hit1 | hit2).astype(F32)
    before = jnp.dot(tri_ref[...], cnt.astype(BF16), preferred_element_type=F32) + carry_ref[...]
    r1 = jnp.sum(jnp.where(hit1, before, 0.0), axis=-1, keepdims=True)
    r2 = jnp.sum(jnp.where(hit2, before, 0.0), axis=-1, keepdims=True)
    subs = [before[h * TM_MOVE:h * TM_MOVE + 1, :] for h in range(tm // TM_MOVE)]
    sub_ref[...] = jnp.concatenate(subs + [jnp.zeros((SUBLANES - len(subs), LANES), F32)], axis=0)
    carry_ref[...] += jnp.sum(cnt, axis=0, keepdims=True)
    cnt_ref[...] = carry_ref[...]

    rec = jnp.where(lane == ROUTE_E1, e1.astype(F32), 0.0)
    rec = jnp.where(lane == ROUTE_E2, e2.astype(F32), rec)
    rec = jnp.where(lane == ROUTE_R1, r1, rec)
    rec = jnp.where(lane == ROUTE_R2, r2, rec)
    rec = jnp.where(lane == ROUTE_G1, g1, rec)
    rec = jnp.where(lane == ROUTE_G2, g2, rec)
    route_ref[...] = rec
    route_t_ref[...] = rec.T[0:ROUTE_ROWS, :]


def _out_router(x, ya, ys_tm, yc, ga, wo, fn, wr, br, batch):
    N, D = x.shape
    seq = N // batch
    tm = min(TM_OUT, seq)
    per_b = seq // tm
    full = lambda shape: pl.BlockSpec(shape, lambda i: (0,) * len(shape))
    tile = lambda w: pl.BlockSpec((tm, w), lambda i: (i, 0))
    return pl.pallas_call(
        functools.partial(_out_router_kernel, tm=tm),
        grid=(N // tm,),
        in_specs=[
            tile(D), tile(ATTN_WIDTH),
            pl.BlockSpec((tm, SSM_WIDTH), lambda i: (i % per_b, i // per_b)),
            tile(CONV_WIDTH),
            full((1, ATTN_WIDTH)), full((D, D)), full((1, D)), full((D, 2 * LANES)), full((1, LANES)),
        ],
        out_specs=[tile(D), tile(D_PACK), tile(LANES), pl.BlockSpec((ROUTE_ROWS, tm), lambda i: (0, i)),
                   full((1, LANES)), pl.BlockSpec((SUBLANES, LANES), lambda i: (i, 0))],
        out_shape=[
            jax.ShapeDtypeStruct((N, D), F32),
            jax.ShapeDtypeStruct((N, D_PACK), U32),
            jax.ShapeDtypeStruct((N, LANES), F32),
            jax.ShapeDtypeStruct((ROUTE_ROWS, N), F32),
            jax.ShapeDtypeStruct((1, LANES), F32),
            jax.ShapeDtypeStruct((N // tm * SUBLANES, LANES), F32),
        ],
        scratch_shapes=[pltpu.VMEM((1, LANES), F32), pltpu.VMEM((tm, tm), BF16)],
        compiler_params=_params("arbitrary"),
    )(x, ya, ys_tm, yc, ga, wo, fn, wr, br)


def _dispatch(dest_flat, h2, n_rows):
    N, D = h2.shape
    workers = SC_CORES * SC_SUBCORES
    per_worker = 2 * N // workers
    chunks = per_worker // SC_CHUNK
    idx = dest_flat.reshape(workers, chunks, SC_CHUNK)
    mesh = plsc.VectorSubcoreMesh(core_axis_name="c", subcore_axis_name="s")

    @functools.partial(
        pl.kernel, mesh=mesh,
        out_type=jax.ShapeDtypeStruct((n_rows, D), h2.dtype),
        scratch_types=[pltpu.VMEM((chunks, SC_CHUNK), jnp.int32), pltpu.VMEM((2, SC_CHUNK, D), h2.dtype),
                       pltpu.SemaphoreType.DMA((2,)), pltpu.SemaphoreType.DMA((2,))],
    )
    def scatter(h_hbm, idx_hbm, out_hbm, idx_v, rows_v, sem_in, sem_out):
        wid = lax.axis_index("s") * SC_CORES + lax.axis_index("c")
        pltpu.sync_copy(idx_hbm.at[wid], idx_v)

        def load(c):
            first = lax.rem(wid * per_worker + c * SC_CHUNK, N)
            src = h_hbm.at[pl.ds(pl.multiple_of(first, SC_CHUNK), SC_CHUNK)]
            return pltpu.async_copy(src, rows_v.at[c % 2], sem_in.at[c % 2])

        loads = {0: load(0)}
        stores = {}
        for c in range(chunks):
            loads[c].wait()
            if c >= 1:
                stores[c - 1].wait()
            if c + 1 < chunks:
                loads[c + 1] = load(c + 1)
            stores[c] = pltpu.async_copy(rows_v.at[c % 2], out_hbm.at[idx_v.at[c]], sem_out.at[c % 2])
        stores[chunks - 1].wait()

    return scatter(h2, idx)


def _experts_kernel(blk_ref, exp_ref, valid_ref, slot_ref, next_ref, used_ref, x_ref, w1_hbm, w3_hbm, w2_hbm, y_ref,
                    w1_f, w3_f, w2_f, w13_s, w2_s, sem, *, layer):
    j = pl.program_id(0)

    def weight_copies(e, slot):
        return (pltpu.make_async_copy(w1_hbm.at[layer, e], w1_f.at[slot], sem.at[slot, 0]),
                pltpu.make_async_copy(w3_hbm.at[layer, e], w3_f.at[slot], sem.at[slot, 1]),
                pltpu.make_async_copy(w2_hbm.at[layer, e], w2_f.at[slot], sem.at[slot, 2]))

    @pl.when(j < used_ref[0])
    def _():
        e = exp_ref[j]
        slot = slot_ref[j]

        @pl.when(j == 0)
        def _():
            for cp in weight_copies(e, slot):
                cp.start()

        @pl.when((j == 0) | (e != exp_ref[jnp.maximum(j - 1, 0)]))
        def _():
            for cp in weight_copies(e, slot):
                cp.wait()
            for c in range(D_FF_E // FF_CHUNK):
                w13_s[:, 2 * FF_CHUNK * c:2 * FF_CHUNK * c + FF_CHUNK] = \
                    w1_f[slot, :, FF_CHUNK * c:FF_CHUNK * (c + 1)].astype(BF16)
                w13_s[:, 2 * FF_CHUNK * c + FF_CHUNK:2 * FF_CHUNK * (c + 1)] = \
                    w3_f[slot, :, FF_CHUNK * c:FF_CHUNK * (c + 1)].astype(BF16)
            w2_s[...] = w2_f[slot].astype(BF16)

            @pl.when(next_ref[j] >= 0)
            def _():
                for cp in weight_copies(next_ref[j], 1 - slot):
                    cp.start()

        x = _unpack_rows(x_ref[...])
        y = None
        for c in range(D_FF_E // FF_CHUNK):
            ab = jnp.dot(x, w13_s[:, 2 * FF_CHUNK * c:2 * FF_CHUNK * (c + 1)], preferred_element_type=F32)
            a = ab[:, 0:FF_CHUNK]
            hmid = (a * jax.nn.sigmoid(a) * ab[:, FF_CHUNK:]).astype(BF16)
            part = jnp.dot(hmid, w2_s[FF_CHUNK * c:FF_CHUNK * (c + 1), :], preferred_element_type=F32)
            y = part if y is None else y + part
        row = lax.broadcasted_iota(jnp.int32, (TB_EXP, 1), 0)
        y_ref[...] = jnp.where(row < valid_ref[j], _pack_rows(y), jnp.uint32(0))


def _experts(blk_map, blk_exp, blk_valid, blk_slot, blk_next, n_used, x_rows, w1, w3, w2, layer):
    R, D = x_rows.shape[0], D_MODEL
    nb = R // TB_EXP
    row_block = lambda j, bm, be, bv, bs, bn, nu: (bm[j], 0)
    return pl.pallas_call(
        functools.partial(_experts_kernel, layer=layer),
        grid_spec=pltpu.PrefetchScalarGridSpec(
            num_scalar_prefetch=6,
            grid=(nb,),
            in_specs=[
                pl.BlockSpec((TB_EXP, D_PACK), row_block),
                pl.BlockSpec(memory_space=pl.ANY), pl.BlockSpec(memory_space=pl.ANY), pl.BlockSpec(memory_space=pl.ANY),
            ],
            out_specs=pl.BlockSpec((TB_EXP, D_PACK), row_block),
            scratch_shapes=[
                pltpu.VMEM((2, D, D_FF_E), F32), pltpu.VMEM((2, D, D_FF_E), F32), pltpu.VMEM((2, D_FF_E, D), F32),
                pltpu.VMEM((D, 2 * D_FF_E), BF16), pltpu.VMEM((D_FF_E, D), BF16),
                pltpu.SemaphoreType.DMA((2, 3)),
            ],
        ),
        out_shape=jax.ShapeDtypeStruct((R, D_PACK), U32),
        compiler_params=_params("arbitrary"),
    )(blk_map, blk_exp, blk_valid, blk_slot, blk_next, n_used, x_rows, w1, w3, w2)


def _combine_kernel(src_ref, nslot_ref, x1_ref, route_ref, off_ref, rows_ref, o_ref, buf, sem, *, tm):
    i = pl.program_id(0)
    cur = i % 2

    def slot_copy(tile, half, s):
        src = pl.multiple_of(src_ref[tile * RUN_SLOTS + s], RUN_ROWS)
        dst = buf.at[half, pl.ds(pl.multiple_of(s * RUN_ROWS, RUN_ROWS), RUN_ROWS)]
        return pltpu.make_async_copy(rows_ref.at[pl.ds(src, RUN_ROWS)], dst, sem.at[half])

    def fetch(tile, half):
        def issue(g, c):
            for u in range(RUN_GROUP):
                slot_copy(tile, half, g * RUN_GROUP + u).start()
            return c
        lax.fori_loop(0, nslot_ref[tile], issue, 0)

    def wait(g, c):
        rows = RUN_GROUP * RUN_ROWS
        pltpu.make_async_copy(rows_ref.at[pl.ds(0, rows)], buf.at[cur, pl.ds(0, rows)], sem.at[cur]).wait()
        return c

    @pl.when(i == 0)
    def _():
        buf[...] = jnp.zeros_like(buf)
        fetch(0, 0)

    @pl.when(i + 1 < pl.num_programs(0))
    def _():
        fetch(i + 1, 1 - cur)

    route = route_ref[...]
    lane = lax.broadcasted_iota(jnp.int32, (tm, LANES), 1).astype(F32)
    off = off_ref[0]

    def buf_row(e_lane, r_lane):
        e = route[:, e_lane:e_lane + 1]
        base = jnp.sum(jnp.where(lane == e, off, 0.0), axis=-1, keepdims=True)
        return (base + route[:, r_lane:r_lane + 1]).astype(jnp.int32)

    col = lax.broadcasted_iota(jnp.int32, (tm, RUN_SLOTS * RUN_ROWS), 1)
    pick = (jnp.where(col == buf_row(ROUTE_E1, ROUTE_R1), route[:, ROUTE_G1:ROUTE_G1 + 1], 0.0)
            + jnp.where(col == buf_row(ROUTE_E2, ROUTE_R2), route[:, ROUTE_G2:ROUTE_G2 + 1], 0.0))
    lax.fori_loop(0, nslot_ref[i], wait, 0)
    o_ref[...] = x1_ref[...] + jnp.dot(pick.astype(BF16), _unpack_rows(buf[cur]), preferred_element_type=F32)


def _combine(src, nslot, x1, route, off, y_rows):
    N, D = x1.shape
    tm = min(TM_MOVE, N)
    return pl.pallas_call(
        functools.partial(_combine_kernel, tm=tm),
        grid_spec=pltpu.PrefetchScalarGridSpec(
            num_scalar_prefetch=2,
            grid=(N // tm,),
            in_specs=[
                pl.BlockSpec((tm, D), lambda i, s, n: (i, 0)),
                pl.BlockSpec((tm, LANES), lambda i, s, n: (i, 0)),
                pl.BlockSpec((1, 1, LANES), lambda i, s, n: (i, 0, 0)),
                pl.BlockSpec(memory_space=pl.ANY),
            ],
            out_specs=pl.BlockSpec((tm, D), lambda i, s, n: (i, 0)),
            scratch_shapes=[pltpu.VMEM((2, RUN_SLOTS * RUN_ROWS, D_PACK), U32), pltpu.SemaphoreType.DMA((2,))],
        ),
        out_shape=jax.ShapeDtypeStruct((N, D), F32),
        compiler_params=_params("arbitrary"),
    )(src, nslot, x1, route, off, y_rows)


def _moe(x1, h2, route, route_t, counts, sub_carry, w1, w3, w2, layer):
    N, _ = x1.shape
    tm = min(TM_MOVE, N)
    n_tiles = N // tm
    experts = jnp.arange(N_EXPERTS, dtype=jnp.int32)
    nb = (2 * N + N_EXPERTS * (TB_EXP - 1)) // TB_EXP + 1
    e_id = route_t[ROUTE_E1:ROUTE_E2 + 1].astype(jnp.int32)
    rank = route_t[ROUTE_R1:ROUTE_R2 + 1].astype(jnp.int32)
    cnt = counts[0, :N_EXPERTS].astype(jnp.int32)
    padded = (cnt + TB_EXP - 1) // TB_EXP * TB_EXP
    pad_end = jnp.cumsum(padded)
    pad_start = pad_end - padded
    e_flat = e_id.reshape(1, 2 * N)
    dest = jnp.sum(jnp.where(e_flat == experts[:, None], pad_start[:, None], 0), axis=0) + rank.reshape(2 * N)
    n_used = pad_end[-1] // TB_EXP
    blk_map = jnp.minimum(jnp.arange(nb, dtype=jnp.int32), jnp.maximum(n_used - 1, 0))
    blk_exp = jnp.sum(pad_end[None, :] <= (blk_map * TB_EXP)[:, None], axis=1).astype(jnp.int32)
    blk_exp = jnp.minimum(blk_exp, N_EXPERTS - 1)
    is_blk_e = blk_exp[:, None] == experts[None, :]
    blk_valid = jnp.clip(jnp.sum(jnp.where(is_blk_e, (pad_start + cnt)[None, :], 0), axis=1) - blk_map * TB_EXP,
                         0, TB_EXP).astype(jnp.int32)
    has_rows = cnt > 0
    slot_of_e = (jnp.cumsum(has_rows.astype(jnp.int32)) - 1) % 2
    later = jnp.where(has_rows, experts, N_EXPERTS)
    next_of_e = lax.cummin(jnp.concatenate([later[1:], jnp.full((1,), N_EXPERTS, jnp.int32)]), reverse=True)
    next_of_e = jnp.where(next_of_e < N_EXPERTS, next_of_e, -1)
    blk_slot = jnp.sum(jnp.where(is_blk_e, slot_of_e[None, :], 0), axis=1).astype(jnp.int32)
    blk_next = jnp.sum(jnp.where(is_blk_e, next_of_e[None, :], 0), axis=1).astype(jnp.int32)

    per_router_tile = TM_OUT // tm if N >= TM_OUT else 1
    carry = sub_carry.reshape(-1, SUBLANES, LANES)[:, :per_router_tile, :N_EXPERTS].reshape(n_tiles, N_EXPERTS)
    carry = carry.astype(jnp.int32)
    tile_cnt = jnp.concatenate([carry[1:], cnt[None, :]], axis=0) - carry
    run_start = pad_start[None, :] + carry
    first_blk = run_start // RUN_ROWS
    nslot = jnp.where(tile_cnt > 0, (run_start + tile_cnt - 1) // RUN_ROWS - first_blk + 1, 0)
    slot_end = jnp.cumsum(nslot, axis=1)
    slot_base = slot_end - nslot
    slots = jnp.arange(RUN_SLOTS, dtype=jnp.int32)
    slot_e = jnp.minimum(jnp.sum(slot_end[:, None, :] <= slots[None, :, None], axis=2), N_EXPERTS - 1)
    is_slot_e = slot_e[:, :, None] == experts[None, None, :]
    pick = lambda tbl: jnp.sum(jnp.where(is_slot_e, tbl[:, None, :], 0), axis=2)
    src = (pick(first_blk) + slots[None, :] - pick(slot_base)) * RUN_ROWS
    src = jnp.where(slots[None, :] < slot_end[:, -1:], src, 0)
    src = jnp.clip(src, 0, nb * TB_EXP - RUN_ROWS).reshape(-1).astype(jnp.int32)
    n_groups = (slot_end[:, -1] + RUN_GROUP - 1) // RUN_GROUP
    off = (slot_base - first_blk) * RUN_ROWS + pad_start[None, :]
    off = jnp.pad(off.astype(F32), ((0, 0), (0, LANES - N_EXPERTS)))

    x_rows = _dispatch(dest, h2, nb * TB_EXP)
    y_rows = _experts(blk_map, blk_exp, blk_valid, blk_slot, blk_next, n_used.reshape(1).astype(jnp.int32),
                      x_rows, w1, w3, w2, layer)
    return _combine(src, n_groups.astype(jnp.int32), x1, route, off.reshape(n_tiles, 1, LANES), y_rows)


def _pad_heads(w, width):
    k = w.shape[0]
    w = w.reshape(k, N_HEADS, width)
    return jnp.pad(w, ((0, 0), (0, 0), (0, HEAD_PAD - width))).reshape(k, N_HEADS * HEAD_PAD)


def _swap_rope(w):
    half = QK_ROPE // 2
    lo, hi = w[..., QK_NOPE:QK_NOPE + half], w[..., QK_NOPE + half:QK_HEAD]
    pad = [(0, 0)] * (w.ndim - 1)
    return jnp.pad(jnp.concatenate([hi, lo], -1), pad + [(QK_NOPE, HEAD_PAD - QK_HEAD)])


def _rope_tables(positions):
    inv_freq = ROPE_THETA ** (-jnp.arange(0, QK_ROPE, 2, dtype=F32) / QK_ROPE)
    ang = positions.astype(F32)[..., None] * inv_freq
    table = jnp.concatenate([jnp.cos(ang), jnp.sin(ang)], -1)
    return jnp.pad(table, ((0, 0), (0, 0), (QK_NOPE, LANES - QK_HEAD)))


def _ssm_params(lam_re, lam_im, b_re, b_im, c_re, c_im, log_dt):
    lam = lax.complex(lam_re, lam_im)
    dt = jnp.exp(log_dt)[:, None]
    lam_bar = jnp.exp(lam * dt)
    b_bar = ((lam_bar - 1.0) / lam)[..., None] * lax.complex(b_re, b_im)
    eye = jnp.eye(SSM_GROUPS, dtype=F32)

    def in_blockdiag(m):
        return jnp.einsum("gpc,gh->gchp", m, eye).reshape(SSM_WIDTH, SSM_FLAT)

    def out_blockdiag(m):
        return jnp.einsum("gcp,gh->gphc", m, eye).reshape(SSM_FLAT, SSM_WIDTH)

    bbd = jnp.concatenate([in_blockdiag(jnp.real(b_bar)), in_blockdiag(jnp.imag(b_bar))], axis=1)
    cbd = jnp.concatenate([out_blockdiag(c_re), out_blockdiag(-c_im)], axis=0)
    lam_rows = jnp.stack([jnp.real(lam_bar).reshape(SSM_FLAT), jnp.imag(lam_bar).reshape(SSM_FLAT)])
    return bbd.astype(BF16), lam_rows, cbd.astype(BF16)


def kernel(x, positions, mix_norm, w_in, q_a_norm, w_uq, kv_a_norm, w_ukv, q_norm, k_norm, ssm_lam_re, ssm_lam_im, ssm_b_re, ssm_b_im, ssm_c_re, ssm_c_im, ssm_d, ssm_log_dt, ssm_w_glu, conv_dw_w, conv_dw_b, conv_ln_w, conv_ln_b, out_norm, w_out, ffn_norm, w_grp, b_grp, w_exp, b_exp, w1, w3, w2):
    B, L, D = x.shape
    depth = w_in.shape[0]
    rope = _rope_tables(positions)
    row = lambda v: v.reshape(1, -1)
    lane_pad = lambda v: jnp.pad(v, (0, LANES - v.shape[0])).reshape(1, LANES)
    for l in range(depth):
        c_q, c_kv, k_pe, u_s, c_a, c_g = jnp.split(
            w_in[l], [Q_LORA, Q_LORA + KV_LORA, Q_LORA + KV_LORA + QK_ROPE,
                      Q_LORA + KV_LORA + QK_ROPE + SSM_WIDTH,
                      Q_LORA + KV_LORA + QK_ROPE + SSM_WIDTH + CONV_WIDTH], axis=1)
        k_pe_full = jnp.pad(k_pe, ((0, 0), (QK_NOPE, 0)))
        win = jnp.concatenate([c_q, c_kv, u_s, c_a, c_g, jnp.pad(k_pe_full, ((0, 0), (0, HEAD_PAD - QK_HEAD))),
                               _swap_rope(k_pe_full)], axis=1).astype(BF16)
        wkv = w_ukv[l].reshape(KV_LORA, N_HEADS, QK_NOPE + V_HEAD)
        wuk = _pad_heads(wkv[:, :, :QK_NOPE].reshape(KV_LORA, N_HEADS * QK_NOPE), QK_NOPE).astype(BF16)
        wuv = wkv[:, :, QK_NOPE:].reshape(KV_LORA, ATTN_WIDTH).astype(BF16)
        wuq_sw = _swap_rope(w_uq[l].reshape(Q_LORA, N_HEADS, QK_HEAD)).reshape(Q_LORA, N_HEADS * HEAD_PAD)
        wuq = jnp.concatenate([_pad_heads(w_uq[l], QK_HEAD), wuq_sw], axis=1).astype(BF16)
        qn_scaled = q_norm[l] * (QK_HEAD ** -0.5)
        norm_rows = lambda w: jnp.stack([jnp.pad(w, (0, HEAD_PAD - QK_HEAD)), _swap_rope(w)])
        q, k, v, u_tm, ca, cg = _in_proj(
            x, rope, row(mix_norm[l]), win, row(q_a_norm[l]), wuq, row(kv_a_norm[l]), wuk, wuv,
            norm_rows(qn_scaled), norm_rows(k_norm[l]))
        y_attn = _attention(q, k, v)

        g_out = out_norm[l]
        bbd, lam_rows, cbd = _ssm_params(ssm_lam_re[l], ssm_lam_im[l], ssm_b_re[l], ssm_b_im[l],
                                         ssm_c_re[l], ssm_c_im[l], ssm_log_dt[l])
        y_ssm_tm = _ssm(u_tm.reshape(L * B, SSM_WIDTH), bbd, lam_rows, cbd, row(ssm_d[l]),
                        ssm_w_glu[l].astype(BF16), row(g_out[ATTN_WIDTH:ATTN_WIDTH + SSM_WIDTH]), B)
        y_conv = _conv(ca, cg, conv_dw_w[l], row(conv_dw_b[l]), row(conv_ln_w[l]), row(conv_ln_b[l]),
                       row(g_out[ATTN_WIDTH + SSM_WIDTH:]))

        w_route = jnp.pad(jnp.concatenate([w_exp[l], w_grp[l]], axis=1), ((0, 0), (0, LANES - N_EXPERTS - N_EGROUPS)))
        w_route_hi = w_route.astype(BF16)
        w_route = jnp.concatenate([w_route_hi, (w_route - w_route_hi.astype(F32)).astype(BF16)], axis=1)
        b_route = lane_pad(jnp.concatenate([b_exp[l], b_grp[l]]))
        x1, h2, route, route_t, counts, sub_carry = _out_router(
            x.reshape(B * L, D), y_attn.reshape(B * L, ATTN_WIDTH), y_ssm_tm.reshape(L, B * SSM_WIDTH),
            y_conv.reshape(B * L, CONV_WIDTH), row(g_out[:ATTN_WIDTH]), w_out[l].astype(BF16),
            row(ffn_norm[l]), w_route, b_route, B)
        x = _moe(x1, h2, route, route_t, counts, sub_carry, w1, w3, w2, l).reshape(B, L, D)
    return x
```

```python
import functools
import math

import jax
import jax.numpy as jnp
from jax import lax
from jax.experimental import pallas as pl
from jax.experimental.pallas import tpu as pltpu
from jax.experimental.pallas import tpu_sc as plsc

D_MODEL = 1024
CHUNK = 64
EPS = 1e-6
N_HEADS = 8
QK_NOPE = 64
QK_ROPE = 32
QK_HEAD = QK_NOPE + QK_ROPE
V_HEAD = 64
Q_LORA = 256
KV_LORA = 128
ROPE_THETA = 10000.0
ATTN_WIDTH = N_HEADS * V_HEAD
SSM_WIDTH = 256
SSM_GROUP = 16
SSM_GROUPS = SSM_WIDTH // SSM_GROUP
SSM_STATE = 64
SSM_FLAT = SSM_GROUPS * SSM_STATE
CONV_WIDTH = 256
CONV_K = 31
N_EGROUPS = 4
EXP_PER_GROUP = 8
N_EXPERTS = N_EGROUPS * EXP_PER_GROUP
D_FF_E = 512

LANES = 128
SUBLANES = 8
HEAD_PAD = LANES
IN_PROJ_PAD = Q_LORA + KV_LORA + SSM_WIDTH + 2 * CONV_WIDTH + 2 * LANES
CONV_HALO = 32
VMEM_LIMIT = 48 * 1024 * 1024

TM_PROJ = 512
TQ_ATTN = 256
TC_SSM = 64
CONV_SUB = 64
TM_OUT = 512
TM_MOVE = 256
RUN_ROWS = SUBLANES
RUN_SLOTS = 2 * N_EXPERTS + 2 * TM_MOVE // RUN_ROWS
RUN_GROUP = 8
SC_CORES, SC_SUBCORES = 2, 16
SC_CHUNK = 64
TB_EXP = 512
FF_CHUNK = 256

BF16 = jnp.bfloat16
F32 = jnp.float32
U32 = jnp.uint32
D_PACK = D_MODEL // 2


def _pack_rows(v):
    bits = lax.bitcast_convert_type(v.astype(BF16).astype(F32), U32)
    half = v.shape[1] // 2
    return bits[:, 0:half] | (bits[:, half:] >> 16)


def _unpack_rows(w):
    hi = lax.bitcast_convert_type(w & jnp.uint32(0xFFFF0000), F32)
    lo = lax.bitcast_convert_type(w << 16, F32)
    return jnp.concatenate([hi, lo], axis=1).astype(BF16)


def _rms(x, w):
    return x * lax.rsqrt(jnp.mean(x * x, axis=-1, keepdims=True) + EPS) * w


def _params(*sem):
    return pltpu.CompilerParams(dimension_semantics=sem, vmem_limit_bytes=VMEM_LIMIT)


def _in_proj_kernel(x_ref, rope_ref, mixn_ref, win_ref, qan_ref, wuq_ref, kvan_ref, wuk_ref, wuv_ref,
                    qn_ref, kn_ref, q_ref, k_ref, v_ref, u_ref, ca_ref, cg_ref):
    x = x_ref[0]
    h = _rms(x, mixn_ref[...]).astype(BF16)
    proj = jnp.dot(h, win_ref[...], preferred_element_type=F32)
    o = 0
    c_q = proj[:, o:o + Q_LORA]; o += Q_LORA
    c_kv = proj[:, o:o + KV_LORA]; o += KV_LORA
    u_ref[...] = proj[:, o:o + SSM_WIDTH].astype(BF16); o += SSM_WIDTH
    ca_ref[0] = proj[:, o:o + CONV_WIDTH].astype(BF16); o += CONV_WIDTH
    cg_ref[0] = proj[:, o:o + CONV_WIDTH].astype(BF16); o += CONV_WIDTH
    k_pe = proj[:, o:o + LANES]; o += LANES
    k_pe_sw = proj[:, o:o + LANES]

    width = N_HEADS * HEAD_PAD
    q2 = jnp.dot(_rms(c_q, qan_ref[...]).astype(BF16), wuq_ref[...], preferred_element_type=F32)
    q, q_sw = q2[:, 0:width], q2[:, width:2 * width]
    ckv_n = _rms(c_kv, kvan_ref[...]).astype(BF16)
    kn = jnp.dot(ckv_n, wuk_ref[...], preferred_element_type=F32)
    v_ref[0] = jnp.dot(ckv_n, wuv_ref[...], preferred_element_type=F32).astype(BF16)

    rope = rope_ref[0]
    half = QK_ROPE // 2
    lane = lax.broadcasted_iota(jnp.int32, rope.shape, 1)
    lo = (lane >= QK_NOPE) & (lane < QK_NOPE + half)
    hi = (lane >= QK_NOPE + half) & (lane < QK_HEAD)
    cos_t = jnp.where(lane < QK_NOPE, 1.0, jnp.where(lo, rope, jnp.where(hi, pltpu.roll(rope, half, 1), 0.0)))
    sin_t = jnp.where(lo, -pltpu.roll(rope, LANES - half, 1), jnp.where(hi, rope, 0.0))
    a_q, b_q = qn_ref[0:1, :] * cos_t, qn_ref[1:2, :] * sin_t
    a_k = kn_ref[0:1, :] * cos_t
    k_sw_term = k_pe_sw * (kn_ref[1:2, :] * sin_t)

    def inv_rms(y):
        return lax.rsqrt(jnp.sum(y * y, axis=-1, keepdims=True) * (1.0 / QK_HEAD) + EPS)

    for hd in range(N_HEADS):
        sl = slice(hd * HEAD_PAD, (hd + 1) * HEAD_PAD)
        yq = q[:, sl]
        q_ref[0, hd] = ((yq * a_q + q_sw[:, sl] * b_q) * inv_rms(yq)).astype(BF16)
        yk = kn[:, sl] + k_pe
        k_ref[0, hd] = ((yk * a_k + k_sw_term) * inv_rms(yk)).astype(BF16)


def _in_proj(x, rope, mixn, win, qan, wuq, kvan, wuk, wuv, qn, kn):
    B, L, D = x.shape
    tm = min(TM_PROJ, L)
    full = lambda shape: pl.BlockSpec(shape, lambda b, t: (0,) * len(shape))
    return pl.pallas_call(
        _in_proj_kernel,
        grid=(B, L // tm),
        in_specs=[
            pl.BlockSpec((1, tm, D), lambda b, t: (b, t, 0)),
            pl.BlockSpec((1, tm, LANES), lambda b, t: (b, t, 0)),
            full((1, D)), full((D, IN_PROJ_PAD)), full((1, Q_LORA)), full((Q_LORA, 2 * N_HEADS * HEAD_PAD)),
            full((1, KV_LORA)), full((KV_LORA, N_HEADS * HEAD_PAD)), full((KV_LORA, ATTN_WIDTH)),
            full((2, HEAD_PAD)), full((2, HEAD_PAD)),
        ],
        out_specs=[
            pl.BlockSpec((1, N_HEADS, tm, HEAD_PAD), lambda b, t: (b, 0, t, 0)),
            pl.BlockSpec((1, N_HEADS, tm, HEAD_PAD), lambda b, t: (b, 0, t, 0)),
            pl.BlockSpec((1, tm, ATTN_WIDTH), lambda b, t: (b, t, 0)),
            pl.BlockSpec((tm, SSM_WIDTH), lambda b, t: (t, b)),
            pl.BlockSpec((1, tm, CONV_WIDTH), lambda b, t: (b, t, 0)),
            pl.BlockSpec((1, tm, CONV_WIDTH), lambda b, t: (b, t, 0)),
        ],
        out_shape=[
            jax.ShapeDtypeStruct((B, N_HEADS, L, HEAD_PAD), BF16),
            jax.ShapeDtypeStruct((B, N_HEADS, L, HEAD_PAD), BF16),
            jax.ShapeDtypeStruct((B, L, ATTN_WIDTH), BF16),
            jax.ShapeDtypeStruct((L, B * SSM_WIDTH), BF16),
            jax.ShapeDtypeStruct((B, L, CONV_WIDTH), BF16),
            jax.ShapeDtypeStruct((B, L, CONV_WIDTH), BF16),
        ],
        compiler_params=_params("parallel", "parallel"),
    )(x, rope, mixn, win, qan, wuq, kvan, wuk, wuv, qn, kn)


def _attention_kernel(q_ref, k_ref, v_ref, o_ref, vext_ref, *, seq, tq):
    pair = 2 * V_HEAD
    vext_ref[:, 0:pair] = v_ref[0]
    vext_ref[:, pair:] = jnp.ones((seq, LANES), BF16)
    row_chunk = lax.broadcasted_iota(jnp.int32, (tq, tq), 0) // CHUNK
    col_chunk = lax.broadcasted_iota(jnp.int32, (tq, tq), 1) // CHUNK
    visible = col_chunk <= row_chunk
    neg = jnp.finfo(F32).min
    nt = (((1,), (1,)), ((), ()))
    low_lanes = lax.broadcasted_iota(jnp.int32, (tq, pair), 1) < V_HEAD
    for i in range(seq // tq):
        q0 = i * tq
        outs = []
        for hh in range(2):
            qb = q_ref[0, hh, q0:q0 + tq, :]
            s_d = lax.dot_general(qb, k_ref[0, hh, q0:q0 + tq, :], nt, preferred_element_type=F32)
            s_d = jnp.where(visible, s_d, neg)
            m = jnp.max(s_d, axis=-1, keepdims=True)
            if i > 0:
                s_l = lax.dot_general(qb, k_ref[0, hh, 0:q0, :], nt, preferred_element_type=F32)
                m = jnp.maximum(m, jnp.max(s_l, axis=-1, keepdims=True))
            acc = jnp.dot(jnp.exp(s_d - m).astype(BF16), vext_ref[q0:q0 + tq, :], preferred_element_type=F32)
            if i > 0:
                acc = acc + jnp.dot(jnp.exp(s_l - m).astype(BF16), vext_ref[0:q0, :], preferred_element_type=F32)
            outs.append(acc[:, 0:pair] / acc[:, pair:])
        o_ref[0, q0:q0 + tq, :] = jnp.where(low_lanes, outs[0], outs[1]).astype(BF16)


def _attention(q, k, v):
    B, H, L, _ = q.shape
    tq = min(TQ_ATTN, L)
    return pl.pallas_call(
        functools.partial(_attention_kernel, seq=L, tq=tq),
        grid=(B, H // 2),
        in_specs=[
            pl.BlockSpec((1, 2, L, HEAD_PAD), lambda b, p: (b, p, 0, 0)),
            pl.BlockSpec((1, 2, L, HEAD_PAD), lambda b, p: (b, p, 0, 0)),
            pl.BlockSpec((1, L, 2 * V_HEAD), lambda b, p: (b, 0, p)),
        ],
        out_specs=pl.BlockSpec((1, L, 2 * V_HEAD), lambda b, p: (b, 0, p)),
        out_shape=jax.ShapeDtypeStruct((B, L, ATTN_WIDTH), BF16),
        scratch_shapes=[pltpu.VMEM((L, 2 * V_HEAD + LANES), BF16)],
        compiler_params=_params("parallel", "parallel"),
    )(q, k, v)


def _ssm_kernel(u_ref, bbd_ref, lam_ref, cbd_ref, d_ref, wglu_ref, g_ref, o_ref, state_ref, bu_a, bu_b, *, batch, tc):
    @pl.when(pl.program_id(0) == 0)
    def _():
        state_ref[...] = jnp.zeros_like(state_ref)

    half = tc // 2 * batch
    halves = ((bu_a, slice(0, half)), (bu_b, slice(half, 2 * half)))
    for bu_ref, rows in halves:
        bu_ref[...] = jnp.dot(u_ref[rows, :], bbd_ref[...], preferred_element_type=F32)
    lam_re = jnp.broadcast_to(lam_ref[0:1, :], (batch, SSM_FLAT))
    lam_im = jnp.broadcast_to(lam_ref[1:2, :], (batch, SSM_FLAT))

    carry = (state_ref[:, 0:SSM_FLAT], state_ref[:, SSM_FLAT:2 * SSM_FLAT])
    for bu_ref, _ in halves:
        def step(t, xs, bu_ref=bu_ref):
            xr, xi = xs
            at = pl.ds(pl.multiple_of(t * batch, batch), batch)
            nr = lam_re * xr - lam_im * xi + bu_ref[at, 0:SSM_FLAT]
            ni = lam_re * xi + lam_im * xr + bu_ref[at, SSM_FLAT:2 * SSM_FLAT]
            bu_ref[at, 0:SSM_FLAT] = nr
            bu_ref[at, SSM_FLAT:2 * SSM_FLAT] = ni
            return nr, ni

        carry = lax.fori_loop(0, tc // 2, step, carry, unroll=True)
    state_ref[:, 0:SSM_FLAT] = carry[0]
    state_ref[:, SSM_FLAT:2 * SSM_FLAT] = carry[1]

    for bu_ref, rows in halves:
        y = jnp.dot(bu_ref[...].astype(BF16), cbd_ref[...], preferred_element_type=F32)
        y = y + d_ref[...] * u_ref[rows, :].astype(F32)
        z = jax.nn.gelu(y)
        gate = jax.nn.sigmoid(jnp.dot(z.astype(BF16), wglu_ref[...], preferred_element_type=F32))
        o_ref[rows, :] = _rms(z * gate, g_ref[...]).astype(BF16)


def _ssm(u_tm, bbd, lam, cbd, d, wglu, g, batch):
    rows = u_tm.shape[0]
    seq = rows // batch
    tc = min(TC_SSM, seq)
    blk = tc * batch
    full = lambda shape: pl.BlockSpec(shape, lambda t: (0,) * len(shape))
    return pl.pallas_call(
        functools.partial(_ssm_kernel, batch=batch, tc=tc),
        grid=(seq // tc,),
        in_specs=[
            pl.BlockSpec((blk, SSM_WIDTH), lambda t: (t, 0)),
            full((SSM_WIDTH, 2 * SSM_FLAT)), full((2, SSM_FLAT)), full((2 * SSM_FLAT, SSM_WIDTH)),
            full((1, SSM_WIDTH)), full((SSM_WIDTH, SSM_WIDTH)), full((1, SSM_WIDTH)),
        ],
        out_specs=pl.BlockSpec((blk, SSM_WIDTH), lambda t: (t, 0)),
        out_shape=jax.ShapeDtypeStruct((rows, SSM_WIDTH), BF16),
        scratch_shapes=[pltpu.VMEM((batch, 2 * SSM_FLAT), F32), pltpu.VMEM((blk // 2, 2 * SSM_FLAT), F32),
                        pltpu.VMEM((blk // 2, 2 * SSM_FLAT), F32)],
        compiler_params=_params("arbitrary"),
    )(u_tm, bbd, lam, cbd, d, wglu, g)


def _conv_rows(upad_ref, r0, w_ref, b_ref, lnw_ref, lnb_ref, gn_ref):
    base = CONV_HALO - (CONV_K - 1)
    acc = jnp.zeros((CONV_SUB, CONV_WIDTH), F32)
    for off in range(SUBLANES):
        n = CONV_SUB + (SUBLANES if off else 0)
        part = None
        for a8 in range(0, CONV_HALO + 1, SUBLANES):
            kk = a8 + off - base
            if 0 <= kk < CONV_K:
                term = w_ref[kk:kk + 1, :] * upad_ref[r0 + a8:r0 + a8 + n, :]
                part = term if part is None else part + term
        acc = acc + part[off:off + CONV_SUB]
    y = acc + b_ref[...]
    mu = jnp.mean(y, axis=-1, keepdims=True)
    var = jnp.mean(jnp.square(y - mu), axis=-1, keepdims=True)
    y = (y - mu) * lax.rsqrt(var + 1e-5) * lnw_ref[...] + lnb_ref[...]
    y = y * jax.nn.sigmoid(y)
    return _rms(y, gn_ref[...])


ROUTE_E1, ROUTE_E2, ROUTE_R1, ROUTE_R2, ROUTE_G1, ROUTE_G2 = range(6)
ROUTE_ROWS = 8
GRP_LANE0 = N_EXPERTS


def _out_router_kernel(x_ref, ya_ref, ys_ref, ca_ref, cg_ref, ga_ref, wo_ref, fn_ref, wr_ref, br_ref,
                       cw_ref, cb_ref, clnw_ref, clnb_ref, cgn_ref,
                       x1_ref, h2_ref, route_ref, route_t_ref, cnt_ref, sub_ref,
                       carry_ref, tri_ref, upad_ref, yc_ref, *, tm, per_b):
    @pl.when(pl.program_id(0) == 0)
    def _():
        carry_ref[...] = jnp.zeros_like(carry_ref)
        rr = lax.broadcasted_iota(jnp.int32, (tm, tm), 0)
        cc = lax.broadcasted_iota(jnp.int32, (tm, tm), 1)
        tri_ref[...] = (cc < rr).astype(BF16)

    first = pl.program_id(0) % per_b == 0
    upad_ref[0:CONV_HALO, :] = jnp.where(first, 0.0, upad_ref[tm:tm + CONV_HALO, :])
    upad_ref[CONV_HALO:, :] = ca_ref[...].astype(F32) * jax.nn.sigmoid(cg_ref[...].astype(F32))
    for r0 in range(0, tm, CONV_SUB):
        yc_ref[r0:r0 + CONV_SUB, :] = _conv_rows(upad_ref, r0, cw_ref, cb_ref, clnw_ref, clnb_ref, cgn_ref).astype(BF16)

    ya = _rms(ya_ref[...].astype(F32), ga_ref[...]).astype(BF16)
    acc = jnp.dot(ya, wo_ref[0:ATTN_WIDTH, :], preferred_element_type=F32)
    acc += jnp.dot(ys_ref[...], wo_ref[ATTN_WIDTH:ATTN_WIDTH + SSM_WIDTH, :], preferred_element_type=F32)
    acc += jnp.dot(yc_ref[...], wo_ref[ATTN_WIDTH + SSM_WIDTH:, :], preferred_element_type=F32)
    x1 = x_ref[...] + acc
    x1_ref[...] = x1
    h2 = _rms(x1, fn_ref[...])
    h2_ref[...] = _pack_rows(h2)

    h_hi = h2.astype(BF16)
    h_lo = (h2 - h_hi.astype(F32)).astype(BF16)
    part = jnp.dot(h_hi, wr_ref[...], preferred_element_type=F32)
    logits = (part[:, 0:LANES] + part[:, LANES:2 * LANES]
              + jnp.dot(h_lo, wr_ref[:, 0:LANES], preferred_element_type=F32) + br_ref[...])
    lane = lax.broadcasted_iota(jnp.int32, (tm, LANES), 1)
    ninf = -jnp.inf
    big = LANES

    def first_argmax(vals, vmax):
        return jnp.min(jnp.where(vals == vmax, lane, big), axis=-1, keepdims=True)

    grp = jnp.where((lane >= GRP_LANE0) & (lane < GRP_LANE0 + N_EGROUPS), logits, ninf)
    gmax = jnp.max(grp, axis=-1, keepdims=True)
    gsel = first_argmax(grp, gmax) - GRP_LANE0
    p_grp = 1.0 / jnp.sum(jnp.exp(grp - gmax), axis=-1, keepdims=True)

    el = jnp.where((lane < N_EXPERTS) & ((lane // EXP_PER_GROUP) == gsel), logits, ninf)
    m1 = jnp.max(el, axis=-1, keepdims=True)
    e1 = first_argmax(el, m1)
    el2 = jnp.where(lane == e1, ninf, el)
    m2 = jnp.max(el2, axis=-1, keepdims=True)
    e2 = first_argmax(el2, m2)
    t2 = jnp.exp(m2 - m1)
    g1 = p_grp / (1.0 + t2)
    g2 = p_grp * t2 / (1.0 + t2)

    hit1 = lane == e1
    hit2 = lane == e2
    cnt = (hit1 | hit2).astype(F32)
    before = jnp.dot(tri_ref[...], cnt.astype(BF16), preferred_element_type=F32) + carry_ref[...]
    r1 = jnp.sum(jnp.where(hit1, before, 0.0), axis=-1, keepdims=True)
    r2 = jnp.sum(jnp.where(hit2, before, 0.0), axis=-1, keepdims=True)
    subs = [before[h * TM_MOVE:h * TM_MOVE + 1, :] for h in range(tm // TM_MOVE)]
    sub_ref[...] = jnp.concatenate(subs + [jnp.zeros((SUBLANES - len(subs), LANES), F32)], axis=0)
    carry_ref[...] += jnp.sum(cnt, axis=0, keepdims=True)
    cnt_ref[...] = carry_ref[...]

    rec = jnp.where(lane == ROUTE_E1, e1.astype(F32), 0.0)
    rec = jnp.where(lane == ROUTE_E2, e2.astype(F32), rec)
    rec = jnp.where(lane == ROUTE_R1, r1, rec)
    rec = jnp.where(lane == ROUTE_R2, r2, rec)
    rec = jnp.where(lane == ROUTE_G1, g1, rec)
    rec = jnp.where(lane == ROUTE_G2, g2, rec)
    route_ref[...] = rec
    route_t_ref[...] = rec.T[0:ROUTE_ROWS, :]


def _out_router(x, ya, ys_tm, ca, cg, ga, wo, fn, wr, br, conv_params, batch):
    N, D = x.shape
    seq = N // batch
    tm = min(TM_OUT, seq)
    per_b = seq // tm
    full = lambda shape: pl.BlockSpec(shape, lambda i: (0,) * len(shape))
    tile = lambda w: pl.BlockSpec((tm, w), lambda i: (i, 0))
    return pl.pallas_call(
        functools.partial(_out_router_kernel, tm=tm, per_b=per_b),
        grid=(N // tm,),
        in_specs=[
            tile(D), tile(ATTN_WIDTH),
            pl.BlockSpec((tm, SSM_WIDTH), lambda i: (i % per_b, i // per_b)),
            tile(CONV_WIDTH), tile(CONV_WIDTH),
            full((1, ATTN_WIDTH)), full((D, D)), full((1, D)), full((D, 2 * LANES)), full((1, LANES)),
            full((CONV_K, CONV_WIDTH)), full((1, CONV_WIDTH)), full((1, CONV_WIDTH)), full((1, CONV_WIDTH)),
            full((1, CONV_WIDTH)),
        ],
        out_specs=[tile(D), tile(D_PACK), tile(LANES), pl.BlockSpec((ROUTE_ROWS, tm), lambda i: (0, i)),
                   full((1, LANES)), pl.BlockSpec((SUBLANES, LANES), lambda i: (i, 0))],
        out_shape=[
            jax.ShapeDtypeStruct((N, D), F32),
            jax.ShapeDtypeStruct((N, D_PACK), U32),
            jax.ShapeDtypeStruct((N, LANES), F32),
            jax.ShapeDtypeStruct((ROUTE_ROWS, N), F32),
            jax.ShapeDtypeStruct((1, LANES), F32),
            jax.ShapeDtypeStruct((N // tm * SUBLANES, LANES), F32),
        ],
        scratch_shapes=[pltpu.VMEM((1, LANES), F32), pltpu.VMEM((tm, tm), BF16),
                        pltpu.VMEM((tm + CONV_HALO, CONV_WIDTH), F32), pltpu.VMEM((tm, CONV_WIDTH), BF16)],
        compiler_params=_params("arbitrary"),
    )(x, ya, ys_tm, ca, cg, ga, wo, fn, wr, br, *conv_params)


def _dispatch(dest_flat, h2, n_rows):
    N, D = h2.shape
    workers = SC_CORES * SC_SUBCORES
    per_worker = 2 * N // workers
    chunks = per_worker // SC_CHUNK
    idx = dest_flat.reshape(workers, chunks, SC_CHUNK)
    mesh = plsc.VectorSubcoreMesh(core_axis_name="c", subcore_axis_name="s")

    @functools.partial(
        pl.kernel, mesh=mesh,
        out_type=jax.ShapeDtypeStruct((n_rows, D), h2.dtype),
        scratch_types=[pltpu.VMEM((chunks, SC_CHUNK), jnp.int32), pltpu.VMEM((2, SC_CHUNK, D), h2.dtype),
                       pltpu.SemaphoreType.DMA((2,)), pltpu.SemaphoreType.DMA((2,))],
    )
    def scatter(h_hbm, idx_hbm, out_hbm, idx_v, rows_v, sem_in, sem_out):
        wid = lax.axis_index("s") * SC_CORES + lax.axis_index("c")
        pltpu.sync_copy(idx_hbm.at[wid], idx_v)

        def load(c):
            first = lax.rem(wid * per_worker + c * SC_CHUNK, N)
            src = h_hbm.at[pl.ds(pl.multiple_of(first, SC_CHUNK), SC_CHUNK)]
            return pltpu.async_copy(src, rows_v.at[c % 2], sem_in.at[c % 2])

        loads = {0: load(0)}
        stores = {}
        for c in range(chunks):
            loads[c].wait()
            if c >= 1:
                stores[c - 1].wait()
            if c + 1 < chunks:
                loads[c + 1] = load(c + 1)
            stores[c] = pltpu.async_copy(rows_v.at[c % 2], out_hbm.at[idx_v.at[c]], sem_out.at[c % 2])
        stores[chunks - 1].wait()

    return scatter(h2, idx)


def _experts_kernel(blk_ref, exp_ref, valid_ref, slot_ref, next_ref, used_ref, x_ref, w1_hbm, w3_hbm, w2_hbm, y_ref,
                    w1_f, w3_f, w2_f, w13_s, w2_s, sem, *, layer):
    j = pl.program_id(0)

    def weight_copies(e, slot):
        return (pltpu.make_async_copy(w1_hbm.at[layer, e], w1_f.at[slot], sem.at[slot, 0]),
                pltpu.make_async_copy(w3_hbm.at[layer, e], w3_f.at[slot], sem.at[slot, 1]),
                pltpu.make_async_copy(w2_hbm.at[layer, e], w2_f.at[slot], sem.at[slot, 2]))

    @pl.when(j < used_ref[0])
    def _():
        e = exp_ref[j]
        slot = slot_ref[j]

        @pl.when(j == 0)
        def _():
            for cp in weight_copies(e, slot):
                cp.start()

        @pl.when((j == 0) | (e != exp_ref[jnp.maximum(j - 1, 0)]))
        def _():
            for cp in weight_copies(e, slot):
                cp.wait()
            for c in range(D_FF_E // FF_CHUNK):
                w13_s[:, 2 * FF_CHUNK * c:2 * FF_CHUNK * c + FF_CHUNK] = \
                    w1_f[slot, :, FF_CHUNK * c:FF_CHUNK * (c + 1)].astype(BF16)
                w13_s[:, 2 * FF_CHUNK * c + FF_CHUNK:2 * FF_CHUNK * (c + 1)] = \
                    w3_f[slot, :, FF_CHUNK * c:FF_CHUNK * (c + 1)].astype(BF16)
            w2_s[...] = w2_f[slot].astype(BF16)

            @pl.when(next_ref[j] >= 0)
            def _():
                for cp in weight_copies(next_ref[j], 1 - slot):
                    cp.start()

        x = _unpack_rows(x_ref[...])
        y = None
        for c in range(D_FF_E // FF_CHUNK):
            ab = jnp.dot(x, w13_s[:, 2 * FF_CHUNK * c:2 * FF_CHUNK * (c + 1)], preferred_element_type=F32)
            a = ab[:, 0:FF_CHUNK]
            hmid = (a * jax.nn.sigmoid(a) * ab[:, FF_CHUNK:]).astype(BF16)
            part = jnp.dot(hmid, w2_s[FF_CHUNK * c:FF_CHUNK * (c + 1), :], preferred_element_type=F32)
            y = part if y is None else y + part
        row = lax.broadcasted_iota(jnp.int32, (TB_EXP, 1), 0)
        y_ref[...] = jnp.where(row < valid_ref[j], _pack_rows(y), jnp.uint32(0))


def _experts(blk_map, blk_exp, blk_valid, blk_slot, blk_next, n_used, x_rows, w1, w3, w2, layer):
    R, D = x_rows.shape[0], D_MODEL
    nb = R // TB_EXP
    row_block = lambda j, bm, be, bv, bs, bn, nu: (bm[j], 0)
    return pl.pallas_call(
        functools.partial(_experts_kernel, layer=layer),
        grid_spec=pltpu.PrefetchScalarGridSpec(
            num_scalar_prefetch=6,
            grid=(nb,),
            in_specs=[
                pl.BlockSpec((TB_EXP, D_PACK), row_block),
                pl.BlockSpec(memory_space=pl.ANY), pl.BlockSpec(memory_space=pl.ANY), pl.BlockSpec(memory_space=pl.ANY),
            ],
            out_specs=pl.BlockSpec((TB_EXP, D_PACK), row_block),
            scratch_shapes=[
                pltpu.VMEM((2, D, D_FF_E), F32), pltpu.VMEM((2, D, D_FF_E), F32), pltpu.VMEM((2, D_FF_E, D), F32),
                pltpu.VMEM((D, 2 * D_FF_E), BF16), pltpu.VMEM((D_FF_E, D), BF16),
                pltpu.SemaphoreType.DMA((2, 3)),
            ],
        ),
        out_shape=jax.ShapeDtypeStruct((R, D_PACK), U32),
        compiler_params=_params("arbitrary"),
    )(blk_map, blk_exp, blk_valid, blk_slot, blk_next, n_used, x_rows, w1, w3, w2)


def _combine_kernel(src_ref, nslot_ref, x1_ref, route_ref, off_ref, rows_ref, o_ref, buf, sem, *, tm):
    i = pl.program_id(0)
    cur = i % 2

    def slot_copy(tile, half, s):
        src = pl.multiple_of(src_ref[tile * RUN_SLOTS + s], RUN_ROWS)
        dst = buf.at[half, pl.ds(pl.multiple_of(s * RUN_ROWS, RUN_ROWS), RUN_ROWS)]
        return pltpu.make_async_copy(rows_ref.at[pl.ds(src, RUN_ROWS)], dst, sem.at[half])

    def fetch(tile, half):
        def issue(g, c):
            for u in range(RUN_GROUP):
                slot_copy(tile, half, g * RUN_GROUP + u).start()
            return c
        lax.fori_loop(0, nslot_ref[tile], issue, 0)

    def wait(g, c):
        rows = RUN_GROUP * RUN_ROWS
        pltpu.make_async_copy(rows_ref.at[pl.ds(0, rows)], buf.at[cur, pl.ds(0, rows)], sem.at[cur]).wait()
        return c

    @pl.when(i == 0)
    def _():
        buf[...] = jnp.zeros_like(buf)
        fetch(0, 0)

    @pl.when(i + 1 < pl.num_programs(0))
    def _():
        fetch(i + 1, 1 - cur)

    route = route_ref[...]
    lane = lax.broadcasted_iota(jnp.int32, (tm, LANES), 1).astype(F32)
    off = off_ref[0]

    def buf_row(e_lane, r_lane):
        e = route[:, e_lane:e_lane + 1]
        base = jnp.sum(jnp.where(lane == e, off, 0.0), axis=-1, keepdims=True)
        return (base + route[:, r_lane:r_lane + 1]).astype(jnp.int32)

    col = lax.broadcasted_iota(jnp.int32, (tm, RUN_SLOTS * RUN_ROWS), 1)
    pick = (jnp.where(col == buf_row(ROUTE_E1, ROUTE_R1), route[:, ROUTE_G1:ROUTE_G1 + 1], 0.0)
            + jnp.where(col == buf_row(ROUTE_E2, ROUTE_R2), route[:, ROUTE_G2:ROUTE_G2 + 1], 0.0))
    lax.fori_loop(0, nslot_ref[i], wait, 0)
    o_ref[...] = x1_ref[...] + jnp.dot(pick.astype(BF16), _unpack_rows(buf[cur]), preferred_element_type=F32)


def _combine(src, nslot, x1, route, off, y_rows):
    N, D = x1.shape
    tm = min(TM_MOVE, N)
    return pl.pallas_call(
        functools.partial(_combine_kernel, tm=tm),
        grid_spec=pltpu.PrefetchScalarGridSpec(
            num_scalar_prefetch=2,
            grid=(N // tm,),
            in_specs=[
                pl.BlockSpec((tm, D), lambda i, s, n: (i, 0)),
                pl.BlockSpec((tm, LANES), lambda i, s, n: (i, 0)),
                pl.BlockSpec((1, 1, LANES), lambda i, s, n: (i, 0, 0)),
                pl.BlockSpec(memory_space=pl.ANY),
            ],
            out_specs=pl.BlockSpec((tm, D), lambda i, s, n: (i, 0)),
            scratch_shapes=[pltpu.VMEM((2, RUN_SLOTS * RUN_ROWS, D_PACK), U32), pltpu.SemaphoreType.DMA((2,))],
        ),
        out_shape=jax.ShapeDtypeStruct((N, D), F32),
        compiler_params=_params("arbitrary"),
    )(src, nslot, x1, route, off, y_rows)


def _moe(x1, h2, route, route_t, counts, sub_carry, w1, w3, w2, layer):
    N, _ = x1.shape
    tm = min(TM_MOVE, N)
    n_tiles = N // tm
    experts = jnp.arange(N_EXPERTS, dtype=jnp.int32)
    nb = (2 * N + N_EXPERTS * (TB_EXP - 1)) // TB_EXP + 1
    e_id = route_t[ROUTE_E1:ROUTE_E2 + 1].astype(jnp.int32)
    rank = route_t[ROUTE_R1:ROUTE_R2 + 1].astype(jnp.int32)
    cnt = counts[0, :N_EXPERTS].astype(jnp.int32)
    padded = (cnt + TB_EXP - 1) // TB_EXP * TB_EXP
    pad_end = jnp.cumsum(padded)
    pad_start = pad_end - padded
    e_flat = e_id.reshape(1, 2 * N)
    dest = jnp.sum(jnp.where(e_flat == experts[:, None], pad_start[:, None], 0), axis=0) + rank.reshape(2 * N)
    n_used = pad_end[-1] // TB_EXP
    blk_map = jnp.minimum(jnp.arange(nb, dtype=jnp.int32), jnp.maximum(n_used - 1, 0))
    blk_exp = jnp.sum(pad_end[None, :] <= (blk_map * TB_EXP)[:, None], axis=1).astype(jnp.int32)
    blk_exp = jnp.minimum(blk_exp, N_EXPERTS - 1)
    is_blk_e = blk_exp[:, None] == experts[None, :]
    blk_valid = jnp.clip(jnp.sum(jnp.where(is_blk_e, (pad_start + cnt)[None, :], 0), axis=1) - blk_map * TB_EXP,
                         0, TB_EXP).astype(jnp.int32)
    has_rows = cnt > 0
    slot_of_e = (jnp.cumsum(has_rows.astype(jnp.int32)) - 1) % 2
    later = jnp.where(has_rows, experts, N_EXPERTS)
    next_of_e = lax.cummin(jnp.concatenate([later[1:], jnp.full((1,), N_EXPERTS, jnp.int32)]), reverse=True)
    next_of_e = jnp.where(next_of_e < N_EXPERTS, next_of_e, -1)
    blk_slot = jnp.sum(jnp.where(is_blk_e, slot_of_e[None, :], 0), axis=1).astype(jnp.int32)
    blk_next = jnp.sum(jnp.where(is_blk_e, next_of_e[None, :], 0), axis=1).astype(jnp.int32)

    per_router_tile = TM_OUT // tm if N >= TM_OUT else 1
    carry = sub_carry.reshape(-1, SUBLANES, LANES)[:, :per_router_tile, :N_EXPERTS].reshape(n_tiles, N_EXPERTS)
    carry = carry.astype(jnp.int32)
    tile_cnt = jnp.concatenate([carry[1:], cnt[None, :]], axis=0) - carry
    run_start = pad_start[None, :] + carry
    first_blk = run_start // RUN_ROWS
    nslot = jnp.where(tile_cnt > 0, (run_start + tile_cnt - 1) // RUN_ROWS - first_blk + 1, 0)
    slot_end = jnp.cumsum(nslot, axis=1)
    slot_base = slot_end - nslot
    slots = jnp.arange(RUN_SLOTS, dtype=jnp.int32)
    slot_e = jnp.minimum(jnp.sum(slot_end[:, None, :] <= slots[None, :, None], axis=2), N_EXPERTS - 1)
    is_slot_e = slot_e[:, :, None] == experts[None, None, :]
    pick = lambda tbl: jnp.sum(jnp.where(is_slot_e, tbl[:, None, :], 0), axis=2)
    src = (pick(first_blk) + slots[None, :] - pick(slot_base)) * RUN_ROWS
    src = jnp.where(slots[None, :] < slot_end[:, -1:], src, 0)
    src = jnp.clip(src, 0, nb * TB_EXP - RUN_ROWS).reshape(-1).astype(jnp.int32)
    n_groups = (slot_end[:, -1] + RUN_GROUP - 1) // RUN_GROUP
    off = (slot_base - first_blk) * RUN_ROWS + pad_start[None, :]
    off = jnp.pad(off.astype(F32), ((0, 0), (0, LANES - N_EXPERTS)))

    x_rows = _dispatch(dest, h2, nb * TB_EXP)
    y_rows = _experts(blk_map, blk_exp, blk_valid, blk_slot, blk_next, n_used.reshape(1).astype(jnp.int32),
                      x_rows, w1, w3, w2, layer)
    return _combine(src, n_groups.astype(jnp.int32), x1, route, off.reshape(n_tiles, 1, LANES), y_rows)


def _pad_heads(w, width):
    k = w.shape[0]
    w = w.reshape(k, N_HEADS, width)
    return jnp.pad(w, ((0, 0), (0, 0), (0, HEAD_PAD - width))).reshape(k, N_HEADS * HEAD_PAD)


def _swap_rope(w):
    half = QK_ROPE // 2
    lo, hi = w[..., QK_NOPE:QK_NOPE + half], w[..., QK_NOPE + half:QK_HEAD]
    pad = [(0, 0)] * (w.ndim - 1)
    return jnp.pad(jnp.concatenate([hi, lo], -1), pad + [(QK_NOPE, HEAD_PAD - QK_HEAD)])


def _rope_tables(positions):
    inv_freq = ROPE_THETA ** (-jnp.arange(0, QK_ROPE, 2, dtype=F32) / QK_ROPE)
    ang = positions.astype(F32)[..., None] * inv_freq
    table = jnp.concatenate([jnp.cos(ang), jnp.sin(ang)], -1)
    return jnp.pad(table, ((0, 0), (0, 0), (QK_NOPE, LANES - QK_HEAD)))


def _ssm_params(lam_re, lam_im, b_re, b_im, c_re, c_im, log_dt):
    lam = lax.complex(lam_re, lam_im)
    dt = jnp.exp(log_dt)[:, None]
    lam_bar = jnp.exp(lam * dt)
    b_bar = ((lam_bar - 1.0) / lam)[..., None] * lax.complex(b_re, b_im)
    eye = jnp.eye(SSM_GROUPS, dtype=F32)

    def in_blockdiag(m):
        return jnp.einsum("gpc,gh->gchp", m, eye).reshape(SSM_WIDTH, SSM_FLAT)

    def out_blockdiag(m):
        return jnp.einsum("gcp,gh->gphc", m, eye).reshape(SSM_FLAT, SSM_WIDTH)

    bbd = jnp.concatenate([in_blockdiag(jnp.real(b_bar)), in_blockdiag(jnp.imag(b_bar))], axis=1)
    cbd = jnp.concatenate([out_blockdiag(c_re), out_blockdiag(-c_im)], axis=0)
    lam_rows = jnp.stack([jnp.real(lam_bar).reshape(SSM_FLAT), jnp.imag(lam_bar).reshape(SSM_FLAT)])
    return bbd.astype(BF16), lam_rows, cbd.astype(BF16)


def kernel(x, positions, mix_norm, w_in, q_a_norm, w_uq, kv_a_norm, w_ukv, q_norm, k_norm, ssm_lam_re, ssm_lam_im, ssm_b_re, ssm_b_im, ssm_c_re, ssm_c_im, ssm_d, ssm_log_dt, ssm_w_glu, conv_dw_w, conv_dw_b, conv_ln_w, conv_ln_b, out_norm, w_out, ffn_norm, w_grp, b_grp, w_exp, b_exp, w1, w3, w2):
    B, L, D = x.shape
    depth = w_in.shape[0]
    rope = _rope_tables(positions)
    row = lambda v: v.reshape(1, -1)
    lane_pad = lambda v: jnp.pad(v, (0, LANES - v.shape[0])).reshape(1, LANES)
    for l in range(depth):
        c_q, c_kv, k_pe, u_s, c_a, c_g = jnp.split(
            w_in[l], [Q_LORA, Q_LORA + KV_LORA, Q_LORA + KV_LORA + QK_ROPE,
                      Q_LORA + KV_LORA + QK_ROPE + SSM_WIDTH,
                      Q_LORA + KV_LORA + QK_ROPE + SSM_WIDTH + CONV_WIDTH], axis=1)
        k_pe_full = jnp.pad(k_pe, ((0, 0), (QK_NOPE, 0)))
        win = jnp.concatenate([c_q, c_kv, u_s, c_a, c_g, jnp.pad(k_pe_full, ((0, 0), (0, HEAD_PAD - QK_HEAD))),
                               _swap_rope(k_pe_full)], axis=1).astype(BF16)
        wkv = w_ukv[l].reshape(KV_LORA, N_HEADS, QK_NOPE + V_HEAD)
        wuk = _pad_heads(wkv[:, :, :QK_NOPE].reshape(KV_LORA, N_HEADS * QK_NOPE), QK_NOPE).astype(BF16)
        wuv = wkv[:, :, QK_NOPE:].reshape(KV_LORA, ATTN_WIDTH).astype(BF16)
        wuq_sw = _swap_rope(w_uq[l].reshape(Q_LORA, N_HEADS, QK_HEAD)).reshape(Q_LORA, N_HEADS * HEAD_PAD)
        wuq = jnp.concatenate([_pad_heads(w_uq[l], QK_HEAD), wuq_sw], axis=1).astype(BF16)
        qn_scaled = q_norm[l] * (QK_HEAD ** -0.5)
        norm_rows = lambda w: jnp.stack([jnp.pad(w, (0, HEAD_PAD - QK_HEAD)), _swap_rope(w)])
        q, k, v, u_tm, ca, cg = _in_proj(
            x, rope, row(mix_norm[l]), win, row(q_a_norm[l]), wuq, row(kv_a_norm[l]), wuk, wuv,
            norm_rows(qn_scaled), norm_rows(k_norm[l]))
        y_attn = _attention(q, k, v)

        g_out = out_norm[l]
        bbd, lam_rows, cbd = _ssm_params(ssm_lam_re[l], ssm_lam_im[l], ssm_b_re[l], ssm_b_im[l],
                                         ssm_c_re[l], ssm_c_im[l], ssm_log_dt[l])
        y_ssm_tm = _ssm(u_tm.reshape(L * B, SSM_WIDTH), bbd, lam_rows, cbd, row(ssm_d[l]),
                        ssm_w_glu[l].astype(BF16), row(g_out[ATTN_WIDTH:ATTN_WIDTH + SSM_WIDTH]), B)
        conv_params = (conv_dw_w[l], row(conv_dw_b[l]), row(conv_ln_w[l]), row(conv_ln_b[l]),
                       row(g_out[ATTN_WIDTH + SSM_WIDTH:]))

        w_route = jnp.pad(jnp.concatenate([w_exp[l], w_grp[l]], axis=1), ((0, 0), (0, LANES - N_EXPERTS - N_EGROUPS)))
        w_route_hi = w_route.astype(BF16)
        w_route = jnp.concatenate([w_route_hi, (w_route - w_route_hi.astype(F32)).astype(BF16)], axis=1)
        b_route = lane_pad(jnp.concatenate([b_exp[l], b_grp[l]]))
        x1, h2, route, route_t, counts, sub_carry = _out_router(
            x.reshape(B * L, D), y_attn.reshape(B * L, ATTN_WIDTH), y_ssm_tm.reshape(L, B * SSM_WIDTH),
            ca.reshape(B * L, CONV_WIDTH), cg.reshape(B * L, CONV_WIDTH), row(g_out[:ATTN_WIDTH]),
            w_out[l].astype(BF16), row(ffn_norm[l]), w_route, b_route, conv_params, B)
        x = _moe(x1, h2, route, route_t, counts, sub_carry, w1, w3, w2, l).reshape(B, L, D)
    return x
```

```python
import functools
import math

import jax
import jax.numpy as jnp
from jax import lax
from jax.experimental import pallas as pl
from jax.experimental.pallas import tpu as pltpu
from jax.experimental.pallas import tpu_sc as plsc

D_MODEL = 1024
CHUNK = 64
EPS = 1e-6
N_HEADS = 8
QK_NOPE = 64
QK_ROPE = 32
QK_HEAD = QK_NOPE + QK_ROPE
V_HEAD = 64
Q_LORA = 256
KV_LORA = 128
ROPE_THETA = 10000.0
ATTN_WIDTH = N_HEADS * V_HEAD
SSM_WIDTH = 256
SSM_GROUP = 16
SSM_GROUPS = SSM_WIDTH // SSM_GROUP
SSM_STATE = 64
SSM_FLAT = SSM_GROUPS * SSM_STATE
CONV_WIDTH = 256
CONV_K = 31
N_EGROUPS = 4
EXP_PER_GROUP = 8
N_EXPERTS = N_EGROUPS * EXP_PER_GROUP
D_FF_E = 512

LANES = 128
SUBLANES = 8
HEAD_PAD = LANES
IN_PROJ_PAD = Q_LORA + KV_LORA + SSM_WIDTH + 2 * CONV_WIDTH + 2 * LANES
CONV_HALO = 32
VMEM_LIMIT = 48 * 1024 * 1024

TM_PROJ = 512
TQ_ATTN = 256
TC_SSM = 64
CONV_SUB = 64
TM_OUT = 512
TM_MOVE = 256
RUN_ROWS = SUBLANES
RUN_SLOTS = 2 * N_EXPERTS + 2 * TM_MOVE // RUN_ROWS
RUN_GROUP = 8
SC_CORES, SC_SUBCORES = 2, 16
SC_CHUNK = 64
TB_EXP = 512
FF_CHUNK = 256

BF16 = jnp.bfloat16
F32 = jnp.float32
U32 = jnp.uint32
D_PACK = D_MODEL // 2


def _pack_rows(v):
    bits = lax.bitcast_convert_type(v.astype(BF16).astype(F32), U32)
    half = v.shape[1] // 2
    return bits[:, 0:half] | (bits[:, half:] >> 16)


def _unpack_rows(w):
    hi = lax.bitcast_convert_type(w & jnp.uint32(0xFFFF0000), F32)
    lo = lax.bitcast_convert_type(w << 16, F32)
    return jnp.concatenate([hi, lo], axis=1).astype(BF16)


def _rms(x, w):
    return x * lax.rsqrt(jnp.mean(x * x, axis=-1, keepdims=True) + EPS) * w


def _params(*sem):
    return pltpu.CompilerParams(dimension_semantics=sem, vmem_limit_bytes=VMEM_LIMIT)


def _in_proj_kernel(x_ref, rope_ref, mixn_ref, win_ref, qan_ref, wuq_ref, kvan_ref, wuk_ref, wuv_ref,
                    qn_ref, kn_ref, q_ref, k_ref, v_ref, u_ref, ca_ref, cg_ref):
    x = x_ref[0]
    h = _rms(x, mixn_ref[...]).astype(BF16)
    proj = jnp.dot(h, win_ref[...], preferred_element_type=F32)
    o = 0
    c_q = proj[:, o:o + Q_LORA]; o += Q_LORA
    c_kv = proj[:, o:o + KV_LORA]; o += KV_LORA
    u_ref[...] = proj[:, o:o + SSM_WIDTH].astype(BF16); o += SSM_WIDTH
    ca_ref[0] = proj[:, o:o + CONV_WIDTH].astype(BF16); o += CONV_WIDTH
    cg_ref[0] = proj[:, o:o + CONV_WIDTH].astype(BF16); o += CONV_WIDTH
    k_pe = proj[:, o:o + LANES]; o += LANES
    k_pe_sw = proj[:, o:o + LANES]

    width = N_HEADS * HEAD_PAD
    q2 = jnp.dot(_rms(c_q, qan_ref[...]).astype(BF16), wuq_ref[...], preferred_element_type=F32)
    q, q_sw = q2[:, 0:width], q2[:, width:2 * width]
    ckv_n = _rms(c_kv, kvan_ref[...]).astype(BF16)
    kn = jnp.dot(ckv_n, wuk_ref[...], preferred_element_type=F32)
    v_ref[0] = jnp.dot(ckv_n, wuv_ref[...], preferred_element_type=F32).astype(BF16)

    rope = rope_ref[0]
    half = QK_ROPE // 2
    lane = lax.broadcasted_iota(jnp.int32, rope.shape, 1)
    lo = (lane >= QK_NOPE) & (lane < QK_NOPE + half)
    hi = (lane >= QK_NOPE + half) & (lane < QK_HEAD)
    cos_t = jnp.where(lane < QK_NOPE, 1.0, jnp.where(lo, rope, jnp.where(hi, pltpu.roll(rope, half, 1), 0.0)))
    sin_t = jnp.where(lo, -pltpu.roll(rope, LANES - half, 1), jnp.where(hi, rope, 0.0))
    a_q, b_q = qn_ref[0:1, :] * cos_t, qn_ref[1:2, :] * sin_t
    a_k = kn_ref[0:1, :] * cos_t
    k_sw_term = k_pe_sw * (kn_ref[1:2, :] * sin_t)

    def inv_rms(y):
        return lax.rsqrt(jnp.sum(y * y, axis=-1, keepdims=True) * (1.0 / QK_HEAD) + EPS)

    for hd in range(N_HEADS):
        sl = slice(hd * HEAD_PAD, (hd + 1) * HEAD_PAD)
        yq = q[:, sl]
        q_ref[0, hd] = ((yq * a_q + q_sw[:, sl] * b_q) * inv_rms(yq)).astype(BF16)
        yk = kn[:, sl] + k_pe
        k_ref[0, hd] = ((yk * a_k + k_sw_term) * inv_rms(yk)).astype(BF16)


def _in_proj(x, rope, mixn, win, qan, wuq, kvan, wuk, wuv, qn, kn):
    B, L, D = x.shape
    tm = min(TM_PROJ, L)
    full = lambda shape: pl.BlockSpec(shape, lambda b, t: (0,) * len(shape))
    return pl.pallas_call(
        _in_proj_kernel,
        grid=(B, L // tm),
        in_specs=[
            pl.BlockSpec((1, tm, D), lambda b, t: (b, t, 0)),
            pl.BlockSpec((1, tm, LANES), lambda b, t: (b, t, 0)),
            full((1, D)), full((D, IN_PROJ_PAD)), full((1, Q_LORA)), full((Q_LORA, 2 * N_HEADS * HEAD_PAD)),
            full((1, KV_LORA)), full((KV_LORA, N_HEADS * HEAD_PAD)), full((KV_LORA, ATTN_WIDTH)),
            full((2, HEAD_PAD)), full((2, HEAD_PAD)),
        ],
        out_specs=[
            pl.BlockSpec((1, N_HEADS, tm, HEAD_PAD), lambda b, t: (b, 0, t, 0)),
            pl.BlockSpec((1, N_HEADS, tm, HEAD_PAD), lambda b, t: (b, 0, t, 0)),
            pl.BlockSpec((1, tm, ATTN_WIDTH), lambda b, t: (b, t, 0)),
            pl.BlockSpec((tm, SSM_WIDTH), lambda b, t: (t, b)),
            pl.BlockSpec((1, tm, CONV_WIDTH), lambda b, t: (b, t, 0)),
            pl.BlockSpec((1, tm, CONV_WIDTH), lambda b, t: (b, t, 0)),
        ],
        out_shape=[
            jax.ShapeDtypeStruct((B, N_HEADS, L, HEAD_PAD), BF16),
            jax.ShapeDtypeStruct((B, N_HEADS, L, HEAD_PAD), BF16),
            jax.ShapeDtypeStruct((B, L, ATTN_WIDTH), BF16),
            jax.ShapeDtypeStruct((L, B * SSM_WIDTH), BF16),
            jax.ShapeDtypeStruct((B, L, CONV_WIDTH), BF16),
            jax.ShapeDtypeStruct((B, L, CONV_WIDTH), BF16),
        ],
        compiler_params=_params("parallel", "parallel"),
    )(x, rope, mixn, win, qan, wuq, kvan, wuk, wuv, qn, kn)


def _attention_kernel(q_ref, k_ref, v_ref, o_ref, vext_ref, *, seq, tq):
    pair = 2 * V_HEAD
    vext_ref[:, 0:pair] = v_ref[0]
    vext_ref[:, pair:] = jnp.ones((seq, LANES), BF16)
    row_chunk = lax.broadcasted_iota(jnp.int32, (tq, tq), 0) // CHUNK
    col_chunk = lax.broadcasted_iota(jnp.int32, (tq, tq), 1) // CHUNK
    visible = col_chunk <= row_chunk
    neg = jnp.finfo(F32).min
    nt = (((1,), (1,)), ((), ()))
    low_lanes = lax.broadcasted_iota(jnp.int32, (tq, pair), 1) < V_HEAD
    for i in range(seq // tq):
        q0 = i * tq
        outs = []
        for hh in range(2):
            qb = q_ref[0, hh, q0:q0 + tq, :]
            s_d = lax.dot_general(qb, k_ref[0, hh, q0:q0 + tq, :], nt, preferred_element_type=F32)
            s_d = jnp.where(visible, s_d, neg)
            m = jnp.max(s_d, axis=-1, keepdims=True)
            if i > 0:
                s_l = lax.dot_general(qb, k_ref[0, hh, 0:q0, :], nt, preferred_element_type=F32)
                m = jnp.maximum(m, jnp.max(s_l, axis=-1, keepdims=True))
            acc = jnp.dot(jnp.exp(s_d - m).astype(BF16), vext_ref[q0:q0 + tq, :], preferred_element_type=F32)
            if i > 0:
                acc = acc + jnp.dot(jnp.exp(s_l - m).astype(BF16), vext_ref[0:q0, :], preferred_element_type=F32)
            outs.append(acc[:, 0:pair] / acc[:, pair:])
        o_ref[0, q0:q0 + tq, :] = jnp.where(low_lanes, outs[0], outs[1]).astype(BF16)


def _attention(q, k, v):
    B, H, L, _ = q.shape
    tq = min(TQ_ATTN, L)
    return pl.pallas_call(
        functools.partial(_attention_kernel, seq=L, tq=tq),
        grid=(B, H // 2),
        in_specs=[
            pl.BlockSpec((1, 2, L, HEAD_PAD), lambda b, p: (b, p, 0, 0)),
            pl.BlockSpec((1, 2, L, HEAD_PAD), lambda b, p: (b, p, 0, 0)),
            pl.BlockSpec((1, L, 2 * V_HEAD), lambda b, p: (b, 0, p)),
        ],
        out_specs=pl.BlockSpec((1, L, 2 * V_HEAD), lambda b, p: (b, 0, p)),
        out_shape=jax.ShapeDtypeStruct((B, L, ATTN_WIDTH), BF16),
        scratch_shapes=[pltpu.VMEM((L, 2 * V_HEAD + LANES), BF16)],
        compiler_params=_params("parallel", "parallel"),
    )(q, k, v)


def _ssm_kernel(u_ref, bbd_ref, lam_ref, cbd_ref, d_ref, wglu_ref, g_ref, o_ref, state_ref, bu_a, bu_b, *, batch, tc):
    @pl.when(pl.program_id(0) == 0)
    def _():
        state_ref[...] = jnp.zeros_like(state_ref)

    half = tc // 2 * batch
    halves = ((bu_a, slice(0, half)), (bu_b, slice(half, 2 * half)))
    for bu_ref, rows in halves:
        bu_ref[...] = jnp.dot(u_ref[rows, :], bbd_ref[...], preferred_element_type=F32)
    lam_re = jnp.broadcast_to(lam_ref[0:1, :], (batch, SSM_FLAT))
    lam_im = jnp.broadcast_to(lam_ref[1:2, :], (batch, SSM_FLAT))

    carry = (state_ref[:, 0:SSM_FLAT], state_ref[:, SSM_FLAT:2 * SSM_FLAT])
    for bu_ref, _ in halves:
        def step(t, xs, bu_ref=bu_ref):
            xr, xi = xs
            at = pl.ds(pl.multiple_of(t * batch, batch), batch)
            nr = lam_re * xr - lam_im * xi + bu_ref[at, 0:SSM_FLAT]
            ni = lam_re * xi + lam_im * xr + bu_ref[at, SSM_FLAT:2 * SSM_FLAT]
            bu_ref[at, 0:SSM_FLAT] = nr
            bu_ref[at, SSM_FLAT:2 * SSM_FLAT] = ni
            return nr, ni

        carry = lax.fori_loop(0, tc // 2, step, carry, unroll=True)
    state_ref[:, 0:SSM_FLAT] = carry[0]
    state_ref[:, SSM_FLAT:2 * SSM_FLAT] = carry[1]

    for bu_ref, rows in halves:
        y = jnp.dot(bu_ref[...].astype(BF16), cbd_ref[...], preferred_element_type=F32)
        y = y + d_ref[...] * u_ref[rows, :].astype(F32)
        z = jax.nn.gelu(y)
        gate = jax.nn.sigmoid(jnp.dot(z.astype(BF16), wglu_ref[...], preferred_element_type=F32))
        o_ref[rows, :] = _rms(z * gate, g_ref[...]).astype(BF16)


def _ssm(u_tm, bbd, lam, cbd, d, wglu, g, batch):
    rows = u_tm.shape[0]
    seq = rows // batch
    tc = min(TC_SSM, seq)
    blk = tc * batch
    full = lambda shape: pl.BlockSpec(shape, lambda t: (0,) * len(shape))
    return pl.pallas_call(
        functools.partial(_ssm_kernel, batch=batch, tc=tc),
        grid=(seq // tc,),
        in_specs=[
            pl.BlockSpec((blk, SSM_WIDTH), lambda t: (t, 0)),
            full((SSM_WIDTH, 2 * SSM_FLAT)), full((2, SSM_FLAT)), full((2 * SSM_FLAT, SSM_WIDTH)),
            full((1, SSM_WIDTH)), full((SSM_WIDTH, SSM_WIDTH)), full((1, SSM_WIDTH)),
        ],
        out_specs=pl.BlockSpec((blk, SSM_WIDTH), lambda t: (t, 0)),
        out_shape=jax.ShapeDtypeStruct((rows, SSM_WIDTH), BF16),
        scratch_shapes=[pltpu.VMEM((batch, 2 * SSM_FLAT), F32), pltpu.VMEM((blk // 2, 2 * SSM_FLAT), F32),
                        pltpu.VMEM((blk // 2, 2 * SSM_FLAT), F32)],
        compiler_params=_params("arbitrary"),
    )(u_tm, bbd, lam, cbd, d, wglu, g)


def _conv_rows(upad_ref, r0, w_ref, b_ref, lnw_ref, lnb_ref, gn_ref):
    base = CONV_HALO - (CONV_K - 1)
    acc = jnp.zeros((CONV_SUB, CONV_WIDTH), F32)
    for off in range(SUBLANES):
        n = CONV_SUB + (SUBLANES if off else 0)
        part = None
        for a8 in range(0, CONV_HALO + 1, SUBLANES):
            kk = a8 + off - base
            if 0 <= kk < CONV_K:
                term = w_ref[kk:kk + 1, :] * upad_ref[r0 + a8:r0 + a8 + n, :]
                part = term if part is None else part + term
        acc = acc + part[off:off + CONV_SUB]
    y = acc + b_ref[...]
    mu = jnp.mean(y, axis=-1, keepdims=True)
    var = jnp.mean(jnp.square(y - mu), axis=-1, keepdims=True)
    y = (y - mu) * lax.rsqrt(var + 1e-5) * lnw_ref[...] + lnb_ref[...]
    y = y * jax.nn.sigmoid(y)
    return _rms(y, gn_ref[...])


ROUTE_E1, ROUTE_E2, ROUTE_R1, ROUTE_R2, ROUTE_G1, ROUTE_G2 = range(6)
ROUTE_ROWS = 8
GRP_LANE0 = N_EXPERTS


def _out_router_kernel(x_ref, ya_ref, ys_ref, ca_ref, cg_ref, ga_ref, wo_ref, fn_ref, wr_ref, br_ref,
                       cw_ref, cb_ref, clnw_ref, clnb_ref, cgn_ref,
                       x1_ref, h2_ref, route_ref, route_t_ref, cnt_ref, sub_ref,
                       carry_ref, tri_ref, upad_ref, yc_ref, *, tm, per_b):
    @pl.when(pl.program_id(0) == 0)
    def _():
        carry_ref[...] = jnp.zeros_like(carry_ref)
        rr = lax.broadcasted_iota(jnp.int32, (tm, tm), 0)
        cc = lax.broadcasted_iota(jnp.int32, (tm, tm), 1)
        tri_ref[...] = (cc < rr).astype(BF16)

    first = pl.program_id(0) % per_b == 0
    upad_ref[0:CONV_HALO, :] = jnp.where(first, 0.0, upad_ref[tm:tm + CONV_HALO, :])
    upad_ref[CONV_HALO:, :] = ca_ref[...].astype(F32) * jax.nn.sigmoid(cg_ref[...].astype(F32))
    for r0 in range(0, tm, CONV_SUB):
        yc_ref[r0:r0 + CONV_SUB, :] = _conv_rows(upad_ref, r0, cw_ref, cb_ref, clnw_ref, clnb_ref, cgn_ref).astype(BF16)

    ya = _rms(ya_ref[...].astype(F32), ga_ref[...]).astype(BF16)
    acc = jnp.dot(ya, wo_ref[0:ATTN_WIDTH, :], preferred_element_type=F32)
    acc += jnp.dot(ys_ref[...], wo_ref[ATTN_WIDTH:ATTN_WIDTH + SSM_WIDTH, :], preferred_element_type=F32)
    acc += jnp.dot(yc_ref[...], wo_ref[ATTN_WIDTH + SSM_WIDTH:, :], preferred_element_type=F32)
    x1 = x_ref[...] + acc
    x1_ref[...] = x1
    h2 = _rms(x1, fn_ref[...])
    h2_ref[...] = _pack_rows(h2)

    h_hi = h2.astype(BF16)
    h_lo = (h2 - h_hi.astype(F32)).astype(BF16)
    part = jnp.dot(h_hi, wr_ref[...], preferred_element_type=F32)
    logits = (part[:, 0:LANES] + part[:, LANES:2 * LANES]
              + jnp.dot(h_lo, wr_ref[:, 0:LANES], preferred_element_type=F32) + br_ref[...])
    lane = lax.broadcasted_iota(jnp.int32, (tm, LANES), 1)
    ninf = -jnp.inf
    big = LANES

    def first_argmax(vals, vmax):
        return jnp.min(jnp.where(vals == vmax, lane, big), axis=-1, keepdims=True)

    grp = jnp.where((lane >= GRP_LANE0) & (lane < GRP_LANE0 + N_EGROUPS), logits, ninf)
    gmax = jnp.max(grp, axis=-1, keepdims=True)
    gsel = first_argmax(grp, gmax) - GRP_LANE0
    p_grp = 1.0 / jnp.sum(jnp.exp(grp - gmax), axis=-1, keepdims=True)

    el = jnp.where((lane < N_EXPERTS) & ((lane // EXP_PER_GROUP) == gsel), logits, ninf)
    m1 = jnp.max(el, axis=-1, keepdims=True)
    e1 = first_argmax(el, m1)
    el2 = jnp.where(lane == e1, ninf, el)
    m2 = jnp.max(el2, axis=-1, keepdims=True)
    e2 = first_argmax(el2, m2)
    t2 = jnp.exp(m2 - m1)
    g1 = p_grp / (1.0 + t2)
    g2 = p_grp * t2 / (1.0 + t2)

    hit1 = lane == e1
    hit2 = lane == e2
    cnt = (hit1 | hit2).astype(F32)
    before = jnp.dot(tri_ref[...], cnt.astype(BF16), preferred_element_type=F32) + carry_ref[...]
    r1 = jnp.sum(jnp.where(hit1, before, 0.0), axis=-1, keepdims=True)
    r2 = jnp.sum(jnp.where(hit2, before, 0.0), axis=-1, keepdims=True)
    subs = [before[h * TM_MOVE:h * TM_MOVE + 1, :] for h in range(tm // TM_MOVE)]
    sub_ref[...] = jnp.concatenate(subs + [jnp.zeros((SUBLANES - len(subs), LANES), F32)], axis=0)
    carry_ref[...] += jnp.sum(cnt, axis=0, keepdims=True)
    cnt_ref[...] = carry_ref[...]

    rec = jnp.where(lane == ROUTE_E1, e1.astype(F32), 0.0)
    rec = jnp.where(lane == ROUTE_E2, e2.astype(F32), rec)
    rec = jnp.where(lane == ROUTE_R1, r1, rec)
    rec = jnp.where(lane == ROUTE_R2, r2, rec)
    rec = jnp.where(lane == ROUTE_G1, g1, rec)
    rec = jnp.where(lane == ROUTE_G2, g2, rec)
    route_ref[...] = rec
    route_t_ref[...] = rec.T[0:ROUTE_ROWS, :]


def _out_router(x, ya, ys_tm, ca, cg, ga, wo, fn, wr, br, conv_params, batch):
    N, D = x.shape
    seq = N // batch
    tm = min(TM_OUT, seq)
    per_b = seq // tm
    full = lambda shape: pl.BlockSpec(shape, lambda i: (0,) * len(shape))
    tile = lambda w: pl.BlockSpec((tm, w), lambda i: (i, 0))
    return pl.pallas_call(
        functools.partial(_out_router_kernel, tm=tm, per_b=per_b),
        grid=(N // tm,),
        in_specs=[
            tile(D), tile(ATTN_WIDTH),
            pl.BlockSpec((tm, SSM_WIDTH), lambda i: (i % per_b, i // per_b)),
            tile(CONV_WIDTH), tile(CONV_WIDTH),
            full((1, ATTN_WIDTH)), full((D, D)), full((1, D)), full((D, 2 * LANES)), full((1, LANES)),
            full((CONV_K, CONV_WIDTH)), full((1, CONV_WIDTH)), full((1, CONV_WIDTH)), full((1, CONV_WIDTH)),
            full((1, CONV_WIDTH)),
        ],
        out_specs=[tile(D), tile(D_PACK), tile(LANES), pl.BlockSpec((ROUTE_ROWS, tm), lambda i: (0, i)),
                   full((1, LANES)), pl.BlockSpec((SUBLANES, LANES), lambda i: (i, 0))],
        out_shape=[
            jax.ShapeDtypeStruct((N, D), F32),
            jax.ShapeDtypeStruct((N, D_PACK), U32),
            jax.ShapeDtypeStruct((N, LANES), F32),
            jax.ShapeDtypeStruct((ROUTE_ROWS, N), F32),
            jax.ShapeDtypeStruct((1, LANES), F32),
            jax.ShapeDtypeStruct((N // tm * SUBLANES, LANES), F32),
        ],
        scratch_shapes=[pltpu.VMEM((1, LANES), F32), pltpu.VMEM((tm, tm), BF16),
                        pltpu.VMEM((tm + CONV_HALO, CONV_WIDTH), F32), pltpu.VMEM((tm, CONV_WIDTH), BF16)],
        compiler_params=_params("arbitrary"),
    )(x, ya, ys_tm, ca, cg, ga, wo, fn, wr, br, *conv_params)


def _dispatch(dest_flat, h2, n_rows):
    N, D = h2.shape
    workers = SC_CORES * SC_SUBCORES
    per_worker = N // workers
    chunks = per_worker // SC_CHUNK
    idx = dest_flat.reshape(2, workers, chunks, SC_CHUNK).transpose(1, 0, 2, 3).reshape(workers, 2 * chunks, SC_CHUNK)
    mesh = plsc.VectorSubcoreMesh(core_axis_name="c", subcore_axis_name="s")

    @functools.partial(
        pl.kernel, mesh=mesh,
        out_type=jax.ShapeDtypeStruct((n_rows, D), h2.dtype),
        scratch_types=[pltpu.VMEM((2 * chunks, SC_CHUNK), jnp.int32), pltpu.VMEM((2, SC_CHUNK, D), h2.dtype),
                       pltpu.SemaphoreType.DMA((2,)), pltpu.SemaphoreType.DMA((2, 2))],
    )
    def scatter(h_hbm, idx_hbm, out_hbm, idx_v, rows_v, sem_in, sem_out):
        wid = lax.axis_index("s") * SC_CORES + lax.axis_index("c")
        pltpu.sync_copy(idx_hbm.at[wid], idx_v)

        def load(c):
            first = pl.multiple_of(wid * per_worker + c * SC_CHUNK, SC_CHUNK)
            return pltpu.async_copy(h_hbm.at[pl.ds(first, SC_CHUNK)], rows_v.at[c % 2], sem_in.at[c % 2])

        def store(c, k):
            return pltpu.async_copy(rows_v.at[c % 2], out_hbm.at[idx_v.at[k * chunks + c]], sem_out.at[c % 2, k])

        loads = {0: load(0)}
        stores = {}
        for c in range(chunks):
            loads[c].wait()
            if c >= 1:
                for st in stores[c - 1]:
                    st.wait()
            if c + 1 < chunks:
                loads[c + 1] = load(c + 1)
            stores[c] = (store(c, 0), store(c, 1))
        for st in stores[chunks - 1]:
            st.wait()

    return scatter(h2, idx)


def _experts_kernel(blk_ref, exp_ref, valid_ref, slot_ref, next_ref, used_ref, x_ref, w1_hbm, w3_hbm, w2_hbm, y_ref,
                    w1_f, w3_f, w2_f, w13_s, w2_s, sem, *, layer):
    j = pl.program_id(0)

    def weight_copies(e, slot):
        return (pltpu.make_async_copy(w1_hbm.at[layer, e], w1_f.at[slot], sem.at[slot, 0]),
                pltpu.make_async_copy(w3_hbm.at[layer, e], w3_f.at[slot], sem.at[slot, 1]),
                pltpu.make_async_copy(w2_hbm.at[layer, e], w2_f.at[slot], sem.at[slot, 2]))

    @pl.when(j < used_ref[0])
    def _():
        e = exp_ref[j]
        slot = slot_ref[j]

        @pl.when(j == 0)
        def _():
            for cp in weight_copies(e, slot):
                cp.start()

        @pl.when((j == 0) | (e != exp_ref[jnp.maximum(j - 1, 0)]))
        def _():
            for cp in weight_copies(e, slot):
                cp.wait()
            for c in range(D_FF_E // FF_CHUNK):
                w13_s[:, 2 * FF_CHUNK * c:2 * FF_CHUNK * c + FF_CHUNK] = \
                    w1_f[slot, :, FF_CHUNK * c:FF_CHUNK * (c + 1)].astype(BF16)
                w13_s[:, 2 * FF_CHUNK * c + FF_CHUNK:2 * FF_CHUNK * (c + 1)] = \
                    w3_f[slot, :, FF_CHUNK * c:FF_CHUNK * (c + 1)].astype(BF16)
            w2_s[...] = w2_f[slot].astype(BF16)

            @pl.when(next_ref[j] >= 0)
            def _():
                for cp in weight_copies(next_ref[j], 1 - slot):
                    cp.start()

        x = _unpack_rows(x_ref[...])
        y = None
        for c in range(D_FF_E // FF_CHUNK):
            ab = jnp.dot(x, w13_s[:, 2 * FF_CHUNK * c:2 * FF_CHUNK * (c + 1)], preferred_element_type=F32)
            a = ab[:, 0:FF_CHUNK]
            hmid = (a * jax.nn.sigmoid(a) * ab[:, FF_CHUNK:]).astype(BF16)
            part = jnp.dot(hmid, w2_s[FF_CHUNK * c:FF_CHUNK * (c + 1), :], preferred_element_type=F32)
            y = part if y is None else y + part
        row = lax.broadcasted_iota(jnp.int32, (TB_EXP, 1), 0)
        y_ref[...] = jnp.where(row < valid_ref[j], _pack_rows(y), jnp.uint32(0))


def _experts(blk_map, blk_exp, blk_valid, blk_slot, blk_next, n_used, x_rows, w1, w3, w2, layer):
    R, D = x_rows.shape[0], D_MODEL
    nb = R // TB_EXP
    row_block = lambda j, bm, be, bv, bs, bn, nu: (bm[j], 0)
    return pl.pallas_call(
        functools.partial(_experts_kernel, layer=layer),
        grid_spec=pltpu.PrefetchScalarGridSpec(
            num_scalar_prefetch=6,
            grid=(nb,),
            in_specs=[
                pl.BlockSpec((TB_EXP, D_PACK), row_block),
                pl.BlockSpec(memory_space=pl.ANY), pl.BlockSpec(memory_space=pl.ANY), pl.BlockSpec(memory_space=pl.ANY),
            ],
            out_specs=pl.BlockSpec((TB_EXP, D_PACK), row_block),
            scratch_shapes=[
                pltpu.VMEM((2, D, D_FF_E), F32), pltpu.VMEM((2, D, D_FF_E), F32), pltpu.VMEM((2, D_FF_E, D), F32),
                pltpu.VMEM((D, 2 * D_FF_E), BF16), pltpu.VMEM((D_FF_E, D), BF16),
                pltpu.SemaphoreType.DMA((2, 3)),
            ],
        ),
        out_shape=jax.ShapeDtypeStruct((R, D_PACK), U32),
        compiler_params=_params("arbitrary"),
    )(blk_map, blk_exp, blk_valid, blk_slot, blk_next, n_used, x_rows, w1, w3, w2)


def _combine_kernel(src_ref, nslot_ref, x1_ref, route_ref, off_ref, rows_ref, o_ref, buf, sem, *, tm):
    i = pl.program_id(0)
    cur = i % 2

    def slot_copy(tile, half, s):
        src = pl.multiple_of(src_ref[tile * RUN_SLOTS + s], RUN_ROWS)
        dst = buf.at[half, pl.ds(pl.multiple_of(s * RUN_ROWS, RUN_ROWS), RUN_ROWS)]
        return pltpu.make_async_copy(rows_ref.at[pl.ds(src, RUN_ROWS)], dst, sem.at[half])

    def fetch(tile, half):
        def issue(g, c):
            for u in range(RUN_GROUP):
                slot_copy(tile, half, g * RUN_GROUP + u).start()
            return c
        lax.fori_loop(0, nslot_ref[tile], issue, 0)

    def wait(g, c):
        rows = RUN_GROUP * RUN_ROWS
        pltpu.make_async_copy(rows_ref.at[pl.ds(0, rows)], buf.at[cur, pl.ds(0, rows)], sem.at[cur]).wait()
        return c

    @pl.when(i == 0)
    def _():
        buf[...] = jnp.zeros_like(buf)
        fetch(0, 0)

    @pl.when(i + 1 < pl.num_programs(0))
    def _():
        fetch(i + 1, 1 - cur)

    route = route_ref[...]
    lane = lax.broadcasted_iota(jnp.int32, (tm, LANES), 1).astype(F32)
    off = off_ref[0]

    def buf_row(e_lane, r_lane):
        e = route[:, e_lane:e_lane + 1]
        base = jnp.sum(jnp.where(lane == e, off, 0.0), axis=-1, keepdims=True)
        return (base + route[:, r_lane:r_lane + 1]).astype(jnp.int32)

    col = lax.broadcasted_iota(jnp.int32, (tm, RUN_SLOTS * RUN_ROWS), 1)
    pick = (jnp.where(col == buf_row(ROUTE_E1, ROUTE_R1), route[:, ROUTE_G1:ROUTE_G1 + 1], 0.0)
            + jnp.where(col == buf_row(ROUTE_E2, ROUTE_R2), route[:, ROUTE_G2:ROUTE_G2 + 1], 0.0))
    lax.fori_loop(0, nslot_ref[i], wait, 0)
    o_ref[...] = x1_ref[...] + jnp.dot(pick.astype(BF16), _unpack_rows(buf[cur]), preferred_element_type=F32)


def _combine(src, nslot, x1, route, off, y_rows):
    N, D = x1.shape
    tm = min(TM_MOVE, N)
    return pl.pallas_call(
        functools.partial(_combine_kernel, tm=tm),
        grid_spec=pltpu.PrefetchScalarGridSpec(
            num_scalar_prefetch=2,
            grid=(N // tm,),
            in_specs=[
                pl.BlockSpec((tm, D), lambda i, s, n: (i, 0)),
                pl.BlockSpec((tm, LANES), lambda i, s, n: (i, 0)),
                pl.BlockSpec((1, 1, LANES), lambda i, s, n: (i, 0, 0)),
                pl.BlockSpec(memory_space=pl.ANY),
            ],
            out_specs=pl.BlockSpec((tm, D), lambda i, s, n: (i, 0)),
            scratch_shapes=[pltpu.VMEM((2, RUN_SLOTS * RUN_ROWS, D_PACK), U32), pltpu.SemaphoreType.DMA((2,))],
        ),
        out_shape=jax.ShapeDtypeStruct((N, D), F32),
        compiler_params=_params("arbitrary"),
    )(src, nslot, x1, route, off, y_rows)


def _moe(x1, h2, route, route_t, counts, sub_carry, w1, w3, w2, layer):
    N, _ = x1.shape
    tm = min(TM_MOVE, N)
    n_tiles = N // tm
    experts = jnp.arange(N_EXPERTS, dtype=jnp.int32)
    nb = (2 * N + N_EXPERTS * (TB_EXP - 1)) // TB_EXP + 1
    e_id = route_t[ROUTE_E1:ROUTE_E2 + 1].astype(jnp.int32)
    rank = route_t[ROUTE_R1:ROUTE_R2 + 1].astype(jnp.int32)
    cnt = counts[0, :N_EXPERTS].astype(jnp.int32)
    padded = (cnt + TB_EXP - 1) // TB_EXP * TB_EXP
    pad_end = jnp.cumsum(padded)
    pad_start = pad_end - padded
    e_flat = e_id.reshape(1, 2 * N)
    dest = jnp.sum(jnp.where(e_flat == experts[:, None], pad_start[:, None], 0), axis=0) + rank.reshape(2 * N)
    n_used = pad_end[-1] // TB_EXP
    blk_map = jnp.minimum(jnp.arange(nb, dtype=jnp.int32), jnp.maximum(n_used - 1, 0))
    blk_exp = jnp.sum(pad_end[None, :] <= (blk_map * TB_EXP)[:, None], axis=1).astype(jnp.int32)
    blk_exp = jnp.minimum(blk_exp, N_EXPERTS - 1)
    is_blk_e = blk_exp[:, None] == experts[None, :]
    blk_valid = jnp.clip(jnp.sum(jnp.where(is_blk_e, (pad_start + cnt)[None, :], 0), axis=1) - blk_map * TB_EXP,
                         0, TB_EXP).astype(jnp.int32)
    has_rows = cnt > 0
    slot_of_e = (jnp.cumsum(has_rows.astype(jnp.int32)) - 1) % 2
    later = jnp.where(has_rows, experts, N_EXPERTS)
    next_of_e = lax.cummin(jnp.concatenate([later[1:], jnp.full((1,), N_EXPERTS, jnp.int32)]), reverse=True)
    next_of_e = jnp.where(next_of_e < N_EXPERTS, next_of_e, -1)
    blk_slot = jnp.sum(jnp.where(is_blk_e, slot_of_e[None, :], 0), axis=1).astype(jnp.int32)
    blk_next = jnp.sum(jnp.where(is_blk_e, next_of_e[None, :], 0), axis=1).astype(jnp.int32)

    per_router_tile = TM_OUT // tm if N >= TM_OUT else 1
    carry = sub_carry.reshape(-1, SUBLANES, LANES)[:, :per_router_tile, :N_EXPERTS].reshape(n_tiles, N_EXPERTS)
    carry = carry.astype(jnp.int32)
    tile_cnt = jnp.concatenate([carry[1:], cnt[None, :]], axis=0) - carry
    run_start = pad_start[None, :] + carry
    first_blk = run_start // RUN_ROWS
    nslot = jnp.where(tile_cnt > 0, (run_start + tile_cnt - 1) // RUN_ROWS - first_blk + 1, 0)
    slot_end = jnp.cumsum(nslot, axis=1)
    slot_base = slot_end - nslot
    slots = jnp.arange(RUN_SLOTS, dtype=jnp.int32)
    slot_e = jnp.minimum(jnp.sum(slot_end[:, None, :] <= slots[None, :, None], axis=2), N_EXPERTS - 1)
    is_slot_e = slot_e[:, :, None] == experts[None, None, :]
    pick = lambda tbl: jnp.sum(jnp.where(is_slot_e, tbl[:, None, :], 0), axis=2)
    src = (pick(first_blk) + slots[None, :] - pick(slot_base)) * RUN_ROWS
    src = jnp.where(slots[None, :] < slot_end[:, -1:], src, 0)
    src = jnp.clip(src, 0, nb * TB_EXP - RUN_ROWS).reshape(-1).astype(jnp.int32)
    n_groups = (slot_end[:, -1] + RUN_GROUP - 1) // RUN_GROUP
    off = (slot_base - first_blk) * RUN_ROWS + pad_start[None, :]
    off = jnp.pad(off.astype(F32), ((0, 0), (0, LANES - N_EXPERTS)))

    x_rows = _dispatch(dest, h2, nb * TB_EXP)
    y_rows = _experts(blk_map, blk_exp, blk_valid, blk_slot, blk_next, n_used.reshape(1).astype(jnp.int32),
                      x_rows, w1, w3, w2, layer)
    return _combine(src, n_groups.astype(jnp.int32), x1, route, off.reshape(n_tiles, 1, LANES), y_rows)


def _pad_heads(w, width):
    k = w.shape[0]
    w = w.reshape(k, N_HEADS, width)
    return jnp.pad(w, ((0, 0), (0, 0), (0, HEAD_PAD - width))).reshape(k, N_HEADS * HEAD_PAD)


def _swap_rope(w):
    half = QK_ROPE // 2
    lo, hi = w[..., QK_NOPE:QK_NOPE + half], w[..., QK_NOPE + half:QK_HEAD]
    pad = [(0, 0)] * (w.ndim - 1)
    return jnp.pad(jnp.concatenate([hi, lo], -1), pad + [(QK_NOPE, HEAD_PAD - QK_HEAD)])


def _rope_tables(positions):
    inv_freq = ROPE_THETA ** (-jnp.arange(0, QK_ROPE, 2, dtype=F32) / QK_ROPE)
    ang = positions.astype(F32)[..., None] * inv_freq
    table = jnp.concatenate([jnp.cos(ang), jnp.sin(ang)], -1)
    return jnp.pad(table, ((0, 0), (0, 0), (QK_NOPE, LANES - QK_HEAD)))


def _ssm_params(lam_re, lam_im, b_re, b_im, c_re, c_im, log_dt):
    lam = lax.complex(lam_re, lam_im)
    dt = jnp.exp(log_dt)[:, None]
    lam_bar = jnp.exp(lam * dt)
    b_bar = ((lam_bar - 1.0) / lam)[..., None] * lax.complex(b_re, b_im)
    eye = jnp.eye(SSM_GROUPS, dtype=F32)

    def in_blockdiag(m):
        return jnp.einsum("gpc,gh->gchp", m, eye).reshape(SSM_WIDTH, SSM_FLAT)

    def out_blockdiag(m):
        return jnp.einsum("gcp,gh->gphc", m, eye).reshape(SSM_FLAT, SSM_WIDTH)

    bbd = jnp.concatenate([in_blockdiag(jnp.real(b_bar)), in_blockdiag(jnp.imag(b_bar))], axis=1)
    cbd = jnp.concatenate([out_blockdiag(c_re), out_blockdiag(-c_im)], axis=0)
    lam_rows = jnp.stack([jnp.real(lam_bar).reshape(SSM_FLAT), jnp.imag(lam_bar).reshape(SSM_FLAT)])
    return bbd.astype(BF16), lam_rows, cbd.astype(BF16)


def kernel(x, positions, mix_norm, w_in, q_a_norm, w_uq, kv_a_norm, w_ukv, q_norm, k_norm, ssm_lam_re, ssm_lam_im, ssm_b_re, ssm_b_im, ssm_c_re, ssm_c_im, ssm_d, ssm_log_dt, ssm_w_glu, conv_dw_w, conv_dw_b, conv_ln_w, conv_ln_b, out_norm, w_out, ffn_norm, w_grp, b_grp, w_exp, b_exp, w1, w3, w2):
    B, L, D = x.shape
    depth = w_in.shape[0]
    rope = _rope_tables(positions)
    row = lambda v: v.reshape(1, -1)
    lane_pad = lambda v: jnp.pad(v, (0, LANES - v.shape[0])).reshape(1, LANES)
    for l in range(depth):
        c_q, c_kv, k_pe, u_s, c_a, c_g = jnp.split(
            w_in[l], [Q_LORA, Q_LORA + KV_LORA, Q_LORA + KV_LORA + QK_ROPE,
                      Q_LORA + KV_LORA + QK_ROPE + SSM_WIDTH,
                      Q_LORA + KV_LORA + QK_ROPE + SSM_WIDTH + CONV_WIDTH], axis=1)
        k_pe_full = jnp.pad(k_pe, ((0, 0), (QK_NOPE, 0)))
        win = jnp.concatenate([c_q, c_kv, u_s, c_a, c_g, jnp.pad(k_pe_full, ((0, 0), (0, HEAD_PAD - QK_HEAD))),
                               _swap_rope(k_pe_full)], axis=1).astype(BF16)
        wkv = w_ukv[l].reshape(KV_LORA, N_HEADS, QK_NOPE + V_HEAD)
        wuk = _pad_heads(wkv[:, :, :QK_NOPE].reshape(KV_LORA, N_HEADS * QK_NOPE), QK_NOPE).astype(BF16)
        wuv = wkv[:, :, QK_NOPE:].reshape(KV_LORA, ATTN_WIDTH).astype(BF16)
        wuq_sw = _swap_rope(w_uq[l].reshape(Q_LORA, N_HEADS, QK_HEAD)).reshape(Q_LORA, N_HEADS * HEAD_PAD)
        wuq = jnp.concatenate([_pad_heads(w_uq[l], QK_HEAD), wuq_sw], axis=1).astype(BF16)
        qn_scaled = q_norm[l] * (QK_HEAD ** -0.5)
        norm_rows = lambda w: jnp.stack([jnp.pad(w, (0, HEAD_PAD - QK_HEAD)), _swap_rope(w)])
        q, k, v, u_tm, ca, cg = _in_proj(
            x, rope, row(mix_norm[l]), win, row(q_a_norm[l]), wuq, row(kv_a_norm[l]), wuk, wuv,
            norm_rows(qn_scaled), norm_rows(k_norm[l]))
        y_attn = _attention(q, k, v)

        g_out = out_norm[l]
        bbd, lam_rows, cbd = _ssm_params(ssm_lam_re[l], ssm_lam_im[l], ssm_b_re[l], ssm_b_im[l],
                                         ssm_c_re[l], ssm_c_im[l], ssm_log_dt[l])
        y_ssm_tm = _ssm(u_tm.reshape(L * B, SSM_WIDTH), bbd, lam_rows, cbd, row(ssm_d[l]),
                        ssm_w_glu[l].astype(BF16), row(g_out[ATTN_WIDTH:ATTN_WIDTH + SSM_WIDTH]), B)
        conv_params = (conv_dw_w[l], row(conv_dw_b[l]), row(conv_ln_w[l]), row(conv_ln_b[l]),
                       row(g_out[ATTN_WIDTH + SSM_WIDTH:]))

        w_route = jnp.pad(jnp.concatenate([w_exp[l], w_grp[l]], axis=1), ((0, 0), (0, LANES - N_EXPERTS - N_EGROUPS)))
        w_route_hi = w_route.astype(BF16)
        w_route = jnp.concatenate([w_route_hi, (w_route - w_route_hi.astype(F32)).astype(BF16)], axis=1)
        b_route = lane_pad(jnp.concatenate([b_exp[l], b_grp[l]]))
        x1, h2, route, route_t, counts, sub_carry = _out_router(
            x.reshape(B * L, D), y_attn.reshape(B * L, ATTN_WIDTH), y_ssm_tm.reshape(L, B * SSM_WIDTH),
            ca.reshape(B * L, CONV_WIDTH), cg.reshape(B * L, CONV_WIDTH), row(g_out[:ATTN_WIDTH]),
            w_out[l].astype(BF16), row(ffn_norm[l]), w_route, b_route, conv_params, B)
        x = _moe(x1, h2, route, route_t, counts, sub_carry, w1, w3, w2, l).reshape(B, L, D)
    return x
```

```python
import functools
import math

import jax
import jax.numpy as jnp
from jax import lax
from jax.experimental import pallas as pl
from jax.experimental.pallas import tpu as pltpu
from jax.experimental.pallas import tpu_sc as plsc

D_MODEL = 1024
CHUNK = 64
EPS = 1e-6
N_HEADS = 8
QK_NOPE = 64
QK_ROPE = 32
QK_HEAD = QK_NOPE + QK_ROPE
V_HEAD = 64
Q_LORA = 256
KV_LORA = 128
ROPE_THETA = 10000.0
ATTN_WIDTH = N_HEADS * V_HEAD
SSM_WIDTH = 256
SSM_GROUP = 16
SSM_GROUPS = SSM_WIDTH // SSM_GROUP
SSM_STATE = 64
SSM_FLAT = SSM_GROUPS * SSM_STATE
CONV_WIDTH = 256
CONV_K = 31
N_EGROUPS = 4
EXP_PER_GROUP = 8
N_EXPERTS = N_EGROUPS * EXP_PER_GROUP
D_FF_E = 512

LANES = 128
SUBLANES = 8
HEAD_PAD = LANES
IN_PROJ_PAD = Q_LORA + KV_LORA + SSM_WIDTH + 2 * CONV_WIDTH + 2 * LANES
CONV_HALO = 32
VMEM_LIMIT = 48 * 1024 * 1024

TM_PROJ = 512
TQ_ATTN = 256
TC_SSM = 64
CONV_SUB = 64
TM_OUT = 512
TM_MOVE = 256
RUN_ROWS = SUBLANES
RUN_SLOTS = 2 * N_EXPERTS + 2 * TM_MOVE // RUN_ROWS
RUN_GROUP = 8
SC_CORES, SC_SUBCORES = 2, 16
SC_CHUNK = 64
TB_EXP = 512
FF_CHUNK = 256

BF16 = jnp.bfloat16
F32 = jnp.float32
U32 = jnp.uint32
D_PACK = D_MODEL // 2


def _pack_rows(v):
    bits = lax.bitcast_convert_type(v.astype(BF16).astype(F32), U32)
    half = v.shape[1] // 2
    return bits[:, 0:half] | (bits[:, half:] >> 16)


def _unpack_rows(w):
    hi = lax.bitcast_convert_type(w & jnp.uint32(0xFFFF0000), F32)
    lo = lax.bitcast_convert_type(w << 16, F32)
    return jnp.concatenate([hi, lo], axis=1).astype(BF16)


def _rms(x, w):
    return x * lax.rsqrt(jnp.mean(x * x, axis=-1, keepdims=True) + EPS) * w


def _params(*sem):
    return pltpu.CompilerParams(dimension_semantics=sem, vmem_limit_bytes=VMEM_LIMIT)


def _in_proj_kernel(x_ref, rope_ref, mixn_ref, win_ref, qan_ref, wuq_ref, kvan_ref, wuk_ref, wuv_ref,
                    qn_ref, kn_ref, q_ref, k_ref, v_ref, u_ref, ca_ref, cg_ref):
    x = x_ref[0]
    h = _rms(x, mixn_ref[...]).astype(BF16)
    head_cols = Q_LORA + KV_LORA + 2 * LANES
    proj = jnp.dot(h, win_ref[:, 0:head_cols], preferred_element_type=F32)
    o = 0
    c_q = proj[:, o:o + Q_LORA]; o += Q_LORA
    c_kv = proj[:, o:o + KV_LORA]; o += KV_LORA
    k_pe = proj[:, o:o + LANES]; o += LANES
    k_pe_sw = proj[:, o:o + LANES]

    width = N_HEADS * HEAD_PAD
    q2 = jnp.dot(_rms(c_q, qan_ref[...]).astype(BF16), wuq_ref[...], preferred_element_type=F32)
    q, q_sw = q2[:, 0:width], q2[:, width:2 * width]
    ckv_n = _rms(c_kv, kvan_ref[...]).astype(BF16)
    kn = jnp.dot(ckv_n, wuk_ref[...], preferred_element_type=F32)
    v_ref[0] = jnp.dot(ckv_n, wuv_ref[...], preferred_element_type=F32).astype(BF16)

    rope = rope_ref[0]
    half = QK_ROPE // 2
    lane = lax.broadcasted_iota(jnp.int32, rope.shape, 1)
    lo = (lane >= QK_NOPE) & (lane < QK_NOPE + half)
    hi = (lane >= QK_NOPE + half) & (lane < QK_HEAD)
    cos_t = jnp.where(lane < QK_NOPE, 1.0, jnp.where(lo, rope, jnp.where(hi, pltpu.roll(rope, half, 1), 0.0)))
    sin_t = jnp.where(lo, -pltpu.roll(rope, LANES - half, 1), jnp.where(hi, rope, 0.0))
    a_q, b_q = qn_ref[0:1, :] * cos_t, qn_ref[1:2, :] * sin_t
    a_k = kn_ref[0:1, :] * cos_t
    k_sw_term = k_pe_sw * (kn_ref[1:2, :] * sin_t)

    def inv_rms(y):
        return lax.rsqrt(jnp.sum(y * y, axis=-1, keepdims=True) * (1.0 / QK_HEAD) + EPS)

    def rest(j):
        cols = slice(head_cols + j * SSM_WIDTH, head_cols + (j + 1) * SSM_WIDTH)
        return jnp.dot(h, win_ref[:, cols], preferred_element_type=F32).astype(BF16)

    for hd in range(N_HEADS):
        if hd == 0:
            u_ref[...] = rest(0)
        elif hd == 3:
            ca_ref[0] = rest(1)
        elif hd == 6:
            cg_ref[0] = rest(2)
        sl = slice(hd * HEAD_PAD, (hd + 1) * HEAD_PAD)
        yq = q[:, sl]
        q_ref[0, hd] = ((yq * a_q + q_sw[:, sl] * b_q) * inv_rms(yq)).astype(BF16)
        yk = kn[:, sl] + k_pe
        k_ref[0, hd] = ((yk * a_k + k_sw_term) * inv_rms(yk)).astype(BF16)


def _in_proj(x, rope, mixn, win, qan, wuq, kvan, wuk, wuv, qn, kn):
    B, L, D = x.shape
    tm = min(TM_PROJ, L)
    full = lambda shape: pl.BlockSpec(shape, lambda b, t: (0,) * len(shape))
    return pl.pallas_call(
        _in_proj_kernel,
        grid=(B, L // tm),
        in_specs=[
            pl.BlockSpec((1, tm, D), lambda b, t: (b, t, 0)),
            pl.BlockSpec((1, tm, LANES), lambda b, t: (b, t, 0)),
            full((1, D)), full((D, IN_PROJ_PAD)), full((1, Q_LORA)), full((Q_LORA, 2 * N_HEADS * HEAD_PAD)),
            full((1, KV_LORA)), full((KV_LORA, N_HEADS * HEAD_PAD)), full((KV_LORA, ATTN_WIDTH)),
            full((2, HEAD_PAD)), full((2, HEAD_PAD)),
        ],
        out_specs=[
            pl.BlockSpec((1, N_HEADS, tm, HEAD_PAD), lambda b, t: (b, 0, t, 0)),
            pl.BlockSpec((1, N_HEADS, tm, HEAD_PAD), lambda b, t: (b, 0, t, 0)),
            pl.BlockSpec((1, tm, ATTN_WIDTH), lambda b, t: (b, t, 0)),
            pl.BlockSpec((tm, SSM_WIDTH), lambda b, t: (t, b)),
            pl.BlockSpec((1, tm, CONV_WIDTH), lambda b, t: (b, t, 0)),
            pl.BlockSpec((1, tm, CONV_WIDTH), lambda b, t: (b, t, 0)),
        ],
        out_shape=[
            jax.ShapeDtypeStruct((B, N_HEADS, L, HEAD_PAD), BF16),
            jax.ShapeDtypeStruct((B, N_HEADS, L, HEAD_PAD), BF16),
            jax.ShapeDtypeStruct((B, L, ATTN_WIDTH), BF16),
            jax.ShapeDtypeStruct((L, B * SSM_WIDTH), BF16),
            jax.ShapeDtypeStruct((B, L, CONV_WIDTH), BF16),
            jax.ShapeDtypeStruct((B, L, CONV_WIDTH), BF16),
        ],
        compiler_params=_params("parallel", "parallel"),
    )(x, rope, mixn, win, qan, wuq, kvan, wuk, wuv, qn, kn)


def _attention_kernel(q_ref, k_ref, v_ref, o_ref, vext_ref, *, seq, tq):
    pair = 2 * V_HEAD
    vext_ref[:, 0:pair] = v_ref[0]
    vext_ref[:, pair:] = jnp.ones((seq, LANES), BF16)
    row_chunk = lax.broadcasted_iota(jnp.int32, (tq, tq), 0) // CHUNK
    col_chunk = lax.broadcasted_iota(jnp.int32, (tq, tq), 1) // CHUNK
    visible = col_chunk <= row_chunk
    neg = jnp.finfo(F32).min
    nt = (((1,), (1,)), ((), ()))
    low_lanes = lax.broadcasted_iota(jnp.int32, (tq, pair), 1) < V_HEAD
    def scores(i, hh):
        q0 = i * tq
        qb = q_ref[0, hh, q0:q0 + tq, :]
        s_d = lax.dot_general(qb, k_ref[0, hh, q0:q0 + tq, :], nt, preferred_element_type=F32)
        s_d = jnp.where(visible, s_d, neg)
        s_l = lax.dot_general(qb, k_ref[0, hh, 0:q0, :], nt, preferred_element_type=F32) if i > 0 else None
        return s_d, s_l

    def weighted_values(i, s_d, s_l):
        q0 = i * tq
        m = jnp.max(s_d, axis=-1, keepdims=True)
        if s_l is not None:
            m = jnp.maximum(m, jnp.max(s_l, axis=-1, keepdims=True))
        acc = jnp.dot(jnp.exp(s_d - m).astype(BF16), vext_ref[q0:q0 + tq, :], preferred_element_type=F32)
        if s_l is not None:
            acc = acc + jnp.dot(jnp.exp(s_l - m).astype(BF16), vext_ref[0:q0, :], preferred_element_type=F32)
        return acc[:, 0:pair] / acc[:, pair:]

    nq = seq // tq
    order = [x for p in zip(reversed(range(nq)), range(nq)) for x in p][:nq]
    chains = [(i, hh) for i in order for hh in range(2)]
    pending = scores(*chains[0])
    outs = {}
    for n, (i, hh) in enumerate(chains):
        upcoming = scores(*chains[n + 1]) if n + 1 < len(chains) else None
        outs[hh] = weighted_values(i, *pending)
        pending = upcoming
        if hh == 1:
            o_ref[0, i * tq:(i + 1) * tq, :] = jnp.where(low_lanes, outs[0], outs[1]).astype(BF16)


def _attention(q, k, v):
    B, H, L, _ = q.shape
    tq = min(TQ_ATTN, L)
    return pl.pallas_call(
        functools.partial(_attention_kernel, seq=L, tq=tq),
        grid=(B, H // 2),
        in_specs=[
            pl.BlockSpec((1, 2, L, HEAD_PAD), lambda b, p: (b, p, 0, 0)),
            pl.BlockSpec((1, 2, L, HEAD_PAD), lambda b, p: (b, p, 0, 0)),
            pl.BlockSpec((1, L, 2 * V_HEAD), lambda b, p: (b, 0, p)),
        ],
        out_specs=pl.BlockSpec((1, L, 2 * V_HEAD), lambda b, p: (b, 0, p)),
        out_shape=jax.ShapeDtypeStruct((B, L, ATTN_WIDTH), BF16),
        scratch_shapes=[pltpu.VMEM((L, 2 * V_HEAD + LANES), BF16)],
        compiler_params=_params("parallel", "parallel"),
    )(q, k, v)


def _ssm_kernel(u_ref, bbd_ref, lam_ref, cbd_ref, d_ref, wglu_ref, g_ref, o_ref, state_ref, bu_a, bu_b, *, batch, tc):
    @pl.when(pl.program_id(0) == 0)
    def _():
        state_ref[...] = jnp.zeros_like(state_ref)

    half = tc // 2 * batch
    halves = ((bu_a, slice(0, half)), (bu_b, slice(half, 2 * half)))
    for bu_ref, rows in halves:
        bu_ref[...] = jnp.dot(u_ref[rows, :], bbd_ref[...], preferred_element_type=F32)
    lam_re = jnp.broadcast_to(lam_ref[0:1, :], (batch, SSM_FLAT))
    lam_im = jnp.broadcast_to(lam_ref[1:2, :], (batch, SSM_FLAT))

    carry = (state_ref[:, 0:SSM_FLAT], state_ref[:, SSM_FLAT:2 * SSM_FLAT])
    for bu_ref, _ in halves:
        def step(t, xs, bu_ref=bu_ref):
            xr, xi = xs
            at = pl.ds(pl.multiple_of(t * batch, batch), batch)
            nr = lam_re * xr - lam_im * xi + bu_ref[at, 0:SSM_FLAT]
            ni = lam_re * xi + lam_im * xr + bu_ref[at, SSM_FLAT:2 * SSM_FLAT]
            bu_ref[at, 0:SSM_FLAT] = nr
            bu_ref[at, SSM_FLAT:2 * SSM_FLAT] = ni
            return nr, ni

        carry = lax.fori_loop(0, tc // 2, step, carry, unroll=True)
    state_ref[:, 0:SSM_FLAT] = carry[0]
    state_ref[:, SSM_FLAT:2 * SSM_FLAT] = carry[1]

    for bu_ref, rows in halves:
        y = jnp.dot(bu_ref[...].astype(BF16), cbd_ref[...], preferred_element_type=F32)
        y = y + d_ref[...] * u_ref[rows, :].astype(F32)
        z = jax.nn.gelu(y)
        gate = jax.nn.sigmoid(jnp.dot(z.astype(BF16), wglu_ref[...], preferred_element_type=F32))
        o_ref[rows, :] = _rms(z * gate, g_ref[...]).astype(BF16)


def _ssm(u_tm, bbd, lam, cbd, d, wglu, g, batch):
    rows = u_tm.shape[0]
    seq = rows // batch
    tc = min(TC_SSM, seq)
    blk = tc * batch
    full = lambda shape: pl.BlockSpec(shape, lambda t: (0,) * len(shape))
    return pl.pallas_call(
        functools.partial(_ssm_kernel, batch=batch, tc=tc),
        grid=(seq // tc,),
        in_specs=[
            pl.BlockSpec((blk, SSM_WIDTH), lambda t: (t, 0)),
            full((SSM_WIDTH, 2 * SSM_FLAT)), full((2, SSM_FLAT)), full((2 * SSM_FLAT, SSM_WIDTH)),
            full((1, SSM_WIDTH)), full((SSM_WIDTH, SSM_WIDTH)), full((1, SSM_WIDTH)),
        ],
        out_specs=pl.BlockSpec((blk, SSM_WIDTH), lambda t: (t, 0)),
        out_shape=jax.ShapeDtypeStruct((rows, SSM_WIDTH), BF16),
        scratch_shapes=[pltpu.VMEM((batch, 2 * SSM_FLAT), F32), pltpu.VMEM((blk // 2, 2 * SSM_FLAT), F32),
                        pltpu.VMEM((blk // 2, 2 * SSM_FLAT), F32)],
        compiler_params=_params("arbitrary"),
    )(u_tm, bbd, lam, cbd, d, wglu, g)


def _conv_rows(upad_ref, r0, w_ref, b_ref, lnw_ref, lnb_ref, gn_ref):
    base = CONV_HALO - (CONV_K - 1)
    acc = jnp.zeros((CONV_SUB, CONV_WIDTH), F32)
    for off in range(SUBLANES):
        n = CONV_SUB + (SUBLANES if off else 0)
        part = None
        for a8 in range(0, CONV_HALO + 1, SUBLANES):
            kk = a8 + off - base
            if 0 <= kk < CONV_K:
                term = w_ref[kk:kk + 1, :] * upad_ref[r0 + a8:r0 + a8 + n, :]
                part = term if part is None else part + term
        acc = acc + part[off:off + CONV_SUB]
    y = acc + b_ref[...]
    mu = jnp.mean(y, axis=-1, keepdims=True)
    var = jnp.mean(jnp.square(y - mu), axis=-1, keepdims=True)
    y = (y - mu) * lax.rsqrt(var + 1e-5) * lnw_ref[...] + lnb_ref[...]
    y = y * jax.nn.sigmoid(y)
    return _rms(y, gn_ref[...])


ROUTE_E1, ROUTE_E2, ROUTE_R1, ROUTE_R2, ROUTE_G1, ROUTE_G2 = range(6)
ROUTE_ROWS = 8
GRP_LANE0 = N_EXPERTS


def _out_router_kernel(x_ref, ya_ref, ys_ref, ca_ref, cg_ref, ga_ref, wo_ref, fn_ref, wr_ref, br_ref,
                       cw_ref, cb_ref, clnw_ref, clnb_ref, cgn_ref,
                       x1_ref, h2_ref, route_ref, route_t_ref, cnt_ref, sub_ref,
                       carry_ref, tri_ref, upad_ref, yc_ref, *, tm, per_b):
    @pl.when(pl.program_id(0) == 0)
    def _():
        carry_ref[...] = jnp.zeros_like(carry_ref)
        rr = lax.broadcasted_iota(jnp.int32, (tm, tm), 0)
        cc = lax.broadcasted_iota(jnp.int32, (tm, tm), 1)
        tri_ref[...] = (cc < rr).astype(BF16)

    first = pl.program_id(0) % per_b == 0
    upad_ref[0:CONV_HALO, :] = jnp.where(first, 0.0, upad_ref[tm:tm + CONV_HALO, :])
    upad_ref[CONV_HALO:, :] = ca_ref[...].astype(F32) * jax.nn.sigmoid(cg_ref[...].astype(F32))
    ya = _rms(ya_ref[...].astype(F32), ga_ref[...]).astype(BF16)
    acc = jnp.dot(ya, wo_ref[0:ATTN_WIDTH, :], preferred_element_type=F32)
    acc += jnp.dot(ys_ref[...], wo_ref[ATTN_WIDTH:ATTN_WIDTH + SSM_WIDTH, :], preferred_element_type=F32)
    for r0 in range(0, tm, CONV_SUB):
        yc_ref[r0:r0 + CONV_SUB, :] = _conv_rows(upad_ref, r0, cw_ref, cb_ref, clnw_ref, clnb_ref, cgn_ref).astype(BF16)
    acc += jnp.dot(yc_ref[...], wo_ref[ATTN_WIDTH + SSM_WIDTH:, :], preferred_element_type=F32)
    x1 = x_ref[...] + acc
    x1_ref[...] = x1
    h2 = _rms(x1, fn_ref[...])
    h2_ref[...] = _pack_rows(h2)

    h_hi = h2.astype(BF16)
    h_lo = (h2 - h_hi.astype(F32)).astype(BF16)
    part = jnp.dot(h_hi, wr_ref[...], preferred_element_type=F32)
    logits = (part[:, 0:LANES] + part[:, LANES:2 * LANES]
              + jnp.dot(h_lo, wr_ref[:, 0:LANES], preferred_element_type=F32) + br_ref[...])
    lane = lax.broadcasted_iota(jnp.int32, (tm, LANES), 1)
    ninf = -jnp.inf
    big = LANES

    def first_argmax(vals, vmax):
        return jnp.min(jnp.where(vals == vmax, lane, big), axis=-1, keepdims=True)

    grp = jnp.where((lane >= GRP_LANE0) & (lane < GRP_LANE0 + N_EGROUPS), logits, ninf)
    gmax = jnp.max(grp, axis=-1, keepdims=True)
    gsel = first_argmax(grp, gmax) - GRP_LANE0
    p_grp = 1.0 / jnp.sum(jnp.exp(grp - gmax), axis=-1, keepdims=True)

    el = jnp.where((lane < N_EXPERTS) & ((lane // EXP_PER_GROUP) == gsel), logits, ninf)
    m1 = jnp.max(el, axis=-1, keepdims=True)
    e1 = first_argmax(el, m1)
    el2 = jnp.where(lane == e1, ninf, el)
    m2 = jnp.max(el2, axis=-1, keepdims=True)
    e2 = first_argmax(el2, m2)
    t2 = jnp.exp(m2 - m1)
    g1 = p_grp / (1.0 + t2)
    g2 = p_grp * t2 / (1.0 + t2)

    hit1 = lane == e1
    hit2 = lane == e2
    cnt = (hit1 | hit2).astype(F32)
    before = jnp.dot(tri_ref[...], cnt.astype(BF16), preferred_element_type=F32) + carry_ref[...]
    r1 = jnp.sum(jnp.where(hit1, before, 0.0), axis=-1, keepdims=True)
    r2 = jnp.sum(jnp.where(hit2, before, 0.0), axis=-1, keepdims=True)
    subs = [before[h * TM_MOVE:h * TM_MOVE + 1, :] for h in range(tm // TM_MOVE)]
    sub_ref[...] = jnp.concatenate(subs + [jnp.zeros((SUBLANES - len(subs), LANES), F32)], axis=0)
    carry_ref[...] += jnp.sum(cnt, axis=0, keepdims=True)
    cnt_ref[...] = carry_ref[...]

    rec = jnp.where(lane == ROUTE_E1, e1.astype(F32), 0.0)
    rec = jnp.where(lane == ROUTE_E2, e2.astype(F32), rec)
    rec = jnp.where(lane == ROUTE_R1, r1, rec)
    rec = jnp.where(lane == ROUTE_R2, r2, rec)
    rec = jnp.where(lane == ROUTE_G1, g1, rec)
    rec = jnp.where(lane == ROUTE_G2, g2, rec)
    route_ref[...] = rec
    route_t_ref[...] = rec.T[0:ROUTE_ROWS, :]


def _out_router(x, ya, ys_tm, ca, cg, ga, wo, fn, wr, br, conv_params, batch):
    N, D = x.shape
    seq = N // batch
    tm = min(TM_OUT, seq)
    per_b = seq // tm
    full = lambda shape: pl.BlockSpec(shape, lambda i: (0,) * len(shape))
    tile = lambda w: pl.BlockSpec((tm, w), lambda i: (i, 0))
    return pl.pallas_call(
        functools.partial(_out_router_kernel, tm=tm, per_b=per_b),
        grid=(N // tm,),
        in_specs=[
            tile(D), tile(ATTN_WIDTH),
            pl.BlockSpec((tm, SSM_WIDTH), lambda i: (i % per_b, i // per_b)),
            tile(CONV_WIDTH), tile(CONV_WIDTH),
            full((1, ATTN_WIDTH)), full((D, D)), full((1, D)), full((D, 2 * LANES)), full((1, LANES)),
            full((CONV_K, CONV_WIDTH)), full((1, CONV_WIDTH)), full((1, CONV_WIDTH)), full((1, CONV_WIDTH)),
            full((1, CONV_WIDTH)),
        ],
        out_specs=[tile(D), tile(D_PACK), tile(LANES), pl.BlockSpec((ROUTE_ROWS, tm), lambda i: (0, i)),
                   full((1, LANES)), pl.BlockSpec((SUBLANES, LANES), lambda i: (i, 0))],
        out_shape=[
            jax.ShapeDtypeStruct((N, D), F32),
            jax.ShapeDtypeStruct((N, D_PACK), U32),
            jax.ShapeDtypeStruct((N, LANES), F32),
            jax.ShapeDtypeStruct((ROUTE_ROWS, N), F32),
            jax.ShapeDtypeStruct((1, LANES), F32),
            jax.ShapeDtypeStruct((N // tm * SUBLANES, LANES), F32),
        ],
        scratch_shapes=[pltpu.VMEM((1, LANES), F32), pltpu.VMEM((tm, tm), BF16),
                        pltpu.VMEM((tm + CONV_HALO, CONV_WIDTH), F32), pltpu.VMEM((tm, CONV_WIDTH), BF16)],
        compiler_params=_params("arbitrary"),
    )(x, ya, ys_tm, ca, cg, ga, wo, fn, wr, br, *conv_params)


def _dispatch(dest_flat, h2, n_rows):
    N, D = h2.shape
    workers = SC_CORES * SC_SUBCORES
    per_worker = N // workers
    chunks = per_worker // SC_CHUNK
    idx = dest_flat.reshape(2, workers, chunks, SC_CHUNK).transpose(1, 0, 2, 3).reshape(workers, 2 * chunks, SC_CHUNK)
    mesh = plsc.VectorSubcoreMesh(core_axis_name="c", subcore_axis_name="s")

    @functools.partial(
        pl.kernel, mesh=mesh,
        out_type=jax.ShapeDtypeStruct((n_rows, D), h2.dtype),
        scratch_types=[pltpu.VMEM((2 * chunks, SC_CHUNK), jnp.int32), pltpu.VMEM((2, SC_CHUNK, D), h2.dtype),
                       pltpu.SemaphoreType.DMA((2,)), pltpu.SemaphoreType.DMA((2, 2))],
    )
    def scatter(h_hbm, idx_hbm, out_hbm, idx_v, rows_v, sem_in, sem_out):
        wid = lax.axis_index("s") * SC_CORES + lax.axis_index("c")
        pltpu.sync_copy(idx_hbm.at[wid], idx_v)

        def load(c):
            first = pl.multiple_of(wid * per_worker + c * SC_CHUNK, SC_CHUNK)
            return pltpu.async_copy(h_hbm.at[pl.ds(first, SC_CHUNK)], rows_v.at[c % 2], sem_in.at[c % 2])

        def store(c, k):
            return pltpu.async_copy(rows_v.at[c % 2], out_hbm.at[idx_v.at[k * chunks + c]], sem_out.at[c % 2, k])

        loads = {0: load(0)}
        stores = {}
        for c in range(chunks):
            loads[c].wait()
            if c >= 1:
                for st in stores[c - 1]:
                    st.wait()
            if c + 1 < chunks:
                loads[c + 1] = load(c + 1)
            stores[c] = (store(c, 0), store(c, 1))
        for st in stores[chunks - 1]:
            st.wait()

    return scatter(h2, idx)


def _experts_kernel(blk_ref, exp_ref, valid_ref, slot_ref, next_ref, used_ref, x_ref, w1_hbm, w3_hbm, w2_hbm, y_ref,
                    w1_f, w3_f, w2_f, w13_s, w2_s, sem, *, layer):
    j = pl.program_id(0)

    def weight_copies(e, slot):
        return (pltpu.make_async_copy(w1_hbm.at[layer, e], w1_f.at[slot], sem.at[slot, 0]),
                pltpu.make_async_copy(w3_hbm.at[layer, e], w3_f.at[slot], sem.at[slot, 1]),
                pltpu.make_async_copy(w2_hbm.at[layer, e], w2_f.at[slot], sem.at[slot, 2]))

    @pl.when(j < used_ref[0])
    def _():
        e = exp_ref[j]
        slot = slot_ref[j]

        @pl.when(j == 0)
        def _():
            for cp in weight_copies(e, slot):
                cp.start()

        @pl.when((j == 0) | (e != exp_ref[jnp.maximum(j - 1, 0)]))
        def _():
            for cp in weight_copies(e, slot):
                cp.wait()
            for c in range(D_FF_E // FF_CHUNK):
                w13_s[:, 2 * FF_CHUNK * c:2 * FF_CHUNK * c + FF_CHUNK] = \
                    w1_f[slot, :, FF_CHUNK * c:FF_CHUNK * (c + 1)].astype(BF16)
                w13_s[:, 2 * FF_CHUNK * c + FF_CHUNK:2 * FF_CHUNK * (c + 1)] = \
                    w3_f[slot, :, FF_CHUNK * c:FF_CHUNK * (c + 1)].astype(BF16)
            w2_s[...] = w2_f[slot].astype(BF16)

            @pl.when(next_ref[j] >= 0)
            def _():
                for cp in weight_copies(next_ref[j], 1 - slot):
                    cp.start()

        x = _unpack_rows(x_ref[...])
        y = None
        for c in range(D_FF_E // FF_CHUNK):
            ab = jnp.dot(x, w13_s[:, 2 * FF_CHUNK * c:2 * FF_CHUNK * (c + 1)], preferred_element_type=F32)
            a = ab[:, 0:FF_CHUNK]
            hmid = (a * jax.nn.sigmoid(a) * ab[:, FF_CHUNK:]).astype(BF16)
            part = jnp.dot(hmid, w2_s[FF_CHUNK * c:FF_CHUNK * (c + 1), :], preferred_element_type=F32)
            y = part if y is None else y + part
        row = lax.broadcasted_iota(jnp.int32, (TB_EXP, 1), 0)
        y_ref[...] = jnp.where(row < valid_ref[j], _pack_rows(y), jnp.uint32(0))


def _experts(blk_map, blk_exp, blk_valid, blk_slot, blk_next, n_used, x_rows, w1, w3, w2, layer):
    R, D = x_rows.shape[0], D_MODEL
    nb = R // TB_EXP
    row_block = lambda j, bm, be, bv, bs, bn, nu: (bm[j], 0)
    return pl.pallas_call(
        functools.partial(_experts_kernel, layer=layer),
        grid_spec=pltpu.PrefetchScalarGridSpec(
            num_scalar_prefetch=6,
            grid=(nb,),
            in_specs=[
                pl.BlockSpec((TB_EXP, D_PACK), row_block),
                pl.BlockSpec(memory_space=pl.ANY), pl.BlockSpec(memory_space=pl.ANY), pl.BlockSpec(memory_space=pl.ANY),
            ],
            out_specs=pl.BlockSpec((TB_EXP, D_PACK), row_block),
            scratch_shapes=[
                pltpu.VMEM((2, D, D_FF_E), F32), pltpu.VMEM((2, D, D_FF_E), F32), pltpu.VMEM((2, D_FF_E, D), F32),
                pltpu.VMEM((D, 2 * D_FF_E), BF16), pltpu.VMEM((D_FF_E, D), BF16),
                pltpu.SemaphoreType.DMA((2, 3)),
            ],
        ),
        out_shape=jax.ShapeDtypeStruct((R, D_PACK), U32),
        compiler_params=_params("arbitrary"),
    )(blk_map, blk_exp, blk_valid, blk_slot, blk_next, n_used, x_rows, w1, w3, w2)


def _combine_kernel(src_ref, nslot_ref, x1_ref, route_ref, off_ref, rows_ref, o_ref, buf, sem, *, tm):
    i = pl.program_id(0)
    cur = i % 2

    def slot_copy(tile, half, s):
        src = pl.multiple_of(src_ref[tile * RUN_SLOTS + s], RUN_ROWS)
        dst = buf.at[half, pl.ds(pl.multiple_of(s * RUN_ROWS, RUN_ROWS), RUN_ROWS)]
        return pltpu.make_async_copy(rows_ref.at[pl.ds(src, RUN_ROWS)], dst, sem.at[half])

    def fetch(tile, half):
        def issue(g, c):
            for u in range(RUN_GROUP):
                slot_copy(tile, half, g * RUN_GROUP + u).start()
            return c
        lax.fori_loop(0, nslot_ref[tile], issue, 0)

    def wait(g, c):
        rows = RUN_GROUP * RUN_ROWS
        pltpu.make_async_copy(rows_ref.at[pl.ds(0, rows)], buf.at[cur, pl.ds(0, rows)], sem.at[cur]).wait()
        return c

    @pl.when(i == 0)
    def _():
        buf[...] = jnp.zeros_like(buf)
        fetch(0, 0)

    @pl.when(i + 1 < pl.num_programs(0))
    def _():
        fetch(i + 1, 1 - cur)

    route = route_ref[...]
    lane = lax.broadcasted_iota(jnp.int32, (tm, LANES), 1).astype(F32)
    off = off_ref[0]

    def buf_row(e_lane, r_lane):
        e = route[:, e_lane:e_lane + 1]
        base = jnp.sum(jnp.where(lane == e, off, 0.0), axis=-1, keepdims=True)
        return (base + route[:, r_lane:r_lane + 1]).astype(jnp.int32)

    col = lax.broadcasted_iota(jnp.int32, (tm, RUN_SLOTS * RUN_ROWS), 1)
    pick = (jnp.where(col == buf_row(ROUTE_E1, ROUTE_R1), route[:, ROUTE_G1:ROUTE_G1 + 1], 0.0)
            + jnp.where(col == buf_row(ROUTE_E2, ROUTE_R2), route[:, ROUTE_G2:ROUTE_G2 + 1], 0.0))
    lax.fori_loop(0, nslot_ref[i], wait, 0)
    o_ref[...] = x1_ref[...] + jnp.dot(pick.astype(BF16), _unpack_rows(buf[cur]), preferred_element_type=F32)


def _combine(src, nslot, x1, route, off, y_rows):
    N, D = x1.shape
    tm = min(TM_MOVE, N)
    return pl.pallas_call(
        functools.partial(_combine_kernel, tm=tm),
        grid_spec=pltpu.PrefetchScalarGridSpec(
            num_scalar_prefetch=2,
            grid=(N // tm,),
            in_specs=[
                pl.BlockSpec((tm, D), lambda i, s, n: (i, 0)),
                pl.BlockSpec((tm, LANES), lambda i, s, n: (i, 0)),
                pl.BlockSpec((1, 1, LANES), lambda i, s, n: (i, 0, 0)),
                pl.BlockSpec(memory_space=pl.ANY),
            ],
            out_specs=pl.BlockSpec((tm, D), lambda i, s, n: (i, 0)),
            scratch_shapes=[pltpu.VMEM((2, RUN_SLOTS * RUN_ROWS, D_PACK), U32), pltpu.SemaphoreType.DMA((2,))],
        ),
        out_shape=jax.ShapeDtypeStruct((N, D), F32),
        compiler_params=_params("arbitrary"),
    )(src, nslot, x1, route, off, y_rows)


def _moe(x1, h2, route, route_t, counts, sub_carry, w1, w3, w2, layer):
    N, _ = x1.shape
    tm = min(TM_MOVE, N)
    n_tiles = N // tm
    experts = jnp.arange(N_EXPERTS, dtype=jnp.int32)
    nb = (2 * N + N_EXPERTS * (TB_EXP - 1)) // TB_EXP + 1
    e_id = route_t[ROUTE_E1:ROUTE_E2 + 1].astype(jnp.int32)
    rank = route_t[ROUTE_R1:ROUTE_R2 + 1].astype(jnp.int32)
    cnt = counts[0, :N_EXPERTS].astype(jnp.int32)
    padded = (cnt + TB_EXP - 1) // TB_EXP * TB_EXP
    pad_end = jnp.cumsum(padded)
    pad_start = pad_end - padded
    e_flat = e_id.reshape(1, 2 * N)
    dest = jnp.sum(jnp.where(e_flat == experts[:, None], pad_start[:, None], 0), axis=0) + rank.reshape(2 * N)
    n_used = pad_end[-1] // TB_EXP
    blk_map = jnp.minimum(jnp.arange(nb, dtype=jnp.int32), jnp.maximum(n_used - 1, 0))
    blk_exp = jnp.sum(pad_end[None, :] <= (blk_map * TB_EXP)[:, None], axis=1).astype(jnp.int32)
    blk_exp = jnp.minimum(blk_exp, N_EXPERTS - 1)
    is_blk_e = blk_exp[:, None] == experts[None, :]
    blk_valid = jnp.clip(jnp.sum(jnp.where(is_blk_e, (pad_start + cnt)[None, :], 0), axis=1) - blk_map * TB_EXP,
                         0, TB_EXP).astype(jnp.int32)
    has_rows = cnt > 0
    slot_of_e = (jnp.cumsum(has_rows.astype(jnp.int32)) - 1) % 2
    later = jnp.where(has_rows, experts, N_EXPERTS)
    next_of_e = lax.cummin(jnp.concatenate([later[1:], jnp.full((1,), N_EXPERTS, jnp.int32)]), reverse=True)
    next_of_e = jnp.where(next_of_e < N_EXPERTS, next_of_e, -1)
    blk_slot = jnp.sum(jnp.where(is_blk_e, slot_of_e[None, :], 0), axis=1).astype(jnp.int32)
    blk_next = jnp.sum(jnp.where(is_blk_e, next_of_e[None, :], 0), axis=1).astype(jnp.int32)

    per_router_tile = TM_OUT // tm if N >= TM_OUT else 1
    carry = sub_carry.reshape(-1, SUBLANES, LANES)[:, :per_router_tile, :N_EXPERTS].reshape(n_tiles, N_EXPERTS)
    carry = carry.astype(jnp.int32)
    tile_cnt = jnp.concatenate([carry[1:], cnt[None, :]], axis=0) - carry
    run_start = pad_start[None, :] + carry
    first_blk = run_start // RUN_ROWS
    nslot = jnp.where(tile_cnt > 0, (run_start + tile_cnt - 1) // RUN_ROWS - first_blk + 1, 0)
    slot_end = jnp.cumsum(nslot, axis=1)
    slot_base = slot_end - nslot
    slots = jnp.arange(RUN_SLOTS, dtype=jnp.int32)
    slot_e = jnp.minimum(jnp.sum(slot_end[:, None, :] <= slots[None, :, None], axis=2), N_EXPERTS - 1)
    is_slot_e = slot_e[:, :, None] == experts[None, None, :]
    pick = lambda tbl: jnp.sum(jnp.where(is_slot_e, tbl[:, None, :], 0), axis=2)
    src = (pick(first_blk) + slots[None, :] - pick(slot_base)) * RUN_ROWS
    src = jnp.where(slots[None, :] < slot_end[:, -1:], src, 0)
    src = jnp.clip(src, 0, nb * TB_EXP - RUN_ROWS).reshape(-1).astype(jnp.int32)
    n_groups = (slot_end[:, -1] + RUN_GROUP - 1) // RUN_GROUP
    off = (slot_base - first_blk) * RUN_ROWS + pad_start[None, :]
    off = jnp.pad(off.astype(F32), ((0, 0), (0, LANES - N_EXPERTS)))

    x_rows = _dispatch(dest, h2, nb * TB_EXP)
    y_rows = _experts(blk_map, blk_exp, blk_valid, blk_slot, blk_next, n_used.reshape(1).astype(jnp.int32),
                      x_rows, w1, w3, w2, layer)
    return _combine(src, n_groups.astype(jnp.int32), x1, route, off.reshape(n_tiles, 1, LANES), y_rows)


def _pad_heads(w, width):
    k = w.shape[0]
    w = w.reshape(k, N_HEADS, width)
    return jnp.pad(w, ((0, 0), (0, 0), (0, HEAD_PAD - width))).reshape(k, N_HEADS * HEAD_PAD)


def _swap_rope(w):
    half = QK_ROPE // 2
    lo, hi = w[..., QK_NOPE:QK_NOPE + half], w[..., QK_NOPE + half:QK_HEAD]
    pad = [(0, 0)] * (w.ndim - 1)
    return jnp.pad(jnp.concatenate([hi, lo], -1), pad + [(QK_NOPE, HEAD_PAD - QK_HEAD)])


def _rope_tables(positions):
    inv_freq = ROPE_THETA ** (-jnp.arange(0, QK_ROPE, 2, dtype=F32) / QK_ROPE)
    ang = positions.astype(F32)[..., None] * inv_freq
    table = jnp.concatenate([jnp.cos(ang), jnp.sin(ang)], -1)
    return jnp.pad(table, ((0, 0), (0, 0), (QK_NOPE, LANES - QK_HEAD)))


def _ssm_params(lam_re, lam_im, b_re, b_im, c_re, c_im, log_dt):
    lam = lax.complex(lam_re, lam_im)
    dt = jnp.exp(log_dt)[:, None]
    lam_bar = jnp.exp(lam * dt)
    b_bar = ((lam_bar - 1.0) / lam)[..., None] * lax.complex(b_re, b_im)
    eye = jnp.eye(SSM_GROUPS, dtype=F32)

    def in_blockdiag(m):
        return jnp.einsum("gpc,gh->gchp", m, eye).reshape(SSM_WIDTH, SSM_FLAT)

    def out_blockdiag(m):
        return jnp.einsum("gcp,gh->gphc", m, eye).reshape(SSM_FLAT, SSM_WIDTH)

    bbd = jnp.concatenate([in_blockdiag(jnp.real(b_bar)), in_blockdiag(jnp.imag(b_bar))], axis=1)
    cbd = jnp.concatenate([out_blockdiag(c_re), out_blockdiag(-c_im)], axis=0)
    lam_rows = jnp.stack([jnp.real(lam_bar).reshape(SSM_FLAT), jnp.imag(lam_bar).reshape(SSM_FLAT)])
    return bbd.astype(BF16), lam_rows, cbd.astype(BF16)


def kernel(x, positions, mix_norm, w_in, q_a_norm, w_uq, kv_a_norm, w_ukv, q_norm, k_norm, ssm_lam_re, ssm_lam_im, ssm_b_re, ssm_b_im, ssm_c_re, ssm_c_im, ssm_d, ssm_log_dt, ssm_w_glu, conv_dw_w, conv_dw_b, conv_ln_w, conv_ln_b, out_norm, w_out, ffn_norm, w_grp, b_grp, w_exp, b_exp, w1, w3, w2):
    B, L, D = x.shape
    depth = w_in.shape[0]
    rope = _rope_tables(positions)
    row = lambda v: v.reshape(1, -1)
    lane_pad = lambda v: jnp.pad(v, (0, LANES - v.shape[0])).reshape(1, LANES)
    for l in range(depth):
        c_q, c_kv, k_pe, u_s, c_a, c_g = jnp.split(
            w_in[l], [Q_LORA, Q_LORA + KV_LORA, Q_LORA + KV_LORA + QK_ROPE,
                      Q_LORA + KV_LORA + QK_ROPE + SSM_WIDTH,
                      Q_LORA + KV_LORA + QK_ROPE + SSM_WIDTH + CONV_WIDTH], axis=1)
        k_pe_full = jnp.pad(k_pe, ((0, 0), (QK_NOPE, 0)))
        win = jnp.concatenate([c_q, c_kv, jnp.pad(k_pe_full, ((0, 0), (0, HEAD_PAD - QK_HEAD))), _swap_rope(k_pe_full),
                               u_s, c_a, c_g], axis=1).astype(BF16)
        wkv = w_ukv[l].reshape(KV_LORA, N_HEADS, QK_NOPE + V_HEAD)
        wuk = _pad_heads(wkv[:, :, :QK_NOPE].reshape(KV_LORA, N_HEADS * QK_NOPE), QK_NOPE).astype(BF16)
        wuv = wkv[:, :, QK_NOPE:].reshape(KV_LORA, ATTN_WIDTH).astype(BF16)
        wuq_sw = _swap_rope(w_uq[l].reshape(Q_LORA, N_HEADS, QK_HEAD)).reshape(Q_LORA, N_HEADS * HEAD_PAD)
        wuq = jnp.concatenate([_pad_heads(w_uq[l], QK_HEAD), wuq_sw], axis=1).astype(BF16)
        qn_scaled = q_norm[l] * (QK_HEAD ** -0.5)
        norm_rows = lambda w: jnp.stack([jnp.pad(w, (0, HEAD_PAD - QK_HEAD)), _swap_rope(w)])
        q, k, v, u_tm, ca, cg = _in_proj(
            x, rope, row(mix_norm[l]), win, row(q_a_norm[l]), wuq, row(kv_a_norm[l]), wuk, wuv,
            norm_rows(qn_scaled), norm_rows(k_norm[l]))
        y_attn = _attention(q, k, v)

        g_out = out_norm[l]
        bbd, lam_rows, cbd = _ssm_params(ssm_lam_re[l], ssm_lam_im[l], ssm_b_re[l], ssm_b_im[l],
                                         ssm_c_re[l], ssm_c_im[l], ssm_log_dt[l])
        y_ssm_tm = _ssm(u_tm.reshape(L * B, SSM_WIDTH), bbd, lam_rows, cbd, row(ssm_d[l]),
                        ssm_w_glu[l].astype(BF16), row(g_out[ATTN_WIDTH:ATTN_WIDTH + SSM_WIDTH]), B)
        conv_params = (conv_dw_w[l], row(conv_dw_b[l]), row(conv_ln_w[l]), row(conv_ln_b[l]),
                       row(g_out[ATTN_WIDTH + SSM_WIDTH:]))

        w_route = jnp.pad(jnp.concatenate([w_exp[l], w_grp[l]], axis=1), ((0, 0), (0, LANES - N_EXPERTS - N_EGROUPS)))
        w_route_hi = w_route.astype(BF16)
        w_route = jnp.concatenate([w_route_hi, (w_route - w_route_hi.astype(F32)).astype(BF16)], axis=1)
        b_route = lane_pad(jnp.concatenate([b_exp[l], b_grp[l]]))
        x1, h2, route, route_t, counts, sub_carry = _out_router(
            x.reshape(B * L, D), y_attn.reshape(B * L, ATTN_WIDTH), y_ssm_tm.reshape(L, B * SSM_WIDTH),
            ca.reshape(B * L, CONV_WIDTH), cg.reshape(B * L, CONV_WIDTH), row(g_out[:ATTN_WIDTH]),
            w_out[l].astype(BF16), row(ffn_norm[l]), w_route, b_route, conv_params, B)
        x = _moe(x1, h2, route, route_t, counts, sub_carry, w1, w3, w2, l).reshape(B, L, D)
    return x
```

```python
import functools
import math

import jax
import jax.numpy as jnp
from jax import lax
from jax.experimental import pallas as pl
from jax.experimental.pallas import tpu as pltpu
from jax.experimental.pallas import tpu_sc as plsc

D_MODEL = 1024
CHUNK = 64
EPS = 1e-6
N_HEADS = 8
QK_NOPE = 64
QK_ROPE = 32
QK_HEAD = QK_NOPE + QK_ROPE
V_HEAD = 64
Q_LORA = 256
KV_LORA = 128
ROPE_THETA = 10000.0
ATTN_WIDTH = N_HEADS * V_HEAD
SSM_WIDTH = 256
SSM_GROUP = 16
SSM_GROUPS = SSM_WIDTH // SSM_GROUP
SSM_STATE = 64
SSM_FLAT = SSM_GROUPS * SSM_STATE
CONV_WIDTH = 256
CONV_K = 31
N_EGROUPS = 4
EXP_PER_GROUP = 8
N_EXPERTS = N_EGROUPS * EXP_PER_GROUP
D_FF_E = 512

LANES = 128
SUBLANES = 8
HEAD_PAD = LANES
IN_PROJ_PAD = Q_LORA + KV_LORA + SSM_WIDTH + 2 * CONV_WIDTH + 2 * LANES
CONV_HALO = 32
VMEM_LIMIT = 48 * 1024 * 1024

TM_PROJ = 512
TQ_ATTN = 256
TC_SSM = 128
SSM_PARTS = 2
CONV_SUB = 64
TM_OUT = 512
TM_MOVE = 256
RUN_ROWS = SUBLANES
RUN_SLOTS = 2 * N_EXPERTS + 2 * TM_MOVE // RUN_ROWS
RUN_GROUP = 8
SC_CORES, SC_SUBCORES = 2, 16
SC_CHUNK = 64
TB_EXP = 512
FF_CHUNK = 256

BF16 = jnp.bfloat16
F32 = jnp.float32
U32 = jnp.uint32
D_PACK = D_MODEL // 2


def _pack_rows(v):
    bits = lax.bitcast_convert_type(v.astype(BF16).astype(F32), U32)
    half = v.shape[1] // 2
    return bits[:, 0:half] | (bits[:, half:] >> 16)


def _unpack_rows(w):
    hi = lax.bitcast_convert_type(w & jnp.uint32(0xFFFF0000), F32)
    lo = lax.bitcast_convert_type(w << 16, F32)
    return jnp.concatenate([hi, lo], axis=1).astype(BF16)


def _rms(x, w):
    return x * lax.rsqrt(jnp.mean(x * x, axis=-1, keepdims=True) + EPS) * w


def _params(*sem):
    return pltpu.CompilerParams(dimension_semantics=sem, vmem_limit_bytes=VMEM_LIMIT)


def _in_proj_kernel(x_ref, rope_ref, mixn_ref, win_ref, qan_ref, wuq_ref, kvan_ref, wuk_ref, wuv_ref,
                    qn_ref, kn_ref, q_ref, k_ref, v_ref, u_ref, ca_ref, cg_ref):
    x = x_ref[0]
    h = _rms(x, mixn_ref[...]).astype(BF16)
    head_cols = Q_LORA + KV_LORA + 2 * LANES
    proj = jnp.dot(h, win_ref[:, 0:head_cols], preferred_element_type=F32)
    o = 0
    c_q = proj[:, o:o + Q_LORA]; o += Q_LORA
    c_kv = proj[:, o:o + KV_LORA]; o += KV_LORA
    k_pe = proj[:, o:o + LANES]; o += LANES
    k_pe_sw = proj[:, o:o + LANES]

    width = N_HEADS * HEAD_PAD
    q2 = jnp.dot(_rms(c_q, qan_ref[...]).astype(BF16), wuq_ref[...], preferred_element_type=F32)
    q, q_sw = q2[:, 0:width], q2[:, width:2 * width]
    ckv_n = _rms(c_kv, kvan_ref[...]).astype(BF16)
    kn = jnp.dot(ckv_n, wuk_ref[...], preferred_element_type=F32)
    v_ref[0] = jnp.dot(ckv_n, wuv_ref[...], preferred_element_type=F32).astype(BF16)

    rope = rope_ref[0]
    half = QK_ROPE // 2
    lane = lax.broadcasted_iota(jnp.int32, rope.shape, 1)
    lo = (lane >= QK_NOPE) & (lane < QK_NOPE + half)
    hi = (lane >= QK_NOPE + half) & (lane < QK_HEAD)
    cos_t = jnp.where(lane < QK_NOPE, 1.0, jnp.where(lo, rope, jnp.where(hi, pltpu.roll(rope, half, 1), 0.0)))
    sin_t = jnp.where(lo, -pltpu.roll(rope, LANES - half, 1), jnp.where(hi, rope, 0.0))
    a_q, b_q = qn_ref[0:1, :] * cos_t, qn_ref[1:2, :] * sin_t
    a_k = kn_ref[0:1, :] * cos_t
    k_sw_term = k_pe_sw * (kn_ref[1:2, :] * sin_t)

    def inv_rms(y):
        return lax.rsqrt(jnp.sum(y * y, axis=-1, keepdims=True) * (1.0 / QK_HEAD) + EPS)

    def rest(j):
        cols = slice(head_cols + j * SSM_WIDTH, head_cols + (j + 1) * SSM_WIDTH)
        return jnp.dot(h, win_ref[:, cols], preferred_element_type=F32).astype(BF16)

    for hd in range(N_HEADS):
        if hd == 0:
            u_ref[...] = rest(0)
        elif hd == 3:
            ca_ref[0] = rest(1)
        elif hd == 6:
            cg_ref[0] = rest(2)
        sl = slice(hd * HEAD_PAD, (hd + 1) * HEAD_PAD)
        yq = q[:, sl]
        q_ref[0, hd] = ((yq * a_q + q_sw[:, sl] * b_q) * inv_rms(yq)).astype(BF16)
        yk = kn[:, sl] + k_pe
        k_ref[0, hd] = ((yk * a_k + k_sw_term) * inv_rms(yk)).astype(BF16)


def _in_proj(x, rope, mixn, win, qan, wuq, kvan, wuk, wuv, qn, kn):
    B, L, D = x.shape
    tm = min(TM_PROJ, L)
    full = lambda shape: pl.BlockSpec(shape, lambda b, t: (0,) * len(shape))
    return pl.pallas_call(
        _in_proj_kernel,
        grid=(B, L // tm),
        in_specs=[
            pl.BlockSpec((1, tm, D), lambda b, t: (b, t, 0)),
            pl.BlockSpec((1, tm, LANES), lambda b, t: (b, t, 0)),
            full((1, D)), full((D, IN_PROJ_PAD)), full((1, Q_LORA)), full((Q_LORA, 2 * N_HEADS * HEAD_PAD)),
            full((1, KV_LORA)), full((KV_LORA, N_HEADS * HEAD_PAD)), full((KV_LORA, ATTN_WIDTH)),
            full((2, HEAD_PAD)), full((2, HEAD_PAD)),
        ],
        out_specs=[
            pl.BlockSpec((1, N_HEADS, tm, HEAD_PAD), lambda b, t: (b, 0, t, 0)),
            pl.BlockSpec((1, N_HEADS, tm, HEAD_PAD), lambda b, t: (b, 0, t, 0)),
            pl.BlockSpec((1, tm, ATTN_WIDTH), lambda b, t: (b, t, 0)),
            pl.BlockSpec((tm, SSM_WIDTH), lambda b, t: (t, b)),
            pl.BlockSpec((1, tm, CONV_WIDTH), lambda b, t: (b, t, 0)),
            pl.BlockSpec((1, tm, CONV_WIDTH), lambda b, t: (b, t, 0)),
        ],
        out_shape=[
            jax.ShapeDtypeStruct((B, N_HEADS, L, HEAD_PAD), BF16),
            jax.ShapeDtypeStruct((B, N_HEADS, L, HEAD_PAD), BF16),
            jax.ShapeDtypeStruct((B, L, ATTN_WIDTH), BF16),
            jax.ShapeDtypeStruct((L, B * SSM_WIDTH), BF16),
            jax.ShapeDtypeStruct((B, L, CONV_WIDTH), BF16),
            jax.ShapeDtypeStruct((B, L, CONV_WIDTH), BF16),
        ],
        compiler_params=_params("parallel", "parallel"),
    )(x, rope, mixn, win, qan, wuq, kvan, wuk, wuv, qn, kn)


def _attention_kernel(q_ref, k_ref, v_ref, o_ref, vext_ref, *, seq, tq):
    pair = 2 * V_HEAD
    vext_ref[:, 0:pair] = v_ref[0]
    vext_ref[:, pair:] = jnp.ones((seq, LANES), BF16)
    row_chunk = lax.broadcasted_iota(jnp.int32, (tq, tq), 0) // CHUNK
    col_chunk = lax.broadcasted_iota(jnp.int32, (tq, tq), 1) // CHUNK
    visible = col_chunk <= row_chunk
    neg = jnp.finfo(F32).min
    nt = (((1,), (1,)), ((), ()))
    low_lanes = lax.broadcasted_iota(jnp.int32, (tq, pair), 1) < V_HEAD
    def scores(i, hh):
        q0 = i * tq
        qb = q_ref[0, hh, q0:q0 + tq, :]
        s_d = lax.dot_general(qb, k_ref[0, hh, q0:q0 + tq, :], nt, preferred_element_type=F32)
        s_d = jnp.where(visible, s_d, neg)
        s_l = lax.dot_general(qb, k_ref[0, hh, 0:q0, :], nt, preferred_element_type=F32) if i > 0 else None
        return s_d, s_l

    def weighted_values(i, s_d, s_l):
        q0 = i * tq
        m = jnp.max(s_d, axis=-1, keepdims=True)
        if s_l is not None:
            m = jnp.maximum(m, jnp.max(s_l, axis=-1, keepdims=True))
        acc = jnp.dot(jnp.exp(s_d - m).astype(BF16), vext_ref[q0:q0 + tq, :], preferred_element_type=F32)
        if s_l is not None:
            acc = acc + jnp.dot(jnp.exp(s_l - m).astype(BF16), vext_ref[0:q0, :], preferred_element_type=F32)
        return acc[:, 0:pair] / acc[:, pair:]

    nq = seq // tq
    order = [x for p in zip(reversed(range(nq)), range(nq)) for x in p][:nq]
    chains = [(i, hh) for i in order for hh in range(2)]
    pending = scores(*chains[0])
    outs = {}
    for n, (i, hh) in enumerate(chains):
        upcoming = scores(*chains[n + 1]) if n + 1 < len(chains) else None
        outs[hh] = weighted_values(i, *pending)
        pending = upcoming
        if hh == 1:
            o_ref[0, i * tq:(i + 1) * tq, :] = jnp.where(low_lanes, outs[0], outs[1]).astype(BF16)


def _attention(q, k, v):
    B, H, L, _ = q.shape
    tq = min(TQ_ATTN, L)
    return pl.pallas_call(
        functools.partial(_attention_kernel, seq=L, tq=tq),
        grid=(B, H // 2),
        in_specs=[
            pl.BlockSpec((1, 2, L, HEAD_PAD), lambda b, p: (b, p, 0, 0)),
            pl.BlockSpec((1, 2, L, HEAD_PAD), lambda b, p: (b, p, 0, 0)),
            pl.BlockSpec((1, L, 2 * V_HEAD), lambda b, p: (b, 0, p)),
        ],
        out_specs=pl.BlockSpec((1, L, 2 * V_HEAD), lambda b, p: (b, 0, p)),
        out_shape=jax.ShapeDtypeStruct((B, L, ATTN_WIDTH), BF16),
        scratch_shapes=[pltpu.VMEM((L, 2 * V_HEAD + LANES), BF16)],
        compiler_params=_params("parallel", "parallel"),
    )(q, k, v)


def _ssm_kernel(u_ref, bbd_ref, lam_ref, cbd_ref, d_ref, wglu_ref, g_ref, o_ref, state_ref, *bu_refs, batch, tc):
    @pl.when(pl.program_id(0) == 0)
    def _():
        state_ref[...] = jnp.zeros_like(state_ref)

    steps = tc // SSM_PARTS
    halves = tuple((bu, slice(p * steps * batch, (p + 1) * steps * batch)) for p, bu in enumerate(bu_refs))
    for bu_ref, rows in halves:
        bu_ref[...] = jnp.dot(u_ref[rows, :], bbd_ref[...], preferred_element_type=F32)
    lam_re = jnp.broadcast_to(lam_ref[0:1, :], (batch, SSM_FLAT))
    lam_im = jnp.broadcast_to(lam_ref[1:2, :], (batch, SSM_FLAT))

    carry = (state_ref[:, 0:SSM_FLAT], state_ref[:, SSM_FLAT:2 * SSM_FLAT])
    for bu_ref, rows in halves:
        def step(t, xs, bu_ref=bu_ref):
            xr, xi = xs
            at = pl.ds(pl.multiple_of(t * batch, batch), batch)
            nr = lam_re * xr - lam_im * xi + bu_ref[at, 0:SSM_FLAT]
            ni = lam_re * xi + lam_im * xr + bu_ref[at, SSM_FLAT:2 * SSM_FLAT]
            bu_ref[at, 0:SSM_FLAT] = nr
            bu_ref[at, SSM_FLAT:2 * SSM_FLAT] = ni
            return nr, ni

        carry = lax.fori_loop(0, steps, step, carry, unroll=True)
        y = jnp.dot(bu_ref[...].astype(BF16), cbd_ref[...], preferred_element_type=F32)
        y = y + d_ref[...] * u_ref[rows, :].astype(F32)
        z = jax.nn.gelu(y)
        gate = jax.nn.sigmoid(jnp.dot(z.astype(BF16), wglu_ref[...], preferred_element_type=F32))
        o_ref[rows, :] = _rms(z * gate, g_ref[...]).astype(BF16)
    state_ref[:, 0:SSM_FLAT] = carry[0]
    state_ref[:, SSM_FLAT:2 * SSM_FLAT] = carry[1]


def _ssm(u_tm, bbd, lam, cbd, d, wglu, g, batch):
    rows = u_tm.shape[0]
    seq = rows // batch
    tc = min(TC_SSM, seq)
    blk = tc * batch
    full = lambda shape: pl.BlockSpec(shape, lambda t: (0,) * len(shape))
    return pl.pallas_call(
        functools.partial(_ssm_kernel, batch=batch, tc=tc),
        grid=(seq // tc,),
        in_specs=[
            pl.BlockSpec((blk, SSM_WIDTH), lambda t: (t, 0)),
            full((SSM_WIDTH, 2 * SSM_FLAT)), full((2, SSM_FLAT)), full((2 * SSM_FLAT, SSM_WIDTH)),
            full((1, SSM_WIDTH)), full((SSM_WIDTH, SSM_WIDTH)), full((1, SSM_WIDTH)),
        ],
        out_specs=pl.BlockSpec((blk, SSM_WIDTH), lambda t: (t, 0)),
        out_shape=jax.ShapeDtypeStruct((rows, SSM_WIDTH), BF16),
        scratch_shapes=[pltpu.VMEM((batch, 2 * SSM_FLAT), F32)]
        + [pltpu.VMEM((blk // SSM_PARTS, 2 * SSM_FLAT), F32)] * SSM_PARTS,
        compiler_params=_params("arbitrary"),
    )(u_tm, bbd, lam, cbd, d, wglu, g)


def _conv_rows(upad_ref, r0, w_ref, b_ref, lnw_ref, lnb_ref, gn_ref):
    base = CONV_HALO - (CONV_K - 1)
    acc = jnp.zeros((CONV_SUB, CONV_WIDTH), F32)
    for off in range(SUBLANES):
        n = CONV_SUB + (SUBLANES if off else 0)
        part = None
        for a8 in range(0, CONV_HALO + 1, SUBLANES):
            kk = a8 + off - base
            if 0 <= kk < CONV_K:
                term = w_ref[kk:kk + 1, :] * upad_ref[r0 + a8:r0 + a8 + n, :]
                part = term if part is None else part + term
        acc = acc + part[off:off + CONV_SUB]
    y = acc + b_ref[...]
    mu = jnp.mean(y, axis=-1, keepdims=True)
    var = jnp.mean(jnp.square(y - mu), axis=-1, keepdims=True)
    y = (y - mu) * lax.rsqrt(var + 1e-5) * lnw_ref[...] + lnb_ref[...]
    y = y * jax.nn.sigmoid(y)
    return _rms(y, gn_ref[...])


ROUTE_E1, ROUTE_E2, ROUTE_R1, ROUTE_R2, ROUTE_G1, ROUTE_G2 = range(6)
ROUTE_ROWS = 8
GRP_LANE0 = N_EXPERTS


def _out_router_kernel(x_ref, ya_ref, ys_ref, ca_ref, cg_ref, ga_ref, wo_ref, fn_ref, wr_ref, br_ref,
                       cw_ref, cb_ref, clnw_ref, clnb_ref, cgn_ref,
                       x1_ref, h2_ref, route_ref, route_t_ref, cnt_ref, sub_ref,
                       carry_ref, tri_ref, upad_ref, yc_ref, *, tm, per_b):
    @pl.when(pl.program_id(0) == 0)
    def _():
        carry_ref[...] = jnp.zeros_like(carry_ref)
        rr = lax.broadcasted_iota(jnp.int32, (tm, tm), 0)
        cc = lax.broadcasted_iota(jnp.int32, (tm, tm), 1)
        tri_ref[...] = (cc < rr).astype(BF16)

    first = pl.program_id(0) % per_b == 0
    upad_ref[0:CONV_HALO, :] = jnp.where(first, 0.0, upad_ref[tm:tm + CONV_HALO, :])
    upad_ref[CONV_HALO:, :] = ca_ref[...].astype(F32) * jax.nn.sigmoid(cg_ref[...].astype(F32))
    ya = _rms(ya_ref[...].astype(F32), ga_ref[...]).astype(BF16)
    acc = jnp.dot(ya, wo_ref[0:ATTN_WIDTH, :], preferred_element_type=F32)
    acc += jnp.dot(ys_ref[...], wo_ref[ATTN_WIDTH:ATTN_WIDTH + SSM_WIDTH, :], preferred_element_type=F32)
    for r0 in range(0, tm, CONV_SUB):
        yc_ref[r0:r0 + CONV_SUB, :] = _conv_rows(upad_ref, r0, cw_ref, cb_ref, clnw_ref, clnb_ref, cgn_ref).astype(BF16)
    acc += jnp.dot(yc_ref[...], wo_ref[ATTN_WIDTH + SSM_WIDTH:, :], preferred_element_type=F32)
    x1 = x_ref[...] + acc
    x1_ref[...] = x1
    h2 = _rms(x1, fn_ref[...])
    h2_ref[...] = _pack_rows(h2)

    h_hi = h2.astype(BF16)
    h_lo = (h2 - h_hi.astype(F32)).astype(BF16)
    part = jnp.dot(h_hi, wr_ref[...], preferred_element_type=F32)
    logits = (part[:, 0:LANES] + part[:, LANES:2 * LANES]
              + jnp.dot(h_lo, wr_ref[:, 0:LANES], preferred_element_type=F32) + br_ref[...])
    lane = lax.broadcasted_iota(jnp.int32, (tm, LANES), 1)
    ninf = -jnp.inf
    big = LANES

    def first_argmax(vals, vmax):
        return jnp.min(jnp.where(vals == vmax, lane, big), axis=-1, keepdims=True)

    grp = jnp.where((lane >= GRP_LANE0) & (lane < GRP_LANE0 + N_EGROUPS), logits, ninf)
    gmax = jnp.max(grp, axis=-1, keepdims=True)
    gsel = first_argmax(grp, gmax) - GRP_LANE0
    p_grp = 1.0 / jnp.sum(jnp.exp(grp - gmax), axis=-1, keepdims=True)

    el = jnp.where((lane < N_EXPERTS) & ((lane // EXP_PER_GROUP) == gsel), logits, ninf)
    m1 = jnp.max(el, axis=-1, keepdims=True)
    e1 = first_argmax(el, m1)
    el2 = jnp.where(lane == e1, ninf, el)
    m2 = jnp.max(el2, axis=-1, keepdims=True)
    e2 = first_argmax(el2, m2)
    t2 = jnp.exp(m2 - m1)
    g1 = p_grp / (1.0 + t2)
    g2 = p_grp * t2 / (1.0 + t2)

    hit1 = lane == e1
    hit2 = lane == e2
    cnt = (hit1 | hit2).astype(F32)
    before = jnp.dot(tri_ref[...], cnt.astype(BF16), preferred_element_type=F32) + carry_ref[...]
    r1 = jnp.sum(jnp.where(hit1, before, 0.0), axis=-1, keepdims=True)
    r2 = jnp.sum(jnp.where(hit2, before, 0.0), axis=-1, keepdims=True)
    subs = [before[h * TM_MOVE:h * TM_MOVE + 1, :] for h in range(tm // TM_MOVE)]
    sub_ref[...] = jnp.concatenate(subs + [jnp.zeros((SUBLANES - len(subs), LANES), F32)], axis=0)
    carry_ref[...] += jnp.sum(cnt, axis=0, keepdims=True)
    cnt_ref[...] = carry_ref[...]

    rec = jnp.where(lane == ROUTE_E1, e1.astype(F32), 0.0)
    rec = jnp.where(lane == ROUTE_E2, e2.astype(F32), rec)
    rec = jnp.where(lane == ROUTE_R1, r1, rec)
    rec = jnp.where(lane == ROUTE_R2, r2, rec)
    rec = jnp.where(lane == ROUTE_G1, g1, rec)
    rec = jnp.where(lane == ROUTE_G2, g2, rec)
    route_ref[...] = rec
    route_t_ref[...] = rec.T[0:ROUTE_ROWS, :]


def _out_router(x, ya, ys_tm, ca, cg, ga, wo, fn, wr, br, conv_params, batch):
    N, D = x.shape
    seq = N // batch
    tm = min(TM_OUT, seq)
    per_b = seq // tm
    full = lambda shape: pl.BlockSpec(shape, lambda i: (0,) * len(shape))
    tile = lambda w: pl.BlockSpec((tm, w), lambda i: (i, 0))
    return pl.pallas_call(
        functools.partial(_out_router_kernel, tm=tm, per_b=per_b),
        grid=(N // tm,),
        in_specs=[
            tile(D), tile(ATTN_WIDTH),
            pl.BlockSpec((tm, SSM_WIDTH), lambda i: (i % per_b, i // per_b)),
            tile(CONV_WIDTH), tile(CONV_WIDTH),
            full((1, ATTN_WIDTH)), full((D, D)), full((1, D)), full((D, 2 * LANES)), full((1, LANES)),
            full((CONV_K, CONV_WIDTH)), full((1, CONV_WIDTH)), full((1, CONV_WIDTH)), full((1, CONV_WIDTH)),
            full((1, CONV_WIDTH)),
        ],
        out_specs=[tile(D), tile(D_PACK), tile(LANES), pl.BlockSpec((ROUTE_ROWS, tm), lambda i: (0, i)),
                   full((1, LANES)), pl.BlockSpec((SUBLANES, LANES), lambda i: (i, 0))],
        out_shape=[
            jax.ShapeDtypeStruct((N, D), F32),
            jax.ShapeDtypeStruct((N, D_PACK), U32),
            jax.ShapeDtypeStruct((N, LANES), F32),
            jax.ShapeDtypeStruct((ROUTE_ROWS, N), F32),
            jax.ShapeDtypeStruct((1, LANES), F32),
            jax.ShapeDtypeStruct((N // tm * SUBLANES, LANES), F32),
        ],
        scratch_shapes=[pltpu.VMEM((1, LANES), F32), pltpu.VMEM((tm, tm), BF16),
                        pltpu.VMEM((tm + CONV_HALO, CONV_WIDTH), F32), pltpu.VMEM((tm, CONV_WIDTH), BF16)],
        compiler_params=_params("arbitrary"),
    )(x, ya, ys_tm, ca, cg, ga, wo, fn, wr, br, *conv_params)


def _dispatch(dest_flat, h2, n_rows):
    N, D = h2.shape
    workers = SC_CORES * SC_SUBCORES
    per_worker = N // workers
    chunks = per_worker // SC_CHUNK
    idx = dest_flat.reshape(2, workers, chunks, SC_CHUNK).transpose(1, 0, 2, 3).reshape(workers, 2 * chunks, SC_CHUNK)
    mesh = plsc.VectorSubcoreMesh(core_axis_name="c", subcore_axis_name="s")

    @functools.partial(
        pl.kernel, mesh=mesh,
        out_type=jax.ShapeDtypeStruct((n_rows, D), h2.dtype),
        scratch_types=[pltpu.VMEM((2 * chunks, SC_CHUNK), jnp.int32), pltpu.VMEM((2, SC_CHUNK, D), h2.dtype),
                       pltpu.SemaphoreType.DMA((2,)), pltpu.SemaphoreType.DMA((2, 2))],
    )
    def scatter(h_hbm, idx_hbm, out_hbm, idx_v, rows_v, sem_in, sem_out):
        wid = lax.axis_index("s") * SC_CORES + lax.axis_index("c")
        pltpu.sync_copy(idx_hbm.at[wid], idx_v)

        def load(c):
            first = pl.multiple_of(wid * per_worker + c * SC_CHUNK, SC_CHUNK)
            return pltpu.async_copy(h_hbm.at[pl.ds(first, SC_CHUNK)], rows_v.at[c % 2], sem_in.at[c % 2])

        def store(c, k):
            return pltpu.async_copy(rows_v.at[c % 2], out_hbm.at[idx_v.at[k * chunks + c]], sem_out.at[c % 2, k])

        loads = {0: load(0)}
        stores = {}
        for c in range(chunks):
            loads[c].wait()
            if c >= 1:
                for st in stores[c - 1]:
                    st.wait()
            if c + 1 < chunks:
                loads[c + 1] = load(c + 1)
            stores[c] = (store(c, 0), store(c, 1))
        for st in stores[chunks - 1]:
            st.wait()

    return scatter(h2, idx)


def _experts_kernel(blk_ref, exp_ref, valid_ref, slot_ref, next_ref, used_ref, x_ref, w1_hbm, w3_hbm, w2_hbm, y_ref,
                    w1_f, w3_f, w2_f, w13_s, w2_s, sem, *, layer):
    j = pl.program_id(0)

    def weight_copies(e, slot):
        return (pltpu.make_async_copy(w1_hbm.at[layer, e], w1_f.at[slot], sem.at[slot, 0]),
                pltpu.make_async_copy(w3_hbm.at[layer, e], w3_f.at[slot], sem.at[slot, 1]),
                pltpu.make_async_copy(w2_hbm.at[layer, e], w2_f.at[slot], sem.at[slot, 2]))

    @pl.when(j < used_ref[0])
    def _():
        e = exp_ref[j]
        slot = slot_ref[j]

        @pl.when(j == 0)
        def _():
            for cp in weight_copies(e, slot):
                cp.start()

        @pl.when((j == 0) | (e != exp_ref[jnp.maximum(j - 1, 0)]))
        def _():
            for cp in weight_copies(e, slot):
                cp.wait()
            for c in range(D_FF_E // FF_CHUNK):
                w13_s[:, 2 * FF_CHUNK * c:2 * FF_CHUNK * c + FF_CHUNK] = \
                    w1_f[slot, :, FF_CHUNK * c:FF_CHUNK * (c + 1)].astype(BF16)
                w13_s[:, 2 * FF_CHUNK * c + FF_CHUNK:2 * FF_CHUNK * (c + 1)] = \
                    w3_f[slot, :, FF_CHUNK * c:FF_CHUNK * (c + 1)].astype(BF16)
            w2_s[...] = w2_f[slot].astype(BF16)

            @pl.when(next_ref[j] >= 0)
            def _():
                for cp in weight_copies(next_ref[j], 1 - slot):
                    cp.start()

        x = _unpack_rows(x_ref[...])
        y = None
        for c in range(D_FF_E // FF_CHUNK):
            ab = jnp.dot(x, w13_s[:, 2 * FF_CHUNK * c:2 * FF_CHUNK * (c + 1)], preferred_element_type=F32)
            a = ab[:, 0:FF_CHUNK]
            hmid = (a * jax.nn.sigmoid(a) * ab[:, FF_CHUNK:]).astype(BF16)
            part = jnp.dot(hmid, w2_s[FF_CHUNK * c:FF_CHUNK * (c + 1), :], preferred_element_type=F32)
            y = part if y is None else y + part
        row = lax.broadcasted_iota(jnp.int32, (TB_EXP, 1), 0)
        y_ref[...] = jnp.where(row < valid_ref[j], _pack_rows(y), jnp.uint32(0))


def _experts(blk_map, blk_exp, blk_valid, blk_slot, blk_next, n_used, x_rows, w1, w3, w2, layer):
    R, D = x_rows.shape[0], D_MODEL
    nb = R // TB_EXP
    row_block = lambda j, bm, be, bv, bs, bn, nu: (bm[j], 0)
    return pl.pallas_call(
        functools.partial(_experts_kernel, layer=layer),
        grid_spec=pltpu.PrefetchScalarGridSpec(
            num_scalar_prefetch=6,
            grid=(nb,),
            in_specs=[
                pl.BlockSpec((TB_EXP, D_PACK), row_block),
                pl.BlockSpec(memory_space=pl.ANY), pl.BlockSpec(memory_space=pl.ANY), pl.BlockSpec(memory_space=pl.ANY),
            ],
            out_specs=pl.BlockSpec((TB_EXP, D_PACK), row_block),
            scratch_shapes=[
                pltpu.VMEM((2, D, D_FF_E), F32), pltpu.VMEM((2, D, D_FF_E), F32), pltpu.VMEM((2, D_FF_E, D), F32),
                pltpu.VMEM((D, 2 * D_FF_E), BF16), pltpu.VMEM((D_FF_E, D), BF16),
                pltpu.SemaphoreType.DMA((2, 3)),
            ],
        ),
        out_shape=jax.ShapeDtypeStruct((R, D_PACK), U32),
        compiler_params=_params("arbitrary"),
    )(blk_map, blk_exp, blk_valid, blk_slot, blk_next, n_used, x_rows, w1, w3, w2)


def _combine_kernel(src_ref, nslot_ref, x1_ref, route_ref, off_ref, rows_ref, o_ref, buf, sem, *, tm):
    i = pl.program_id(0)
    cur = i % 2

    def slot_copy(tile, half, s):
        src = pl.multiple_of(src_ref[tile * RUN_SLOTS + s], RUN_ROWS)
        dst = buf.at[half, pl.ds(pl.multiple_of(s * RUN_ROWS, RUN_ROWS), RUN_ROWS)]
        return pltpu.make_async_copy(rows_ref.at[pl.ds(src, RUN_ROWS)], dst, sem.at[half])

    def fetch(tile, half):
        def issue(g, c):
            for u in range(RUN_GROUP):
                slot_copy(tile, half, g * RUN_GROUP + u).start()
            return c
        lax.fori_loop(0, nslot_ref[tile], issue, 0)

    def wait(g, c):
        rows = RUN_GROUP * RUN_ROWS
        pltpu.make_async_copy(rows_ref.at[pl.ds(0, rows)], buf.at[cur, pl.ds(0, rows)], sem.at[cur]).wait()
        return c

    @pl.when(i == 0)
    def _():
        buf[...] = jnp.zeros_like(buf)
        fetch(0, 0)

    @pl.when(i + 1 < pl.num_programs(0))
    def _():
        fetch(i + 1, 1 - cur)

    route = route_ref[...]
    lane = lax.broadcasted_iota(jnp.int32, (tm, LANES), 1).astype(F32)
    off = off_ref[0]

    def buf_row(e_lane, r_lane):
        e = route[:, e_lane:e_lane + 1]
        base = jnp.sum(jnp.where(lane == e, off, 0.0), axis=-1, keepdims=True)
        return (base + route[:, r_lane:r_lane + 1]).astype(jnp.int32)

    col = lax.broadcasted_iota(jnp.int32, (tm, RUN_SLOTS * RUN_ROWS), 1)
    pick = (jnp.where(col == buf_row(ROUTE_E1, ROUTE_R1), route[:, ROUTE_G1:ROUTE_G1 + 1], 0.0)
            + jnp.where(col == buf_row(ROUTE_E2, ROUTE_R2), route[:, ROUTE_G2:ROUTE_G2 + 1], 0.0))
    lax.fori_loop(0, nslot_ref[i], wait, 0)
    o_ref[...] = x1_ref[...] + jnp.dot(pick.astype(BF16), _unpack_rows(buf[cur]), preferred_element_type=F32)


def _combine(src, nslot, x1, route, off, y_rows):
    N, D = x1.shape
    tm = min(TM_MOVE, N)
    return pl.pallas_call(
        functools.partial(_combine_kernel, tm=tm),
        grid_spec=pltpu.PrefetchScalarGridSpec(
            num_scalar_prefetch=2,
            grid=(N // tm,),
            in_specs=[
                pl.BlockSpec((tm, D), lambda i, s, n: (i, 0)),
                pl.BlockSpec((tm, LANES), lambda i, s, n: (i, 0)),
                pl.BlockSpec((1, 1, LANES), lambda i, s, n: (i, 0, 0)),
                pl.BlockSpec(memory_space=pl.ANY),
            ],
            out_specs=pl.BlockSpec((tm, D), lambda i, s, n: (i, 0)),
            scratch_shapes=[pltpu.VMEM((2, RUN_SLOTS * RUN_ROWS, D_PACK), U32), pltpu.SemaphoreType.DMA((2,))],
        ),
        out_shape=jax.ShapeDtypeStruct((N, D), F32),
        compiler_params=_params("arbitrary"),
    )(src, nslot, x1, route, off, y_rows)


def _moe(x1, h2, route, route_t, counts, sub_carry, w1, w3, w2, layer):
    N, _ = x1.shape
    tm = min(TM_MOVE, N)
    n_tiles = N // tm
    experts = jnp.arange(N_EXPERTS, dtype=jnp.int32)
    nb = (2 * N + N_EXPERTS * (TB_EXP - 1)) // TB_EXP + 1
    e_id = route_t[ROUTE_E1:ROUTE_E2 + 1].astype(jnp.int32)
    rank = route_t[ROUTE_R1:ROUTE_R2 + 1].astype(jnp.int32)
    cnt = counts[0, :N_EXPERTS].astype(jnp.int32)
    padded = (cnt + TB_EXP - 1) // TB_EXP * TB_EXP
    pad_end = jnp.cumsum(padded)
    pad_start = pad_end - padded
    e_flat = e_id.reshape(1, 2 * N)
    dest = jnp.sum(jnp.where(e_flat == experts[:, None], pad_start[:, None], 0), axis=0) + rank.reshape(2 * N)
    n_used = pad_end[-1] // TB_EXP
    blk_map = jnp.minimum(jnp.arange(nb, dtype=jnp.int32), jnp.maximum(n_used - 1, 0))
    blk_exp = jnp.sum(pad_end[None, :] <= (blk_map * TB_EXP)[:, None], axis=1).astype(jnp.int32)
    blk_exp = jnp.minimum(blk_exp, N_EXPERTS - 1)
    is_blk_e = blk_exp[:, None] == experts[None, :]
    blk_valid = jnp.clip(jnp.sum(jnp.where(is_blk_e, (pad_start + cnt)[None, :], 0), axis=1) - blk_map * TB_EXP,
                         0, TB_EXP).astype(jnp.int32)
    has_rows = cnt > 0
    slot_of_e = (jnp.cumsum(has_rows.astype(jnp.int32)) - 1) % 2
    later = jnp.where(has_rows, experts, N_EXPERTS)
    next_of_e = lax.cummin(jnp.concatenate([later[1:], jnp.full((1,), N_EXPERTS, jnp.int32)]), reverse=True)
    next_of_e = jnp.where(next_of_e < N_EXPERTS, next_of_e, -1)
    blk_slot = jnp.sum(jnp.where(is_blk_e, slot_of_e[None, :], 0), axis=1).astype(jnp.int32)
    blk_next = jnp.sum(jnp.where(is_blk_e, next_of_e[None, :], 0), axis=1).astype(jnp.int32)

    per_router_tile = TM_OUT // tm if N >= TM_OUT else 1
    carry = sub_carry.reshape(-1, SUBLANES, LANES)[:, :per_router_tile, :N_EXPERTS].reshape(n_tiles, N_EXPERTS)
    carry = carry.astype(jnp.int32)
    tile_cnt = jnp.concatenate([carry[1:], cnt[None, :]], axis=0) - carry
    run_start = pad_start[None, :] + carry
    first_blk = run_start // RUN_ROWS
    nslot = jnp.where(tile_cnt > 0, (run_start + tile_cnt - 1) // RUN_ROWS - first_blk + 1, 0)
    slot_end = jnp.cumsum(nslot, axis=1)
    slot_base = slot_end - nslot
    slots = jnp.arange(RUN_SLOTS, dtype=jnp.int32)
    slot_e = jnp.minimum(jnp.sum(slot_end[:, None, :] <= slots[None, :, None], axis=2), N_EXPERTS - 1)
    is_slot_e = slot_e[:, :, None] == experts[None, None, :]
    pick = lambda tbl: jnp.sum(jnp.where(is_slot_e, tbl[:, None, :], 0), axis=2)
    src = (pick(first_blk) + slots[None, :] - pick(slot_base)) * RUN_ROWS
    src = jnp.where(slots[None, :] < slot_end[:, -1:], src, 0)
    src = jnp.clip(src, 0, nb * TB_EXP - RUN_ROWS).reshape(-1).astype(jnp.int32)
    n_groups = (slot_end[:, -1] + RUN_GROUP - 1) // RUN_GROUP
    off = (slot_base - first_blk) * RUN_ROWS + pad_start[None, :]
    off = jnp.pad(off.astype(F32), ((0, 0), (0, LANES - N_EXPERTS)))

    x_rows = _dispatch(dest, h2, nb * TB_EXP)
    y_rows = _experts(blk_map, blk_exp, blk_valid, blk_slot, blk_next, n_used.reshape(1).astype(jnp.int32),
                      x_rows, w1, w3, w2, layer)
    return _combine(src, n_groups.astype(jnp.int32), x1, route, off.reshape(n_tiles, 1, LANES), y_rows)


def _pad_heads(w, width):
    k = w.shape[0]
    w = w.reshape(k, N_HEADS, width)
    return jnp.pad(w, ((0, 0), (0, 0), (0, HEAD_PAD - width))).reshape(k, N_HEADS * HEAD_PAD)


def _swap_rope(w):
    half = QK_ROPE // 2
    lo, hi = w[..., QK_NOPE:QK_NOPE + half], w[..., QK_NOPE + half:QK_HEAD]
    pad = [(0, 0)] * (w.ndim - 1)
    return jnp.pad(jnp.concatenate([hi, lo], -1), pad + [(QK_NOPE, HEAD_PAD - QK_HEAD)])


def _rope_tables(positions):
    inv_freq = ROPE_THETA ** (-jnp.arange(0, QK_ROPE, 2, dtype=F32) / QK_ROPE)
    ang = positions.astype(F32)[..., None] * inv_freq
    table = jnp.concatenate([jnp.cos(ang), jnp.sin(ang)], -1)
    return jnp.pad(table, ((0, 0), (0, 0), (QK_NOPE, LANES - QK_HEAD)))


def _ssm_params(lam_re, lam_im, b_re, b_im, c_re, c_im, log_dt):
    lam = lax.complex(lam_re, lam_im)
    dt = jnp.exp(log_dt)[:, None]
    lam_bar = jnp.exp(lam * dt)
    b_bar = ((lam_bar - 1.0) / lam)[..., None] * lax.complex(b_re, b_im)
    eye = jnp.eye(SSM_GROUPS, dtype=F32)

    def in_blockdiag(m):
        return jnp.einsum("gpc,gh->gchp", m, eye).reshape(SSM_WIDTH, SSM_FLAT)

    def out_blockdiag(m):
        return jnp.einsum("gcp,gh->gphc", m, eye).reshape(SSM_FLAT, SSM_WIDTH)

    bbd = jnp.concatenate([in_blockdiag(jnp.real(b_bar)), in_blockdiag(jnp.imag(b_bar))], axis=1)
    cbd = jnp.concatenate([out_blockdiag(c_re), out_blockdiag(-c_im)], axis=0)
    lam_rows = jnp.stack([jnp.real(lam_bar).reshape(SSM_FLAT), jnp.imag(lam_bar).reshape(SSM_FLAT)])
    return bbd.astype(BF16), lam_rows, cbd.astype(BF16)


def kernel(x, positions, mix_norm, w_in, q_a_norm, w_uq, kv_a_norm, w_ukv, q_norm, k_norm, ssm_lam_re, ssm_lam_im, ssm_b_re, ssm_b_im, ssm_c_re, ssm_c_im, ssm_d, ssm_log_dt, ssm_w_glu, conv_dw_w, conv_dw_b, conv_ln_w, conv_ln_b, out_norm, w_out, ffn_norm, w_grp, b_grp, w_exp, b_exp, w1, w3, w2):
    B, L, D = x.shape
    depth = w_in.shape[0]
    rope = _rope_tables(positions)
    row = lambda v: v.reshape(1, -1)
    lane_pad = lambda v: jnp.pad(v, (0, LANES - v.shape[0])).reshape(1, LANES)
    for l in range(depth):
        c_q, c_kv, k_pe, u_s, c_a, c_g = jnp.split(
            w_in[l], [Q_LORA, Q_LORA + KV_LORA, Q_LORA + KV_LORA + QK_ROPE,
                      Q_LORA + KV_LORA + QK_ROPE + SSM_WIDTH,
                      Q_LORA + KV_LORA + QK_ROPE + SSM_WIDTH + CONV_WIDTH], axis=1)
        k_pe_full = jnp.pad(k_pe, ((0, 0), (QK_NOPE, 0)))
        win = jnp.concatenate([c_q, c_kv, jnp.pad(k_pe_full, ((0, 0), (0, HEAD_PAD - QK_HEAD))), _swap_rope(k_pe_full),
                               u_s, c_a, c_g], axis=1).astype(BF16)
        wkv = w_ukv[l].reshape(KV_LORA, N_HEADS, QK_NOPE + V_HEAD)
        wuk = _pad_heads(wkv[:, :, :QK_NOPE].reshape(KV_LORA, N_HEADS * QK_NOPE), QK_NOPE).astype(BF16)
        wuv = wkv[:, :, QK_NOPE:].reshape(KV_LORA, ATTN_WIDTH).astype(BF16)
        wuq_sw = _swap_rope(w_uq[l].reshape(Q_LORA, N_HEADS, QK_HEAD)).reshape(Q_LORA, N_HEADS * HEAD_PAD)
        wuq = jnp.concatenate([_pad_heads(w_uq[l], QK_HEAD), wuq_sw], axis=1).astype(BF16)
        qn_scaled = q_norm[l] * (QK_HEAD ** -0.5)
        norm_rows = lambda w: jnp.stack([jnp.pad(w, (0, HEAD_PAD - QK_HEAD)), _swap_rope(w)])
        q, k, v, u_tm, ca, cg = _in_proj(
            x, rope, row(mix_norm[l]), win, row(q_a_norm[l]), wuq, row(kv_a_norm[l]), wuk, wuv,
            norm_rows(qn_scaled), norm_rows(k_norm[l]))
        y_attn = _attention(q, k, v)

        g_out = out_norm[l]
        bbd, lam_rows, cbd = _ssm_params(ssm_lam_re[l], ssm_lam_im[l], ssm_b_re[l], ssm_b_im[l],
                                         ssm_c_re[l], ssm_c_im[l], ssm_log_dt[l])
        y_ssm_tm = _ssm(u_tm.reshape(L * B, SSM_WIDTH), bbd, lam_rows, cbd, row(ssm_d[l]),
                        ssm_w_glu[l].astype(BF16), row(g_out[ATTN_WIDTH:ATTN_WIDTH + SSM_WIDTH]), B)
        conv_params = (conv_dw_w[l], row(conv_dw_b[l]), row(conv_ln_w[l]), row(conv_ln_b[l]),
                       row(g_out[ATTN_WIDTH + SSM_WIDTH:]))

        w_route = jnp.pad(jnp.concatenate([w_exp[l], w_grp[l]], axis=1), ((0, 0), (0, LANES - N_EXPERTS - N_EGROUPS)))
        w_route_hi = w_route.astype(BF16)
        w_route = jnp.concatenate([w_route_hi, (w_route - w_route_hi.astype(F32)).astype(BF16)], axis=1)
        b_route = lane_pad(jnp.concatenate([b_exp[l], b_grp[l]]))
        x1, h2, route, route_t, counts, sub_carry = _out_router(
            x.reshape(B * L, D), y_attn.reshape(B * L, ATTN_WIDTH), y_ssm_tm.reshape(L, B * SSM_WIDTH),
            ca.reshape(B * L, CONV_WIDTH), cg.reshape(B * L, CONV_WIDTH), row(g_out[:ATTN_WIDTH]),
            w_out[l].astype(BF16), row(ffn_norm[l]), w_route, b_route, conv_params, B)
        x = _moe(x1, h2, route, route_t, counts, sub_carry, w1, w3, w2, l).reshape(B, L, D)
    return x
```

```python
import functools
import math

import jax
import jax.numpy as jnp
from jax import lax
from jax.experimental import pallas as pl
from jax.experimental.pallas import tpu as pltpu
from jax.experimental.pallas import tpu_sc as plsc

D_MODEL = 1024
CHUNK = 64
EPS = 1e-6
N_HEADS = 8
QK_NOPE = 64
QK_ROPE = 32
QK_HEAD = QK_NOPE + QK_ROPE
V_HEAD = 64
Q_LORA = 256
KV_LORA = 128
ROPE_THETA = 10000.0
ATTN_WIDTH = N_HEADS * V_HEAD
SSM_WIDTH = 256
SSM_GROUP = 16
SSM_GROUPS = SSM_WIDTH // SSM_GROUP
SSM_STATE = 64
SSM_FLAT = SSM_GROUPS * SSM_STATE
CONV_WIDTH = 256
CONV_K = 31
N_EGROUPS = 4
EXP_PER_GROUP = 8
N_EXPERTS = N_EGROUPS * EXP_PER_GROUP
D_FF_E = 512

LANES = 128
SUBLANES = 8
HEAD_PAD = LANES
IN_PROJ_PAD = Q_LORA + KV_LORA + SSM_WIDTH + 2 * CONV_WIDTH + 2 * LANES
CONV_HALO = 32
VMEM_LIMIT = 48 * 1024 * 1024

TM_PROJ = 512
TQ_ATTN = 256
TC_SSM = 128
SSM_PARTS = 2
CONV_SUB = 128
TM_OUT = 512
TM_MOVE = 256
RUN_ROWS = SUBLANES
RUN_SLOTS = 2 * N_EXPERTS + 2 * TM_MOVE // RUN_ROWS
RUN_GROUP = 8
SC_CORES, SC_SUBCORES = 2, 16
SC_CHUNK = 64
TB_EXP = 512
EXPERT_TAIL_DIVISORS = (1, 2, 4)
FF_CHUNK = 256

BF16 = jnp.bfloat16
F32 = jnp.float32
U32 = jnp.uint32
D_PACK = D_MODEL // 2


def _pack_rows(v):
    bits = lax.bitcast_convert_type(v.astype(BF16).astype(F32), U32)
    half = v.shape[1] // 2
    return bits[:, 0:half] | (bits[:, half:] >> 16)


def _unpack_rows(w):
    hi = lax.bitcast_convert_type(w & jnp.uint32(0xFFFF0000), F32)
    lo = lax.bitcast_convert_type(w << 16, F32)
    return jnp.concatenate([hi, lo], axis=1).astype(BF16)


def _rms(x, w):
    return x * lax.rsqrt(jnp.mean(x * x, axis=-1, keepdims=True) + EPS) * w


def _params(*sem):
    return pltpu.CompilerParams(dimension_semantics=sem, vmem_limit_bytes=VMEM_LIMIT)


def _in_proj_kernel(x_ref, rope_ref, mixn_ref, win_ref, qan_ref, wuq_ref, kvan_ref, wuk_ref, wuv_ref,
                    qn_ref, kn_ref, q_ref, k_ref, v_ref, u_ref, ca_ref, cg_ref):
    x = x_ref[0]
    h = _rms(x, mixn_ref[...]).astype(BF16)
    head_cols = Q_LORA + KV_LORA + 2 * LANES
    proj = jnp.dot(h, win_ref[:, 0:head_cols], preferred_element_type=F32)
    o = 0
    c_q = proj[:, o:o + Q_LORA]; o += Q_LORA
    c_kv = proj[:, o:o + KV_LORA]; o += KV_LORA
    k_pe = proj[:, o:o + LANES]; o += LANES
    k_pe_sw = proj[:, o:o + LANES]

    width = N_HEADS * HEAD_PAD
    q2 = jnp.dot(_rms(c_q, qan_ref[...]).astype(BF16), wuq_ref[...], preferred_element_type=F32)
    q, q_sw = q2[:, 0:width], q2[:, width:2 * width]
    ckv_n = _rms(c_kv, kvan_ref[...]).astype(BF16)
    kn = jnp.dot(ckv_n, wuk_ref[...], preferred_element_type=F32)
    v_ref[0] = jnp.dot(ckv_n, wuv_ref[...], preferred_element_type=F32).astype(BF16)

    rope = rope_ref[0]
    half = QK_ROPE // 2
    lane = lax.broadcasted_iota(jnp.int32, rope.shape, 1)
    lo = (lane >= QK_NOPE) & (lane < QK_NOPE + half)
    hi = (lane >= QK_NOPE + half) & (lane < QK_HEAD)
    cos_t = jnp.where(lane < QK_NOPE, 1.0, jnp.where(lo, rope, jnp.where(hi, pltpu.roll(rope, half, 1), 0.0)))
    sin_t = jnp.where(lo, -pltpu.roll(rope, LANES - half, 1), jnp.where(hi, rope, 0.0))
    a_q, b_q = qn_ref[0:1, :] * cos_t, qn_ref[1:2, :] * sin_t
    a_k = kn_ref[0:1, :] * cos_t
    k_sw_term = k_pe_sw * (kn_ref[1:2, :] * sin_t)

    def inv_rms(y):
        return lax.rsqrt(jnp.sum(y * y, axis=-1, keepdims=True) * (1.0 / QK_HEAD) + EPS)

    def rest(j):
        cols = slice(head_cols + j * SSM_WIDTH, head_cols + (j + 1) * SSM_WIDTH)
        return jnp.dot(h, win_ref[:, cols], preferred_element_type=F32).astype(BF16)

    for hd in range(N_HEADS):
        if hd == 0:
            u_ref[...] = rest(0)
        elif hd == 3:
            ca_ref[0] = rest(1)
        elif hd == 6:
            cg_ref[0] = rest(2)
        sl = slice(hd * HEAD_PAD, (hd + 1) * HEAD_PAD)
        yq = q[:, sl]
        q_ref[0, hd] = ((yq * a_q + q_sw[:, sl] * b_q) * inv_rms(yq)).astype(BF16)
        yk = kn[:, sl] + k_pe
        k_ref[0, hd] = ((yk * a_k + k_sw_term) * inv_rms(yk)).astype(BF16)


def _in_proj(x, rope, mixn, win, qan, wuq, kvan, wuk, wuv, qn, kn):
    B, L, D = x.shape
    tm = min(TM_PROJ, L)
    full = lambda shape: pl.BlockSpec(shape, lambda b, t: (0,) * len(shape))
    return pl.pallas_call(
        _in_proj_kernel,
        grid=(B, L // tm),
        in_specs=[
            pl.BlockSpec((1, tm, D), lambda b, t: (b, t, 0)),
            pl.BlockSpec((1, tm, LANES), lambda b, t: (b, t, 0)),
            full((1, D)), full((D, IN_PROJ_PAD)), full((1, Q_LORA)), full((Q_LORA, 2 * N_HEADS * HEAD_PAD)),
            full((1, KV_LORA)), full((KV_LORA, N_HEADS * HEAD_PAD)), full((KV_LORA, ATTN_WIDTH)),
            full((2, HEAD_PAD)), full((2, HEAD_PAD)),
        ],
        out_specs=[
            pl.BlockSpec((1, N_HEADS, tm, HEAD_PAD), lambda b, t: (b, 0, t, 0)),
            pl.BlockSpec((1, N_HEADS, tm, HEAD_PAD), lambda b, t: (b, 0, t, 0)),
            pl.BlockSpec((1, tm, ATTN_WIDTH), lambda b, t: (b, t, 0)),
            pl.BlockSpec((tm, SSM_WIDTH), lambda b, t: (t, b)),
            pl.BlockSpec((1, tm, CONV_WIDTH), lambda b, t: (b, t, 0)),
            pl.BlockSpec((1, tm, CONV_WIDTH), lambda b, t: (b, t, 0)),
        ],
        out_shape=[
            jax.ShapeDtypeStruct((B, N_HEADS, L, HEAD_PAD), BF16),
            jax.ShapeDtypeStruct((B, N_HEADS, L, HEAD_PAD), BF16),
            jax.ShapeDtypeStruct((B, L, ATTN_WIDTH), BF16),
            jax.ShapeDtypeStruct((L, B * SSM_WIDTH), BF16),
            jax.ShapeDtypeStruct((B, L, CONV_WIDTH), BF16),
            jax.ShapeDtypeStruct((B, L, CONV_WIDTH), BF16),
        ],
        compiler_params=_params("parallel", "parallel"),
    )(x, rope, mixn, win, qan, wuq, kvan, wuk, wuv, qn, kn)


def _attention_kernel(q_ref, k_ref, v_ref, o_ref, vext_ref, *, seq, tq):
    pair = 2 * V_HEAD
    vext_ref[:, 0:pair] = v_ref[0]
    vext_ref[:, pair:] = jnp.ones((seq, LANES), BF16)
    row_chunk = lax.broadcasted_iota(jnp.int32, (tq, tq), 0) // CHUNK
    col_chunk = lax.broadcasted_iota(jnp.int32, (tq, tq), 1) // CHUNK
    visible = col_chunk <= row_chunk
    neg = jnp.finfo(F32).min
    nt = (((1,), (1,)), ((), ()))
    low_lanes = lax.broadcasted_iota(jnp.int32, (tq, pair), 1) < V_HEAD
    def scores(i, hh):
        q0 = i * tq
        qb = q_ref[0, hh, q0:q0 + tq, :]
        s_d = lax.dot_general(qb, k_ref[0, hh, q0:q0 + tq, :], nt, preferred_element_type=F32)
        s_d = jnp.where(visible, s_d, neg)
        s_l = lax.dot_general(qb, k_ref[0, hh, 0:q0, :], nt, preferred_element_type=F32) if i > 0 else None
        return s_d, s_l

    def weighted_values(i, s_d, s_l):
        q0 = i * tq
        m = jnp.max(s_d, axis=-1, keepdims=True)
        if s_l is not None:
            m = jnp.maximum(m, jnp.max(s_l, axis=-1, keepdims=True))
        acc = jnp.dot(jnp.exp(s_d - m).astype(BF16), vext_ref[q0:q0 + tq, :], preferred_element_type=F32)
        if s_l is not None:
            acc = acc + jnp.dot(jnp.exp(s_l - m).astype(BF16), vext_ref[0:q0, :], preferred_element_type=F32)
        return acc[:, 0:pair] / acc[:, pair:]

    nq = seq // tq
    order = [x for p in zip(reversed(range(nq)), range(nq)) for x in p][:nq]
    chains = [(i, hh) for i in order for hh in range(2)]
    pending = scores(*chains[0])
    outs = {}
    for n, (i, hh) in enumerate(chains):
        upcoming = scores(*chains[n + 1]) if n + 1 < len(chains) else None
        outs[hh] = weighted_values(i, *pending)
        pending = upcoming
        if hh == 1:
            o_ref[0, i * tq:(i + 1) * tq, :] = jnp.where(low_lanes, outs[0], outs[1]).astype(BF16)


def _attention(q, k, v):
    B, H, L, _ = q.shape
    tq = min(TQ_ATTN, L)
    return pl.pallas_call(
        functools.partial(_attention_kernel, seq=L, tq=tq),
        grid=(B, H // 2),
        in_specs=[
            pl.BlockSpec((1, 2, L, HEAD_PAD), lambda b, p: (b, p, 0, 0)),
            pl.BlockSpec((1, 2, L, HEAD_PAD), lambda b, p: (b, p, 0, 0)),
            pl.BlockSpec((1, L, 2 * V_HEAD), lambda b, p: (b, 0, p)),
        ],
        out_specs=pl.BlockSpec((1, L, 2 * V_HEAD), lambda b, p: (b, 0, p)),
        out_shape=jax.ShapeDtypeStruct((B, L, ATTN_WIDTH), BF16),
        scratch_shapes=[pltpu.VMEM((L, 2 * V_HEAD + LANES), BF16)],
        compiler_params=_params("parallel", "parallel"),
    )(q, k, v)


def _ssm_kernel(u_ref, bbd_ref, lam_ref, cbd_ref, d_ref, wglu_ref, g_ref, o_ref, state_ref, *bu_refs, batch, tc):
    @pl.when(pl.program_id(0) == 0)
    def _():
        state_ref[...] = jnp.zeros_like(state_ref)

    steps = tc // SSM_PARTS
    halves = tuple((bu, slice(p * steps * batch, (p + 1) * steps * batch)) for p, bu in enumerate(bu_refs))
    for bu_ref, rows in halves:
        bu_ref[...] = jnp.dot(u_ref[rows, :], bbd_ref[...], preferred_element_type=F32)
    lam_re = jnp.broadcast_to(lam_ref[0:1, :], (batch, SSM_FLAT))
    lam_im = jnp.broadcast_to(lam_ref[1:2, :], (batch, SSM_FLAT))

    carry = (state_ref[:, 0:SSM_FLAT], state_ref[:, SSM_FLAT:2 * SSM_FLAT])
    for bu_ref, rows in halves:
        def step(t, xs, bu_ref=bu_ref):
            xr, xi = xs
            at = pl.ds(pl.multiple_of(t * batch, batch), batch)
            nr = lam_re * xr - lam_im * xi + bu_ref[at, 0:SSM_FLAT]
            ni = lam_re * xi + lam_im * xr + bu_ref[at, SSM_FLAT:2 * SSM_FLAT]
            bu_ref[at, 0:SSM_FLAT] = nr
            bu_ref[at, SSM_FLAT:2 * SSM_FLAT] = ni
            return nr, ni

        carry = lax.fori_loop(0, steps, step, carry, unroll=True)
        y = jnp.dot(bu_ref[...].astype(BF16), cbd_ref[...], preferred_element_type=F32)
        y = y + d_ref[...] * u_ref[rows, :].astype(F32)
        z = jax.nn.gelu(y)
        gate = jax.nn.sigmoid(jnp.dot(z.astype(BF16), wglu_ref[...], preferred_element_type=F32))
        o_ref[rows, :] = _rms(z * gate, g_ref[...]).astype(BF16)
    state_ref[:, 0:SSM_FLAT] = carry[0]
    state_ref[:, SSM_FLAT:2 * SSM_FLAT] = carry[1]


def _ssm(u_tm, bbd, lam, cbd, d, wglu, g, batch):
    rows = u_tm.shape[0]
    seq = rows // batch
    tc = min(TC_SSM, seq)
    blk = tc * batch
    full = lambda shape: pl.BlockSpec(shape, lambda t: (0,) * len(shape))
    return pl.pallas_call(
        functools.partial(_ssm_kernel, batch=batch, tc=tc),
        grid=(seq // tc,),
        in_specs=[
            pl.BlockSpec((blk, SSM_WIDTH), lambda t: (t, 0)),
            full((SSM_WIDTH, 2 * SSM_FLAT)), full((2, SSM_FLAT)), full((2 * SSM_FLAT, SSM_WIDTH)),
            full((1, SSM_WIDTH)), full((SSM_WIDTH, SSM_WIDTH)), full((1, SSM_WIDTH)),
        ],
        out_specs=pl.BlockSpec((blk, SSM_WIDTH), lambda t: (t, 0)),
        out_shape=jax.ShapeDtypeStruct((rows, SSM_WIDTH), BF16),
        scratch_shapes=[pltpu.VMEM((batch, 2 * SSM_FLAT), F32)]
        + [pltpu.VMEM((blk // SSM_PARTS, 2 * SSM_FLAT), F32)] * SSM_PARTS,
        compiler_params=_params("arbitrary"),
    )(u_tm, bbd, lam, cbd, d, wglu, g)


def _conv_rows(upad_ref, r0, w_ref, b_ref, lnw_ref, lnb_ref, gn_ref):
    base = CONV_HALO - (CONV_K - 1)
    acc = jnp.zeros((CONV_SUB, CONV_WIDTH), F32)
    for off in range(SUBLANES):
        n = CONV_SUB + (SUBLANES if off else 0)
        part = None
        for a8 in range(0, CONV_HALO + 1, SUBLANES):
            kk = a8 + off - base
            if 0 <= kk < CONV_K:
                term = w_ref[kk:kk + 1, :] * upad_ref[r0 + a8:r0 + a8 + n, :]
                part = term if part is None else part + term
        acc = acc + part[off:off + CONV_SUB]
    y = acc + b_ref[...]
    mu = jnp.mean(y, axis=-1, keepdims=True)
    var = jnp.mean(jnp.square(y - mu), axis=-1, keepdims=True)
    y = (y - mu) * lax.rsqrt(var + 1e-5) * lnw_ref[...] + lnb_ref[...]
    y = y * jax.nn.sigmoid(y)
    return _rms(y, gn_ref[...])


ROUTE_E1, ROUTE_E2, ROUTE_R1, ROUTE_R2, ROUTE_G1, ROUTE_G2 = range(6)
ROUTE_ROWS = 8
GRP_LANE0 = N_EXPERTS


def _out_router_kernel(x_ref, ya_ref, ys_ref, ca_ref, cg_ref, ga_ref, wo_ref, fn_ref, wr_ref, br_ref,
                       cw_ref, cb_ref, clnw_ref, clnb_ref, cgn_ref,
                       x1_ref, h2_ref, route_ref, route_t_ref, cnt_ref, sub_ref,
                       carry_ref, tri_ref, upad_ref, yc_ref, *, tm, per_b):
    @pl.when(pl.program_id(0) == 0)
    def _():
        carry_ref[...] = jnp.zeros_like(carry_ref)
        rr = lax.broadcasted_iota(jnp.int32, (tm, tm), 0)
        cc = lax.broadcasted_iota(jnp.int32, (tm, tm), 1)
        tri_ref[...] = (cc < rr).astype(BF16)

    first = pl.program_id(0) % per_b == 0
    upad_ref[0:CONV_HALO, :] = jnp.where(first, 0.0, upad_ref[tm:tm + CONV_HALO, :])
    upad_ref[CONV_HALO:, :] = ca_ref[...].astype(F32) * jax.nn.sigmoid(cg_ref[...].astype(F32))
    ya = _rms(ya_ref[...].astype(F32), ga_ref[...]).astype(BF16)
    acc = jnp.dot(ya, wo_ref[0:ATTN_WIDTH, :], preferred_element_type=F32)
    acc += jnp.dot(ys_ref[...], wo_ref[ATTN_WIDTH:ATTN_WIDTH + SSM_WIDTH, :], preferred_element_type=F32)
    for r0 in range(0, tm, CONV_SUB):
        yc_ref[r0:r0 + CONV_SUB, :] = _conv_rows(upad_ref, r0, cw_ref, cb_ref, clnw_ref, clnb_ref, cgn_ref).astype(BF16)
    acc += jnp.dot(yc_ref[...], wo_ref[ATTN_WIDTH + SSM_WIDTH:, :], preferred_element_type=F32)
    x1 = x_ref[...] + acc
    x1_ref[...] = x1
    h2 = _rms(x1, fn_ref[...])
    h2_ref[...] = _pack_rows(h2)

    h_hi = h2.astype(BF16)
    h_lo = (h2 - h_hi.astype(F32)).astype(BF16)
    part = jnp.dot(h_hi, wr_ref[...], preferred_element_type=F32)
    logits = (part[:, 0:LANES] + part[:, LANES:2 * LANES]
              + jnp.dot(h_lo, wr_ref[:, 0:LANES], preferred_element_type=F32) + br_ref[...])
    lane = lax.broadcasted_iota(jnp.int32, (tm, LANES), 1)
    ninf = -jnp.inf
    big = LANES

    def first_argmax(vals, vmax):
        return jnp.min(jnp.where(vals == vmax, lane, big), axis=-1, keepdims=True)

    grp = jnp.where((lane >= GRP_LANE0) & (lane < GRP_LANE0 + N_EGROUPS), logits, ninf)
    gmax = jnp.max(grp, axis=-1, keepdims=True)
    gsel = first_argmax(grp, gmax) - GRP_LANE0
    p_grp = 1.0 / jnp.sum(jnp.exp(grp - gmax), axis=-1, keepdims=True)

    el = jnp.where((lane < N_EXPERTS) & ((lane // EXP_PER_GROUP) == gsel), logits, ninf)
    m1 = jnp.max(el, axis=-1, keepdims=True)
    e1 = first_argmax(el, m1)
    el2 = jnp.where(lane == e1, ninf, el)
    m2 = jnp.max(el2, axis=-1, keepdims=True)
    e2 = first_argmax(el2, m2)
    t2 = jnp.exp(m2 - m1)
    g1 = p_grp / (1.0 + t2)
    g2 = p_grp * t2 / (1.0 + t2)

    hit1 = lane == e1
    hit2 = lane == e2
    cnt = (hit1 | hit2).astype(F32)
    before = jnp.dot(tri_ref[...], cnt.astype(BF16), preferred_element_type=F32) + carry_ref[...]
    r1 = jnp.sum(jnp.where(hit1, before, 0.0), axis=-1, keepdims=True)
    r2 = jnp.sum(jnp.where(hit2, before, 0.0), axis=-1, keepdims=True)
    subs = [before[h * TM_MOVE:h * TM_MOVE + 1, :] for h in range(tm // TM_MOVE)]
    sub_ref[...] = jnp.concatenate(subs + [jnp.zeros((SUBLANES - len(subs), LANES), F32)], axis=0)
    carry_ref[...] += jnp.sum(cnt, axis=0, keepdims=True)
    cnt_ref[...] = carry_ref[...]

    rec = jnp.where(lane == ROUTE_E1, e1.astype(F32), 0.0)
    rec = jnp.where(lane == ROUTE_E2, e2.astype(F32), rec)
    rec = jnp.where(lane == ROUTE_R1, r1, rec)
    rec = jnp.where(lane == ROUTE_R2, r2, rec)
    rec = jnp.where(lane == ROUTE_G1, g1, rec)
    rec = jnp.where(lane == ROUTE_G2, g2, rec)
    route_ref[...] = rec
    route_t_ref[...] = rec.T[0:ROUTE_ROWS, :]


def _out_router(x, ya, ys_tm, ca, cg, ga, wo, fn, wr, br, conv_params, batch):
    N, D = x.shape
    seq = N // batch
    tm = min(TM_OUT, seq)
    per_b = seq // tm
    full = lambda shape: pl.BlockSpec(shape, lambda i: (0,) * len(shape))
    tile = lambda w: pl.BlockSpec((tm, w), lambda i: (i, 0))
    return pl.pallas_call(
        functools.partial(_out_router_kernel, tm=tm, per_b=per_b),
        grid=(N // tm,),
        in_specs=[
            tile(D), tile(ATTN_WIDTH),
            pl.BlockSpec((tm, SSM_WIDTH), lambda i: (i % per_b, i // per_b)),
            tile(CONV_WIDTH), tile(CONV_WIDTH),
            full((1, ATTN_WIDTH)), full((D, D)), full((1, D)), full((D, 2 * LANES)), full((1, LANES)),
            full((CONV_K, CONV_WIDTH)), full((1, CONV_WIDTH)), full((1, CONV_WIDTH)), full((1, CONV_WIDTH)),
            full((1, CONV_WIDTH)),
        ],
        out_specs=[tile(D), tile(D_PACK), tile(LANES), pl.BlockSpec((ROUTE_ROWS, tm), lambda i: (0, i)),
                   full((1, LANES)), pl.BlockSpec((SUBLANES, LANES), lambda i: (i, 0))],
        out_shape=[
            jax.ShapeDtypeStruct((N, D), F32),
            jax.ShapeDtypeStruct((N, D_PACK), U32),
            jax.ShapeDtypeStruct((N, LANES), F32),
            jax.ShapeDtypeStruct((ROUTE_ROWS, N), F32),
            jax.ShapeDtypeStruct((1, LANES), F32),
            jax.ShapeDtypeStruct((N // tm * SUBLANES, LANES), F32),
        ],
        scratch_shapes=[pltpu.VMEM((1, LANES), F32), pltpu.VMEM((tm, tm), BF16),
                        pltpu.VMEM((tm + CONV_HALO, CONV_WIDTH), F32), pltpu.VMEM((tm, CONV_WIDTH), BF16)],
        compiler_params=_params("arbitrary"),
    )(x, ya, ys_tm, ca, cg, ga, wo, fn, wr, br, *conv_params)


def _dispatch(dest_flat, h2, n_rows):
    N, D = h2.shape
    workers = SC_CORES * SC_SUBCORES
    per_worker = N // workers
    chunks = per_worker // SC_CHUNK
    idx = dest_flat.reshape(2, workers, chunks, SC_CHUNK).transpose(1, 0, 2, 3).reshape(workers, 2 * chunks, SC_CHUNK)
    mesh = plsc.VectorSubcoreMesh(core_axis_name="c", subcore_axis_name="s")

    @functools.partial(
        pl.kernel, mesh=mesh,
        out_type=jax.ShapeDtypeStruct((n_rows, D), h2.dtype),
        scratch_types=[pltpu.VMEM((2 * chunks, SC_CHUNK), jnp.int32), pltpu.VMEM((2, SC_CHUNK, D), h2.dtype),
                       pltpu.SemaphoreType.DMA((2,)), pltpu.SemaphoreType.DMA((2, 2))],
    )
    def scatter(h_hbm, idx_hbm, out_hbm, idx_v, rows_v, sem_in, sem_out):
        wid = lax.axis_index("s") * SC_CORES + lax.axis_index("c")
        pltpu.sync_copy(idx_hbm.at[wid], idx_v)

        def load(c):
            first = pl.multiple_of(wid * per_worker + c * SC_CHUNK, SC_CHUNK)
            return pltpu.async_copy(h_hbm.at[pl.ds(first, SC_CHUNK)], rows_v.at[c % 2], sem_in.at[c % 2])

        def store(c, k):
            return pltpu.async_copy(rows_v.at[c % 2], out_hbm.at[idx_v.at[k * chunks + c]], sem_out.at[c % 2, k])

        loads = {0: load(0)}
        stores = {}
        for c in range(chunks):
            loads[c].wait()
            if c >= 1:
                for st in stores[c - 1]:
                    st.wait()
            if c + 1 < chunks:
                loads[c + 1] = load(c + 1)
            stores[c] = (store(c, 0), store(c, 1))
        for st in stores[chunks - 1]:
            st.wait()

    return scatter(h2, idx)


def _experts_kernel(blk_ref, exp_ref, valid_ref, slot_ref, next_ref, used_ref, x_ref, w1_hbm, w3_hbm, w2_hbm, y_ref,
                    w1_f, w3_f, w2_f, w13_s, w2_s, sem, *, layer):
    j = pl.program_id(0)

    def weight_copies(e, slot):
        return (pltpu.make_async_copy(w1_hbm.at[layer, e], w1_f.at[slot], sem.at[slot, 0]),
                pltpu.make_async_copy(w3_hbm.at[layer, e], w3_f.at[slot], sem.at[slot, 1]),
                pltpu.make_async_copy(w2_hbm.at[layer, e], w2_f.at[slot], sem.at[slot, 2]))

    @pl.when(j < used_ref[0])
    def _():
        e = exp_ref[j]
        slot = slot_ref[j]

        @pl.when(j == 0)
        def _():
            for cp in weight_copies(e, slot):
                cp.start()

        @pl.when((j == 0) | (e != exp_ref[jnp.maximum(j - 1, 0)]))
        def _():
            for cp in weight_copies(e, slot):
                cp.wait()
            for c in range(D_FF_E // FF_CHUNK):
                w13_s[:, 2 * FF_CHUNK * c:2 * FF_CHUNK * c + FF_CHUNK] = \
                    w1_f[slot, :, FF_CHUNK * c:FF_CHUNK * (c + 1)].astype(BF16)
                w13_s[:, 2 * FF_CHUNK * c + FF_CHUNK:2 * FF_CHUNK * (c + 1)] = \
                    w3_f[slot, :, FF_CHUNK * c:FF_CHUNK * (c + 1)].astype(BF16)
            w2_s[...] = w2_f[slot].astype(BF16)

            @pl.when(next_ref[j] >= 0)
            def _():
                for cp in weight_copies(next_ref[j], 1 - slot):
                    cp.start()

        valid = valid_ref[j]

        def ffn(rows):
            x = _unpack_rows(x_ref[0:rows, :])
            y = None
            for c in range(D_FF_E // FF_CHUNK):
                ab = jnp.dot(x, w13_s[:, 2 * FF_CHUNK * c:2 * FF_CHUNK * (c + 1)], preferred_element_type=F32)
                a = ab[:, 0:FF_CHUNK]
                hmid = (a * jax.nn.sigmoid(a) * ab[:, FF_CHUNK:]).astype(BF16)
                part = jnp.dot(hmid, w2_s[FF_CHUNK * c:FF_CHUNK * (c + 1), :], preferred_element_type=F32)
                y = part if y is None else y + part
            row = lax.broadcasted_iota(jnp.int32, (rows, 1), 0)
            y_ref[0:rows, :] = jnp.where(row < valid, _pack_rows(y), jnp.uint32(0))
            if rows < TB_EXP:
                y_ref[rows:, :] = jnp.zeros((TB_EXP - rows, D_PACK), U32)

        sizes = [TB_EXP // d for d in EXPERT_TAIL_DIVISORS]
        for n, rows in enumerate(sizes):
            below = sizes[n + 1] if n + 1 < len(sizes) else 0

            @pl.when((valid > below) & (valid <= rows) if n else valid > below)
            def _(rows=rows):
                ffn(rows)


def _experts(blk_map, blk_exp, blk_valid, blk_slot, blk_next, n_used, x_rows, w1, w3, w2, layer):
    R, D = x_rows.shape[0], D_MODEL
    nb = R // TB_EXP
    row_block = lambda j, bm, be, bv, bs, bn, nu: (bm[j], 0)
    return pl.pallas_call(
        functools.partial(_experts_kernel, layer=layer),
        grid_spec=pltpu.PrefetchScalarGridSpec(
            num_scalar_prefetch=6,
            grid=(nb,),
            in_specs=[
                pl.BlockSpec((TB_EXP, D_PACK), row_block),
                pl.BlockSpec(memory_space=pl.ANY), pl.BlockSpec(memory_space=pl.ANY), pl.BlockSpec(memory_space=pl.ANY),
            ],
            out_specs=pl.BlockSpec((TB_EXP, D_PACK), row_block),
            scratch_shapes=[
                pltpu.VMEM((2, D, D_FF_E), F32), pltpu.VMEM((2, D, D_FF_E), F32), pltpu.VMEM((2, D_FF_E, D), F32),
                pltpu.VMEM((D, 2 * D_FF_E), BF16), pltpu.VMEM((D_FF_E, D), BF16),
                pltpu.SemaphoreType.DMA((2, 3)),
            ],
        ),
        out_shape=jax.ShapeDtypeStruct((R, D_PACK), U32),
        compiler_params=_params("arbitrary"),
    )(blk_map, blk_exp, blk_valid, blk_slot, blk_next, n_used, x_rows, w1, w3, w2)


def _combine_kernel(src_ref, nslot_ref, x1_ref, route_ref, off_ref, rows_ref, o_ref, buf, sem, *, tm):
    i = pl.program_id(0)
    cur = i % 2

    def slot_copy(tile, half, s):
        src = pl.multiple_of(src_ref[tile * RUN_SLOTS + s], RUN_ROWS)
        dst = buf.at[half, pl.ds(pl.multiple_of(s * RUN_ROWS, RUN_ROWS), RUN_ROWS)]
        return pltpu.make_async_copy(rows_ref.at[pl.ds(src, RUN_ROWS)], dst, sem.at[half])

    def fetch(tile, half):
        def issue(g, c):
            for u in range(RUN_GROUP):
                slot_copy(tile, half, g * RUN_GROUP + u).start()
            return c
        lax.fori_loop(0, nslot_ref[tile], issue, 0)

    def wait(g, c):
        rows = RUN_GROUP * RUN_ROWS
        pltpu.make_async_copy(rows_ref.at[pl.ds(0, rows)], buf.at[cur, pl.ds(0, rows)], sem.at[cur]).wait()
        return c

    @pl.when(i == 0)
    def _():
        buf[...] = jnp.zeros_like(buf)
        fetch(0, 0)

    @pl.when(i + 1 < pl.num_programs(0))
    def _():
        fetch(i + 1, 1 - cur)

    route = route_ref[...]
    lane = lax.broadcasted_iota(jnp.int32, (tm, LANES), 1).astype(F32)
    off = off_ref[0]

    def buf_row(e_lane, r_lane):
        e = route[:, e_lane:e_lane + 1]
        base = jnp.sum(jnp.where(lane == e, off, 0.0), axis=-1, keepdims=True)
        return (base + route[:, r_lane:r_lane + 1]).astype(jnp.int32)

    col = lax.broadcasted_iota(jnp.int32, (tm, RUN_SLOTS * RUN_ROWS), 1)
    pick = (jnp.where(col == buf_row(ROUTE_E1, ROUTE_R1), route[:, ROUTE_G1:ROUTE_G1 + 1], 0.0)
            + jnp.where(col == buf_row(ROUTE_E2, ROUTE_R2), route[:, ROUTE_G2:ROUTE_G2 + 1], 0.0))
    lax.fori_loop(0, nslot_ref[i], wait, 0)
    o_ref[...] = x1_ref[...] + jnp.dot(pick.astype(BF16), _unpack_rows(buf[cur]), preferred_element_type=F32)


def _combine(src, nslot, x1, route, off, y_rows):
    N, D = x1.shape
    tm = min(TM_MOVE, N)
    return pl.pallas_call(
        functools.partial(_combine_kernel, tm=tm),
        grid_spec=pltpu.PrefetchScalarGridSpec(
            num_scalar_prefetch=2,
            grid=(N // tm,),
            in_specs=[
                pl.BlockSpec((tm, D), lambda i, s, n: (i, 0)),
                pl.BlockSpec((tm, LANES), lambda i, s, n: (i, 0)),
                pl.BlockSpec((1, 1, LANES), lambda i, s, n: (i, 0, 0)),
                pl.BlockSpec(memory_space=pl.ANY),
            ],
            out_specs=pl.BlockSpec((tm, D), lambda i, s, n: (i, 0)),
            scratch_shapes=[pltpu.VMEM((2, RUN_SLOTS * RUN_ROWS, D_PACK), U32), pltpu.SemaphoreType.DMA((2,))],
        ),
        out_shape=jax.ShapeDtypeStruct((N, D), F32),
        compiler_params=_params("arbitrary"),
    )(src, nslot, x1, route, off, y_rows)


def _moe(x1, h2, route, route_t, counts, sub_carry, w1, w3, w2, layer):
    N, _ = x1.shape
    tm = min(TM_MOVE, N)
    n_tiles = N // tm
    experts = jnp.arange(N_EXPERTS, dtype=jnp.int32)
    nb = (2 * N + N_EXPERTS * (TB_EXP - 1)) // TB_EXP + 1
    e_id = route_t[ROUTE_E1:ROUTE_E2 + 1].astype(jnp.int32)
    rank = route_t[ROUTE_R1:ROUTE_R2 + 1].astype(jnp.int32)
    cnt = counts[0, :N_EXPERTS].astype(jnp.int32)
    padded = (cnt + TB_EXP - 1) // TB_EXP * TB_EXP
    pad_end = jnp.cumsum(padded)
    pad_start = pad_end - padded
    e_flat = e_id.reshape(1, 2 * N)
    dest = jnp.sum(jnp.where(e_flat == experts[:, None], pad_start[:, None], 0), axis=0) + rank.reshape(2 * N)
    n_used = pad_end[-1] // TB_EXP
    blk_map = jnp.minimum(jnp.arange(nb, dtype=jnp.int32), jnp.maximum(n_used - 1, 0))
    blk_exp = jnp.sum(pad_end[None, :] <= (blk_map * TB_EXP)[:, None], axis=1).astype(jnp.int32)
    blk_exp = jnp.minimum(blk_exp, N_EXPERTS - 1)
    is_blk_e = blk_exp[:, None] == experts[None, :]
    blk_valid = jnp.clip(jnp.sum(jnp.where(is_blk_e, (pad_start + cnt)[None, :], 0), axis=1) - blk_map * TB_EXP,
                         0, TB_EXP).astype(jnp.int32)
    has_rows = cnt > 0
    slot_of_e = (jnp.cumsum(has_rows.astype(jnp.int32)) - 1) % 2
    later = jnp.where(has_rows, experts, N_EXPERTS)
    next_of_e = lax.cummin(jnp.concatenate([later[1:], jnp.full((1,), N_EXPERTS, jnp.int32)]), reverse=True)
    next_of_e = jnp.where(next_of_e < N_EXPERTS, next_of_e, -1)
    blk_slot = jnp.sum(jnp.where(is_blk_e, slot_of_e[None, :], 0), axis=1).astype(jnp.int32)
    blk_next = jnp.sum(jnp.where(is_blk_e, next_of_e[None, :], 0), axis=1).astype(jnp.int32)

    per_router_tile = TM_OUT // tm if N >= TM_OUT else 1
    carry = sub_carry.reshape(-1, SUBLANES, LANES)[:, :per_router_tile, :N_EXPERTS].reshape(n_tiles, N_EXPERTS)
    carry = carry.astype(jnp.int32)
    tile_cnt = jnp.concatenate([carry[1:], cnt[None, :]], axis=0) - carry
    run_start = pad_start[None, :] + carry
    first_blk = run_start // RUN_ROWS
    nslot = jnp.where(tile_cnt > 0, (run_start + tile_cnt - 1) // RUN_ROWS - first_blk + 1, 0)
    slot_end = jnp.cumsum(nslot, axis=1)
    slot_base = slot_end - nslot
    slots = jnp.arange(RUN_SLOTS, dtype=jnp.int32)
    slot_e = jnp.minimum(jnp.sum(slot_end[:, None, :] <= slots[None, :, None], axis=2), N_EXPERTS - 1)
    is_slot_e = slot_e[:, :, None] == experts[None, None, :]
    pick = lambda tbl: jnp.sum(jnp.where(is_slot_e, tbl[:, None, :], 0), axis=2)
    src = (pick(first_blk) + slots[None, :] - pick(slot_base)) * RUN_ROWS
    src = jnp.where(slots[None, :] < slot_end[:, -1:], src, 0)
    src = jnp.clip(src, 0, nb * TB_EXP - RUN_ROWS).reshape(-1).astype(jnp.int32)
    n_groups = (slot_end[:, -1] + RUN_GROUP - 1) // RUN_GROUP
    off = (slot_base - first_blk) * RUN_ROWS + pad_start[None, :]
    off = jnp.pad(off.astype(F32), ((0, 0), (0, LANES - N_EXPERTS)))

    x_rows = _dispatch(dest, h2, nb * TB_EXP)
    y_rows = _experts(blk_map, blk_exp, blk_valid, blk_slot, blk_next, n_used.reshape(1).astype(jnp.int32),
                      x_rows, w1, w3, w2, layer)
    return _combine(src, n_groups.astype(jnp.int32), x1, route, off.reshape(n_tiles, 1, LANES), y_rows)


def _pad_heads(w, width):
    k = w.shape[0]
    w = w.reshape(k, N_HEADS, width)
    return jnp.pad(w, ((0, 0), (0, 0), (0, HEAD_PAD - width))).reshape(k, N_HEADS * HEAD_PAD)


def _swap_rope(w):
    half = QK_ROPE // 2
    lo, hi = w[..., QK_NOPE:QK_NOPE + half], w[..., QK_NOPE + half:QK_HEAD]
    pad = [(0, 0)] * (w.ndim - 1)
    return jnp.pad(jnp.concatenate([hi, lo], -1), pad + [(QK_NOPE, HEAD_PAD - QK_HEAD)])


def _rope_tables(positions):
    inv_freq = ROPE_THETA ** (-jnp.arange(0, QK_ROPE, 2, dtype=F32) / QK_ROPE)
    ang = positions.astype(F32)[..., None] * inv_freq
    table = jnp.concatenate([jnp.cos(ang), jnp.sin(ang)], -1)
    return jnp.pad(table, ((0, 0), (0, 0), (QK_NOPE, LANES - QK_HEAD)))


def _ssm_params(lam_re, lam_im, b_re, b_im, c_re, c_im, log_dt):
    lam = lax.complex(lam_re, lam_im)
    dt = jnp.exp(log_dt)[:, None]
    lam_bar = jnp.exp(lam * dt)
    b_bar = ((lam_bar - 1.0) / lam)[..., None] * lax.complex(b_re, b_im)
    eye = jnp.eye(SSM_GROUPS, dtype=F32)

    def in_blockdiag(m):
        return jnp.einsum("gpc,gh->gchp", m, eye).reshape(SSM_WIDTH, SSM_FLAT)

    def out_blockdiag(m):
        return jnp.einsum("gcp,gh->gphc", m, eye).reshape(SSM_FLAT, SSM_WIDTH)

    bbd = jnp.concatenate([in_blockdiag(jnp.real(b_bar)), in_blockdiag(jnp.imag(b_bar))], axis=1)
    cbd = jnp.concatenate([out_blockdiag(c_re), out_blockdiag(-c_im)], axis=0)
    lam_rows = jnp.stack([jnp.real(lam_bar).reshape(SSM_FLAT), jnp.imag(lam_bar).reshape(SSM_FLAT)])
    return bbd.astype(BF16), lam_rows, cbd.astype(BF16)


def kernel(x, positions, mix_norm, w_in, q_a_norm, w_uq, kv_a_norm, w_ukv, q_norm, k_norm, ssm_lam_re, ssm_lam_im, ssm_b_re, ssm_b_im, ssm_c_re, ssm_c_im, ssm_d, ssm_log_dt, ssm_w_glu, conv_dw_w, conv_dw_b, conv_ln_w, conv_ln_b, out_norm, w_out, ffn_norm, w_grp, b_grp, w_exp, b_exp, w1, w3, w2):
    B, L, D = x.shape
    depth = w_in.shape[0]
    rope = _rope_tables(positions)
    row = lambda v: v.reshape(1, -1)
    lane_pad = lambda v: jnp.pad(v, (0, LANES - v.shape[0])).reshape(1, LANES)
    for l in range(depth):
        c_q, c_kv, k_pe, u_s, c_a, c_g = jnp.split(
            w_in[l], [Q_LORA, Q_LORA + KV_LORA, Q_LORA + KV_LORA + QK_ROPE,
                      Q_LORA + KV_LORA + QK_ROPE + SSM_WIDTH,
                      Q_LORA + KV_LORA + QK_ROPE + SSM_WIDTH + CONV_WIDTH], axis=1)
        k_pe_full = jnp.pad(k_pe, ((0, 0), (QK_NOPE, 0)))
        win = jnp.concatenate([c_q, c_kv, jnp.pad(k_pe_full, ((0, 0), (0, HEAD_PAD - QK_HEAD))), _swap_rope(k_pe_full),
                               u_s, c_a, c_g], axis=1).astype(BF16)
        wkv = w_ukv[l].reshape(KV_LORA, N_HEADS, QK_NOPE + V_HEAD)
        wuk = _pad_heads(wkv[:, :, :QK_NOPE].reshape(KV_LORA, N_HEADS * QK_NOPE), QK_NOPE).astype(BF16)
        wuv = wkv[:, :, QK_NOPE:].reshape(KV_LORA, ATTN_WIDTH).astype(BF16)
        wuq_sw = _swap_rope(w_uq[l].reshape(Q_LORA, N_HEADS, QK_HEAD)).reshape(Q_LORA, N_HEADS * HEAD_PAD)
        wuq = jnp.concatenate([_pad_heads(w_uq[l], QK_HEAD), wuq_sw], axis=1).astype(BF16)
        qn_scaled = q_norm[l] * (QK_HEAD ** -0.5)
        norm_rows = lambda w: jnp.stack([jnp.pad(w, (0, HEAD_PAD - QK_HEAD)), _swap_rope(w)])
        q, k, v, u_tm, ca, cg = _in_proj(
            x, rope, row(mix_norm[l]), win, row(q_a_norm[l]), wuq, row(kv_a_norm[l]), wuk, wuv,
            norm_rows(qn_scaled), norm_rows(k_norm[l]))
        y_attn = _attention(q, k, v)

        g_out = out_norm[l]
        bbd, lam_rows, cbd = _ssm_params(ssm_lam_re[l], ssm_lam_im[l], ssm_b_re[l], ssm_b_im[l],
                                         ssm_c_re[l], ssm_c_im[l], ssm_log_dt[l])
        y_ssm_tm = _ssm(u_tm.reshape(L * B, SSM_WIDTH), bbd, lam_rows, cbd, row(ssm_d[l]),
                        ssm_w_glu[l].astype(BF16), row(g_out[ATTN_WIDTH:ATTN_WIDTH + SSM_WIDTH]), B)
        conv_params = (conv_dw_w[l], row(conv_dw_b[l]), row(conv_ln_w[l]), row(conv_ln_b[l]),
                       row(g_out[ATTN_WIDTH + SSM_WIDTH:]))

        w_route = jnp.pad(jnp.concatenate([w_exp[l], w_grp[l]], axis=1), ((0, 0), (0, LANES - N_EXPERTS - N_EGROUPS)))
        w_route_hi = w_route.astype(BF16)
        w_route = jnp.concatenate([w_route_hi, (w_route - w_route_hi.astype(F32)).astype(BF16)], axis=1)
        b_route = lane_pad(jnp.concatenate([b_exp[l], b_grp[l]]))
        x1, h2, route, route_t, counts, sub_carry = _out_router(
            x.reshape(B * L, D), y_attn.reshape(B * L, ATTN_WIDTH), y_ssm_tm.reshape(L, B * SSM_WIDTH),
            ca.reshape(B * L, CONV_WIDTH), cg.reshape(B * L, CONV_WIDTH), row(g_out[:ATTN_WIDTH]),
            w_out[l].astype(BF16), row(ffn_norm[l]), w_route, b_route, conv_params, B)
        x = _moe(x1, h2, route, route_t, counts, sub_carry, w1, w3, w2, l).reshape(B, L, D)
    return x
```

```python
import functools

import jax
import jax.numpy as jnp
from jax import lax
from jax.experimental import pallas as pl
from jax.experimental.pallas import tpu as pltpu
from jax.experimental.pallas import tpu_sc as plsc

D_MODEL = 1024
CHUNK = 64
EPS = 1e-6
LN_EPS = 1e-5
N_HEADS = 8
QK_NOPE = 64
QK_ROPE = 32
QK_HEAD = QK_NOPE + QK_ROPE
V_HEAD = 64
Q_LORA = 256
KV_LORA = 128
ROPE_THETA = 10000.0
ATTN_WIDTH = N_HEADS * V_HEAD
SSM_WIDTH = 256
SSM_GROUP = 16
SSM_GROUPS = SSM_WIDTH // SSM_GROUP
SSM_STATE = 64
SSM_FLAT = SSM_GROUPS * SSM_STATE
CONV_WIDTH = 256
CONV_K = 31
N_EGROUPS = 4
EXP_PER_GROUP = 8
N_EXPERTS = N_EGROUPS * EXP_PER_GROUP
D_FF_E = 512

LANES = 128
SUBLANES = 8
HEAD_PAD = LANES
IN_PROJ_PAD = Q_LORA + KV_LORA + SSM_WIDTH + 2 * CONV_WIDTH + 2 * LANES
CONV_HALO = 32
VMEM_LIMIT = 48 * 1024 * 1024

TM_PROJ = 512
TQ_ATTN = 256
TC_SSM = 128
SSM_PARTS = 2
CONV_SUB = 128
TM_OUT = 512
TM_MOVE = 256
RUN_ROWS = SUBLANES
RUN_SLOTS = 2 * N_EXPERTS + 2 * TM_MOVE // RUN_ROWS
RUN_GROUP = 8
SC_CORES, SC_SUBCORES = 2, 16
SC_CHUNK = 64
TB_EXP = 512
EXPERT_TAIL_DIVISORS = (1, 2, 4)
FF_CHUNK = 256

BF16 = jnp.bfloat16
F32 = jnp.float32
U32 = jnp.uint32
D_PACK = D_MODEL // 2


def _pack_rows(v):
    bits = lax.bitcast_convert_type(v.astype(BF16).astype(F32), U32)
    half = v.shape[1] // 2
    return bits[:, 0:half] | (bits[:, half:] >> 16)


def _unpack_rows(w):
    hi = lax.bitcast_convert_type(w & jnp.uint32(0xFFFF0000), F32)
    lo = lax.bitcast_convert_type(w << 16, F32)
    return jnp.concatenate([hi, lo], axis=1).astype(BF16)


def _rms(x, w):
    return x * lax.rsqrt(jnp.mean(x * x, axis=-1, keepdims=True) + EPS) * w


def _params(*sem):
    return pltpu.CompilerParams(dimension_semantics=sem, vmem_limit_bytes=VMEM_LIMIT)


def _in_proj_kernel(x_ref, rope_ref, mixn_ref, win_ref, qan_ref, wuq_ref, kvan_ref, wuk_ref, wuv_ref,
                    qn_ref, kn_ref, q_ref, k_ref, v_ref, u_ref, ca_ref, cg_ref):
    x = x_ref[0]
    h = _rms(x, mixn_ref[...]).astype(BF16)
    head_cols = Q_LORA + KV_LORA + 2 * LANES
    proj = jnp.dot(h, win_ref[:, 0:head_cols], preferred_element_type=F32)
    o = 0
    c_q = proj[:, o:o + Q_LORA]; o += Q_LORA
    c_kv = proj[:, o:o + KV_LORA]; o += KV_LORA
    k_pe = proj[:, o:o + LANES]; o += LANES
    k_pe_sw = proj[:, o:o + LANES]

    width = N_HEADS * HEAD_PAD
    q2 = jnp.dot(_rms(c_q, qan_ref[...]).astype(BF16), wuq_ref[...], preferred_element_type=F32)
    q, q_sw = q2[:, 0:width], q2[:, width:2 * width]
    ckv_n = _rms(c_kv, kvan_ref[...]).astype(BF16)
    kn = jnp.dot(ckv_n, wuk_ref[...], preferred_element_type=F32)
    v_ref[0] = jnp.dot(ckv_n, wuv_ref[...], preferred_element_type=F32).astype(BF16)

    rope = rope_ref[0]
    half = QK_ROPE // 2
    lane = lax.broadcasted_iota(jnp.int32, rope.shape, 1)
    lo = (lane >= QK_NOPE) & (lane < QK_NOPE + half)
    hi = (lane >= QK_NOPE + half) & (lane < QK_HEAD)
    cos_t = jnp.where(lane < QK_NOPE, 1.0, jnp.where(lo, rope, jnp.where(hi, pltpu.roll(rope, half, 1), 0.0)))
    sin_t = jnp.where(lo, -pltpu.roll(rope, LANES - half, 1), jnp.where(hi, rope, 0.0))
    a_q, b_q = qn_ref[0:1, :] * cos_t, qn_ref[1:2, :] * sin_t
    a_k = kn_ref[0:1, :] * cos_t
    k_sw_term = k_pe_sw * (kn_ref[1:2, :] * sin_t)

    def inv_rms(y):
        return lax.rsqrt(jnp.sum(y * y, axis=-1, keepdims=True) * (1.0 / QK_HEAD) + EPS)

    def rest(j):
        cols = slice(head_cols + j * SSM_WIDTH, head_cols + (j + 1) * SSM_WIDTH)
        return jnp.dot(h, win_ref[:, cols], preferred_element_type=F32).astype(BF16)

    for hd in range(N_HEADS):
        if hd == 0:
            u_ref[...] = rest(0)
        elif hd == 3:
            ca_ref[0] = rest(1)
        elif hd == 6:
            cg_ref[0] = rest(2)
        sl = slice(hd * HEAD_PAD, (hd + 1) * HEAD_PAD)
        yq = q[:, sl]
        q_ref[0, hd] = ((yq * a_q + q_sw[:, sl] * b_q) * inv_rms(yq)).astype(BF16)
        yk = kn[:, sl] + k_pe
        k_ref[0, hd] = ((yk * a_k + k_sw_term) * inv_rms(yk)).astype(BF16)


def _in_proj(x, rope, mixn, win, qan, wuq, kvan, wuk, wuv, qn, kn):
    B, L, D = x.shape
    tm = min(TM_PROJ, L)
    full = lambda shape: pl.BlockSpec(shape, lambda b, t: (0,) * len(shape))
    return pl.pallas_call(
        _in_proj_kernel,
        grid=(B, L // tm),
        in_specs=[
            pl.BlockSpec((1, tm, D), lambda b, t: (b, t, 0)),
            pl.BlockSpec((1, tm, LANES), lambda b, t: (b, t, 0)),
            full((1, D)), full((D, IN_PROJ_PAD)), full((1, Q_LORA)), full((Q_LORA, 2 * N_HEADS * HEAD_PAD)),
            full((1, KV_LORA)), full((KV_LORA, N_HEADS * HEAD_PAD)), full((KV_LORA, ATTN_WIDTH)),
            full((2, HEAD_PAD)), full((2, HEAD_PAD)),
        ],
        out_specs=[
            pl.BlockSpec((1, N_HEADS, tm, HEAD_PAD), lambda b, t: (b, 0, t, 0)),
            pl.BlockSpec((1, N_HEADS, tm, HEAD_PAD), lambda b, t: (b, 0, t, 0)),
            pl.BlockSpec((1, tm, ATTN_WIDTH), lambda b, t: (b, t, 0)),
            pl.BlockSpec((tm, SSM_WIDTH), lambda b, t: (t, b)),
            pl.BlockSpec((1, tm, CONV_WIDTH), lambda b, t: (b, t, 0)),
            pl.BlockSpec((1, tm, CONV_WIDTH), lambda b, t: (b, t, 0)),
        ],
        out_shape=[
            jax.ShapeDtypeStruct((B, N_HEADS, L, HEAD_PAD), BF16),
            jax.ShapeDtypeStruct((B, N_HEADS, L, HEAD_PAD), BF16),
            jax.ShapeDtypeStruct((B, L, ATTN_WIDTH), BF16),
            jax.ShapeDtypeStruct((L, B * SSM_WIDTH), BF16),
            jax.ShapeDtypeStruct((B, L, CONV_WIDTH), BF16),
            jax.ShapeDtypeStruct((B, L, CONV_WIDTH), BF16),
        ],
        compiler_params=_params("parallel", "parallel"),
    )(x, rope, mixn, win, qan, wuq, kvan, wuk, wuv, qn, kn)


def _attention_kernel(q_ref, k_ref, v_ref, o_ref, vext_ref, *, seq, tq):
    pair = 2 * V_HEAD
    vext_ref[:, 0:pair] = v_ref[0]
    vext_ref[:, pair:] = jnp.ones((seq, LANES), BF16)
    row_chunk = lax.broadcasted_iota(jnp.int32, (tq, tq), 0) // CHUNK
    col_chunk = lax.broadcasted_iota(jnp.int32, (tq, tq), 1) // CHUNK
    visible = col_chunk <= row_chunk
    neg = jnp.finfo(F32).min
    nt = (((1,), (1,)), ((), ()))
    low_lanes = lax.broadcasted_iota(jnp.int32, (tq, pair), 1) < V_HEAD
    def scores(i, hh):
        q0 = i * tq
        qb = q_ref[0, hh, q0:q0 + tq, :]
        s_d = lax.dot_general(qb, k_ref[0, hh, q0:q0 + tq, :], nt, preferred_element_type=F32)
        s_d = jnp.where(visible, s_d, neg)
        s_l = lax.dot_general(qb, k_ref[0, hh, 0:q0, :], nt, preferred_element_type=F32) if i > 0 else None
        return s_d, s_l

    def weighted_values(i, s_d, s_l):
        q0 = i * tq
        m = jnp.max(s_d, axis=-1, keepdims=True)
        if s_l is not None:
            m = jnp.maximum(m, jnp.max(s_l, axis=-1, keepdims=True))
        acc = jnp.dot(jnp.exp(s_d - m).astype(BF16), vext_ref[q0:q0 + tq, :], preferred_element_type=F32)
        if s_l is not None:
            acc = acc + jnp.dot(jnp.exp(s_l - m).astype(BF16), vext_ref[0:q0, :], preferred_element_type=F32)
        return acc[:, 0:pair] / acc[:, pair:]

    nq = seq // tq
    order = [x for p in zip(reversed(range(nq)), range(nq)) for x in p][:nq]
    chains = [(i, hh) for i in order for hh in range(2)]
    pending = scores(*chains[0])
    outs = {}
    for n, (i, hh) in enumerate(chains):
        upcoming = scores(*chains[n + 1]) if n + 1 < len(chains) else None
        outs[hh] = weighted_values(i, *pending)
        pending = upcoming
        if hh == 1:
            o_ref[0, i * tq:(i + 1) * tq, :] = jnp.where(low_lanes, outs[0], outs[1]).astype(BF16)


def _attention(q, k, v):
    B, H, L, _ = q.shape
    tq = min(TQ_ATTN, L)
    return pl.pallas_call(
        functools.partial(_attention_kernel, seq=L, tq=tq),
        grid=(B, H // 2),
        in_specs=[
            pl.BlockSpec((1, 2, L, HEAD_PAD), lambda b, p: (b, p, 0, 0)),
            pl.BlockSpec((1, 2, L, HEAD_PAD), lambda b, p: (b, p, 0, 0)),
            pl.BlockSpec((1, L, 2 * V_HEAD), lambda b, p: (b, 0, p)),
        ],
        out_specs=pl.BlockSpec((1, L, 2 * V_HEAD), lambda b, p: (b, 0, p)),
        out_shape=jax.ShapeDtypeStruct((B, L, ATTN_WIDTH), BF16),
        scratch_shapes=[pltpu.VMEM((L, 2 * V_HEAD + LANES), BF16)],
        compiler_params=_params("parallel", "parallel"),
    )(q, k, v)


def _ssm_kernel(u_ref, bbd_ref, lam_ref, cbd_ref, d_ref, wglu_ref, g_ref, o_ref, state_ref, *bu_refs, batch, tc):
    @pl.when(pl.program_id(0) == 0)
    def _():
        state_ref[...] = jnp.zeros_like(state_ref)

    steps = tc // SSM_PARTS
    halves = tuple((bu, slice(p * steps * batch, (p + 1) * steps * batch)) for p, bu in enumerate(bu_refs))
    for bu_ref, rows in halves:
        bu_ref[...] = jnp.dot(u_ref[rows, :], bbd_ref[...], preferred_element_type=F32)
    lam_re = jnp.broadcast_to(lam_ref[0:1, :], (batch, SSM_FLAT))
    lam_im = jnp.broadcast_to(lam_ref[1:2, :], (batch, SSM_FLAT))

    carry = (state_ref[:, 0:SSM_FLAT], state_ref[:, SSM_FLAT:2 * SSM_FLAT])
    for bu_ref, rows in halves:
        def step(t, xs, bu_ref=bu_ref):
            xr, xi = xs
            at = pl.ds(pl.multiple_of(t * batch, batch), batch)
            nr = lam_re * xr - lam_im * xi + bu_ref[at, 0:SSM_FLAT]
            ni = lam_re * xi + lam_im * xr + bu_ref[at, SSM_FLAT:2 * SSM_FLAT]
            bu_ref[at, 0:SSM_FLAT] = nr
            bu_ref[at, SSM_FLAT:2 * SSM_FLAT] = ni
            return nr, ni

        carry = lax.fori_loop(0, steps, step, carry, unroll=True)
        y = jnp.dot(bu_ref[...].astype(BF16), cbd_ref[...], preferred_element_type=F32)
        y = y + d_ref[...] * u_ref[rows, :].astype(F32)
        z = jax.nn.gelu(y)
        gate = jax.nn.sigmoid(jnp.dot(z.astype(BF16), wglu_ref[...], preferred_element_type=F32))
        o_ref[rows, :] = _rms(z * gate, g_ref[...]).astype(BF16)
    state_ref[:, 0:SSM_FLAT] = carry[0]
    state_ref[:, SSM_FLAT:2 * SSM_FLAT] = carry[1]


def _ssm(u_tm, bbd, lam, cbd, d, wglu, g, batch):
    rows = u_tm.shape[0]
    seq = rows // batch
    tc = min(TC_SSM, seq)
    blk = tc * batch
    full = lambda shape: pl.BlockSpec(shape, lambda t: (0,) * len(shape))
    return pl.pallas_call(
        functools.partial(_ssm_kernel, batch=batch, tc=tc),
        grid=(seq // tc,),
        in_specs=[
            pl.BlockSpec((blk, SSM_WIDTH), lambda t: (t, 0)),
            full((SSM_WIDTH, 2 * SSM_FLAT)), full((2, SSM_FLAT)), full((2 * SSM_FLAT, SSM_WIDTH)),
            full((1, SSM_WIDTH)), full((SSM_WIDTH, SSM_WIDTH)), full((1, SSM_WIDTH)),
        ],
        out_specs=pl.BlockSpec((blk, SSM_WIDTH), lambda t: (t, 0)),
        out_shape=jax.ShapeDtypeStruct((rows, SSM_WIDTH), BF16),
        scratch_shapes=[pltpu.VMEM((batch, 2 * SSM_FLAT), F32)]
        + [pltpu.VMEM((blk // SSM_PARTS, 2 * SSM_FLAT), F32)] * SSM_PARTS,
        compiler_params=_params("arbitrary"),
    )(u_tm, bbd, lam, cbd, d, wglu, g)


def _conv_rows(upad_ref, r0, w_ref, b_ref, lnw_ref, lnb_ref, gn_ref):
    base = CONV_HALO - (CONV_K - 1)
    acc = jnp.zeros((CONV_SUB, CONV_WIDTH), F32)
    for off in range(SUBLANES):
        n = CONV_SUB + (SUBLANES if off else 0)
        part = None
        for a8 in range(0, CONV_HALO + 1, SUBLANES):
            kk = a8 + off - base
            if 0 <= kk < CONV_K:
                term = w_ref[kk:kk + 1, :] * upad_ref[r0 + a8:r0 + a8 + n, :]
                part = term if part is None else part + term
        acc = acc + part[off:off + CONV_SUB]
    y = acc + b_ref[...]
    mu = jnp.mean(y, axis=-1, keepdims=True)
    var = jnp.mean(jnp.square(y - mu), axis=-1, keepdims=True)
    y = (y - mu) * lax.rsqrt(var + LN_EPS) * lnw_ref[...] + lnb_ref[...]
    y = y * jax.nn.sigmoid(y)
    return _rms(y, gn_ref[...])


ROUTE_E1, ROUTE_E2, ROUTE_R1, ROUTE_R2, ROUTE_G1, ROUTE_G2 = range(6)
ROUTE_ROWS = 8
GRP_LANE0 = N_EXPERTS


def _out_router_kernel(x_ref, ya_ref, ys_ref, ca_ref, cg_ref, ga_ref, wo_ref, fn_ref, wr_ref, br_ref,
                       cw_ref, cb_ref, clnw_ref, clnb_ref, cgn_ref,
                       x1_ref, h2_ref, route_ref, route_t_ref, cnt_ref, sub_ref,
                       carry_ref, tri_ref, upad_ref, yc_ref, *, tm, per_b):
    @pl.when(pl.program_id(0) == 0)
    def _():
        carry_ref[...] = jnp.zeros_like(carry_ref)
        rr = lax.broadcasted_iota(jnp.int32, (tm, tm), 0)
        cc = lax.broadcasted_iota(jnp.int32, (tm, tm), 1)
        tri_ref[...] = (cc < rr).astype(BF16)

    first = pl.program_id(0) % per_b == 0
    upad_ref[0:CONV_HALO, :] = jnp.where(first, 0.0, upad_ref[tm:tm + CONV_HALO, :])
    upad_ref[CONV_HALO:, :] = ca_ref[...].astype(F32) * jax.nn.sigmoid(cg_ref[...].astype(F32))
    ya = _rms(ya_ref[...].astype(F32), ga_ref[...]).astype(BF16)
    acc = jnp.dot(ya, wo_ref[0:ATTN_WIDTH, :], preferred_element_type=F32)
    acc += jnp.dot(ys_ref[...], wo_ref[ATTN_WIDTH:ATTN_WIDTH + SSM_WIDTH, :], preferred_element_type=F32)
    for r0 in range(0, tm, CONV_SUB):
        yc_ref[r0:r0 + CONV_SUB, :] = _conv_rows(upad_ref, r0, cw_ref, cb_ref, clnw_ref, clnb_ref, cgn_ref).astype(BF16)
    acc += jnp.dot(yc_ref[...], wo_ref[ATTN_WIDTH + SSM_WIDTH:, :], preferred_element_type=F32)
    x1 = x_ref[...] + acc
    x1_ref[...] = x1
    h2 = _rms(x1, fn_ref[...])
    h2_ref[...] = _pack_rows(h2)

    h_hi = h2.astype(BF16)
    h_lo = (h2 - h_hi.astype(F32)).astype(BF16)
    part = jnp.dot(h_hi, wr_ref[...], preferred_element_type=F32)
    logits = (part[:, 0:LANES] + part[:, LANES:2 * LANES]
              + jnp.dot(h_lo, wr_ref[:, 0:LANES], preferred_element_type=F32) + br_ref[...])
    lane = lax.broadcasted_iota(jnp.int32, (tm, LANES), 1)
    ninf = -jnp.inf
    big = LANES

    def first_argmax(vals, vmax):
        return jnp.min(jnp.where(vals == vmax, lane, big), axis=-1, keepdims=True)

    grp = jnp.where((lane >= GRP_LANE0) & (lane < GRP_LANE0 + N_EGROUPS), logits, ninf)
    gmax = jnp.max(grp, axis=-1, keepdims=True)
    gsel = first_argmax(grp, gmax) - GRP_LANE0
    p_grp = 1.0 / jnp.sum(jnp.exp(grp - gmax), axis=-1, keepdims=True)

    el = jnp.where((lane < N_EXPERTS) & ((lane // EXP_PER_GROUP) == gsel), logits, ninf)
    m1 = jnp.max(el, axis=-1, keepdims=True)
    e1 = first_argmax(el, m1)
    el2 = jnp.where(lane == e1, ninf, el)
    m2 = jnp.max(el2, axis=-1, keepdims=True)
    e2 = first_argmax(el2, m2)
    t2 = jnp.exp(m2 - m1)
    g1 = p_grp / (1.0 + t2)
    g2 = p_grp * t2 / (1.0 + t2)

    hit1 = lane == e1
    hit2 = lane == e2
    cnt = (hit1 | hit2).astype(F32)
    before = jnp.dot(tri_ref[...], cnt.astype(BF16), preferred_element_type=F32) + carry_ref[...]
    r1 = jnp.sum(jnp.where(hit1, before, 0.0), axis=-1, keepdims=True)
    r2 = jnp.sum(jnp.where(hit2, before, 0.0), axis=-1, keepdims=True)
    subs = [before[h * TM_MOVE:h * TM_MOVE + 1, :] for h in range(tm // TM_MOVE)]
    sub_ref[...] = jnp.concatenate(subs + [jnp.zeros((SUBLANES - len(subs), LANES), F32)], axis=0)
    carry_ref[...] += jnp.sum(cnt, axis=0, keepdims=True)
    cnt_ref[...] = carry_ref[...]

    rec = jnp.where(lane == ROUTE_E1, e1.astype(F32), 0.0)
    rec = jnp.where(lane == ROUTE_E2, e2.astype(F32), rec)
    rec = jnp.where(lane == ROUTE_R1, r1, rec)
    rec = jnp.where(lane == ROUTE_R2, r2, rec)
    rec = jnp.where(lane == ROUTE_G1, g1, rec)
    rec = jnp.where(lane == ROUTE_G2, g2, rec)
    route_ref[...] = rec
    route_t_ref[...] = rec.T[0:ROUTE_ROWS, :]


def _out_router(x, ya, ys_tm, ca, cg, ga, wo, fn, wr, br, conv_params, batch):
    N, D = x.shape
    seq = N // batch
    tm = min(TM_OUT, seq)
    per_b = seq // tm
    full = lambda shape: pl.BlockSpec(shape, lambda i: (0,) * len(shape))
    tile = lambda w: pl.BlockSpec((tm, w), lambda i: (i, 0))
    return pl.pallas_call(
        functools.partial(_out_router_kernel, tm=tm, per_b=per_b),
        grid=(N // tm,),
        in_specs=[
            tile(D), tile(ATTN_WIDTH),
            pl.BlockSpec((tm, SSM_WIDTH), lambda i: (i % per_b, i // per_b)),
            tile(CONV_WIDTH), tile(CONV_WIDTH),
            full((1, ATTN_WIDTH)), full((D, D)), full((1, D)), full((D, 2 * LANES)), full((1, LANES)),
            full((CONV_K, CONV_WIDTH)), full((1, CONV_WIDTH)), full((1, CONV_WIDTH)), full((1, CONV_WIDTH)),
            full((1, CONV_WIDTH)),
        ],
        out_specs=[tile(D), tile(D_PACK), tile(LANES), pl.BlockSpec((ROUTE_ROWS, tm), lambda i: (0, i)),
                   full((1, LANES)), pl.BlockSpec((SUBLANES, LANES), lambda i: (i, 0))],
        out_shape=[
            jax.ShapeDtypeStruct((N, D), F32),
            jax.ShapeDtypeStruct((N, D_PACK), U32),
            jax.ShapeDtypeStruct((N, LANES), F32),
            jax.ShapeDtypeStruct((ROUTE_ROWS, N), F32),
            jax.ShapeDtypeStruct((1, LANES), F32),
            jax.ShapeDtypeStruct((N // tm * SUBLANES, LANES), F32),
        ],
        scratch_shapes=[pltpu.VMEM((1, LANES), F32), pltpu.VMEM((tm, tm), BF16),
                        pltpu.VMEM((tm + CONV_HALO, CONV_WIDTH), F32), pltpu.VMEM((tm, CONV_WIDTH), BF16)],
        compiler_params=_params("arbitrary"),
    )(x, ya, ys_tm, ca, cg, ga, wo, fn, wr, br, *conv_params)


def _dispatch(dest_flat, h2, n_rows):
    N, D = h2.shape
    workers = SC_CORES * SC_SUBCORES
    per_worker = N // workers
    chunks = per_worker // SC_CHUNK
    idx = dest_flat.reshape(2, workers, chunks, SC_CHUNK).transpose(1, 0, 2, 3).reshape(workers, 2 * chunks, SC_CHUNK)
    mesh = plsc.VectorSubcoreMesh(core_axis_name="c", subcore_axis_name="s")

    @functools.partial(
        pl.kernel, mesh=mesh,
        out_type=jax.ShapeDtypeStruct((n_rows, D), h2.dtype),
        scratch_types=[pltpu.VMEM((2 * chunks, SC_CHUNK), jnp.int32), pltpu.VMEM((2, SC_CHUNK, D), h2.dtype),
                       pltpu.SemaphoreType.DMA((2,)), pltpu.SemaphoreType.DMA((2, 2))],
    )
    def scatter(h_hbm, idx_hbm, out_hbm, idx_v, rows_v, sem_in, sem_out):
        wid = lax.axis_index("s") * SC_CORES + lax.axis_index("c")
        pltpu.sync_copy(idx_hbm.at[wid], idx_v)

        def load(c):
            first = pl.multiple_of(wid * per_worker + c * SC_CHUNK, SC_CHUNK)
            return pltpu.async_copy(h_hbm.at[pl.ds(first, SC_CHUNK)], rows_v.at[c % 2], sem_in.at[c % 2])

        def store(c, k):
            return pltpu.async_copy(rows_v.at[c % 2], out_hbm.at[idx_v.at[k * chunks + c]], sem_out.at[c % 2, k])

        loads = {0: load(0)}
        stores = {}
        for c in range(chunks):
            loads[c].wait()
            if c >= 1:
                for st in stores[c - 1]:
                    st.wait()
            if c + 1 < chunks:
                loads[c + 1] = load(c + 1)
            stores[c] = (store(c, 0), store(c, 1))
        for st in stores[chunks - 1]:
            st.wait()

    return scatter(h2, idx)


def _experts_kernel(blk_ref, exp_ref, valid_ref, slot_ref, next_ref, used_ref, x_ref, w1_hbm, w3_hbm, w2_hbm, y_ref,
                    w1_f, w3_f, w2_f, w13_s, w2_s, sem, *, layer):
    j = pl.program_id(0)

    def weight_copies(e, slot):
        return (pltpu.make_async_copy(w1_hbm.at[layer, e], w1_f.at[slot], sem.at[slot, 0]),
                pltpu.make_async_copy(w3_hbm.at[layer, e], w3_f.at[slot], sem.at[slot, 1]),
                pltpu.make_async_copy(w2_hbm.at[layer, e], w2_f.at[slot], sem.at[slot, 2]))

    @pl.when(j < used_ref[0])
    def _():
        e = exp_ref[j]
        slot = slot_ref[j]

        @pl.when(j == 0)
        def _():
            for cp in weight_copies(e, slot):
                cp.start()

        @pl.when((j == 0) | (e != exp_ref[jnp.maximum(j - 1, 0)]))
        def _():
            for cp in weight_copies(e, slot):
                cp.wait()
            for c in range(D_FF_E // FF_CHUNK):
                w13_s[:, 2 * FF_CHUNK * c:2 * FF_CHUNK * c + FF_CHUNK] = \
                    w1_f[slot, :, FF_CHUNK * c:FF_CHUNK * (c + 1)].astype(BF16)
                w13_s[:, 2 * FF_CHUNK * c + FF_CHUNK:2 * FF_CHUNK * (c + 1)] = \
                    w3_f[slot, :, FF_CHUNK * c:FF_CHUNK * (c + 1)].astype(BF16)
            w2_s[...] = w2_f[slot].astype(BF16)

            @pl.when(next_ref[j] >= 0)
            def _():
                for cp in weight_copies(next_ref[j], 1 - slot):
                    cp.start()

        valid = valid_ref[j]

        def ffn(rows):
            x = _unpack_rows(x_ref[0:rows, :])
            y = None
            for c in range(D_FF_E // FF_CHUNK):
                ab = jnp.dot(x, w13_s[:, 2 * FF_CHUNK * c:2 * FF_CHUNK * (c + 1)], preferred_element_type=F32)
                a = ab[:, 0:FF_CHUNK]
                hmid = (a * jax.nn.sigmoid(a) * ab[:, FF_CHUNK:]).astype(BF16)
                part = jnp.dot(hmid, w2_s[FF_CHUNK * c:FF_CHUNK * (c + 1), :], preferred_element_type=F32)
                y = part if y is None else y + part
            row = lax.broadcasted_iota(jnp.int32, (rows, 1), 0)
            y_ref[0:rows, :] = jnp.where(row < valid, _pack_rows(y), jnp.uint32(0))
            if rows < TB_EXP:
                y_ref[rows:, :] = jnp.zeros((TB_EXP - rows, D_PACK), U32)

        sizes = [TB_EXP // d for d in EXPERT_TAIL_DIVISORS]
        for n, rows in enumerate(sizes):
            below = sizes[n + 1] if n + 1 < len(sizes) else 0

            @pl.when((valid > below) & (valid <= rows) if n else valid > below)
            def _(rows=rows):
                ffn(rows)


def _experts(blk_map, blk_exp, blk_valid, blk_slot, blk_next, n_used, x_rows, w1, w3, w2, layer):
    R, D = x_rows.shape[0], D_MODEL
    nb = R // TB_EXP
    row_block = lambda j, bm, be, bv, bs, bn, nu: (bm[j], 0)
    return pl.pallas_call(
        functools.partial(_experts_kernel, layer=layer),
        grid_spec=pltpu.PrefetchScalarGridSpec(
            num_scalar_prefetch=6,
            grid=(nb,),
            in_specs=[
                pl.BlockSpec((TB_EXP, D_PACK), row_block),
                pl.BlockSpec(memory_space=pl.ANY), pl.BlockSpec(memory_space=pl.ANY), pl.BlockSpec(memory_space=pl.ANY),
            ],
            out_specs=pl.BlockSpec((TB_EXP, D_PACK), row_block),
            scratch_shapes=[
                pltpu.VMEM((2, D, D_FF_E), F32), pltpu.VMEM((2, D, D_FF_E), F32), pltpu.VMEM((2, D_FF_E, D), F32),
                pltpu.VMEM((D, 2 * D_FF_E), BF16), pltpu.VMEM((D_FF_E, D), BF16),
                pltpu.SemaphoreType.DMA((2, 3)),
            ],
        ),
        out_shape=jax.ShapeDtypeStruct((R, D_PACK), U32),
        compiler_params=_params("arbitrary"),
    )(blk_map, blk_exp, blk_valid, blk_slot, blk_next, n_used, x_rows, w1, w3, w2)


def _combine_kernel(src_ref, ngroup_ref, x1_ref, route_ref, off_ref, rows_ref, o_ref, buf, sem, *, tm):
    i = pl.program_id(0)
    cur = i % 2

    def slot_copy(tile, half, s):
        src = pl.multiple_of(src_ref[tile * RUN_SLOTS + s], RUN_ROWS)
        dst = buf.at[half, pl.ds(pl.multiple_of(s * RUN_ROWS, RUN_ROWS), RUN_ROWS)]
        return pltpu.make_async_copy(rows_ref.at[pl.ds(src, RUN_ROWS)], dst, sem.at[half])

    def fetch(tile, half):
        def issue(g, c):
            for u in range(RUN_GROUP):
                slot_copy(tile, half, g * RUN_GROUP + u).start()
            return c
        lax.fori_loop(0, ngroup_ref[tile], issue, 0)

    def wait(g, c):
        rows = RUN_GROUP * RUN_ROWS
        pltpu.make_async_copy(rows_ref.at[pl.ds(0, rows)], buf.at[cur, pl.ds(0, rows)], sem.at[cur]).wait()
        return c

    @pl.when(i == 0)
    def _():
        buf[...] = jnp.zeros_like(buf)
        fetch(0, 0)

    @pl.when(i + 1 < pl.num_programs(0))
    def _():
        fetch(i + 1, 1 - cur)

    route = route_ref[...]
    lane = lax.broadcasted_iota(jnp.int32, (tm, LANES), 1).astype(F32)
    off = off_ref[0]

    def buf_row(e_lane, r_lane):
        e = route[:, e_lane:e_lane + 1]
        base = jnp.sum(jnp.where(lane == e, off, 0.0), axis=-1, keepdims=True)
        return (base + route[:, r_lane:r_lane + 1]).astype(jnp.int32)

    col = lax.broadcasted_iota(jnp.int32, (tm, RUN_SLOTS * RUN_ROWS), 1)
    pick = (jnp.where(col == buf_row(ROUTE_E1, ROUTE_R1), route[:, ROUTE_G1:ROUTE_G1 + 1], 0.0)
            + jnp.where(col == buf_row(ROUTE_E2, ROUTE_R2), route[:, ROUTE_G2:ROUTE_G2 + 1], 0.0))
    lax.fori_loop(0, ngroup_ref[i], wait, 0)
    o_ref[...] = x1_ref[...] + jnp.dot(pick.astype(BF16), _unpack_rows(buf[cur]), preferred_element_type=F32)


def _combine(src, nslot, x1, route, off, y_rows):
    N, D = x1.shape
    tm = min(TM_MOVE, N)
    return pl.pallas_call(
        functools.partial(_combine_kernel, tm=tm),
        grid_spec=pltpu.PrefetchScalarGridSpec(
            num_scalar_prefetch=2,
            grid=(N // tm,),
            in_specs=[
                pl.BlockSpec((tm, D), lambda i, s, n: (i, 0)),
                pl.BlockSpec((tm, LANES), lambda i, s, n: (i, 0)),
                pl.BlockSpec((1, 1, LANES), lambda i, s, n: (i, 0, 0)),
                pl.BlockSpec(memory_space=pl.ANY),
            ],
            out_specs=pl.BlockSpec((tm, D), lambda i, s, n: (i, 0)),
            scratch_shapes=[pltpu.VMEM((2, RUN_SLOTS * RUN_ROWS, D_PACK), U32), pltpu.SemaphoreType.DMA((2,))],
        ),
        out_shape=jax.ShapeDtypeStruct((N, D), F32),
        compiler_params=_params("arbitrary"),
    )(src, nslot, x1, route, off, y_rows)


def _moe(x1, h2, route, route_t, counts, sub_carry, w1, w3, w2, layer):
    N, _ = x1.shape
    tm = min(TM_MOVE, N)
    n_tiles = N // tm
    experts = jnp.arange(N_EXPERTS, dtype=jnp.int32)
    nb = (2 * N + N_EXPERTS * (TB_EXP - 1)) // TB_EXP + 1
    e_id = route_t[ROUTE_E1:ROUTE_E2 + 1].astype(jnp.int32)
    rank = route_t[ROUTE_R1:ROUTE_R2 + 1].astype(jnp.int32)
    cnt = counts[0, :N_EXPERTS].astype(jnp.int32)
    padded = (cnt + TB_EXP - 1) // TB_EXP * TB_EXP
    pad_end = jnp.cumsum(padded)
    pad_start = pad_end - padded
    e_flat = e_id.reshape(1, 2 * N)
    dest = jnp.sum(jnp.where(e_flat == experts[:, None], pad_start[:, None], 0), axis=0) + rank.reshape(2 * N)
    x_rows = _dispatch(dest, h2, nb * TB_EXP)
    n_used = pad_end[-1] // TB_EXP
    blk_map = jnp.minimum(jnp.arange(nb, dtype=jnp.int32), jnp.maximum(n_used - 1, 0))
    blk_exp = jnp.sum(pad_end[None, :] <= (blk_map * TB_EXP)[:, None], axis=1).astype(jnp.int32)
    blk_exp = jnp.minimum(blk_exp, N_EXPERTS - 1)
    is_blk_e = blk_exp[:, None] == experts[None, :]
    blk_valid = jnp.clip(jnp.sum(jnp.where(is_blk_e, (pad_start + cnt)[None, :], 0), axis=1) - blk_map * TB_EXP,
                         0, TB_EXP).astype(jnp.int32)
    has_rows = cnt > 0
    slot_of_e = (jnp.cumsum(has_rows.astype(jnp.int32)) - 1) % 2
    later = jnp.where(has_rows, experts, N_EXPERTS)
    next_of_e = lax.cummin(jnp.concatenate([later[1:], jnp.full((1,), N_EXPERTS, jnp.int32)]), reverse=True)
    next_of_e = jnp.where(next_of_e < N_EXPERTS, next_of_e, -1)
    blk_slot = jnp.sum(jnp.where(is_blk_e, slot_of_e[None, :], 0), axis=1).astype(jnp.int32)
    blk_next = jnp.sum(jnp.where(is_blk_e, next_of_e[None, :], 0), axis=1).astype(jnp.int32)

    per_router_tile = TM_OUT // tm if N >= TM_OUT else 1
    carry = sub_carry.reshape(-1, SUBLANES, LANES)[:, :per_router_tile, :N_EXPERTS].reshape(n_tiles, N_EXPERTS)
    carry = carry.astype(jnp.int32)
    tile_cnt = jnp.concatenate([carry[1:], cnt[None, :]], axis=0) - carry
    run_start = pad_start[None, :] + carry
    first_blk = run_start // RUN_ROWS
    nslot = jnp.where(tile_cnt > 0, (run_start + tile_cnt - 1) // RUN_ROWS - first_blk + 1, 0)
    slot_end = jnp.cumsum(nslot, axis=1)
    slot_base = slot_end - nslot
    slots = jnp.arange(RUN_SLOTS, dtype=jnp.int32)
    slot_e = jnp.minimum(jnp.sum(slot_end[:, None, :] <= slots[None, :, None], axis=2), N_EXPERTS - 1)
    is_slot_e = slot_e[:, :, None] == experts[None, None, :]
    pick = lambda tbl: jnp.sum(jnp.where(is_slot_e, tbl[:, None, :], 0), axis=2)
    src = (pick(first_blk) + slots[None, :] - pick(slot_base)) * RUN_ROWS
    src = jnp.where(slots[None, :] < slot_end[:, -1:], src, 0)
    src = jnp.clip(src, 0, nb * TB_EXP - RUN_ROWS).reshape(-1).astype(jnp.int32)
    n_groups = (slot_end[:, -1] + RUN_GROUP - 1) // RUN_GROUP
    off = (slot_base - first_blk) * RUN_ROWS + pad_start[None, :]
    off = jnp.pad(off.astype(F32), ((0, 0), (0, LANES - N_EXPERTS)))

    y_rows = _experts(blk_map, blk_exp, blk_valid, blk_slot, blk_next, n_used.reshape(1).astype(jnp.int32),
                      x_rows, w1, w3, w2, layer)
    return _combine(src, n_groups.astype(jnp.int32), x1, route, off.reshape(n_tiles, 1, LANES), y_rows)


def _pad_heads(w, width):
    k = w.shape[0]
    w = w.reshape(k, N_HEADS, width)
    return jnp.pad(w, ((0, 0), (0, 0), (0, HEAD_PAD - width))).reshape(k, N_HEADS * HEAD_PAD)


def _swap_rope(w):
    half = QK_ROPE // 2
    lo, hi = w[..., QK_NOPE:QK_NOPE + half], w[..., QK_NOPE + half:QK_HEAD]
    pad = [(0, 0)] * (w.ndim - 1)
    return jnp.pad(jnp.concatenate([hi, lo], -1), pad + [(QK_NOPE, HEAD_PAD - QK_HEAD)])


def _rope_tables(positions):
    inv_freq = ROPE_THETA ** (-jnp.arange(0, QK_ROPE, 2, dtype=F32) / QK_ROPE)
    ang = positions.astype(F32)[..., None] * inv_freq
    table = jnp.concatenate([jnp.cos(ang), jnp.sin(ang)], -1)
    return jnp.pad(table, ((0, 0), (0, 0), (QK_NOPE, LANES - QK_HEAD)))


def _ssm_params(lam_re, lam_im, b_re, b_im, c_re, c_im, log_dt):
    lam = lax.complex(lam_re, lam_im)
    dt = jnp.exp(log_dt)[:, None]
    lam_bar = jnp.exp(lam * dt)
    b_bar = ((lam_bar - 1.0) / lam)[..., None] * lax.complex(b_re, b_im)
    eye = jnp.eye(SSM_GROUPS, dtype=F32)

    def in_blockdiag(m):
        return jnp.einsum("gpc,gh->gchp", m, eye).reshape(SSM_WIDTH, SSM_FLAT)

    def out_blockdiag(m):
        return jnp.einsum("gcp,gh->gphc", m, eye).reshape(SSM_FLAT, SSM_WIDTH)

    bbd = jnp.concatenate([in_blockdiag(jnp.real(b_bar)), in_blockdiag(jnp.imag(b_bar))], axis=1)
    cbd = jnp.concatenate([out_blockdiag(c_re), out_blockdiag(-c_im)], axis=0)
    lam_rows = jnp.stack([jnp.real(lam_bar).reshape(SSM_FLAT), jnp.imag(lam_bar).reshape(SSM_FLAT)])
    return bbd.astype(BF16), lam_rows, cbd.astype(BF16)


def kernel(x, positions, mix_norm, w_in, q_a_norm, w_uq, kv_a_norm, w_ukv, q_norm, k_norm, ssm_lam_re, ssm_lam_im, ssm_b_re, ssm_b_im, ssm_c_re, ssm_c_im, ssm_d, ssm_log_dt, ssm_w_glu, conv_dw_w, conv_dw_b, conv_ln_w, conv_ln_b, out_norm, w_out, ffn_norm, w_grp, b_grp, w_exp, b_exp, w1, w3, w2):
    B, L, D = x.shape
    depth = w_in.shape[0]
    rope = _rope_tables(positions)
    row = lambda v: v.reshape(1, -1)
    lane_pad = lambda v: jnp.pad(v, (0, LANES - v.shape[0])).reshape(1, LANES)
    for l in range(depth):
        c_q, c_kv, k_pe, u_s, c_a, c_g = jnp.split(
            w_in[l], [Q_LORA, Q_LORA + KV_LORA, Q_LORA + KV_LORA + QK_ROPE,
                      Q_LORA + KV_LORA + QK_ROPE + SSM_WIDTH,
                      Q_LORA + KV_LORA + QK_ROPE + SSM_WIDTH + CONV_WIDTH], axis=1)
        k_pe_full = jnp.pad(k_pe, ((0, 0), (QK_NOPE, 0)))
        win = jnp.concatenate([c_q, c_kv, jnp.pad(k_pe_full, ((0, 0), (0, HEAD_PAD - QK_HEAD))), _swap_rope(k_pe_full),
                               u_s, c_a, c_g], axis=1).astype(BF16)
        wkv = w_ukv[l].reshape(KV_LORA, N_HEADS, QK_NOPE + V_HEAD)
        wuk = _pad_heads(wkv[:, :, :QK_NOPE].reshape(KV_LORA, N_HEADS * QK_NOPE), QK_NOPE).astype(BF16)
        wuv = wkv[:, :, QK_NOPE:].reshape(KV_LORA, ATTN_WIDTH).astype(BF16)
        wuq_sw = _swap_rope(w_uq[l].reshape(Q_LORA, N_HEADS, QK_HEAD)).reshape(Q_LORA, N_HEADS * HEAD_PAD)
        wuq = jnp.concatenate([_pad_heads(w_uq[l], QK_HEAD), wuq_sw], axis=1).astype(BF16)
        qn_scaled = q_norm[l] * (QK_HEAD ** -0.5)
        norm_rows = lambda w: jnp.stack([jnp.pad(w, (0, HEAD_PAD - QK_HEAD)), _swap_rope(w)])
        q, k, v, u_tm, ca, cg = _in_proj(
            x, rope, row(mix_norm[l]), win, row(q_a_norm[l]), wuq, row(kv_a_norm[l]), wuk, wuv,
            norm_rows(qn_scaled), norm_rows(k_norm[l]))
        y_attn = _attention(q, k, v)

        g_out = out_norm[l]
        bbd, lam_rows, cbd = _ssm_params(ssm_lam_re[l], ssm_lam_im[l], ssm_b_re[l], ssm_b_im[l],
                                         ssm_c_re[l], ssm_c_im[l], ssm_log_dt[l])
        y_ssm_tm = _ssm(u_tm.reshape(L * B, SSM_WIDTH), bbd, lam_rows, cbd, row(ssm_d[l]),
                        ssm_w_glu[l].astype(BF16), row(g_out[ATTN_WIDTH:ATTN_WIDTH + SSM_WIDTH]), B)
        conv_params = (conv_dw_w[l], row(conv_dw_b[l]), row(conv_ln_w[l]), row(conv_ln_b[l]),
                       row(g_out[ATTN_WIDTH + SSM_WIDTH:]))

        w_route = jnp.pad(jnp.concatenate([w_exp[l], w_grp[l]], axis=1), ((0, 0), (0, LANES - N_EXPERTS - N_EGROUPS)))
        w_route_hi = w_route.astype(BF16)
        w_route = jnp.concatenate([w_route_hi, (w_route - w_route_hi.astype(F32)).astype(BF16)], axis=1)
        b_route = lane_pad(jnp.concatenate([b_exp[l], b_grp[l]]))
        x1, h2, route, route_t, counts, sub_carry = _out_router(
            x.reshape(B * L, D), y_attn.reshape(B * L, ATTN_WIDTH), y_ssm_tm.reshape(L, B * SSM_WIDTH),
            ca.reshape(B * L, CONV_WIDTH), cg.reshape(B * L, CONV_WIDTH), row(g_out[:ATTN_WIDTH]),
            w_out[l].astype(BF16), row(ffn_norm[l]), w_route, b_route, conv_params, B)
        x = _moe(x1, h2, route, route_t, counts, sub_carry, w1, w3, w2, l).reshape(B, L, D)
    return x
```

```python
import functools

import jax
import jax.numpy as jnp
from jax import lax
from jax.experimental import pallas as pl
from jax.experimental.pallas import tpu as pltpu
from jax.experimental.pallas import tpu_sc as plsc

D_MODEL = 1024
CHUNK = 64
EPS = 1e-6
LN_EPS = 1e-5
N_HEADS = 8
QK_NOPE = 64
QK_ROPE = 32
QK_HEAD = QK_NOPE + QK_ROPE
V_HEAD = 64
Q_LORA = 256
KV_LORA = 128
ROPE_THETA = 10000.0
ATTN_WIDTH = N_HEADS * V_HEAD
SSM_WIDTH = 256
SSM_GROUP = 16
SSM_GROUPS = SSM_WIDTH // SSM_GROUP
SSM_STATE = 64
SSM_FLAT = SSM_GROUPS * SSM_STATE
CONV_WIDTH = 256
CONV_K = 31
N_EGROUPS = 4
EXP_PER_GROUP = 8
N_EXPERTS = N_EGROUPS * EXP_PER_GROUP
D_FF_E = 512

LANES = 128
SUBLANES = 8
HEAD_PAD = LANES
IN_PROJ_PAD = Q_LORA + KV_LORA + SSM_WIDTH + 2 * CONV_WIDTH + 2 * LANES
CONV_HALO = 32
VMEM_LIMIT = 48 * 1024 * 1024

TM_PROJ = 512
TQ_ATTN = 256
TC_SSM = 128
SSM_PARTS = 2
CONV_SUB = 128
TM_OUT = 512
TM_MOVE = 256
RUN_ROWS = SUBLANES
RUN_SLOTS = 2 * N_EXPERTS + 2 * TM_MOVE // RUN_ROWS
RUN_GROUP = 8
SC_CORES, SC_SUBCORES = 2, 16
SC_CHUNK = 64
TB_EXP = 512
EXPERT_TAIL_DIVISORS = (1, 2, 4)
FF_CHUNK = 256

BF16 = jnp.bfloat16
F32 = jnp.float32
U32 = jnp.uint32
D_PACK = D_MODEL // 2


def _pack_rows(v):
    bits = lax.bitcast_convert_type(v.astype(BF16).astype(F32), U32)
    half = v.shape[1] // 2
    return bits[:, 0:half] | (bits[:, half:] >> 16)


def _unpack_rows(w):
    hi = lax.bitcast_convert_type(w & jnp.uint32(0xFFFF0000), F32)
    lo = lax.bitcast_convert_type(w << 16, F32)
    return jnp.concatenate([hi, lo], axis=1).astype(BF16)


def _rms(x, w):
    return x * lax.rsqrt(jnp.mean(x * x, axis=-1, keepdims=True) + EPS) * w


def _params(*sem):
    return pltpu.CompilerParams(dimension_semantics=sem, vmem_limit_bytes=VMEM_LIMIT)


def _in_proj_kernel(x_ref, rope_ref, mixn_ref, win_ref, qan_ref, wuq_ref, kvan_ref, wuk_ref, wuv_ref,
                    qn_ref, kn_ref, q_ref, k_ref, v_ref, u_ref, ca_ref, cg_ref):
    x = x_ref[0]
    h = _rms(x, mixn_ref[0]).astype(BF16)
    head_cols = Q_LORA + KV_LORA + 2 * LANES
    proj = jnp.dot(h, win_ref[0, :, 0:head_cols], preferred_element_type=F32)
    o = 0
    c_q = proj[:, o:o + Q_LORA]; o += Q_LORA
    c_kv = proj[:, o:o + KV_LORA]; o += KV_LORA
    k_pe = proj[:, o:o + LANES]; o += LANES
    k_pe_sw = proj[:, o:o + LANES]

    width = N_HEADS * HEAD_PAD
    q2 = jnp.dot(_rms(c_q, qan_ref[0]).astype(BF16), wuq_ref[0], preferred_element_type=F32)
    q, q_sw = q2[:, 0:width], q2[:, width:2 * width]
    ckv_n = _rms(c_kv, kvan_ref[0]).astype(BF16)
    kn = jnp.dot(ckv_n, wuk_ref[0], preferred_element_type=F32)
    v_ref[0] = jnp.dot(ckv_n, wuv_ref[0], preferred_element_type=F32).astype(BF16)

    rope = rope_ref[0]
    half = QK_ROPE // 2
    lane = lax.broadcasted_iota(jnp.int32, rope.shape, 1)
    lo = (lane >= QK_NOPE) & (lane < QK_NOPE + half)
    hi = (lane >= QK_NOPE + half) & (lane < QK_HEAD)
    cos_t = jnp.where(lane < QK_NOPE, 1.0, jnp.where(lo, rope, jnp.where(hi, pltpu.roll(rope, half, 1), 0.0)))
    sin_t = jnp.where(lo, -pltpu.roll(rope, LANES - half, 1), jnp.where(hi, rope, 0.0))
    a_q, b_q = qn_ref[0, 0:1, :] * cos_t, qn_ref[0, 1:2, :] * sin_t
    a_k = kn_ref[0, 0:1, :] * cos_t
    k_sw_term = k_pe_sw * (kn_ref[0, 1:2, :] * sin_t)

    def inv_rms(y):
        return lax.rsqrt(jnp.sum(y * y, axis=-1, keepdims=True) * (1.0 / QK_HEAD) + EPS)

    def rest(j):
        cols = slice(head_cols + j * SSM_WIDTH, head_cols + (j + 1) * SSM_WIDTH)
        return jnp.dot(h, win_ref[0, :, cols], preferred_element_type=F32).astype(BF16)

    for hd in range(N_HEADS):
        if hd == 0:
            u_ref[...] = rest(0)
        elif hd == 3:
            ca_ref[0] = rest(1)
        elif hd == 6:
            cg_ref[0] = rest(2)
        sl = slice(hd * HEAD_PAD, (hd + 1) * HEAD_PAD)
        yq = q[:, sl]
        q_ref[0, hd] = ((yq * a_q + q_sw[:, sl] * b_q) * inv_rms(yq)).astype(BF16)
        yk = kn[:, sl] + k_pe
        k_ref[0, hd] = ((yk * a_k + k_sw_term) * inv_rms(yk)).astype(BF16)


def _in_proj(x, rope, mixn, win, qan, wuq, kvan, wuk, wuv, qn, kn, layer):
    B, L, D = x.shape
    tm = min(TM_PROJ, L)
    full = lambda shape: pl.BlockSpec((1,) + shape, lambda b, t: (layer,) + (0,) * len(shape))
    return pl.pallas_call(
        _in_proj_kernel,
        grid=(B, L // tm),
        in_specs=[
            pl.BlockSpec((1, tm, D), lambda b, t: (b, t, 0)),
            pl.BlockSpec((1, tm, LANES), lambda b, t: (b, t, 0)),
            full((1, D)), full((D, IN_PROJ_PAD)), full((1, Q_LORA)), full((Q_LORA, 2 * N_HEADS * HEAD_PAD)),
            full((1, KV_LORA)), full((KV_LORA, N_HEADS * HEAD_PAD)), full((KV_LORA, ATTN_WIDTH)),
            full((2, HEAD_PAD)), full((2, HEAD_PAD)),
        ],
        out_specs=[
            pl.BlockSpec((1, N_HEADS, tm, HEAD_PAD), lambda b, t: (b, 0, t, 0)),
            pl.BlockSpec((1, N_HEADS, tm, HEAD_PAD), lambda b, t: (b, 0, t, 0)),
            pl.BlockSpec((1, tm, ATTN_WIDTH), lambda b, t: (b, t, 0)),
            pl.BlockSpec((tm, SSM_WIDTH), lambda b, t: (t, b)),
            pl.BlockSpec((1, tm, CONV_WIDTH), lambda b, t: (b, t, 0)),
            pl.BlockSpec((1, tm, CONV_WIDTH), lambda b, t: (b, t, 0)),
        ],
        out_shape=[
            jax.ShapeDtypeStruct((B, N_HEADS, L, HEAD_PAD), BF16),
            jax.ShapeDtypeStruct((B, N_HEADS, L, HEAD_PAD), BF16),
            jax.ShapeDtypeStruct((B, L, ATTN_WIDTH), BF16),
            jax.ShapeDtypeStruct((L, B * SSM_WIDTH), BF16),
            jax.ShapeDtypeStruct((B, L, CONV_WIDTH), BF16),
            jax.ShapeDtypeStruct((B, L, CONV_WIDTH), BF16),
        ],
        compiler_params=_params("parallel", "parallel"),
    )(x, rope, mixn, win, qan, wuq, kvan, wuk, wuv, qn, kn)


def _attention_kernel(q_ref, k_ref, v_ref, o_ref, vext_ref, *, seq, tq):
    pair = 2 * V_HEAD
    vext_ref[:, 0:pair] = v_ref[0]
    vext_ref[:, pair:] = jnp.ones((seq, LANES), BF16)
    row_chunk = lax.broadcasted_iota(jnp.int32, (tq, tq), 0) // CHUNK
    col_chunk = lax.broadcasted_iota(jnp.int32, (tq, tq), 1) // CHUNK
    visible = col_chunk <= row_chunk
    neg = jnp.finfo(F32).min
    nt = (((1,), (1,)), ((), ()))
    low_lanes = lax.broadcasted_iota(jnp.int32, (tq, pair), 1) < V_HEAD
    def scores(i, hh):
        q0 = i * tq
        qb = q_ref[0, hh, q0:q0 + tq, :]
        s_d = lax.dot_general(qb, k_ref[0, hh, q0:q0 + tq, :], nt, preferred_element_type=F32)
        s_d = jnp.where(visible, s_d, neg)
        s_l = lax.dot_general(qb, k_ref[0, hh, 0:q0, :], nt, preferred_element_type=F32) if i > 0 else None
        return s_d, s_l

    def weighted_values(i, s_d, s_l):
        q0 = i * tq
        m = jnp.max(s_d, axis=-1, keepdims=True)
        if s_l is not None:
            m = jnp.maximum(m, jnp.max(s_l, axis=-1, keepdims=True))
        acc = jnp.dot(jnp.exp(s_d - m).astype(BF16), vext_ref[q0:q0 + tq, :], preferred_element_type=F32)
        if s_l is not None:
            acc = acc + jnp.dot(jnp.exp(s_l - m).astype(BF16), vext_ref[0:q0, :], preferred_element_type=F32)
        return acc[:, 0:pair] / acc[:, pair:]

    nq = seq // tq
    order = [x for p in zip(reversed(range(nq)), range(nq)) for x in p][:nq]
    chains = [(i, hh) for i in order for hh in range(2)]
    pending = scores(*chains[0])
    outs = {}
    for n, (i, hh) in enumerate(chains):
        upcoming = scores(*chains[n + 1]) if n + 1 < len(chains) else None
        outs[hh] = weighted_values(i, *pending)
        pending = upcoming
        if hh == 1:
            o_ref[0, i * tq:(i + 1) * tq, :] = jnp.where(low_lanes, outs[0], outs[1]).astype(BF16)


def _attention(q, k, v):
    B, H, L, _ = q.shape
    tq = min(TQ_ATTN, L)
    return pl.pallas_call(
        functools.partial(_attention_kernel, seq=L, tq=tq),
        grid=(B, H // 2),
        in_specs=[
            pl.BlockSpec((1, 2, L, HEAD_PAD), lambda b, p: (b, p, 0, 0)),
            pl.BlockSpec((1, 2, L, HEAD_PAD), lambda b, p: (b, p, 0, 0)),
            pl.BlockSpec((1, L, 2 * V_HEAD), lambda b, p: (b, 0, p)),
        ],
        out_specs=pl.BlockSpec((1, L, 2 * V_HEAD), lambda b, p: (b, 0, p)),
        out_shape=jax.ShapeDtypeStruct((B, L, ATTN_WIDTH), BF16),
        scratch_shapes=[pltpu.VMEM((L, 2 * V_HEAD + LANES), BF16)],
        compiler_params=_params("parallel", "parallel"),
    )(q, k, v)


def _ssm_kernel(u_ref, bbd_ref, lam_ref, cbd_ref, d_ref, wglu_ref, g_ref, o_ref, state_ref, *bu_refs, batch, tc):
    @pl.when(pl.program_id(0) == 0)
    def _():
        state_ref[...] = jnp.zeros_like(state_ref)

    steps = tc // SSM_PARTS
    halves = tuple((bu, slice(p * steps * batch, (p + 1) * steps * batch)) for p, bu in enumerate(bu_refs))
    for bu_ref, rows in halves:
        bu_ref[...] = jnp.dot(u_ref[rows, :], bbd_ref[0], preferred_element_type=F32)
    lam_re = jnp.broadcast_to(lam_ref[0, 0:1, :], (batch, SSM_FLAT))
    lam_im = jnp.broadcast_to(lam_ref[0, 1:2, :], (batch, SSM_FLAT))

    carry = (state_ref[:, 0:SSM_FLAT], state_ref[:, SSM_FLAT:2 * SSM_FLAT])
    for bu_ref, rows in halves:
        def step(t, xs, bu_ref=bu_ref):
            xr, xi = xs
            at = pl.ds(pl.multiple_of(t * batch, batch), batch)
            nr = lam_re * xr - lam_im * xi + bu_ref[at, 0:SSM_FLAT]
            ni = lam_re * xi + lam_im * xr + bu_ref[at, SSM_FLAT:2 * SSM_FLAT]
            bu_ref[at, 0:SSM_FLAT] = nr
            bu_ref[at, SSM_FLAT:2 * SSM_FLAT] = ni
            return nr, ni

        carry = lax.fori_loop(0, steps, step, carry, unroll=True)
        y = jnp.dot(bu_ref[...].astype(BF16), cbd_ref[0], preferred_element_type=F32)
        y = y + d_ref[0] * u_ref[rows, :].astype(F32)
        z = jax.nn.gelu(y)
        gate = jax.nn.sigmoid(jnp.dot(z.astype(BF16), wglu_ref[0], preferred_element_type=F32))
        o_ref[rows, :] = _rms(z * gate, g_ref[0]).astype(BF16)
    state_ref[:, 0:SSM_FLAT] = carry[0]
    state_ref[:, SSM_FLAT:2 * SSM_FLAT] = carry[1]


def _ssm(u_tm, bbd, lam, cbd, d, wglu, g, batch, layer):
    rows = u_tm.shape[0]
    seq = rows // batch
    tc = min(TC_SSM, seq)
    blk = tc * batch
    full = lambda shape: pl.BlockSpec((1,) + shape, lambda t: (layer,) + (0,) * len(shape))
    return pl.pallas_call(
        functools.partial(_ssm_kernel, batch=batch, tc=tc),
        grid=(seq // tc,),
        in_specs=[
            pl.BlockSpec((blk, SSM_WIDTH), lambda t: (t, 0)),
            full((SSM_WIDTH, 2 * SSM_FLAT)), full((2, SSM_FLAT)), full((2 * SSM_FLAT, SSM_WIDTH)),
            full((1, SSM_WIDTH)), full((SSM_WIDTH, SSM_WIDTH)), full((1, SSM_WIDTH)),
        ],
        out_specs=pl.BlockSpec((blk, SSM_WIDTH), lambda t: (t, 0)),
        out_shape=jax.ShapeDtypeStruct((rows, SSM_WIDTH), BF16),
        scratch_shapes=[pltpu.VMEM((batch, 2 * SSM_FLAT), F32)]
        + [pltpu.VMEM((blk // SSM_PARTS, 2 * SSM_FLAT), F32)] * SSM_PARTS,
        compiler_params=_params("arbitrary"),
    )(u_tm, bbd, lam, cbd, d, wglu, g)


def _conv_rows(upad_ref, r0, w_ref, b_ref, lnw_ref, lnb_ref, gn_ref):
    base = CONV_HALO - (CONV_K - 1)
    acc = jnp.zeros((CONV_SUB, CONV_WIDTH), F32)
    for off in range(SUBLANES):
        n = CONV_SUB + (SUBLANES if off else 0)
        part = None
        for a8 in range(0, CONV_HALO + 1, SUBLANES):
            kk = a8 + off - base
            if 0 <= kk < CONV_K:
                term = w_ref[0, kk:kk + 1, :] * upad_ref[r0 + a8:r0 + a8 + n, :]
                part = term if part is None else part + term
        acc = acc + part[off:off + CONV_SUB]
    y = acc + b_ref[0]
    mu = jnp.mean(y, axis=-1, keepdims=True)
    var = jnp.mean(jnp.square(y - mu), axis=-1, keepdims=True)
    y = (y - mu) * lax.rsqrt(var + LN_EPS) * lnw_ref[0] + lnb_ref[0]
    y = y * jax.nn.sigmoid(y)
    return _rms(y, gn_ref[0])


ROUTE_E1, ROUTE_E2, ROUTE_R1, ROUTE_R2, ROUTE_G1, ROUTE_G2 = range(6)
ROUTE_ROWS = 8
GRP_LANE0 = N_EXPERTS


def _out_router_kernel(x_ref, ya_ref, ys_ref, ca_ref, cg_ref, ga_ref, wo_ref, fn_ref, wr_ref, br_ref,
                       cw_ref, cb_ref, clnw_ref, clnb_ref, cgn_ref,
                       x1_ref, h2_ref, route_ref, route_t_ref, cnt_ref, sub_ref,
                       carry_ref, tri_ref, upad_ref, yc_ref, *, tm, per_b):
    @pl.when(pl.program_id(0) == 0)
    def _():
        carry_ref[...] = jnp.zeros_like(carry_ref)
        rr = lax.broadcasted_iota(jnp.int32, (tm, tm), 0)
        cc = lax.broadcasted_iota(jnp.int32, (tm, tm), 1)
        tri_ref[...] = (cc < rr).astype(BF16)

    first = pl.program_id(0) % per_b == 0
    upad_ref[0:CONV_HALO, :] = jnp.where(first, 0.0, upad_ref[tm:tm + CONV_HALO, :])
    upad_ref[CONV_HALO:, :] = ca_ref[...].astype(F32) * jax.nn.sigmoid(cg_ref[...].astype(F32))
    ya = _rms(ya_ref[...].astype(F32), ga_ref[0]).astype(BF16)
    acc = jnp.dot(ya, wo_ref[0, 0:ATTN_WIDTH, :], preferred_element_type=F32)
    acc += jnp.dot(ys_ref[...], wo_ref[0, ATTN_WIDTH:ATTN_WIDTH + SSM_WIDTH, :], preferred_element_type=F32)
    for r0 in range(0, tm, CONV_SUB):
        yc_ref[r0:r0 + CONV_SUB, :] = _conv_rows(upad_ref, r0, cw_ref, cb_ref, clnw_ref, clnb_ref, cgn_ref).astype(BF16)
    acc += jnp.dot(yc_ref[...], wo_ref[0, ATTN_WIDTH + SSM_WIDTH:, :], preferred_element_type=F32)
    x1 = x_ref[...] + acc
    x1_ref[...] = x1
    h2 = _rms(x1, fn_ref[0])
    h2_ref[...] = _pack_rows(h2)

    h_hi = h2.astype(BF16)
    h_lo = (h2 - h_hi.astype(F32)).astype(BF16)
    part = jnp.dot(h_hi, wr_ref[0], preferred_element_type=F32)
    logits = (part[:, 0:LANES] + part[:, LANES:2 * LANES]
              + jnp.dot(h_lo, wr_ref[0, :, 0:LANES], preferred_element_type=F32) + br_ref[0])
    lane = lax.broadcasted_iota(jnp.int32, (tm, LANES), 1)
    ninf = -jnp.inf
    big = LANES

    def first_argmax(vals, vmax):
        return jnp.min(jnp.where(vals == vmax, lane, big), axis=-1, keepdims=True)

    grp = jnp.where((lane >= GRP_LANE0) & (lane < GRP_LANE0 + N_EGROUPS), logits, ninf)
    gmax = jnp.max(grp, axis=-1, keepdims=True)
    gsel = first_argmax(grp, gmax) - GRP_LANE0
    p_grp = 1.0 / jnp.sum(jnp.exp(grp - gmax), axis=-1, keepdims=True)

    el = jnp.where((lane < N_EXPERTS) & ((lane // EXP_PER_GROUP) == gsel), logits, ninf)
    m1 = jnp.max(el, axis=-1, keepdims=True)
    e1 = first_argmax(el, m1)
    el2 = jnp.where(lane == e1, ninf, el)
    m2 = jnp.max(el2, axis=-1, keepdims=True)
    e2 = first_argmax(el2, m2)
    t2 = jnp.exp(m2 - m1)
    g1 = p_grp / (1.0 + t2)
    g2 = p_grp * t2 / (1.0 + t2)

    hit1 = lane == e1
    hit2 = lane == e2
    cnt = (hit1 | hit2).astype(F32)
    before = jnp.dot(tri_ref[...], cnt.astype(BF16), preferred_element_type=F32) + carry_ref[...]
    r1 = jnp.sum(jnp.where(hit1, before, 0.0), axis=-1, keepdims=True)
    r2 = jnp.sum(jnp.where(hit2, before, 0.0), axis=-1, keepdims=True)
    subs = [before[h * TM_MOVE:h * TM_MOVE + 1, :] for h in range(tm // TM_MOVE)]
    sub_ref[...] = jnp.concatenate(subs + [jnp.zeros((SUBLANES - len(subs), LANES), F32)], axis=0)
    carry_ref[...] += jnp.sum(cnt, axis=0, keepdims=True)
    cnt_ref[...] = carry_ref[...]

    rec = jnp.where(lane == ROUTE_E1, e1.astype(F32), 0.0)
    rec = jnp.where(lane == ROUTE_E2, e2.astype(F32), rec)
    rec = jnp.where(lane == ROUTE_R1, r1, rec)
    rec = jnp.where(lane == ROUTE_R2, r2, rec)
    rec = jnp.where(lane == ROUTE_G1, g1, rec)
    rec = jnp.where(lane == ROUTE_G2, g2, rec)
    route_ref[...] = rec
    route_t_ref[...] = rec.T[0:ROUTE_ROWS, :]


def _out_router(x, ya, ys_tm, ca, cg, ga, wo, fn, wr, br, conv_params, batch, layer):
    N, D = x.shape
    seq = N // batch
    tm = min(TM_OUT, seq)
    per_b = seq // tm
    full = lambda shape: pl.BlockSpec((1,) + shape, lambda i: (layer,) + (0,) * len(shape))
    whole = lambda shape: pl.BlockSpec(shape, lambda i: (0,) * len(shape))
    tile = lambda w: pl.BlockSpec((tm, w), lambda i: (i, 0))
    return pl.pallas_call(
        functools.partial(_out_router_kernel, tm=tm, per_b=per_b),
        grid=(N // tm,),
        in_specs=[
            tile(D), tile(ATTN_WIDTH),
            pl.BlockSpec((tm, SSM_WIDTH), lambda i: (i % per_b, i // per_b)),
            tile(CONV_WIDTH), tile(CONV_WIDTH),
            full((1, ATTN_WIDTH)), full((D, D)), full((1, D)), full((D, 2 * LANES)), full((1, LANES)),
            full((CONV_K, CONV_WIDTH)), full((1, CONV_WIDTH)), full((1, CONV_WIDTH)), full((1, CONV_WIDTH)),
            full((1, CONV_WIDTH)),
        ],
        out_specs=[tile(D), tile(D_PACK), tile(LANES), pl.BlockSpec((ROUTE_ROWS, tm), lambda i: (0, i)),
                   whole((1, LANES)), pl.BlockSpec((SUBLANES, LANES), lambda i: (i, 0))],
        out_shape=[
            jax.ShapeDtypeStruct((N, D), F32),
            jax.ShapeDtypeStruct((N, D_PACK), U32),
            jax.ShapeDtypeStruct((N, LANES), F32),
            jax.ShapeDtypeStruct((ROUTE_ROWS, N), F32),
            jax.ShapeDtypeStruct((1, LANES), F32),
            jax.ShapeDtypeStruct((N // tm * SUBLANES, LANES), F32),
        ],
        scratch_shapes=[pltpu.VMEM((1, LANES), F32), pltpu.VMEM((tm, tm), BF16),
                        pltpu.VMEM((tm + CONV_HALO, CONV_WIDTH), F32), pltpu.VMEM((tm, CONV_WIDTH), BF16)],
        compiler_params=_params("arbitrary"),
    )(x, ya, ys_tm, ca, cg, ga, wo, fn, wr, br, *conv_params)


def _dispatch(dest_flat, h2, n_rows):
    N, D = h2.shape
    workers = SC_CORES * SC_SUBCORES
    per_worker = N // workers
    chunks = per_worker // SC_CHUNK
    idx = dest_flat.reshape(2, workers, chunks, SC_CHUNK).transpose(1, 0, 2, 3).reshape(workers, 2 * chunks, SC_CHUNK)
    mesh = plsc.VectorSubcoreMesh(core_axis_name="c", subcore_axis_name="s")

    @functools.partial(
        pl.kernel, mesh=mesh,
        out_type=jax.ShapeDtypeStruct((n_rows, D), h2.dtype),
        scratch_types=[pltpu.VMEM((2 * chunks, SC_CHUNK), jnp.int32), pltpu.VMEM((2, SC_CHUNK, D), h2.dtype),
                       pltpu.SemaphoreType.DMA((2,)), pltpu.SemaphoreType.DMA((2, 2))],
    )
    def scatter(h_hbm, idx_hbm, out_hbm, idx_v, rows_v, sem_in, sem_out):
        wid = lax.axis_index("s") * SC_CORES + lax.axis_index("c")
        pltpu.sync_copy(idx_hbm.at[wid], idx_v)

        def load(c):
            first = pl.multiple_of(wid * per_worker + c * SC_CHUNK, SC_CHUNK)
            return pltpu.async_copy(h_hbm.at[pl.ds(first, SC_CHUNK)], rows_v.at[c % 2], sem_in.at[c % 2])

        def store(c, k):
            return pltpu.async_copy(rows_v.at[c % 2], out_hbm.at[idx_v.at[k * chunks + c]], sem_out.at[c % 2, k])

        loads = {0: load(0)}
        stores = {}
        for c in range(chunks):
            loads[c].wait()
            if c >= 1:
                for st in stores[c - 1]:
                    st.wait()
            if c + 1 < chunks:
                loads[c + 1] = load(c + 1)
            stores[c] = (store(c, 0), store(c, 1))
        for st in stores[chunks - 1]:
            st.wait()

    return scatter(h2, idx)


def _experts_kernel(blk_ref, exp_ref, valid_ref, slot_ref, next_ref, used_ref, x_ref, w1_hbm, w3_hbm, w2_hbm, y_ref,
                    w1_f, w3_f, w2_f, w13_s, w2_s, sem, *, layer):
    j = pl.program_id(0)

    def weight_copies(e, slot):
        return (pltpu.make_async_copy(w1_hbm.at[layer, e], w1_f.at[slot], sem.at[slot, 0]),
                pltpu.make_async_copy(w3_hbm.at[layer, e], w3_f.at[slot], sem.at[slot, 1]),
                pltpu.make_async_copy(w2_hbm.at[layer, e], w2_f.at[slot], sem.at[slot, 2]))

    @pl.when(j < used_ref[0])
    def _():
        e = exp_ref[j]
        slot = slot_ref[j]

        @pl.when(j == 0)
        def _():
            for cp in weight_copies(e, slot):
                cp.start()

        @pl.when((j == 0) | (e != exp_ref[jnp.maximum(j - 1, 0)]))
        def _():
            for cp in weight_copies(e, slot):
                cp.wait()
            for c in range(D_FF_E // FF_CHUNK):
                w13_s[:, 2 * FF_CHUNK * c:2 * FF_CHUNK * c + FF_CHUNK] = \
                    w1_f[slot, :, FF_CHUNK * c:FF_CHUNK * (c + 1)].astype(BF16)
                w13_s[:, 2 * FF_CHUNK * c + FF_CHUNK:2 * FF_CHUNK * (c + 1)] = \
                    w3_f[slot, :, FF_CHUNK * c:FF_CHUNK * (c + 1)].astype(BF16)
            w2_s[...] = w2_f[slot].astype(BF16)

            @pl.when(next_ref[j] >= 0)
            def _():
                for cp in weight_copies(next_ref[j], 1 - slot):
                    cp.start()

        valid = valid_ref[j]

        def ffn(rows):
            x = _unpack_rows(x_ref[0:rows, :])
            y = None
            for c in range(D_FF_E // FF_CHUNK):
                ab = jnp.dot(x, w13_s[:, 2 * FF_CHUNK * c:2 * FF_CHUNK * (c + 1)], preferred_element_type=F32)
                a = ab[:, 0:FF_CHUNK]
                hmid = (a * jax.nn.sigmoid(a) * ab[:, FF_CHUNK:]).astype(BF16)
                part = jnp.dot(hmid, w2_s[FF_CHUNK * c:FF_CHUNK * (c + 1), :], preferred_element_type=F32)
                y = part if y is None else y + part
            row = lax.broadcasted_iota(jnp.int32, (rows, 1), 0)
            y_ref[0:rows, :] = jnp.where(row < valid, _pack_rows(y), jnp.uint32(0))
            if rows < TB_EXP:
                y_ref[rows:, :] = jnp.zeros((TB_EXP - rows, D_PACK), U32)

        sizes = [TB_EXP // d for d in EXPERT_TAIL_DIVISORS]
        for n, rows in enumerate(sizes):
            below = sizes[n + 1] if n + 1 < len(sizes) else 0

            @pl.when((valid > below) & (valid <= rows) if n else valid > below)
            def _(rows=rows):
                ffn(rows)


def _experts(blk_map, blk_exp, blk_valid, blk_slot, blk_next, n_used, x_rows, w1, w3, w2, layer):
    R, D = x_rows.shape[0], D_MODEL
    nb = R // TB_EXP
    row_block = lambda j, bm, be, bv, bs, bn, nu: (bm[j], 0)
    return pl.pallas_call(
        functools.partial(_experts_kernel, layer=layer),
        grid_spec=pltpu.PrefetchScalarGridSpec(
            num_scalar_prefetch=6,
            grid=(nb,),
            in_specs=[
                pl.BlockSpec((TB_EXP, D_PACK), row_block),
                pl.BlockSpec(memory_space=pl.ANY), pl.BlockSpec(memory_space=pl.ANY), pl.BlockSpec(memory_space=pl.ANY),
            ],
            out_specs=pl.BlockSpec((TB_EXP, D_PACK), row_block),
            scratch_shapes=[
                pltpu.VMEM((2, D, D_FF_E), F32), pltpu.VMEM((2, D, D_FF_E), F32), pltpu.VMEM((2, D_FF_E, D), F32),
                pltpu.VMEM((D, 2 * D_FF_E), BF16), pltpu.VMEM((D_FF_E, D), BF16),
                pltpu.SemaphoreType.DMA((2, 3)),
            ],
        ),
        out_shape=jax.ShapeDtypeStruct((R, D_PACK), U32),
        compiler_params=_params("arbitrary"),
    )(blk_map, blk_exp, blk_valid, blk_slot, blk_next, n_used, x_rows, w1, w3, w2)


def _combine_kernel(src_ref, ngroup_ref, x1_ref, route_ref, off_ref, rows_ref, o_ref, buf, sem, *, tm):
    i = pl.program_id(0)
    cur = i % 2

    def slot_copy(tile, half, s):
        src = pl.multiple_of(src_ref[tile * RUN_SLOTS + s], RUN_ROWS)
        dst = buf.at[half, pl.ds(pl.multiple_of(s * RUN_ROWS, RUN_ROWS), RUN_ROWS)]
        return pltpu.make_async_copy(rows_ref.at[pl.ds(src, RUN_ROWS)], dst, sem.at[half])

    def fetch(tile, half):
        def issue(g, c):
            for u in range(RUN_GROUP):
                slot_copy(tile, half, g * RUN_GROUP + u).start()
            return c
        lax.fori_loop(0, ngroup_ref[tile], issue, 0)

    def wait(g, c):
        rows = RUN_GROUP * RUN_ROWS
        pltpu.make_async_copy(rows_ref.at[pl.ds(0, rows)], buf.at[cur, pl.ds(0, rows)], sem.at[cur]).wait()
        return c

    @pl.when(i == 0)
    def _():
        buf[...] = jnp.zeros_like(buf)
        fetch(0, 0)

    @pl.when(i + 1 < pl.num_programs(0))
    def _():
        fetch(i + 1, 1 - cur)

    route = route_ref[...]
    lane = lax.broadcasted_iota(jnp.int32, (tm, LANES), 1).astype(F32)
    off = off_ref[0]

    def buf_row(e_lane, r_lane):
        e = route[:, e_lane:e_lane + 1]
        base = jnp.sum(jnp.where(lane == e, off, 0.0), axis=-1, keepdims=True)
        return (base + route[:, r_lane:r_lane + 1]).astype(jnp.int32)

    col = lax.broadcasted_iota(jnp.int32, (tm, RUN_SLOTS * RUN_ROWS), 1)
    pick = (jnp.where(col == buf_row(ROUTE_E1, ROUTE_R1), route[:, ROUTE_G1:ROUTE_G1 + 1], 0.0)
            + jnp.where(col == buf_row(ROUTE_E2, ROUTE_R2), route[:, ROUTE_G2:ROUTE_G2 + 1], 0.0))
    lax.fori_loop(0, ngroup_ref[i], wait, 0)
    o_ref[...] = x1_ref[...] + jnp.dot(pick.astype(BF16), _unpack_rows(buf[cur]), preferred_element_type=F32)


def _combine(src, nslot, x1, route, off, y_rows):
    N, D = x1.shape
    tm = min(TM_MOVE, N)
    return pl.pallas_call(
        functools.partial(_combine_kernel, tm=tm),
        grid_spec=pltpu.PrefetchScalarGridSpec(
            num_scalar_prefetch=2,
            grid=(N // tm,),
            in_specs=[
                pl.BlockSpec((tm, D), lambda i, s, n: (i, 0)),
                pl.BlockSpec((tm, LANES), lambda i, s, n: (i, 0)),
                pl.BlockSpec((1, 1, LANES), lambda i, s, n: (i, 0, 0)),
                pl.BlockSpec(memory_space=pl.ANY),
            ],
            out_specs=pl.BlockSpec((tm, D), lambda i, s, n: (i, 0)),
            scratch_shapes=[pltpu.VMEM((2, RUN_SLOTS * RUN_ROWS, D_PACK), U32), pltpu.SemaphoreType.DMA((2,))],
        ),
        out_shape=jax.ShapeDtypeStruct((N, D), F32),
        compiler_params=_params("arbitrary"),
    )(src, nslot, x1, route, off, y_rows)


def _moe(x1, h2, route, route_t, counts, sub_carry, w1, w3, w2, layer):
    N, _ = x1.shape
    tm = min(TM_MOVE, N)
    n_tiles = N // tm
    experts = jnp.arange(N_EXPERTS, dtype=jnp.int32)
    nb = (2 * N + N_EXPERTS * (TB_EXP - 1)) // TB_EXP + 1
    e_id = route_t[ROUTE_E1:ROUTE_E2 + 1].astype(jnp.int32)
    rank = route_t[ROUTE_R1:ROUTE_R2 + 1].astype(jnp.int32)
    cnt = counts[0, :N_EXPERTS].astype(jnp.int32)
    padded = (cnt + TB_EXP - 1) // TB_EXP * TB_EXP
    pad_end = jnp.cumsum(padded)
    pad_start = pad_end - padded
    e_flat = e_id.reshape(1, 2 * N)
    dest = jnp.sum(jnp.where(e_flat == experts[:, None], pad_start[:, None], 0), axis=0) + rank.reshape(2 * N)
    x_rows = _dispatch(dest, h2, nb * TB_EXP)
    n_used = pad_end[-1] // TB_EXP
    blk_map = jnp.minimum(jnp.arange(nb, dtype=jnp.int32), jnp.maximum(n_used - 1, 0))
    blk_exp = jnp.sum(pad_end[None, :] <= (blk_map * TB_EXP)[:, None], axis=1).astype(jnp.int32)
    blk_exp = jnp.minimum(blk_exp, N_EXPERTS - 1)
    is_blk_e = blk_exp[:, None] == experts[None, :]
    blk_valid = jnp.clip(jnp.sum(jnp.where(is_blk_e, (pad_start + cnt)[None, :], 0), axis=1) - blk_map * TB_EXP,
                         0, TB_EXP).astype(jnp.int32)
    has_rows = cnt > 0
    slot_of_e = (jnp.cumsum(has_rows.astype(jnp.int32)) - 1) % 2
    later = jnp.where(has_rows, experts, N_EXPERTS)
    next_of_e = lax.cummin(jnp.concatenate([later[1:], jnp.full((1,), N_EXPERTS, jnp.int32)]), reverse=True)
    next_of_e = jnp.where(next_of_e < N_EXPERTS, next_of_e, -1)
    blk_slot = jnp.sum(jnp.where(is_blk_e, slot_of_e[None, :], 0), axis=1).astype(jnp.int32)
    blk_next = jnp.sum(jnp.where(is_blk_e, next_of_e[None, :], 0), axis=1).astype(jnp.int32)

    per_router_tile = TM_OUT // tm if N >= TM_OUT else 1
    carry = sub_carry.reshape(-1, SUBLANES, LANES)[:, :per_router_tile, :N_EXPERTS].reshape(n_tiles, N_EXPERTS)
    carry = carry.astype(jnp.int32)
    tile_cnt = jnp.concatenate([carry[1:], cnt[None, :]], axis=0) - carry
    run_start = pad_start[None, :] + carry
    first_blk = run_start // RUN_ROWS
    nslot = jnp.where(tile_cnt > 0, (run_start + tile_cnt - 1) // RUN_ROWS - first_blk + 1, 0)
    slot_end = jnp.cumsum(nslot, axis=1)
    slot_base = slot_end - nslot
    slots = jnp.arange(RUN_SLOTS, dtype=jnp.int32)
    slot_e = jnp.minimum(jnp.sum(slot_end[:, None, :] <= slots[None, :, None], axis=2), N_EXPERTS - 1)
    is_slot_e = slot_e[:, :, None] == experts[None, None, :]
    pick = lambda tbl: jnp.sum(jnp.where(is_slot_e, tbl[:, None, :], 0), axis=2)
    src = (pick(first_blk) + slots[None, :] - pick(slot_base)) * RUN_ROWS
    src = jnp.where(slots[None, :] < slot_end[:, -1:], src, 0)
    src = jnp.clip(src, 0, nb * TB_EXP - RUN_ROWS).reshape(-1).astype(jnp.int32)
    n_groups = (slot_end[:, -1] + RUN_GROUP - 1) // RUN_GROUP
    off = (slot_base - first_blk) * RUN_ROWS + pad_start[None, :]
    off = jnp.pad(off.astype(F32), ((0, 0), (0, LANES - N_EXPERTS)))

    y_rows = _experts(blk_map, blk_exp, blk_valid, blk_slot, blk_next, n_used.reshape(1).astype(jnp.int32),
                      x_rows, w1, w3, w2, layer)
    return _combine(src, n_groups.astype(jnp.int32), x1, route, off.reshape(n_tiles, 1, LANES), y_rows)


def _pad_heads(w, width):
    lead = w.shape[:-1]
    w = w.reshape(lead + (N_HEADS, width))
    pad = [(0, 0)] * (w.ndim - 1) + [(0, HEAD_PAD - width)]
    return jnp.pad(w, pad).reshape(lead + (N_HEADS * HEAD_PAD,))


def _swap_rope(w):
    half = QK_ROPE // 2
    lo, hi = w[..., QK_NOPE:QK_NOPE + half], w[..., QK_NOPE + half:QK_HEAD]
    pad = [(0, 0)] * (w.ndim - 1)
    return jnp.pad(jnp.concatenate([hi, lo], -1), pad + [(QK_NOPE, HEAD_PAD - QK_HEAD)])


def _rope_tables(positions):
    inv_freq = ROPE_THETA ** (-jnp.arange(0, QK_ROPE, 2, dtype=F32) / QK_ROPE)
    ang = positions.astype(F32)[..., None] * inv_freq
    table = jnp.concatenate([jnp.cos(ang), jnp.sin(ang)], -1)
    return jnp.pad(table, ((0, 0), (0, 0), (QK_NOPE, LANES - QK_HEAD)))


def _ssm_params(lam_re, lam_im, b_re, b_im, c_re, c_im, log_dt):
    depth = lam_re.shape[0]
    lam = lax.complex(lam_re, lam_im)
    dt = jnp.exp(log_dt)[..., None]
    lam_bar = jnp.exp(lam * dt)
    b_bar = ((lam_bar - 1.0) / lam)[..., None] * lax.complex(b_re, b_im)
    eye = jnp.eye(SSM_GROUPS, dtype=F32)

    def in_blockdiag(m):
        return jnp.einsum("lgpc,gh->lgchp", m, eye).reshape(depth, SSM_WIDTH, SSM_FLAT)

    def out_blockdiag(m):
        return jnp.einsum("lgcp,gh->lgphc", m, eye).reshape(depth, SSM_FLAT, SSM_WIDTH)

    bbd = jnp.concatenate([in_blockdiag(jnp.real(b_bar)), in_blockdiag(jnp.imag(b_bar))], axis=2)
    cbd = jnp.concatenate([out_blockdiag(c_re), out_blockdiag(-c_im)], axis=1)
    lam_rows = jnp.stack([jnp.real(lam_bar).reshape(depth, SSM_FLAT), jnp.imag(lam_bar).reshape(depth, SSM_FLAT)],
                         axis=1)
    return bbd.astype(BF16), lam_rows, cbd.astype(BF16)


def kernel(x, positions, mix_norm, w_in, q_a_norm, w_uq, kv_a_norm, w_ukv, q_norm, k_norm, ssm_lam_re, ssm_lam_im, ssm_b_re, ssm_b_im, ssm_c_re, ssm_c_im, ssm_d, ssm_log_dt, ssm_w_glu, conv_dw_w, conv_dw_b, conv_ln_w, conv_ln_b, out_norm, w_out, ffn_norm, w_grp, b_grp, w_exp, b_exp, w1, w3, w2):
    B, L, D = x.shape
    depth = w_in.shape[0]
    rope = _rope_tables(positions)
    rows3 = lambda v: v[:, None, :]
    lane_tail = lambda v: jnp.pad(v, [(0, 0)] * (v.ndim - 1) + [(0, LANES - v.shape[-1])])
    c_q, c_kv, k_pe, u_s, c_a, c_g = jnp.split(
        w_in, [Q_LORA, Q_LORA + KV_LORA, Q_LORA + KV_LORA + QK_ROPE, Q_LORA + KV_LORA + QK_ROPE + SSM_WIDTH,
               Q_LORA + KV_LORA + QK_ROPE + SSM_WIDTH + CONV_WIDTH], axis=2)
    k_pe_full = jnp.pad(k_pe, ((0, 0), (0, 0), (QK_NOPE, 0)))
    win = jnp.concatenate([c_q, c_kv, lane_tail(k_pe_full), _swap_rope(k_pe_full), u_s, c_a, c_g], axis=2).astype(BF16)
    wkv = w_ukv.reshape(depth, KV_LORA, N_HEADS, QK_NOPE + V_HEAD)
    wuk = _pad_heads(wkv[..., :QK_NOPE].reshape(depth, KV_LORA, N_HEADS * QK_NOPE), QK_NOPE).astype(BF16)
    wuv = wkv[..., QK_NOPE:].reshape(depth, KV_LORA, ATTN_WIDTH).astype(BF16)
    wuq_sw = _swap_rope(w_uq.reshape(depth, Q_LORA, N_HEADS, QK_HEAD)).reshape(depth, Q_LORA, N_HEADS * HEAD_PAD)
    wuq = jnp.concatenate([_pad_heads(w_uq, QK_HEAD), wuq_sw], axis=2).astype(BF16)
    norm_rows = lambda w: jnp.stack([lane_tail(w), _swap_rope(w)], axis=1)
    qn, kn = norm_rows(q_norm * (QK_HEAD ** -0.5)), norm_rows(k_norm)
    bbd, lam_rows, cbd = _ssm_params(ssm_lam_re, ssm_lam_im, ssm_b_re, ssm_b_im, ssm_c_re, ssm_c_im, ssm_log_dt)
    ssm_skip = rows3(ssm_d.reshape(depth, SSM_WIDTH))
    wglu = ssm_w_glu.astype(BF16)
    g_attn = rows3(out_norm[:, :ATTN_WIDTH])
    g_ssm = rows3(out_norm[:, ATTN_WIDTH:ATTN_WIDTH + SSM_WIDTH])
    conv_params = (conv_dw_w, rows3(conv_dw_b), rows3(conv_ln_w), rows3(conv_ln_b),
                   rows3(out_norm[:, ATTN_WIDTH + SSM_WIDTH:]))
    w_route = lane_tail(jnp.concatenate([w_exp, w_grp], axis=2))
    w_route_hi = w_route.astype(BF16)
    w_route = jnp.concatenate([w_route_hi, (w_route - w_route_hi.astype(F32)).astype(BF16)], axis=2)
    b_route = rows3(lane_tail(jnp.concatenate([b_exp, b_grp], axis=1)))
    wo = w_out.astype(BF16)
    for l in range(depth):
        q, k, v, u_tm, ca, cg = _in_proj(x, rope, rows3(mix_norm), win, rows3(q_a_norm), wuq, rows3(kv_a_norm), wuk, wuv,
                                         qn, kn, l)
        y_attn = _attention(q, k, v)
        y_ssm_tm = _ssm(u_tm.reshape(L * B, SSM_WIDTH), bbd, lam_rows, cbd, ssm_skip, wglu, g_ssm, B, l)
        x1, h2, route, route_t, counts, sub_carry = _out_router(
            x.reshape(B * L, D), y_attn.reshape(B * L, ATTN_WIDTH), y_ssm_tm.reshape(L, B * SSM_WIDTH),
            ca.reshape(B * L, CONV_WIDTH), cg.reshape(B * L, CONV_WIDTH), g_attn, wo, rows3(ffn_norm),
            w_route, b_route, conv_params, B, l)
        x = _moe(x1, h2, route, route_t, counts, sub_carry, w1, w3, w2, l).reshape(B, L, D)
    return x
```

```python
import functools

import jax
import jax.numpy as jnp
from jax import lax
from jax.experimental import pallas as pl
from jax.experimental.pallas import tpu as pltpu
from jax.experimental.pallas import tpu_sc as plsc

D_MODEL = 1024
CHUNK = 64
EPS = 1e-6
LN_EPS = 1e-5
N_HEADS = 8
QK_NOPE = 64
QK_ROPE = 32
QK_HEAD = QK_NOPE + QK_ROPE
V_HEAD = 64
Q_LORA = 256
KV_LORA = 128
ROPE_THETA = 10000.0
ATTN_WIDTH = N_HEADS * V_HEAD
SSM_WIDTH = 256
SSM_GROUP = 16
SSM_GROUPS = SSM_WIDTH // SSM_GROUP
SSM_STATE = 64
SSM_FLAT = SSM_GROUPS * SSM_STATE
CONV_WIDTH = 256
CONV_K = 31
N_EGROUPS = 4
EXP_PER_GROUP = 8
N_EXPERTS = N_EGROUPS * EXP_PER_GROUP
D_FF_E = 512

LANES = 128
SUBLANES = 8
HEAD_PAD = LANES
IN_PROJ_PAD = Q_LORA + KV_LORA + SSM_WIDTH + 2 * CONV_WIDTH + 2 * LANES
CONV_HALO = 32
VMEM_LIMIT = 48 * 1024 * 1024

TM_PROJ = 512
TQ_ATTN = 256
TC_SSM = 128
SSM_PARTS = 2
CONV_SUB = 128
TM_OUT = 512
TM_MOVE = 256
RUN_ROWS = SUBLANES
RUN_SLOTS = 2 * N_EXPERTS + 2 * TM_MOVE // RUN_ROWS
RUN_GROUP = 8
SC_CORES, SC_SUBCORES = 2, 16
SC_CHUNK = 64
TB_EXP = 512
EXPERT_TAIL_DIVISORS = (1, 2, 4)
FF_CHUNK = 256

BF16 = jnp.bfloat16
F32 = jnp.float32
U32 = jnp.uint32
D_PACK = D_MODEL // 2


def _pack_rows(v):
    bits = lax.bitcast_convert_type(v.astype(BF16).astype(F32), U32)
    half = v.shape[1] // 2
    return bits[:, 0:half] | (bits[:, half:] >> 16)


def _unpack_rows(w):
    hi = lax.bitcast_convert_type(w & jnp.uint32(0xFFFF0000), F32)
    lo = lax.bitcast_convert_type(w << 16, F32)
    return jnp.concatenate([hi, lo], axis=1).astype(BF16)


def _rms(x, w):
    return x * lax.rsqrt(jnp.mean(x * x, axis=-1, keepdims=True) + EPS) * w


def _params(*sem):
    return pltpu.CompilerParams(dimension_semantics=sem, vmem_limit_bytes=VMEM_LIMIT)


def _in_proj_kernel(x_ref, rope_ref, mixn_ref, win_ref, qan_ref, wuq_ref, kvan_ref, wuk_ref, wuv_ref,
                    qn_ref, kn_ref, q_ref, k_ref, v_ref, u_ref, ca_ref, cg_ref):
    x = x_ref[0]
    h = _rms(x, mixn_ref[0]).astype(BF16)
    head_cols = Q_LORA + KV_LORA + 2 * LANES
    proj = jnp.dot(h, win_ref[0, :, 0:head_cols], preferred_element_type=F32)
    o = 0
    c_q = proj[:, o:o + Q_LORA]; o += Q_LORA
    c_kv = proj[:, o:o + KV_LORA]; o += KV_LORA
    k_pe = proj[:, o:o + LANES]; o += LANES
    k_pe_sw = proj[:, o:o + LANES]

    width = N_HEADS * HEAD_PAD
    q2 = jnp.dot(_rms(c_q, qan_ref[0]).astype(BF16), wuq_ref[0], preferred_element_type=F32)
    q, q_sw = q2[:, 0:width], q2[:, width:2 * width]
    ckv_n = _rms(c_kv, kvan_ref[0]).astype(BF16)
    kn = jnp.dot(ckv_n, wuk_ref[0], preferred_element_type=F32)
    v_ref[0] = jnp.dot(ckv_n, wuv_ref[0], preferred_element_type=F32).astype(BF16)

    rope = rope_ref[0]
    half = QK_ROPE // 2
    lane = lax.broadcasted_iota(jnp.int32, rope.shape, 1)
    lo = (lane >= QK_NOPE) & (lane < QK_NOPE + half)
    hi = (lane >= QK_NOPE + half) & (lane < QK_HEAD)
    cos_t = jnp.where(lane < QK_NOPE, 1.0, jnp.where(lo, rope, jnp.where(hi, pltpu.roll(rope, half, 1), 0.0)))
    sin_t = jnp.where(lo, -pltpu.roll(rope, LANES - half, 1), jnp.where(hi, rope, 0.0))
    a_q, b_q = qn_ref[0, 0:1, :] * cos_t, qn_ref[0, 1:2, :] * sin_t
    a_k = kn_ref[0, 0:1, :] * cos_t
    k_sw_term = k_pe_sw * (kn_ref[0, 1:2, :] * sin_t)

    def inv_rms(y):
        return lax.rsqrt(jnp.sum(y * y, axis=-1, keepdims=True) * (1.0 / QK_HEAD) + EPS)

    def rest(j):
        cols = slice(head_cols + j * SSM_WIDTH, head_cols + (j + 1) * SSM_WIDTH)
        return jnp.dot(h, win_ref[0, :, cols], preferred_element_type=F32).astype(BF16)

    for hd in range(N_HEADS):
        if hd == 0:
            u_ref[...] = rest(0)
        elif hd == 3:
            ca_ref[0] = rest(1)
        elif hd == 6:
            cg_ref[0] = rest(2)
        sl = slice(hd * HEAD_PAD, (hd + 1) * HEAD_PAD)
        yq = q[:, sl]
        q_ref[0, hd] = ((yq * a_q + q_sw[:, sl] * b_q) * inv_rms(yq)).astype(BF16)
        yk = kn[:, sl] + k_pe
        k_ref[0, hd] = ((yk * a_k + k_sw_term) * inv_rms(yk)).astype(BF16)


def _in_proj(x, rope, mixn, win, qan, wuq, kvan, wuk, wuv, qn, kn, layer):
    B, L, D = x.shape
    tm = min(TM_PROJ, L)
    full = lambda shape: pl.BlockSpec((1,) + shape, lambda b, t: (layer,) + (0,) * len(shape))
    return pl.pallas_call(
        _in_proj_kernel,
        grid=(B, L // tm),
        in_specs=[
            pl.BlockSpec((1, tm, D), lambda b, t: (b, t, 0)),
            pl.BlockSpec((1, tm, LANES), lambda b, t: (b, t, 0)),
            full((1, D)), full((D, IN_PROJ_PAD)), full((1, Q_LORA)), full((Q_LORA, 2 * N_HEADS * HEAD_PAD)),
            full((1, KV_LORA)), full((KV_LORA, N_HEADS * HEAD_PAD)), full((KV_LORA, ATTN_WIDTH)),
            full((2, HEAD_PAD)), full((2, HEAD_PAD)),
        ],
        out_specs=[
            pl.BlockSpec((1, N_HEADS, tm, HEAD_PAD), lambda b, t: (b, 0, t, 0)),
            pl.BlockSpec((1, N_HEADS, tm, HEAD_PAD), lambda b, t: (b, 0, t, 0)),
            pl.BlockSpec((1, tm, ATTN_WIDTH), lambda b, t: (b, t, 0)),
            pl.BlockSpec((tm, SSM_WIDTH), lambda b, t: (t, b)),
            pl.BlockSpec((1, tm, CONV_WIDTH), lambda b, t: (b, t, 0)),
            pl.BlockSpec((1, tm, CONV_WIDTH), lambda b, t: (b, t, 0)),
        ],
        out_shape=[
            jax.ShapeDtypeStruct((B, N_HEADS, L, HEAD_PAD), BF16),
            jax.ShapeDtypeStruct((B, N_HEADS, L, HEAD_PAD), BF16),
            jax.ShapeDtypeStruct((B, L, ATTN_WIDTH), BF16),
            jax.ShapeDtypeStruct((L, B * SSM_WIDTH), BF16),
            jax.ShapeDtypeStruct((B, L, CONV_WIDTH), BF16),
            jax.ShapeDtypeStruct((B, L, CONV_WIDTH), BF16),
        ],
        compiler_params=_params("parallel", "parallel"),
    )(x, rope, mixn, win, qan, wuq, kvan, wuk, wuv, qn, kn)


def _attention_kernel(q_ref, k_ref, v_ref, o_ref, vext_ref, *, seq, tq):
    pair = 2 * V_HEAD
    vext_ref[:, 0:pair] = v_ref[0]
    vext_ref[:, pair:] = jnp.ones((seq, LANES), BF16)
    row_chunk = lax.broadcasted_iota(jnp.int32, (tq, tq), 0) // CHUNK
    col_chunk = lax.broadcasted_iota(jnp.int32, (tq, tq), 1) // CHUNK
    visible = col_chunk <= row_chunk
    neg = jnp.finfo(F32).min
    nt = (((1,), (1,)), ((), ()))
    low_lanes = lax.broadcasted_iota(jnp.int32, (tq, pair), 1) < V_HEAD

    def scores(i, hh):
        q0 = i * tq
        qb = q_ref[0, hh, q0:q0 + tq, :]
        s_d = lax.dot_general(qb, k_ref[0, hh, q0:q0 + tq, :], nt, preferred_element_type=F32)
        s_d = jnp.where(visible, s_d, neg)
        s_l = lax.dot_general(qb, k_ref[0, hh, 0:q0, :], nt, preferred_element_type=F32) if i > 0 else None
        return s_d, s_l

    def weighted_values(i, s_d, s_l):
        q0 = i * tq
        m = jnp.max(s_d, axis=-1, keepdims=True)
        if s_l is not None:
            m = jnp.maximum(m, jnp.max(s_l, axis=-1, keepdims=True))
        acc = jnp.dot(jnp.exp(s_d - m).astype(BF16), vext_ref[q0:q0 + tq, :], preferred_element_type=F32)
        if s_l is not None:
            acc = acc + jnp.dot(jnp.exp(s_l - m).astype(BF16), vext_ref[0:q0, :], preferred_element_type=F32)
        return acc[:, 0:pair] / acc[:, pair:]

    nq = seq // tq
    order = [x for p in zip(reversed(range(nq)), range(nq)) for x in p][:nq]
    chains = [(i, hh) for i in order for hh in range(2)]
    pending = scores(*chains[0])
    outs = {}
    for n, (i, hh) in enumerate(chains):
        upcoming = scores(*chains[n + 1]) if n + 1 < len(chains) else None
        outs[hh] = weighted_values(i, *pending)
        pending = upcoming
        if hh == 1:
            o_ref[0, i * tq:(i + 1) * tq, :] = jnp.where(low_lanes, outs[0], outs[1]).astype(BF16)


def _attention(q, k, v):
    B, H, L, _ = q.shape
    tq = min(TQ_ATTN, L)
    return pl.pallas_call(
        functools.partial(_attention_kernel, seq=L, tq=tq),
        grid=(B, H // 2),
        in_specs=[
            pl.BlockSpec((1, 2, L, HEAD_PAD), lambda b, p: (b, p, 0, 0)),
            pl.BlockSpec((1, 2, L, HEAD_PAD), lambda b, p: (b, p, 0, 0)),
            pl.BlockSpec((1, L, 2 * V_HEAD), lambda b, p: (b, 0, p)),
        ],
        out_specs=pl.BlockSpec((1, L, 2 * V_HEAD), lambda b, p: (b, 0, p)),
        out_shape=jax.ShapeDtypeStruct((B, L, ATTN_WIDTH), BF16),
        scratch_shapes=[pltpu.VMEM((L, 2 * V_HEAD + LANES), BF16)],
        compiler_params=_params("parallel", "parallel"),
    )(q, k, v)


def _ssm_kernel(u_ref, bbd_ref, lam_ref, cbd_ref, d_ref, wglu_ref, g_ref, o_ref, state_ref, *bu_refs, batch, tc):
    @pl.when(pl.program_id(0) == 0)
    def _():
        state_ref[...] = jnp.zeros_like(state_ref)

    steps = tc // SSM_PARTS
    halves = tuple((bu, slice(p * steps * batch, (p + 1) * steps * batch)) for p, bu in enumerate(bu_refs))
    for bu_ref, rows in halves:
        bu_ref[...] = jnp.dot(u_ref[rows, :], bbd_ref[0], preferred_element_type=F32)
    lam_re = jnp.broadcast_to(lam_ref[0, 0:1, :], (batch, SSM_FLAT))
    lam_im = jnp.broadcast_to(lam_ref[0, 1:2, :], (batch, SSM_FLAT))

    carry = (state_ref[:, 0:SSM_FLAT], state_ref[:, SSM_FLAT:2 * SSM_FLAT])
    for bu_ref, rows in halves:
        def step(t, xs, bu_ref=bu_ref):
            xr, xi = xs
            at = pl.ds(pl.multiple_of(t * batch, batch), batch)
            nr = lam_re * xr - lam_im * xi + bu_ref[at, 0:SSM_FLAT]
            ni = lam_re * xi + lam_im * xr + bu_ref[at, SSM_FLAT:2 * SSM_FLAT]
            bu_ref[at, 0:SSM_FLAT] = nr
            bu_ref[at, SSM_FLAT:2 * SSM_FLAT] = ni
            return nr, ni

        carry = lax.fori_loop(0, steps, step, carry, unroll=True)
        y = jnp.dot(bu_ref[...].astype(BF16), cbd_ref[0], preferred_element_type=F32)
        y = y + d_ref[0] * u_ref[rows, :].astype(F32)
        z = jax.nn.gelu(y)
        gate = jax.nn.sigmoid(jnp.dot(z.astype(BF16), wglu_ref[0], preferred_element_type=F32))
        o_ref[rows, :] = _rms(z * gate, g_ref[0]).astype(BF16)
    state_ref[:, 0:SSM_FLAT] = carry[0]
    state_ref[:, SSM_FLAT:2 * SSM_FLAT] = carry[1]


def _ssm(u_tm, bbd, lam, cbd, d, wglu, g, batch, layer):
    rows = u_tm.shape[0]
    seq = rows // batch
    tc = min(TC_SSM, seq)
    blk = tc * batch
    full = lambda shape: pl.BlockSpec((1,) + shape, lambda t: (layer,) + (0,) * len(shape))
    return pl.pallas_call(
        functools.partial(_ssm_kernel, batch=batch, tc=tc),
        grid=(seq // tc,),
        in_specs=[
            pl.BlockSpec((blk, SSM_WIDTH), lambda t: (t, 0)),
            full((SSM_WIDTH, 2 * SSM_FLAT)), full((2, SSM_FLAT)), full((2 * SSM_FLAT, SSM_WIDTH)),
            full((1, SSM_WIDTH)), full((SSM_WIDTH, SSM_WIDTH)), full((1, SSM_WIDTH)),
        ],
        out_specs=pl.BlockSpec((blk, SSM_WIDTH), lambda t: (t, 0)),
        out_shape=jax.ShapeDtypeStruct((rows, SSM_WIDTH), BF16),
        scratch_shapes=[pltpu.VMEM((batch, 2 * SSM_FLAT), F32)]
        + [pltpu.VMEM((blk // SSM_PARTS, 2 * SSM_FLAT), F32)] * SSM_PARTS,
        compiler_params=_params("arbitrary"),
    )(u_tm, bbd, lam, cbd, d, wglu, g)


def _conv_rows(upad_ref, r0, w_ref, b_ref, lnw_ref, lnb_ref, gn_ref):
    base = CONV_HALO - (CONV_K - 1)
    acc = jnp.zeros((CONV_SUB, CONV_WIDTH), F32)
    for off in range(SUBLANES):
        n = CONV_SUB + (SUBLANES if off else 0)
        part = None
        for a8 in range(0, CONV_HALO + 1, SUBLANES):
            kk = a8 + off - base
            if 0 <= kk < CONV_K:
                term = w_ref[0, kk:kk + 1, :] * upad_ref[r0 + a8:r0 + a8 + n, :]
                part = term if part is None else part + term
        acc = acc + part[off:off + CONV_SUB]
    y = acc + b_ref[0]
    mu = jnp.mean(y, axis=-1, keepdims=True)
    var = jnp.mean(jnp.square(y - mu), axis=-1, keepdims=True)
    y = (y - mu) * lax.rsqrt(var + LN_EPS) * lnw_ref[0] + lnb_ref[0]
    y = y * jax.nn.sigmoid(y)
    return _rms(y, gn_ref[0])


ROUTE_E1, ROUTE_E2, ROUTE_R1, ROUTE_R2, ROUTE_G1, ROUTE_G2 = range(6)
ROUTE_ROWS = 8
GRP_LANE0 = N_EXPERTS


def _out_router_kernel(x_ref, ya_ref, ys_ref, ca_ref, cg_ref, ga_ref, wo_ref, fn_ref, wr_ref, br_ref,
                       cw_ref, cb_ref, clnw_ref, clnb_ref, cgn_ref,
                       x1_ref, h2_ref, route_ref, route_t_ref, cnt_ref, sub_ref,
                       carry_ref, tri_ref, upad_ref, yc_ref, *, tm, per_b):
    @pl.when(pl.program_id(0) == 0)
    def _():
        carry_ref[...] = jnp.zeros_like(carry_ref)
        rr = lax.broadcasted_iota(jnp.int32, (tm, tm), 0)
        cc = lax.broadcasted_iota(jnp.int32, (tm, tm), 1)
        tri_ref[...] = (cc < rr).astype(BF16)

    first = pl.program_id(0) % per_b == 0
    upad_ref[0:CONV_HALO, :] = jnp.where(first, 0.0, upad_ref[tm:tm + CONV_HALO, :])
    upad_ref[CONV_HALO:, :] = ca_ref[...].astype(F32) * jax.nn.sigmoid(cg_ref[...].astype(F32))
    ya = _rms(ya_ref[...].astype(F32), ga_ref[0]).astype(BF16)
    acc = jnp.dot(ya, wo_ref[0, 0:ATTN_WIDTH, :], preferred_element_type=F32)
    acc += jnp.dot(ys_ref[...], wo_ref[0, ATTN_WIDTH:ATTN_WIDTH + SSM_WIDTH, :], preferred_element_type=F32)
    for r0 in range(0, tm, CONV_SUB):
        yc_ref[r0:r0 + CONV_SUB, :] = _conv_rows(upad_ref, r0, cw_ref, cb_ref, clnw_ref, clnb_ref, cgn_ref).astype(BF16)
    acc += jnp.dot(yc_ref[...], wo_ref[0, ATTN_WIDTH + SSM_WIDTH:, :], preferred_element_type=F32)
    x1 = x_ref[...] + acc
    x1_ref[...] = x1
    h2 = _rms(x1, fn_ref[0])
    h2_ref[...] = _pack_rows(h2)

    h_hi = h2.astype(BF16)
    h_lo = (h2 - h_hi.astype(F32)).astype(BF16)
    part = jnp.dot(h_hi, wr_ref[0], preferred_element_type=F32)
    logits = (part[:, 0:LANES] + part[:, LANES:2 * LANES]
              + jnp.dot(h_lo, wr_ref[0, :, 0:LANES], preferred_element_type=F32) + br_ref[0])
    lane = lax.broadcasted_iota(jnp.int32, (tm, LANES), 1)
    ninf = -jnp.inf
    big = LANES

    def first_argmax(vals, vmax):
        return jnp.min(jnp.where(vals == vmax, lane, big), axis=-1, keepdims=True)

    grp = jnp.where((lane >= GRP_LANE0) & (lane < GRP_LANE0 + N_EGROUPS), logits, ninf)
    gmax = jnp.max(grp, axis=-1, keepdims=True)
    gsel = first_argmax(grp, gmax) - GRP_LANE0
    p_grp = 1.0 / jnp.sum(jnp.exp(grp - gmax), axis=-1, keepdims=True)

    el = jnp.where((lane < N_EXPERTS) & ((lane // EXP_PER_GROUP) == gsel), logits, ninf)
    m1 = jnp.max(el, axis=-1, keepdims=True)
    e1 = first_argmax(el, m1)
    el2 = jnp.where(lane == e1, ninf, el)
    m2 = jnp.max(el2, axis=-1, keepdims=True)
    e2 = first_argmax(el2, m2)
    t2 = jnp.exp(m2 - m1)
    g1 = p_grp / (1.0 + t2)
    g2 = p_grp * t2 / (1.0 + t2)

    hit1 = lane == e1
    hit2 = lane == e2
    cnt = (hit1 | hit2).astype(F32)
    before = jnp.dot(tri_ref[...], cnt.astype(BF16), preferred_element_type=F32) + carry_ref[...]
    r1 = jnp.sum(jnp.where(hit1, before, 0.0), axis=-1, keepdims=True)
    r2 = jnp.sum(jnp.where(hit2, before, 0.0), axis=-1, keepdims=True)
    subs = [before[h * TM_MOVE:h * TM_MOVE + 1, :] for h in range(tm // TM_MOVE)]
    sub_ref[...] = jnp.concatenate(subs + [jnp.zeros((SUBLANES - len(subs), LANES), F32)], axis=0)
    carry_ref[...] += jnp.sum(cnt, axis=0, keepdims=True)
    cnt_ref[...] = carry_ref[...]

    rec = jnp.where(lane == ROUTE_E1, e1.astype(F32), 0.0)
    rec = jnp.where(lane == ROUTE_E2, e2.astype(F32), rec)
    rec = jnp.where(lane == ROUTE_R1, r1, rec)
    rec = jnp.where(lane == ROUTE_R2, r2, rec)
    rec = jnp.where(lane == ROUTE_G1, g1, rec)
    rec = jnp.where(lane == ROUTE_G2, g2, rec)
    route_ref[...] = rec
    route_t_ref[...] = rec.T[0:ROUTE_ROWS, :]


def _out_router(x, ya, ys_tm, ca, cg, ga, wo, fn, wr, br, conv_params, batch, layer):
    N, D = x.shape
    seq = N // batch
    tm = min(TM_OUT, seq)
    per_b = seq // tm
    full = lambda shape: pl.BlockSpec((1,) + shape, lambda i: (layer,) + (0,) * len(shape))
    whole = lambda shape: pl.BlockSpec(shape, lambda i: (0,) * len(shape))
    tile = lambda w: pl.BlockSpec((tm, w), lambda i: (i, 0))
    return pl.pallas_call(
        functools.partial(_out_router_kernel, tm=tm, per_b=per_b),
        grid=(N // tm,),
        in_specs=[
            tile(D), tile(ATTN_WIDTH),
            pl.BlockSpec((tm, SSM_WIDTH), lambda i: (i % per_b, i // per_b)),
            tile(CONV_WIDTH), tile(CONV_WIDTH),
            full((1, ATTN_WIDTH)), full((D, D)), full((1, D)), full((D, 2 * LANES)), full((1, LANES)),
            full((CONV_K, CONV_WIDTH)), full((1, CONV_WIDTH)), full((1, CONV_WIDTH)), full((1, CONV_WIDTH)),
            full((1, CONV_WIDTH)),
        ],
        out_specs=[tile(D), tile(D_PACK), tile(LANES), pl.BlockSpec((ROUTE_ROWS, tm), lambda i: (0, i)),
                   whole((1, LANES)), pl.BlockSpec((SUBLANES, LANES), lambda i: (i, 0))],
        out_shape=[
            jax.ShapeDtypeStruct((N, D), F32),
            jax.ShapeDtypeStruct((N, D_PACK), U32),
            jax.ShapeDtypeStruct((N, LANES), F32),
            jax.ShapeDtypeStruct((ROUTE_ROWS, N), F32),
            jax.ShapeDtypeStruct((1, LANES), F32),
            jax.ShapeDtypeStruct((N // tm * SUBLANES, LANES), F32),
        ],
        scratch_shapes=[pltpu.VMEM((1, LANES), F32), pltpu.VMEM((tm, tm), BF16),
                        pltpu.VMEM((tm + CONV_HALO, CONV_WIDTH), F32), pltpu.VMEM((tm, CONV_WIDTH), BF16)],
        compiler_params=_params("arbitrary"),
    )(x, ya, ys_tm, ca, cg, ga, wo, fn, wr, br, *conv_params)


def _dispatch(dest_flat, h2, n_rows):
    N, D = h2.shape
    workers = SC_CORES * SC_SUBCORES
    per_worker = N // workers
    chunks = per_worker // SC_CHUNK
    idx = dest_flat.reshape(2, workers, chunks, SC_CHUNK).transpose(1, 0, 2, 3).reshape(workers, 2 * chunks, SC_CHUNK)
    mesh = plsc.VectorSubcoreMesh(core_axis_name="c", subcore_axis_name="s")

    @functools.partial(
        pl.kernel, mesh=mesh,
        out_type=jax.ShapeDtypeStruct((n_rows, D), h2.dtype),
        scratch_types=[pltpu.VMEM((2 * chunks, SC_CHUNK), jnp.int32), pltpu.VMEM((2, SC_CHUNK, D), h2.dtype),
                       pltpu.SemaphoreType.DMA((2,)), pltpu.SemaphoreType.DMA((2, 2))],
    )
    def scatter(h_hbm, idx_hbm, out_hbm, idx_v, rows_v, sem_in, sem_out):
        wid = lax.axis_index("s") * SC_CORES + lax.axis_index("c")
        pltpu.sync_copy(idx_hbm.at[wid], idx_v)

        def load(c):
            first = pl.multiple_of(wid * per_worker + c * SC_CHUNK, SC_CHUNK)
            return pltpu.async_copy(h_hbm.at[pl.ds(first, SC_CHUNK)], rows_v.at[c % 2], sem_in.at[c % 2])

        def store(c, k):
            return pltpu.async_copy(rows_v.at[c % 2], out_hbm.at[idx_v.at[k * chunks + c]], sem_out.at[c % 2, k])

        loads = {0: load(0)}
        stores = {}
        for c in range(chunks):
            loads[c].wait()
            if c >= 1:
                for st in stores[c - 1]:
                    st.wait()
            if c + 1 < chunks:
                loads[c + 1] = load(c + 1)
            stores[c] = (store(c, 0), store(c, 1))
        for st in stores[chunks - 1]:
            st.wait()

    return scatter(h2, idx)


def _experts_kernel(blk_ref, exp_ref, valid_ref, slot_ref, next_ref, used_ref, x_ref, w1_hbm, w3_hbm, w2_hbm, y_ref,
                    w1_f, w3_f, w2_f, w13_s, w2_s, sem, *, layer):
    j = pl.program_id(0)

    def weight_copies(e, slot):
        return (pltpu.make_async_copy(w1_hbm.at[layer, e], w1_f.at[slot], sem.at[slot, 0]),
                pltpu.make_async_copy(w3_hbm.at[layer, e], w3_f.at[slot], sem.at[slot, 1]),
                pltpu.make_async_copy(w2_hbm.at[layer, e], w2_f.at[slot], sem.at[slot, 2]))

    @pl.when(j < used_ref[0])
    def _():
        e = exp_ref[j]
        slot = slot_ref[j]

        @pl.when(j == 0)
        def _():
            for cp in weight_copies(e, slot):
                cp.start()

        @pl.when((j == 0) | (e != exp_ref[jnp.maximum(j - 1, 0)]))
        def _():
            for cp in weight_copies(e, slot):
                cp.wait()
            for c in range(D_FF_E // FF_CHUNK):
                w13_s[:, 2 * FF_CHUNK * c:2 * FF_CHUNK * c + FF_CHUNK] = \
                    w1_f[slot, :, FF_CHUNK * c:FF_CHUNK * (c + 1)].astype(BF16)
                w13_s[:, 2 * FF_CHUNK * c + FF_CHUNK:2 * FF_CHUNK * (c + 1)] = \
                    w3_f[slot, :, FF_CHUNK * c:FF_CHUNK * (c + 1)].astype(BF16)
            w2_s[...] = w2_f[slot].astype(BF16)

            @pl.when(next_ref[j] >= 0)
            def _():
                for cp in weight_copies(next_ref[j], 1 - slot):
                    cp.start()

        valid = valid_ref[j]

        def ffn(rows):
            x = _unpack_rows(x_ref[0:rows, :])
            y = None
            for c in range(D_FF_E // FF_CHUNK):
                ab = jnp.dot(x, w13_s[:, 2 * FF_CHUNK * c:2 * FF_CHUNK * (c + 1)], preferred_element_type=F32)
                a = ab[:, 0:FF_CHUNK]
                hmid = (a * jax.nn.sigmoid(a) * ab[:, FF_CHUNK:]).astype(BF16)
                part = jnp.dot(hmid, w2_s[FF_CHUNK * c:FF_CHUNK * (c + 1), :], preferred_element_type=F32)
                y = part if y is None else y + part
            row = lax.broadcasted_iota(jnp.int32, (rows, 1), 0)
            y_ref[0:rows, :] = jnp.where(row < valid, _pack_rows(y), jnp.uint32(0))
            if rows < TB_EXP:
                y_ref[rows:, :] = jnp.zeros((TB_EXP - rows, D_PACK), U32)

        sizes = [TB_EXP // d for d in EXPERT_TAIL_DIVISORS]
        for n, rows in enumerate(sizes):
            below = sizes[n + 1] if n + 1 < len(sizes) else 0

            @pl.when((valid > below) & (valid <= rows) if n else valid > below)
            def _(rows=rows):
                ffn(rows)


def _experts(blk_map, blk_exp, blk_valid, blk_slot, blk_next, n_used, x_rows, w1, w3, w2, layer):
    R, D = x_rows.shape[0], D_MODEL
    nb = R // TB_EXP
    row_block = lambda j, bm, be, bv, bs, bn, nu: (bm[j], 0)
    return pl.pallas_call(
        functools.partial(_experts_kernel, layer=layer),
        grid_spec=pltpu.PrefetchScalarGridSpec(
            num_scalar_prefetch=6,
            grid=(nb,),
            in_specs=[
                pl.BlockSpec((TB_EXP, D_PACK), row_block),
                pl.BlockSpec(memory_space=pl.ANY), pl.BlockSpec(memory_space=pl.ANY), pl.BlockSpec(memory_space=pl.ANY),
            ],
            out_specs=pl.BlockSpec((TB_EXP, D_PACK), row_block),
            scratch_shapes=[
                pltpu.VMEM((2, D, D_FF_E), F32), pltpu.VMEM((2, D, D_FF_E), F32), pltpu.VMEM((2, D_FF_E, D), F32),
                pltpu.VMEM((D, 2 * D_FF_E), BF16), pltpu.VMEM((D_FF_E, D), BF16),
                pltpu.SemaphoreType.DMA((2, 3)),
            ],
        ),
        out_shape=jax.ShapeDtypeStruct((R, D_PACK), U32),
        compiler_params=_params("arbitrary"),
    )(blk_map, blk_exp, blk_valid, blk_slot, blk_next, n_used, x_rows, w1, w3, w2)


def _combine_kernel(src_ref, ngroup_ref, x1_ref, route_ref, off_ref, rows_ref, o_ref, buf, sem, *, tm):
    i = pl.program_id(0)
    cur = i % 2

    def slot_copy(tile, half, s):
        src = pl.multiple_of(src_ref[tile * RUN_SLOTS + s], RUN_ROWS)
        dst = buf.at[half, pl.ds(pl.multiple_of(s * RUN_ROWS, RUN_ROWS), RUN_ROWS)]
        return pltpu.make_async_copy(rows_ref.at[pl.ds(src, RUN_ROWS)], dst, sem.at[half])

    def fetch(tile, half):
        def issue(g, c):
            for u in range(RUN_GROUP):
                slot_copy(tile, half, g * RUN_GROUP + u).start()
            return c
        lax.fori_loop(0, ngroup_ref[tile], issue, 0)

    def wait(g, c):
        rows = RUN_GROUP * RUN_ROWS
        pltpu.make_async_copy(rows_ref.at[pl.ds(0, rows)], buf.at[cur, pl.ds(0, rows)], sem.at[cur]).wait()
        return c

    @pl.when(i == 0)
    def _():
        buf[...] = jnp.zeros_like(buf)
        fetch(0, 0)

    @pl.when(i + 1 < pl.num_programs(0))
    def _():
        fetch(i + 1, 1 - cur)

    route = route_ref[...]
    lane = lax.broadcasted_iota(jnp.int32, (tm, LANES), 1).astype(F32)
    off = off_ref[0]

    def buf_row(e_lane, r_lane):
        e = route[:, e_lane:e_lane + 1]
        base = jnp.sum(jnp.where(lane == e, off, 0.0), axis=-1, keepdims=True)
        return (base + route[:, r_lane:r_lane + 1]).astype(jnp.int32)

    col = lax.broadcasted_iota(jnp.int32, (tm, RUN_SLOTS * RUN_ROWS), 1)
    pick = (jnp.where(col == buf_row(ROUTE_E1, ROUTE_R1), route[:, ROUTE_G1:ROUTE_G1 + 1], 0.0)
            + jnp.where(col == buf_row(ROUTE_E2, ROUTE_R2), route[:, ROUTE_G2:ROUTE_G2 + 1], 0.0))
    lax.fori_loop(0, ngroup_ref[i], wait, 0)
    o_ref[...] = x1_ref[...] + jnp.dot(pick.astype(BF16), _unpack_rows(buf[cur]), preferred_element_type=F32)


def _combine(src, n_groups, x1, route, off, y_rows):
    N, D = x1.shape
    tm = min(TM_MOVE, N)
    return pl.pallas_call(
        functools.partial(_combine_kernel, tm=tm),
        grid_spec=pltpu.PrefetchScalarGridSpec(
            num_scalar_prefetch=2,
            grid=(N // tm,),
            in_specs=[
                pl.BlockSpec((tm, D), lambda i, s, n: (i, 0)),
                pl.BlockSpec((tm, LANES), lambda i, s, n: (i, 0)),
                pl.BlockSpec((1, 1, LANES), lambda i, s, n: (i, 0, 0)),
                pl.BlockSpec(memory_space=pl.ANY),
            ],
            out_specs=pl.BlockSpec((tm, D), lambda i, s, n: (i, 0)),
            scratch_shapes=[pltpu.VMEM((2, RUN_SLOTS * RUN_ROWS, D_PACK), U32), pltpu.SemaphoreType.DMA((2,))],
        ),
        out_shape=jax.ShapeDtypeStruct((N, D), F32),
        compiler_params=_params("arbitrary"),
    )(src, n_groups, x1, route, off, y_rows)


def _moe(x1, h2, route, route_t, counts, sub_carry, w1, w3, w2, layer):
    N, _ = x1.shape
    tm = min(TM_MOVE, N)
    n_tiles = N // tm
    experts = jnp.arange(N_EXPERTS, dtype=jnp.int32)
    nb = (2 * N + N_EXPERTS * (TB_EXP - 1)) // TB_EXP + 1
    e_id = route_t[ROUTE_E1:ROUTE_E2 + 1].astype(jnp.int32)
    rank = route_t[ROUTE_R1:ROUTE_R2 + 1].astype(jnp.int32)
    cnt = counts[0, :N_EXPERTS].astype(jnp.int32)
    padded = (cnt + TB_EXP - 1) // TB_EXP * TB_EXP
    pad_end = jnp.cumsum(padded)
    pad_start = pad_end - padded
    e_flat = e_id.reshape(1, 2 * N)
    dest = jnp.sum(jnp.where(e_flat == experts[:, None], pad_start[:, None], 0), axis=0) + rank.reshape(2 * N)
    x_rows = _dispatch(dest, h2, nb * TB_EXP)
    n_used = pad_end[-1] // TB_EXP
    blk_map = jnp.minimum(jnp.arange(nb, dtype=jnp.int32), jnp.maximum(n_used - 1, 0))
    blk_exp = jnp.sum(pad_end[None, :] <= (blk_map * TB_EXP)[:, None], axis=1).astype(jnp.int32)
    blk_exp = jnp.minimum(blk_exp, N_EXPERTS - 1)
    is_blk_e = blk_exp[:, None] == experts[None, :]
    blk_valid = jnp.clip(jnp.sum(jnp.where(is_blk_e, (pad_start + cnt)[None, :], 0), axis=1) - blk_map * TB_EXP,
                         0, TB_EXP).astype(jnp.int32)
    has_rows = cnt > 0
    slot_of_e = (jnp.cumsum(has_rows.astype(jnp.int32)) - 1) % 2
    later = jnp.where(has_rows, experts, N_EXPERTS)
    next_of_e = lax.cummin(jnp.concatenate([later[1:], jnp.full((1,), N_EXPERTS, jnp.int32)]), reverse=True)
    next_of_e = jnp.where(next_of_e < N_EXPERTS, next_of_e, -1)
    blk_slot = jnp.sum(jnp.where(is_blk_e, slot_of_e[None, :], 0), axis=1).astype(jnp.int32)
    blk_next = jnp.sum(jnp.where(is_blk_e, next_of_e[None, :], 0), axis=1).astype(jnp.int32)

    per_router_tile = TM_OUT // tm if N >= TM_OUT else 1
    carry = sub_carry.reshape(-1, SUBLANES, LANES)[:, :per_router_tile, :N_EXPERTS].reshape(n_tiles, N_EXPERTS)
    carry = carry.astype(jnp.int32)
    tile_cnt = jnp.concatenate([carry[1:], cnt[None, :]], axis=0) - carry
    run_start = pad_start[None, :] + carry
    first_blk = run_start // RUN_ROWS
    nslot = jnp.where(tile_cnt > 0, (run_start + tile_cnt - 1) // RUN_ROWS - first_blk + 1, 0)
    slot_end = jnp.cumsum(nslot, axis=1)
    slot_base = slot_end - nslot
    slots = jnp.arange(RUN_SLOTS, dtype=jnp.int32)
    slot_e = jnp.minimum(jnp.sum(slot_end[:, None, :] <= slots[None, :, None], axis=2), N_EXPERTS - 1)
    is_slot_e = slot_e[:, :, None] == experts[None, None, :]
    pick = lambda tbl: jnp.sum(jnp.where(is_slot_e, tbl[:, None, :], 0), axis=2)
    src = (pick(first_blk) + slots[None, :] - pick(slot_base)) * RUN_ROWS
    src = jnp.where(slots[None, :] < slot_end[:, -1:], src, 0)
    src = jnp.clip(src, 0, nb * TB_EXP - RUN_ROWS).reshape(-1).astype(jnp.int32)
    n_groups = (slot_end[:, -1] + RUN_GROUP - 1) // RUN_GROUP
    off = (slot_base - first_blk) * RUN_ROWS + pad_start[None, :]
    off = jnp.pad(off.astype(F32), ((0, 0), (0, LANES - N_EXPERTS)))

    y_rows = _experts(blk_map, blk_exp, blk_valid, blk_slot, blk_next, n_used.reshape(1).astype(jnp.int32),
                      x_rows, w1, w3, w2, layer)
    return _combine(src, n_groups.astype(jnp.int32), x1, route, off.reshape(n_tiles, 1, LANES), y_rows)


def _pad_heads(w, width):
    lead = w.shape[:-1]
    w = w.reshape(lead + (N_HEADS, width))
    pad = [(0, 0)] * (w.ndim - 1) + [(0, HEAD_PAD - width)]
    return jnp.pad(w, pad).reshape(lead + (N_HEADS * HEAD_PAD,))


def _swap_rope(w):
    half = QK_ROPE // 2
    lo, hi = w[..., QK_NOPE:QK_NOPE + half], w[..., QK_NOPE + half:QK_HEAD]
    pad = [(0, 0)] * (w.ndim - 1)
    return jnp.pad(jnp.concatenate([hi, lo], -1), pad + [(QK_NOPE, HEAD_PAD - QK_HEAD)])


def _rope_tables(positions):
    inv_freq = ROPE_THETA ** (-jnp.arange(0, QK_ROPE, 2, dtype=F32) / QK_ROPE)
    ang = positions.astype(F32)[..., None] * inv_freq
    table = jnp.concatenate([jnp.cos(ang), jnp.sin(ang)], -1)
    return jnp.pad(table, ((0, 0), (0, 0), (QK_NOPE, LANES - QK_HEAD)))


def _ssm_params(lam_re, lam_im, b_re, b_im, c_re, c_im, log_dt):
    depth = lam_re.shape[0]
    lam = lax.complex(lam_re, lam_im)
    dt = jnp.exp(log_dt)[..., None]
    lam_bar = jnp.exp(lam * dt)
    b_bar = ((lam_bar - 1.0) / lam)[..., None] * lax.complex(b_re, b_im)
    eye = jnp.eye(SSM_GROUPS, dtype=F32)

    def in_blockdiag(m):
        return jnp.einsum("lgpc,gh->lgchp", m, eye).reshape(depth, SSM_WIDTH, SSM_FLAT)

    def out_blockdiag(m):
        return jnp.einsum("lgcp,gh->lgphc", m, eye).reshape(depth, SSM_FLAT, SSM_WIDTH)

    bbd = jnp.concatenate([in_blockdiag(jnp.real(b_bar)), in_blockdiag(jnp.imag(b_bar))], axis=2)
    cbd = jnp.concatenate([out_blockdiag(c_re), out_blockdiag(-c_im)], axis=1)
    lam_rows = jnp.stack([jnp.real(lam_bar).reshape(depth, SSM_FLAT), jnp.imag(lam_bar).reshape(depth, SSM_FLAT)],
                         axis=1)
    return bbd.astype(BF16), lam_rows, cbd.astype(BF16)


def kernel(x, positions, mix_norm, w_in, q_a_norm, w_uq, kv_a_norm, w_ukv, q_norm, k_norm, ssm_lam_re, ssm_lam_im, ssm_b_re, ssm_b_im, ssm_c_re, ssm_c_im, ssm_d, ssm_log_dt, ssm_w_glu, conv_dw_w, conv_dw_b, conv_ln_w, conv_ln_b, out_norm, w_out, ffn_norm, w_grp, b_grp, w_exp, b_exp, w1, w3, w2):
    B, L, D = x.shape
    depth = w_in.shape[0]
    rope = _rope_tables(positions)
    rows3 = lambda v: v[:, None, :]
    lane_tail = lambda v: jnp.pad(v, [(0, 0)] * (v.ndim - 1) + [(0, LANES - v.shape[-1])])
    c_q, c_kv, k_pe, u_s, c_a, c_g = jnp.split(
        w_in, [Q_LORA, Q_LORA + KV_LORA, Q_LORA + KV_LORA + QK_ROPE, Q_LORA + KV_LORA + QK_ROPE + SSM_WIDTH,
               Q_LORA + KV_LORA + QK_ROPE + SSM_WIDTH + CONV_WIDTH], axis=2)
    k_pe_full = jnp.pad(k_pe, ((0, 0), (0, 0), (QK_NOPE, 0)))
    win = jnp.concatenate([c_q, c_kv, lane_tail(k_pe_full), _swap_rope(k_pe_full), u_s, c_a, c_g], axis=2).astype(BF16)
    wkv = w_ukv.reshape(depth, KV_LORA, N_HEADS, QK_NOPE + V_HEAD)
    wuk = _pad_heads(wkv[..., :QK_NOPE].reshape(depth, KV_LORA, N_HEADS * QK_NOPE), QK_NOPE).astype(BF16)
    wuv = wkv[..., QK_NOPE:].reshape(depth, KV_LORA, ATTN_WIDTH).astype(BF16)
    wuq_sw = _swap_rope(w_uq.reshape(depth, Q_LORA, N_HEADS, QK_HEAD)).reshape(depth, Q_LORA, N_HEADS * HEAD_PAD)
    wuq = jnp.concatenate([_pad_heads(w_uq, QK_HEAD), wuq_sw], axis=2).astype(BF16)
    norm_rows = lambda w: jnp.stack([lane_tail(w), _swap_rope(w)], axis=1)
    qn, kn = norm_rows(q_norm * (QK_HEAD ** -0.5)), norm_rows(k_norm)
    bbd, lam_rows, cbd = _ssm_params(ssm_lam_re, ssm_lam_im, ssm_b_re, ssm_b_im, ssm_c_re, ssm_c_im, ssm_log_dt)
    ssm_skip = rows3(ssm_d.reshape(depth, SSM_WIDTH))
    wglu = ssm_w_glu.astype(BF16)
    g_attn = rows3(out_norm[:, :ATTN_WIDTH])
    g_ssm = rows3(out_norm[:, ATTN_WIDTH:ATTN_WIDTH + SSM_WIDTH])
    conv_params = (conv_dw_w, rows3(conv_dw_b), rows3(conv_ln_w), rows3(conv_ln_b),
                   rows3(out_norm[:, ATTN_WIDTH + SSM_WIDTH:]))
    w_route = lane_tail(jnp.concatenate([w_exp, w_grp], axis=2))
    w_route_hi = w_route.astype(BF16)
    w_route = jnp.concatenate([w_route_hi, (w_route - w_route_hi.astype(F32)).astype(BF16)], axis=2)
    b_route = rows3(lane_tail(jnp.concatenate([b_exp, b_grp], axis=1)))
    wo = w_out.astype(BF16)
    for l in range(depth):
        q, k, v, u_tm, ca, cg = _in_proj(x, rope, rows3(mix_norm), win, rows3(q_a_norm), wuq, rows3(kv_a_norm), wuk, wuv,
                                         qn, kn, l)
        y_attn = _attention(q, k, v)
        y_ssm_tm = _ssm(u_tm.reshape(L * B, SSM_WIDTH), bbd, lam_rows, cbd, ssm_skip, wglu, g_ssm, B, l)
        x1, h2, route, route_t, counts, sub_carry = _out_router(
            x.reshape(B * L, D), y_attn.reshape(B * L, ATTN_WIDTH), y_ssm_tm.reshape(L, B * SSM_WIDTH),
            ca.reshape(B * L, CONV_WIDTH), cg.reshape(B * L, CONV_WIDTH), g_attn, wo, rows3(ffn_norm),
            w_route, b_route, conv_params, B, l)
        x = _moe(x1, h2, route, route_t, counts, sub_carry, w1, w3, w2, l).reshape(B, L, D)
    return x
```

```python
import functools

import jax
import jax.numpy as jnp
from jax import lax
from jax.experimental import pallas as pl
from jax.experimental.pallas import tpu as pltpu
from jax.experimental.pallas import tpu_sc as plsc

D_MODEL = 1024
CHUNK = 64
EPS = 1e-6
LN_EPS = 1e-5
N_HEADS = 8
QK_NOPE = 64
QK_ROPE = 32
QK_HEAD = QK_NOPE + QK_ROPE
V_HEAD = 64
Q_LORA = 256
KV_LORA = 128
ROPE_THETA = 10000.0
ATTN_WIDTH = N_HEADS * V_HEAD
SSM_WIDTH = 256
SSM_GROUP = 16
SSM_GROUPS = SSM_WIDTH // SSM_GROUP
SSM_STATE = 64
SSM_FLAT = SSM_GROUPS * SSM_STATE
CONV_WIDTH = 256
CONV_K = 31
N_EGROUPS = 4
EXP_PER_GROUP = 8
N_EXPERTS = N_EGROUPS * EXP_PER_GROUP
D_FF_E = 512

LANES = 128
SUBLANES = 8
HEAD_PAD = LANES
IN_PROJ_PAD = Q_LORA + KV_LORA + SSM_WIDTH + 2 * CONV_WIDTH + 2 * LANES
CONV_HALO = 32
VMEM_LIMIT = 48 * 1024 * 1024

TM_PROJ = 512
TQ_ATTN = 256
TC_SSM = 128
SSM_PARTS = 2
CONV_SUB = 128
TM_OUT = 512
TM_MOVE = 256
RUN_ROWS = SUBLANES
RUN_SLOTS = 2 * N_EXPERTS + 2 * TM_MOVE // RUN_ROWS
RUN_GROUP = 8
SC_CORES, SC_SUBCORES = 2, 16
SC_CHUNK = 64
TB_EXP = 512
EXPERT_TAIL_DIVISORS = (1, 2, 4)
FF_CHUNK = 256

BF16 = jnp.bfloat16
F32 = jnp.float32
U32 = jnp.uint32
D_PACK = D_MODEL // 2


def _pack_rows(v):
    bits = lax.bitcast_convert_type(v.astype(BF16).astype(F32), U32)
    half = v.shape[1] // 2
    return bits[:, 0:half] | (bits[:, half:] >> 16)


def _unpack_rows(w):
    hi = lax.bitcast_convert_type(w & jnp.uint32(0xFFFF0000), F32)
    lo = lax.bitcast_convert_type(w << 16, F32)
    return jnp.concatenate([hi, lo], axis=1).astype(BF16)


def _rms(x, w):
    return x * lax.rsqrt(jnp.mean(x * x, axis=-1, keepdims=True) + EPS) * w


def _params(*sem):
    return pltpu.CompilerParams(dimension_semantics=sem, vmem_limit_bytes=VMEM_LIMIT)


def _in_proj_kernel(x_ref, rope_ref, mixn_ref, win_ref, qan_ref, wuq_ref, kvan_ref, wuk_ref, wuv_ref,
                    qn_ref, kn_ref, q_ref, k_ref, v_ref, u_ref, ca_ref, cg_ref):
    x = x_ref[0]
    h = _rms(x, mixn_ref[0]).astype(BF16)
    head_cols = Q_LORA + KV_LORA + 2 * LANES
    proj = jnp.dot(h, win_ref[0, :, 0:head_cols], preferred_element_type=F32)
    o = 0
    c_q = proj[:, o:o + Q_LORA]; o += Q_LORA
    c_kv = proj[:, o:o + KV_LORA]; o += KV_LORA
    k_pe = proj[:, o:o + LANES]; o += LANES
    k_pe_sw = proj[:, o:o + LANES]

    width = N_HEADS * HEAD_PAD
    q2 = jnp.dot(_rms(c_q, qan_ref[0]).astype(BF16), wuq_ref[0], preferred_element_type=F32)
    q, q_sw = q2[:, 0:width], q2[:, width:2 * width]
    ckv_n = _rms(c_kv, kvan_ref[0]).astype(BF16)
    kn = jnp.dot(ckv_n, wuk_ref[0], preferred_element_type=F32)
    v_ref[0] = jnp.dot(ckv_n, wuv_ref[0], preferred_element_type=F32).astype(BF16)

    rope = rope_ref[0]
    half = QK_ROPE // 2
    lane = lax.broadcasted_iota(jnp.int32, rope.shape, 1)
    lo = (lane >= QK_NOPE) & (lane < QK_NOPE + half)
    hi = (lane >= QK_NOPE + half) & (lane < QK_HEAD)
    cos_t = jnp.where(lane < QK_NOPE, 1.0, jnp.where(lo, rope, jnp.where(hi, pltpu.roll(rope, half, 1), 0.0)))
    sin_t = jnp.where(lo, -pltpu.roll(rope, LANES - half, 1), jnp.where(hi, rope, 0.0))
    a_q, b_q = qn_ref[0, 0:1, :] * cos_t, qn_ref[0, 1:2, :] * sin_t
    a_k = kn_ref[0, 0:1, :] * cos_t
    k_sw_term = k_pe_sw * (kn_ref[0, 1:2, :] * sin_t)

    def inv_rms(y):
        return lax.rsqrt(jnp.sum(y * y, axis=-1, keepdims=True) * (1.0 / QK_HEAD) + EPS)

    def rest(j):
        cols = slice(head_cols + j * SSM_WIDTH, head_cols + (j + 1) * SSM_WIDTH)
        return jnp.dot(h, win_ref[0, :, cols], preferred_element_type=F32).astype(BF16)

    for hd in range(N_HEADS):
        if hd == 0:
            u_ref[...] = rest(0)
        elif hd == 3:
            ca_ref[0] = rest(1)
        elif hd == 6:
            cg_ref[0] = rest(2)
        sl = slice(hd * HEAD_PAD, (hd + 1) * HEAD_PAD)
        yq = q[:, sl]
        q_ref[0, hd] = ((yq * a_q + q_sw[:, sl] * b_q) * inv_rms(yq)).astype(BF16)
        yk = kn[:, sl] + k_pe
        k_ref[0, hd] = ((yk * a_k + k_sw_term) * inv_rms(yk)).astype(BF16)


def _in_proj(x, rope, mixn, win, qan, wuq, kvan, wuk, wuv, qn, kn, layer):
    B, L, D = x.shape
    tm = min(TM_PROJ, L)
    full = lambda shape: pl.BlockSpec((1,) + shape, lambda b, t: (layer,) + (0,) * len(shape))
    return pl.pallas_call(
        _in_proj_kernel,
        grid=(B, L // tm),
        in_specs=[
            pl.BlockSpec((1, tm, D), lambda b, t: (b, t, 0)),
            pl.BlockSpec((1, tm, LANES), lambda b, t: (b, t, 0)),
            full((1, D)), full((D, IN_PROJ_PAD)), full((1, Q_LORA)), full((Q_LORA, 2 * N_HEADS * HEAD_PAD)),
            full((1, KV_LORA)), full((KV_LORA, N_HEADS * HEAD_PAD)), full((KV_LORA, ATTN_WIDTH)),
            full((2, HEAD_PAD)), full((2, HEAD_PAD)),
        ],
        out_specs=[
            pl.BlockSpec((1, N_HEADS, tm, HEAD_PAD), lambda b, t: (b, 0, t, 0)),
            pl.BlockSpec((1, N_HEADS, tm, HEAD_PAD), lambda b, t: (b, 0, t, 0)),
            pl.BlockSpec((1, tm, ATTN_WIDTH), lambda b, t: (b, t, 0)),
            pl.BlockSpec((tm, SSM_WIDTH), lambda b, t: (t, b)),
            pl.BlockSpec((1, tm, CONV_WIDTH), lambda b, t: (b, t, 0)),
            pl.BlockSpec((1, tm, CONV_WIDTH), lambda b, t: (b, t, 0)),
        ],
        out_shape=[
            jax.ShapeDtypeStruct((B, N_HEADS, L, HEAD_PAD), BF16),
            jax.ShapeDtypeStruct((B, N_HEADS, L, HEAD_PAD), BF16),
            jax.ShapeDtypeStruct((B, L, ATTN_WIDTH), BF16),
            jax.ShapeDtypeStruct((L, B * SSM_WIDTH), BF16),
            jax.ShapeDtypeStruct((B, L, CONV_WIDTH), BF16),
            jax.ShapeDtypeStruct((B, L, CONV_WIDTH), BF16),
        ],
        compiler_params=_params("parallel", "parallel"),
    )(x, rope, mixn, win, qan, wuq, kvan, wuk, wuv, qn, kn)


def _attention_kernel(q_ref, k_ref, v_ref, o_ref, vext_ref, *, seq, tq):
    pair = 2 * V_HEAD
    vext_ref[:, 0:pair] = v_ref[0]
    vext_ref[:, pair:] = jnp.ones((seq, LANES), BF16)
    row_chunk = lax.broadcasted_iota(jnp.int32, (tq, tq), 0) // CHUNK
    col_chunk = lax.broadcasted_iota(jnp.int32, (tq, tq), 1) // CHUNK
    visible = col_chunk <= row_chunk
    neg = jnp.finfo(F32).min
    nt = (((1,), (1,)), ((), ()))
    low_lanes = lax.broadcasted_iota(jnp.int32, (tq, pair), 1) < V_HEAD

    def scores(i, hh):
        q0 = i * tq
        qb = q_ref[0, hh, q0:q0 + tq, :]
        s_d = lax.dot_general(qb, k_ref[0, hh, q0:q0 + tq, :], nt, preferred_element_type=F32)
        s_d = jnp.where(visible, s_d, neg)
        s_l = lax.dot_general(qb, k_ref[0, hh, 0:q0, :], nt, preferred_element_type=F32) if i > 0 else None
        return s_d, s_l

    def weighted_values(i, s_d, s_l):
        q0 = i * tq
        m = jnp.max(s_d, axis=-1, keepdims=True)
        if s_l is not None:
            m = jnp.maximum(m, jnp.max(s_l, axis=-1, keepdims=True))
        acc = jnp.dot(jnp.exp(s_d - m).astype(BF16), vext_ref[q0:q0 + tq, :], preferred_element_type=F32)
        if s_l is not None:
            acc = acc + jnp.dot(jnp.exp(s_l - m).astype(BF16), vext_ref[0:q0, :], preferred_element_type=F32)
        return acc[:, 0:pair] / acc[:, pair:]

    nq = seq // tq
    order = [x for p in zip(reversed(range(nq)), range(nq)) for x in p][:nq]
    chains = [(i, hh) for i in order for hh in range(2)]
    pending = scores(*chains[0])
    outs = {}
    for n, (i, hh) in enumerate(chains):
        upcoming = scores(*chains[n + 1]) if n + 1 < len(chains) else None
        outs[hh] = weighted_values(i, *pending)
        pending = upcoming
        if hh == 1:
            o_ref[0, i * tq:(i + 1) * tq, :] = jnp.where(low_lanes, outs[0], outs[1]).astype(BF16)


def _attention(q, k, v):
    B, H, L, _ = q.shape
    tq = min(TQ_ATTN, L)
    return pl.pallas_call(
        functools.partial(_attention_kernel, seq=L, tq=tq),
        grid=(B, H // 2),
        in_specs=[
            pl.BlockSpec((1, 2, L, HEAD_PAD), lambda b, p: (b, p, 0, 0)),
            pl.BlockSpec((1, 2, L, HEAD_PAD), lambda b, p: (b, p, 0, 0)),
            pl.BlockSpec((1, L, 2 * V_HEAD), lambda b, p: (b, 0, p)),
        ],
        out_specs=pl.BlockSpec((1, L, 2 * V_HEAD), lambda b, p: (b, 0, p)),
        out_shape=jax.ShapeDtypeStruct((B, L, ATTN_WIDTH), BF16),
        scratch_shapes=[pltpu.VMEM((L, 2 * V_HEAD + LANES), BF16)],
        compiler_params=_params("parallel", "parallel"),
    )(q, k, v)


def _ssm_kernel(u_ref, bbd_ref, lam_ref, cbd_ref, d_ref, wglu_ref, g_ref, o_ref, state_ref, in_s, out_s, *bu_refs,
                batch, tc):
    @pl.when(pl.program_id(0) == 0)
    def _():
        state_ref[...] = jnp.zeros_like(state_ref)

    steps = tc // SSM_PARTS
    halves = tuple((bu, slice(p * steps * batch, (p + 1) * steps * batch)) for p, bu in enumerate(bu_refs))
    for b in range(batch):
        for hl in range(SSM_WIDTH // LANES):
            lanes = slice(b * SSM_WIDTH + hl * LANES, b * SSM_WIDTH + (hl + 1) * LANES)
            in_s[hl, pl.ds(b, tc, stride=batch), :] = u_ref[:, lanes].astype(F32)

    def u_rows(rows):
        return jnp.concatenate([in_s[hl, rows, :] for hl in range(SSM_WIDTH // LANES)], axis=1)

    for bu_ref, rows in halves:
        bu_ref[...] = jnp.dot(u_rows(rows).astype(BF16), bbd_ref[0], preferred_element_type=F32)
    lam_re = jnp.broadcast_to(lam_ref[0, 0:1, :], (batch, SSM_FLAT))
    lam_im = jnp.broadcast_to(lam_ref[0, 1:2, :], (batch, SSM_FLAT))

    carry = (state_ref[:, 0:SSM_FLAT], state_ref[:, SSM_FLAT:2 * SSM_FLAT])
    for bu_ref, rows in halves:
        def step(t, xs, bu_ref=bu_ref):
            xr, xi = xs
            at = pl.ds(pl.multiple_of(t * batch, batch), batch)
            nr = lam_re * xr - lam_im * xi + bu_ref[at, 0:SSM_FLAT]
            ni = lam_re * xi + lam_im * xr + bu_ref[at, SSM_FLAT:2 * SSM_FLAT]
            bu_ref[at, 0:SSM_FLAT] = nr
            bu_ref[at, SSM_FLAT:2 * SSM_FLAT] = ni
            return nr, ni

        carry = lax.fori_loop(0, steps, step, carry, unroll=True)
        y = jnp.dot(bu_ref[...].astype(BF16), cbd_ref[0], preferred_element_type=F32)
        y = y + d_ref[0] * u_rows(rows)
        z = jax.nn.gelu(y)
        gate = jax.nn.sigmoid(jnp.dot(z.astype(BF16), wglu_ref[0], preferred_element_type=F32))
        res = _rms(z * gate, g_ref[0])
        for hl in range(SSM_WIDTH // LANES):
            out_s[hl, rows, :] = res[:, hl * LANES:(hl + 1) * LANES]
    state_ref[:, 0:SSM_FLAT] = carry[0]
    state_ref[:, SSM_FLAT:2 * SSM_FLAT] = carry[1]
    for b in range(batch):
        for hl in range(SSM_WIDTH // LANES):
            lanes = slice(b * SSM_WIDTH + hl * LANES, b * SSM_WIDTH + (hl + 1) * LANES)
            o_ref[:, lanes] = out_s[hl, pl.ds(b, tc, stride=batch), :].astype(BF16)


def _ssm(u_tm, bbd, lam, cbd, d, wglu, g, batch, layer):
    seq = u_tm.shape[0]
    tc = min(TC_SSM, seq)
    blk = tc * batch
    full = lambda shape: pl.BlockSpec((1,) + shape, lambda t: (layer,) + (0,) * len(shape))
    return pl.pallas_call(
        functools.partial(_ssm_kernel, batch=batch, tc=tc),
        grid=(seq // tc,),
        in_specs=[
            pl.BlockSpec((tc, batch * SSM_WIDTH), lambda t: (t, 0)),
            full((SSM_WIDTH, 2 * SSM_FLAT)), full((2, SSM_FLAT)), full((2 * SSM_FLAT, SSM_WIDTH)),
            full((1, SSM_WIDTH)), full((SSM_WIDTH, SSM_WIDTH)), full((1, SSM_WIDTH)),
        ],
        out_specs=pl.BlockSpec((tc, batch * SSM_WIDTH), lambda t: (t, 0)),
        out_shape=jax.ShapeDtypeStruct((seq, batch * SSM_WIDTH), BF16),
        scratch_shapes=[pltpu.VMEM((batch, 2 * SSM_FLAT), F32), pltpu.VMEM((SSM_WIDTH // LANES, blk, LANES), F32),
                        pltpu.VMEM((SSM_WIDTH // LANES, blk, LANES), F32)]
        + [pltpu.VMEM((blk // SSM_PARTS, 2 * SSM_FLAT), F32)] * SSM_PARTS,
        compiler_params=_params("arbitrary"),
    )(u_tm, bbd, lam, cbd, d, wglu, g)


def _conv_rows(upad_ref, r0, w_ref, b_ref, lnw_ref, lnb_ref, gn_ref):
    base = CONV_HALO - (CONV_K - 1)
    acc = jnp.zeros((CONV_SUB, CONV_WIDTH), F32)
    for off in range(SUBLANES):
        n = CONV_SUB + (SUBLANES if off else 0)
        part = None
        for a8 in range(0, CONV_HALO + 1, SUBLANES):
            kk = a8 + off - base
            if 0 <= kk < CONV_K:
                term = w_ref[0, kk:kk + 1, :] * upad_ref[r0 + a8:r0 + a8 + n, :]
                part = term if part is None else part + term
        acc = acc + part[off:off + CONV_SUB]
    y = acc + b_ref[0]
    mu = jnp.mean(y, axis=-1, keepdims=True)
    var = jnp.mean(jnp.square(y - mu), axis=-1, keepdims=True)
    y = (y - mu) * lax.rsqrt(var + LN_EPS) * lnw_ref[0] + lnb_ref[0]
    y = y * jax.nn.sigmoid(y)
    return _rms(y, gn_ref[0])


ROUTE_E1, ROUTE_E2, ROUTE_R1, ROUTE_R2, ROUTE_G1, ROUTE_G2 = range(6)
ROUTE_ROWS = 8
GRP_LANE0 = N_EXPERTS


def _out_router_kernel(x_ref, ya_ref, ys_ref, ca_ref, cg_ref, ga_ref, wo_ref, fn_ref, wr_ref, br_ref,
                       cw_ref, cb_ref, clnw_ref, clnb_ref, cgn_ref,
                       x1_ref, h2_ref, route_ref, route_t_ref, cnt_ref, sub_ref,
                       carry_ref, tri_ref, upad_ref, yc_ref, *, tm, per_b):
    @pl.when(pl.program_id(0) == 0)
    def _():
        carry_ref[...] = jnp.zeros_like(carry_ref)
        rr = lax.broadcasted_iota(jnp.int32, (tm, tm), 0)
        cc = lax.broadcasted_iota(jnp.int32, (tm, tm), 1)
        tri_ref[...] = (cc < rr).astype(BF16)

    first = pl.program_id(0) % per_b == 0
    upad_ref[0:CONV_HALO, :] = jnp.where(first, 0.0, upad_ref[tm:tm + CONV_HALO, :])
    upad_ref[CONV_HALO:, :] = ca_ref[...].astype(F32) * jax.nn.sigmoid(cg_ref[...].astype(F32))
    ya = _rms(ya_ref[...].astype(F32), ga_ref[0]).astype(BF16)
    acc = jnp.dot(ya, wo_ref[0, 0:ATTN_WIDTH, :], preferred_element_type=F32)
    acc += jnp.dot(ys_ref[...], wo_ref[0, ATTN_WIDTH:ATTN_WIDTH + SSM_WIDTH, :], preferred_element_type=F32)
    for r0 in range(0, tm, CONV_SUB):
        yc_ref[r0:r0 + CONV_SUB, :] = _conv_rows(upad_ref, r0, cw_ref, cb_ref, clnw_ref, clnb_ref, cgn_ref).astype(BF16)
    acc += jnp.dot(yc_ref[...], wo_ref[0, ATTN_WIDTH + SSM_WIDTH:, :], preferred_element_type=F32)
    x1 = x_ref[...] + acc
    x1_ref[...] = x1
    h2 = _rms(x1, fn_ref[0])
    h2_ref[...] = _pack_rows(h2)

    h_hi = h2.astype(BF16)
    h_lo = (h2 - h_hi.astype(F32)).astype(BF16)
    part = jnp.dot(h_hi, wr_ref[0], preferred_element_type=F32)
    logits = (part[:, 0:LANES] + part[:, LANES:2 * LANES]
              + jnp.dot(h_lo, wr_ref[0, :, 0:LANES], preferred_element_type=F32) + br_ref[0])
    lane = lax.broadcasted_iota(jnp.int32, (tm, LANES), 1)
    ninf = -jnp.inf
    big = LANES

    def first_argmax(vals, vmax):
        return jnp.min(jnp.where(vals == vmax, lane, big), axis=-1, keepdims=True)

    grp = jnp.where((lane >= GRP_LANE0) & (lane < GRP_LANE0 + N_EGROUPS), logits, ninf)
    gmax = jnp.max(grp, axis=-1, keepdims=True)
    gsel = first_argmax(grp, gmax) - GRP_LANE0
    p_grp = 1.0 / jnp.sum(jnp.exp(grp - gmax), axis=-1, keepdims=True)

    el = jnp.where((lane < N_EXPERTS) & ((lane // EXP_PER_GROUP) == gsel), logits, ninf)
    m1 = jnp.max(el, axis=-1, keepdims=True)
    e1 = first_argmax(el, m1)
    el2 = jnp.where(lane == e1, ninf, el)
    m2 = jnp.max(el2, axis=-1, keepdims=True)
    e2 = first_argmax(el2, m2)
    t2 = jnp.exp(m2 - m1)
    g1 = p_grp / (1.0 + t2)
    g2 = p_grp * t2 / (1.0 + t2)

    hit1 = lane == e1
    hit2 = lane == e2
    cnt = (hit1 | hit2).astype(F32)
    before = jnp.dot(tri_ref[...], cnt.astype(BF16), preferred_element_type=F32) + carry_ref[...]
    r1 = jnp.sum(jnp.where(hit1, before, 0.0), axis=-1, keepdims=True)
    r2 = jnp.sum(jnp.where(hit2, before, 0.0), axis=-1, keepdims=True)
    subs = [before[h * TM_MOVE:h * TM_MOVE + 1, :] for h in range(tm // TM_MOVE)]
    sub_ref[...] = jnp.concatenate(subs + [jnp.zeros((SUBLANES - len(subs), LANES), F32)], axis=0)
    carry_ref[...] += jnp.sum(cnt, axis=0, keepdims=True)
    cnt_ref[...] = carry_ref[...]

    rec = jnp.where(lane == ROUTE_E1, e1.astype(F32), 0.0)
    rec = jnp.where(lane == ROUTE_E2, e2.astype(F32), rec)
    rec = jnp.where(lane == ROUTE_R1, r1, rec)
    rec = jnp.where(lane == ROUTE_R2, r2, rec)
    rec = jnp.where(lane == ROUTE_G1, g1, rec)
    rec = jnp.where(lane == ROUTE_G2, g2, rec)
    route_ref[...] = rec
    route_t_ref[...] = rec.T[0:ROUTE_ROWS, :]


def _out_router(x, ya, ys_tm, ca, cg, ga, wo, fn, wr, br, conv_params, batch, layer):
    N, D = x.shape
    seq = N // batch
    tm = min(TM_OUT, seq)
    per_b = seq // tm
    full = lambda shape: pl.BlockSpec((1,) + shape, lambda i: (layer,) + (0,) * len(shape))
    whole = lambda shape: pl.BlockSpec(shape, lambda i: (0,) * len(shape))
    tile = lambda w: pl.BlockSpec((tm, w), lambda i: (i, 0))
    return pl.pallas_call(
        functools.partial(_out_router_kernel, tm=tm, per_b=per_b),
        grid=(N // tm,),
        in_specs=[
            tile(D), tile(ATTN_WIDTH),
            pl.BlockSpec((tm, SSM_WIDTH), lambda i: (i % per_b, i // per_b)),
            tile(CONV_WIDTH), tile(CONV_WIDTH),
            full((1, ATTN_WIDTH)), full((D, D)), full((1, D)), full((D, 2 * LANES)), full((1, LANES)),
            full((CONV_K, CONV_WIDTH)), full((1, CONV_WIDTH)), full((1, CONV_WIDTH)), full((1, CONV_WIDTH)),
            full((1, CONV_WIDTH)),
        ],
        out_specs=[tile(D), tile(D_PACK), tile(LANES), pl.BlockSpec((ROUTE_ROWS, tm), lambda i: (0, i)),
                   whole((1, LANES)), pl.BlockSpec((SUBLANES, LANES), lambda i: (i, 0))],
        out_shape=[
            jax.ShapeDtypeStruct((N, D), F32),
            jax.ShapeDtypeStruct((N, D_PACK), U32),
            jax.ShapeDtypeStruct((N, LANES), F32),
            jax.ShapeDtypeStruct((ROUTE_ROWS, N), F32),
            jax.ShapeDtypeStruct((1, LANES), F32),
            jax.ShapeDtypeStruct((N // tm * SUBLANES, LANES), F32),
        ],
        scratch_shapes=[pltpu.VMEM((1, LANES), F32), pltpu.VMEM((tm, tm), BF16),
                        pltpu.VMEM((tm + CONV_HALO, CONV_WIDTH), F32), pltpu.VMEM((tm, CONV_WIDTH), BF16)],
        compiler_params=_params("arbitrary"),
    )(x, ya, ys_tm, ca, cg, ga, wo, fn, wr, br, *conv_params)


def _dispatch(dest_flat, h2, n_rows):
    N, D = h2.shape
    workers = SC_CORES * SC_SUBCORES
    per_worker = N // workers
    chunks = per_worker // SC_CHUNK
    idx = dest_flat.reshape(2, workers, chunks, SC_CHUNK).transpose(1, 0, 2, 3).reshape(workers, 2 * chunks, SC_CHUNK)
    mesh = plsc.VectorSubcoreMesh(core_axis_name="c", subcore_axis_name="s")

    @functools.partial(
        pl.kernel, mesh=mesh,
        out_type=jax.ShapeDtypeStruct((n_rows, D), h2.dtype),
        scratch_types=[pltpu.VMEM((2 * chunks, SC_CHUNK), jnp.int32), pltpu.VMEM((2, SC_CHUNK, D), h2.dtype),
                       pltpu.SemaphoreType.DMA((2,)), pltpu.SemaphoreType.DMA((2, 2))],
    )
    def scatter(h_hbm, idx_hbm, out_hbm, idx_v, rows_v, sem_in, sem_out):
        wid = lax.axis_index("s") * SC_CORES + lax.axis_index("c")
        pltpu.sync_copy(idx_hbm.at[wid], idx_v)

        def load(c):
            first = pl.multiple_of(wid * per_worker + c * SC_CHUNK, SC_CHUNK)
            return pltpu.async_copy(h_hbm.at[pl.ds(first, SC_CHUNK)], rows_v.at[c % 2], sem_in.at[c % 2])

        def store(c, k):
            return pltpu.async_copy(rows_v.at[c % 2], out_hbm.at[idx_v.at[k * chunks + c]], sem_out.at[c % 2, k])

        loads = {0: load(0)}
        stores = {}
        for c in range(chunks):
            loads[c].wait()
            if c >= 1:
                for st in stores[c - 1]:
                    st.wait()
            if c + 1 < chunks:
                loads[c + 1] = load(c + 1)
            stores[c] = (store(c, 0), store(c, 1))
        for st in stores[chunks - 1]:
            st.wait()

    return scatter(h2, idx)


def _experts_kernel(blk_ref, exp_ref, valid_ref, slot_ref, next_ref, used_ref, x_ref, w1_hbm, w3_hbm, w2_hbm, y_ref,
                    w1_f, w3_f, w2_f, w13_s, w2_s, sem, *, layer):
    j = pl.program_id(0)

    def weight_copies(e, slot):
        return (pltpu.make_async_copy(w1_hbm.at[layer, e], w1_f.at[slot], sem.at[slot, 0]),
                pltpu.make_async_copy(w3_hbm.at[layer, e], w3_f.at[slot], sem.at[slot, 1]),
                pltpu.make_async_copy(w2_hbm.at[layer, e], w2_f.at[slot], sem.at[slot, 2]))

    @pl.when(j < used_ref[0])
    def _():
        e = exp_ref[j]
        slot = slot_ref[j]

        @pl.when(j == 0)
        def _():
            for cp in weight_copies(e, slot):
                cp.start()

        @pl.when((j == 0) | (e != exp_ref[jnp.maximum(j - 1, 0)]))
        def _():
            for cp in weight_copies(e, slot):
                cp.wait()
            for c in range(D_FF_E // FF_CHUNK):
                w13_s[:, 2 * FF_CHUNK * c:2 * FF_CHUNK * c + FF_CHUNK] = \
                    w1_f[slot, :, FF_CHUNK * c:FF_CHUNK * (c + 1)].astype(BF16)
                w13_s[:, 2 * FF_CHUNK * c + FF_CHUNK:2 * FF_CHUNK * (c + 1)] = \
                    w3_f[slot, :, FF_CHUNK * c:FF_CHUNK * (c + 1)].astype(BF16)
            w2_s[...] = w2_f[slot].astype(BF16)

            @pl.when(next_ref[j] >= 0)
            def _():
                for cp in weight_copies(next_ref[j], 1 - slot):
                    cp.start()

        valid = valid_ref[j]

        def ffn(rows):
            x = _unpack_rows(x_ref[0:rows, :])
            y = None
            for c in range(D_FF_E // FF_CHUNK):
                ab = jnp.dot(x, w13_s[:, 2 * FF_CHUNK * c:2 * FF_CHUNK * (c + 1)], preferred_element_type=F32)
                a = ab[:, 0:FF_CHUNK]
                hmid = (a * jax.nn.sigmoid(a) * ab[:, FF_CHUNK:]).astype(BF16)
                part = jnp.dot(hmid, w2_s[FF_CHUNK * c:FF_CHUNK * (c + 1), :], preferred_element_type=F32)
                y = part if y is None else y + part
            row = lax.broadcasted_iota(jnp.int32, (rows, 1), 0)
            y_ref[0:rows, :] = jnp.where(row < valid, _pack_rows(y), jnp.uint32(0))
            if rows < TB_EXP:
                y_ref[rows:, :] = jnp.zeros((TB_EXP - rows, D_PACK), U32)

        sizes = [TB_EXP // d for d in EXPERT_TAIL_DIVISORS]
        for n, rows in enumerate(sizes):
            below = sizes[n + 1] if n + 1 < len(sizes) else 0

            @pl.when((valid > below) & (valid <= rows) if n else valid > below)
            def _(rows=rows):
                ffn(rows)


def _experts(blk_map, blk_exp, blk_valid, blk_slot, blk_next, n_used, x_rows, w1, w3, w2, layer):
    R, D = x_rows.shape[0], D_MODEL
    nb = R // TB_EXP
    row_block = lambda j, bm, be, bv, bs, bn, nu: (bm[j], 0)
    return pl.pallas_call(
        functools.partial(_experts_kernel, layer=layer),
        grid_spec=pltpu.PrefetchScalarGridSpec(
            num_scalar_prefetch=6,
            grid=(nb,),
            in_specs=[
                pl.BlockSpec((TB_EXP, D_PACK), row_block),
                pl.BlockSpec(memory_space=pl.ANY), pl.BlockSpec(memory_space=pl.ANY), pl.BlockSpec(memory_space=pl.ANY),
            ],
            out_specs=pl.BlockSpec((TB_EXP, D_PACK), row_block),
            scratch_shapes=[
                pltpu.VMEM((2, D, D_FF_E), F32), pltpu.VMEM((2, D, D_FF_E), F32), pltpu.VMEM((2, D_FF_E, D), F32),
                pltpu.VMEM((D, 2 * D_FF_E), BF16), pltpu.VMEM((D_FF_E, D), BF16),
                pltpu.SemaphoreType.DMA((2, 3)),
            ],
        ),
        out_shape=jax.ShapeDtypeStruct((R, D_PACK), U32),
        compiler_params=_params("arbitrary"),
    )(blk_map, blk_exp, blk_valid, blk_slot, blk_next, n_used, x_rows, w1, w3, w2)


def _combine_kernel(src_ref, ngroup_ref, x1_ref, route_ref, off_ref, rows_ref, o_ref, buf, sem, *, tm):
    i = pl.program_id(0)
    cur = i % 2

    def slot_copy(tile, half, s):
        src = pl.multiple_of(src_ref[tile * RUN_SLOTS + s], RUN_ROWS)
        dst = buf.at[half, pl.ds(pl.multiple_of(s * RUN_ROWS, RUN_ROWS), RUN_ROWS)]
        return pltpu.make_async_copy(rows_ref.at[pl.ds(src, RUN_ROWS)], dst, sem.at[half])

    def fetch(tile, half):
        def issue(g, c):
            for u in range(RUN_GROUP):
                slot_copy(tile, half, g * RUN_GROUP + u).start()
            return c
        lax.fori_loop(0, ngroup_ref[tile], issue, 0)

    def wait(g, c):
        rows = RUN_GROUP * RUN_ROWS
        pltpu.make_async_copy(rows_ref.at[pl.ds(0, rows)], buf.at[cur, pl.ds(0, rows)], sem.at[cur]).wait()
        return c

    @pl.when(i == 0)
    def _():
        buf[...] = jnp.zeros_like(buf)
        fetch(0, 0)

    @pl.when(i + 1 < pl.num_programs(0))
    def _():
        fetch(i + 1, 1 - cur)

    route = route_ref[...]
    lane = lax.broadcasted_iota(jnp.int32, (tm, LANES), 1).astype(F32)
    off = off_ref[0]

    def buf_row(e_lane, r_lane):
        e = route[:, e_lane:e_lane + 1]
        base = jnp.sum(jnp.where(lane == e, off, 0.0), axis=-1, keepdims=True)
        return (base + route[:, r_lane:r_lane + 1]).astype(jnp.int32)

    col = lax.broadcasted_iota(jnp.int32, (tm, RUN_SLOTS * RUN_ROWS), 1)
    pick = (jnp.where(col == buf_row(ROUTE_E1, ROUTE_R1), route[:, ROUTE_G1:ROUTE_G1 + 1], 0.0)
            + jnp.where(col == buf_row(ROUTE_E2, ROUTE_R2), route[:, ROUTE_G2:ROUTE_G2 + 1], 0.0))
    lax.fori_loop(0, ngroup_ref[i], wait, 0)
    o_ref[...] = x1_ref[...] + jnp.dot(pick.astype(BF16), _unpack_rows(buf[cur]), preferred_element_type=F32)


def _combine(src, n_groups, x1, route, off, y_rows):
    N, D = x1.shape
    tm = min(TM_MOVE, N)
    return pl.pallas_call(
        functools.partial(_combine_kernel, tm=tm),
        grid_spec=pltpu.PrefetchScalarGridSpec(
            num_scalar_prefetch=2,
            grid=(N // tm,),
            in_specs=[
                pl.BlockSpec((tm, D), lambda i, s, n: (i, 0)),
                pl.BlockSpec((tm, LANES), lambda i, s, n: (i, 0)),
                pl.BlockSpec((1, 1, LANES), lambda i, s, n: (i, 0, 0)),
                pl.BlockSpec(memory_space=pl.ANY),
            ],
            out_specs=pl.BlockSpec((tm, D), lambda i, s, n: (i, 0)),
            scratch_shapes=[pltpu.VMEM((2, RUN_SLOTS * RUN_ROWS, D_PACK), U32), pltpu.SemaphoreType.DMA((2,))],
        ),
        out_shape=jax.ShapeDtypeStruct((N, D), F32),
        compiler_params=_params("arbitrary"),
    )(src, n_groups, x1, route, off, y_rows)


def _moe(x1, h2, route, route_t, counts, sub_carry, w1, w3, w2, layer):
    N, _ = x1.shape
    tm = min(TM_MOVE, N)
    n_tiles = N // tm
    experts = jnp.arange(N_EXPERTS, dtype=jnp.int32)
    nb = (2 * N + N_EXPERTS * (TB_EXP - 1)) // TB_EXP + 1
    e_id = route_t[ROUTE_E1:ROUTE_E2 + 1].astype(jnp.int32)
    rank = route_t[ROUTE_R1:ROUTE_R2 + 1].astype(jnp.int32)
    cnt = counts[0, :N_EXPERTS].astype(jnp.int32)
    padded = (cnt + TB_EXP - 1) // TB_EXP * TB_EXP
    pad_end = jnp.cumsum(padded)
    pad_start = pad_end - padded
    e_flat = e_id.reshape(1, 2 * N)
    dest = jnp.sum(jnp.where(e_flat == experts[:, None], pad_start[:, None], 0), axis=0) + rank.reshape(2 * N)
    x_rows = _dispatch(dest, h2, nb * TB_EXP)
    n_used = pad_end[-1] // TB_EXP
    blk_map = jnp.minimum(jnp.arange(nb, dtype=jnp.int32), jnp.maximum(n_used - 1, 0))
    blk_exp = jnp.sum(pad_end[None, :] <= (blk_map * TB_EXP)[:, None], axis=1).astype(jnp.int32)
    blk_exp = jnp.minimum(blk_exp, N_EXPERTS - 1)
    is_blk_e = blk_exp[:, None] == experts[None, :]
    blk_valid = jnp.clip(jnp.sum(jnp.where(is_blk_e, (pad_start + cnt)[None, :], 0), axis=1) - blk_map * TB_EXP,
                         0, TB_EXP).astype(jnp.int32)
    has_rows = cnt > 0
    slot_of_e = (jnp.cumsum(has_rows.astype(jnp.int32)) - 1) % 2
    later = jnp.where(has_rows, experts, N_EXPERTS)
    next_of_e = lax.cummin(jnp.concatenate([later[1:], jnp.full((1,), N_EXPERTS, jnp.int32)]), reverse=True)
    next_of_e = jnp.where(next_of_e < N_EXPERTS, next_of_e, -1)
    blk_slot = jnp.sum(jnp.where(is_blk_e, slot_of_e[None, :], 0), axis=1).astype(jnp.int32)
    blk_next = jnp.sum(jnp.where(is_blk_e, next_of_e[None, :], 0), axis=1).astype(jnp.int32)

    per_router_tile = TM_OUT // tm if N >= TM_OUT else 1
    carry = sub_carry.reshape(-1, SUBLANES, LANES)[:, :per_router_tile, :N_EXPERTS].reshape(n_tiles, N_EXPERTS)
    carry = carry.astype(jnp.int32)
    tile_cnt = jnp.concatenate([carry[1:], cnt[None, :]], axis=0) - carry
    run_start = pad_start[None, :] + carry
    first_blk = run_start // RUN_ROWS
    nslot = jnp.where(tile_cnt > 0, (run_start + tile_cnt - 1) // RUN_ROWS - first_blk + 1, 0)
    slot_end = jnp.cumsum(nslot, axis=1)
    slot_base = slot_end - nslot
    slots = jnp.arange(RUN_SLOTS, dtype=jnp.int32)
    slot_e = jnp.minimum(jnp.sum(slot_end[:, None, :] <= slots[None, :, None], axis=2), N_EXPERTS - 1)
    is_slot_e = slot_e[:, :, None] == experts[None, None, :]
    pick = lambda tbl: jnp.sum(jnp.where(is_slot_e, tbl[:, None, :], 0), axis=2)
    src = (pick(first_blk) + slots[None, :] - pick(slot_base)) * RUN_ROWS
    src = jnp.where(slots[None, :] < slot_end[:, -1:], src, 0)
    src = jnp.clip(src, 0, nb * TB_EXP - RUN_ROWS).reshape(-1).astype(jnp.int32)
    n_groups = (slot_end[:, -1] + RUN_GROUP - 1) // RUN_GROUP
    off = (slot_base - first_blk) * RUN_ROWS + pad_start[None, :]
    off = jnp.pad(off.astype(F32), ((0, 0), (0, LANES - N_EXPERTS)))

    y_rows = _experts(blk_map, blk_exp, blk_valid, blk_slot, blk_next, n_used.reshape(1).astype(jnp.int32),
                      x_rows, w1, w3, w2, layer)
    return _combine(src, n_groups.astype(jnp.int32), x1, route, off.reshape(n_tiles, 1, LANES), y_rows)


def _pad_heads(w, width):
    lead = w.shape[:-1]
    w = w.reshape(lead + (N_HEADS, width))
    pad = [(0, 0)] * (w.ndim - 1) + [(0, HEAD_PAD - width)]
    return jnp.pad(w, pad).reshape(lead + (N_HEADS * HEAD_PAD,))


def _swap_rope(w):
    half = QK_ROPE // 2
    lo, hi = w[..., QK_NOPE:QK_NOPE + half], w[..., QK_NOPE + half:QK_HEAD]
    pad = [(0, 0)] * (w.ndim - 1)
    return jnp.pad(jnp.concatenate([hi, lo], -1), pad + [(QK_NOPE, HEAD_PAD - QK_HEAD)])


def _rope_tables(positions):
    inv_freq = ROPE_THETA ** (-jnp.arange(0, QK_ROPE, 2, dtype=F32) / QK_ROPE)
    ang = positions.astype(F32)[..., None] * inv_freq
    table = jnp.concatenate([jnp.cos(ang), jnp.sin(ang)], -1)
    return jnp.pad(table, ((0, 0), (0, 0), (QK_NOPE, LANES - QK_HEAD)))


def _ssm_params(lam_re, lam_im, b_re, b_im, c_re, c_im, log_dt):
    depth = lam_re.shape[0]
    lam = lax.complex(lam_re, lam_im)
    dt = jnp.exp(log_dt)[..., None]
    lam_bar = jnp.exp(lam * dt)
    b_bar = ((lam_bar - 1.0) / lam)[..., None] * lax.complex(b_re, b_im)
    eye = jnp.eye(SSM_GROUPS, dtype=F32)

    def in_blockdiag(m):
        return jnp.einsum("lgpc,gh->lgchp", m, eye).reshape(depth, SSM_WIDTH, SSM_FLAT)

    def out_blockdiag(m):
        return jnp.einsum("lgcp,gh->lgphc", m, eye).reshape(depth, SSM_FLAT, SSM_WIDTH)

    bbd = jnp.concatenate([in_blockdiag(jnp.real(b_bar)), in_blockdiag(jnp.imag(b_bar))], axis=2)
    cbd = jnp.concatenate([out_blockdiag(c_re), out_blockdiag(-c_im)], axis=1)
    lam_rows = jnp.stack([jnp.real(lam_bar).reshape(depth, SSM_FLAT), jnp.imag(lam_bar).reshape(depth, SSM_FLAT)],
                         axis=1)
    return bbd.astype(BF16), lam_rows, cbd.astype(BF16)


def kernel(x, positions, mix_norm, w_in, q_a_norm, w_uq, kv_a_norm, w_ukv, q_norm, k_norm, ssm_lam_re, ssm_lam_im, ssm_b_re, ssm_b_im, ssm_c_re, ssm_c_im, ssm_d, ssm_log_dt, ssm_w_glu, conv_dw_w, conv_dw_b, conv_ln_w, conv_ln_b, out_norm, w_out, ffn_norm, w_grp, b_grp, w_exp, b_exp, w1, w3, w2):
    B, L, D = x.shape
    depth = w_in.shape[0]
    rope = _rope_tables(positions)
    rows3 = lambda v: v[:, None, :]
    lane_tail = lambda v: jnp.pad(v, [(0, 0)] * (v.ndim - 1) + [(0, LANES - v.shape[-1])])
    c_q, c_kv, k_pe, u_s, c_a, c_g = jnp.split(
        w_in, [Q_LORA, Q_LORA + KV_LORA, Q_LORA + KV_LORA + QK_ROPE, Q_LORA + KV_LORA + QK_ROPE + SSM_WIDTH,
               Q_LORA + KV_LORA + QK_ROPE + SSM_WIDTH + CONV_WIDTH], axis=2)
    k_pe_full = jnp.pad(k_pe, ((0, 0), (0, 0), (QK_NOPE, 0)))
    win = jnp.concatenate([c_q, c_kv, lane_tail(k_pe_full), _swap_rope(k_pe_full), u_s, c_a, c_g], axis=2).astype(BF16)
    wkv = w_ukv.reshape(depth, KV_LORA, N_HEADS, QK_NOPE + V_HEAD)
    wuk = _pad_heads(wkv[..., :QK_NOPE].reshape(depth, KV_LORA, N_HEADS * QK_NOPE), QK_NOPE).astype(BF16)
    wuv = wkv[..., QK_NOPE:].reshape(depth, KV_LORA, ATTN_WIDTH).astype(BF16)
    wuq_sw = _swap_rope(w_uq.reshape(depth, Q_LORA, N_HEADS, QK_HEAD)).reshape(depth, Q_LORA, N_HEADS * HEAD_PAD)
    wuq = jnp.concatenate([_pad_heads(w_uq, QK_HEAD), wuq_sw], axis=2).astype(BF16)
    norm_rows = lambda w: jnp.stack([lane_tail(w), _swap_rope(w)], axis=1)
    qn, kn = norm_rows(q_norm * (QK_HEAD ** -0.5)), norm_rows(k_norm)
    bbd, lam_rows, cbd = _ssm_params(ssm_lam_re, ssm_lam_im, ssm_b_re, ssm_b_im, ssm_c_re, ssm_c_im, ssm_log_dt)
    ssm_skip = rows3(ssm_d.reshape(depth, SSM_WIDTH))
    wglu = ssm_w_glu.astype(BF16)
    g_attn = rows3(out_norm[:, :ATTN_WIDTH])
    g_ssm = rows3(out_norm[:, ATTN_WIDTH:ATTN_WIDTH + SSM_WIDTH])
    conv_params = (conv_dw_w, rows3(conv_dw_b), rows3(conv_ln_w), rows3(conv_ln_b),
                   rows3(out_norm[:, ATTN_WIDTH + SSM_WIDTH:]))
    w_route = lane_tail(jnp.concatenate([w_exp, w_grp], axis=2))
    w_route_hi = w_route.astype(BF16)
    w_route = jnp.concatenate([w_route_hi, (w_route - w_route_hi.astype(F32)).astype(BF16)], axis=2)
    b_route = rows3(lane_tail(jnp.concatenate([b_exp, b_grp], axis=1)))
    wo = w_out.astype(BF16)
    for l in range(depth):
        q, k, v, u_tm, ca, cg = _in_proj(x, rope, rows3(mix_norm), win, rows3(q_a_norm), wuq, rows3(kv_a_norm), wuk, wuv,
                                         qn, kn, l)
        y_attn = _attention(q, k, v)
        y_ssm_tm = _ssm(u_tm, bbd, lam_rows, cbd, ssm_skip, wglu, g_ssm, B, l)
        x1, h2, route, route_t, counts, sub_carry = _out_router(
            x.reshape(B * L, D), y_attn.reshape(B * L, ATTN_WIDTH), y_ssm_tm,
            ca.reshape(B * L, CONV_WIDTH), cg.reshape(B * L, CONV_WIDTH), g_attn, wo, rows3(ffn_norm),
            w_route, b_route, conv_params, B, l)
        x = _moe(x1, h2, route, route_t, counts, sub_carry, w1, w3, w2, l).reshape(B, L, D)
    return x
```

```python
import functools

import jax
import jax.numpy as jnp
from jax import lax
from jax.experimental import pallas as pl
from jax.experimental.pallas import tpu as pltpu
from jax.experimental.pallas import tpu_sc as plsc

D_MODEL = 1024
CHUNK = 64
EPS = 1e-6
LN_EPS = 1e-5
N_HEADS = 8
QK_NOPE = 64
QK_ROPE = 32
QK_HEAD = QK_NOPE + QK_ROPE
V_HEAD = 64
Q_LORA = 256
KV_LORA = 128
ROPE_THETA = 10000.0
ATTN_WIDTH = N_HEADS * V_HEAD
SSM_WIDTH = 256
SSM_GROUP = 16
SSM_GROUPS = SSM_WIDTH // SSM_GROUP
SSM_STATE = 64
SSM_FLAT = SSM_GROUPS * SSM_STATE
CONV_WIDTH = 256
CONV_K = 31
N_EGROUPS = 4
EXP_PER_GROUP = 8
N_EXPERTS = N_EGROUPS * EXP_PER_GROUP
D_FF_E = 512

LANES = 128
SUBLANES = 8
HEAD_PAD = LANES
IN_PROJ_PAD = Q_LORA + KV_LORA + SSM_WIDTH + 2 * CONV_WIDTH + 2 * LANES
CONV_HALO = 32
VMEM_LIMIT = 48 * 1024 * 1024

TM_PROJ = 512
TQ_ATTN = 256
TC_SSM = 128
SSM_PARTS = 2
CONV_SUB = 128
TM_OUT = 512
TM_MOVE = 256
RUN_ROWS = SUBLANES
RUN_SLOTS = 2 * N_EXPERTS + 2 * TM_MOVE // RUN_ROWS
RUN_GROUP = 8
SC_CORES, SC_SUBCORES = 2, 16
SC_CHUNK = 64
TB_EXP = 1024
EXPERT_TAIL_DIVISORS = (1, 2, 4, 8)
FF_CHUNK = 256

BF16 = jnp.bfloat16
F32 = jnp.float32
U32 = jnp.uint32
D_PACK = D_MODEL // 2


def _pack_rows(v):
    bits = lax.bitcast_convert_type(v.astype(BF16).astype(F32), U32)
    half = v.shape[1] // 2
    return bits[:, 0:half] | (bits[:, half:] >> 16)


def _unpack_rows(w):
    hi = lax.bitcast_convert_type(w & jnp.uint32(0xFFFF0000), F32)
    lo = lax.bitcast_convert_type(w << 16, F32)
    return jnp.concatenate([hi, lo], axis=1).astype(BF16)


def _rms(x, w):
    return x * lax.rsqrt(jnp.mean(x * x, axis=-1, keepdims=True) + EPS) * w


def _params(*sem):
    return pltpu.CompilerParams(dimension_semantics=sem, vmem_limit_bytes=VMEM_LIMIT)


def _in_proj_kernel(x_ref, rope_ref, mixn_ref, win_ref, qan_ref, wuq_ref, kvan_ref, wuk_ref, wuv_ref,
                    qn_ref, kn_ref, q_ref, k_ref, v_ref, u_ref, ca_ref, cg_ref):
    x = x_ref[0]
    h = _rms(x, mixn_ref[0]).astype(BF16)
    head_cols = Q_LORA + KV_LORA + 2 * LANES
    proj = jnp.dot(h, win_ref[0, :, 0:head_cols], preferred_element_type=F32)
    o = 0
    c_q = proj[:, o:o + Q_LORA]; o += Q_LORA
    c_kv = proj[:, o:o + KV_LORA]; o += KV_LORA
    k_pe = proj[:, o:o + LANES]; o += LANES
    k_pe_sw = proj[:, o:o + LANES]

    width = N_HEADS * HEAD_PAD
    q2 = jnp.dot(_rms(c_q, qan_ref[0]).astype(BF16), wuq_ref[0], preferred_element_type=F32)
    q, q_sw = q2[:, 0:width], q2[:, width:2 * width]
    ckv_n = _rms(c_kv, kvan_ref[0]).astype(BF16)
    kn = jnp.dot(ckv_n, wuk_ref[0], preferred_element_type=F32)
    v_ref[0] = jnp.dot(ckv_n, wuv_ref[0], preferred_element_type=F32).astype(BF16)

    rope = rope_ref[0]
    half = QK_ROPE // 2
    lane = lax.broadcasted_iota(jnp.int32, rope.shape, 1)
    lo = (lane >= QK_NOPE) & (lane < QK_NOPE + half)
    hi = (lane >= QK_NOPE + half) & (lane < QK_HEAD)
    cos_t = jnp.where(lane < QK_NOPE, 1.0, jnp.where(lo, rope, jnp.where(hi, pltpu.roll(rope, half, 1), 0.0)))
    sin_t = jnp.where(lo, -pltpu.roll(rope, LANES - half, 1), jnp.where(hi, rope, 0.0))
    a_q, b_q = qn_ref[0, 0:1, :] * cos_t, qn_ref[0, 1:2, :] * sin_t
    a_k = kn_ref[0, 0:1, :] * cos_t
    k_sw_term = k_pe_sw * (kn_ref[0, 1:2, :] * sin_t)

    def inv_rms(y):
        return lax.rsqrt(jnp.sum(y * y, axis=-1, keepdims=True) * (1.0 / QK_HEAD) + EPS)

    def rest(j):
        cols = slice(head_cols + j * SSM_WIDTH, head_cols + (j + 1) * SSM_WIDTH)
        return jnp.dot(h, win_ref[0, :, cols], preferred_element_type=F32).astype(BF16)

    for hd in range(N_HEADS):
        if hd == 0:
            u_ref[...] = rest(0)
        elif hd == 3:
            ca_ref[0] = rest(1)
        elif hd == 6:
            cg_ref[0] = rest(2)
        sl = slice(hd * HEAD_PAD, (hd + 1) * HEAD_PAD)
        yq = q[:, sl]
        q_ref[0, hd] = ((yq * a_q + q_sw[:, sl] * b_q) * inv_rms(yq)).astype(BF16)
        yk = kn[:, sl] + k_pe
        k_ref[0, hd] = ((yk * a_k + k_sw_term) * inv_rms(yk)).astype(BF16)


def _in_proj(x, rope, mixn, win, qan, wuq, kvan, wuk, wuv, qn, kn, layer):
    B, L, D = x.shape
    tm = min(TM_PROJ, L)
    full = lambda shape: pl.BlockSpec((1,) + shape, lambda b, t: (layer,) + (0,) * len(shape))
    return pl.pallas_call(
        _in_proj_kernel,
        grid=(B, L // tm),
        in_specs=[
            pl.BlockSpec((1, tm, D), lambda b, t: (b, t, 0)),
            pl.BlockSpec((1, tm, LANES), lambda b, t: (b, t, 0)),
            full((1, D)), full((D, IN_PROJ_PAD)), full((1, Q_LORA)), full((Q_LORA, 2 * N_HEADS * HEAD_PAD)),
            full((1, KV_LORA)), full((KV_LORA, N_HEADS * HEAD_PAD)), full((KV_LORA, ATTN_WIDTH)),
            full((2, HEAD_PAD)), full((2, HEAD_PAD)),
        ],
        out_specs=[
            pl.BlockSpec((1, N_HEADS, tm, HEAD_PAD), lambda b, t: (b, 0, t, 0)),
            pl.BlockSpec((1, N_HEADS, tm, HEAD_PAD), lambda b, t: (b, 0, t, 0)),
            pl.BlockSpec((1, tm, ATTN_WIDTH), lambda b, t: (b, t, 0)),
            pl.BlockSpec((tm, SSM_WIDTH), lambda b, t: (t, b)),
            pl.BlockSpec((1, tm, CONV_WIDTH), lambda b, t: (b, t, 0)),
            pl.BlockSpec((1, tm, CONV_WIDTH), lambda b, t: (b, t, 0)),
        ],
        out_shape=[
            jax.ShapeDtypeStruct((B, N_HEADS, L, HEAD_PAD), BF16),
            jax.ShapeDtypeStruct((B, N_HEADS, L, HEAD_PAD), BF16),
            jax.ShapeDtypeStruct((B, L, ATTN_WIDTH), BF16),
            jax.ShapeDtypeStruct((L, B * SSM_WIDTH), BF16),
            jax.ShapeDtypeStruct((B, L, CONV_WIDTH), BF16),
            jax.ShapeDtypeStruct((B, L, CONV_WIDTH), BF16),
        ],
        compiler_params=_params("parallel", "parallel"),
    )(x, rope, mixn, win, qan, wuq, kvan, wuk, wuv, qn, kn)


def _attention_kernel(q_ref, k_ref, v_ref, o_ref, vext_ref, *, seq, tq):
    pair = 2 * V_HEAD
    vext_ref[:, 0:pair] = v_ref[0]
    vext_ref[:, pair:] = jnp.ones((seq, LANES), BF16)
    row_chunk = lax.broadcasted_iota(jnp.int32, (tq, tq), 0) // CHUNK
    col_chunk = lax.broadcasted_iota(jnp.int32, (tq, tq), 1) // CHUNK
    visible = col_chunk <= row_chunk
    neg = jnp.finfo(F32).min
    nt = (((1,), (1,)), ((), ()))
    low_lanes = lax.broadcasted_iota(jnp.int32, (tq, pair), 1) < V_HEAD

    def scores(i, hh):
        q0 = i * tq
        qb = q_ref[0, hh, q0:q0 + tq, :]
        s_d = lax.dot_general(qb, k_ref[0, hh, q0:q0 + tq, :], nt, preferred_element_type=F32)
        s_d = jnp.where(visible, s_d, neg)
        s_l = lax.dot_general(qb, k_ref[0, hh, 0:q0, :], nt, preferred_element_type=F32) if i > 0 else None
        return s_d, s_l

    def weighted_values(i, s_d, s_l):
        q0 = i * tq
        m = jnp.max(s_d, axis=-1, keepdims=True)
        if s_l is not None:
            m = jnp.maximum(m, jnp.max(s_l, axis=-1, keepdims=True))
        acc = jnp.dot(jnp.exp(s_d - m).astype(BF16), vext_ref[q0:q0 + tq, :], preferred_element_type=F32)
        if s_l is not None:
            acc = acc + jnp.dot(jnp.exp(s_l - m).astype(BF16), vext_ref[0:q0, :], preferred_element_type=F32)
        return acc[:, 0:pair] / acc[:, pair:]

    nq = seq // tq
    order = [x for p in zip(reversed(range(nq)), range(nq)) for x in p][:nq]
    chains = [(i, hh) for i in order for hh in range(2)]
    pending = scores(*chains[0])
    outs = {}
    for n, (i, hh) in enumerate(chains):
        upcoming = scores(*chains[n + 1]) if n + 1 < len(chains) else None
        outs[hh] = weighted_values(i, *pending)
        pending = upcoming
        if hh == 1:
            o_ref[0, i * tq:(i + 1) * tq, :] = jnp.where(low_lanes, outs[0], outs[1]).astype(BF16)


def _attention(q, k, v):
    B, H, L, _ = q.shape
    tq = min(TQ_ATTN, L)
    return pl.pallas_call(
        functools.partial(_attention_kernel, seq=L, tq=tq),
        grid=(B, H // 2),
        in_specs=[
            pl.BlockSpec((1, 2, L, HEAD_PAD), lambda b, p: (b, p, 0, 0)),
            pl.BlockSpec((1, 2, L, HEAD_PAD), lambda b, p: (b, p, 0, 0)),
            pl.BlockSpec((1, L, 2 * V_HEAD), lambda b, p: (b, 0, p)),
        ],
        out_specs=pl.BlockSpec((1, L, 2 * V_HEAD), lambda b, p: (b, 0, p)),
        out_shape=jax.ShapeDtypeStruct((B, L, ATTN_WIDTH), BF16),
        scratch_shapes=[pltpu.VMEM((L, 2 * V_HEAD + LANES), BF16)],
        compiler_params=_params("parallel", "parallel"),
    )(q, k, v)


def _ssm_kernel(u_ref, bbd_ref, lam_ref, cbd_ref, d_ref, wglu_ref, g_ref, o_ref, state_ref, in_s, out_s, *bu_refs,
                batch, tc):
    @pl.when(pl.program_id(0) == 0)
    def _():
        state_ref[...] = jnp.zeros_like(state_ref)

    steps = tc // SSM_PARTS
    halves = tuple((bu, slice(p * steps * batch, (p + 1) * steps * batch)) for p, bu in enumerate(bu_refs))
    for b in range(batch):
        for hl in range(SSM_WIDTH // LANES):
            lanes = slice(b * SSM_WIDTH + hl * LANES, b * SSM_WIDTH + (hl + 1) * LANES)
            in_s[hl, pl.ds(b, tc, stride=batch), :] = u_ref[:, lanes].astype(F32)

    def u_rows(rows):
        return jnp.concatenate([in_s[hl, rows, :] for hl in range(SSM_WIDTH // LANES)], axis=1)

    for bu_ref, rows in halves:
        bu_ref[...] = jnp.dot(u_rows(rows).astype(BF16), bbd_ref[0], preferred_element_type=F32)
    lam_re = jnp.broadcast_to(lam_ref[0, 0:1, :], (batch, SSM_FLAT))
    lam_im = jnp.broadcast_to(lam_ref[0, 1:2, :], (batch, SSM_FLAT))

    carry = (state_ref[:, 0:SSM_FLAT], state_ref[:, SSM_FLAT:2 * SSM_FLAT])
    for bu_ref, rows in halves:
        def step(t, xs, bu_ref=bu_ref):
            xr, xi = xs
            at = pl.ds(pl.multiple_of(t * batch, batch), batch)
            nr = lam_re * xr - lam_im * xi + bu_ref[at, 0:SSM_FLAT]
            ni = lam_re * xi + lam_im * xr + bu_ref[at, SSM_FLAT:2 * SSM_FLAT]
            bu_ref[at, 0:SSM_FLAT] = nr
            bu_ref[at, SSM_FLAT:2 * SSM_FLAT] = ni
            return nr, ni

        carry = lax.fori_loop(0, steps, step, carry, unroll=True)
        y = jnp.dot(bu_ref[...].astype(BF16), cbd_ref[0], preferred_element_type=F32)
        y = y + d_ref[0] * u_rows(rows)
        z = jax.nn.gelu(y)
        gate = jax.nn.sigmoid(jnp.dot(z.astype(BF16), wglu_ref[0], preferred_element_type=F32))
        res = _rms(z * gate, g_ref[0])
        for hl in range(SSM_WIDTH // LANES):
            out_s[hl, rows, :] = res[:, hl * LANES:(hl + 1) * LANES]
    state_ref[:, 0:SSM_FLAT] = carry[0]
    state_ref[:, SSM_FLAT:2 * SSM_FLAT] = carry[1]
    for b in range(batch):
        for hl in range(SSM_WIDTH // LANES):
            lanes = slice(b * SSM_WIDTH + hl * LANES, b * SSM_WIDTH + (hl + 1) * LANES)
            o_ref[:, lanes] = out_s[hl, pl.ds(b, tc, stride=batch), :].astype(BF16)


def _ssm(u_tm, bbd, lam, cbd, d, wglu, g, batch, layer):
    seq = u_tm.shape[0]
    tc = min(TC_SSM, seq)
    blk = tc * batch
    full = lambda shape: pl.BlockSpec((1,) + shape, lambda t: (layer,) + (0,) * len(shape))
    return pl.pallas_call(
        functools.partial(_ssm_kernel, batch=batch, tc=tc),
        grid=(seq // tc,),
        in_specs=[
            pl.BlockSpec((tc, batch * SSM_WIDTH), lambda t: (t, 0)),
            full((SSM_WIDTH, 2 * SSM_FLAT)), full((2, SSM_FLAT)), full((2 * SSM_FLAT, SSM_WIDTH)),
            full((1, SSM_WIDTH)), full((SSM_WIDTH, SSM_WIDTH)), full((1, SSM_WIDTH)),
        ],
        out_specs=pl.BlockSpec((tc, batch * SSM_WIDTH), lambda t: (t, 0)),
        out_shape=jax.ShapeDtypeStruct((seq, batch * SSM_WIDTH), BF16),
        scratch_shapes=[pltpu.VMEM((batch, 2 * SSM_FLAT), F32), pltpu.VMEM((SSM_WIDTH // LANES, blk, LANES), F32),
                        pltpu.VMEM((SSM_WIDTH // LANES, blk, LANES), F32)]
        + [pltpu.VMEM((blk // SSM_PARTS, 2 * SSM_FLAT), F32)] * SSM_PARTS,
        compiler_params=_params("arbitrary"),
    )(u_tm, bbd, lam, cbd, d, wglu, g)


def _conv_rows(upad_ref, r0, w_ref, b_ref, lnw_ref, lnb_ref, gn_ref):
    base = CONV_HALO - (CONV_K - 1)
    acc = jnp.zeros((CONV_SUB, CONV_WIDTH), F32)
    for off in range(SUBLANES):
        n = CONV_SUB + (SUBLANES if off else 0)
        part = None
        for a8 in range(0, CONV_HALO + 1, SUBLANES):
            kk = a8 + off - base
            if 0 <= kk < CONV_K:
                term = w_ref[0, kk:kk + 1, :] * upad_ref[r0 + a8:r0 + a8 + n, :]
                part = term if part is None else part + term
        acc = acc + part[off:off + CONV_SUB]
    y = acc + b_ref[0]
    mu = jnp.mean(y, axis=-1, keepdims=True)
    var = jnp.mean(jnp.square(y - mu), axis=-1, keepdims=True)
    y = (y - mu) * lax.rsqrt(var + LN_EPS) * lnw_ref[0] + lnb_ref[0]
    y = y * jax.nn.sigmoid(y)
    return _rms(y, gn_ref[0])


ROUTE_E1, ROUTE_E2, ROUTE_R1, ROUTE_R2, ROUTE_G1, ROUTE_G2 = range(6)
ROUTE_ROWS = 8
GRP_LANE0 = N_EXPERTS


def _out_router_kernel(x_ref, ya_ref, ys_ref, ca_ref, cg_ref, ga_ref, wo_ref, fn_ref, wr_ref, br_ref,
                       cw_ref, cb_ref, clnw_ref, clnb_ref, cgn_ref,
                       x1_ref, h2_ref, route_ref, route_t_ref, cnt_ref, sub_ref,
                       carry_ref, tri_ref, upad_ref, yc_ref, *, tm, per_b):
    @pl.when(pl.program_id(0) == 0)
    def _():
        carry_ref[...] = jnp.zeros_like(carry_ref)
        rr = lax.broadcasted_iota(jnp.int32, (tm, tm), 0)
        cc = lax.broadcasted_iota(jnp.int32, (tm, tm), 1)
        tri_ref[...] = (cc < rr).astype(BF16)

    first = pl.program_id(0) % per_b == 0
    upad_ref[0:CONV_HALO, :] = jnp.where(first, 0.0, upad_ref[tm:tm + CONV_HALO, :])
    upad_ref[CONV_HALO:, :] = ca_ref[...].astype(F32) * jax.nn.sigmoid(cg_ref[...].astype(F32))
    ya = _rms(ya_ref[...].astype(F32), ga_ref[0]).astype(BF16)
    acc = jnp.dot(ya, wo_ref[0, 0:ATTN_WIDTH, :], preferred_element_type=F32)
    acc += jnp.dot(ys_ref[...], wo_ref[0, ATTN_WIDTH:ATTN_WIDTH + SSM_WIDTH, :], preferred_element_type=F32)
    for r0 in range(0, tm, CONV_SUB):
        yc_ref[r0:r0 + CONV_SUB, :] = _conv_rows(upad_ref, r0, cw_ref, cb_ref, clnw_ref, clnb_ref, cgn_ref).astype(BF16)
    acc += jnp.dot(yc_ref[...], wo_ref[0, ATTN_WIDTH + SSM_WIDTH:, :], preferred_element_type=F32)
    x1 = x_ref[...] + acc
    x1_ref[...] = x1
    h2 = _rms(x1, fn_ref[0])
    h2_ref[...] = _pack_rows(h2)

    h_hi = h2.astype(BF16)
    h_lo = (h2 - h_hi.astype(F32)).astype(BF16)
    part = jnp.dot(h_hi, wr_ref[0], preferred_element_type=F32)
    logits = (part[:, 0:LANES] + part[:, LANES:2 * LANES]
              + jnp.dot(h_lo, wr_ref[0, :, 0:LANES], preferred_element_type=F32) + br_ref[0])
    lane = lax.broadcasted_iota(jnp.int32, (tm, LANES), 1)
    ninf = -jnp.inf
    big = LANES

    def first_argmax(vals, vmax):
        return jnp.min(jnp.where(vals == vmax, lane, big), axis=-1, keepdims=True)

    grp = jnp.where((lane >= GRP_LANE0) & (lane < GRP_LANE0 + N_EGROUPS), logits, ninf)
    gmax = jnp.max(grp, axis=-1, keepdims=True)
    gsel = first_argmax(grp, gmax) - GRP_LANE0
    p_grp = 1.0 / jnp.sum(jnp.exp(grp - gmax), axis=-1, keepdims=True)

    el = jnp.where((lane < N_EXPERTS) & ((lane // EXP_PER_GROUP) == gsel), logits, ninf)
    m1 = jnp.max(el, axis=-1, keepdims=True)
    e1 = first_argmax(el, m1)
    el2 = jnp.where(lane == e1, ninf, el)
    m2 = jnp.max(el2, axis=-1, keepdims=True)
    e2 = first_argmax(el2, m2)
    t2 = jnp.exp(m2 - m1)
    g1 = p_grp / (1.0 + t2)
    g2 = p_grp * t2 / (1.0 + t2)

    hit1 = lane == e1
    hit2 = lane == e2
    cnt = (hit1 | hit2).astype(F32)
    before = jnp.dot(tri_ref[...], cnt.astype(BF16), preferred_element_type=F32) + carry_ref[...]
    r1 = jnp.sum(jnp.where(hit1, before, 0.0), axis=-1, keepdims=True)
    r2 = jnp.sum(jnp.where(hit2, before, 0.0), axis=-1, keepdims=True)
    subs = [before[h * TM_MOVE:h * TM_MOVE + 1, :] for h in range(tm // TM_MOVE)]
    sub_ref[...] = jnp.concatenate(subs + [jnp.zeros((SUBLANES - len(subs), LANES), F32)], axis=0)
    carry_ref[...] += jnp.sum(cnt, axis=0, keepdims=True)
    cnt_ref[...] = carry_ref[...]

    rec = jnp.where(lane == ROUTE_E1, e1.astype(F32), 0.0)
    rec = jnp.where(lane == ROUTE_E2, e2.astype(F32), rec)
    rec = jnp.where(lane == ROUTE_R1, r1, rec)
    rec = jnp.where(lane == ROUTE_R2, r2, rec)
    rec = jnp.where(lane == ROUTE_G1, g1, rec)
    rec = jnp.where(lane == ROUTE_G2, g2, rec)
    route_ref[...] = rec
    route_t_ref[...] = rec.T[0:ROUTE_ROWS, :]


def _out_router(x, ya, ys_tm, ca, cg, ga, wo, fn, wr, br, conv_params, batch, layer):
    N, D = x.shape
    seq = N // batch
    tm = min(TM_OUT, seq)
    per_b = seq // tm
    full = lambda shape: pl.BlockSpec((1,) + shape, lambda i: (layer,) + (0,) * len(shape))
    whole = lambda shape: pl.BlockSpec(shape, lambda i: (0,) * len(shape))
    tile = lambda w: pl.BlockSpec((tm, w), lambda i: (i, 0))
    return pl.pallas_call(
        functools.partial(_out_router_kernel, tm=tm, per_b=per_b),
        grid=(N // tm,),
        in_specs=[
            tile(D), tile(ATTN_WIDTH),
            pl.BlockSpec((tm, SSM_WIDTH), lambda i: (i % per_b, i // per_b)),
            tile(CONV_WIDTH), tile(CONV_WIDTH),
            full((1, ATTN_WIDTH)), full((D, D)), full((1, D)), full((D, 2 * LANES)), full((1, LANES)),
            full((CONV_K, CONV_WIDTH)), full((1, CONV_WIDTH)), full((1, CONV_WIDTH)), full((1, CONV_WIDTH)),
            full((1, CONV_WIDTH)),
        ],
        out_specs=[tile(D), tile(D_PACK), tile(LANES), pl.BlockSpec((ROUTE_ROWS, tm), lambda i: (0, i)),
                   whole((1, LANES)), pl.BlockSpec((SUBLANES, LANES), lambda i: (i, 0))],
        out_shape=[
            jax.ShapeDtypeStruct((N, D), F32),
            jax.ShapeDtypeStruct((N, D_PACK), U32),
            jax.ShapeDtypeStruct((N, LANES), F32),
            jax.ShapeDtypeStruct((ROUTE_ROWS, N), F32),
            jax.ShapeDtypeStruct((1, LANES), F32),
            jax.ShapeDtypeStruct((N // tm * SUBLANES, LANES), F32),
        ],
        scratch_shapes=[pltpu.VMEM((1, LANES), F32), pltpu.VMEM((tm, tm), BF16),
                        pltpu.VMEM((tm + CONV_HALO, CONV_WIDTH), F32), pltpu.VMEM((tm, CONV_WIDTH), BF16)],
        compiler_params=_params("arbitrary"),
    )(x, ya, ys_tm, ca, cg, ga, wo, fn, wr, br, *conv_params)


def _dispatch(dest_flat, h2, n_rows):
    N, D = h2.shape
    workers = SC_CORES * SC_SUBCORES
    per_worker = N // workers
    chunks = per_worker // SC_CHUNK
    idx = dest_flat.reshape(2, workers, chunks, SC_CHUNK).transpose(1, 0, 2, 3).reshape(workers, 2 * chunks, SC_CHUNK)
    mesh = plsc.VectorSubcoreMesh(core_axis_name="c", subcore_axis_name="s")

    @functools.partial(
        pl.kernel, mesh=mesh,
        out_type=jax.ShapeDtypeStruct((n_rows, D), h2.dtype),
        scratch_types=[pltpu.VMEM((2 * chunks, SC_CHUNK), jnp.int32), pltpu.VMEM((2, SC_CHUNK, D), h2.dtype),
                       pltpu.SemaphoreType.DMA((2,)), pltpu.SemaphoreType.DMA((2, 2))],
    )
    def scatter(h_hbm, idx_hbm, out_hbm, idx_v, rows_v, sem_in, sem_out):
        wid = lax.axis_index("s") * SC_CORES + lax.axis_index("c")
        pltpu.sync_copy(idx_hbm.at[wid], idx_v)

        def load(c):
            first = pl.multiple_of(wid * per_worker + c * SC_CHUNK, SC_CHUNK)
            return pltpu.async_copy(h_hbm.at[pl.ds(first, SC_CHUNK)], rows_v.at[c % 2], sem_in.at[c % 2])

        def store(c, k):
            return pltpu.async_copy(rows_v.at[c % 2], out_hbm.at[idx_v.at[k * chunks + c]], sem_out.at[c % 2, k])

        loads = {0: load(0)}
        stores = {}
        for c in range(chunks):
            loads[c].wait()
            if c >= 1:
                for st in stores[c - 1]:
                    st.wait()
            if c + 1 < chunks:
                loads[c + 1] = load(c + 1)
            stores[c] = (store(c, 0), store(c, 1))
        for st in stores[chunks - 1]:
            st.wait()

    return scatter(h2, idx)


def _experts_kernel(blk_ref, exp_ref, valid_ref, slot_ref, next_ref, used_ref, x_ref, w1_hbm, w3_hbm, w2_hbm, y_ref,
                    w1_f, w3_f, w2_f, w13_s, w2_s, sem, *, layer):
    j = pl.program_id(0)

    def weight_copies(e, slot):
        return (pltpu.make_async_copy(w1_hbm.at[layer, e], w1_f.at[slot], sem.at[slot, 0]),
                pltpu.make_async_copy(w3_hbm.at[layer, e], w3_f.at[slot], sem.at[slot, 1]),
                pltpu.make_async_copy(w2_hbm.at[layer, e], w2_f.at[slot], sem.at[slot, 2]))

    @pl.when(j < used_ref[0])
    def _():
        e = exp_ref[j]
        slot = slot_ref[j]

        @pl.when(j == 0)
        def _():
            for cp in weight_copies(e, slot):
                cp.start()

        @pl.when((j == 0) | (e != exp_ref[jnp.maximum(j - 1, 0)]))
        def _():
            for cp in weight_copies(e, slot):
                cp.wait()
            for c in range(D_FF_E // FF_CHUNK):
                w13_s[:, 2 * FF_CHUNK * c:2 * FF_CHUNK * c + FF_CHUNK] = \
                    w1_f[slot, :, FF_CHUNK * c:FF_CHUNK * (c + 1)].astype(BF16)
                w13_s[:, 2 * FF_CHUNK * c + FF_CHUNK:2 * FF_CHUNK * (c + 1)] = \
                    w3_f[slot, :, FF_CHUNK * c:FF_CHUNK * (c + 1)].astype(BF16)
            w2_s[...] = w2_f[slot].astype(BF16)

            @pl.when(next_ref[j] >= 0)
            def _():
                for cp in weight_copies(next_ref[j], 1 - slot):
                    cp.start()

        valid = valid_ref[j]

        def ffn(rows):
            x = _unpack_rows(x_ref[0:rows, :])
            y = None
            for c in range(D_FF_E // FF_CHUNK):
                ab = jnp.dot(x, w13_s[:, 2 * FF_CHUNK * c:2 * FF_CHUNK * (c + 1)], preferred_element_type=F32)
                a = ab[:, 0:FF_CHUNK]
                hmid = (a * jax.nn.sigmoid(a) * ab[:, FF_CHUNK:]).astype(BF16)
                part = jnp.dot(hmid, w2_s[FF_CHUNK * c:FF_CHUNK * (c + 1), :], preferred_element_type=F32)
                y = part if y is None else y + part
            row = lax.broadcasted_iota(jnp.int32, (rows, 1), 0)
            y_ref[0:rows, :] = jnp.where(row < valid, _pack_rows(y), jnp.uint32(0))
            if rows < TB_EXP:
                y_ref[rows:, :] = jnp.zeros((TB_EXP - rows, D_PACK), U32)

        sizes = [TB_EXP // d for d in EXPERT_TAIL_DIVISORS]
        for n, rows in enumerate(sizes):
            below = sizes[n + 1] if n + 1 < len(sizes) else 0

            @pl.when((valid > below) & (valid <= rows) if n else valid > below)
            def _(rows=rows):
                ffn(rows)


def _experts(blk_map, blk_exp, blk_valid, blk_slot, blk_next, n_used, x_rows, w1, w3, w2, layer):
    R, D = x_rows.shape[0], D_MODEL
    nb = R // TB_EXP
    row_block = lambda j, bm, be, bv, bs, bn, nu: (bm[j], 0)
    return pl.pallas_call(
        functools.partial(_experts_kernel, layer=layer),
        grid_spec=pltpu.PrefetchScalarGridSpec(
            num_scalar_prefetch=6,
            grid=(nb,),
            in_specs=[
                pl.BlockSpec((TB_EXP, D_PACK), row_block),
                pl.BlockSpec(memory_space=pl.ANY), pl.BlockSpec(memory_space=pl.ANY), pl.BlockSpec(memory_space=pl.ANY),
            ],
            out_specs=pl.BlockSpec((TB_EXP, D_PACK), row_block),
            scratch_shapes=[
                pltpu.VMEM((2, D, D_FF_E), F32), pltpu.VMEM((2, D, D_FF_E), F32), pltpu.VMEM((2, D_FF_E, D), F32),
                pltpu.VMEM((D, 2 * D_FF_E), BF16), pltpu.VMEM((D_FF_E, D), BF16),
                pltpu.SemaphoreType.DMA((2, 3)),
            ],
        ),
        out_shape=jax.ShapeDtypeStruct((R, D_PACK), U32),
        compiler_params=_params("arbitrary"),
    )(blk_map, blk_exp, blk_valid, blk_slot, blk_next, n_used, x_rows, w1, w3, w2)


def _combine_kernel(src_ref, ngroup_ref, x1_ref, route_ref, off_ref, rows_ref, o_ref, buf, sem, *, tm):
    i = pl.program_id(0)
    cur = i % 2

    def slot_copy(tile, half, s):
        src = pl.multiple_of(src_ref[tile * RUN_SLOTS + s], RUN_ROWS)
        dst = buf.at[half, pl.ds(pl.multiple_of(s * RUN_ROWS, RUN_ROWS), RUN_ROWS)]
        return pltpu.make_async_copy(rows_ref.at[pl.ds(src, RUN_ROWS)], dst, sem.at[half])

    def fetch(tile, half):
        def issue(g, c):
            for u in range(RUN_GROUP):
                slot_copy(tile, half, g * RUN_GROUP + u).start()
            return c
        lax.fori_loop(0, ngroup_ref[tile], issue, 0)

    def wait(g, c):
        rows = RUN_GROUP * RUN_ROWS
        pltpu.make_async_copy(rows_ref.at[pl.ds(0, rows)], buf.at[cur, pl.ds(0, rows)], sem.at[cur]).wait()
        return c

    @pl.when(i == 0)
    def _():
        buf[...] = jnp.zeros_like(buf)
        fetch(0, 0)

    @pl.when(i + 1 < pl.num_programs(0))
    def _():
        fetch(i + 1, 1 - cur)

    route = route_ref[...]
    lane = lax.broadcasted_iota(jnp.int32, (tm, LANES), 1).astype(F32)
    off = off_ref[0]

    def buf_row(e_lane, r_lane):
        e = route[:, e_lane:e_lane + 1]
        base = jnp.sum(jnp.where(lane == e, off, 0.0), axis=-1, keepdims=True)
        return (base + route[:, r_lane:r_lane + 1]).astype(jnp.int32)

    col = lax.broadcasted_iota(jnp.int32, (tm, RUN_SLOTS * RUN_ROWS), 1)
    pick = (jnp.where(col == buf_row(ROUTE_E1, ROUTE_R1), route[:, ROUTE_G1:ROUTE_G1 + 1], 0.0)
            + jnp.where(col == buf_row(ROUTE_E2, ROUTE_R2), route[:, ROUTE_G2:ROUTE_G2 + 1], 0.0))
    lax.fori_loop(0, ngroup_ref[i], wait, 0)
    o_ref[...] = x1_ref[...] + jnp.dot(pick.astype(BF16), _unpack_rows(buf[cur]), preferred_element_type=F32)


def _combine(src, n_groups, x1, route, off, y_rows):
    N, D = x1.shape
    tm = min(TM_MOVE, N)
    return pl.pallas_call(
        functools.partial(_combine_kernel, tm=tm),
        grid_spec=pltpu.PrefetchScalarGridSpec(
            num_scalar_prefetch=2,
            grid=(N // tm,),
            in_specs=[
                pl.BlockSpec((tm, D), lambda i, s, n: (i, 0)),
                pl.BlockSpec((tm, LANES), lambda i, s, n: (i, 0)),
                pl.BlockSpec((1, 1, LANES), lambda i, s, n: (i, 0, 0)),
                pl.BlockSpec(memory_space=pl.ANY),
            ],
            out_specs=pl.BlockSpec((tm, D), lambda i, s, n: (i, 0)),
            scratch_shapes=[pltpu.VMEM((2, RUN_SLOTS * RUN_ROWS, D_PACK), U32), pltpu.SemaphoreType.DMA((2,))],
        ),
        out_shape=jax.ShapeDtypeStruct((N, D), F32),
        compiler_params=_params("arbitrary"),
    )(src, n_groups, x1, route, off, y_rows)


def _moe(x1, h2, route, route_t, counts, sub_carry, w1, w3, w2, layer):
    N, _ = x1.shape
    tm = min(TM_MOVE, N)
    n_tiles = N // tm
    experts = jnp.arange(N_EXPERTS, dtype=jnp.int32)
    nb = (2 * N + N_EXPERTS * (TB_EXP - 1)) // TB_EXP + 1
    e_id = route_t[ROUTE_E1:ROUTE_E2 + 1].astype(jnp.int32)
    rank = route_t[ROUTE_R1:ROUTE_R2 + 1].astype(jnp.int32)
    cnt = counts[0, :N_EXPERTS].astype(jnp.int32)
    padded = (cnt + TB_EXP - 1) // TB_EXP * TB_EXP
    pad_end = jnp.cumsum(padded)
    pad_start = pad_end - padded
    e_flat = e_id.reshape(1, 2 * N)
    dest = jnp.sum(jnp.where(e_flat == experts[:, None], pad_start[:, None], 0), axis=0) + rank.reshape(2 * N)
    x_rows = _dispatch(dest, h2, nb * TB_EXP)
    n_used = pad_end[-1] // TB_EXP
    blk_map = jnp.minimum(jnp.arange(nb, dtype=jnp.int32), jnp.maximum(n_used - 1, 0))
    blk_exp = jnp.sum(pad_end[None, :] <= (blk_map * TB_EXP)[:, None], axis=1).astype(jnp.int32)
    blk_exp = jnp.minimum(blk_exp, N_EXPERTS - 1)
    is_blk_e = blk_exp[:, None] == experts[None, :]
    blk_valid = jnp.clip(jnp.sum(jnp.where(is_blk_e, (pad_start + cnt)[None, :], 0), axis=1) - blk_map * TB_EXP,
                         0, TB_EXP).astype(jnp.int32)
    has_rows = cnt > 0
    slot_of_e = (jnp.cumsum(has_rows.astype(jnp.int32)) - 1) % 2
    later = jnp.where(has_rows, experts, N_EXPERTS)
    next_of_e = lax.cummin(jnp.concatenate([later[1:], jnp.full((1,), N_EXPERTS, jnp.int32)]), reverse=True)
    next_of_e = jnp.where(next_of_e < N_EXPERTS, next_of_e, -1)
    blk_slot = jnp.sum(jnp.where(is_blk_e, slot_of_e[None, :], 0), axis=1).astype(jnp.int32)
    blk_next = jnp.sum(jnp.where(is_blk_e, next_of_e[None, :], 0), axis=1).astype(jnp.int32)

    per_router_tile = TM_OUT // tm if N >= TM_OUT else 1
    carry = sub_carry.reshape(-1, SUBLANES, LANES)[:, :per_router_tile, :N_EXPERTS].reshape(n_tiles, N_EXPERTS)
    carry = carry.astype(jnp.int32)
    tile_cnt = jnp.concatenate([carry[1:], cnt[None, :]], axis=0) - carry
    run_start = pad_start[None, :] + carry
    first_blk = run_start // RUN_ROWS
    nslot = jnp.where(tile_cnt > 0, (run_start + tile_cnt - 1) // RUN_ROWS - first_blk + 1, 0)
    slot_end = jnp.cumsum(nslot, axis=1)
    slot_base = slot_end - nslot
    slots = jnp.arange(RUN_SLOTS, dtype=jnp.int32)
    slot_e = jnp.minimum(jnp.sum(slot_end[:, None, :] <= slots[None, :, None], axis=2), N_EXPERTS - 1)
    is_slot_e = slot_e[:, :, None] == experts[None, None, :]
    pick = lambda tbl: jnp.sum(jnp.where(is_slot_e, tbl[:, None, :], 0), axis=2)
    src = (pick(first_blk) + slots[None, :] - pick(slot_base)) * RUN_ROWS
    src = jnp.where(slots[None, :] < slot_end[:, -1:], src, 0)
    src = jnp.clip(src, 0, nb * TB_EXP - RUN_ROWS).reshape(-1).astype(jnp.int32)
    n_groups = (slot_end[:, -1] + RUN_GROUP - 1) // RUN_GROUP
    off = (slot_base - first_blk) * RUN_ROWS + pad_start[None, :]
    off = jnp.pad(off.astype(F32), ((0, 0), (0, LANES - N_EXPERTS)))

    y_rows = _experts(blk_map, blk_exp, blk_valid, blk_slot, blk_next, n_used.reshape(1).astype(jnp.int32),
                      x_rows, w1, w3, w2, layer)
    return _combine(src, n_groups.astype(jnp.int32), x1, route, off.reshape(n_tiles, 1, LANES), y_rows)


def _pad_heads(w, width):
    lead = w.shape[:-1]
    w = w.reshape(lead + (N_HEADS, width))
    pad = [(0, 0)] * (w.ndim - 1) + [(0, HEAD_PAD - width)]
    return jnp.pad(w, pad).reshape(lead + (N_HEADS * HEAD_PAD,))


def _swap_rope(w):
    half = QK_ROPE // 2
    lo, hi = w[..., QK_NOPE:QK_NOPE + half], w[..., QK_NOPE + half:QK_HEAD]
    pad = [(0, 0)] * (w.ndim - 1)
    return jnp.pad(jnp.concatenate([hi, lo], -1), pad + [(QK_NOPE, HEAD_PAD - QK_HEAD)])


def _rope_tables(positions):
    inv_freq = ROPE_THETA ** (-jnp.arange(0, QK_ROPE, 2, dtype=F32) / QK_ROPE)
    ang = positions.astype(F32)[..., None] * inv_freq
    table = jnp.concatenate([jnp.cos(ang), jnp.sin(ang)], -1)
    return jnp.pad(table, ((0, 0), (0, 0), (QK_NOPE, LANES - QK_HEAD)))


def _ssm_params(lam_re, lam_im, b_re, b_im, c_re, c_im, log_dt):
    depth = lam_re.shape[0]
    lam = lax.complex(lam_re, lam_im)
    dt = jnp.exp(log_dt)[..., None]
    lam_bar = jnp.exp(lam * dt)
    b_bar = ((lam_bar - 1.0) / lam)[..., None] * lax.complex(b_re, b_im)
    eye = jnp.eye(SSM_GROUPS, dtype=F32)

    def in_blockdiag(m):
        return jnp.einsum("lgpc,gh->lgchp", m, eye).reshape(depth, SSM_WIDTH, SSM_FLAT)

    def out_blockdiag(m):
        return jnp.einsum("lgcp,gh->lgphc", m, eye).reshape(depth, SSM_FLAT, SSM_WIDTH)

    bbd = jnp.concatenate([in_blockdiag(jnp.real(b_bar)), in_blockdiag(jnp.imag(b_bar))], axis=2)
    cbd = jnp.concatenate([out_blockdiag(c_re), out_blockdiag(-c_im)], axis=1)
    lam_rows = jnp.stack([jnp.real(lam_bar).reshape(depth, SSM_FLAT), jnp.imag(lam_bar).reshape(depth, SSM_FLAT)],
                         axis=1)
    return bbd.astype(BF16), lam_rows, cbd.astype(BF16)


def kernel(x, positions, mix_norm, w_in, q_a_norm, w_uq, kv_a_norm, w_ukv, q_norm, k_norm, ssm_lam_re, ssm_lam_im, ssm_b_re, ssm_b_im, ssm_c_re, ssm_c_im, ssm_d, ssm_log_dt, ssm_w_glu, conv_dw_w, conv_dw_b, conv_ln_w, conv_ln_b, out_norm, w_out, ffn_norm, w_grp, b_grp, w_exp, b_exp, w1, w3, w2):
    B, L, D = x.shape
    depth = w_in.shape[0]
    rope = _rope_tables(positions)
    rows3 = lambda v: v[:, None, :]
    lane_tail = lambda v: jnp.pad(v, [(0, 0)] * (v.ndim - 1) + [(0, LANES - v.shape[-1])])
    c_q, c_kv, k_pe, u_s, c_a, c_g = jnp.split(
        w_in, [Q_LORA, Q_LORA + KV_LORA, Q_LORA + KV_LORA + QK_ROPE, Q_LORA + KV_LORA + QK_ROPE + SSM_WIDTH,
               Q_LORA + KV_LORA + QK_ROPE + SSM_WIDTH + CONV_WIDTH], axis=2)
    k_pe_full = jnp.pad(k_pe, ((0, 0), (0, 0), (QK_NOPE, 0)))
    win = jnp.concatenate([c_q, c_kv, lane_tail(k_pe_full), _swap_rope(k_pe_full), u_s, c_a, c_g], axis=2).astype(BF16)
    wkv = w_ukv.reshape(depth, KV_LORA, N_HEADS, QK_NOPE + V_HEAD)
    wuk = _pad_heads(wkv[..., :QK_NOPE].reshape(depth, KV_LORA, N_HEADS * QK_NOPE), QK_NOPE).astype(BF16)
    wuv = wkv[..., QK_NOPE:].reshape(depth, KV_LORA, ATTN_WIDTH).astype(BF16)
    wuq_sw = _swap_rope(w_uq.reshape(depth, Q_LORA, N_HEADS, QK_HEAD)).reshape(depth, Q_LORA, N_HEADS * HEAD_PAD)
    wuq = jnp.concatenate([_pad_heads(w_uq, QK_HEAD), wuq_sw], axis=2).astype(BF16)
    norm_rows = lambda w: jnp.stack([lane_tail(w), _swap_rope(w)], axis=1)
    qn, kn = norm_rows(q_norm * (QK_HEAD ** -0.5)), norm_rows(k_norm)
    bbd, lam_rows, cbd = _ssm_params(ssm_lam_re, ssm_lam_im, ssm_b_re, ssm_b_im, ssm_c_re, ssm_c_im, ssm_log_dt)
    ssm_skip = rows3(ssm_d.reshape(depth, SSM_WIDTH))
    wglu = ssm_w_glu.astype(BF16)
    g_attn = rows3(out_norm[:, :ATTN_WIDTH])
    g_ssm = rows3(out_norm[:, ATTN_WIDTH:ATTN_WIDTH + SSM_WIDTH])
    conv_params = (conv_dw_w, rows3(conv_dw_b), rows3(conv_ln_w), rows3(conv_ln_b),
                   rows3(out_norm[:, ATTN_WIDTH + SSM_WIDTH:]))
    w_route = lane_tail(jnp.concatenate([w_exp, w_grp], axis=2))
    w_route_hi = w_route.astype(BF16)
    w_route = jnp.concatenate([w_route_hi, (w_route - w_route_hi.astype(F32)).astype(BF16)], axis=2)
    b_route = rows3(lane_tail(jnp.concatenate([b_exp, b_grp], axis=1)))
    wo = w_out.astype(BF16)
    for l in range(depth):
        q, k, v, u_tm, ca, cg = _in_proj(x, rope, rows3(mix_norm), win, rows3(q_a_norm), wuq, rows3(kv_a_norm), wuk, wuv,
                                         qn, kn, l)
        y_attn = _attention(q, k, v)
        y_ssm_tm = _ssm(u_tm, bbd, lam_rows, cbd, ssm_skip, wglu, g_ssm, B, l)
        x1, h2, route, route_t, counts, sub_carry = _out_router(
            x.reshape(B * L, D), y_attn.reshape(B * L, ATTN_WIDTH), y_ssm_tm,
            ca.reshape(B * L, CONV_WIDTH), cg.reshape(B * L, CONV_WIDTH), g_attn, wo, rows3(ffn_norm),
            w_route, b_route, conv_params, B, l)
        x = _moe(x1, h2, route, route_t, counts, sub_carry, w1, w3, w2, l).reshape(B, L, D)
    return x
```

```python
import functools

import jax
import jax.numpy as jnp
from jax import lax
from jax.experimental import pallas as pl
from jax.experimental.pallas import tpu as pltpu
from jax.experimental.pallas import tpu_sc as plsc

D_MODEL = 1024
CHUNK = 64
EPS = 1e-6
LN_EPS = 1e-5
N_HEADS = 8
QK_NOPE = 64
QK_ROPE = 32
QK_HEAD = QK_NOPE + QK_ROPE
V_HEAD = 64
Q_LORA = 256
KV_LORA = 128
ROPE_THETA = 10000.0
ATTN_WIDTH = N_HEADS * V_HEAD
SSM_WIDTH = 256
SSM_GROUP = 16
SSM_GROUPS = SSM_WIDTH // SSM_GROUP
SSM_STATE = 64
SSM_FLAT = SSM_GROUPS * SSM_STATE
CONV_WIDTH = 256
CONV_K = 31
N_EGROUPS = 4
EXP_PER_GROUP = 8
N_EXPERTS = N_EGROUPS * EXP_PER_GROUP
D_FF_E = 512

LANES = 128
SUBLANES = 8
HEAD_PAD = LANES
IN_PROJ_PAD = Q_LORA + KV_LORA + SSM_WIDTH + 2 * CONV_WIDTH + 2 * LANES
CONV_HALO = 32
VMEM_LIMIT = 48 * 1024 * 1024

TM_PROJ = 512
TQ_ATTN = 256
TC_SSM = 128
SSM_PARTS = 2
CONV_SUB = 128
TM_OUT = 512
TM_MOVE = 256
RUN_ROWS = SUBLANES
RUN_SLOTS = 2 * N_EXPERTS + 2 * TM_MOVE // RUN_ROWS
RUN_GROUP = 8
SC_CORES, SC_SUBCORES = 2, 16
SC_CHUNK = 64
TB_EXP = 512
EXPERT_TAIL_DIVISORS = (1, 2, 4)
FF_CHUNK = 256

BF16 = jnp.bfloat16
F32 = jnp.float32
U32 = jnp.uint32
D_PACK = D_MODEL // 2


def _pack_rows(v):
    bits = lax.bitcast_convert_type(v.astype(BF16).astype(F32), U32)
    half = v.shape[1] // 2
    return bits[:, 0:half] | (bits[:, half:] >> 16)


def _unpack_rows(w):
    hi = lax.bitcast_convert_type(w & jnp.uint32(0xFFFF0000), F32)
    lo = lax.bitcast_convert_type(w << 16, F32)
    return jnp.concatenate([hi, lo], axis=1).astype(BF16)


def _rms(x, w):
    return x * lax.rsqrt(jnp.mean(x * x, axis=-1, keepdims=True) + EPS) * w


def _params(*sem):
    return pltpu.CompilerParams(dimension_semantics=sem, vmem_limit_bytes=VMEM_LIMIT)


def _in_proj_kernel(x_ref, rope_ref, mixn_ref, win_ref, qan_ref, wuq_ref, kvan_ref, wuk_ref, wuv_ref,
                    qn_ref, kn_ref, q_ref, k_ref, v_ref, u_ref, cu_ref):
    x = x_ref[0]
    h = _rms(x, mixn_ref[0]).astype(BF16)
    head_cols = Q_LORA + KV_LORA + 2 * LANES
    proj = jnp.dot(h, win_ref[0, :, 0:head_cols], preferred_element_type=F32)
    o = 0
    c_q = proj[:, o:o + Q_LORA]; o += Q_LORA
    c_kv = proj[:, o:o + KV_LORA]; o += KV_LORA
    k_pe = proj[:, o:o + LANES]; o += LANES
    k_pe_sw = proj[:, o:o + LANES]

    width = N_HEADS * HEAD_PAD
    q2 = jnp.dot(_rms(c_q, qan_ref[0]).astype(BF16), wuq_ref[0], preferred_element_type=F32)
    q, q_sw = q2[:, 0:width], q2[:, width:2 * width]
    ckv_n = _rms(c_kv, kvan_ref[0]).astype(BF16)
    kn = jnp.dot(ckv_n, wuk_ref[0], preferred_element_type=F32)
    v_ref[0] = jnp.dot(ckv_n, wuv_ref[0], preferred_element_type=F32).astype(BF16)

    rope = rope_ref[0]
    half = QK_ROPE // 2
    lane = lax.broadcasted_iota(jnp.int32, rope.shape, 1)
    lo = (lane >= QK_NOPE) & (lane < QK_NOPE + half)
    hi = (lane >= QK_NOPE + half) & (lane < QK_HEAD)
    cos_t = jnp.where(lane < QK_NOPE, 1.0, jnp.where(lo, rope, jnp.where(hi, pltpu.roll(rope, half, 1), 0.0)))
    sin_t = jnp.where(lo, -pltpu.roll(rope, LANES - half, 1), jnp.where(hi, rope, 0.0))
    a_q, b_q = qn_ref[0, 0:1, :] * cos_t, qn_ref[0, 1:2, :] * sin_t
    a_k = kn_ref[0, 0:1, :] * cos_t
    k_sw_term = k_pe_sw * (kn_ref[0, 1:2, :] * sin_t)

    def inv_rms(y):
        return lax.rsqrt(jnp.sum(y * y, axis=-1, keepdims=True) * (1.0 / QK_HEAD) + EPS)

    def rest(j):
        cols = slice(head_cols + j * SSM_WIDTH, head_cols + (j + 1) * SSM_WIDTH)
        return jnp.dot(h, win_ref[0, :, cols], preferred_element_type=F32)

    for hd in range(N_HEADS):
        if hd == 0:
            u_ref[...] = rest(0).astype(BF16)
        elif hd == 4:
            cu_ref[0] = (rest(1) * jax.nn.sigmoid(rest(2))).astype(BF16)
        sl = slice(hd * HEAD_PAD, (hd + 1) * HEAD_PAD)
        yq = q[:, sl]
        q_ref[0, hd] = ((yq * a_q + q_sw[:, sl] * b_q) * inv_rms(yq)).astype(BF16)
        yk = kn[:, sl] + k_pe
        k_ref[0, hd] = ((yk * a_k + k_sw_term) * inv_rms(yk)).astype(BF16)


def _in_proj(x, rope, mixn, win, qan, wuq, kvan, wuk, wuv, qn, kn, layer):
    B, L, D = x.shape
    tm = min(TM_PROJ, L)
    full = lambda shape: pl.BlockSpec((1,) + shape, lambda b, t: (layer,) + (0,) * len(shape))
    return pl.pallas_call(
        _in_proj_kernel,
        grid=(B, L // tm),
        in_specs=[
            pl.BlockSpec((1, tm, D), lambda b, t: (b, t, 0)),
            pl.BlockSpec((1, tm, LANES), lambda b, t: (b, t, 0)),
            full((1, D)), full((D, IN_PROJ_PAD)), full((1, Q_LORA)), full((Q_LORA, 2 * N_HEADS * HEAD_PAD)),
            full((1, KV_LORA)), full((KV_LORA, N_HEADS * HEAD_PAD)), full((KV_LORA, ATTN_WIDTH)),
            full((2, HEAD_PAD)), full((2, HEAD_PAD)),
        ],
        out_specs=[
            pl.BlockSpec((1, N_HEADS, tm, HEAD_PAD), lambda b, t: (b, 0, t, 0)),
            pl.BlockSpec((1, N_HEADS, tm, HEAD_PAD), lambda b, t: (b, 0, t, 0)),
            pl.BlockSpec((1, tm, ATTN_WIDTH), lambda b, t: (b, t, 0)),
            pl.BlockSpec((tm, SSM_WIDTH), lambda b, t: (t, b)),
            pl.BlockSpec((1, tm, CONV_WIDTH), lambda b, t: (b, t, 0)),
        ],
        out_shape=[
            jax.ShapeDtypeStruct((B, N_HEADS, L, HEAD_PAD), BF16),
            jax.ShapeDtypeStruct((B, N_HEADS, L, HEAD_PAD), BF16),
            jax.ShapeDtypeStruct((B, L, ATTN_WIDTH), BF16),
            jax.ShapeDtypeStruct((L, B * SSM_WIDTH), BF16),
            jax.ShapeDtypeStruct((B, L, CONV_WIDTH), BF16),
        ],
        compiler_params=_params("parallel", "parallel"),
    )(x, rope, mixn, win, qan, wuq, kvan, wuk, wuv, qn, kn)


def _attention_kernel(q_ref, k_ref, v_ref, o_ref, vext_ref, *, seq, tq):
    pair = 2 * V_HEAD
    vext_ref[:, 0:pair] = v_ref[0]
    vext_ref[:, pair:] = jnp.ones((seq, LANES), BF16)
    row_chunk = lax.broadcasted_iota(jnp.int32, (tq, tq), 0) // CHUNK
    col_chunk = lax.broadcasted_iota(jnp.int32, (tq, tq), 1) // CHUNK
    visible = col_chunk <= row_chunk
    neg = jnp.finfo(F32).min
    nt = (((1,), (1,)), ((), ()))
    low_lanes = lax.broadcasted_iota(jnp.int32, (tq, pair), 1) < V_HEAD

    def scores(i, hh):
        q0 = i * tq
        qb = q_ref[0, hh, q0:q0 + tq, :]
        s_d = lax.dot_general(qb, k_ref[0, hh, q0:q0 + tq, :], nt, preferred_element_type=F32)
        s_d = jnp.where(visible, s_d, neg)
        s_l = lax.dot_general(qb, k_ref[0, hh, 0:q0, :], nt, preferred_element_type=F32) if i > 0 else None
        return s_d, s_l

    def weighted_values(i, s_d, s_l):
        q0 = i * tq
        m = jnp.max(s_d, axis=-1, keepdims=True)
        if s_l is not None:
            m = jnp.maximum(m, jnp.max(s_l, axis=-1, keepdims=True))
        acc = jnp.dot(jnp.exp(s_d - m).astype(BF16), vext_ref[q0:q0 + tq, :], preferred_element_type=F32)
        if s_l is not None:
            acc = acc + jnp.dot(jnp.exp(s_l - m).astype(BF16), vext_ref[0:q0, :], preferred_element_type=F32)
        return acc[:, 0:pair] / acc[:, pair:]

    nq = seq // tq
    order = [x for p in zip(reversed(range(nq)), range(nq)) for x in p][:nq]
    chains = [(i, hh) for i in order for hh in range(2)]
    pending = scores(*chains[0])
    outs = {}
    for n, (i, hh) in enumerate(chains):
        upcoming = scores(*chains[n + 1]) if n + 1 < len(chains) else None
        outs[hh] = weighted_values(i, *pending)
        pending = upcoming
        if hh == 1:
            o_ref[0, i * tq:(i + 1) * tq, :] = jnp.where(low_lanes, outs[0], outs[1]).astype(BF16)


def _attention(q, k, v):
    B, H, L, _ = q.shape
    tq = min(TQ_ATTN, L)
    return pl.pallas_call(
        functools.partial(_attention_kernel, seq=L, tq=tq),
        grid=(B, H // 2),
        in_specs=[
            pl.BlockSpec((1, 2, L, HEAD_PAD), lambda b, p: (b, p, 0, 0)),
            pl.BlockSpec((1, 2, L, HEAD_PAD), lambda b, p: (b, p, 0, 0)),
            pl.BlockSpec((1, L, 2 * V_HEAD), lambda b, p: (b, 0, p)),
        ],
        out_specs=pl.BlockSpec((1, L, 2 * V_HEAD), lambda b, p: (b, 0, p)),
        out_shape=jax.ShapeDtypeStruct((B, L, ATTN_WIDTH), BF16),
        scratch_shapes=[pltpu.VMEM((L, 2 * V_HEAD + LANES), BF16)],
        compiler_params=_params("parallel", "parallel"),
    )(q, k, v)


def _ssm_kernel(u_ref, bbd_ref, lam_ref, cbd_ref, d_ref, wglu_ref, g_ref, o_ref, state_ref, in_s, out_s, *bu_refs,
                batch, tc):
    @pl.when(pl.program_id(0) == 0)
    def _():
        state_ref[...] = jnp.zeros_like(state_ref)

    steps = tc // SSM_PARTS
    halves = tuple((bu, slice(p * steps * batch, (p + 1) * steps * batch)) for p, bu in enumerate(bu_refs))
    for b in range(batch):
        for hl in range(SSM_WIDTH // LANES):
            lanes = slice(b * SSM_WIDTH + hl * LANES, b * SSM_WIDTH + (hl + 1) * LANES)
            in_s[hl, pl.ds(b, tc, stride=batch), :] = u_ref[:, lanes].astype(F32)

    def u_rows(rows):
        return jnp.concatenate([in_s[hl, rows, :] for hl in range(SSM_WIDTH // LANES)], axis=1)

    for bu_ref, rows in halves:
        bu_ref[...] = jnp.dot(u_rows(rows).astype(BF16), bbd_ref[0], preferred_element_type=F32)
    lam_re = jnp.broadcast_to(lam_ref[0, 0:1, :], (batch, SSM_FLAT))
    lam_im = jnp.broadcast_to(lam_ref[0, 1:2, :], (batch, SSM_FLAT))

    carry = (state_ref[:, 0:SSM_FLAT], state_ref[:, SSM_FLAT:2 * SSM_FLAT])
    for bu_ref, rows in halves:
        def step(t, xs, bu_ref=bu_ref):
            xr, xi = xs
            at = pl.ds(pl.multiple_of(t * batch, batch), batch)
            nr = lam_re * xr - lam_im * xi + bu_ref[at, 0:SSM_FLAT]
            ni = lam_re * xi + lam_im * xr + bu_ref[at, SSM_FLAT:2 * SSM_FLAT]
            bu_ref[at, 0:SSM_FLAT] = nr
            bu_ref[at, SSM_FLAT:2 * SSM_FLAT] = ni
            return nr, ni

        carry = lax.fori_loop(0, steps, step, carry, unroll=True)
        y = jnp.dot(bu_ref[...].astype(BF16), cbd_ref[0], preferred_element_type=F32)
        y = y + d_ref[0] * u_rows(rows)
        z = jax.nn.gelu(y)
        gate = jax.nn.sigmoid(jnp.dot(z.astype(BF16), wglu_ref[0], preferred_element_type=F32))
        res = _rms(z * gate, g_ref[0])
        for hl in range(SSM_WIDTH // LANES):
            out_s[hl, rows, :] = res[:, hl * LANES:(hl + 1) * LANES]
    state_ref[:, 0:SSM_FLAT] = carry[0]
    state_ref[:, SSM_FLAT:2 * SSM_FLAT] = carry[1]
    for b in range(batch):
        for hl in range(SSM_WIDTH // LANES):
            lanes = slice(b * SSM_WIDTH + hl * LANES, b * SSM_WIDTH + (hl + 1) * LANES)
            o_ref[:, lanes] = out_s[hl, pl.ds(b, tc, stride=batch), :].astype(BF16)


def _ssm(u, bbd, lam, cbd, d, wglu, g, batch, layer):
    seq = u.shape[0]
    tc = min(TC_SSM, seq)
    blk = tc * batch
    full = lambda shape: pl.BlockSpec((1,) + shape, lambda t: (layer,) + (0,) * len(shape))
    return pl.pallas_call(
        functools.partial(_ssm_kernel, batch=batch, tc=tc),
        grid=(seq // tc,),
        in_specs=[
            pl.BlockSpec((tc, batch * SSM_WIDTH), lambda t: (t, 0)),
            full((SSM_WIDTH, 2 * SSM_FLAT)), full((2, SSM_FLAT)), full((2 * SSM_FLAT, SSM_WIDTH)),
            full((1, SSM_WIDTH)), full((SSM_WIDTH, SSM_WIDTH)), full((1, SSM_WIDTH)),
        ],
        out_specs=pl.BlockSpec((tc, batch * SSM_WIDTH), lambda t: (t, 0)),
        out_shape=jax.ShapeDtypeStruct((seq, batch * SSM_WIDTH), BF16),
        scratch_shapes=[pltpu.VMEM((batch, 2 * SSM_FLAT), F32), pltpu.VMEM((SSM_WIDTH // LANES, blk, LANES), F32),
                        pltpu.VMEM((SSM_WIDTH // LANES, blk, LANES), F32)]
        + [pltpu.VMEM((blk // SSM_PARTS, 2 * SSM_FLAT), F32)] * SSM_PARTS,
        compiler_params=_params("arbitrary"),
    )(u, bbd, lam, cbd, d, wglu, g)


def _conv_rows(upad_ref, r0, w_ref, b_ref, lnw_ref, lnb_ref, gn_ref):
    base = CONV_HALO - (CONV_K - 1)
    acc = jnp.zeros((CONV_SUB, CONV_WIDTH), F32)
    for off in range(SUBLANES):
        n = CONV_SUB + (SUBLANES if off else 0)
        part = None
        for a8 in range(0, CONV_HALO + 1, SUBLANES):
            kk = a8 + off - base
            if 0 <= kk < CONV_K:
                term = w_ref[0, kk:kk + 1, :] * upad_ref[r0 + a8:r0 + a8 + n, :]
                part = term if part is None else part + term
        acc = acc + part[off:off + CONV_SUB]
    y = acc + b_ref[0]
    mu = jnp.mean(y, axis=-1, keepdims=True)
    var = jnp.mean(jnp.square(y - mu), axis=-1, keepdims=True)
    y = (y - mu) * lax.rsqrt(var + LN_EPS) * lnw_ref[0] + lnb_ref[0]
    y = y * jax.nn.sigmoid(y)
    return _rms(y, gn_ref[0])


ROUTE_E1, ROUTE_E2, ROUTE_R1, ROUTE_R2, ROUTE_G1, ROUTE_G2 = range(6)
ROUTE_ROWS = 8
GRP_LANE0 = N_EXPERTS


def _out_router_kernel(x_ref, ya_ref, ys_ref, cu_ref, ga_ref, wo_ref, fn_ref, wr_ref, br_ref,
                       cw_ref, cb_ref, clnw_ref, clnb_ref, cgn_ref,
                       x1_ref, h2_ref, route_ref, route_t_ref, cnt_ref, sub_ref,
                       carry_ref, tri_ref, upad_ref, yc_ref, *, tm, per_b):
    @pl.when(pl.program_id(0) == 0)
    def _():
        carry_ref[...] = jnp.zeros_like(carry_ref)
        rr = lax.broadcasted_iota(jnp.int32, (tm, tm), 0)
        cc = lax.broadcasted_iota(jnp.int32, (tm, tm), 1)
        tri_ref[...] = (cc < rr).astype(BF16)

    first = pl.program_id(0) % per_b == 0
    upad_ref[0:CONV_HALO, :] = jnp.where(first, 0.0, upad_ref[tm:tm + CONV_HALO, :])
    upad_ref[CONV_HALO:, :] = cu_ref[...].astype(F32)
    ya = _rms(ya_ref[...].astype(F32), ga_ref[0]).astype(BF16)
    acc = jnp.dot(ya, wo_ref[0, 0:ATTN_WIDTH, :], preferred_element_type=F32)
    acc += jnp.dot(ys_ref[...], wo_ref[0, ATTN_WIDTH:ATTN_WIDTH + SSM_WIDTH, :], preferred_element_type=F32)
    for r0 in range(0, tm, CONV_SUB):
        yc_ref[r0:r0 + CONV_SUB, :] = _conv_rows(upad_ref, r0, cw_ref, cb_ref, clnw_ref, clnb_ref, cgn_ref).astype(BF16)
    acc += jnp.dot(yc_ref[...], wo_ref[0, ATTN_WIDTH + SSM_WIDTH:, :], preferred_element_type=F32)
    x1 = x_ref[...] + acc
    x1_ref[...] = x1
    h2 = _rms(x1, fn_ref[0])
    h2_ref[...] = _pack_rows(h2)

    h_hi = h2.astype(BF16)
    h_lo = (h2 - h_hi.astype(F32)).astype(BF16)
    part = jnp.dot(h_hi, wr_ref[0], preferred_element_type=F32)
    logits = (part[:, 0:LANES] + part[:, LANES:2 * LANES]
              + jnp.dot(h_lo, wr_ref[0, :, 0:LANES], preferred_element_type=F32) + br_ref[0])
    lane = lax.broadcasted_iota(jnp.int32, (tm, LANES), 1)
    ninf = -jnp.inf
    big = LANES

    def first_argmax(vals, vmax):
        return jnp.min(jnp.where(vals == vmax, lane, big), axis=-1, keepdims=True)

    grp = jnp.where((lane >= GRP_LANE0) & (lane < GRP_LANE0 + N_EGROUPS), logits, ninf)
    gmax = jnp.max(grp, axis=-1, keepdims=True)
    gsel = first_argmax(grp, gmax) - GRP_LANE0
    p_grp = 1.0 / jnp.sum(jnp.exp(grp - gmax), axis=-1, keepdims=True)

    el = jnp.where((lane < N_EXPERTS) & ((lane // EXP_PER_GROUP) == gsel), logits, ninf)
    m1 = jnp.max(el, axis=-1, keepdims=True)
    e1 = first_argmax(el, m1)
    el2 = jnp.where(lane == e1, ninf, el)
    m2 = jnp.max(el2, axis=-1, keepdims=True)
    e2 = first_argmax(el2, m2)
    t2 = jnp.exp(m2 - m1)
    g1 = p_grp / (1.0 + t2)
    g2 = p_grp * t2 / (1.0 + t2)

    hit1 = lane == e1
    hit2 = lane == e2
    cnt = (hit1 | hit2).astype(F32)
    before = jnp.dot(tri_ref[...], cnt.astype(BF16), preferred_element_type=F32) + carry_ref[...]
    r1 = jnp.sum(jnp.where(hit1, before, 0.0), axis=-1, keepdims=True)
    r2 = jnp.sum(jnp.where(hit2, before, 0.0), axis=-1, keepdims=True)
    subs = [before[h * TM_MOVE:h * TM_MOVE + 1, :] for h in range(tm // TM_MOVE)]
    sub_ref[...] = jnp.concatenate(subs + [jnp.zeros((SUBLANES - len(subs), LANES), F32)], axis=0)
    carry_ref[...] += jnp.sum(cnt, axis=0, keepdims=True)
    cnt_ref[...] = carry_ref[...]

    rec = jnp.where(lane == ROUTE_E1, e1.astype(F32), 0.0)
    rec = jnp.where(lane == ROUTE_E2, e2.astype(F32), rec)
    rec = jnp.where(lane == ROUTE_R1, r1, rec)
    rec = jnp.where(lane == ROUTE_R2, r2, rec)
    rec = jnp.where(lane == ROUTE_G1, g1, rec)
    rec = jnp.where(lane == ROUTE_G2, g2, rec)
    route_ref[...] = rec
    route_t_ref[...] = rec.T[0:ROUTE_ROWS, :]


def _out_router(x, ya, ys, cu, ga, wo, fn, wr, br, conv_params, batch, layer):
    N, D = x.shape
    seq = N // batch
    tm = min(TM_OUT, seq)
    per_b = seq // tm
    full = lambda shape: pl.BlockSpec((1,) + shape, lambda i: (layer,) + (0,) * len(shape))
    whole = lambda shape: pl.BlockSpec(shape, lambda i: (0,) * len(shape))
    tile = lambda w: pl.BlockSpec((tm, w), lambda i: (i, 0))
    return pl.pallas_call(
        functools.partial(_out_router_kernel, tm=tm, per_b=per_b),
        grid=(N // tm,),
        in_specs=[
            tile(D), tile(ATTN_WIDTH),
            pl.BlockSpec((tm, SSM_WIDTH), lambda i: (i % per_b, i // per_b)),
            tile(CONV_WIDTH),
            full((1, ATTN_WIDTH)), full((D, D)), full((1, D)), full((D, 2 * LANES)), full((1, LANES)),
            full((CONV_K, CONV_WIDTH)), full((1, CONV_WIDTH)), full((1, CONV_WIDTH)), full((1, CONV_WIDTH)),
            full((1, CONV_WIDTH)),
        ],
        out_specs=[tile(D), tile(D_PACK), tile(LANES), pl.BlockSpec((ROUTE_ROWS, tm), lambda i: (0, i)),
                   whole((1, LANES)), pl.BlockSpec((SUBLANES, LANES), lambda i: (i, 0))],
        out_shape=[
            jax.ShapeDtypeStruct((N, D), F32),
            jax.ShapeDtypeStruct((N, D_PACK), U32),
            jax.ShapeDtypeStruct((N, LANES), F32),
            jax.ShapeDtypeStruct((ROUTE_ROWS, N), F32),
            jax.ShapeDtypeStruct((1, LANES), F32),
            jax.ShapeDtypeStruct((N // tm * SUBLANES, LANES), F32),
        ],
        scratch_shapes=[pltpu.VMEM((1, LANES), F32), pltpu.VMEM((tm, tm), BF16),
                        pltpu.VMEM((tm + CONV_HALO, CONV_WIDTH), F32), pltpu.VMEM((tm, CONV_WIDTH), BF16)],
        compiler_params=_params("arbitrary"),
    )(x, ya, ys, cu, ga, wo, fn, wr, br, *conv_params)


def _dispatch(dest_flat, h2, n_rows):
    N, D = h2.shape
    workers = SC_CORES * SC_SUBCORES
    per_worker = N // workers
    chunks = per_worker // SC_CHUNK
    idx = dest_flat.reshape(2, workers, chunks, SC_CHUNK).transpose(1, 0, 2, 3).reshape(workers, 2 * chunks, SC_CHUNK)
    mesh = plsc.VectorSubcoreMesh(core_axis_name="c", subcore_axis_name="s")

    @functools.partial(
        pl.kernel, mesh=mesh,
        out_type=jax.ShapeDtypeStruct((n_rows, D), h2.dtype),
        scratch_types=[pltpu.VMEM((2 * chunks, SC_CHUNK), jnp.int32), pltpu.VMEM((2, SC_CHUNK, D), h2.dtype),
                       pltpu.SemaphoreType.DMA((2,)), pltpu.SemaphoreType.DMA((2, 2))],
    )
    def scatter(h_hbm, idx_hbm, out_hbm, idx_v, rows_v, sem_in, sem_out):
        wid = lax.axis_index("s") * SC_CORES + lax.axis_index("c")
        pltpu.sync_copy(idx_hbm.at[wid], idx_v)

        def load(c):
            first = pl.multiple_of(wid * per_worker + c * SC_CHUNK, SC_CHUNK)
            return pltpu.async_copy(h_hbm.at[pl.ds(first, SC_CHUNK)], rows_v.at[c % 2], sem_in.at[c % 2])

        def store(c, k):
            return pltpu.async_copy(rows_v.at[c % 2], out_hbm.at[idx_v.at[k * chunks + c]], sem_out.at[c % 2, k])

        loads = {0: load(0)}
        stores = {}
        for c in range(chunks):
            loads[c].wait()
            if c >= 1:
                for st in stores[c - 1]:
                    st.wait()
            if c + 1 < chunks:
                loads[c + 1] = load(c + 1)
            stores[c] = (store(c, 0), store(c, 1))
        for st in stores[chunks - 1]:
            st.wait()

    return scatter(h2, idx)


def _experts_kernel(blk_ref, exp_ref, valid_ref, slot_ref, next_ref, used_ref, x_ref, w1_hbm, w3_hbm, w2_hbm, y_ref,
                    w1_f, w3_f, w2_f, w13_s, w2_s, sem, *, layer):
    j = pl.program_id(0)

    def weight_copies(e, slot):
        return (pltpu.make_async_copy(w1_hbm.at[layer, e], w1_f.at[slot], sem.at[slot, 0]),
                pltpu.make_async_copy(w3_hbm.at[layer, e], w3_f.at[slot], sem.at[slot, 1]),
                pltpu.make_async_copy(w2_hbm.at[layer, e], w2_f.at[slot], sem.at[slot, 2]))

    @pl.when(j < used_ref[0])
    def _():
        e = exp_ref[j]
        slot = slot_ref[j]

        @pl.when(j == 0)
        def _():
            for cp in weight_copies(e, slot):
                cp.start()

        @pl.when((j == 0) | (e != exp_ref[jnp.maximum(j - 1, 0)]))
        def _():
            for cp in weight_copies(e, slot):
                cp.wait()
            for c in range(D_FF_E // FF_CHUNK):
                w13_s[:, 2 * FF_CHUNK * c:2 * FF_CHUNK * c + FF_CHUNK] = \
                    w1_f[slot, :, FF_CHUNK * c:FF_CHUNK * (c + 1)].astype(BF16)
                w13_s[:, 2 * FF_CHUNK * c + FF_CHUNK:2 * FF_CHUNK * (c + 1)] = \
                    w3_f[slot, :, FF_CHUNK * c:FF_CHUNK * (c + 1)].astype(BF16)
            w2_s[...] = w2_f[slot].astype(BF16)

            @pl.when(next_ref[j] >= 0)
            def _():
                for cp in weight_copies(next_ref[j], 1 - slot):
                    cp.start()

        valid = valid_ref[j]

        def ffn(rows):
            x = _unpack_rows(x_ref[0:rows, :])
            y = None
            for c in range(D_FF_E // FF_CHUNK):
                ab = jnp.dot(x, w13_s[:, 2 * FF_CHUNK * c:2 * FF_CHUNK * (c + 1)], preferred_element_type=F32)
                a = ab[:, 0:FF_CHUNK]
                hmid = (a * jax.nn.sigmoid(a) * ab[:, FF_CHUNK:]).astype(BF16)
                part = jnp.dot(hmid, w2_s[FF_CHUNK * c:FF_CHUNK * (c + 1), :], preferred_element_type=F32)
                y = part if y is None else y + part
            row = lax.broadcasted_iota(jnp.int32, (rows, 1), 0)
            y_ref[0:rows, :] = jnp.where(row < valid, _pack_rows(y), jnp.uint32(0))
            if rows < TB_EXP:
                y_ref[rows:, :] = jnp.zeros((TB_EXP - rows, D_PACK), U32)

        sizes = [TB_EXP // d for d in EXPERT_TAIL_DIVISORS]
        for n, rows in enumerate(sizes):
            below = sizes[n + 1] if n + 1 < len(sizes) else 0

            @pl.when((valid > below) & (valid <= rows) if n else valid > below)
            def _(rows=rows):
                ffn(rows)


def _experts(blk_map, blk_exp, blk_valid, blk_slot, blk_next, n_used, x_rows, w1, w3, w2, layer):
    R, D = x_rows.shape[0], D_MODEL
    nb = R // TB_EXP
    row_block = lambda j, bm, be, bv, bs, bn, nu: (bm[j], 0)
    return pl.pallas_call(
        functools.partial(_experts_kernel, layer=layer),
        grid_spec=pltpu.PrefetchScalarGridSpec(
            num_scalar_prefetch=6,
            grid=(nb,),
            in_specs=[
                pl.BlockSpec((TB_EXP, D_PACK), row_block),
                pl.BlockSpec(memory_space=pl.ANY), pl.BlockSpec(memory_space=pl.ANY), pl.BlockSpec(memory_space=pl.ANY),
            ],
            out_specs=pl.BlockSpec((TB_EXP, D_PACK), row_block),
            scratch_shapes=[
                pltpu.VMEM((2, D, D_FF_E), F32), pltpu.VMEM((2, D, D_FF_E), F32), pltpu.VMEM((2, D_FF_E, D), F32),
                pltpu.VMEM((D, 2 * D_FF_E), BF16), pltpu.VMEM((D_FF_E, D), BF16),
                pltpu.SemaphoreType.DMA((2, 3)),
            ],
        ),
        out_shape=jax.ShapeDtypeStruct((R, D_PACK), U32),
        compiler_params=_params("arbitrary"),
    )(blk_map, blk_exp, blk_valid, blk_slot, blk_next, n_used, x_rows, w1, w3, w2)


def _combine_kernel(src_ref, ngroup_ref, x1_ref, route_ref, off_ref, rows_ref, o_ref, buf, sem, *, tm):
    i = pl.program_id(0)
    cur = i % 2

    def slot_copy(tile, half, s):
        src = pl.multiple_of(src_ref[tile * RUN_SLOTS + s], RUN_ROWS)
        dst = buf.at[half, pl.ds(pl.multiple_of(s * RUN_ROWS, RUN_ROWS), RUN_ROWS)]
        return pltpu.make_async_copy(rows_ref.at[pl.ds(src, RUN_ROWS)], dst, sem.at[half])

    def fetch(tile, half):
        def issue(g, c):
            for u in range(RUN_GROUP):
                slot_copy(tile, half, g * RUN_GROUP + u).start()
            return c
        lax.fori_loop(0, ngroup_ref[tile], issue, 0)

    def wait(g, c):
        rows = RUN_GROUP * RUN_ROWS
        pltpu.make_async_copy(rows_ref.at[pl.ds(0, rows)], buf.at[cur, pl.ds(0, rows)], sem.at[cur]).wait()
        return c

    @pl.when(i == 0)
    def _():
        buf[...] = jnp.zeros_like(buf)
        fetch(0, 0)

    @pl.when(i + 1 < pl.num_programs(0))
    def _():
        fetch(i + 1, 1 - cur)

    route = route_ref[...]
    lane = lax.broadcasted_iota(jnp.int32, (tm, LANES), 1).astype(F32)
    off = off_ref[0]

    def buf_row(e_lane, r_lane):
        e = route[:, e_lane:e_lane + 1]
        base = jnp.sum(jnp.where(lane == e, off, 0.0), axis=-1, keepdims=True)
        return (base + route[:, r_lane:r_lane + 1]).astype(jnp.int32)

    col = lax.broadcasted_iota(jnp.int32, (tm, RUN_SLOTS * RUN_ROWS), 1)
    pick = (jnp.where(col == buf_row(ROUTE_E1, ROUTE_R1), route[:, ROUTE_G1:ROUTE_G1 + 1], 0.0)
            + jnp.where(col == buf_row(ROUTE_E2, ROUTE_R2), route[:, ROUTE_G2:ROUTE_G2 + 1], 0.0))
    lax.fori_loop(0, ngroup_ref[i], wait, 0)
    o_ref[...] = x1_ref[...] + jnp.dot(pick.astype(BF16), _unpack_rows(buf[cur]), preferred_element_type=F32)


def _combine(src, n_groups, x1, route, off, y_rows):
    N, D = x1.shape
    tm = min(TM_MOVE, N)
    return pl.pallas_call(
        functools.partial(_combine_kernel, tm=tm),
        grid_spec=pltpu.PrefetchScalarGridSpec(
            num_scalar_prefetch=2,
            grid=(N // tm,),
            in_specs=[
                pl.BlockSpec((tm, D), lambda i, s, n: (i, 0)),
                pl.BlockSpec((tm, LANES), lambda i, s, n: (i, 0)),
                pl.BlockSpec((1, 1, LANES), lambda i, s, n: (i, 0, 0)),
                pl.BlockSpec(memory_space=pl.ANY),
            ],
            out_specs=pl.BlockSpec((tm, D), lambda i, s, n: (i, 0)),
            scratch_shapes=[pltpu.VMEM((2, RUN_SLOTS * RUN_ROWS, D_PACK), U32), pltpu.SemaphoreType.DMA((2,))],
        ),
        out_shape=jax.ShapeDtypeStruct((N, D), F32),
        compiler_params=_params("arbitrary"),
    )(src, n_groups, x1, route, off, y_rows)


def _moe(x1, h2, route, route_t, counts, sub_carry, w1, w3, w2, layer):
    N, _ = x1.shape
    tm = min(TM_MOVE, N)
    n_tiles = N // tm
    experts = jnp.arange(N_EXPERTS, dtype=jnp.int32)
    nb = (2 * N + N_EXPERTS * (TB_EXP - 1)) // TB_EXP + 1
    e_id = route_t[ROUTE_E1:ROUTE_E2 + 1].astype(jnp.int32)
    rank = route_t[ROUTE_R1:ROUTE_R2 + 1].astype(jnp.int32)
    cnt = counts[0, :N_EXPERTS].astype(jnp.int32)
    padded = (cnt + TB_EXP - 1) // TB_EXP * TB_EXP
    pad_end = jnp.cumsum(padded)
    pad_start = pad_end - padded
    e_flat = e_id.reshape(1, 2 * N)
    dest = jnp.sum(jnp.where(e_flat == experts[:, None], pad_start[:, None], 0), axis=0) + rank.reshape(2 * N)
    x_rows = _dispatch(dest, h2, nb * TB_EXP)
    n_used = pad_end[-1] // TB_EXP
    blk_map = jnp.minimum(jnp.arange(nb, dtype=jnp.int32), jnp.maximum(n_used - 1, 0))
    blk_exp = jnp.sum(pad_end[None, :] <= (blk_map * TB_EXP)[:, None], axis=1).astype(jnp.int32)
    blk_exp = jnp.minimum(blk_exp, N_EXPERTS - 1)
    is_blk_e = blk_exp[:, None] == experts[None, :]
    blk_valid = jnp.clip(jnp.sum(jnp.where(is_blk_e, (pad_start + cnt)[None, :], 0), axis=1) - blk_map * TB_EXP,
                         0, TB_EXP).astype(jnp.int32)
    has_rows = cnt > 0
    slot_of_e = (jnp.cumsum(has_rows.astype(jnp.int32)) - 1) % 2
    later = jnp.where(has_rows, experts, N_EXPERTS)
    next_of_e = lax.cummin(jnp.concatenate([later[1:], jnp.full((1,), N_EXPERTS, jnp.int32)]), reverse=True)
    next_of_e = jnp.where(next_of_e < N_EXPERTS, next_of_e, -1)
    blk_slot = jnp.sum(jnp.where(is_blk_e, slot_of_e[None, :], 0), axis=1).astype(jnp.int32)
    blk_next = jnp.sum(jnp.where(is_blk_e, next_of_e[None, :], 0), axis=1).astype(jnp.int32)

    per_router_tile = TM_OUT // tm if N >= TM_OUT else 1
    carry = sub_carry.reshape(-1, SUBLANES, LANES)[:, :per_router_tile, :N_EXPERTS].reshape(n_tiles, N_EXPERTS)
    carry = carry.astype(jnp.int32)
    tile_cnt = jnp.concatenate([carry[1:], cnt[None, :]], axis=0) - carry
    run_start = pad_start[None, :] + carry
    first_blk = run_start // RUN_ROWS
    nslot = jnp.where(tile_cnt > 0, (run_start + tile_cnt - 1) // RUN_ROWS - first_blk + 1, 0)
    slot_end = jnp.cumsum(nslot, axis=1)
    slot_base = slot_end - nslot
    slots = jnp.arange(RUN_SLOTS, dtype=jnp.int32)
    slot_e = jnp.minimum(jnp.sum(slot_end[:, None, :] <= slots[None, :, None], axis=2), N_EXPERTS - 1)
    is_slot_e = slot_e[:, :, None] == experts[None, None, :]
    pick = lambda tbl: jnp.sum(jnp.where(is_slot_e, tbl[:, None, :], 0), axis=2)
    src = (pick(first_blk) + slots[None, :] - pick(slot_base)) * RUN_ROWS
    src = jnp.where(slots[None, :] < slot_end[:, -1:], src, 0)
    src = jnp.clip(src, 0, nb * TB_EXP - RUN_ROWS).reshape(-1).astype(jnp.int32)
    n_groups = (slot_end[:, -1] + RUN_GROUP - 1) // RUN_GROUP
    off = (slot_base - first_blk) * RUN_ROWS + pad_start[None, :]
    off = jnp.pad(off.astype(F32), ((0, 0), (0, LANES - N_EXPERTS)))

    y_rows = _experts(blk_map, blk_exp, blk_valid, blk_slot, blk_next, n_used.reshape(1).astype(jnp.int32),
                      x_rows, w1, w3, w2, layer)
    return _combine(src, n_groups.astype(jnp.int32), x1, route, off.reshape(n_tiles, 1, LANES), y_rows)


def _pad_heads(w, width):
    lead = w.shape[:-1]
    w = w.reshape(lead + (N_HEADS, width))
    pad = [(0, 0)] * (w.ndim - 1) + [(0, HEAD_PAD - width)]
    return jnp.pad(w, pad).reshape(lead + (N_HEADS * HEAD_PAD,))


def _swap_rope(w):
    half = QK_ROPE // 2
    lo, hi = w[..., QK_NOPE:QK_NOPE + half], w[..., QK_NOPE + half:QK_HEAD]
    pad = [(0, 0)] * (w.ndim - 1)
    return jnp.pad(jnp.concatenate([hi, lo], -1), pad + [(QK_NOPE, HEAD_PAD - QK_HEAD)])


def _rope_tables(positions):
    inv_freq = ROPE_THETA ** (-jnp.arange(0, QK_ROPE, 2, dtype=F32) / QK_ROPE)
    ang = positions.astype(F32)[..., None] * inv_freq
    table = jnp.concatenate([jnp.cos(ang), jnp.sin(ang)], -1)
    return jnp.pad(table, ((0, 0), (0, 0), (QK_NOPE, LANES - QK_HEAD)))


def _ssm_params(lam_re, lam_im, b_re, b_im, c_re, c_im, log_dt):
    depth = lam_re.shape[0]
    lam = lax.complex(lam_re, lam_im)
    dt = jnp.exp(log_dt)[..., None]
    lam_bar = jnp.exp(lam * dt)
    b_bar = ((lam_bar - 1.0) / lam)[..., None] * lax.complex(b_re, b_im)
    eye = jnp.eye(SSM_GROUPS, dtype=F32)

    def in_blockdiag(m):
        return jnp.einsum("lgpc,gh->lgchp", m, eye).reshape(depth, SSM_WIDTH, SSM_FLAT)

    def out_blockdiag(m):
        return jnp.einsum("lgcp,gh->lgphc", m, eye).reshape(depth, SSM_FLAT, SSM_WIDTH)

    bbd = jnp.concatenate([in_blockdiag(jnp.real(b_bar)), in_blockdiag(jnp.imag(b_bar))], axis=2)
    cbd = jnp.concatenate([out_blockdiag(c_re), out_blockdiag(-c_im)], axis=1)
    lam_rows = jnp.stack([jnp.real(lam_bar).reshape(depth, SSM_FLAT), jnp.imag(lam_bar).reshape(depth, SSM_FLAT)],
                         axis=1)
    return bbd.astype(BF16), lam_rows, cbd.astype(BF16)


def kernel(x, positions, mix_norm, w_in, q_a_norm, w_uq, kv_a_norm, w_ukv, q_norm, k_norm, ssm_lam_re, ssm_lam_im, ssm_b_re, ssm_b_im, ssm_c_re, ssm_c_im, ssm_d, ssm_log_dt, ssm_w_glu, conv_dw_w, conv_dw_b, conv_ln_w, conv_ln_b, out_norm, w_out, ffn_norm, w_grp, b_grp, w_exp, b_exp, w1, w3, w2):
    B, L, D = x.shape
    depth = w_in.shape[0]
    rope = _rope_tables(positions)
    rows3 = lambda v: v[:, None, :]
    lane_tail = lambda v: jnp.pad(v, [(0, 0)] * (v.ndim - 1) + [(0, LANES - v.shape[-1])])
    c_q, c_kv, k_pe, u_s, c_a, c_g = jnp.split(
        w_in, [Q_LORA, Q_LORA + KV_LORA, Q_LORA + KV_LORA + QK_ROPE, Q_LORA + KV_LORA + QK_ROPE + SSM_WIDTH,
               Q_LORA + KV_LORA + QK_ROPE + SSM_WIDTH + CONV_WIDTH], axis=2)
    k_pe_full = jnp.pad(k_pe, ((0, 0), (0, 0), (QK_NOPE, 0)))
    win = jnp.concatenate([c_q, c_kv, lane_tail(k_pe_full), _swap_rope(k_pe_full), u_s, c_a, c_g], axis=2).astype(BF16)
    wkv = w_ukv.reshape(depth, KV_LORA, N_HEADS, QK_NOPE + V_HEAD)
    wuk = _pad_heads(wkv[..., :QK_NOPE].reshape(depth, KV_LORA, N_HEADS * QK_NOPE), QK_NOPE).astype(BF16)
    wuv = wkv[..., QK_NOPE:].reshape(depth, KV_LORA, ATTN_WIDTH).astype(BF16)
    wuq_sw = _swap_rope(w_uq.reshape(depth, Q_LORA, N_HEADS, QK_HEAD)).reshape(depth, Q_LORA, N_HEADS * HEAD_PAD)
    wuq = jnp.concatenate([_pad_heads(w_uq, QK_HEAD), wuq_sw], axis=2).astype(BF16)
    norm_rows = lambda w: jnp.stack([lane_tail(w), _swap_rope(w)], axis=1)
    qn, kn = norm_rows(q_norm * (QK_HEAD ** -0.5)), norm_rows(k_norm)
    bbd, lam_rows, cbd = _ssm_params(ssm_lam_re, ssm_lam_im, ssm_b_re, ssm_b_im, ssm_c_re, ssm_c_im, ssm_log_dt)
    ssm_skip = rows3(ssm_d.reshape(depth, SSM_WIDTH))
    wglu = ssm_w_glu.astype(BF16)
    g_attn = rows3(out_norm[:, :ATTN_WIDTH])
    g_ssm = rows3(out_norm[:, ATTN_WIDTH:ATTN_WIDTH + SSM_WIDTH])
    conv_params = (conv_dw_w, rows3(conv_dw_b), rows3(conv_ln_w), rows3(conv_ln_b),
                   rows3(out_norm[:, ATTN_WIDTH + SSM_WIDTH:]))
    w_route = lane_tail(jnp.concatenate([w_exp, w_grp], axis=2))
    w_route_hi = w_route.astype(BF16)
    w_route = jnp.concatenate([w_route_hi, (w_route - w_route_hi.astype(F32)).astype(BF16)], axis=2)
    b_route = rows3(lane_tail(jnp.concatenate([b_exp, b_grp], axis=1)))
    wo = w_out.astype(BF16)
    for l in range(depth):
        q, k, v, u_ssm, cu = _in_proj(x, rope, rows3(mix_norm), win, rows3(q_a_norm), wuq, rows3(kv_a_norm), wuk, wuv,
                                         qn, kn, l)
        y_attn = _attention(q, k, v)
        y_ssm = _ssm(u_ssm, bbd, lam_rows, cbd, ssm_skip, wglu, g_ssm, B, l)
        x1, h2, route, route_t, counts, sub_carry = _out_router(
            x.reshape(B * L, D), y_attn.reshape(B * L, ATTN_WIDTH), y_ssm, cu.reshape(B * L, CONV_WIDTH), g_attn, wo, rows3(ffn_norm),
            w_route, b_route, conv_params, B, l)
        x = _moe(x1, h2, route, route_t, counts, sub_carry, w1, w3, w2, l).reshape(B, L, D)
    return x
```

```python
import functools

import jax
import jax.numpy as jnp
from jax import lax
from jax.experimental import pallas as pl
from jax.experimental.pallas import tpu as pltpu
from jax.experimental.pallas import tpu_sc as plsc

D_MODEL = 1024
CHUNK = 64
EPS = 1e-6
LN_EPS = 1e-5
N_HEADS = 8
QK_NOPE = 64
QK_ROPE = 32
QK_HEAD = QK_NOPE + QK_ROPE
V_HEAD = 64
Q_LORA = 256
KV_LORA = 128
ROPE_THETA = 10000.0
ATTN_WIDTH = N_HEADS * V_HEAD
SSM_WIDTH = 256
SSM_GROUP = 16
SSM_GROUPS = SSM_WIDTH // SSM_GROUP
SSM_STATE = 64
SSM_FLAT = SSM_GROUPS * SSM_STATE
CONV_WIDTH = 256
CONV_K = 31
N_EGROUPS = 4
EXP_PER_GROUP = 8
N_EXPERTS = N_EGROUPS * EXP_PER_GROUP
D_FF_E = 512

LANES = 128
SUBLANES = 8
HEAD_PAD = LANES
IN_PROJ_PAD = Q_LORA + KV_LORA + SSM_WIDTH + 2 * CONV_WIDTH + 2 * LANES
CONV_HALO = 32
VMEM_LIMIT = 48 * 1024 * 1024

TM_PROJ = 512
TQ_ATTN = 256
TC_SSM = 128
SSM_PARTS = 2
CONV_SUB = 128
TM_OUT = 512
TM_MOVE = 256
RUN_ROWS = SUBLANES
RUN_SLOTS = 2 * N_EXPERTS + 2 * TM_MOVE // RUN_ROWS
RUN_GROUP = 8
SC_CORES, SC_SUBCORES = 2, 16
SC_CHUNK = 64
SC_BUFFERS = 3
TB_EXP = 512
EXPERT_TAIL_DIVISORS = (1, 2, 4)
FF_CHUNK = 256

BF16 = jnp.bfloat16
F32 = jnp.float32
U32 = jnp.uint32
D_PACK = D_MODEL // 2


def _pack_rows(v):
    bits = lax.bitcast_convert_type(v.astype(BF16).astype(F32), U32)
    half = v.shape[1] // 2
    return bits[:, 0:half] | (bits[:, half:] >> 16)


def _unpack_rows(w):
    hi = lax.bitcast_convert_type(w & jnp.uint32(0xFFFF0000), F32)
    lo = lax.bitcast_convert_type(w << 16, F32)
    return jnp.concatenate([hi, lo], axis=1).astype(BF16)


def _rms(x, w):
    return x * lax.rsqrt(jnp.mean(x * x, axis=-1, keepdims=True) + EPS) * w


def _params(*sem):
    return pltpu.CompilerParams(dimension_semantics=sem, vmem_limit_bytes=VMEM_LIMIT)


def _in_proj_kernel(x_ref, rope_ref, mixn_ref, win_ref, qan_ref, wuq_ref, kvan_ref, wuk_ref, wuv_ref,
                    qn_ref, kn_ref, q_ref, k_ref, v_ref, u_ref, cu_ref):
    x = x_ref[0]
    h = _rms(x, mixn_ref[0]).astype(BF16)
    head_cols = Q_LORA + KV_LORA + 2 * LANES
    proj = jnp.dot(h, win_ref[0, :, 0:head_cols], preferred_element_type=F32)
    o = 0
    c_q = proj[:, o:o + Q_LORA]; o += Q_LORA
    c_kv = proj[:, o:o + KV_LORA]; o += KV_LORA
    k_pe = proj[:, o:o + LANES]; o += LANES
    k_pe_sw = proj[:, o:o + LANES]

    width = N_HEADS * HEAD_PAD
    q2 = jnp.dot(_rms(c_q, qan_ref[0]).astype(BF16), wuq_ref[0], preferred_element_type=F32)
    q, q_sw = q2[:, 0:width], q2[:, width:2 * width]
    ckv_n = _rms(c_kv, kvan_ref[0]).astype(BF16)
    kn = jnp.dot(ckv_n, wuk_ref[0], preferred_element_type=F32)
    v_ref[0] = jnp.dot(ckv_n, wuv_ref[0], preferred_element_type=F32).astype(BF16)

    rope = rope_ref[0]
    half = QK_ROPE // 2
    lane = lax.broadcasted_iota(jnp.int32, rope.shape, 1)
    lo = (lane >= QK_NOPE) & (lane < QK_NOPE + half)
    hi = (lane >= QK_NOPE + half) & (lane < QK_HEAD)
    cos_t = jnp.where(lane < QK_NOPE, 1.0, jnp.where(lo, rope, jnp.where(hi, pltpu.roll(rope, half, 1), 0.0)))
    sin_t = jnp.where(lo, -pltpu.roll(rope, LANES - half, 1), jnp.where(hi, rope, 0.0))
    a_q, b_q = qn_ref[0, 0:1, :] * cos_t, qn_ref[0, 1:2, :] * sin_t
    a_k = kn_ref[0, 0:1, :] * cos_t
    k_sw_term = k_pe_sw * (kn_ref[0, 1:2, :] * sin_t)

    def inv_rms(y):
        return lax.rsqrt(jnp.sum(y * y, axis=-1, keepdims=True) * (1.0 / QK_HEAD) + EPS)

    def rest(j):
        cols = slice(head_cols + j * SSM_WIDTH, head_cols + (j + 1) * SSM_WIDTH)
        return jnp.dot(h, win_ref[0, :, cols], preferred_element_type=F32)

    for hd in range(N_HEADS):
        if hd == 0:
            u_ref[...] = rest(0).astype(BF16)
        elif hd == 4:
            cu_ref[0] = (rest(1) * jax.nn.sigmoid(rest(2))).astype(BF16)
        sl = slice(hd * HEAD_PAD, (hd + 1) * HEAD_PAD)
        yq = q[:, sl]
        q_ref[0, hd] = ((yq * a_q + q_sw[:, sl] * b_q) * inv_rms(yq)).astype(BF16)
        yk = kn[:, sl] + k_pe
        k_ref[0, hd] = ((yk * a_k + k_sw_term) * inv_rms(yk)).astype(BF16)


def _in_proj(x, rope, mixn, win, qan, wuq, kvan, wuk, wuv, qn, kn, layer):
    B, L, D = x.shape
    tm = min(TM_PROJ, L)
    full = lambda shape: pl.BlockSpec((1,) + shape, lambda b, t: (layer,) + (0,) * len(shape))
    return pl.pallas_call(
        _in_proj_kernel,
        grid=(B, L // tm),
        in_specs=[
            pl.BlockSpec((1, tm, D), lambda b, t: (b, t, 0)),
            pl.BlockSpec((1, tm, LANES), lambda b, t: (b, t, 0)),
            full((1, D)), full((D, IN_PROJ_PAD)), full((1, Q_LORA)), full((Q_LORA, 2 * N_HEADS * HEAD_PAD)),
            full((1, KV_LORA)), full((KV_LORA, N_HEADS * HEAD_PAD)), full((KV_LORA, ATTN_WIDTH)),
            full((2, HEAD_PAD)), full((2, HEAD_PAD)),
        ],
        out_specs=[
            pl.BlockSpec((1, N_HEADS, tm, HEAD_PAD), lambda b, t: (b, 0, t, 0)),
            pl.BlockSpec((1, N_HEADS, tm, HEAD_PAD), lambda b, t: (b, 0, t, 0)),
            pl.BlockSpec((1, tm, ATTN_WIDTH), lambda b, t: (b, t, 0)),
            pl.BlockSpec((tm, SSM_WIDTH), lambda b, t: (t, b)),
            pl.BlockSpec((1, tm, CONV_WIDTH), lambda b, t: (b, t, 0)),
        ],
        out_shape=[
            jax.ShapeDtypeStruct((B, N_HEADS, L, HEAD_PAD), BF16),
            jax.ShapeDtypeStruct((B, N_HEADS, L, HEAD_PAD), BF16),
            jax.ShapeDtypeStruct((B, L, ATTN_WIDTH), BF16),
            jax.ShapeDtypeStruct((L, B * SSM_WIDTH), BF16),
            jax.ShapeDtypeStruct((B, L, CONV_WIDTH), BF16),
        ],
        compiler_params=_params("parallel", "parallel"),
    )(x, rope, mixn, win, qan, wuq, kvan, wuk, wuv, qn, kn)


def _attention_kernel(q_ref, k_ref, v_ref, o_ref, vext_ref, *, seq, tq):
    pair = 2 * V_HEAD
    vext_ref[:, 0:pair] = v_ref[0]
    vext_ref[:, pair:] = jnp.ones((seq, LANES), BF16)
    row_chunk = lax.broadcasted_iota(jnp.int32, (tq, tq), 0) // CHUNK
    col_chunk = lax.broadcasted_iota(jnp.int32, (tq, tq), 1) // CHUNK
    visible = col_chunk <= row_chunk
    neg = jnp.finfo(F32).min
    nt = (((1,), (1,)), ((), ()))
    low_lanes = lax.broadcasted_iota(jnp.int32, (tq, pair), 1) < V_HEAD

    def scores(i, hh):
        q0 = i * tq
        qb = q_ref[0, hh, q0:q0 + tq, :]
        s_d = lax.dot_general(qb, k_ref[0, hh, q0:q0 + tq, :], nt, preferred_element_type=F32)
        s_d = jnp.where(visible, s_d, neg)
        s_l = lax.dot_general(qb, k_ref[0, hh, 0:q0, :], nt, preferred_element_type=F32) if i > 0 else None
        return s_d, s_l

    def weighted_values(i, s_d, s_l):
        q0 = i * tq
        m = jnp.max(s_d, axis=-1, keepdims=True)
        if s_l is not None:
            m = jnp.maximum(m, jnp.max(s_l, axis=-1, keepdims=True))
        acc = jnp.dot(jnp.exp(s_d - m).astype(BF16), vext_ref[q0:q0 + tq, :], preferred_element_type=F32)
        if s_l is not None:
            acc = acc + jnp.dot(jnp.exp(s_l - m).astype(BF16), vext_ref[0:q0, :], preferred_element_type=F32)
        return acc[:, 0:pair] / acc[:, pair:]

    nq = seq // tq
    order = [x for p in zip(reversed(range(nq)), range(nq)) for x in p][:nq]
    chains = [(i, hh) for i in order for hh in range(2)]
    pending = scores(*chains[0])
    outs = {}
    for n, (i, hh) in enumerate(chains):
        upcoming = scores(*chains[n + 1]) if n + 1 < len(chains) else None
        outs[hh] = weighted_values(i, *pending)
        pending = upcoming
        if hh == 1:
            o_ref[0, i * tq:(i + 1) * tq, :] = jnp.where(low_lanes, outs[0], outs[1]).astype(BF16)


def _attention(q, k, v):
    B, H, L, _ = q.shape
    tq = min(TQ_ATTN, L)
    return pl.pallas_call(
        functools.partial(_attention_kernel, seq=L, tq=tq),
        grid=(B, H // 2),
        in_specs=[
            pl.BlockSpec((1, 2, L, HEAD_PAD), lambda b, p: (b, p, 0, 0)),
            pl.BlockSpec((1, 2, L, HEAD_PAD), lambda b, p: (b, p, 0, 0)),
            pl.BlockSpec((1, L, 2 * V_HEAD), lambda b, p: (b, 0, p)),
        ],
        out_specs=pl.BlockSpec((1, L, 2 * V_HEAD), lambda b, p: (b, 0, p)),
        out_shape=jax.ShapeDtypeStruct((B, L, ATTN_WIDTH), BF16),
        scratch_shapes=[pltpu.VMEM((L, 2 * V_HEAD + LANES), BF16)],
        compiler_params=_params("parallel", "parallel"),
    )(q, k, v)


def _ssm_kernel(u_ref, bbd_ref, lam_ref, cbd_ref, d_ref, wglu_ref, g_ref, o_ref, state_ref, in_s, out_s, *bu_refs,
                batch, tc):
    @pl.when(pl.program_id(0) == 0)
    def _():
        state_ref[...] = jnp.zeros_like(state_ref)

    steps = tc // SSM_PARTS
    halves = tuple((bu, slice(p * steps * batch, (p + 1) * steps * batch)) for p, bu in enumerate(bu_refs))
    for b in range(batch):
        for hl in range(SSM_WIDTH // LANES):
            lanes = slice(b * SSM_WIDTH + hl * LANES, b * SSM_WIDTH + (hl + 1) * LANES)
            in_s[hl, pl.ds(b, tc, stride=batch), :] = u_ref[:, lanes].astype(F32)

    def u_rows(rows):
        return jnp.concatenate([in_s[hl, rows, :] for hl in range(SSM_WIDTH // LANES)], axis=1)

    for bu_ref, rows in halves:
        bu_ref[...] = jnp.dot(u_rows(rows).astype(BF16), bbd_ref[0], preferred_element_type=F32)
    lam_re = jnp.broadcast_to(lam_ref[0, 0:1, :], (batch, SSM_FLAT))
    lam_im = jnp.broadcast_to(lam_ref[0, 1:2, :], (batch, SSM_FLAT))

    carry = (state_ref[:, 0:SSM_FLAT], state_ref[:, SSM_FLAT:2 * SSM_FLAT])
    for bu_ref, rows in halves:
        def step(t, xs, bu_ref=bu_ref):
            xr, xi = xs
            at = pl.ds(pl.multiple_of(t * batch, batch), batch)
            nr = lam_re * xr - lam_im * xi + bu_ref[at, 0:SSM_FLAT]
            ni = lam_re * xi + lam_im * xr + bu_ref[at, SSM_FLAT:2 * SSM_FLAT]
            bu_ref[at, 0:SSM_FLAT] = nr
            bu_ref[at, SSM_FLAT:2 * SSM_FLAT] = ni
            return nr, ni

        carry = lax.fori_loop(0, steps, step, carry, unroll=True)
        y = jnp.dot(bu_ref[...].astype(BF16), cbd_ref[0], preferred_element_type=F32)
        y = y + d_ref[0] * u_rows(rows)
        z = jax.nn.gelu(y)
        gate = jax.nn.sigmoid(jnp.dot(z.astype(BF16), wglu_ref[0], preferred_element_type=F32))
        res = _rms(z * gate, g_ref[0])
        for hl in range(SSM_WIDTH // LANES):
            out_s[hl, rows, :] = res[:, hl * LANES:(hl + 1) * LANES]
    state_ref[:, 0:SSM_FLAT] = carry[0]
    state_ref[:, SSM_FLAT:2 * SSM_FLAT] = carry[1]
    for b in range(batch):
        for hl in range(SSM_WIDTH // LANES):
            lanes = slice(b * SSM_WIDTH + hl * LANES, b * SSM_WIDTH + (hl + 1) * LANES)
            o_ref[:, lanes] = out_s[hl, pl.ds(b, tc, stride=batch), :].astype(BF16)


def _ssm(u, bbd, lam, cbd, d, wglu, g, batch, layer):
    seq = u.shape[0]
    tc = min(TC_SSM, seq)
    blk = tc * batch
    full = lambda shape: pl.BlockSpec((1,) + shape, lambda t: (layer,) + (0,) * len(shape))
    return pl.pallas_call(
        functools.partial(_ssm_kernel, batch=batch, tc=tc),
        grid=(seq // tc,),
        in_specs=[
            pl.BlockSpec((tc, batch * SSM_WIDTH), lambda t: (t, 0)),
            full((SSM_WIDTH, 2 * SSM_FLAT)), full((2, SSM_FLAT)), full((2 * SSM_FLAT, SSM_WIDTH)),
            full((1, SSM_WIDTH)), full((SSM_WIDTH, SSM_WIDTH)), full((1, SSM_WIDTH)),
        ],
        out_specs=pl.BlockSpec((tc, batch * SSM_WIDTH), lambda t: (t, 0)),
        out_shape=jax.ShapeDtypeStruct((seq, batch * SSM_WIDTH), BF16),
        scratch_shapes=[pltpu.VMEM((batch, 2 * SSM_FLAT), F32), pltpu.VMEM((SSM_WIDTH // LANES, blk, LANES), F32),
                        pltpu.VMEM((SSM_WIDTH // LANES, blk, LANES), F32)]
        + [pltpu.VMEM((blk // SSM_PARTS, 2 * SSM_FLAT), F32)] * SSM_PARTS,
        compiler_params=_params("arbitrary"),
    )(u, bbd, lam, cbd, d, wglu, g)


def _conv_rows(upad_ref, r0, w_ref, b_ref, lnw_ref, lnb_ref, gn_ref):
    base = CONV_HALO - (CONV_K - 1)
    acc = jnp.zeros((CONV_SUB, CONV_WIDTH), F32)
    for off in range(SUBLANES):
        n = CONV_SUB + (SUBLANES if off else 0)
        part = None
        for a8 in range(0, CONV_HALO + 1, SUBLANES):
            kk = a8 + off - base
            if 0 <= kk < CONV_K:
                term = w_ref[0, kk:kk + 1, :] * upad_ref[r0 + a8:r0 + a8 + n, :]
                part = term if part is None else part + term
        acc = acc + part[off:off + CONV_SUB]
    y = acc + b_ref[0]
    mu = jnp.mean(y, axis=-1, keepdims=True)
    var = jnp.mean(jnp.square(y - mu), axis=-1, keepdims=True)
    y = (y - mu) * lax.rsqrt(var + LN_EPS) * lnw_ref[0] + lnb_ref[0]
    y = y * jax.nn.sigmoid(y)
    return _rms(y, gn_ref[0])


ROUTE_E1, ROUTE_E2, ROUTE_R1, ROUTE_R2, ROUTE_G1, ROUTE_G2 = range(6)
ROUTE_ROWS = 8
GRP_LANE0 = N_EXPERTS


def _out_router_kernel(x_ref, ya_ref, ys_ref, cu_ref, ga_ref, wo_ref, fn_ref, wr_ref, br_ref,
                       cw_ref, cb_ref, clnw_ref, clnb_ref, cgn_ref,
                       x1_ref, h2_ref, route_ref, route_t_ref, cnt_ref, sub_ref,
                       carry_ref, tri_ref, upad_ref, yc_ref, *, tm, per_b):
    @pl.when(pl.program_id(0) == 0)
    def _():
        carry_ref[...] = jnp.zeros_like(carry_ref)
        rr = lax.broadcasted_iota(jnp.int32, (tm, tm), 0)
        cc = lax.broadcasted_iota(jnp.int32, (tm, tm), 1)
        tri_ref[...] = (cc < rr).astype(BF16)

    first = pl.program_id(0) % per_b == 0
    upad_ref[0:CONV_HALO, :] = jnp.where(first, 0.0, upad_ref[tm:tm + CONV_HALO, :])
    upad_ref[CONV_HALO:, :] = cu_ref[...].astype(F32)
    ya = _rms(ya_ref[...].astype(F32), ga_ref[0]).astype(BF16)
    acc = jnp.dot(ya, wo_ref[0, 0:ATTN_WIDTH, :], preferred_element_type=F32)
    acc += jnp.dot(ys_ref[...], wo_ref[0, ATTN_WIDTH:ATTN_WIDTH + SSM_WIDTH, :], preferred_element_type=F32)
    for r0 in range(0, tm, CONV_SUB):
        yc_ref[r0:r0 + CONV_SUB, :] = _conv_rows(upad_ref, r0, cw_ref, cb_ref, clnw_ref, clnb_ref, cgn_ref).astype(BF16)
    acc += jnp.dot(yc_ref[...], wo_ref[0, ATTN_WIDTH + SSM_WIDTH:, :], preferred_element_type=F32)
    x1 = x_ref[...] + acc
    x1_ref[...] = x1
    h2 = _rms(x1, fn_ref[0])
    h2_ref[...] = _pack_rows(h2)

    h_hi = h2.astype(BF16)
    h_lo = (h2 - h_hi.astype(F32)).astype(BF16)
    part = jnp.dot(h_hi, wr_ref[0], preferred_element_type=F32)
    logits = (part[:, 0:LANES] + part[:, LANES:2 * LANES]
              + jnp.dot(h_lo, wr_ref[0, :, 0:LANES], preferred_element_type=F32) + br_ref[0])
    lane = lax.broadcasted_iota(jnp.int32, (tm, LANES), 1)
    ninf = -jnp.inf
    big = LANES

    def first_argmax(vals, vmax):
        return jnp.min(jnp.where(vals == vmax, lane, big), axis=-1, keepdims=True)

    grp = jnp.where((lane >= GRP_LANE0) & (lane < GRP_LANE0 + N_EGROUPS), logits, ninf)
    gmax = jnp.max(grp, axis=-1, keepdims=True)
    gsel = first_argmax(grp, gmax) - GRP_LANE0
    p_grp = 1.0 / jnp.sum(jnp.exp(grp - gmax), axis=-1, keepdims=True)

    el = jnp.where((lane < N_EXPERTS) & ((lane // EXP_PER_GROUP) == gsel), logits, ninf)
    m1 = jnp.max(el, axis=-1, keepdims=True)
    e1 = first_argmax(el, m1)
    el2 = jnp.where(lane == e1, ninf, el)
    m2 = jnp.max(el2, axis=-1, keepdims=True)
    e2 = first_argmax(el2, m2)
    t2 = jnp.exp(m2 - m1)
    g1 = p_grp / (1.0 + t2)
    g2 = p_grp * t2 / (1.0 + t2)

    hit1 = lane == e1
    hit2 = lane == e2
    cnt = (hit1 | hit2).astype(F32)
    before = jnp.dot(tri_ref[...], cnt.astype(BF16), preferred_element_type=F32) + carry_ref[...]
    r1 = jnp.sum(jnp.where(hit1, before, 0.0), axis=-1, keepdims=True)
    r2 = jnp.sum(jnp.where(hit2, before, 0.0), axis=-1, keepdims=True)
    subs = [before[h * TM_MOVE:h * TM_MOVE + 1, :] for h in range(tm // TM_MOVE)]
    sub_ref[...] = jnp.concatenate(subs + [jnp.zeros((SUBLANES - len(subs), LANES), F32)], axis=0)
    carry_ref[...] += jnp.sum(cnt, axis=0, keepdims=True)
    cnt_ref[...] = carry_ref[...]

    rec = jnp.where(lane == ROUTE_E1, e1.astype(F32), 0.0)
    rec = jnp.where(lane == ROUTE_E2, e2.astype(F32), rec)
    rec = jnp.where(lane == ROUTE_R1, r1, rec)
    rec = jnp.where(lane == ROUTE_R2, r2, rec)
    rec = jnp.where(lane == ROUTE_G1, g1, rec)
    rec = jnp.where(lane == ROUTE_G2, g2, rec)
    route_ref[...] = rec
    route_t_ref[...] = rec.T[0:ROUTE_ROWS, :]


def _out_router(x, ya, ys, cu, ga, wo, fn, wr, br, conv_params, batch, layer):
    N, D = x.shape
    seq = N // batch
    tm = min(TM_OUT, seq)
    per_b = seq // tm
    full = lambda shape: pl.BlockSpec((1,) + shape, lambda i: (layer,) + (0,) * len(shape))
    whole = lambda shape: pl.BlockSpec(shape, lambda i: (0,) * len(shape))
    tile = lambda w: pl.BlockSpec((tm, w), lambda i: (i, 0))
    return pl.pallas_call(
        functools.partial(_out_router_kernel, tm=tm, per_b=per_b),
        grid=(N // tm,),
        in_specs=[
            tile(D), tile(ATTN_WIDTH),
            pl.BlockSpec((tm, SSM_WIDTH), lambda i: (i % per_b, i // per_b)),
            tile(CONV_WIDTH),
            full((1, ATTN_WIDTH)), full((D, D)), full((1, D)), full((D, 2 * LANES)), full((1, LANES)),
            full((CONV_K, CONV_WIDTH)), full((1, CONV_WIDTH)), full((1, CONV_WIDTH)), full((1, CONV_WIDTH)),
            full((1, CONV_WIDTH)),
        ],
        out_specs=[tile(D), tile(D_PACK), tile(LANES), pl.BlockSpec((ROUTE_ROWS, tm), lambda i: (0, i)),
                   whole((1, LANES)), pl.BlockSpec((SUBLANES, LANES), lambda i: (i, 0))],
        out_shape=[
            jax.ShapeDtypeStruct((N, D), F32),
            jax.ShapeDtypeStruct((N, D_PACK), U32),
            jax.ShapeDtypeStruct((N, LANES), F32),
            jax.ShapeDtypeStruct((ROUTE_ROWS, N), F32),
            jax.ShapeDtypeStruct((1, LANES), F32),
            jax.ShapeDtypeStruct((N // tm * SUBLANES, LANES), F32),
        ],
        scratch_shapes=[pltpu.VMEM((1, LANES), F32), pltpu.VMEM((tm, tm), BF16),
                        pltpu.VMEM((tm + CONV_HALO, CONV_WIDTH), F32), pltpu.VMEM((tm, CONV_WIDTH), BF16)],
        compiler_params=_params("arbitrary"),
    )(x, ya, ys, cu, ga, wo, fn, wr, br, *conv_params)


def _dispatch(dest_flat, h2, n_rows):
    N, D = h2.shape
    workers = SC_CORES * SC_SUBCORES
    per_worker = N // workers
    chunks = per_worker // SC_CHUNK
    idx = dest_flat.reshape(2, workers, chunks, SC_CHUNK).transpose(1, 0, 2, 3).reshape(workers, 2 * chunks, SC_CHUNK)
    mesh = plsc.VectorSubcoreMesh(core_axis_name="c", subcore_axis_name="s")

    @functools.partial(
        pl.kernel, mesh=mesh,
        out_type=jax.ShapeDtypeStruct((n_rows, D), h2.dtype),
        scratch_types=[pltpu.VMEM((2 * chunks, SC_CHUNK), jnp.int32), pltpu.VMEM((SC_BUFFERS, SC_CHUNK, D), h2.dtype),
                       pltpu.SemaphoreType.DMA((SC_BUFFERS,)), pltpu.SemaphoreType.DMA((SC_BUFFERS, 2))],
    )
    def scatter(h_hbm, idx_hbm, out_hbm, idx_v, rows_v, sem_in, sem_out):
        wid = lax.axis_index("s") * SC_CORES + lax.axis_index("c")
        pltpu.sync_copy(idx_hbm.at[wid], idx_v)

        def load(c):
            first = pl.multiple_of(wid * per_worker + c * SC_CHUNK, SC_CHUNK)
            return pltpu.async_copy(h_hbm.at[pl.ds(first, SC_CHUNK)], rows_v.at[c % SC_BUFFERS], sem_in.at[c % SC_BUFFERS])

        def store(c, k):
            return pltpu.async_copy(rows_v.at[c % SC_BUFFERS], out_hbm.at[idx_v.at[k * chunks + c]],
                                    sem_out.at[c % SC_BUFFERS, k])

        loads = {0: load(0)}
        stores = {}
        for c in range(chunks):
            loads[c].wait()
            reuse = c + 1 - SC_BUFFERS
            if reuse >= 0:
                for st in stores.pop(reuse):
                    st.wait()
            if c + 1 < chunks:
                loads[c + 1] = load(c + 1)
            stores[c] = (store(c, 0), store(c, 1))
        for c in sorted(stores):
            for st in stores[c]:
                st.wait()

    return scatter(h2, idx)


def _experts_kernel(blk_ref, exp_ref, valid_ref, slot_ref, next_ref, used_ref, x_ref, w1_hbm, w3_hbm, w2_hbm, y_ref,
                    w1_f, w3_f, w2_f, w13_s, w2_s, sem, *, layer):
    j = pl.program_id(0)

    def weight_copies(e, slot):
        return (pltpu.make_async_copy(w1_hbm.at[layer, e], w1_f.at[slot], sem.at[slot, 0]),
                pltpu.make_async_copy(w3_hbm.at[layer, e], w3_f.at[slot], sem.at[slot, 1]),
                pltpu.make_async_copy(w2_hbm.at[layer, e], w2_f.at[slot], sem.at[slot, 2]))

    @pl.when(j < used_ref[0])
    def _():
        e = exp_ref[j]
        slot = slot_ref[j]

        @pl.when(j == 0)
        def _():
            for cp in weight_copies(e, slot):
                cp.start()

        @pl.when((j == 0) | (e != exp_ref[jnp.maximum(j - 1, 0)]))
        def _():
            for cp in weight_copies(e, slot):
                cp.wait()
            for c in range(D_FF_E // FF_CHUNK):
                w13_s[:, 2 * FF_CHUNK * c:2 * FF_CHUNK * c + FF_CHUNK] = \
                    w1_f[slot, :, FF_CHUNK * c:FF_CHUNK * (c + 1)].astype(BF16)
                w13_s[:, 2 * FF_CHUNK * c + FF_CHUNK:2 * FF_CHUNK * (c + 1)] = \
                    w3_f[slot, :, FF_CHUNK * c:FF_CHUNK * (c + 1)].astype(BF16)
            w2_s[...] = w2_f[slot].astype(BF16)

            @pl.when(next_ref[j] >= 0)
            def _():
                for cp in weight_copies(next_ref[j], 1 - slot):
                    cp.start()

        valid = valid_ref[j]

        def ffn(rows):
            x = _unpack_rows(x_ref[0:rows, :])
            y = None
            for c in range(D_FF_E // FF_CHUNK):
                ab = jnp.dot(x, w13_s[:, 2 * FF_CHUNK * c:2 * FF_CHUNK * (c + 1)], preferred_element_type=F32)
                a = ab[:, 0:FF_CHUNK]
                hmid = (a * jax.nn.sigmoid(a) * ab[:, FF_CHUNK:]).astype(BF16)
                part = jnp.dot(hmid, w2_s[FF_CHUNK * c:FF_CHUNK * (c + 1), :], preferred_element_type=F32)
                y = part if y is None else y + part
            row = lax.broadcasted_iota(jnp.int32, (rows, 1), 0)
            y_ref[0:rows, :] = jnp.where(row < valid, _pack_rows(y), jnp.uint32(0))
            if rows < TB_EXP:
                y_ref[rows:, :] = jnp.zeros((TB_EXP - rows, D_PACK), U32)

        sizes = [TB_EXP // d for d in EXPERT_TAIL_DIVISORS]
        for n, rows in enumerate(sizes):
            below = sizes[n + 1] if n + 1 < len(sizes) else 0

            @pl.when((valid > below) & (valid <= rows) if n else valid > below)
            def _(rows=rows):
                ffn(rows)


def _experts(blk_map, blk_exp, blk_valid, blk_slot, blk_next, n_used, x_rows, w1, w3, w2, layer):
    R, D = x_rows.shape[0], D_MODEL
    nb = R // TB_EXP
    row_block = lambda j, bm, be, bv, bs, bn, nu: (bm[j], 0)
    return pl.pallas_call(
        functools.partial(_experts_kernel, layer=layer),
        grid_spec=pltpu.PrefetchScalarGridSpec(
            num_scalar_prefetch=6,
            grid=(nb,),
            in_specs=[
                pl.BlockSpec((TB_EXP, D_PACK), row_block),
                pl.BlockSpec(memory_space=pl.ANY), pl.BlockSpec(memory_space=pl.ANY), pl.BlockSpec(memory_space=pl.ANY),
            ],
            out_specs=pl.BlockSpec((TB_EXP, D_PACK), row_block),
            scratch_shapes=[
                pltpu.VMEM((2, D, D_FF_E), F32), pltpu.VMEM((2, D, D_FF_E), F32), pltpu.VMEM((2, D_FF_E, D), F32),
                pltpu.VMEM((D, 2 * D_FF_E), BF16), pltpu.VMEM((D_FF_E, D), BF16),
                pltpu.SemaphoreType.DMA((2, 3)),
            ],
        ),
        out_shape=jax.ShapeDtypeStruct((R, D_PACK), U32),
        compiler_params=_params("arbitrary"),
    )(blk_map, blk_exp, blk_valid, blk_slot, blk_next, n_used, x_rows, w1, w3, w2)


def _combine_kernel(src_ref, ngroup_ref, x1_ref, route_ref, off_ref, rows_ref, o_ref, buf, sem, *, tm):
    i = pl.program_id(0)
    cur = i % 2

    def slot_copy(tile, half, s):
        src = pl.multiple_of(src_ref[tile * RUN_SLOTS + s], RUN_ROWS)
        dst = buf.at[half, pl.ds(pl.multiple_of(s * RUN_ROWS, RUN_ROWS), RUN_ROWS)]
        return pltpu.make_async_copy(rows_ref.at[pl.ds(src, RUN_ROWS)], dst, sem.at[half])

    def fetch(tile, half):
        def issue(g, c):
            for u in range(RUN_GROUP):
                slot_copy(tile, half, g * RUN_GROUP + u).start()
            return c
        lax.fori_loop(0, ngroup_ref[tile], issue, 0)

    def wait(g, c):
        rows = RUN_GROUP * RUN_ROWS
        pltpu.make_async_copy(rows_ref.at[pl.ds(0, rows)], buf.at[cur, pl.ds(0, rows)], sem.at[cur]).wait()
        return c

    @pl.when(i == 0)
    def _():
        buf[...] = jnp.zeros_like(buf)
        fetch(0, 0)

    @pl.when(i + 1 < pl.num_programs(0))
    def _():
        fetch(i + 1, 1 - cur)

    route = route_ref[...]
    lane = lax.broadcasted_iota(jnp.int32, (tm, LANES), 1).astype(F32)
    off = off_ref[0]

    def buf_row(e_lane, r_lane):
        e = route[:, e_lane:e_lane + 1]
        base = jnp.sum(jnp.where(lane == e, off, 0.0), axis=-1, keepdims=True)
        return (base + route[:, r_lane:r_lane + 1]).astype(jnp.int32)

    col = lax.broadcasted_iota(jnp.int32, (tm, RUN_SLOTS * RUN_ROWS), 1)
    pick = (jnp.where(col == buf_row(ROUTE_E1, ROUTE_R1), route[:, ROUTE_G1:ROUTE_G1 + 1], 0.0)
            + jnp.where(col == buf_row(ROUTE_E2, ROUTE_R2), route[:, ROUTE_G2:ROUTE_G2 + 1], 0.0))
    lax.fori_loop(0, ngroup_ref[i], wait, 0)
    o_ref[...] = x1_ref[...] + jnp.dot(pick.astype(BF16), _unpack_rows(buf[cur]), preferred_element_type=F32)


def _combine(src, n_groups, x1, route, off, y_rows):
    N, D = x1.shape
    tm = min(TM_MOVE, N)
    return pl.pallas_call(
        functools.partial(_combine_kernel, tm=tm),
        grid_spec=pltpu.PrefetchScalarGridSpec(
            num_scalar_prefetch=2,
            grid=(N // tm,),
            in_specs=[
                pl.BlockSpec((tm, D), lambda i, s, n: (i, 0)),
                pl.BlockSpec((tm, LANES), lambda i, s, n: (i, 0)),
                pl.BlockSpec((1, 1, LANES), lambda i, s, n: (i, 0, 0)),
                pl.BlockSpec(memory_space=pl.ANY),
            ],
            out_specs=pl.BlockSpec((tm, D), lambda i, s, n: (i, 0)),
            scratch_shapes=[pltpu.VMEM((2, RUN_SLOTS * RUN_ROWS, D_PACK), U32), pltpu.SemaphoreType.DMA((2,))],
        ),
        out_shape=jax.ShapeDtypeStruct((N, D), F32),
        compiler_params=_params("arbitrary"),
    )(src, n_groups, x1, route, off, y_rows)


def _moe(x1, h2, route, route_t, counts, sub_carry, w1, w3, w2, layer):
    N, _ = x1.shape
    tm = min(TM_MOVE, N)
    n_tiles = N // tm
    experts = jnp.arange(N_EXPERTS, dtype=jnp.int32)
    nb = (2 * N + N_EXPERTS * (TB_EXP - 1)) // TB_EXP + 1
    e_id = route_t[ROUTE_E1:ROUTE_E2 + 1].astype(jnp.int32)
    rank = route_t[ROUTE_R1:ROUTE_R2 + 1].astype(jnp.int32)
    cnt = counts[0, :N_EXPERTS].astype(jnp.int32)
    padded = (cnt + TB_EXP - 1) // TB_EXP * TB_EXP
    pad_end = jnp.cumsum(padded)
    pad_start = pad_end - padded
    e_flat = e_id.reshape(1, 2 * N)
    dest = jnp.sum(jnp.where(e_flat == experts[:, None], pad_start[:, None], 0), axis=0) + rank.reshape(2 * N)
    x_rows = _dispatch(dest, h2, nb * TB_EXP)
    n_used = pad_end[-1] // TB_EXP
    blk_map = jnp.minimum(jnp.arange(nb, dtype=jnp.int32), jnp.maximum(n_used - 1, 0))
    blk_exp = jnp.sum(pad_end[None, :] <= (blk_map * TB_EXP)[:, None], axis=1).astype(jnp.int32)
    blk_exp = jnp.minimum(blk_exp, N_EXPERTS - 1)
    is_blk_e = blk_exp[:, None] == experts[None, :]
    blk_valid = jnp.clip(jnp.sum(jnp.where(is_blk_e, (pad_start + cnt)[None, :], 0), axis=1) - blk_map * TB_EXP,
                         0, TB_EXP).astype(jnp.int32)
    has_rows = cnt > 0
    slot_of_e = (jnp.cumsum(has_rows.astype(jnp.int32)) - 1) % 2
    later = jnp.where(has_rows, experts, N_EXPERTS)
    next_of_e = lax.cummin(jnp.concatenate([later[1:], jnp.full((1,), N_EXPERTS, jnp.int32)]), reverse=True)
    next_of_e = jnp.where(next_of_e < N_EXPERTS, next_of_e, -1)
    blk_slot = jnp.sum(jnp.where(is_blk_e, slot_of_e[None, :], 0), axis=1).astype(jnp.int32)
    blk_next = jnp.sum(jnp.where(is_blk_e, next_of_e[None, :], 0), axis=1).astype(jnp.int32)

    per_router_tile = TM_OUT // tm if N >= TM_OUT else 1
    carry = sub_carry.reshape(-1, SUBLANES, LANES)[:, :per_router_tile, :N_EXPERTS].reshape(n_tiles, N_EXPERTS)
    carry = carry.astype(jnp.int32)
    tile_cnt = jnp.concatenate([carry[1:], cnt[None, :]], axis=0) - carry
    run_start = pad_start[None, :] + carry
    first_blk = run_start // RUN_ROWS
    nslot = jnp.where(tile_cnt > 0, (run_start + tile_cnt - 1) // RUN_ROWS - first_blk + 1, 0)
    slot_end = jnp.cumsum(nslot, axis=1)
    slot_base = slot_end - nslot
    slots = jnp.arange(RUN_SLOTS, dtype=jnp.int32)
    slot_e = jnp.minimum(jnp.sum(slot_end[:, None, :] <= slots[None, :, None], axis=2), N_EXPERTS - 1)
    is_slot_e = slot_e[:, :, None] == experts[None, None, :]
    pick = lambda tbl: jnp.sum(jnp.where(is_slot_e, tbl[:, None, :], 0), axis=2)
    src = (pick(first_blk) + slots[None, :] - pick(slot_base)) * RUN_ROWS
    src = jnp.where(slots[None, :] < slot_end[:, -1:], src, 0)
    src = jnp.clip(src, 0, nb * TB_EXP - RUN_ROWS).reshape(-1).astype(jnp.int32)
    n_groups = (slot_end[:, -1] + RUN_GROUP - 1) // RUN_GROUP
    off = (slot_base - first_blk) * RUN_ROWS + pad_start[None, :]
    off = jnp.pad(off.astype(F32), ((0, 0), (0, LANES - N_EXPERTS)))

    y_rows = _experts(blk_map, blk_exp, blk_valid, blk_slot, blk_next, n_used.reshape(1).astype(jnp.int32),
                      x_rows, w1, w3, w2, layer)
    return _combine(src, n_groups.astype(jnp.int32), x1, route, off.reshape(n_tiles, 1, LANES), y_rows)


def _pad_heads(w, width):
    lead = w.shape[:-1]
    w = w.reshape(lead + (N_HEADS, width))
    pad = [(0, 0)] * (w.ndim - 1) + [(0, HEAD_PAD - width)]
    return jnp.pad(w, pad).reshape(lead + (N_HEADS * HEAD_PAD,))


def _swap_rope(w):
    half = QK_ROPE // 2
    lo, hi = w[..., QK_NOPE:QK_NOPE + half], w[..., QK_NOPE + half:QK_HEAD]
    pad = [(0, 0)] * (w.ndim - 1)
    return jnp.pad(jnp.concatenate([hi, lo], -1), pad + [(QK_NOPE, HEAD_PAD - QK_HEAD)])


def _rope_tables(positions):
    inv_freq = ROPE_THETA ** (-jnp.arange(0, QK_ROPE, 2, dtype=F32) / QK_ROPE)
    ang = positions.astype(F32)[..., None] * inv_freq
    table = jnp.concatenate([jnp.cos(ang), jnp.sin(ang)], -1)
    return jnp.pad(table, ((0, 0), (0, 0), (QK_NOPE, LANES - QK_HEAD)))


def _ssm_params(lam_re, lam_im, b_re, b_im, c_re, c_im, log_dt):
    depth = lam_re.shape[0]
    lam = lax.complex(lam_re, lam_im)
    dt = jnp.exp(log_dt)[..., None]
    lam_bar = jnp.exp(lam * dt)
    b_bar = ((lam_bar - 1.0) / lam)[..., None] * lax.complex(b_re, b_im)
    eye = jnp.eye(SSM_GROUPS, dtype=F32)

    def in_blockdiag(m):
        return jnp.einsum("lgpc,gh->lgchp", m, eye).reshape(depth, SSM_WIDTH, SSM_FLAT)

    def out_blockdiag(m):
        return jnp.einsum("lgcp,gh->lgphc", m, eye).reshape(depth, SSM_FLAT, SSM_WIDTH)

    bbd = jnp.concatenate([in_blockdiag(jnp.real(b_bar)), in_blockdiag(jnp.imag(b_bar))], axis=2)
    cbd = jnp.concatenate([out_blockdiag(c_re), out_blockdiag(-c_im)], axis=1)
    lam_rows = jnp.stack([jnp.real(lam_bar).reshape(depth, SSM_FLAT), jnp.imag(lam_bar).reshape(depth, SSM_FLAT)],
                         axis=1)
    return bbd.astype(BF16), lam_rows, cbd.astype(BF16)


def kernel(x, positions, mix_norm, w_in, q_a_norm, w_uq, kv_a_norm, w_ukv, q_norm, k_norm, ssm_lam_re, ssm_lam_im, ssm_b_re, ssm_b_im, ssm_c_re, ssm_c_im, ssm_d, ssm_log_dt, ssm_w_glu, conv_dw_w, conv_dw_b, conv_ln_w, conv_ln_b, out_norm, w_out, ffn_norm, w_grp, b_grp, w_exp, b_exp, w1, w3, w2):
    B, L, D = x.shape
    depth = w_in.shape[0]
    rope = _rope_tables(positions)
    rows3 = lambda v: v[:, None, :]
    lane_tail = lambda v: jnp.pad(v, [(0, 0)] * (v.ndim - 1) + [(0, LANES - v.shape[-1])])
    c_q, c_kv, k_pe, u_s, c_a, c_g = jnp.split(
        w_in, [Q_LORA, Q_LORA + KV_LORA, Q_LORA + KV_LORA + QK_ROPE, Q_LORA + KV_LORA + QK_ROPE + SSM_WIDTH,
               Q_LORA + KV_LORA + QK_ROPE + SSM_WIDTH + CONV_WIDTH], axis=2)
    k_pe_full = jnp.pad(k_pe, ((0, 0), (0, 0), (QK_NOPE, 0)))
    win = jnp.concatenate([c_q, c_kv, lane_tail(k_pe_full), _swap_rope(k_pe_full), u_s, c_a, c_g], axis=2).astype(BF16)
    wkv = w_ukv.reshape(depth, KV_LORA, N_HEADS, QK_NOPE + V_HEAD)
    wuk = _pad_heads(wkv[..., :QK_NOPE].reshape(depth, KV_LORA, N_HEADS * QK_NOPE), QK_NOPE).astype(BF16)
    wuv = wkv[..., QK_NOPE:].reshape(depth, KV_LORA, ATTN_WIDTH).astype(BF16)
    wuq_sw = _swap_rope(w_uq.reshape(depth, Q_LORA, N_HEADS, QK_HEAD)).reshape(depth, Q_LORA, N_HEADS * HEAD_PAD)
    wuq = jnp.concatenate([_pad_heads(w_uq, QK_HEAD), wuq_sw], axis=2).astype(BF16)
    norm_rows = lambda w: jnp.stack([lane_tail(w), _swap_rope(w)], axis=1)
    qn, kn = norm_rows(q_norm * (QK_HEAD ** -0.5)), norm_rows(k_norm)
    bbd, lam_rows, cbd = _ssm_params(ssm_lam_re, ssm_lam_im, ssm_b_re, ssm_b_im, ssm_c_re, ssm_c_im, ssm_log_dt)
    ssm_skip = rows3(ssm_d.reshape(depth, SSM_WIDTH))
    wglu = ssm_w_glu.astype(BF16)
    g_attn = rows3(out_norm[:, :ATTN_WIDTH])
    g_ssm = rows3(out_norm[:, ATTN_WIDTH:ATTN_WIDTH + SSM_WIDTH])
    conv_params = (conv_dw_w, rows3(conv_dw_b), rows3(conv_ln_w), rows3(conv_ln_b),
                   rows3(out_norm[:, ATTN_WIDTH + SSM_WIDTH:]))
    w_route = lane_tail(jnp.concatenate([w_exp, w_grp], axis=2))
    w_route_hi = w_route.astype(BF16)
    w_route = jnp.concatenate([w_route_hi, (w_route - w_route_hi.astype(F32)).astype(BF16)], axis=2)
    b_route = rows3(lane_tail(jnp.concatenate([b_exp, b_grp], axis=1)))
    wo = w_out.astype(BF16)
    for l in range(depth):
        q, k, v, u_ssm, cu = _in_proj(x, rope, rows3(mix_norm), win, rows3(q_a_norm), wuq, rows3(kv_a_norm), wuk, wuv,
                                         qn, kn, l)
        y_attn = _attention(q, k, v)
        y_ssm = _ssm(u_ssm, bbd, lam_rows, cbd, ssm_skip, wglu, g_ssm, B, l)
        x1, h2, route, route_t, counts, sub_carry = _out_router(
            x.reshape(B * L, D), y_attn.reshape(B * L, ATTN_WIDTH), y_ssm, cu.reshape(B * L, CONV_WIDTH), g_attn, wo, rows3(ffn_norm),
            w_route, b_route, conv_params, B, l)
        x = _moe(x1, h2, route, route_t, counts, sub_carry, w1, w3, w2, l).reshape(B, L, D)
    return x
```

```python
import functools

import jax
import jax.numpy as jnp
from jax import lax
from jax.experimental import pallas as pl
from jax.experimental.pallas import tpu as pltpu
from jax.experimental.pallas import tpu_sc as plsc

D_MODEL = 1024
CHUNK = 64
EPS = 1e-6
LN_EPS = 1e-5
N_HEADS = 8
QK_NOPE = 64
QK_ROPE = 32
QK_HEAD = QK_NOPE + QK_ROPE
V_HEAD = 64
Q_LORA = 256
KV_LORA = 128
ROPE_THETA = 10000.0
ATTN_WIDTH = N_HEADS * V_HEAD
SSM_WIDTH = 256
SSM_GROUP = 16
SSM_GROUPS = SSM_WIDTH // SSM_GROUP
SSM_STATE = 64
SSM_FLAT = SSM_GROUPS * SSM_STATE
CONV_WIDTH = 256
CONV_K = 31
N_EGROUPS = 4
EXP_PER_GROUP = 8
N_EXPERTS = N_EGROUPS * EXP_PER_GROUP
D_FF_E = 512

LANES = 128
SUBLANES = 8
HEAD_PAD = LANES
IN_PROJ_PAD = Q_LORA + KV_LORA + SSM_WIDTH + 2 * CONV_WIDTH + 2 * LANES
CONV_HALO = 32
VMEM_LIMIT = 48 * 1024 * 1024

TM_PROJ = 512
TQ_ATTN = 256
TC_SSM = 128
SSM_PARTS = 2
CONV_SUB = 512
TM_OUT = 512
TM_MOVE = 256
RUN_ROWS = SUBLANES
RUN_SLOTS = 2 * N_EXPERTS + 2 * TM_MOVE // RUN_ROWS
RUN_GROUP = 8
SC_CORES, SC_SUBCORES = 2, 16
SC_CHUNK = 64
TB_EXP = 512
EXPERT_TAIL_DIVISORS = (1, 2, 4)

BF16 = jnp.bfloat16
F32 = jnp.float32
U32 = jnp.uint32
D_PACK = D_MODEL // 2


def _pack_rows(v):
    bits = lax.bitcast_convert_type(v.astype(BF16).astype(F32), U32)
    half = v.shape[1] // 2
    return bits[:, 0:half] | (bits[:, half:] >> 16)


def _unpack_rows(w):
    hi = lax.bitcast_convert_type(w & jnp.uint32(0xFFFF0000), F32)
    lo = lax.bitcast_convert_type(w << 16, F32)
    return jnp.concatenate([hi, lo], axis=1).astype(BF16)


def _rms(x, w):
    return x * lax.rsqrt(jnp.mean(x * x, axis=-1, keepdims=True) + EPS) * w


def _params(*sem):
    return pltpu.CompilerParams(dimension_semantics=sem, vmem_limit_bytes=VMEM_LIMIT)


def _in_proj_kernel(x_ref, rope_ref, mixn_ref, win_ref, qan_ref, wuq_ref, kvan_ref, wuk_ref, wuv_ref,
                    qn_ref, kn_ref, q_ref, k_ref, v_ref, u_ref, cu_ref):
    x = x_ref[0]
    h = _rms(x, mixn_ref[0]).astype(BF16)
    head_cols = Q_LORA + KV_LORA + 2 * LANES
    proj = jnp.dot(h, win_ref[0, :, 0:head_cols], preferred_element_type=F32)
    o = 0
    c_q = proj[:, o:o + Q_LORA]; o += Q_LORA
    c_kv = proj[:, o:o + KV_LORA]; o += KV_LORA
    k_pe = proj[:, o:o + LANES]; o += LANES
    k_pe_sw = proj[:, o:o + LANES]

    width = N_HEADS * HEAD_PAD
    q2 = jnp.dot(_rms(c_q, qan_ref[0]).astype(BF16), wuq_ref[0], preferred_element_type=F32)
    q, q_sw = q2[:, 0:width], q2[:, width:2 * width]
    ckv_n = _rms(c_kv, kvan_ref[0]).astype(BF16)
    kn = jnp.dot(ckv_n, wuk_ref[0], preferred_element_type=F32)
    v_ref[0] = jnp.dot(ckv_n, wuv_ref[0], preferred_element_type=F32).astype(BF16)

    rope = rope_ref[0]
    half = QK_ROPE // 2
    lane = lax.broadcasted_iota(jnp.int32, rope.shape, 1)
    lo = (lane >= QK_NOPE) & (lane < QK_NOPE + half)
    hi = (lane >= QK_NOPE + half) & (lane < QK_HEAD)
    cos_t = jnp.where(lane < QK_NOPE, 1.0, jnp.where(lo, rope, jnp.where(hi, pltpu.roll(rope, half, 1), 0.0)))
    sin_t = jnp.where(lo, -pltpu.roll(rope, LANES - half, 1), jnp.where(hi, rope, 0.0))
    a_q, b_q = qn_ref[0, 0:1, :] * cos_t, qn_ref[0, 1:2, :] * sin_t
    a_k = kn_ref[0, 0:1, :] * cos_t
    k_sw_term = k_pe_sw * (kn_ref[0, 1:2, :] * sin_t)

    def inv_rms(y):
        return lax.rsqrt(jnp.sum(y * y, axis=-1, keepdims=True) * (1.0 / QK_HEAD) + EPS)

    def rest(j):
        cols = slice(head_cols + j * SSM_WIDTH, head_cols + (j + 1) * SSM_WIDTH)
        return jnp.dot(h, win_ref[0, :, cols], preferred_element_type=F32)

    for hd in range(N_HEADS):
        if hd == 0:
            u_ref[...] = rest(0).astype(BF16)
        elif hd == 4:
            cu_ref[0] = (rest(1) * jax.nn.sigmoid(rest(2))).astype(BF16)
        sl = slice(hd * HEAD_PAD, (hd + 1) * HEAD_PAD)
        yq = q[:, sl]
        q_ref[0, hd] = ((yq * a_q + q_sw[:, sl] * b_q) * inv_rms(yq)).astype(BF16)
        yk = kn[:, sl] + k_pe
        k_ref[0, hd] = ((yk * a_k + k_sw_term) * inv_rms(yk)).astype(BF16)


def _in_proj(x, rope, mixn, win, qan, wuq, kvan, wuk, wuv, qn, kn, layer):
    B, L, D = x.shape
    tm = min(TM_PROJ, L)
    full = lambda shape: pl.BlockSpec((1,) + shape, lambda b, t: (layer,) + (0,) * len(shape))
    return pl.pallas_call(
        _in_proj_kernel,
        grid=(B, L // tm),
        in_specs=[
            pl.BlockSpec((1, tm, D), lambda b, t: (b, t, 0)),
            pl.BlockSpec((1, tm, LANES), lambda b, t: (b, t, 0)),
            full((1, D)), full((D, IN_PROJ_PAD)), full((1, Q_LORA)), full((Q_LORA, 2 * N_HEADS * HEAD_PAD)),
            full((1, KV_LORA)), full((KV_LORA, N_HEADS * HEAD_PAD)), full((KV_LORA, ATTN_WIDTH)),
            full((2, HEAD_PAD)), full((2, HEAD_PAD)),
        ],
        out_specs=[
            pl.BlockSpec((1, N_HEADS, tm, HEAD_PAD), lambda b, t: (b, 0, t, 0)),
            pl.BlockSpec((1, N_HEADS, tm, HEAD_PAD), lambda b, t: (b, 0, t, 0)),
            pl.BlockSpec((1, tm, ATTN_WIDTH), lambda b, t: (b, t, 0)),
            pl.BlockSpec((tm, SSM_WIDTH), lambda b, t: (t, b)),
            pl.BlockSpec((1, tm, CONV_WIDTH), lambda b, t: (b, t, 0)),
        ],
        out_shape=[
            jax.ShapeDtypeStruct((B, N_HEADS, L, HEAD_PAD), BF16),
            jax.ShapeDtypeStruct((B, N_HEADS, L, HEAD_PAD), BF16),
            jax.ShapeDtypeStruct((B, L, ATTN_WIDTH), BF16),
            jax.ShapeDtypeStruct((L, B * SSM_WIDTH), BF16),
            jax.ShapeDtypeStruct((B, L, CONV_WIDTH), BF16),
        ],
        compiler_params=_params("parallel", "parallel"),
    )(x, rope, mixn, win, qan, wuq, kvan, wuk, wuv, qn, kn)


def _attention_kernel(q_ref, k_ref, v_ref, o_ref, vext_ref, *, seq, tq):
    pair = 2 * V_HEAD
    vext_ref[:, 0:pair] = v_ref[0]
    vext_ref[:, pair:] = jnp.ones((seq, LANES), BF16)
    row_chunk = lax.broadcasted_iota(jnp.int32, (tq, tq), 0) // CHUNK
    col_chunk = lax.broadcasted_iota(jnp.int32, (tq, tq), 1) // CHUNK
    visible = col_chunk <= row_chunk
    neg = jnp.finfo(F32).min
    nt = (((1,), (1,)), ((), ()))
    low_lanes = lax.broadcasted_iota(jnp.int32, (tq, pair), 1) < V_HEAD

    def scores(i, hh):
        q0 = i * tq
        qb = q_ref[0, hh, q0:q0 + tq, :]
        s_d = lax.dot_general(qb, k_ref[0, hh, q0:q0 + tq, :], nt, preferred_element_type=F32)
        s_d = jnp.where(visible, s_d, neg)
        s_l = lax.dot_general(qb, k_ref[0, hh, 0:q0, :], nt, preferred_element_type=F32) if i > 0 else None
        return s_d, s_l

    def weighted_values(i, s_d, s_l):
        q0 = i * tq
        m = jnp.max(s_d, axis=-1, keepdims=True)
        if s_l is not None:
            m = jnp.maximum(m, jnp.max(s_l, axis=-1, keepdims=True))
        acc = jnp.dot(jnp.exp(s_d - m).astype(BF16), vext_ref[q0:q0 + tq, :], preferred_element_type=F32)
        if s_l is not None:
            acc = acc + jnp.dot(jnp.exp(s_l - m).astype(BF16), vext_ref[0:q0, :], preferred_element_type=F32)
        return acc[:, 0:pair] / acc[:, pair:]

    nq = seq // tq
    order = [x for p in zip(reversed(range(nq)), range(nq)) for x in p][:nq]
    chains = [(i, hh) for i in order for hh in range(2)]
    pending = scores(*chains[0])
    outs = {}
    for n, (i, hh) in enumerate(chains):
        upcoming = scores(*chains[n + 1]) if n + 1 < len(chains) else None
        outs[hh] = weighted_values(i, *pending)
        pending = upcoming
        if hh == 1:
            o_ref[0, i * tq:(i + 1) * tq, :] = jnp.where(low_lanes, outs[0], outs[1]).astype(BF16)


def _attention(q, k, v):
    B, H, L, _ = q.shape
    tq = min(TQ_ATTN, L)
    return pl.pallas_call(
        functools.partial(_attention_kernel, seq=L, tq=tq),
        grid=(B, H // 2),
        in_specs=[
            pl.BlockSpec((1, 2, L, HEAD_PAD), lambda b, p: (b, p, 0, 0)),
            pl.BlockSpec((1, 2, L, HEAD_PAD), lambda b, p: (b, p, 0, 0)),
            pl.BlockSpec((1, L, 2 * V_HEAD), lambda b, p: (b, 0, p)),
        ],
        out_specs=pl.BlockSpec((1, L, 2 * V_HEAD), lambda b, p: (b, 0, p)),
        out_shape=jax.ShapeDtypeStruct((B, L, ATTN_WIDTH), BF16),
        scratch_shapes=[pltpu.VMEM((L, 2 * V_HEAD + LANES), BF16)],
        compiler_params=_params("parallel", "parallel"),
    )(q, k, v)


def _ssm_kernel(u_ref, bbd_ref, lam_ref, cbd_ref, d_ref, wglu_ref, g_ref, o_ref, state_ref, in_s, out_s, *bu_refs,
                batch, tc):
    @pl.when(pl.program_id(0) == 0)
    def _():
        state_ref[...] = jnp.zeros_like(state_ref)

    steps = tc // SSM_PARTS
    halves = tuple((bu, slice(p * steps * batch, (p + 1) * steps * batch)) for p, bu in enumerate(bu_refs))
    for b in range(batch):
        for hl in range(SSM_WIDTH // LANES):
            lanes = slice(b * SSM_WIDTH + hl * LANES, b * SSM_WIDTH + (hl + 1) * LANES)
            in_s[hl, pl.ds(b, tc, stride=batch), :] = u_ref[:, lanes].astype(F32)

    def u_rows(rows):
        return jnp.concatenate([in_s[hl, rows, :] for hl in range(SSM_WIDTH // LANES)], axis=1)

    for bu_ref, rows in halves:
        bu_ref[...] = jnp.dot(u_rows(rows).astype(BF16), bbd_ref[0], preferred_element_type=F32)
    lam_re = jnp.broadcast_to(lam_ref[0, 0:1, :], (batch, SSM_FLAT))
    lam_im = jnp.broadcast_to(lam_ref[0, 1:2, :], (batch, SSM_FLAT))

    carry = (state_ref[:, 0:SSM_FLAT], state_ref[:, SSM_FLAT:2 * SSM_FLAT])
    for bu_ref, rows in halves:
        def step(t, xs, bu_ref=bu_ref):
            xr, xi = xs
            at = pl.ds(pl.multiple_of(t * batch, batch), batch)
            nr = lam_re * xr - lam_im * xi + bu_ref[at, 0:SSM_FLAT]
            ni = lam_re * xi + lam_im * xr + bu_ref[at, SSM_FLAT:2 * SSM_FLAT]
            bu_ref[at, 0:SSM_FLAT] = nr
            bu_ref[at, SSM_FLAT:2 * SSM_FLAT] = ni
            return nr, ni

        carry = lax.fori_loop(0, steps, step, carry, unroll=True)
        y = jnp.dot(bu_ref[...].astype(BF16), cbd_ref[0], preferred_element_type=F32)
        y = y + d_ref[0] * u_rows(rows)
        z = jax.nn.gelu(y)
        gate = jax.nn.sigmoid(jnp.dot(z.astype(BF16), wglu_ref[0], preferred_element_type=F32))
        res = _rms(z * gate, g_ref[0])
        for hl in range(SSM_WIDTH // LANES):
            out_s[hl, rows, :] = res[:, hl * LANES:(hl + 1) * LANES]
    state_ref[:, 0:SSM_FLAT] = carry[0]
    state_ref[:, SSM_FLAT:2 * SSM_FLAT] = carry[1]
    for b in range(batch):
        for hl in range(SSM_WIDTH // LANES):
            lanes = slice(b * SSM_WIDTH + hl * LANES, b * SSM_WIDTH + (hl + 1) * LANES)
            o_ref[:, lanes] = out_s[hl, pl.ds(b, tc, stride=batch), :].astype(BF16)


def _ssm(u, bbd, lam, cbd, d, wglu, g, batch, layer):
    seq = u.shape[0]
    tc = min(TC_SSM, seq)
    blk = tc * batch
    full = lambda shape: pl.BlockSpec((1,) + shape, lambda t: (layer,) + (0,) * len(shape))
    return pl.pallas_call(
        functools.partial(_ssm_kernel, batch=batch, tc=tc),
        grid=(seq // tc,),
        in_specs=[
            pl.BlockSpec((tc, batch * SSM_WIDTH), lambda t: (t, 0)),
            full((SSM_WIDTH, 2 * SSM_FLAT)), full((2, SSM_FLAT)), full((2 * SSM_FLAT, SSM_WIDTH)),
            full((1, SSM_WIDTH)), full((SSM_WIDTH, SSM_WIDTH)), full((1, SSM_WIDTH)),
        ],
        out_specs=pl.BlockSpec((tc, batch * SSM_WIDTH), lambda t: (t, 0)),
        out_shape=jax.ShapeDtypeStruct((seq, batch * SSM_WIDTH), BF16),
        scratch_shapes=[pltpu.VMEM((batch, 2 * SSM_FLAT), F32), pltpu.VMEM((SSM_WIDTH // LANES, blk, LANES), F32),
                        pltpu.VMEM((SSM_WIDTH // LANES, blk, LANES), F32)]
        + [pltpu.VMEM((blk // SSM_PARTS, 2 * SSM_FLAT), F32)] * SSM_PARTS,
        compiler_params=_params("arbitrary"),
    )(u, bbd, lam, cbd, d, wglu, g)


def _conv_rows(upad_ref, r0, w_ref, b_ref, lnw_ref, lnb_ref, gn_ref):
    base = CONV_HALO - (CONV_K - 1)
    acc = jnp.zeros((CONV_SUB, CONV_WIDTH), F32)
    for off in range(SUBLANES):
        n = CONV_SUB + (SUBLANES if off else 0)
        part = None
        for a8 in range(0, CONV_HALO + 1, SUBLANES):
            kk = a8 + off - base
            if 0 <= kk < CONV_K:
                term = w_ref[0, kk:kk + 1, :] * upad_ref[r0 + a8:r0 + a8 + n, :]
                part = term if part is None else part + term
        acc = acc + part[off:off + CONV_SUB]
    y = acc + b_ref[0]
    mu = jnp.mean(y, axis=-1, keepdims=True)
    var = jnp.mean(jnp.square(y - mu), axis=-1, keepdims=True)
    y = (y - mu) * lax.rsqrt(var + LN_EPS) * lnw_ref[0] + lnb_ref[0]
    y = y * jax.nn.sigmoid(y)
    return _rms(y, gn_ref[0])


ROUTE_E1, ROUTE_E2, ROUTE_R1, ROUTE_R2, ROUTE_G1, ROUTE_G2 = range(6)
ROUTE_ROWS = 8
GRP_LANE0 = N_EXPERTS


def _out_router_kernel(x_ref, ya_ref, ys_ref, cu_ref, ga_ref, wo_ref, fn_ref, wr_ref, br_ref,
                       cw_ref, cb_ref, clnw_ref, clnb_ref, cgn_ref,
                       x1_ref, h2_ref, route_ref, route_t_ref, cnt_ref, sub_ref,
                       carry_ref, tri_ref, upad_ref, yc_ref, *, tm, per_b):
    @pl.when(pl.program_id(0) == 0)
    def _():
        carry_ref[...] = jnp.zeros_like(carry_ref)
        rr = lax.broadcasted_iota(jnp.int32, (tm, tm), 0)
        cc = lax.broadcasted_iota(jnp.int32, (tm, tm), 1)
        tri_ref[...] = (cc < rr).astype(BF16)

    first = pl.program_id(0) % per_b == 0
    upad_ref[0:CONV_HALO, :] = jnp.where(first, 0.0, upad_ref[tm:tm + CONV_HALO, :])
    upad_ref[CONV_HALO:, :] = cu_ref[...].astype(F32)
    ya = _rms(ya_ref[...].astype(F32), ga_ref[0]).astype(BF16)
    acc = jnp.dot(ya, wo_ref[0, 0:ATTN_WIDTH, :], preferred_element_type=F32)
    acc += jnp.dot(ys_ref[...], wo_ref[0, ATTN_WIDTH:ATTN_WIDTH + SSM_WIDTH, :], preferred_element_type=F32)
    for r0 in range(0, tm, CONV_SUB):
        yc_ref[r0:r0 + CONV_SUB, :] = _conv_rows(upad_ref, r0, cw_ref, cb_ref, clnw_ref, clnb_ref, cgn_ref).astype(BF16)
    acc += jnp.dot(yc_ref[...], wo_ref[0, ATTN_WIDTH + SSM_WIDTH:, :], preferred_element_type=F32)
    x1 = x_ref[...] + acc
    x1_ref[...] = x1
    h2 = _rms(x1, fn_ref[0])
    h2_ref[...] = _pack_rows(h2)

    h_hi = h2.astype(BF16)
    h_lo = (h2 - h_hi.astype(F32)).astype(BF16)
    part = jnp.dot(h_hi, wr_ref[0], preferred_element_type=F32)
    logits = (part[:, 0:LANES] + part[:, LANES:2 * LANES]
              + jnp.dot(h_lo, wr_ref[0, :, 0:LANES], preferred_element_type=F32) + br_ref[0])
    lane = lax.broadcasted_iota(jnp.int32, (tm, LANES), 1)
    ninf = -jnp.inf
    big = LANES

    def first_argmax(vals, vmax):
        return jnp.min(jnp.where(vals == vmax, lane, big), axis=-1, keepdims=True)

    grp = jnp.where((lane >= GRP_LANE0) & (lane < GRP_LANE0 + N_EGROUPS), logits, ninf)
    gmax = jnp.max(grp, axis=-1, keepdims=True)
    gsel = first_argmax(grp, gmax) - GRP_LANE0
    p_grp = 1.0 / jnp.sum(jnp.exp(grp - gmax), axis=-1, keepdims=True)

    el = jnp.where((lane < N_EXPERTS) & ((lane // EXP_PER_GROUP) == gsel), logits, ninf)
    m1 = jnp.max(el, axis=-1, keepdims=True)
    e1 = first_argmax(el, m1)
    el2 = jnp.where(lane == e1, ninf, el)
    m2 = jnp.max(el2, axis=-1, keepdims=True)
    e2 = first_argmax(el2, m2)
    t2 = jnp.exp(m2 - m1)
    g1 = p_grp / (1.0 + t2)
    g2 = p_grp * t2 / (1.0 + t2)

    hit1 = lane == e1
    hit2 = lane == e2
    cnt = (hit1 | hit2).astype(F32)
    before = jnp.dot(tri_ref[...], cnt.astype(BF16), preferred_element_type=F32) + carry_ref[...]
    r1 = jnp.sum(jnp.where(hit1, before, 0.0), axis=-1, keepdims=True)
    r2 = jnp.sum(jnp.where(hit2, before, 0.0), axis=-1, keepdims=True)
    subs = [before[h * TM_MOVE:h * TM_MOVE + 1, :] for h in range(tm // TM_MOVE)]
    sub_ref[...] = jnp.concatenate(subs + [jnp.zeros((SUBLANES - len(subs), LANES), F32)], axis=0)
    carry_ref[...] += jnp.sum(cnt, axis=0, keepdims=True)
    cnt_ref[...] = carry_ref[...]

    rec = jnp.where(lane == ROUTE_E1, e1.astype(F32), 0.0)
    rec = jnp.where(lane == ROUTE_E2, e2.astype(F32), rec)
    rec = jnp.where(lane == ROUTE_R1, r1, rec)
    rec = jnp.where(lane == ROUTE_R2, r2, rec)
    rec = jnp.where(lane == ROUTE_G1, g1, rec)
    rec = jnp.where(lane == ROUTE_G2, g2, rec)
    route_ref[...] = rec
    route_t_ref[...] = rec.T[0:ROUTE_ROWS, :]


def _out_router(x, ya, ys, cu, ga, wo, fn, wr, br, conv_params, batch, layer):
    N, D = x.shape
    seq = N // batch
    tm = min(TM_OUT, seq)
    per_b = seq // tm
    full = lambda shape: pl.BlockSpec((1,) + shape, lambda i: (layer,) + (0,) * len(shape))
    whole = lambda shape: pl.BlockSpec(shape, lambda i: (0,) * len(shape))
    tile = lambda w: pl.BlockSpec((tm, w), lambda i: (i, 0))
    return pl.pallas_call(
        functools.partial(_out_router_kernel, tm=tm, per_b=per_b),
        grid=(N // tm,),
        in_specs=[
            tile(D), tile(ATTN_WIDTH),
            pl.BlockSpec((tm, SSM_WIDTH), lambda i: (i % per_b, i // per_b)),
            tile(CONV_WIDTH),
            full((1, ATTN_WIDTH)), full((D, D)), full((1, D)), full((D, 2 * LANES)), full((1, LANES)),
            full((CONV_K, CONV_WIDTH)), full((1, CONV_WIDTH)), full((1, CONV_WIDTH)), full((1, CONV_WIDTH)),
            full((1, CONV_WIDTH)),
        ],
        out_specs=[tile(D), tile(D_PACK), tile(LANES), pl.BlockSpec((ROUTE_ROWS, tm), lambda i: (0, i)),
                   whole((1, LANES)), pl.BlockSpec((SUBLANES, LANES), lambda i: (i, 0))],
        out_shape=[
            jax.ShapeDtypeStruct((N, D), F32),
            jax.ShapeDtypeStruct((N, D_PACK), U32),
            jax.ShapeDtypeStruct((N, LANES), F32),
            jax.ShapeDtypeStruct((ROUTE_ROWS, N), F32),
            jax.ShapeDtypeStruct((1, LANES), F32),
            jax.ShapeDtypeStruct((N // tm * SUBLANES, LANES), F32),
        ],
        scratch_shapes=[pltpu.VMEM((1, LANES), F32), pltpu.VMEM((tm, tm), BF16),
                        pltpu.VMEM((tm + CONV_HALO, CONV_WIDTH), F32), pltpu.VMEM((tm, CONV_WIDTH), BF16)],
        compiler_params=_params("arbitrary"),
    )(x, ya, ys, cu, ga, wo, fn, wr, br, *conv_params)


def _dispatch(dest_flat, h2, n_rows):
    N, D = h2.shape
    workers = SC_CORES * SC_SUBCORES
    per_worker = N // workers
    chunks = per_worker // SC_CHUNK
    idx = dest_flat.reshape(2, workers, chunks, SC_CHUNK).transpose(1, 0, 2, 3).reshape(workers, 2 * chunks, SC_CHUNK)
    mesh = plsc.VectorSubcoreMesh(core_axis_name="c", subcore_axis_name="s")

    @functools.partial(
        pl.kernel, mesh=mesh,
        out_type=jax.ShapeDtypeStruct((n_rows, D), h2.dtype),
        scratch_types=[pltpu.VMEM((2 * chunks, SC_CHUNK), jnp.int32), pltpu.VMEM((2, SC_CHUNK, D), h2.dtype),
                       pltpu.SemaphoreType.DMA((2,)), pltpu.SemaphoreType.DMA((2, 2))],
    )
    def scatter(h_hbm, idx_hbm, out_hbm, idx_v, rows_v, sem_in, sem_out):
        wid = lax.axis_index("s") * SC_CORES + lax.axis_index("c")
        pltpu.sync_copy(idx_hbm.at[wid], idx_v)

        def load(c):
            first = pl.multiple_of(wid * per_worker + c * SC_CHUNK, SC_CHUNK)
            return pltpu.async_copy(h_hbm.at[pl.ds(first, SC_CHUNK)], rows_v.at[c % 2], sem_in.at[c % 2])

        def store(c, k):
            return pltpu.async_copy(rows_v.at[c % 2], out_hbm.at[idx_v.at[k * chunks + c]], sem_out.at[c % 2, k])

        loads = {0: load(0)}
        stores = {}
        for c in range(chunks):
            loads[c].wait()
            if c >= 1:
                for st in stores[c - 1]:
                    st.wait()
            if c + 1 < chunks:
                loads[c + 1] = load(c + 1)
            stores[c] = (store(c, 0), store(c, 1))
        for st in stores[chunks - 1]:
            st.wait()

    return scatter(h2, idx)


def _experts_kernel(blk_ref, exp_ref, valid_ref, slot_ref, next_ref, used_ref, x_ref, w1_hbm, w3_hbm, w2_hbm, y_ref,
                    w1_f, w3_f, w2_f, w13_s, w2_s, sem, *, layer):
    j = pl.program_id(0)

    def weight_copies(e, slot):
        return (pltpu.make_async_copy(w1_hbm.at[layer, e], w1_f.at[slot], sem.at[slot, 0]),
                pltpu.make_async_copy(w3_hbm.at[layer, e], w3_f.at[slot], sem.at[slot, 1]),
                pltpu.make_async_copy(w2_hbm.at[layer, e], w2_f.at[slot], sem.at[slot, 2]))

    @pl.when(j < used_ref[0])
    def _():
        e = exp_ref[j]
        slot = slot_ref[j]

        @pl.when(j == 0)
        def _():
            for cp in weight_copies(e, slot):
                cp.start()

        @pl.when((j == 0) | (e != exp_ref[jnp.maximum(j - 1, 0)]))
        def _():
            for cp in weight_copies(e, slot):
                cp.wait()
            w13_s[:, 0:D_FF_E] = w1_f[slot].astype(BF16)
            w13_s[:, D_FF_E:] = w3_f[slot].astype(BF16)
            w2_s[...] = w2_f[slot].astype(BF16)

            @pl.when(next_ref[j] >= 0)
            def _():
                for cp in weight_copies(next_ref[j], 1 - slot):
                    cp.start()

        valid = valid_ref[j]

        def ffn(rows):
            x = _unpack_rows(x_ref[0:rows, :])
            ab = jnp.dot(x, w13_s[...], preferred_element_type=F32)
            a = ab[:, 0:D_FF_E]
            hmid = (a * jax.nn.sigmoid(a) * ab[:, D_FF_E:]).astype(BF16)
            y = jnp.dot(hmid, w2_s[...], preferred_element_type=F32)
            row = lax.broadcasted_iota(jnp.int32, (rows, 1), 0)
            y_ref[0:rows, :] = jnp.where(row < valid, _pack_rows(y), jnp.uint32(0))
            if rows < TB_EXP:
                y_ref[rows:, :] = jnp.zeros((TB_EXP - rows, D_PACK), U32)

        sizes = [TB_EXP // d for d in EXPERT_TAIL_DIVISORS]
        for n, rows in enumerate(sizes):
            below = sizes[n + 1] if n + 1 < len(sizes) else 0

            @pl.when((valid > below) & (valid <= rows) if n else valid > below)
            def _(rows=rows):
                ffn(rows)


def _experts(blk_map, blk_exp, blk_valid, blk_slot, blk_next, n_used, x_rows, w1, w3, w2, layer):
    R, D = x_rows.shape[0], D_MODEL
    nb = R // TB_EXP
    row_block = lambda j, bm, be, bv, bs, bn, nu: (bm[j], 0)
    return pl.pallas_call(
        functools.partial(_experts_kernel, layer=layer),
        grid_spec=pltpu.PrefetchScalarGridSpec(
            num_scalar_prefetch=6,
            grid=(nb,),
            in_specs=[
                pl.BlockSpec((TB_EXP, D_PACK), row_block),
                pl.BlockSpec(memory_space=pl.ANY), pl.BlockSpec(memory_space=pl.ANY), pl.BlockSpec(memory_space=pl.ANY),
            ],
            out_specs=pl.BlockSpec((TB_EXP, D_PACK), row_block),
            scratch_shapes=[
                pltpu.VMEM((2, D, D_FF_E), F32), pltpu.VMEM((2, D, D_FF_E), F32), pltpu.VMEM((2, D_FF_E, D), F32),
                pltpu.VMEM((D, 2 * D_FF_E), BF16), pltpu.VMEM((D_FF_E, D), BF16),
                pltpu.SemaphoreType.DMA((2, 3)),
            ],
        ),
        out_shape=jax.ShapeDtypeStruct((R, D_PACK), U32),
        compiler_params=_params("arbitrary"),
    )(blk_map, blk_exp, blk_valid, blk_slot, blk_next, n_used, x_rows, w1, w3, w2)


def _combine_kernel(src_ref, ngroup_ref, x1_ref, route_ref, off_ref, rows_ref, o_ref, buf, sem, *, tm):
    i = pl.program_id(0)
    cur = i % 2

    def slot_copy(tile, half, s):
        src = pl.multiple_of(src_ref[tile * RUN_SLOTS + s], RUN_ROWS)
        dst = buf.at[half, pl.ds(pl.multiple_of(s * RUN_ROWS, RUN_ROWS), RUN_ROWS)]
        return pltpu.make_async_copy(rows_ref.at[pl.ds(src, RUN_ROWS)], dst, sem.at[half])

    def fetch(tile, half):
        def issue(g, c):
            for u in range(RUN_GROUP):
                slot_copy(tile, half, g * RUN_GROUP + u).start()
            return c
        lax.fori_loop(0, ngroup_ref[tile], issue, 0)

    def wait(g, c):
        rows = RUN_GROUP * RUN_ROWS
        pltpu.make_async_copy(rows_ref.at[pl.ds(0, rows)], buf.at[cur, pl.ds(0, rows)], sem.at[cur]).wait()
        return c

    @pl.when(i == 0)
    def _():
        buf[...] = jnp.zeros_like(buf)
        fetch(0, 0)

    @pl.when(i + 1 < pl.num_programs(0))
    def _():
        fetch(i + 1, 1 - cur)

    route = route_ref[...]
    lane = lax.broadcasted_iota(jnp.int32, (tm, LANES), 1).astype(F32)
    off = off_ref[0]

    def buf_row(e_lane, r_lane):
        e = route[:, e_lane:e_lane + 1]
        base = jnp.sum(jnp.where(lane == e, off, 0.0), axis=-1, keepdims=True)
        return (base + route[:, r_lane:r_lane + 1]).astype(jnp.int32)

    col = lax.broadcasted_iota(jnp.int32, (tm, RUN_SLOTS * RUN_ROWS), 1)
    pick = (jnp.where(col == buf_row(ROUTE_E1, ROUTE_R1), route[:, ROUTE_G1:ROUTE_G1 + 1], 0.0)
            + jnp.where(col == buf_row(ROUTE_E2, ROUTE_R2), route[:, ROUTE_G2:ROUTE_G2 + 1], 0.0))
    lax.fori_loop(0, ngroup_ref[i], wait, 0)
    o_ref[...] = x1_ref[...] + jnp.dot(pick.astype(BF16), _unpack_rows(buf[cur]), preferred_element_type=F32)


def _combine(src, n_groups, x1, route, off, y_rows):
    N, D = x1.shape
    tm = min(TM_MOVE, N)
    return pl.pallas_call(
        functools.partial(_combine_kernel, tm=tm),
        grid_spec=pltpu.PrefetchScalarGridSpec(
            num_scalar_prefetch=2,
            grid=(N // tm,),
            in_specs=[
                pl.BlockSpec((tm, D), lambda i, s, n: (i, 0)),
                pl.BlockSpec((tm, LANES), lambda i, s, n: (i, 0)),
                pl.BlockSpec((1, 1, LANES), lambda i, s, n: (i, 0, 0)),
                pl.BlockSpec(memory_space=pl.ANY),
            ],
            out_specs=pl.BlockSpec((tm, D), lambda i, s, n: (i, 0)),
            scratch_shapes=[pltpu.VMEM((2, RUN_SLOTS * RUN_ROWS, D_PACK), U32), pltpu.SemaphoreType.DMA((2,))],
        ),
        out_shape=jax.ShapeDtypeStruct((N, D), F32),
        compiler_params=_params("arbitrary"),
    )(src, n_groups, x1, route, off, y_rows)


def _moe(x1, h2, route, route_t, counts, sub_carry, w1, w3, w2, layer):
    N, _ = x1.shape
    tm = min(TM_MOVE, N)
    n_tiles = N // tm
    experts = jnp.arange(N_EXPERTS, dtype=jnp.int32)
    nb = (2 * N + N_EXPERTS * (TB_EXP - 1)) // TB_EXP + 1
    e_id = route_t[ROUTE_E1:ROUTE_E2 + 1].astype(jnp.int32)
    rank = route_t[ROUTE_R1:ROUTE_R2 + 1].astype(jnp.int32)
    cnt = counts[0, :N_EXPERTS].astype(jnp.int32)
    padded = (cnt + TB_EXP - 1) // TB_EXP * TB_EXP
    pad_end = jnp.cumsum(padded)
    pad_start = pad_end - padded
    e_flat = e_id.reshape(1, 2 * N)
    dest = jnp.sum(jnp.where(e_flat == experts[:, None], pad_start[:, None], 0), axis=0) + rank.reshape(2 * N)
    x_rows = _dispatch(dest, h2, nb * TB_EXP)
    n_used = pad_end[-1] // TB_EXP
    blk_map = jnp.minimum(jnp.arange(nb, dtype=jnp.int32), jnp.maximum(n_used - 1, 0))
    blk_exp = jnp.sum(pad_end[None, :] <= (blk_map * TB_EXP)[:, None], axis=1).astype(jnp.int32)
    blk_exp = jnp.minimum(blk_exp, N_EXPERTS - 1)
    is_blk_e = blk_exp[:, None] == experts[None, :]
    blk_valid = jnp.clip(jnp.sum(jnp.where(is_blk_e, (pad_start + cnt)[None, :], 0), axis=1) - blk_map * TB_EXP,
                         0, TB_EXP).astype(jnp.int32)
    has_rows = cnt > 0
    slot_of_e = (jnp.cumsum(has_rows.astype(jnp.int32)) - 1) % 2
    later = jnp.where(has_rows, experts, N_EXPERTS)
    next_of_e = lax.cummin(jnp.concatenate([later[1:], jnp.full((1,), N_EXPERTS, jnp.int32)]), reverse=True)
    next_of_e = jnp.where(next_of_e < N_EXPERTS, next_of_e, -1)
    blk_slot = jnp.sum(jnp.where(is_blk_e, slot_of_e[None, :], 0), axis=1).astype(jnp.int32)
    blk_next = jnp.sum(jnp.where(is_blk_e, next_of_e[None, :], 0), axis=1).astype(jnp.int32)

    per_router_tile = TM_OUT // tm if N >= TM_OUT else 1
    carry = sub_carry.reshape(-1, SUBLANES, LANES)[:, :per_router_tile, :N_EXPERTS].reshape(n_tiles, N_EXPERTS)
    carry = carry.astype(jnp.int32)
    tile_cnt = jnp.concatenate([carry[1:], cnt[None, :]], axis=0) - carry
    run_start = pad_start[None, :] + carry
    first_blk = run_start // RUN_ROWS
    nslot = jnp.where(tile_cnt > 0, (run_start + tile_cnt - 1) // RUN_ROWS - first_blk + 1, 0)
    slot_end = jnp.cumsum(nslot, axis=1)
    slot_base = slot_end - nslot
    slots = jnp.arange(RUN_SLOTS, dtype=jnp.int32)
    slot_e = jnp.minimum(jnp.sum(slot_end[:, None, :] <= slots[None, :, None], axis=2), N_EXPERTS - 1)
    is_slot_e = slot_e[:, :, None] == experts[None, None, :]
    pick = lambda tbl: jnp.sum(jnp.where(is_slot_e, tbl[:, None, :], 0), axis=2)
    src = (pick(first_blk) + slots[None, :] - pick(slot_base)) * RUN_ROWS
    src = jnp.where(slots[None, :] < slot_end[:, -1:], src, 0)
    src = jnp.clip(src, 0, nb * TB_EXP - RUN_ROWS).reshape(-1).astype(jnp.int32)
    n_groups = (slot_end[:, -1] + RUN_GROUP - 1) // RUN_GROUP
    off = (slot_base - first_blk) * RUN_ROWS + pad_start[None, :]
    off = jnp.pad(off.astype(F32), ((0, 0), (0, LANES - N_EXPERTS)))

    y_rows = _experts(blk_map, blk_exp, blk_valid, blk_slot, blk_next, n_used.reshape(1).astype(jnp.int32),
                      x_rows, w1, w3, w2, layer)
    return _combine(src, n_groups.astype(jnp.int32), x1, route, off.reshape(n_tiles, 1, LANES), y_rows)


def _pad_heads(w, width):
    lead = w.shape[:-1]
    w = w.reshape(lead + (N_HEADS, width))
    pad = [(0, 0)] * (w.ndim - 1) + [(0, HEAD_PAD - width)]
    return jnp.pad(w, pad).reshape(lead + (N_HEADS * HEAD_PAD,))


def _swap_rope(w):
    half = QK_ROPE // 2
    lo, hi = w[..., QK_NOPE:QK_NOPE + half], w[..., QK_NOPE + half:QK_HEAD]
    pad = [(0, 0)] * (w.ndim - 1)
    return jnp.pad(jnp.concatenate([hi, lo], -1), pad + [(QK_NOPE, HEAD_PAD - QK_HEAD)])


def _rope_tables(positions):
    inv_freq = ROPE_THETA ** (-jnp.arange(0, QK_ROPE, 2, dtype=F32) / QK_ROPE)
    ang = positions.astype(F32)[..., None] * inv_freq
    table = jnp.concatenate([jnp.cos(ang), jnp.sin(ang)], -1)
    return jnp.pad(table, ((0, 0), (0, 0), (QK_NOPE, LANES - QK_HEAD)))


def _ssm_params(lam_re, lam_im, b_re, b_im, c_re, c_im, log_dt):
    depth = lam_re.shape[0]
    lam = lax.complex(lam_re, lam_im)
    dt = jnp.exp(log_dt)[..., None]
    lam_bar = jnp.exp(lam * dt)
    b_bar = ((lam_bar - 1.0) / lam)[..., None] * lax.complex(b_re, b_im)
    eye = jnp.eye(SSM_GROUPS, dtype=F32)

    def in_blockdiag(m):
        return jnp.einsum("lgpc,gh->lgchp", m, eye).reshape(depth, SSM_WIDTH, SSM_FLAT)

    def out_blockdiag(m):
        return jnp.einsum("lgcp,gh->lgphc", m, eye).reshape(depth, SSM_FLAT, SSM_WIDTH)

    bbd = jnp.concatenate([in_blockdiag(jnp.real(b_bar)), in_blockdiag(jnp.imag(b_bar))], axis=2)
    cbd = jnp.concatenate([out_blockdiag(c_re), out_blockdiag(-c_im)], axis=1)
    lam_rows = jnp.stack([jnp.real(lam_bar).reshape(depth, SSM_FLAT), jnp.imag(lam_bar).reshape(depth, SSM_FLAT)],
                         axis=1)
    return bbd.astype(BF16), lam_rows, cbd.astype(BF16)


def kernel(x, positions, mix_norm, w_in, q_a_norm, w_uq, kv_a_norm, w_ukv, q_norm, k_norm, ssm_lam_re, ssm_lam_im, ssm_b_re, ssm_b_im, ssm_c_re, ssm_c_im, ssm_d, ssm_log_dt, ssm_w_glu, conv_dw_w, conv_dw_b, conv_ln_w, conv_ln_b, out_norm, w_out, ffn_norm, w_grp, b_grp, w_exp, b_exp, w1, w3, w2):
    B, L, D = x.shape
    depth = w_in.shape[0]
    rope = _rope_tables(positions)
    rows3 = lambda v: v[:, None, :]
    lane_tail = lambda v: jnp.pad(v, [(0, 0)] * (v.ndim - 1) + [(0, LANES - v.shape[-1])])
    c_q, c_kv, k_pe, u_s, c_a, c_g = jnp.split(
        w_in, [Q_LORA, Q_LORA + KV_LORA, Q_LORA + KV_LORA + QK_ROPE, Q_LORA + KV_LORA + QK_ROPE + SSM_WIDTH,
               Q_LORA + KV_LORA + QK_ROPE + SSM_WIDTH + CONV_WIDTH], axis=2)
    k_pe_full = jnp.pad(k_pe, ((0, 0), (0, 0), (QK_NOPE, 0)))
    win = jnp.concatenate([c_q, c_kv, lane_tail(k_pe_full), _swap_rope(k_pe_full), u_s, c_a, c_g], axis=2).astype(BF16)
    wkv = w_ukv.reshape(depth, KV_LORA, N_HEADS, QK_NOPE + V_HEAD)
    wuk = _pad_heads(wkv[..., :QK_NOPE].reshape(depth, KV_LORA, N_HEADS * QK_NOPE), QK_NOPE).astype(BF16)
    wuv = wkv[..., QK_NOPE:].reshape(depth, KV_LORA, ATTN_WIDTH).astype(BF16)
    wuq_sw = _swap_rope(w_uq.reshape(depth, Q_LORA, N_HEADS, QK_HEAD)).reshape(depth, Q_LORA, N_HEADS * HEAD_PAD)
    wuq = jnp.concatenate([_pad_heads(w_uq, QK_HEAD), wuq_sw], axis=2).astype(BF16)
    norm_rows = lambda w: jnp.stack([lane_tail(w), _swap_rope(w)], axis=1)
    qn, kn = norm_rows(q_norm * (QK_HEAD ** -0.5)), norm_rows(k_norm)
    bbd, lam_rows, cbd = _ssm_params(ssm_lam_re, ssm_lam_im, ssm_b_re, ssm_b_im, ssm_c_re, ssm_c_im, ssm_log_dt)
    ssm_skip = rows3(ssm_d.reshape(depth, SSM_WIDTH))
    wglu = ssm_w_glu.astype(BF16)
    g_attn = rows3(out_norm[:, :ATTN_WIDTH])
    g_ssm = rows3(out_norm[:, ATTN_WIDTH:ATTN_WIDTH + SSM_WIDTH])
    conv_params = (conv_dw_w, rows3(conv_dw_b), rows3(conv_ln_w), rows3(conv_ln_b),
                   rows3(out_norm[:, ATTN_WIDTH + SSM_WIDTH:]))
    w_route = lane_tail(jnp.concatenate([w_exp, w_grp], axis=2))
    w_route_hi = w_route.astype(BF16)
    w_route = jnp.concatenate([w_route_hi, (w_route - w_route_hi.astype(F32)).astype(BF16)], axis=2)
    b_route = rows3(lane_tail(jnp.concatenate([b_exp, b_grp], axis=1)))
    wo = w_out.astype(BF16)
    for l in range(depth):
        q, k, v, u_ssm, cu = _in_proj(x, rope, rows3(mix_norm), win, rows3(q_a_norm), wuq, rows3(kv_a_norm), wuk, wuv,
                                         qn, kn, l)
        y_attn = _attention(q, k, v)
        y_ssm = _ssm(u_ssm, bbd, lam_rows, cbd, ssm_skip, wglu, g_ssm, B, l)
        x1, h2, route, route_t, counts, sub_carry = _out_router(
            x.reshape(B * L, D), y_attn.reshape(B * L, ATTN_WIDTH), y_ssm, cu.reshape(B * L, CONV_WIDTH), g_attn, wo, rows3(ffn_norm),
            w_route, b_route, conv_params, B, l)
        x = _moe(x1, h2, route, route_t, counts, sub_carry, w1, w3, w2, l).reshape(B, L, D)
    return x
```

```python
import functools

import jax
import jax.numpy as jnp
from jax import lax
from jax.experimental import pallas as pl
from jax.experimental.pallas import tpu as pltpu
from jax.experimental.pallas import tpu_sc as plsc

D_MODEL = 1024
CHUNK = 64
EPS = 1e-6
LN_EPS = 1e-5
N_HEADS = 8
QK_NOPE = 64
QK_ROPE = 32
QK_HEAD = QK_NOPE + QK_ROPE
V_HEAD = 64
Q_LORA = 256
KV_LORA = 128
ROPE_THETA = 10000.0
ATTN_WIDTH = N_HEADS * V_HEAD
SSM_WIDTH = 256
SSM_GROUP = 16
SSM_GROUPS = SSM_WIDTH // SSM_GROUP
SSM_STATE = 64
SSM_FLAT = SSM_GROUPS * SSM_STATE
CONV_WIDTH = 256
CONV_K = 31
N_EGROUPS = 4
EXP_PER_GROUP = 8
N_EXPERTS = N_EGROUPS * EXP_PER_GROUP
D_FF_E = 512

LANES = 128
SUBLANES = 8
HEAD_PAD = LANES
IN_PROJ_PAD = Q_LORA + KV_LORA + SSM_WIDTH + 2 * CONV_WIDTH + 2 * LANES
CONV_HALO = 32
VMEM_LIMIT = 48 * 1024 * 1024

TM_PROJ = 512
TQ_ATTN = 256
TC_SSM = 128
SSM_PARTS = 2
CONV_SUB = 512
TM_OUT = 512
TM_MOVE = 256
RUN_ROWS = SUBLANES
RUN_SLOTS = 2 * N_EXPERTS + 2 * TM_MOVE // RUN_ROWS
RUN_GROUP = 8
SC_CORES, SC_SUBCORES = 2, 16
SC_CHUNK = 64
TB_EXP = 512
EXPERT_TAIL_DIVISORS = (1, 2, 4)

BF16 = jnp.bfloat16
F32 = jnp.float32
U32 = jnp.uint32
D_PACK = D_MODEL // 2


def _pack_rows(v):
    bits = lax.bitcast_convert_type(v.astype(BF16).astype(F32), U32)
    half = v.shape[1] // 2
    return bits[:, 0:half] | (bits[:, half:] >> 16)


def _unpack_rows(w):
    hi = lax.bitcast_convert_type(w & jnp.uint32(0xFFFF0000), F32)
    lo = lax.bitcast_convert_type(w << 16, F32)
    return jnp.concatenate([hi, lo], axis=1).astype(BF16)


def _rms(x, w):
    return x * lax.rsqrt(jnp.mean(x * x, axis=-1, keepdims=True) + EPS) * w


def _params(*sem):
    return pltpu.CompilerParams(dimension_semantics=sem, vmem_limit_bytes=VMEM_LIMIT)


def _in_proj_kernel(x_ref, rope_ref, mixn_ref, win_ref, qan_ref, wuq_ref, kvan_ref, wuk_ref, wuv_ref,
                    qn_ref, kn_ref, q_ref, k_ref, v_ref, u_ref, cu_ref):
    x = x_ref[0]
    h = _rms(x, mixn_ref[0]).astype(BF16)
    head_cols = Q_LORA + KV_LORA + 2 * LANES
    proj = jnp.dot(h, win_ref[0, :, 0:head_cols], preferred_element_type=F32)
    o = 0
    c_q = proj[:, o:o + Q_LORA]; o += Q_LORA
    c_kv = proj[:, o:o + KV_LORA]; o += KV_LORA
    k_pe = proj[:, o:o + LANES]; o += LANES
    k_pe_sw = proj[:, o:o + LANES]

    width = N_HEADS * HEAD_PAD
    q2 = jnp.dot(_rms(c_q, qan_ref[0]).astype(BF16), wuq_ref[0], preferred_element_type=F32)
    q, q_sw = q2[:, 0:width], q2[:, width:2 * width]
    ckv_n = _rms(c_kv, kvan_ref[0]).astype(BF16)
    kn = jnp.dot(ckv_n, wuk_ref[0], preferred_element_type=F32)
    v_ref[0] = jnp.dot(ckv_n, wuv_ref[0], preferred_element_type=F32).astype(BF16)

    rope = rope_ref[0]
    half = QK_ROPE // 2
    lane = lax.broadcasted_iota(jnp.int32, rope.shape, 1)
    lo = (lane >= QK_NOPE) & (lane < QK_NOPE + half)
    hi = (lane >= QK_NOPE + half) & (lane < QK_HEAD)
    cos_t = jnp.where(lane < QK_NOPE, 1.0, jnp.where(lo, rope, jnp.where(hi, pltpu.roll(rope, half, 1), 0.0)))
    sin_t = jnp.where(lo, -pltpu.roll(rope, LANES - half, 1), jnp.where(hi, rope, 0.0))
    a_q, b_q = qn_ref[0, 0:1, :] * cos_t, qn_ref[0, 1:2, :] * sin_t
    a_k = kn_ref[0, 0:1, :] * cos_t
    k_sw_term = k_pe_sw * (kn_ref[0, 1:2, :] * sin_t)

    def inv_rms(y):
        return lax.rsqrt(jnp.sum(y * y, axis=-1, keepdims=True) * (1.0 / QK_HEAD) + EPS)

    def rest(j):
        cols = slice(head_cols + j * SSM_WIDTH, head_cols + (j + 1) * SSM_WIDTH)
        return jnp.dot(h, win_ref[0, :, cols], preferred_element_type=F32)

    for hd in range(N_HEADS):
        if hd == 0:
            u_ref[...] = rest(0).astype(BF16)
        elif hd == 4:
            cu_ref[0] = (rest(1) * jax.nn.sigmoid(rest(2))).astype(BF16)
        sl = slice(hd * HEAD_PAD, (hd + 1) * HEAD_PAD)
        yq = q[:, sl]
        q_ref[0, hd] = ((yq * a_q + q_sw[:, sl] * b_q) * inv_rms(yq)).astype(BF16)
        yk = kn[:, sl] + k_pe
        k_ref[0, hd] = ((yk * a_k + k_sw_term) * inv_rms(yk)).astype(BF16)


def _in_proj(x, rope, mixn, win, qan, wuq, kvan, wuk, wuv, qn, kn, layer):
    B, L, D = x.shape
    tm = min(TM_PROJ, L)
    full = lambda shape: pl.BlockSpec((1,) + shape, lambda b, t: (layer,) + (0,) * len(shape))
    return pl.pallas_call(
        _in_proj_kernel,
        grid=(B, L // tm),
        in_specs=[
            pl.BlockSpec((1, tm, D), lambda b, t: (b, t, 0)),
            pl.BlockSpec((1, tm, LANES), lambda b, t: (b, t, 0)),
            full((1, D)), full((D, IN_PROJ_PAD)), full((1, Q_LORA)), full((Q_LORA, 2 * N_HEADS * HEAD_PAD)),
            full((1, KV_LORA)), full((KV_LORA, N_HEADS * HEAD_PAD)), full((KV_LORA, ATTN_WIDTH)),
            full((2, HEAD_PAD)), full((2, HEAD_PAD)),
        ],
        out_specs=[
            pl.BlockSpec((1, N_HEADS, tm, HEAD_PAD), lambda b, t: (b, 0, t, 0)),
            pl.BlockSpec((1, N_HEADS, tm, HEAD_PAD), lambda b, t: (b, 0, t, 0)),
            pl.BlockSpec((1, tm, ATTN_WIDTH), lambda b, t: (b, t, 0)),
            pl.BlockSpec((tm, SSM_WIDTH), lambda b, t: (t, b)),
            pl.BlockSpec((1, tm, CONV_WIDTH), lambda b, t: (b, t, 0)),
        ],
        out_shape=[
            jax.ShapeDtypeStruct((B, N_HEADS, L, HEAD_PAD), BF16),
            jax.ShapeDtypeStruct((B, N_HEADS, L, HEAD_PAD), BF16),
            jax.ShapeDtypeStruct((B, L, ATTN_WIDTH), BF16),
            jax.ShapeDtypeStruct((L, B * SSM_WIDTH), BF16),
            jax.ShapeDtypeStruct((B, L, CONV_WIDTH), BF16),
        ],
        compiler_params=_params("parallel", "parallel"),
    )(x, rope, mixn, win, qan, wuq, kvan, wuk, wuv, qn, kn)


def _attention_kernel(q_ref, k_ref, v_ref, o_ref, vext_ref, *, seq, tq):
    pair = 2 * V_HEAD
    vext_ref[:, 0:pair] = v_ref[0]
    vext_ref[:, pair:] = jnp.ones((seq, LANES), BF16)
    row_chunk = lax.broadcasted_iota(jnp.int32, (tq, tq), 0) // CHUNK
    col_chunk = lax.broadcasted_iota(jnp.int32, (tq, tq), 1) // CHUNK
    visible = col_chunk <= row_chunk
    neg = jnp.finfo(F32).min
    nt = (((1,), (1,)), ((), ()))
    low_lanes = lax.broadcasted_iota(jnp.int32, (tq, pair), 1) < V_HEAD

    def scores(i, hh):
        q0 = i * tq
        qb = q_ref[0, hh, q0:q0 + tq, :]
        s_d = lax.dot_general(qb, k_ref[0, hh, q0:q0 + tq, :], nt, preferred_element_type=F32)
        s_d = jnp.where(visible, s_d, neg)
        s_l = lax.dot_general(qb, k_ref[0, hh, 0:q0, :], nt, preferred_element_type=F32) if i > 0 else None
        return s_d, s_l

    def weighted_values(i, s_d, s_l):
        q0 = i * tq
        m = jnp.max(s_d, axis=-1, keepdims=True)
        if s_l is not None:
            m = jnp.maximum(m, jnp.max(s_l, axis=-1, keepdims=True))
        acc = jnp.dot(jnp.exp(s_d - m).astype(BF16), vext_ref[q0:q0 + tq, :], preferred_element_type=F32)
        if s_l is not None:
            acc = acc + jnp.dot(jnp.exp(s_l - m).astype(BF16), vext_ref[0:q0, :], preferred_element_type=F32)
        return acc[:, 0:pair] / acc[:, pair:]

    nq = seq // tq
    order = [x for p in zip(reversed(range(nq)), range(nq)) for x in p][:nq]
    chains = [(i, hh) for i in order for hh in range(2)]
    pending = scores(*chains[0])
    outs = {}
    for n, (i, hh) in enumerate(chains):
        upcoming = scores(*chains[n + 1]) if n + 1 < len(chains) else None
        outs[hh] = weighted_values(i, *pending)
        pending = upcoming
        if hh == 1:
            o_ref[0, i * tq:(i + 1) * tq, :] = jnp.where(low_lanes, outs[0], outs[1]).astype(BF16)


def _attention(q, k, v):
    B, H, L, _ = q.shape
    tq = min(TQ_ATTN, L)
    return pl.pallas_call(
        functools.partial(_attention_kernel, seq=L, tq=tq),
        grid=(B, H // 2),
        in_specs=[
            pl.BlockSpec((1, 2, L, HEAD_PAD), lambda b, p: (b, p, 0, 0)),
            pl.BlockSpec((1, 2, L, HEAD_PAD), lambda b, p: (b, p, 0, 0)),
            pl.BlockSpec((1, L, 2 * V_HEAD), lambda b, p: (b, 0, p)),
        ],
        out_specs=pl.BlockSpec((1, L, 2 * V_HEAD), lambda b, p: (b, 0, p)),
        out_shape=jax.ShapeDtypeStruct((B, L, ATTN_WIDTH), BF16),
        scratch_shapes=[pltpu.VMEM((L, 2 * V_HEAD + LANES), BF16)],
        compiler_params=_params("parallel", "parallel"),
    )(q, k, v)


def _ssm_kernel(u_ref, bbd_ref, lam_ref, cbd_ref, d_ref, wglu_ref, g_ref, o_ref, state_ref, in_s, out_s, *bu_refs,
                batch, tc):
    @pl.when(pl.program_id(0) == 0)
    def _():
        state_ref[...] = jnp.zeros_like(state_ref)

    steps = tc // SSM_PARTS
    halves = tuple((bu, slice(p * steps * batch, (p + 1) * steps * batch)) for p, bu in enumerate(bu_refs))
    for b in range(batch):
        for hl in range(SSM_WIDTH // LANES):
            lanes = slice(b * SSM_WIDTH + hl * LANES, b * SSM_WIDTH + (hl + 1) * LANES)
            in_s[hl, pl.ds(b, tc, stride=batch), :] = u_ref[:, lanes].astype(F32)

    def u_rows(rows):
        return jnp.concatenate([in_s[hl, rows, :] for hl in range(SSM_WIDTH // LANES)], axis=1)

    for bu_ref, rows in halves:
        bu_ref[...] = jnp.dot(u_rows(rows).astype(BF16), bbd_ref[0], preferred_element_type=F32)
    lam_re = jnp.broadcast_to(lam_ref[0, 0:1, :], (batch, SSM_FLAT))
    lam_im = jnp.broadcast_to(lam_ref[0, 1:2, :], (batch, SSM_FLAT))

    carry = (state_ref[:, 0:SSM_FLAT], state_ref[:, SSM_FLAT:2 * SSM_FLAT])
    for bu_ref, rows in halves:
        def step(t, xs, bu_ref=bu_ref):
            xr, xi = xs
            at = pl.ds(pl.multiple_of(t * batch, batch), batch)
            nr = lam_re * xr - lam_im * xi + bu_ref[at, 0:SSM_FLAT]
            ni = lam_re * xi + lam_im * xr + bu_ref[at, SSM_FLAT:2 * SSM_FLAT]
            bu_ref[at, 0:SSM_FLAT] = nr
            bu_ref[at, SSM_FLAT:2 * SSM_FLAT] = ni
            return nr, ni

        carry = lax.fori_loop(0, steps, step, carry, unroll=True)
        y = jnp.dot(bu_ref[...].astype(BF16), cbd_ref[0], preferred_element_type=F32)
        y = y + d_ref[0] * u_rows(rows)
        z = jax.nn.gelu(y)
        gate = jax.nn.sigmoid(jnp.dot(z.astype(BF16), wglu_ref[0], preferred_element_type=F32))
        res = _rms(z * gate, g_ref[0])
        for hl in range(SSM_WIDTH // LANES):
            out_s[hl, rows, :] = res[:, hl * LANES:(hl + 1) * LANES]
    state_ref[:, 0:SSM_FLAT] = carry[0]
    state_ref[:, SSM_FLAT:2 * SSM_FLAT] = carry[1]
    for b in range(batch):
        for hl in range(SSM_WIDTH // LANES):
            lanes = slice(b * SSM_WIDTH + hl * LANES, b * SSM_WIDTH + (hl + 1) * LANES)
            o_ref[:, lanes] = out_s[hl, pl.ds(b, tc, stride=batch), :].astype(BF16)


def _ssm(u, bbd, lam, cbd, d, wglu, g, batch, layer):
    seq = u.shape[0]
    tc = min(TC_SSM, seq)
    blk = tc * batch
    full = lambda shape: pl.BlockSpec((1,) + shape, lambda t: (layer,) + (0,) * len(shape))
    return pl.pallas_call(
        functools.partial(_ssm_kernel, batch=batch, tc=tc),
        grid=(seq // tc,),
        in_specs=[
            pl.BlockSpec((tc, batch * SSM_WIDTH), lambda t: (t, 0)),
            full((SSM_WIDTH, 2 * SSM_FLAT)), full((2, SSM_FLAT)), full((2 * SSM_FLAT, SSM_WIDTH)),
            full((1, SSM_WIDTH)), full((SSM_WIDTH, SSM_WIDTH)), full((1, SSM_WIDTH)),
        ],
        out_specs=pl.BlockSpec((tc, batch * SSM_WIDTH), lambda t: (t, 0)),
        out_shape=jax.ShapeDtypeStruct((seq, batch * SSM_WIDTH), BF16),
        scratch_shapes=[pltpu.VMEM((batch, 2 * SSM_FLAT), F32), pltpu.VMEM((SSM_WIDTH // LANES, blk, LANES), F32),
                        pltpu.VMEM((SSM_WIDTH // LANES, blk, LANES), F32)]
        + [pltpu.VMEM((blk // SSM_PARTS, 2 * SSM_FLAT), F32)] * SSM_PARTS,
        compiler_params=_params("arbitrary"),
    )(u, bbd, lam, cbd, d, wglu, g)


def _conv_rows(upad_ref, r0, w_ref, b_ref, lnw_ref, lnb_ref, gn_ref):
    base = CONV_HALO - (CONV_K - 1)
    acc = jnp.zeros((CONV_SUB, CONV_WIDTH), F32)
    for off in range(SUBLANES):
        n = CONV_SUB + (SUBLANES if off else 0)
        part = None
        for a8 in range(0, CONV_HALO + 1, SUBLANES):
            kk = a8 + off - base
            if 0 <= kk < CONV_K:
                term = w_ref[0, kk:kk + 1, :] * upad_ref[r0 + a8:r0 + a8 + n, :]
                part = term if part is None else part + term
        acc = acc + part[off:off + CONV_SUB]
    y = acc + b_ref[0]
    mu = jnp.mean(y, axis=-1, keepdims=True)
    var = jnp.mean(jnp.square(y - mu), axis=-1, keepdims=True)
    y = (y - mu) * lax.rsqrt(var + LN_EPS) * lnw_ref[0] + lnb_ref[0]
    y = y * jax.nn.sigmoid(y)
    return _rms(y, gn_ref[0])


ROUTE_E1, ROUTE_E2, ROUTE_R1, ROUTE_R2, ROUTE_G1, ROUTE_G2 = range(6)
ROUTE_ROWS = 8
GRP_LANE0 = N_EXPERTS


def _out_router_kernel(x_ref, ya_ref, ys_ref, cu_ref, ga_ref, wo_ref, fn_ref, wr_ref, br_ref,
                       cw_ref, cb_ref, clnw_ref, clnb_ref, cgn_ref,
                       x1_ref, h2_ref, route_ref, route_t_ref, cnt_ref, sub_ref,
                       carry_ref, tri_ref, upad_ref, yc_ref, *, tm, per_b):
    @pl.when(pl.program_id(0) == 0)
    def _():
        carry_ref[...] = jnp.zeros_like(carry_ref)
        rr = lax.broadcasted_iota(jnp.int32, (tm, tm), 0)
        cc = lax.broadcasted_iota(jnp.int32, (tm, tm), 1)
        tri_ref[...] = (cc < rr).astype(BF16)

    first = pl.program_id(0) % per_b == 0
    upad_ref[0:CONV_HALO, :] = jnp.where(first, 0.0, upad_ref[tm:tm + CONV_HALO, :])
    upad_ref[CONV_HALO:, :] = cu_ref[...].astype(F32)
    ya = _rms(ya_ref[...].astype(F32), ga_ref[0]).astype(BF16)
    acc = jnp.dot(ya, wo_ref[0, 0:ATTN_WIDTH, :], preferred_element_type=F32)
    acc += jnp.dot(ys_ref[...], wo_ref[0, ATTN_WIDTH:ATTN_WIDTH + SSM_WIDTH, :], preferred_element_type=F32)
    for r0 in range(0, tm, CONV_SUB):
        yc_ref[r0:r0 + CONV_SUB, :] = _conv_rows(upad_ref, r0, cw_ref, cb_ref, clnw_ref, clnb_ref, cgn_ref).astype(BF16)
    acc += jnp.dot(yc_ref[...], wo_ref[0, ATTN_WIDTH + SSM_WIDTH:, :], preferred_element_type=F32)
    x1 = x_ref[...] + acc
    x1_ref[...] = x1
    h2 = _rms(x1, fn_ref[0])
    h2_ref[...] = _pack_rows(h2)

    h_hi = h2.astype(BF16)
    h_lo = (h2 - h_hi.astype(F32)).astype(BF16)
    part = jnp.dot(h_hi, wr_ref[0], preferred_element_type=F32)
    logits = (part[:, 0:LANES] + part[:, LANES:2 * LANES]
              + jnp.dot(h_lo, wr_ref[0, :, 0:LANES], preferred_element_type=F32) + br_ref[0])
    lane = lax.broadcasted_iota(jnp.int32, (tm, LANES), 1)
    ninf = -jnp.inf
    big = LANES

    def first_argmax(vals, vmax):
        return jnp.min(jnp.where(vals == vmax, lane, big), axis=-1, keepdims=True)

    grp = jnp.where((lane >= GRP_LANE0) & (lane < GRP_LANE0 + N_EGROUPS), logits, ninf)
    gmax = jnp.max(grp, axis=-1, keepdims=True)
    gsel = first_argmax(grp, gmax) - GRP_LANE0
    p_grp = 1.0 / jnp.sum(jnp.exp(grp - gmax), axis=-1, keepdims=True)

    el = jnp.where((lane < N_EXPERTS) & ((lane // EXP_PER_GROUP) == gsel), logits, ninf)
    m1 = jnp.max(el, axis=-1, keepdims=True)
    e1 = first_argmax(el, m1)
    el2 = jnp.where(lane == e1, ninf, el)
    m2 = jnp.max(el2, axis=-1, keepdims=True)
    e2 = first_argmax(el2, m2)
    t2 = jnp.exp(m2 - m1)
    g1 = p_grp / (1.0 + t2)
    g2 = p_grp * t2 / (1.0 + t2)

    hit1 = lane == e1
    hit2 = lane == e2
    cnt = (hit1 | hit2).astype(F32)
    before = jnp.dot(tri_ref[...], cnt.astype(BF16), preferred_element_type=F32) + carry_ref[...]
    r1 = jnp.sum(jnp.where(hit1, before, 0.0), axis=-1, keepdims=True)
    r2 = jnp.sum(jnp.where(hit2, before, 0.0), axis=-1, keepdims=True)
    subs = [before[h * TM_MOVE:h * TM_MOVE + 1, :] for h in range(tm // TM_MOVE)]
    sub_ref[...] = jnp.concatenate(subs + [jnp.zeros((SUBLANES - len(subs), LANES), F32)], axis=0)
    carry_ref[...] += jnp.sum(cnt, axis=0, keepdims=True)
    cnt_ref[...] = carry_ref[...]

    rec = jnp.where(lane == ROUTE_E1, e1.astype(F32), 0.0)
    rec = jnp.where(lane == ROUTE_E2, e2.astype(F32), rec)
    rec = jnp.where(lane == ROUTE_R1, r1, rec)
    rec = jnp.where(lane == ROUTE_R2, r2, rec)
    rec = jnp.where(lane == ROUTE_G1, g1, rec)
    rec = jnp.where(lane == ROUTE_G2, g2, rec)
    route_ref[...] = rec
    route_t_ref[...] = rec.T[0:ROUTE_ROWS, :]


def _out_router(x, ya, ys, cu, ga, wo, fn, wr, br, conv_params, batch, layer):
    N, D = x.shape
    seq = N // batch
    tm = min(TM_OUT, seq)
    per_b = seq // tm
    full = lambda shape: pl.BlockSpec((1,) + shape, lambda i: (layer,) + (0,) * len(shape))
    whole = lambda shape: pl.BlockSpec(shape, lambda i: (0,) * len(shape))
    tile = lambda w: pl.BlockSpec((tm, w), lambda i: (i, 0))
    return pl.pallas_call(
        functools.partial(_out_router_kernel, tm=tm, per_b=per_b),
        grid=(N // tm,),
        in_specs=[
            tile(D), tile(ATTN_WIDTH),
            pl.BlockSpec((tm, SSM_WIDTH), lambda i: (i % per_b, i // per_b)),
            tile(CONV_WIDTH),
            full((1, ATTN_WIDTH)), full((D, D)), full((1, D)), full((D, 2 * LANES)), full((1, LANES)),
            full((CONV_K, CONV_WIDTH)), full((1, CONV_WIDTH)), full((1, CONV_WIDTH)), full((1, CONV_WIDTH)),
            full((1, CONV_WIDTH)),
        ],
        out_specs=[tile(D), tile(D_PACK), tile(LANES), pl.BlockSpec((ROUTE_ROWS, tm), lambda i: (0, i)),
                   whole((1, LANES)), pl.BlockSpec((SUBLANES, LANES), lambda i: (i, 0))],
        out_shape=[
            jax.ShapeDtypeStruct((N, D), F32),
            jax.ShapeDtypeStruct((N, D_PACK), U32),
            jax.ShapeDtypeStruct((N, LANES), F32),
            jax.ShapeDtypeStruct((ROUTE_ROWS, N), F32),
            jax.ShapeDtypeStruct((1, LANES), F32),
            jax.ShapeDtypeStruct((N // tm * SUBLANES, LANES), F32),
        ],
        scratch_shapes=[pltpu.VMEM((1, LANES), F32), pltpu.VMEM((tm, tm), BF16),
                        pltpu.VMEM((tm + CONV_HALO, CONV_WIDTH), F32), pltpu.VMEM((tm, CONV_WIDTH), BF16)],
        compiler_params=_params("arbitrary"),
    )(x, ya, ys, cu, ga, wo, fn, wr, br, *conv_params)


def _dispatch(dest_flat, h2, n_rows):
    N, D = h2.shape
    workers = SC_CORES * SC_SUBCORES
    per_worker = N // workers
    chunks = per_worker // SC_CHUNK
    idx = dest_flat.reshape(2, workers, chunks, SC_CHUNK).transpose(1, 0, 2, 3).reshape(workers, 2 * chunks, SC_CHUNK)
    mesh = plsc.VectorSubcoreMesh(core_axis_name="c", subcore_axis_name="s")

    @functools.partial(
        pl.kernel, mesh=mesh,
        out_type=jax.ShapeDtypeStruct((n_rows, D), h2.dtype),
        scratch_types=[pltpu.VMEM((2 * chunks, SC_CHUNK), jnp.int32), pltpu.VMEM((2, SC_CHUNK, D), h2.dtype),
                       pltpu.SemaphoreType.DMA((2,)), pltpu.SemaphoreType.DMA((2, 2))],
    )
    def scatter(h_hbm, idx_hbm, out_hbm, idx_v, rows_v, sem_in, sem_out):
        wid = lax.axis_index("s") * SC_CORES + lax.axis_index("c")
        pltpu.sync_copy(idx_hbm.at[wid], idx_v)

        def load(c):
            first = pl.multiple_of(wid * per_worker + c * SC_CHUNK, SC_CHUNK)
            return pltpu.async_copy(h_hbm.at[pl.ds(first, SC_CHUNK)], rows_v.at[c % 2], sem_in.at[c % 2])

        def store(c, k):
            return pltpu.async_copy(rows_v.at[c % 2], out_hbm.at[idx_v.at[k * chunks + c]], sem_out.at[c % 2, k])

        loads = {0: load(0)}
        stores = {}
        for c in range(chunks):
            loads[c].wait()
            if c >= 1:
                for st in stores[c - 1]:
                    st.wait()
            if c + 1 < chunks:
                loads[c + 1] = load(c + 1)
            stores[c] = (store(c, 0), store(c, 1))
        for st in stores[chunks - 1]:
            st.wait()

    return scatter(h2, idx)


def _experts_kernel(blk_ref, exp_ref, valid_ref, slot_ref, next_ref, used_ref, x_ref, w1_hbm, w3_hbm, w2_hbm, y_ref,
                    w1_f, w3_f, w2_f, w13_s, w2_s, sem, *, layer):
    j = pl.program_id(0)

    def weight_copies(e, slot):
        return (pltpu.make_async_copy(w1_hbm.at[layer, e], w1_f.at[slot], sem.at[slot, 0]),
                pltpu.make_async_copy(w3_hbm.at[layer, e], w3_f.at[slot], sem.at[slot, 1]),
                pltpu.make_async_copy(w2_hbm.at[layer, e], w2_f.at[slot], sem.at[slot, 2]))

    @pl.when(j < used_ref[0])
    def _():
        e = exp_ref[j]
        slot = slot_ref[j]

        @pl.when(j == 0)
        def _():
            for cp in weight_copies(e, slot):
                cp.start()

        @pl.when((j == 0) | (e != exp_ref[jnp.maximum(j - 1, 0)]))
        def _():
            for cp in weight_copies(e, slot):
                cp.wait()
            w13_s[:, 0:D_FF_E] = w1_f[slot].astype(BF16)
            w13_s[:, D_FF_E:] = w3_f[slot].astype(BF16)
            w2_s[...] = w2_f[slot].astype(BF16)

            @pl.when(next_ref[j] >= 0)
            def _():
                for cp in weight_copies(next_ref[j], 1 - slot):
                    cp.start()

        valid = valid_ref[j]

        def ffn(rows):
            x = _unpack_rows(x_ref[0:rows, :])
            ab = jnp.dot(x, w13_s[...], preferred_element_type=F32)
            a = ab[:, 0:D_FF_E]
            hmid = (a * jax.nn.sigmoid(a) * ab[:, D_FF_E:]).astype(BF16)
            y = jnp.dot(hmid, w2_s[...], preferred_element_type=F32)
            row = lax.broadcasted_iota(jnp.int32, (rows, 1), 0)
            y_ref[0:rows, :] = jnp.where(row < valid, _pack_rows(y), jnp.uint32(0))
            if rows < TB_EXP:
                y_ref[rows:, :] = jnp.zeros((TB_EXP - rows, D_PACK), U32)

        sizes = [TB_EXP // d for d in EXPERT_TAIL_DIVISORS]
        for n, rows in enumerate(sizes):
            below = sizes[n + 1] if n + 1 < len(sizes) else 0

            @pl.when((valid > below) & (valid <= rows) if n else valid > below)
            def _(rows=rows):
                ffn(rows)


def _experts(blk_map, blk_exp, blk_valid, blk_slot, blk_next, n_used, x_rows, w1, w3, w2, layer):
    R, D = x_rows.shape[0], D_MODEL
    nb = R // TB_EXP
    row_block = lambda j, bm, be, bv, bs, bn, nu: (bm[j], 0)
    return pl.pallas_call(
        functools.partial(_experts_kernel, layer=layer),
        grid_spec=pltpu.PrefetchScalarGridSpec(
            num_scalar_prefetch=6,
            grid=(nb,),
            in_specs=[
                pl.BlockSpec((TB_EXP, D_PACK), row_block),
                pl.BlockSpec(memory_space=pl.ANY), pl.BlockSpec(memory_space=pl.ANY), pl.BlockSpec(memory_space=pl.ANY),
            ],
            out_specs=pl.BlockSpec((TB_EXP, D_PACK), row_block),
            scratch_shapes=[
                pltpu.VMEM((2, D, D_FF_E), F32), pltpu.VMEM((2, D, D_FF_E), F32), pltpu.VMEM((2, D_FF_E, D), F32),
                pltpu.VMEM((D, 2 * D_FF_E), BF16), pltpu.VMEM((D_FF_E, D), BF16),
                pltpu.SemaphoreType.DMA((2, 3)),
            ],
        ),
        out_shape=jax.ShapeDtypeStruct((R, D_PACK), U32),
        compiler_params=_params("arbitrary"),
    )(blk_map, blk_exp, blk_valid, blk_slot, blk_next, n_used, x_rows, w1, w3, w2)


def _combine_kernel(src_ref, ngroup_ref, x1_ref, route_ref, off_ref, rows_ref, o_ref, buf, sem, *, tm):
    i = pl.program_id(0)
    cur = i % 2

    def slot_copy(tile, half, s):
        src = pl.multiple_of(src_ref[tile * RUN_SLOTS + s], RUN_ROWS)
        dst = buf.at[half, pl.ds(pl.multiple_of(s * RUN_ROWS, RUN_ROWS), RUN_ROWS)]
        return pltpu.make_async_copy(rows_ref.at[pl.ds(src, RUN_ROWS)], dst, sem.at[half])

    def fetch(tile, half):
        def issue(g, c):
            for u in range(RUN_GROUP):
                slot_copy(tile, half, g * RUN_GROUP + u).start(priority=u % 2)
            return c
        lax.fori_loop(0, ngroup_ref[tile], issue, 0)

    def wait(g, c):
        rows = RUN_GROUP * RUN_ROWS
        pltpu.make_async_copy(rows_ref.at[pl.ds(0, rows)], buf.at[cur, pl.ds(0, rows)], sem.at[cur]).wait()
        return c

    @pl.when(i == 0)
    def _():
        buf[...] = jnp.zeros_like(buf)
        fetch(0, 0)

    @pl.when(i + 1 < pl.num_programs(0))
    def _():
        fetch(i + 1, 1 - cur)

    route = route_ref[...]
    lane = lax.broadcasted_iota(jnp.int32, (tm, LANES), 1).astype(F32)
    off = off_ref[0]

    def buf_row(e_lane, r_lane):
        e = route[:, e_lane:e_lane + 1]
        base = jnp.sum(jnp.where(lane == e, off, 0.0), axis=-1, keepdims=True)
        return (base + route[:, r_lane:r_lane + 1]).astype(jnp.int32)

    col = lax.broadcasted_iota(jnp.int32, (tm, RUN_SLOTS * RUN_ROWS), 1)
    pick = (jnp.where(col == buf_row(ROUTE_E1, ROUTE_R1), route[:, ROUTE_G1:ROUTE_G1 + 1], 0.0)
            + jnp.where(col == buf_row(ROUTE_E2, ROUTE_R2), route[:, ROUTE_G2:ROUTE_G2 + 1], 0.0))
    lax.fori_loop(0, ngroup_ref[i], wait, 0)
    o_ref[...] = x1_ref[...] + jnp.dot(pick.astype(BF16), _unpack_rows(buf[cur]), preferred_element_type=F32)


def _combine(src, n_groups, x1, route, off, y_rows):
    N, D = x1.shape
    tm = min(TM_MOVE, N)
    return pl.pallas_call(
        functools.partial(_combine_kernel, tm=tm),
        grid_spec=pltpu.PrefetchScalarGridSpec(
            num_scalar_prefetch=2,
            grid=(N // tm,),
            in_specs=[
                pl.BlockSpec((tm, D), lambda i, s, n: (i, 0)),
                pl.BlockSpec((tm, LANES), lambda i, s, n: (i, 0)),
                pl.BlockSpec((1, 1, LANES), lambda i, s, n: (i, 0, 0)),
                pl.BlockSpec(memory_space=pl.ANY),
            ],
            out_specs=pl.BlockSpec((tm, D), lambda i, s, n: (i, 0)),
            scratch_shapes=[pltpu.VMEM((2, RUN_SLOTS * RUN_ROWS, D_PACK), U32), pltpu.SemaphoreType.DMA((2,))],
        ),
        out_shape=jax.ShapeDtypeStruct((N, D), F32),
        compiler_params=_params("arbitrary"),
    )(src, n_groups, x1, route, off, y_rows)


def _moe(x1, h2, route, route_t, counts, sub_carry, w1, w3, w2, layer):
    N, _ = x1.shape
    tm = min(TM_MOVE, N)
    n_tiles = N // tm
    experts = jnp.arange(N_EXPERTS, dtype=jnp.int32)
    nb = (2 * N + N_EXPERTS * (TB_EXP - 1)) // TB_EXP + 1
    e_id = route_t[ROUTE_E1:ROUTE_E2 + 1].astype(jnp.int32)
    rank = route_t[ROUTE_R1:ROUTE_R2 + 1].astype(jnp.int32)
    cnt = counts[0, :N_EXPERTS].astype(jnp.int32)
    padded = (cnt + TB_EXP - 1) // TB_EXP * TB_EXP
    pad_end = jnp.cumsum(padded)
    pad_start = pad_end - padded
    e_flat = e_id.reshape(1, 2 * N)
    dest = jnp.sum(jnp.where(e_flat == experts[:, None], pad_start[:, None], 0), axis=0) + rank.reshape(2 * N)
    x_rows = _dispatch(dest, h2, nb * TB_EXP)
    n_used = pad_end[-1] // TB_EXP
    blk_map = jnp.minimum(jnp.arange(nb, dtype=jnp.int32), jnp.maximum(n_used - 1, 0))
    blk_exp = jnp.sum(pad_end[None, :] <= (blk_map * TB_EXP)[:, None], axis=1).astype(jnp.int32)
    blk_exp = jnp.minimum(blk_exp, N_EXPERTS - 1)
    is_blk_e = blk_exp[:, None] == experts[None, :]
    blk_valid = jnp.clip(jnp.sum(jnp.where(is_blk_e, (pad_start + cnt)[None, :], 0), axis=1) - blk_map * TB_EXP,
                         0, TB_EXP).astype(jnp.int32)
    has_rows = cnt > 0
    slot_of_e = (jnp.cumsum(has_rows.astype(jnp.int32)) - 1) % 2
    later = jnp.where(has_rows, experts, N_EXPERTS)
    next_of_e = lax.cummin(jnp.concatenate([later[1:], jnp.full((1,), N_EXPERTS, jnp.int32)]), reverse=True)
    next_of_e = jnp.where(next_of_e < N_EXPERTS, next_of_e, -1)
    blk_slot = jnp.sum(jnp.where(is_blk_e, slot_of_e[None, :], 0), axis=1).astype(jnp.int32)
    blk_next = jnp.sum(jnp.where(is_blk_e, next_of_e[None, :], 0), axis=1).astype(jnp.int32)

    per_router_tile = TM_OUT // tm if N >= TM_OUT else 1
    carry = sub_carry.reshape(-1, SUBLANES, LANES)[:, :per_router_tile, :N_EXPERTS].reshape(n_tiles, N_EXPERTS)
    carry = carry.astype(jnp.int32)
    tile_cnt = jnp.concatenate([carry[1:], cnt[None, :]], axis=0) - carry
    run_start = pad_start[None, :] + carry
    first_blk = run_start // RUN_ROWS
    nslot = jnp.where(tile_cnt > 0, (run_start + tile_cnt - 1) // RUN_ROWS - first_blk + 1, 0)
    slot_end = jnp.cumsum(nslot, axis=1)
    slot_base = slot_end - nslot
    slots = jnp.arange(RUN_SLOTS, dtype=jnp.int32)
    slot_e = jnp.minimum(jnp.sum(slot_end[:, None, :] <= slots[None, :, None], axis=2), N_EXPERTS - 1)
    is_slot_e = slot_e[:, :, None] == experts[None, None, :]
    pick = lambda tbl: jnp.sum(jnp.where(is_slot_e, tbl[:, None, :], 0), axis=2)
    src = (pick(first_blk) + slots[None, :] - pick(slot_base)) * RUN_ROWS
    src = jnp.where(slots[None, :] < slot_end[:, -1:], src, 0)
    src = jnp.clip(src, 0, nb * TB_EXP - RUN_ROWS).reshape(-1).astype(jnp.int32)
    n_groups = (slot_end[:, -1] + RUN_GROUP - 1) // RUN_GROUP
    off = (slot_base - first_blk) * RUN_ROWS + pad_start[None, :]
    off = jnp.pad(off.astype(F32), ((0, 0), (0, LANES - N_EXPERTS)))

    y_rows = _experts(blk_map, blk_exp, blk_valid, blk_slot, blk_next, n_used.reshape(1).astype(jnp.int32),
                      x_rows, w1, w3, w2, layer)
    return _combine(src, n_groups.astype(jnp.int32), x1, route, off.reshape(n_tiles, 1, LANES), y_rows)


def _pad_heads(w, width):
    lead = w.shape[:-1]
    w = w.reshape(lead + (N_HEADS, width))
    pad = [(0, 0)] * (w.ndim - 1) + [(0, HEAD_PAD - width)]
    return jnp.pad(w, pad).reshape(lead + (N_HEADS * HEAD_PAD,))


def _swap_rope(w):
    half = QK_ROPE // 2
    lo, hi = w[..., QK_NOPE:QK_NOPE + half], w[..., QK_NOPE + half:QK_HEAD]
    pad = [(0, 0)] * (w.ndim - 1)
    return jnp.pad(jnp.concatenate([hi, lo], -1), pad + [(QK_NOPE, HEAD_PAD - QK_HEAD)])


def _rope_tables(positions):
    inv_freq = ROPE_THETA ** (-jnp.arange(0, QK_ROPE, 2, dtype=F32) / QK_ROPE)
    ang = positions.astype(F32)[..., None] * inv_freq
    table = jnp.concatenate([jnp.cos(ang), jnp.sin(ang)], -1)
    return jnp.pad(table, ((0, 0), (0, 0), (QK_NOPE, LANES - QK_HEAD)))


def _ssm_params(lam_re, lam_im, b_re, b_im, c_re, c_im, log_dt):
    depth = lam_re.shape[0]
    lam = lax.complex(lam_re, lam_im)
    dt = jnp.exp(log_dt)[..., None]
    lam_bar = jnp.exp(lam * dt)
    b_bar = ((lam_bar - 1.0) / lam)[..., None] * lax.complex(b_re, b_im)
    eye = jnp.eye(SSM_GROUPS, dtype=F32)

    def in_blockdiag(m):
        return jnp.einsum("lgpc,gh->lgchp", m, eye).reshape(depth, SSM_WIDTH, SSM_FLAT)

    def out_blockdiag(m):
        return jnp.einsum("lgcp,gh->lgphc", m, eye).reshape(depth, SSM_FLAT, SSM_WIDTH)

    bbd = jnp.concatenate([in_blockdiag(jnp.real(b_bar)), in_blockdiag(jnp.imag(b_bar))], axis=2)
    cbd = jnp.concatenate([out_blockdiag(c_re), out_blockdiag(-c_im)], axis=1)
    lam_rows = jnp.stack([jnp.real(lam_bar).reshape(depth, SSM_FLAT), jnp.imag(lam_bar).reshape(depth, SSM_FLAT)],
                         axis=1)
    return bbd.astype(BF16), lam_rows, cbd.astype(BF16)


def kernel(x, positions, mix_norm, w_in, q_a_norm, w_uq, kv_a_norm, w_ukv, q_norm, k_norm, ssm_lam_re, ssm_lam_im, ssm_b_re, ssm_b_im, ssm_c_re, ssm_c_im, ssm_d, ssm_log_dt, ssm_w_glu, conv_dw_w, conv_dw_b, conv_ln_w, conv_ln_b, out_norm, w_out, ffn_norm, w_grp, b_grp, w_exp, b_exp, w1, w3, w2):
    B, L, D = x.shape
    depth = w_in.shape[0]
    rope = _rope_tables(positions)
    rows3 = lambda v: v[:, None, :]
    lane_tail = lambda v: jnp.pad(v, [(0, 0)] * (v.ndim - 1) + [(0, LANES - v.shape[-1])])
    c_q, c_kv, k_pe, u_s, c_a, c_g = jnp.split(
        w_in, [Q_LORA, Q_LORA + KV_LORA, Q_LORA + KV_LORA + QK_ROPE, Q_LORA + KV_LORA + QK_ROPE + SSM_WIDTH,
               Q_LORA + KV_LORA + QK_ROPE + SSM_WIDTH + CONV_WIDTH], axis=2)
    k_pe_full = jnp.pad(k_pe, ((0, 0), (0, 0), (QK_NOPE, 0)))
    win = jnp.concatenate([c_q, c_kv, lane_tail(k_pe_full), _swap_rope(k_pe_full), u_s, c_a, c_g], axis=2).astype(BF16)
    wkv = w_ukv.reshape(depth, KV_LORA, N_HEADS, QK_NOPE + V_HEAD)
    wuk = _pad_heads(wkv[..., :QK_NOPE].reshape(depth, KV_LORA, N_HEADS * QK_NOPE), QK_NOPE).astype(BF16)
    wuv = wkv[..., QK_NOPE:].reshape(depth, KV_LORA, ATTN_WIDTH).astype(BF16)
    wuq_sw = _swap_rope(w_uq.reshape(depth, Q_LORA, N_HEADS, QK_HEAD)).reshape(depth, Q_LORA, N_HEADS * HEAD_PAD)
    wuq = jnp.concatenate([_pad_heads(w_uq, QK_HEAD), wuq_sw], axis=2).astype(BF16)
    norm_rows = lambda w: jnp.stack([lane_tail(w), _swap_rope(w)], axis=1)
    qn, kn = norm_rows(q_norm * (QK_HEAD ** -0.5)), norm_rows(k_norm)
    bbd, lam_rows, cbd = _ssm_params(ssm_lam_re, ssm_lam_im, ssm_b_re, ssm_b_im, ssm_c_re, ssm_c_im, ssm_log_dt)
    ssm_skip = rows3(ssm_d.reshape(depth, SSM_WIDTH))
    wglu = ssm_w_glu.astype(BF16)
    g_attn = rows3(out_norm[:, :ATTN_WIDTH])
    g_ssm = rows3(out_norm[:, ATTN_WIDTH:ATTN_WIDTH + SSM_WIDTH])
    conv_params = (conv_dw_w, rows3(conv_dw_b), rows3(conv_ln_w), rows3(conv_ln_b),
                   rows3(out_norm[:, ATTN_WIDTH + SSM_WIDTH:]))
    w_route = lane_tail(jnp.concatenate([w_exp, w_grp], axis=2))
    w_route_hi = w_route.astype(BF16)
    w_route = jnp.concatenate([w_route_hi, (w_route - w_route_hi.astype(F32)).astype(BF16)], axis=2)
    b_route = rows3(lane_tail(jnp.concatenate([b_exp, b_grp], axis=1)))
    wo = w_out.astype(BF16)
    for l in range(depth):
        q, k, v, u_ssm, cu = _in_proj(x, rope, rows3(mix_norm), win, rows3(q_a_norm), wuq, rows3(kv_a_norm), wuk, wuv,
                                         qn, kn, l)
        y_attn = _attention(q, k, v)
        y_ssm = _ssm(u_ssm, bbd, lam_rows, cbd, ssm_skip, wglu, g_ssm, B, l)
        x1, h2, route, route_t, counts, sub_carry = _out_router(
            x.reshape(B * L, D), y_attn.reshape(B * L, ATTN_WIDTH), y_ssm, cu.reshape(B * L, CONV_WIDTH), g_attn, wo, rows3(ffn_norm),
            w_route, b_route, conv_params, B, l)
        x = _moe(x1, h2, route, route_t, counts, sub_carry, w1, w3, w2, l).reshape(B, L, D)
    return x
```

```python
import functools

import jax
import jax.numpy as jnp
from jax import lax
from jax.experimental import pallas as pl
from jax.experimental.pallas import tpu as pltpu
from jax.experimental.pallas import tpu_sc as plsc

D_MODEL = 1024
CHUNK = 64
EPS = 1e-6
LN_EPS = 1e-5
N_HEADS = 8
QK_NOPE = 64
QK_ROPE = 32
QK_HEAD = QK_NOPE + QK_ROPE
V_HEAD = 64
Q_LORA = 256
KV_LORA = 128
ROPE_THETA = 10000.0
ATTN_WIDTH = N_HEADS * V_HEAD
SSM_WIDTH = 256
SSM_GROUP = 16
SSM_GROUPS = SSM_WIDTH // SSM_GROUP
SSM_STATE = 64
SSM_FLAT = SSM_GROUPS * SSM_STATE
CONV_WIDTH = 256
CONV_K = 31
N_EGROUPS = 4
EXP_PER_GROUP = 8
N_EXPERTS = N_EGROUPS * EXP_PER_GROUP
D_FF_E = 512

LANES = 128
SUBLANES = 8
HEAD_PAD = LANES
IN_PROJ_PAD = Q_LORA + KV_LORA + SSM_WIDTH + 2 * CONV_WIDTH + 2 * LANES
CONV_HALO = 32
VMEM_LIMIT = 48 * 1024 * 1024

TM_PROJ = 512
TQ_ATTN = 256
TC_SSM = 128
SSM_PARTS = 2
CONV_SUB = 512
TM_OUT = 512
TM_MOVE = 256
RUN_ROWS = SUBLANES
RUN_SLOTS = 2 * N_EXPERTS + 2 * TM_MOVE // RUN_ROWS
RUN_GROUP = 8
SC_CORES, SC_SUBCORES = 2, 16
SC_CHUNK = 64
TB_EXP = 512
EXPERT_TAIL_DIVISORS = (1, 2, 4)

BF16 = jnp.bfloat16
F32 = jnp.float32
U32 = jnp.uint32
D_PACK = D_MODEL // 2


def _pack_rows(v):
    bits = lax.bitcast_convert_type(v.astype(BF16).astype(F32), U32)
    half = v.shape[1] // 2
    return bits[:, 0:half] | (bits[:, half:] >> 16)


def _unpack_rows(w):
    hi = lax.bitcast_convert_type(w & jnp.uint32(0xFFFF0000), F32)
    lo = lax.bitcast_convert_type(w << 16, F32)
    return jnp.concatenate([hi, lo], axis=1).astype(BF16)


def _rms(x, w):
    return x * lax.rsqrt(jnp.mean(x * x, axis=-1, keepdims=True) + EPS) * w


def _params(*sem):
    return pltpu.CompilerParams(dimension_semantics=sem, vmem_limit_bytes=VMEM_LIMIT)


def _in_proj_kernel(x_ref, rope_ref, mixn_ref, win_ref, qan_ref, wuq_ref, kvan_ref, wuk_ref, wuv_ref,
                    qn_ref, kn_ref, q_ref, k_ref, v_ref, u_ref, cu_ref):
    x = x_ref[0]
    h = _rms(x, mixn_ref[0]).astype(BF16)
    head_cols = Q_LORA + KV_LORA + 2 * LANES
    proj = jnp.dot(h, win_ref[0, :, 0:head_cols], preferred_element_type=F32)
    o = 0
    c_q = proj[:, o:o + Q_LORA]; o += Q_LORA
    c_kv = proj[:, o:o + KV_LORA]; o += KV_LORA
    k_pe = proj[:, o:o + LANES]; o += LANES
    k_pe_sw = proj[:, o:o + LANES]

    width = N_HEADS * HEAD_PAD
    q2 = jnp.dot(_rms(c_q, qan_ref[0]).astype(BF16), wuq_ref[0], preferred_element_type=F32)
    q, q_sw = q2[:, 0:width], q2[:, width:2 * width]
    ckv_n = _rms(c_kv, kvan_ref[0]).astype(BF16)
    kn = jnp.dot(ckv_n, wuk_ref[0], preferred_element_type=F32)
    v_ref[0] = jnp.dot(ckv_n, wuv_ref[0], preferred_element_type=F32).astype(BF16)

    rope = rope_ref[0]
    half = QK_ROPE // 2
    lane = lax.broadcasted_iota(jnp.int32, rope.shape, 1)
    lo = (lane >= QK_NOPE) & (lane < QK_NOPE + half)
    hi = (lane >= QK_NOPE + half) & (lane < QK_HEAD)
    cos_t = jnp.where(lane < QK_NOPE, 1.0, jnp.where(lo, rope, jnp.where(hi, pltpu.roll(rope, half, 1), 0.0)))
    sin_t = jnp.where(lo, -pltpu.roll(rope, LANES - half, 1), jnp.where(hi, rope, 0.0))
    a_q, b_q = qn_ref[0, 0:1, :] * cos_t, qn_ref[0, 1:2, :] * sin_t
    a_k = kn_ref[0, 0:1, :] * cos_t
    k_sw_term = k_pe_sw * (kn_ref[0, 1:2, :] * sin_t)

    def inv_rms(y):
        return lax.rsqrt(jnp.sum(y * y, axis=-1, keepdims=True) * (1.0 / QK_HEAD) + EPS)

    def rest(j):
        cols = slice(head_cols + j * SSM_WIDTH, head_cols + (j + 1) * SSM_WIDTH)
        return jnp.dot(h, win_ref[0, :, cols], preferred_element_type=F32)

    for hd in range(N_HEADS):
        if hd == 0:
            u_ref[...] = rest(0).astype(BF16)
        elif hd == 4:
            cu_ref[0] = (rest(1) * jax.nn.sigmoid(rest(2))).astype(BF16)
        sl = slice(hd * HEAD_PAD, (hd + 1) * HEAD_PAD)
        yq = q[:, sl]
        q_ref[0, hd] = ((yq * a_q + q_sw[:, sl] * b_q) * inv_rms(yq)).astype(BF16)
        yk = kn[:, sl] + k_pe
        k_ref[0, hd] = ((yk * a_k + k_sw_term) * inv_rms(yk)).astype(BF16)


def _in_proj(x, rope, mixn, win, qan, wuq, kvan, wuk, wuv, qn, kn, layer):
    B, L, D = x.shape
    tm = min(TM_PROJ, L)
    full = lambda shape: pl.BlockSpec((1,) + shape, lambda b, t: (layer,) + (0,) * len(shape))
    return pl.pallas_call(
        _in_proj_kernel,
        grid=(B, L // tm),
        in_specs=[
            pl.BlockSpec((1, tm, D), lambda b, t: (b, t, 0)),
            pl.BlockSpec((1, tm, LANES), lambda b, t: (b, t, 0)),
            full((1, D)), full((D, IN_PROJ_PAD)), full((1, Q_LORA)), full((Q_LORA, 2 * N_HEADS * HEAD_PAD)),
            full((1, KV_LORA)), full((KV_LORA, N_HEADS * HEAD_PAD)), full((KV_LORA, ATTN_WIDTH)),
            full((2, HEAD_PAD)), full((2, HEAD_PAD)),
        ],
        out_specs=[
            pl.BlockSpec((1, N_HEADS, tm, HEAD_PAD), lambda b, t: (b, 0, t, 0)),
            pl.BlockSpec((1, N_HEADS, tm, HEAD_PAD), lambda b, t: (b, 0, t, 0)),
            pl.BlockSpec((1, tm, ATTN_WIDTH), lambda b, t: (b, t, 0)),
            pl.BlockSpec((tm, SSM_WIDTH), lambda b, t: (t, b)),
            pl.BlockSpec((1, tm, CONV_WIDTH), lambda b, t: (b, t, 0)),
        ],
        out_shape=[
            jax.ShapeDtypeStruct((B, N_HEADS, L, HEAD_PAD), BF16),
            jax.ShapeDtypeStruct((B, N_HEADS, L, HEAD_PAD), BF16),
            jax.ShapeDtypeStruct((B, L, ATTN_WIDTH), BF16),
            jax.ShapeDtypeStruct((L, B * SSM_WIDTH), BF16),
            jax.ShapeDtypeStruct((B, L, CONV_WIDTH), BF16),
        ],
        compiler_params=_params("parallel", "parallel"),
    )(x, rope, mixn, win, qan, wuq, kvan, wuk, wuv, qn, kn)


def _attention_kernel(q_ref, k_ref, v_ref, o_ref, vext_ref, *, seq, tq):
    pair = 2 * V_HEAD
    vext_ref[:, 0:pair] = v_ref[0]
    vext_ref[:, pair:] = jnp.ones((seq, LANES), BF16)
    row_chunk = lax.broadcasted_iota(jnp.int32, (tq, tq), 0) // CHUNK
    col_chunk = lax.broadcasted_iota(jnp.int32, (tq, tq), 1) // CHUNK
    visible = col_chunk <= row_chunk
    neg = jnp.finfo(F32).min
    nt = (((1,), (1,)), ((), ()))
    low_lanes = lax.broadcasted_iota(jnp.int32, (tq, pair), 1) < V_HEAD

    def scores(i, hh):
        q0 = i * tq
        qb = q_ref[0, hh, q0:q0 + tq, :]
        s_d = lax.dot_general(qb, k_ref[0, hh, q0:q0 + tq, :], nt, preferred_element_type=F32)
        s_d = jnp.where(visible, s_d, neg)
        s_l = lax.dot_general(qb, k_ref[0, hh, 0:q0, :], nt, preferred_element_type=F32) if i > 0 else None
        return s_d, s_l

    def weighted_values(i, s_d, s_l):
        q0 = i * tq
        m = jnp.max(s_d, axis=-1, keepdims=True)
        if s_l is not None:
            m = jnp.maximum(m, jnp.max(s_l, axis=-1, keepdims=True))
        acc = jnp.dot(jnp.exp(s_d - m).astype(BF16), vext_ref[q0:q0 + tq, :], preferred_element_type=F32)
        if s_l is not None:
            acc = acc + jnp.dot(jnp.exp(s_l - m).astype(BF16), vext_ref[0:q0, :], preferred_element_type=F32)
        return acc[:, 0:pair] / acc[:, pair:]

    nq = seq // tq
    order = [x for p in zip(reversed(range(nq)), range(nq)) for x in p][:nq]
    chains = [(i, hh) for i in order for hh in range(2)]
    pending = scores(*chains[0])
    outs = {}
    for n, (i, hh) in enumerate(chains):
        upcoming = scores(*chains[n + 1]) if n + 1 < len(chains) else None
        outs[hh] = weighted_values(i, *pending)
        pending = upcoming
        if hh == 1:
            o_ref[0, i * tq:(i + 1) * tq, :] = jnp.where(low_lanes, outs[0], outs[1]).astype(BF16)


def _attention(q, k, v):
    B, H, L, _ = q.shape
    tq = min(TQ_ATTN, L)
    return pl.pallas_call(
        functools.partial(_attention_kernel, seq=L, tq=tq),
        grid=(B, H // 2),
        in_specs=[
            pl.BlockSpec((1, 2, L, HEAD_PAD), lambda b, p: (b, p, 0, 0)),
            pl.BlockSpec((1, 2, L, HEAD_PAD), lambda b, p: (b, p, 0, 0)),
            pl.BlockSpec((1, L, 2 * V_HEAD), lambda b, p: (b, 0, p)),
        ],
        out_specs=pl.BlockSpec((1, L, 2 * V_HEAD), lambda b, p: (b, 0, p)),
        out_shape=jax.ShapeDtypeStruct((B, L, ATTN_WIDTH), BF16),
        scratch_shapes=[pltpu.VMEM((L, 2 * V_HEAD + LANES), BF16)],
        compiler_params=_params("parallel", "parallel"),
    )(q, k, v)


def _ssm_kernel(u_ref, bbd_ref, lam_ref, cbd_ref, d_ref, wglu_ref, g_ref, o_ref, state_ref, in_s, out_s, *bu_refs,
                batch, tc):
    @pl.when(pl.program_id(0) == 0)
    def _():
        state_ref[...] = jnp.zeros_like(state_ref)

    steps = tc // SSM_PARTS
    halves = tuple((bu, slice(p * steps * batch, (p + 1) * steps * batch)) for p, bu in enumerate(bu_refs))
    for b in range(batch):
        for hl in range(SSM_WIDTH // LANES):
            lanes = slice(b * SSM_WIDTH + hl * LANES, b * SSM_WIDTH + (hl + 1) * LANES)
            in_s[hl, pl.ds(b, tc, stride=batch), :] = u_ref[:, lanes].astype(F32)

    def u_rows(rows):
        return jnp.concatenate([in_s[hl, rows, :] for hl in range(SSM_WIDTH // LANES)], axis=1)

    for bu_ref, rows in halves:
        bu_ref[...] = jnp.dot(u_rows(rows).astype(BF16), bbd_ref[0], preferred_element_type=F32)
    lam_re = jnp.broadcast_to(lam_ref[0, 0:1, :], (batch, SSM_FLAT))
    lam_im = jnp.broadcast_to(lam_ref[0, 1:2, :], (batch, SSM_FLAT))

    carry = (state_ref[:, 0:SSM_FLAT], state_ref[:, SSM_FLAT:2 * SSM_FLAT])
    for bu_ref, rows in halves:
        def step(t, xs, bu_ref=bu_ref):
            xr, xi = xs
            at = pl.ds(pl.multiple_of(t * batch, batch), batch)
            nr = lam_re * xr - lam_im * xi + bu_ref[at, 0:SSM_FLAT]
            ni = lam_re * xi + lam_im * xr + bu_ref[at, SSM_FLAT:2 * SSM_FLAT]
            bu_ref[at, 0:SSM_FLAT] = nr
            bu_ref[at, SSM_FLAT:2 * SSM_FLAT] = ni
            return nr, ni

        carry = lax.fori_loop(0, steps, step, carry, unroll=True)
        y = jnp.dot(bu_ref[...].astype(BF16), cbd_ref[0], preferred_element_type=F32)
        y = y + d_ref[0] * u_rows(rows)
        z = jax.nn.gelu(y)
        gate = jax.nn.sigmoid(jnp.dot(z.astype(BF16), wglu_ref[0], preferred_element_type=F32))
        res = _rms(z * gate, g_ref[0])
        for hl in range(SSM_WIDTH // LANES):
            out_s[hl, rows, :] = res[:, hl * LANES:(hl + 1) * LANES]
    state_ref[:, 0:SSM_FLAT] = carry[0]
    state_ref[:, SSM_FLAT:2 * SSM_FLAT] = carry[1]
    for b in range(batch):
        for hl in range(SSM_WIDTH // LANES):
            lanes = slice(b * SSM_WIDTH + hl * LANES, b * SSM_WIDTH + (hl + 1) * LANES)
            o_ref[:, lanes] = out_s[hl, pl.ds(b, tc, stride=batch), :].astype(BF16)


def _ssm(u, bbd, lam, cbd, d, wglu, g, batch, layer):
    seq = u.shape[0]
    tc = min(TC_SSM, seq)
    blk = tc * batch
    full = lambda shape: pl.BlockSpec((1,) + shape, lambda t: (layer,) + (0,) * len(shape))
    return pl.pallas_call(
        functools.partial(_ssm_kernel, batch=batch, tc=tc),
        grid=(seq // tc,),
        in_specs=[
            pl.BlockSpec((tc, batch * SSM_WIDTH), lambda t: (t, 0)),
            full((SSM_WIDTH, 2 * SSM_FLAT)), full((2, SSM_FLAT)), full((2 * SSM_FLAT, SSM_WIDTH)),
            full((1, SSM_WIDTH)), full((SSM_WIDTH, SSM_WIDTH)), full((1, SSM_WIDTH)),
        ],
        out_specs=pl.BlockSpec((tc, batch * SSM_WIDTH), lambda t: (t, 0)),
        out_shape=jax.ShapeDtypeStruct((seq, batch * SSM_WIDTH), BF16),
        scratch_shapes=[pltpu.VMEM((batch, 2 * SSM_FLAT), F32), pltpu.VMEM((SSM_WIDTH // LANES, blk, LANES), F32),
                        pltpu.VMEM((SSM_WIDTH // LANES, blk, LANES), F32)]
        + [pltpu.VMEM((blk // SSM_PARTS, 2 * SSM_FLAT), F32)] * SSM_PARTS,
        compiler_params=_params("arbitrary"),
    )(u, bbd, lam, cbd, d, wglu, g)


def _conv_rows(upad_ref, r0, w_ref, b_ref, lnw_ref, lnb_ref, gn_ref):
    base = CONV_HALO - (CONV_K - 1)
    acc = jnp.zeros((CONV_SUB, CONV_WIDTH), F32)
    for off in range(SUBLANES):
        n = CONV_SUB + (SUBLANES if off else 0)
        part = None
        for a8 in range(0, CONV_HALO + 1, SUBLANES):
            kk = a8 + off - base
            if 0 <= kk < CONV_K:
                term = w_ref[0, kk:kk + 1, :] * upad_ref[r0 + a8:r0 + a8 + n, :]
                part = term if part is None else part + term
        acc = acc + part[off:off + CONV_SUB]
    y = acc + b_ref[0]
    mu = jnp.mean(y, axis=-1, keepdims=True)
    var = jnp.mean(jnp.square(y - mu), axis=-1, keepdims=True)
    y = (y - mu) * lax.rsqrt(var + LN_EPS) * lnw_ref[0] + lnb_ref[0]
    y = y * jax.nn.sigmoid(y)
    return _rms(y, gn_ref[0])


ROUTE_E1, ROUTE_E2, ROUTE_R1, ROUTE_R2, ROUTE_G1, ROUTE_G2 = range(6)
ROUTE_ROWS = 8
GRP_LANE0 = N_EXPERTS


def _out_router_kernel(x_ref, ya_ref, ys_ref, cu_ref, ga_ref, wo_ref, fn_ref, wr_ref, br_ref,
                       cw_ref, cb_ref, clnw_ref, clnb_ref, cgn_ref,
                       x1_ref, h2_ref, route_ref, route_t_ref, cnt_ref, sub_ref,
                       carry_ref, tri_ref, upad_ref, yc_ref, *, tm, per_b):
    @pl.when(pl.program_id(0) == 0)
    def _():
        carry_ref[...] = jnp.zeros_like(carry_ref)
        rr = lax.broadcasted_iota(jnp.int32, (tm, tm), 0)
        cc = lax.broadcasted_iota(jnp.int32, (tm, tm), 1)
        tri_ref[...] = (cc < rr).astype(BF16)

    first = pl.program_id(0) % per_b == 0
    upad_ref[0:CONV_HALO, :] = jnp.where(first, 0.0, upad_ref[tm:tm + CONV_HALO, :])
    upad_ref[CONV_HALO:, :] = cu_ref[...].astype(F32)
    ya = _rms(ya_ref[...].astype(F32), ga_ref[0]).astype(BF16)
    acc = jnp.dot(ya, wo_ref[0, 0:ATTN_WIDTH, :], preferred_element_type=F32)
    acc += jnp.dot(ys_ref[...], wo_ref[0, ATTN_WIDTH:ATTN_WIDTH + SSM_WIDTH, :], preferred_element_type=F32)
    for r0 in range(0, tm, CONV_SUB):
        yc_ref[r0:r0 + CONV_SUB, :] = _conv_rows(upad_ref, r0, cw_ref, cb_ref, clnw_ref, clnb_ref, cgn_ref).astype(BF16)
    acc += jnp.dot(yc_ref[...], wo_ref[0, ATTN_WIDTH + SSM_WIDTH:, :], preferred_element_type=F32)
    x1 = x_ref[...] + acc
    x1_ref[...] = x1
    h2 = _rms(x1, fn_ref[0])
    h2_ref[...] = _pack_rows(h2)

    h_hi = h2.astype(BF16)
    h_lo = (h2 - h_hi.astype(F32)).astype(BF16)
    part = jnp.dot(h_hi, wr_ref[0], preferred_element_type=F32)
    logits = (part[:, 0:LANES] + part[:, LANES:2 * LANES]
              + jnp.dot(h_lo, wr_ref[0, :, 0:LANES], preferred_element_type=F32) + br_ref[0])
    lane = lax.broadcasted_iota(jnp.int32, (tm, LANES), 1)
    ninf = -jnp.inf
    big = LANES

    def first_argmax(vals, vmax):
        return jnp.min(jnp.where(vals == vmax, lane, big), axis=-1, keepdims=True)

    grp = jnp.where((lane >= GRP_LANE0) & (lane < GRP_LANE0 + N_EGROUPS), logits, ninf)
    gmax = jnp.max(grp, axis=-1, keepdims=True)
    gsel = first_argmax(grp, gmax) - GRP_LANE0
    p_grp = 1.0 / jnp.sum(jnp.exp(grp - gmax), axis=-1, keepdims=True)

    el = jnp.where((lane < N_EXPERTS) & ((lane // EXP_PER_GROUP) == gsel), logits, ninf)
    m1 = jnp.max(el, axis=-1, keepdims=True)
    e1 = first_argmax(el, m1)
    el2 = jnp.where(lane == e1, ninf, el)
    m2 = jnp.max(el2, axis=-1, keepdims=True)
    e2 = first_argmax(el2, m2)
    t2 = jnp.exp(m2 - m1)
    g1 = p_grp / (1.0 + t2)
    g2 = p_grp * t2 / (1.0 + t2)

    hit1 = lane == e1
    hit2 = lane == e2
    cnt = (hit1 | hit2).astype(F32)
    before = jnp.dot(tri_ref[...], cnt.astype(BF16), preferred_element_type=F32) + carry_ref[...]
    r1 = jnp.sum(jnp.where(hit1, before, 0.0), axis=-1, keepdims=True)
    r2 = jnp.sum(jnp.where(hit2, before, 0.0), axis=-1, keepdims=True)
    subs = [before[h * TM_MOVE:h * TM_MOVE + 1, :] for h in range(tm // TM_MOVE)]
    sub_ref[...] = jnp.concatenate(subs + [jnp.zeros((SUBLANES - len(subs), LANES), F32)], axis=0)
    carry_ref[...] += jnp.sum(cnt, axis=0, keepdims=True)
    cnt_ref[...] = carry_ref[...]

    rec = jnp.where(lane == ROUTE_E1, e1.astype(F32), 0.0)
    rec = jnp.where(lane == ROUTE_E2, e2.astype(F32), rec)
    rec = jnp.where(lane == ROUTE_R1, r1, rec)
    rec = jnp.where(lane == ROUTE_R2, r2, rec)
    rec = jnp.where(lane == ROUTE_G1, g1, rec)
    rec = jnp.where(lane == ROUTE_G2, g2, rec)
    route_ref[...] = rec
    route_t_ref[...] = rec.T[0:ROUTE_ROWS, :]


def _out_router(x, ya, ys, cu, ga, wo, fn, wr, br, conv_params, batch, layer):
    N, D = x.shape
    seq = N // batch
    tm = min(TM_OUT, seq)
    per_b = seq // tm
    full = lambda shape: pl.BlockSpec((1,) + shape, lambda i: (layer,) + (0,) * len(shape))
    whole = lambda shape: pl.BlockSpec(shape, lambda i: (0,) * len(shape))
    tile = lambda w: pl.BlockSpec((tm, w), lambda i: (i, 0))
    return pl.pallas_call(
        functools.partial(_out_router_kernel, tm=tm, per_b=per_b),
        grid=(N // tm,),
        in_specs=[
            tile(D), tile(ATTN_WIDTH),
            pl.BlockSpec((tm, SSM_WIDTH), lambda i: (i % per_b, i // per_b)),
            tile(CONV_WIDTH),
            full((1, ATTN_WIDTH)), full((D, D)), full((1, D)), full((D, 2 * LANES)), full((1, LANES)),
            full((CONV_K, CONV_WIDTH)), full((1, CONV_WIDTH)), full((1, CONV_WIDTH)), full((1, CONV_WIDTH)),
            full((1, CONV_WIDTH)),
        ],
        out_specs=[tile(D), tile(D_PACK), tile(LANES), pl.BlockSpec((ROUTE_ROWS, tm), lambda i: (0, i)),
                   whole((1, LANES)), pl.BlockSpec((SUBLANES, LANES), lambda i: (i, 0))],
        out_shape=[
            jax.ShapeDtypeStruct((N, D), F32),
            jax.ShapeDtypeStruct((N, D_PACK), U32),
            jax.ShapeDtypeStruct((N, LANES), F32),
            jax.ShapeDtypeStruct((ROUTE_ROWS, N), F32),
            jax.ShapeDtypeStruct((1, LANES), F32),
            jax.ShapeDtypeStruct((N // tm * SUBLANES, LANES), F32),
        ],
        scratch_shapes=[pltpu.VMEM((1, LANES), F32), pltpu.VMEM((tm, tm), BF16),
                        pltpu.VMEM((tm + CONV_HALO, CONV_WIDTH), F32), pltpu.VMEM((tm, CONV_WIDTH), BF16)],
        compiler_params=_params("arbitrary"),
    )(x, ya, ys, cu, ga, wo, fn, wr, br, *conv_params)


def _dispatch(dest_flat, h2, n_rows):
    N, D = h2.shape
    workers = SC_CORES * SC_SUBCORES
    per_worker = N // workers
    chunks = per_worker // SC_CHUNK
    idx = dest_flat.reshape(2, workers, chunks, SC_CHUNK).transpose(1, 0, 2, 3).reshape(workers, 2 * chunks, SC_CHUNK)
    mesh = plsc.VectorSubcoreMesh(core_axis_name="c", subcore_axis_name="s")

    @functools.partial(
        pl.kernel, mesh=mesh,
        out_type=jax.ShapeDtypeStruct((n_rows, D), h2.dtype),
        scratch_types=[pltpu.VMEM((2 * chunks, SC_CHUNK), jnp.int32), pltpu.VMEM((2, SC_CHUNK, D), h2.dtype),
                       pltpu.SemaphoreType.DMA((2,)), pltpu.SemaphoreType.DMA((2, 2))],
    )
    def scatter(h_hbm, idx_hbm, out_hbm, idx_v, rows_v, sem_in, sem_out):
        wid = lax.axis_index("s") * SC_CORES + lax.axis_index("c")
        pltpu.sync_copy(idx_hbm.at[wid], idx_v)

        def load(c):
            first = pl.multiple_of(wid * per_worker + c * SC_CHUNK, SC_CHUNK)
            return pltpu.async_copy(h_hbm.at[pl.ds(first, SC_CHUNK)], rows_v.at[c % 2], sem_in.at[c % 2])

        def store(c, k):
            return pltpu.async_copy(rows_v.at[c % 2], out_hbm.at[idx_v.at[k * chunks + c]], sem_out.at[c % 2, k])

        loads = {0: load(0)}
        stores = {}
        for c in range(chunks):
            loads[c].wait()
            if c >= 1:
                for st in stores[c - 1]:
                    st.wait()
            if c + 1 < chunks:
                loads[c + 1] = load(c + 1)
            stores[c] = (store(c, 0), store(c, 1))
        for st in stores[chunks - 1]:
            st.wait()

    return scatter(h2, idx)


def _experts_kernel(blk_ref, exp_ref, valid_ref, slot_ref, next_ref, used_ref, x_ref, w1_hbm, w3_hbm, w2_hbm, y_ref,
                    w1_f, w3_f, w2_f, w13_s, w2_s, sem, *, layer):
    j = pl.program_id(0)

    def weight_copies(e, slot):
        return (pltpu.make_async_copy(w1_hbm.at[layer, e], w1_f.at[slot], sem.at[slot, 0]),
                pltpu.make_async_copy(w3_hbm.at[layer, e], w3_f.at[slot], sem.at[slot, 1]),
                pltpu.make_async_copy(w2_hbm.at[layer, e], w2_f.at[slot], sem.at[slot, 2]))

    @pl.when(j < used_ref[0])
    def _():
        e = exp_ref[j]
        slot = slot_ref[j]

        @pl.when(j == 0)
        def _():
            for cp in weight_copies(e, slot):
                cp.start()

        @pl.when((j == 0) | (e != exp_ref[jnp.maximum(j - 1, 0)]))
        def _():
            for cp in weight_copies(e, slot):
                cp.wait()
            w13_s[:, 0:D_FF_E] = w1_f[slot].astype(BF16)
            w13_s[:, D_FF_E:] = w3_f[slot].astype(BF16)
            w2_s[...] = w2_f[slot].astype(BF16)

            @pl.when(next_ref[j] >= 0)
            def _():
                for cp in weight_copies(next_ref[j], 1 - slot):
                    cp.start(priority=1)

        valid = valid_ref[j]

        def ffn(rows):
            x = _unpack_rows(x_ref[0:rows, :])
            ab = jnp.dot(x, w13_s[...], preferred_element_type=F32)
            a = ab[:, 0:D_FF_E]
            hmid = (a * jax.nn.sigmoid(a) * ab[:, D_FF_E:]).astype(BF16)
            y = jnp.dot(hmid, w2_s[...], preferred_element_type=F32)
            row = lax.broadcasted_iota(jnp.int32, (rows, 1), 0)
            y_ref[0:rows, :] = jnp.where(row < valid, _pack_rows(y), jnp.uint32(0))
            if rows < TB_EXP:
                y_ref[rows:, :] = jnp.zeros((TB_EXP - rows, D_PACK), U32)

        sizes = [TB_EXP // d for d in EXPERT_TAIL_DIVISORS]
        for n, rows in enumerate(sizes):
            below = sizes[n + 1] if n + 1 < len(sizes) else 0

            @pl.when((valid > below) & (valid <= rows) if n else valid > below)
            def _(rows=rows):
                ffn(rows)


def _experts(blk_map, blk_exp, blk_valid, blk_slot, blk_next, n_used, x_rows, w1, w3, w2, layer):
    R, D = x_rows.shape[0], D_MODEL
    nb = R // TB_EXP
    row_block = lambda j, bm, be, bv, bs, bn, nu: (bm[j], 0)
    return pl.pallas_call(
        functools.partial(_experts_kernel, layer=layer),
        grid_spec=pltpu.PrefetchScalarGridSpec(
            num_scalar_prefetch=6,
            grid=(nb,),
            in_specs=[
                pl.BlockSpec((TB_EXP, D_PACK), row_block),
                pl.BlockSpec(memory_space=pl.ANY), pl.BlockSpec(memory_space=pl.ANY), pl.BlockSpec(memory_space=pl.ANY),
            ],
            out_specs=pl.BlockSpec((TB_EXP, D_PACK), row_block),
            scratch_shapes=[
                pltpu.VMEM((2, D, D_FF_E), F32), pltpu.VMEM((2, D, D_FF_E), F32), pltpu.VMEM((2, D_FF_E, D), F32),
                pltpu.VMEM((D, 2 * D_FF_E), BF16), pltpu.VMEM((D_FF_E, D), BF16),
                pltpu.SemaphoreType.DMA((2, 3)),
            ],
        ),
        out_shape=jax.ShapeDtypeStruct((R, D_PACK), U32),
        compiler_params=_params("arbitrary"),
    )(blk_map, blk_exp, blk_valid, blk_slot, blk_next, n_used, x_rows, w1, w3, w2)


def _combine_kernel(src_ref, ngroup_ref, x1_ref, route_ref, off_ref, rows_ref, o_ref, buf, sem, *, tm):
    i = pl.program_id(0)
    cur = i % 2

    def slot_copy(tile, half, s):
        src = pl.multiple_of(src_ref[tile * RUN_SLOTS + s], RUN_ROWS)
        dst = buf.at[half, pl.ds(pl.multiple_of(s * RUN_ROWS, RUN_ROWS), RUN_ROWS)]
        return pltpu.make_async_copy(rows_ref.at[pl.ds(src, RUN_ROWS)], dst, sem.at[half])

    def fetch(tile, half):
        def issue(g, c):
            for u in range(RUN_GROUP):
                slot_copy(tile, half, g * RUN_GROUP + u).start(priority=u % 2)
            return c
        lax.fori_loop(0, ngroup_ref[tile], issue, 0)

    def wait(g, c):
        rows = RUN_GROUP * RUN_ROWS
        pltpu.make_async_copy(rows_ref.at[pl.ds(0, rows)], buf.at[cur, pl.ds(0, rows)], sem.at[cur]).wait()
        return c

    @pl.when(i == 0)
    def _():
        buf[...] = jnp.zeros_like(buf)
        fetch(0, 0)

    @pl.when(i + 1 < pl.num_programs(0))
    def _():
        fetch(i + 1, 1 - cur)

    route = route_ref[...]
    lane = lax.broadcasted_iota(jnp.int32, (tm, LANES), 1).astype(F32)
    off = off_ref[0]

    def buf_row(e_lane, r_lane):
        e = route[:, e_lane:e_lane + 1]
        base = jnp.sum(jnp.where(lane == e, off, 0.0), axis=-1, keepdims=True)
        return (base + route[:, r_lane:r_lane + 1]).astype(jnp.int32)

    col = lax.broadcasted_iota(jnp.int32, (tm, RUN_SLOTS * RUN_ROWS), 1)
    pick = (jnp.where(col == buf_row(ROUTE_E1, ROUTE_R1), route[:, ROUTE_G1:ROUTE_G1 + 1], 0.0)
            + jnp.where(col == buf_row(ROUTE_E2, ROUTE_R2), route[:, ROUTE_G2:ROUTE_G2 + 1], 0.0))
    lax.fori_loop(0, ngroup_ref[i], wait, 0)
    o_ref[...] = x1_ref[...] + jnp.dot(pick.astype(BF16), _unpack_rows(buf[cur]), preferred_element_type=F32)


def _combine(src, n_groups, x1, route, off, y_rows):
    N, D = x1.shape
    tm = min(TM_MOVE, N)
    return pl.pallas_call(
        functools.partial(_combine_kernel, tm=tm),
        grid_spec=pltpu.PrefetchScalarGridSpec(
            num_scalar_prefetch=2,
            grid=(N // tm,),
            in_specs=[
                pl.BlockSpec((tm, D), lambda i, s, n: (i, 0)),
                pl.BlockSpec((tm, LANES), lambda i, s, n: (i, 0)),
                pl.BlockSpec((1, 1, LANES), lambda i, s, n: (i, 0, 0)),
                pl.BlockSpec(memory_space=pl.ANY),
            ],
            out_specs=pl.BlockSpec((tm, D), lambda i, s, n: (i, 0)),
            scratch_shapes=[pltpu.VMEM((2, RUN_SLOTS * RUN_ROWS, D_PACK), U32), pltpu.SemaphoreType.DMA((2,))],
        ),
        out_shape=jax.ShapeDtypeStruct((N, D), F32),
        compiler_params=_params("arbitrary"),
    )(src, n_groups, x1, route, off, y_rows)


def _moe(x1, h2, route, route_t, counts, sub_carry, w1, w3, w2, layer):
    N, _ = x1.shape
    tm = min(TM_MOVE, N)
    n_tiles = N // tm
    experts = jnp.arange(N_EXPERTS, dtype=jnp.int32)
    nb = (2 * N + N_EXPERTS * (TB_EXP - 1)) // TB_EXP + 1
    e_id = route_t[ROUTE_E1:ROUTE_E2 + 1].astype(jnp.int32)
    rank = route_t[ROUTE_R1:ROUTE_R2 + 1].astype(jnp.int32)
    cnt = counts[0, :N_EXPERTS].astype(jnp.int32)
    padded = (cnt + TB_EXP - 1) // TB_EXP * TB_EXP
    pad_end = jnp.cumsum(padded)
    pad_start = pad_end - padded
    e_flat = e_id.reshape(1, 2 * N)
    dest = jnp.sum(jnp.where(e_flat == experts[:, None], pad_start[:, None], 0), axis=0) + rank.reshape(2 * N)
    x_rows = _dispatch(dest, h2, nb * TB_EXP)
    n_used = pad_end[-1] // TB_EXP
    blk_map = jnp.minimum(jnp.arange(nb, dtype=jnp.int32), jnp.maximum(n_used - 1, 0))
    blk_exp = jnp.sum(pad_end[None, :] <= (blk_map * TB_EXP)[:, None], axis=1).astype(jnp.int32)
    blk_exp = jnp.minimum(blk_exp, N_EXPERTS - 1)
    is_blk_e = blk_exp[:, None] == experts[None, :]
    blk_valid = jnp.clip(jnp.sum(jnp.where(is_blk_e, (pad_start + cnt)[None, :], 0), axis=1) - blk_map * TB_EXP,
                         0, TB_EXP).astype(jnp.int32)
    has_rows = cnt > 0
    slot_of_e = (jnp.cumsum(has_rows.astype(jnp.int32)) - 1) % 2
    later = jnp.where(has_rows, experts, N_EXPERTS)
    next_of_e = lax.cummin(jnp.concatenate([later[1:], jnp.full((1,), N_EXPERTS, jnp.int32)]), reverse=True)
    next_of_e = jnp.where(next_of_e < N_EXPERTS, next_of_e, -1)
    blk_slot = jnp.sum(jnp.where(is_blk_e, slot_of_e[None, :], 0), axis=1).astype(jnp.int32)
    blk_next = jnp.sum(jnp.where(is_blk_e, next_of_e[None, :], 0), axis=1).astype(jnp.int32)

    per_router_tile = TM_OUT // tm if N >= TM_OUT else 1
    carry = sub_carry.reshape(-1, SUBLANES, LANES)[:, :per_router_tile, :N_EXPERTS].reshape(n_tiles, N_EXPERTS)
    carry = carry.astype(jnp.int32)
    tile_cnt = jnp.concatenate([carry[1:], cnt[None, :]], axis=0) - carry
    run_start = pad_start[None, :] + carry
    first_blk = run_start // RUN_ROWS
    nslot = jnp.where(tile_cnt > 0, (run_start + tile_cnt - 1) // RUN_ROWS - first_blk + 1, 0)
    slot_end = jnp.cumsum(nslot, axis=1)
    slot_base = slot_end - nslot
    slots = jnp.arange(RUN_SLOTS, dtype=jnp.int32)
    slot_e = jnp.minimum(jnp.sum(slot_end[:, None, :] <= slots[None, :, None], axis=2), N_EXPERTS - 1)
    is_slot_e = slot_e[:, :, None] == experts[None, None, :]
    pick = lambda tbl: jnp.sum(jnp.where(is_slot_e, tbl[:, None, :], 0), axis=2)
    src = (pick(first_blk) + slots[None, :] - pick(slot_base)) * RUN_ROWS
    src = jnp.where(slots[None, :] < slot_end[:, -1:], src, 0)
    src = jnp.clip(src, 0, nb * TB_EXP - RUN_ROWS).reshape(-1).astype(jnp.int32)
    n_groups = (slot_end[:, -1] + RUN_GROUP - 1) // RUN_GROUP
    off = (slot_base - first_blk) * RUN_ROWS + pad_start[None, :]
    off = jnp.pad(off.astype(F32), ((0, 0), (0, LANES - N_EXPERTS)))

    y_rows = _experts(blk_map, blk_exp, blk_valid, blk_slot, blk_next, n_used.reshape(1).astype(jnp.int32),
                      x_rows, w1, w3, w2, layer)
    return _combine(src, n_groups.astype(jnp.int32), x1, route, off.reshape(n_tiles, 1, LANES), y_rows)


def _pad_heads(w, width):
    lead = w.shape[:-1]
    w = w.reshape(lead + (N_HEADS, width))
    pad = [(0, 0)] * (w.ndim - 1) + [(0, HEAD_PAD - width)]
    return jnp.pad(w, pad).reshape(lead + (N_HEADS * HEAD_PAD,))


def _swap_rope(w):
    half = QK_ROPE // 2
    lo, hi = w[..., QK_NOPE:QK_NOPE + half], w[..., QK_NOPE + half:QK_HEAD]
    pad = [(0, 0)] * (w.ndim - 1)
    return jnp.pad(jnp.concatenate([hi, lo], -1), pad + [(QK_NOPE, HEAD_PAD - QK_HEAD)])


def _rope_tables(positions):
    inv_freq = ROPE_THETA ** (-jnp.arange(0, QK_ROPE, 2, dtype=F32) / QK_ROPE)
    ang = positions.astype(F32)[..., None] * inv_freq
    table = jnp.concatenate([jnp.cos(ang), jnp.sin(ang)], -1)
    return jnp.pad(table, ((0, 0), (0, 0), (QK_NOPE, LANES - QK_HEAD)))


def _ssm_params(lam_re, lam_im, b_re, b_im, c_re, c_im, log_dt):
    depth = lam_re.shape[0]
    lam = lax.complex(lam_re, lam_im)
    dt = jnp.exp(log_dt)[..., None]
    lam_bar = jnp.exp(lam * dt)
    b_bar = ((lam_bar - 1.0) / lam)[..., None] * lax.complex(b_re, b_im)
    eye = jnp.eye(SSM_GROUPS, dtype=F32)

    def in_blockdiag(m):
        return jnp.einsum("lgpc,gh->lgchp", m, eye).reshape(depth, SSM_WIDTH, SSM_FLAT)

    def out_blockdiag(m):
        return jnp.einsum("lgcp,gh->lgphc", m, eye).reshape(depth, SSM_FLAT, SSM_WIDTH)

    bbd = jnp.concatenate([in_blockdiag(jnp.real(b_bar)), in_blockdiag(jnp.imag(b_bar))], axis=2)
    cbd = jnp.concatenate([out_blockdiag(c_re), out_blockdiag(-c_im)], axis=1)
    lam_rows = jnp.stack([jnp.real(lam_bar).reshape(depth, SSM_FLAT), jnp.imag(lam_bar).reshape(depth, SSM_FLAT)],
                         axis=1)
    return bbd.astype(BF16), lam_rows, cbd.astype(BF16)


def kernel(x, positions, mix_norm, w_in, q_a_norm, w_uq, kv_a_norm, w_ukv, q_norm, k_norm, ssm_lam_re, ssm_lam_im, ssm_b_re, ssm_b_im, ssm_c_re, ssm_c_im, ssm_d, ssm_log_dt, ssm_w_glu, conv_dw_w, conv_dw_b, conv_ln_w, conv_ln_b, out_norm, w_out, ffn_norm, w_grp, b_grp, w_exp, b_exp, w1, w3, w2):
    B, L, D = x.shape
    depth = w_in.shape[0]
    rope = _rope_tables(positions)
    rows3 = lambda v: v[:, None, :]
    lane_tail = lambda v: jnp.pad(v, [(0, 0)] * (v.ndim - 1) + [(0, LANES - v.shape[-1])])
    c_q, c_kv, k_pe, u_s, c_a, c_g = jnp.split(
        w_in, [Q_LORA, Q_LORA + KV_LORA, Q_LORA + KV_LORA + QK_ROPE, Q_LORA + KV_LORA + QK_ROPE + SSM_WIDTH,
               Q_LORA + KV_LORA + QK_ROPE + SSM_WIDTH + CONV_WIDTH], axis=2)
    k_pe_full = jnp.pad(k_pe, ((0, 0), (0, 0), (QK_NOPE, 0)))
    win = jnp.concatenate([c_q, c_kv, lane_tail(k_pe_full), _swap_rope(k_pe_full), u_s, c_a, c_g], axis=2).astype(BF16)
    wkv = w_ukv.reshape(depth, KV_LORA, N_HEADS, QK_NOPE + V_HEAD)
    wuk = _pad_heads(wkv[..., :QK_NOPE].reshape(depth, KV_LORA, N_HEADS * QK_NOPE), QK_NOPE).astype(BF16)
    wuv = wkv[..., QK_NOPE:].reshape(depth, KV_LORA, ATTN_WIDTH).astype(BF16)
    wuq_sw = _swap_rope(w_uq.reshape(depth, Q_LORA, N_HEADS, QK_HEAD)).reshape(depth, Q_LORA, N_HEADS * HEAD_PAD)
    wuq = jnp.concatenate([_pad_heads(w_uq, QK_HEAD), wuq_sw], axis=2).astype(BF16)
    norm_rows = lambda w: jnp.stack([lane_tail(w), _swap_rope(w)], axis=1)
    qn, kn = norm_rows(q_norm * (QK_HEAD ** -0.5)), norm_rows(k_norm)
    bbd, lam_rows, cbd = _ssm_params(ssm_lam_re, ssm_lam_im, ssm_b_re, ssm_b_im, ssm_c_re, ssm_c_im, ssm_log_dt)
    ssm_skip = rows3(ssm_d.reshape(depth, SSM_WIDTH))
    wglu = ssm_w_glu.astype(BF16)
    g_attn = rows3(out_norm[:, :ATTN_WIDTH])
    g_ssm = rows3(out_norm[:, ATTN_WIDTH:ATTN_WIDTH + SSM_WIDTH])
    conv_params = (conv_dw_w, rows3(conv_dw_b), rows3(conv_ln_w), rows3(conv_ln_b),
                   rows3(out_norm[:, ATTN_WIDTH + SSM_WIDTH:]))
    w_route = lane_tail(jnp.concatenate([w_exp, w_grp], axis=2))
    w_route_hi = w_route.astype(BF16)
    w_route = jnp.concatenate([w_route_hi, (w_route - w_route_hi.astype(F32)).astype(BF16)], axis=2)
    b_route = rows3(lane_tail(jnp.concatenate([b_exp, b_grp], axis=1)))
    wo = w_out.astype(BF16)
    for l in range(depth):
        q, k, v, u_ssm, cu = _in_proj(x, rope, rows3(mix_norm), win, rows3(q_a_norm), wuq, rows3(kv_a_norm), wuk, wuv,
                                         qn, kn, l)
        y_attn = _attention(q, k, v)
        y_ssm = _ssm(u_ssm, bbd, lam_rows, cbd, ssm_skip, wglu, g_ssm, B, l)
        x1, h2, route, route_t, counts, sub_carry = _out_router(
            x.reshape(B * L, D), y_attn.reshape(B * L, ATTN_WIDTH), y_ssm, cu.reshape(B * L, CONV_WIDTH), g_attn, wo, rows3(ffn_norm),
            w_route, b_route, conv_params, B, l)
        x = _moe(x1, h2, route, route_t, counts, sub_carry, w1, w3, w2, l).reshape(B, L, D)
    return x
```
